```python
import math
import jax, jax.numpy as jnp
from jax import lax
import numpy as np

D_MODEL = 2048
BATCH = 8
SEQ = 4096
DEPTH = 1

MIX_WIDTH = D_MODEL
BLOCK = 128
EPS = 1e-6

SWA_HEADS = 16
SWA_KV_HEADS = 2
SWA_HEAD_DIM = 64
SWA_GROUP = SWA_HEADS // SWA_KV_HEADS
WINDOW = 128

REL_BUCKETS = 32
REL_MAX_DIST = 128

MLA_HEADS = 8
MLA_Q_RANK = 384
MLA_KV_RANK = 128
MLA_NOPE_DIM = 128
MLA_ROPE_DIM = 64
MLA_V_DIM = 128
MLA_QK_DIM = MLA_NOPE_DIM + MLA_ROPE_DIM
ROPE_THETA = 10000.0

D_FF = 4 * D_MODEL

SWA_Q_COLS = SWA_HEADS * SWA_HEAD_DIM
SWA_KV_COLS = SWA_KV_HEADS * SWA_HEAD_DIM
OFF_SWA_Q = 0
OFF_SWA_K = OFF_SWA_Q + SWA_Q_COLS
OFF_SWA_V = OFF_SWA_K + SWA_KV_COLS
OFF_MLA_CQ = OFF_SWA_V + SWA_KV_COLS
OFF_MLA_CKV = OFF_MLA_CQ + MLA_Q_RANK
OFF_MLA_KR = OFF_MLA_CKV + MLA_KV_RANK
IN_COLS = OFF_MLA_KR + MLA_ROPE_DIM
SWA_OUT = SWA_HEADS * SWA_HEAD_DIM
MLA_OUT = MLA_HEADS * MLA_V_DIM

kernel_name = "hymba_swa_sink_mla_adaln_layer"


def rmsnorm(x, g):
    x32 = x.astype(jnp.float32)
    y = x32 * lax.rsqrt(jnp.mean(x32 * x32, axis=-1, keepdims=True) + EPS)
    return y.astype(x.dtype) * g


def t5_causal_bucket(dist):
    n = jnp.maximum(dist, 0)
    max_exact = REL_BUCKETS // 2
    is_small = n < max_exact
    nf = jnp.maximum(n, 1).astype(jnp.float32)
    large = max_exact + (jnp.log(nf / max_exact) / math.log(REL_MAX_DIST / max_exact)
                         * (REL_BUCKETS - max_exact)).astype(jnp.int32)
    large = jnp.minimum(large, REL_BUCKETS - 1)
    return jnp.where(is_small, n, large)


def rope(x, positions):
    half = x.shape[-1] // 2
    inv_freq = ROPE_THETA ** (-jnp.arange(half, dtype=jnp.float32) / half)
    ang = positions.astype(jnp.float32)[:, None] * inv_freq[None, :]
    extra = x.ndim - 3
    ang = ang.reshape((1, ang.shape[0]) + (1,) * extra + (half,))
    cos = jnp.cos(ang).astype(x.dtype)
    sin = jnp.sin(ang).astype(x.dtype)
    x1, x2 = x[..., :half], x[..., half:]
    return jnp.concatenate([x1 * cos - x2 * sin, x2 * cos + x1 * sin], axis=-1)


def swa_sink_attention(q, k, v, sinks, rel_bias):
    B, S = q.shape[0], q.shape[1]
    nb = S // BLOCK
    qb = q.reshape(B, nb, BLOCK, SWA_KV_HEADS, SWA_GROUP, SWA_HEAD_DIM)
    kb = k.reshape(B, nb, BLOCK, SWA_KV_HEADS, SWA_HEAD_DIM)
    vb = v.reshape(B, nb, BLOCK, SWA_KV_HEADS, SWA_HEAD_DIM)
    zero = jnp.zeros_like(kb[:, :1])
    k_band = jnp.concatenate([jnp.concatenate([zero, kb[:, :-1]], axis=1), kb], axis=2)
    v_band = jnp.concatenate([jnp.concatenate([zero, vb[:, :-1]], axis=1), vb], axis=2)
    s = jnp.einsum('bnqhgd,bnkhd->bnhgqk', qb, k_band).astype(jnp.float32) * (SWA_HEAD_DIM ** -0.5)

    q_loc = jnp.arange(BLOCK)[:, None]
    k_loc = jnp.arange(2 * BLOCK)[None, :]
    dist = q_loc + BLOCK - k_loc
    in_window = (dist >= 0) & (dist < WINDOW)
    blk = jnp.arange(nb)[:, None]
    key_valid = (blk * BLOCK - BLOCK + k_loc) >= 0
    mask = in_window[None] & key_valid[:, None, :]

    bias = rel_bias.astype(jnp.float32)[t5_causal_bucket(dist)]
    bias = bias.transpose(2, 0, 1).reshape(SWA_KV_HEADS, SWA_GROUP, BLOCK, 2 * BLOCK)
    s = s + bias[None, None]
    s = jnp.where(mask[None, :, None, None], s, -jnp.inf)

    sink = sinks.astype(jnp.float32).reshape(SWA_KV_HEADS, SWA_GROUP)[None, None, :, :, None, None]
    m = jnp.maximum(jnp.max(s, axis=-1, keepdims=True), sink)
    p = jnp.exp(s - m)
    p = p / (jnp.sum(p, axis=-1, keepdims=True) + jnp.exp(sink - m))
    o = jnp.einsum('bnhgqk,bnkhd->bnqhgd', p.astype(v.dtype), v_band)
    return o.reshape(B, S, SWA_OUT)


def mla_attention(q_nope, q_rope, k_nope, k_rope, v):
    B, S = q_nope.shape[0], q_nope.shape[1]
    nb = S // BLOCK
    qn = q_nope.reshape(B, nb, BLOCK, MLA_HEADS, MLA_NOPE_DIM).transpose(1, 0, 2, 3, 4)
    qr = q_rope.reshape(B, nb, BLOCK, MLA_HEADS, MLA_ROPE_DIM).transpose(1, 0, 2, 3, 4)
    key_pos = jnp.arange(S)
    scale = MLA_QK_DIM ** -0.5

    def one_block(args):
        i, qn_i, qr_i = args
        s = (jnp.einsum('bqhd,bkhd->bhqk', qn_i, k_nope)
             + jnp.einsum('bqhd,bkd->bhqk', qr_i, k_rope)).astype(jnp.float32) * scale
        q_pos = i * BLOCK + jnp.arange(BLOCK)
        causal = key_pos[None, :] <= q_pos[:, None]
        s = jnp.where(causal[None, None], s, -jnp.inf)
        p = jax.nn.softmax(s, axis=-1)
        return jnp.einsum('bhqk,bkhd->bqhd', p.astype(v.dtype), v)

    o = lax.map(one_block, (jnp.arange(nb), qn, qr))
    return o.transpose(1, 0, 2, 3, 4).reshape(B, S, MLA_OUT)


def _fwd_setup_inputs(seed: int = 0) -> dict:
    key = jax.random.key(seed)
    ks = jax.random.split(key, 20)
    f32 = jnp.float32
    nrm = lambda k, shape, s: jax.random.normal(k, shape, f32) * s
    return {
        "x": nrm(ks[0], (BATCH, SEQ, D_MODEL), 1.0),
        "c": nrm(ks[1], (BATCH, D_MODEL), 1.0),
        "w_mod": nrm(ks[2], (DEPTH, D_MODEL, 6 * D_MODEL), 0.5 * D_MODEL ** -0.5),
        "b_mod": nrm(ks[3], (DEPTH, 6 * D_MODEL), 0.01),
        "attn_norm_g": 1.0 + nrm(ks[4], (DEPTH, D_MODEL), 0.02),
        "w_in": nrm(ks[5], (DEPTH, D_MODEL, IN_COLS), D_MODEL ** -0.5),
        "swa_sinks": nrm(ks[6], (DEPTH, SWA_HEADS), 1.0),
        "rel_bias": nrm(ks[7], (REL_BUCKETS, SWA_HEADS), 0.5),
        "mla_q_norm_g": 1.0 + nrm(ks[8], (DEPTH, MLA_Q_RANK), 0.02),
        "w_uq": nrm(ks[9], (DEPTH, MLA_Q_RANK, MLA_HEADS * MLA_QK_DIM), MLA_Q_RANK ** -0.5),
        "mla_kv_norm_g": 1.0 + nrm(ks[10], (DEPTH, MLA_KV_RANK), 0.02),
        "w_ukv": nrm(ks[11], (DEPTH, MLA_KV_RANK, MLA_HEADS * (MLA_NOPE_DIM + MLA_V_DIM)), MLA_KV_RANK ** -0.5),
        "w_out": nrm(ks[12], (DEPTH, MIX_WIDTH, D_MODEL), MIX_WIDTH ** -0.5),
        "mlp_norm_g": 1.0 + nrm(ks[13], (DEPTH, D_MODEL), 0.02),
        "w_ff1": nrm(ks[14], (DEPTH, D_MODEL, D_FF), D_MODEL ** -0.5),
        "w_ff2": nrm(ks[15], (DEPTH, D_FF, D_MODEL), D_FF ** -0.5),
        "final_norm_g": 1.0 + nrm(ks[16], (D_MODEL,), 0.02),
    }


def _fwd_reference(x, c, w_mod, b_mod, attn_norm_g, w_in, swa_sinks, rel_bias, mla_q_norm_g, w_uq,
              mla_kv_norm_g, w_ukv, w_out, mlp_norm_g, w_ff1, w_ff2, final_norm_g):
    B, S, _ = x.shape
    positions = jnp.arange(S)
    c_act = jax.nn.silu(c)
    for l in range(DEPTH):
        mod = c_act @ w_mod[l] + b_mod[l]
        sh1, sc1, g1, sh2, sc2, g2 = [m[:, None, :] for m in jnp.split(mod, 6, axis=-1)]

        h = rmsnorm(x, attn_norm_g[l]) * (1.0 + sc1) + sh1
        proj = jnp.einsum('bsd,df->bsf', h, w_in[l])

        q_a = proj[..., OFF_SWA_Q:OFF_SWA_K].reshape(B, S, SWA_HEADS, SWA_HEAD_DIM)
        k_a = proj[..., OFF_SWA_K:OFF_SWA_V].reshape(B, S, SWA_KV_HEADS, SWA_HEAD_DIM)
        v_a = proj[..., OFF_SWA_V:OFF_MLA_CQ].reshape(B, S, SWA_KV_HEADS, SWA_HEAD_DIM)
        o_a = swa_sink_attention(q_a, k_a, v_a, swa_sinks[l], rel_bias)

        c_q = rmsnorm(proj[..., OFF_MLA_CQ:OFF_MLA_CKV], mla_q_norm_g[l])
        c_kv = rmsnorm(proj[..., OFF_MLA_CKV:OFF_MLA_KR], mla_kv_norm_g[l])
        k_rope = rope(proj[..., OFF_MLA_KR:IN_COLS], positions)
        q_b = jnp.einsum('bsr,rf->bsf', c_q, w_uq[l]).reshape(B, S, MLA_HEADS, MLA_QK_DIM)
        q_nope = q_b[..., :MLA_NOPE_DIM]
        q_rope = rope(q_b[..., MLA_NOPE_DIM:], positions)
        kv_b = jnp.einsum('bsr,rf->bsf', c_kv, w_ukv[l]).reshape(B, S, MLA_HEADS, MLA_NOPE_DIM + MLA_V_DIM)
        k_nope = kv_b[..., :MLA_NOPE_DIM]
        v_b = kv_b[..., MLA_NOPE_DIM:]
        o_b = mla_attention(q_nope, q_rope, k_nope, k_rope, v_b)

        mix = jnp.concatenate([o_a, o_b], axis=-1)
        x = x + g1 * jnp.einsum('bsm,md->bsd', mix, w_out[l])

        h = rmsnorm(x, mlp_norm_g[l]) * (1.0 + sc2) + sh2
        u = jax.nn.relu(jnp.einsum('bsd,df->bsf', h, w_ff1[l]))
        x = x + g2 * jnp.einsum('bsf,fd->bsd', u * u, w_ff2[l])
    return rmsnorm(x, final_norm_g)


import jax as _jax
import jax.numpy as _jnp

TWIN_FORMAT = 'train_step'
FWD_PARAMS = ['x', 'c', 'w_mod', 'b_mod', 'attn_norm_g', 'w_in', 'swa_sinks', 'rel_bias', 'mla_q_norm_g', 'w_uq', 'mla_kv_norm_g', 'w_ukv', 'w_out', 'mlp_norm_g', 'w_ff1', 'w_ff2', 'final_norm_g']
TWIN_WEIGHTS = ['w_mod', 'b_mod', 'attn_norm_g', 'w_in', 'swa_sinks', 'rel_bias', 'mla_q_norm_g', 'w_uq', 'mla_kv_norm_g', 'w_ukv', 'w_out', 'mlp_norm_g', 'w_ff1', 'w_ff2', 'final_norm_g']
TWIN_DIFF_INPUT = 'x'
TWIN_INPUTS = ['x', 'c', 'w_mod', 'b_mod', 'attn_norm_g', 'w_in', 'swa_sinks', 'rel_bias', 'mla_q_norm_g', 'w_uq', 'mla_kv_norm_g', 'w_ukv', 'w_out', 'mlp_norm_g', 'w_ff1', 'w_ff2', 'final_norm_g', 'loss_target', 'm_w_mod', 'm_b_mod', 'm_attn_norm_g', 'm_w_in', 'm_swa_sinks', 'm_rel_bias', 'm_mla_q_norm_g', 'm_w_uq', 'm_mla_kv_norm_g', 'm_w_ukv', 'm_w_out', 'm_mlp_norm_g', 'm_w_ff1', 'm_w_ff2', 'm_final_norm_g', 'v_w_mod', 'v_b_mod', 'v_attn_norm_g', 'v_w_in', 'v_swa_sinks', 'v_rel_bias', 'v_mla_q_norm_g', 'v_w_uq', 'v_mla_kv_norm_g', 'v_w_ukv', 'v_w_out', 'v_mlp_norm_g', 'v_w_ff1', 'v_w_ff2', 'v_final_norm_g']
TWIN_OUTPUTS = ['loss', 'grad_x', 'grad_w_mod', 'grad_b_mod', 'grad_attn_norm_g', 'grad_w_in', 'grad_swa_sinks', 'grad_rel_bias', 'grad_mla_q_norm_g', 'grad_w_uq', 'grad_mla_kv_norm_g', 'grad_w_ukv', 'grad_w_out', 'grad_mlp_norm_g', 'grad_w_ff1', 'grad_w_ff2', 'grad_final_norm_g', 'delta_w_mod', 'delta_b_mod', 'delta_attn_norm_g', 'delta_w_in', 'delta_swa_sinks', 'delta_rel_bias', 'delta_mla_q_norm_g', 'delta_w_uq', 'delta_mla_kv_norm_g', 'delta_w_ukv', 'delta_w_out', 'delta_mlp_norm_g', 'delta_w_ff1', 'delta_w_ff2', 'delta_final_norm_g', 'new_m_w_mod', 'new_m_b_mod', 'new_m_attn_norm_g', 'new_m_w_in', 'new_m_swa_sinks', 'new_m_rel_bias', 'new_m_mla_q_norm_g', 'new_m_w_uq', 'new_m_mla_kv_norm_g', 'new_m_w_ukv', 'new_m_w_out', 'new_m_mlp_norm_g', 'new_m_w_ff1', 'new_m_w_ff2', 'new_m_final_norm_g', 'new_v_w_mod', 'new_v_b_mod', 'new_v_attn_norm_g', 'new_v_w_in', 'new_v_swa_sinks', 'new_v_rel_bias', 'new_v_mla_q_norm_g', 'new_v_w_uq', 'new_v_mla_kv_norm_g', 'new_v_w_ukv', 'new_v_w_out', 'new_v_mlp_norm_g', 'new_v_w_ff1', 'new_v_w_ff2', 'new_v_final_norm_g']
TWIN_LEAF_KINDS = {'loss': 'loss', 'grad_x': 'grad_x', 'grad_w_mod': 'grad_w', 'grad_b_mod': 'grad_w', 'grad_attn_norm_g': 'grad_w', 'grad_w_in': 'grad_w', 'grad_swa_sinks': 'grad_w', 'grad_rel_bias': 'grad_w', 'grad_mla_q_norm_g': 'grad_w', 'grad_w_uq': 'grad_w', 'grad_mla_kv_norm_g': 'grad_w', 'grad_w_ukv': 'grad_w', 'grad_w_out': 'grad_w', 'grad_mlp_norm_g': 'grad_w', 'grad_w_ff1': 'grad_w', 'grad_w_ff2': 'grad_w', 'grad_final_norm_g': 'grad_w', 'delta_w_mod': 'delta_w', 'delta_b_mod': 'delta_w', 'delta_attn_norm_g': 'delta_w', 'delta_w_in': 'delta_w', 'delta_swa_sinks': 'delta_w', 'delta_rel_bias': 'delta_w', 'delta_mla_q_norm_g': 'delta_w', 'delta_w_uq': 'delta_w', 'delta_mla_kv_norm_g': 'delta_w', 'delta_w_ukv': 'delta_w', 'delta_w_out': 'delta_w', 'delta_mlp_norm_g': 'delta_w', 'delta_w_ff1': 'delta_w', 'delta_w_ff2': 'delta_w', 'delta_final_norm_g': 'delta_w', 'new_m_w_mod': 'new_m', 'new_m_b_mod': 'new_m', 'new_m_attn_norm_g': 'new_m', 'new_m_w_in': 'new_m', 'new_m_swa_sinks': 'new_m', 'new_m_rel_bias': 'new_m', 'new_m_mla_q_norm_g': 'new_m', 'new_m_w_uq': 'new_m', 'new_m_mla_kv_norm_g': 'new_m', 'new_m_w_ukv': 'new_m', 'new_m_w_out': 'new_m', 'new_m_mlp_norm_g': 'new_m', 'new_m_w_ff1': 'new_m', 'new_m_w_ff2': 'new_m', 'new_m_final_norm_g': 'new_m', 'new_v_w_mod': 'new_v', 'new_v_b_mod': 'new_v', 'new_v_attn_norm_g': 'new_v', 'new_v_w_in': 'new_v', 'new_v_swa_sinks': 'new_v', 'new_v_rel_bias': 'new_v', 'new_v_mla_q_norm_g': 'new_v', 'new_v_w_uq': 'new_v', 'new_v_mla_kv_norm_g': 'new_v', 'new_v_w_ukv': 'new_v', 'new_v_w_out': 'new_v', 'new_v_mlp_norm_g': 'new_v', 'new_v_w_ff1': 'new_v', 'new_v_w_ff2': 'new_v', 'new_v_final_norm_g': 'new_v'}


def _forward(args):
    return _fwd_reference(*[args[k] for k in FWD_PARAMS])


def _output_shape():
    def fwd():
        inp = _fwd_setup_inputs(0)
        return _fwd_reference(*[inp[k] for k in FWD_PARAMS])
    out = _jax.eval_shape(fwd)
    return out.shape, out.dtype

N_MICROBATCH = 1
ADAM_LR = 0.001
ADAM_B1 = 0.9
ADAM_B2 = 0.999
ADAM_EPS = 1e-08
ADAM_WD = 0.01
ADAM_STEP = 10
PER_EXAMPLE_BATCH_AXIS = {'x': 0, 'c': 0, 'loss_target': 0}
SHARED_INPUTS = []
_WEIGHT_DTYPES = {'w_mod': _jnp.float32, 'b_mod': _jnp.float32, 'attn_norm_g': _jnp.float32, 'w_in': _jnp.float32, 'swa_sinks': _jnp.float32, 'rel_bias': _jnp.float32, 'mla_q_norm_g': _jnp.float32, 'w_uq': _jnp.float32, 'mla_kv_norm_g': _jnp.float32, 'w_ukv': _jnp.float32, 'w_out': _jnp.float32, 'mlp_norm_g': _jnp.float32, 'w_ff1': _jnp.float32, 'w_ff2': _jnp.float32, 'final_norm_g': _jnp.float32}
MOMENT_SCALE = {'w_mod': 3.921198e-02, 'b_mod': 7.287705e-02, 'attn_norm_g': 9.844957e-03, 'w_in': 1.316876e-02, 'swa_sinks': 5.341756e-03, 'rel_bias': 7.679795e-03, 'mla_q_norm_g': 7.895382e-03, 'w_uq': 3.929720e-03, 'mla_kv_norm_g': 2.870267e-02, 'w_ukv': 7.160823e-03, 'w_out': 9.716368e-03, 'mlp_norm_g': 3.963601e-02, 'w_ff1': 2.061502e-02, 'w_ff2': 3.848625e-02, 'final_norm_g': 1.606963e+01}


def _to_microbatches(a, axis):
    t = _jnp.moveaxis(a, axis, 0)
    t = t.reshape((N_MICROBATCH, t.shape[0] // N_MICROBATCH) + t.shape[1:])
    return _jnp.moveaxis(t, 1, axis + 1)


def setup_inputs(seed: int = 0) -> dict:
    inp = _fwd_setup_inputs(seed)
    key = _jax.random.fold_in(_jax.random.key(seed), 7919)
    shape, _ = _output_shape()
    out = dict(inp)
    out["loss_target"] = _jax.random.normal(_jax.random.fold_in(key, 0), shape, _jnp.float32)
    for i, name in enumerate(TWIN_WEIGHTS):
        w = inp[name].astype(_jnp.float32)
        if MOMENT_SCALE is None:
            s = _jnp.sqrt(_jnp.mean(_jnp.square(w)) + 1e-30)
        else:
            s = MOMENT_SCALE[name]
        km, kv = _jax.random.split(_jax.random.fold_in(key, i + 1))
        out[name] = w
        out["m_" + name] = s * _jax.random.normal(km, w.shape, _jnp.float32)
        out["v_" + name] = (s * s) * _jax.random.uniform(kv, w.shape, _jnp.float32, 0.5, 1.5)
    if N_MICROBATCH > 1:
        for name, axis in PER_EXAMPLE_BATCH_AXIS.items():
            out[name] = _to_microbatches(out[name], axis)
    return {'x': out['x'], 'c': out['c'], 'w_mod': out['w_mod'], 'b_mod': out['b_mod'], 'attn_norm_g': out['attn_norm_g'], 'w_in': out['w_in'], 'swa_sinks': out['swa_sinks'], 'rel_bias': out['rel_bias'], 'mla_q_norm_g': out['mla_q_norm_g'], 'w_uq': out['w_uq'], 'mla_kv_norm_g': out['mla_kv_norm_g'], 'w_ukv': out['w_ukv'], 'w_out': out['w_out'], 'mlp_norm_g': out['mlp_norm_g'], 'w_ff1': out['w_ff1'], 'w_ff2': out['w_ff2'], 'final_norm_g': out['final_norm_g'], 'loss_target': out['loss_target'], 'm_w_mod': out['m_w_mod'], 'm_b_mod': out['m_b_mod'], 'm_attn_norm_g': out['m_attn_norm_g'], 'm_w_in': out['m_w_in'], 'm_swa_sinks': out['m_swa_sinks'], 'm_rel_bias': out['m_rel_bias'], 'm_mla_q_norm_g': out['m_mla_q_norm_g'], 'm_w_uq': out['m_w_uq'], 'm_mla_kv_norm_g': out['m_mla_kv_norm_g'], 'm_w_ukv': out['m_w_ukv'], 'm_w_out': out['m_w_out'], 'm_mlp_norm_g': out['m_mlp_norm_g'], 'm_w_ff1': out['m_w_ff1'], 'm_w_ff2': out['m_w_ff2'], 'm_final_norm_g': out['m_final_norm_g'], 'v_w_mod': out['v_w_mod'], 'v_b_mod': out['v_b_mod'], 'v_attn_norm_g': out['v_attn_norm_g'], 'v_w_in': out['v_w_in'], 'v_swa_sinks': out['v_swa_sinks'], 'v_rel_bias': out['v_rel_bias'], 'v_mla_q_norm_g': out['v_mla_q_norm_g'], 'v_w_uq': out['v_w_uq'], 'v_mla_kv_norm_g': out['v_mla_kv_norm_g'], 'v_w_ukv': out['v_w_ukv'], 'v_w_out': out['v_w_out'], 'v_mlp_norm_g': out['v_mlp_norm_g'], 'v_w_ff1': out['v_w_ff1'], 'v_w_ff2': out['v_w_ff2'], 'v_final_norm_g': out['v_final_norm_g']}


def _loss(weights, diff, rest, loss_target):
    with _jax.named_scope("forward"):
        args = {**rest, TWIN_DIFF_INPUT: diff, **{k: w.astype(_WEIGHT_DTYPES[k]) for k, w in weights.items()}}
        y = _forward(args)
    with _jax.named_scope("loss_head"):
        err = _jnp.square(y.astype(_jnp.float32) - loss_target)
        return 0.5 * _jnp.sum(_jnp.mean(err, axis=-1)) if err.ndim else 0.5 * err


def _adamw(w, g, m, v):
    m = ADAM_B1 * m + (1.0 - ADAM_B1) * g
    v = ADAM_B2 * v + (1.0 - ADAM_B2) * _jnp.square(g)
    m_hat = m / (1.0 - ADAM_B1 ** ADAM_STEP)
    v_hat = v / (1.0 - ADAM_B2 ** ADAM_STEP)
    delta = -ADAM_LR * (m_hat / (_jnp.sqrt(v_hat) + ADAM_EPS) + ADAM_WD * w)
    return delta, m, v


def reference(x, c, w_mod, b_mod, attn_norm_g, w_in, swa_sinks, rel_bias, mla_q_norm_g, w_uq, mla_kv_norm_g, w_ukv, w_out, mlp_norm_g, w_ff1, w_ff2, final_norm_g, loss_target, m_w_mod, m_b_mod, m_attn_norm_g, m_w_in, m_swa_sinks, m_rel_bias, m_mla_q_norm_g, m_w_uq, m_mla_kv_norm_g, m_w_ukv, m_w_out, m_mlp_norm_g, m_w_ff1, m_w_ff2, m_final_norm_g, v_w_mod, v_b_mod, v_attn_norm_g, v_w_in, v_swa_sinks, v_rel_bias, v_mla_q_norm_g, v_w_uq, v_mla_kv_norm_g, v_w_ukv, v_w_out, v_mlp_norm_g, v_w_ff1, v_w_ff2, v_final_norm_g):
    given = dict(x=x, c=c, w_mod=w_mod, b_mod=b_mod, attn_norm_g=attn_norm_g, w_in=w_in, swa_sinks=swa_sinks, rel_bias=rel_bias, mla_q_norm_g=mla_q_norm_g, w_uq=w_uq, mla_kv_norm_g=mla_kv_norm_g, w_ukv=w_ukv, w_out=w_out, mlp_norm_g=mlp_norm_g, w_ff1=w_ff1, w_ff2=w_ff2, final_norm_g=final_norm_g, loss_target=loss_target, m_w_mod=m_w_mod, m_b_mod=m_b_mod, m_attn_norm_g=m_attn_norm_g, m_w_in=m_w_in, m_swa_sinks=m_swa_sinks, m_rel_bias=m_rel_bias, m_mla_q_norm_g=m_mla_q_norm_g, m_w_uq=m_w_uq, m_mla_kv_norm_g=m_mla_kv_norm_g, m_w_ukv=m_w_ukv, m_w_out=m_w_out, m_mlp_norm_g=m_mlp_norm_g, m_w_ff1=m_w_ff1, m_w_ff2=m_w_ff2, m_final_norm_g=m_final_norm_g, v_w_mod=v_w_mod, v_b_mod=v_b_mod, v_attn_norm_g=v_attn_norm_g, v_w_in=v_w_in, v_swa_sinks=v_swa_sinks, v_rel_bias=v_rel_bias, v_mla_q_norm_g=v_mla_q_norm_g, v_w_uq=v_w_uq, v_mla_kv_norm_g=v_mla_kv_norm_g, v_w_ukv=v_w_ukv, v_w_out=v_w_out, v_mlp_norm_g=v_mlp_norm_g, v_w_ff1=v_w_ff1, v_w_ff2=v_w_ff2, v_final_norm_g=v_final_norm_g)
    weights = {n: given[n] for n in TWIN_WEIGHTS}
    shared = {n: given[n] for n in SHARED_INPUTS}
    per_example = {n: given[n] for n in ['x', 'c']}
    grad_fn = _jax.value_and_grad(_loss, argnums=(0, 1))

    def one_microbatch(ex, loss_target):
        ex = dict(ex)
        diff = ex.pop(TWIN_DIFF_INPUT)
        return grad_fn(weights, diff, {**shared, **ex}, loss_target)

    if N_MICROBATCH == 1:
        loss, (grad_w, grad_x) = one_microbatch(per_example, given["loss_target"])
    else:
        def body(carry, xs):
            loss_sum, grad_sum = carry
            l_k, (gw_k, gx_k) = one_microbatch(xs[0], xs[1])
            with _jax.named_scope("update"):
                return (loss_sum + l_k, _jax.tree.map(_jnp.add, grad_sum, gw_k)), gx_k

        init = (_jnp.zeros((), _jnp.float32), _jax.tree.map(_jnp.zeros_like, weights))
        (loss, grad_w), grad_x = _jax.lax.scan(body, init, (per_example, given["loss_target"]))
    with _jax.named_scope("update"):
        delta_w, new_m, new_v = {}, {}, {}
        for n in TWIN_WEIGHTS:
            delta_w[n], new_m[n], new_v[n] = _adamw(weights[n], grad_w[n], given["m_" + n], given["v_" + n])
    return (loss, grad_x, *[grad_w[n] for n in TWIN_WEIGHTS], *[delta_w[n] for n in TWIN_WEIGHTS],
            *[new_m[n] for n in TWIN_WEIGHTS], *[new_v[n] for n in TWIN_WEIGHTS])
```

```python
import functools
import math

import jax
import jax.numpy as jnp
from jax import lax
from jax.experimental import pallas as pl
from jax.experimental.pallas import tpu as pltpu

F32 = jnp.float32
BF16 = jnp.bfloat16

NDEV = 8
EPS = 1e-6
BLOCK = 128
SWA_HEADS, SWA_KV, SWA_DH, SWA_GROUP = 16, 2, 64, 8
REL_BUCKETS, REL_MAX_DIST = 32, 128
MLA_H, Q_RANK, KV_RANK, NOPE, ROPE, VDIM = 8, 384, 128, 128, 64, 128
ROPE_THETA = 10000.0
OFF_K, OFF_V, OFF_CQ, OFF_CKV, OFF_KR, IN_COLS = 1024, 1152, 1280, 1664, 1792, 1856
IN_EXT = IN_COLS + ROPE
TAIL0, TAIL = OFF_CQ, IN_EXT - OFF_CQ
QW = NOPE + 2 * ROPE
MIX = SWA_HEADS * SWA_DH + MLA_H * VDIM
MLA_SCALE = (NOPE + ROPE) ** -0.5
SWA_SCALE = SWA_DH ** -0.5

ADAM_LR, ADAM_B1, ADAM_B2, ADAM_EPS, ADAM_WD, ADAM_STEP = 0.001, 0.9, 0.999, 1e-08, 0.01, 10

VMEM_LIMIT = 52 * 1024 * 1024
ROW_TILE = 256
MM_TM, MM_TN, MM_TK = 1024, 1024, 512
ATT_T = 512
ADAM_ELEMS = 128 * 1024


def _pcall(body, **kw):
    return pl.pallas_call(body, **kw)


def _cparams(sem):
    return pltpu.CompilerParams(dimension_semantics=sem, vmem_limit_bytes=VMEM_LIMIT)


def _pick(n, pref, align):
    if n <= pref:
        return n
    t = (pref // align) * align
    while t >= align:
        if n % t == 0:
            return t
        t -= align
    return n


def _split3(x):
    a = x.astype(BF16)
    r = x - a.astype(F32)
    b = r.astype(BF16)
    c = (r - b.astype(F32)).astype(BF16)
    return a, b, c


def _peer(x, y, c, k):
    return (1 - x if k & 4 else x, 1 - y if k & 2 else y, 1 - c if k & 1 else c)


def _exchange(name, arrays, gather):
    n = len(arrays)
    if gather:
        out_shape = [jax.ShapeDtypeStruct((NDEV,) + a.shape, a.dtype) for a in arrays]
    else:
        out_shape = [jax.ShapeDtypeStruct(a.shape, a.dtype) for a in arrays]

    def body(*refs):
        ins, outs = refs[:n], refs[n:2 * n]
        send, recv, loc = refs[2 * n:]
        x, y, c = lax.axis_index("x"), lax.axis_index("y"), lax.axis_index("c")
        me = 4 * x + 2 * y + c

        def src(a, dest):
            return ins[a] if gather else ins[a].at[dest]

        started = []
        for a in range(n):
            lc = pltpu.make_async_copy(src(a, me), outs[a].at[me], loc.at[a])
            lc.start()
            started.append(lc)
        sends = []
        for k in (1, 2, 4, 3, 5, 6, 7):
            px, py, pc = _peer(x, y, c, k)
            pid = 4 * px + 2 * py + pc
            for a in range(n):
                cp = pltpu.make_async_remote_copy(
                    src_ref=src(a, pid), dst_ref=outs[a].at[me],
                    send_sem=send.at[a * 7 + k - 1], recv_sem=recv.at[a * 7 + k - 1],
                    device_id=(px, py, pc), device_id_type=pl.DeviceIdType.MESH)
                cp.start()
                sends.append(cp)
        for k in range(1, 8):
            px, py, pc = _peer(x, y, c, k)
            pid = 4 * px + 2 * py + pc
            for a in range(n):
                pltpu.make_async_remote_copy(
                    src_ref=src(a, pid), dst_ref=outs[a].at[pid],
                    send_sem=send.at[a * 7 + k - 1], recv_sem=recv.at[a * 7 + k - 1],
                    device_id=(px, py, pc), device_id_type=pl.DeviceIdType.MESH).wait_recv()
        for cp in sends:
            cp.wait_send()
        for lc in started:
            lc.wait()

    hbm = pl.BlockSpec(memory_space=pltpu.HBM)
    return _pcall(
        body, name=name, out_shape=out_shape, in_specs=[hbm] * n, out_specs=[hbm] * n,
        scratch_shapes=[pltpu.SemaphoreType.DMA((7 * n,)), pltpu.SemaphoreType.DMA((7 * n,)),
                        pltpu.SemaphoreType.DMA((n,))],
    )(*arrays)


_DIMS = {"nn": (((1,), (0,)), ((), ())), "nt": (((1,), (1,)), ((), ())), "tn": (((0,), (0,)), ((), ()))}


def _mm(name, a, b, kind, grid, a_spec, b_spec, out_shape, out_specs, acc_shape, epilogue,
        extras=(), extra_specs=()):
    nk, ne, no = grid[2], len(extras), len(out_shape)

    def body(*refs):
        a_ref, b_ref = refs[0], refs[1]
        ex, outs, acc = refs[2:2 + ne], refs[2 + ne:2 + ne + no], refs[-1]
        k = pl.program_id(2)

        @pl.when(k == 0)
        def _():
            acc[...] = jnp.zeros_like(acc)

        acc[...] += lax.dot_general(a_ref[...].astype(BF16), b_ref[...].astype(BF16), _DIMS[kind],
                                    preferred_element_type=F32)

        @pl.when(k == nk - 1)
        def _():
            epilogue(acc[...], ex, outs)

    return _pcall(
        body, name=name, grid=grid, in_specs=[a_spec, b_spec, *extra_specs], out_specs=out_specs,
        out_shape=out_shape, scratch_shapes=[pltpu.VMEM(acc_shape, F32)],
        compiler_params=_cparams(("parallel", "parallel", "arbitrary")),
    )(a, b, *extras)


def _store(dtype):
    def epi(acc, ex, outs):
        outs[0][...] = acc.astype(dtype)
    return epi


def _mm_plain(name, a, b, kind, m, n, k, out_dtype, tm=None, tn=None, tk=None):
    tm = _pick(m, tm or MM_TM, 128)
    tn = _pick(n, tn or MM_TN, 128)
    tk = _pick(k, tk or MM_TK, 128)
    a_spec = pl.BlockSpec((tk, tm), lambda i, j, q: (q, i)) if kind == "tn" else pl.BlockSpec((tm, tk), lambda i, j, q: (i, q))
    b_spec = pl.BlockSpec((tn, tk), lambda i, j, q: (j, q)) if kind == "nt" else pl.BlockSpec((tk, tn), lambda i, j, q: (q, j))
    return _mm(name, a, b, kind, (m // tm, n // tn, k // tk), a_spec, b_spec,
               [jax.ShapeDtypeStruct((m, n), out_dtype)], [pl.BlockSpec((tm, tn), lambda i, j, q: (i, j))],
               (tm, tn), _store(out_dtype))[0]


def _row(ts, d):
    return pl.BlockSpec((ts, d), lambda i: (i, 0))


def _vec(d):
    return pl.BlockSpec((1, d), lambda i: (0, 0))


def _norm_mod(name, x, gain, sc, sh):
    s, d = x.shape
    ts = _pick(s, ROW_TILE, 16)

    def body(x_ref, g_ref, sc_ref, sh_ref, h_ref):
        xv = x_ref[...]
        r = lax.rsqrt(jnp.mean(xv * xv, axis=-1, keepdims=True) + EPS)
        h_ref[...] = ((xv * r) * g_ref[...] * (1.0 + sc_ref[...]) + sh_ref[...]).astype(BF16)

    return _pcall(body, name=name, grid=(s // ts,), in_specs=[_row(ts, d), _vec(d), _vec(d), _vec(d)],
                  out_specs=_row(ts, d), out_shape=jax.ShapeDtypeStruct((s, d), BF16),
                  compiler_params=_cparams(("parallel",)))(x, gain, sc, sh)


def _loss_head(x3, tgt, y2, gf, g2):
    s, d = x3.shape
    ts = _pick(s, ROW_TILE, 16)

    def body(x_ref, t_ref, y_ref, gf_ref, g2_ref, dx_ref, dy_ref, loss_ref, dgf_ref, dg2_ref):
        @pl.when(pl.program_id(0) == 0)
        def _():
            loss_ref[...] = jnp.zeros_like(loss_ref)
            dgf_ref[...] = jnp.zeros_like(dgf_ref)
            dg2_ref[...] = jnp.zeros_like(dg2_ref)

        xv = x_ref[...]
        r = lax.rsqrt(jnp.mean(xv * xv, axis=-1, keepdims=True) + EPS)
        xn = xv * r
        err = xn * gf_ref[...] - t_ref[...]
        loss_ref[...] += 0.5 * jnp.sum(jnp.mean(err * err, axis=-1, keepdims=True), axis=0, keepdims=True)
        dout = err * (1.0 / d)
        dgf_ref[...] += jnp.sum(dout * xn, axis=0, keepdims=True)
        dxn = dout * gf_ref[...]
        dx = r * (dxn - xn * jnp.mean(dxn * xn, axis=-1, keepdims=True))
        dx_ref[...] = dx
        dy_ref[...] = (dx * g2_ref[...]).astype(BF16)
        dg2_ref[...] += jnp.sum(dx * y_ref[...], axis=0, keepdims=True)

    one = pl.BlockSpec((1, 1), lambda i: (0, 0))
    return _pcall(
        body, name="loss_head", grid=(s // ts,),
        in_specs=[_row(ts, d), _row(ts, d), _row(ts, d), _vec(d), _vec(d)],
        out_specs=[_row(ts, d), _row(ts, d), one, _vec(d), _vec(d)],
        out_shape=[jax.ShapeDtypeStruct((s, d), F32), jax.ShapeDtypeStruct((s, d), BF16),
                   jax.ShapeDtypeStruct((1, 1), F32), jax.ShapeDtypeStruct((1, d), F32),
                   jax.ShapeDtypeStruct((1, d), F32)],
        compiler_params=_cparams(("arbitrary",)))(x3, tgt, y2, gf, g2)


def _norm_mod_bwd(name, x, dh, dres, gain, sc, y_prev=None, gate=None):
    s, d = x.shape
    ts = _pick(s, ROW_TILE, 16)
    gated = y_prev is not None

    def body(*refs):
        if gated:
            x_ref, dh_ref, dr_ref, g_ref, sc_ref, y_ref, gt_ref, dx_ref, dy_ref, dsc_ref, dsh_ref, dg_ref, dgt_ref = refs
        else:
            x_ref, dh_ref, dr_ref, g_ref, sc_ref, dx_ref, dsc_ref, dsh_ref, dg_ref = refs

        @pl.when(pl.program_id(0) == 0)
        def _():
            dsc_ref[...] = jnp.zeros_like(dsc_ref)
            dsh_ref[...] = jnp.zeros_like(dsh_ref)
            dg_ref[...] = jnp.zeros_like(dg_ref)
            if gated:
                dgt_ref[...] = jnp.zeros_like(dgt_ref)

        xv, dhv = x_ref[...], dh_ref[...]
        r = lax.rsqrt(jnp.mean(xv * xv, axis=-1, keepdims=True) + EPS)
        xn = xv * r
        dsc_ref[...] += jnp.sum(dhv * (xn * g_ref[...]), axis=0, keepdims=True)
        dsh_ref[...] += jnp.sum(dhv, axis=0, keepdims=True)
        da = dhv * (1.0 + sc_ref[...])
        dg_ref[...] += jnp.sum(da * xn, axis=0, keepdims=True)
        dxn = da * g_ref[...]
        dx = dr_ref[...] + r * (dxn - xn * jnp.mean(dxn * xn, axis=-1, keepdims=True))
        dx_ref[...] = dx
        if gated:
            dy_ref[...] = (dx * gt_ref[...]).astype(BF16)
            dgt_ref[...] += jnp.sum(dx * y_ref[...], axis=0, keepdims=True)

    ins = [x, dh, dres, gain, sc] + ([y_prev, gate] if gated else [])
    in_specs = [_row(ts, d)] * 3 + [_vec(d)] * 2 + ([_row(ts, d), _vec(d)] if gated else [])
    vec_out = jax.ShapeDtypeStruct((1, d), F32)
    out_shape = [jax.ShapeDtypeStruct((s, d), F32)] + ([jax.ShapeDtypeStruct((s, d), BF16)] if gated else [])
    out_shape += [vec_out] * (4 if gated else 3)
    out_specs = [_row(ts, d)] * (2 if gated else 1) + [_vec(d)] * (4 if gated else 3)
    return _pcall(body, name=name, grid=(s // ts,), in_specs=in_specs, out_specs=out_specs, out_shape=out_shape,
                  compiler_params=_cparams(("arbitrary",)))(*ins)


def _dot3(a, b, dims):
    a1, a2, _ = _split3(a)
    b1, b2, _ = _split3(b)
    dot = functools.partial(lax.dot_general, dimension_numbers=dims, preferred_element_type=F32)
    return dot(a1, b1) + (dot(a1, b2) + dot(a2, b1))


def _mod_fwd(c_all, w, b_cols):
    nb, d = c_all.shape
    n = w.shape[1]
    tk = _pick(d, 512, 128)
    nk = d // tk

    def body(c_ref, w_ref, b_ref, act_ref, out_ref):
        k = pl.program_id(0)
        cv = c_ref[...]
        act = cv * (1.0 / (1.0 + jnp.exp(-cv)))
        act_ref[...] = act

        @pl.when(k == 0)
        def _():
            out_ref[...] = jnp.broadcast_to(b_ref[...], out_ref.shape)

        out_ref[...] += _dot3(act, w_ref[...], _DIMS["nn"])

    return _pcall(
        body, name="mod_fwd", grid=(nk,),
        in_specs=[pl.BlockSpec((nb, tk), lambda k: (0, k)), pl.BlockSpec((tk, n), lambda k: (k, 0)),
                  pl.BlockSpec((1, n), lambda k: (0, 0))],
        out_specs=[pl.BlockSpec((nb, tk), lambda k: (0, k)), pl.BlockSpec((nb, n), lambda k: (0, 0))],
        out_shape=[jax.ShapeDtypeStruct((nb, d), F32), jax.ShapeDtypeStruct((nb, n), F32)],
        compiler_params=_cparams(("arbitrary",)))(c_all, w, b_cols)


def _mod_wgrad(act_all, dmod_cols):
    nb, d = act_all.shape
    n = dmod_cols.shape[1]
    tm = _pick(d, 512, 128)

    def body(a_ref, d_ref, o_ref):
        o_ref[...] = _dot3(a_ref[...], d_ref[...], _DIMS["tn"])

    return _pcall(
        body, name="mod_wgrad", grid=(d // tm,),
        in_specs=[pl.BlockSpec((nb, tm), lambda i: (0, i)), pl.BlockSpec((nb, n), lambda i: (0, 0))],
        out_specs=pl.BlockSpec((tm, n), lambda i: (i, 0)), out_shape=jax.ShapeDtypeStruct((d, n), F32),
        compiler_params=_cparams(("parallel",)))(act_all, dmod_cols)


def _bias_expand(rel_t, onehot_t):
    h, _ = rel_t.shape
    n = onehot_t.shape[1]

    def body(r_ref, o_ref, out_ref):
        a, b, c = _split3(r_ref[...])
        dot = functools.partial(lax.dot_general, dimension_numbers=_DIMS["nn"], preferred_element_type=F32)
        oh = o_ref[...]
        out_ref[...] = dot(a, oh) + (dot(b, oh) + dot(c, oh))

    full = lambda shp: pl.BlockSpec(shp, lambda: (0,) * len(shp))
    return _pcall(body, name="bias_expand", in_specs=[full(rel_t.shape), full(onehot_t.shape)],
                  out_specs=full((h, n)), out_shape=jax.ShapeDtypeStruct((h, n), F32),
                  compiler_params=pltpu.CompilerParams(vmem_limit_bytes=VMEM_LIMIT))(rel_t, onehot_t)


def _bias_reduce(dbias, onehot, dsink_rows):
    h, n = dbias.shape

    def body(d_ref, o_ref, s_ref, out_ref, so_ref):
        a, b, c = _split3(d_ref[...])
        dot = functools.partial(lax.dot_general, dimension_numbers=_DIMS["nn"], preferred_element_type=F32)
        oh = o_ref[...]
        out_ref[...] = dot(a, oh) + (dot(b, oh) + dot(c, oh))
        so_ref[...] = jnp.sum(s_ref[...], axis=-1, keepdims=True)

    full = lambda shp: pl.BlockSpec(shp, lambda: (0,) * len(shp))
    return _pcall(body, name="bias_reduce", in_specs=[full(dbias.shape), full(onehot.shape), full(dsink_rows.shape)],
                  out_specs=[full((h, REL_BUCKETS)), full((h, 1))],
                  out_shape=[jax.ShapeDtypeStruct((h, REL_BUCKETS), F32), jax.ShapeDtypeStruct((h, 1), F32)],
                  compiler_params=pltpu.CompilerParams(vmem_limit_bytes=VMEM_LIMIT))(dbias, onehot, dsink_rows)


def _swa_specs(s):
    rows = SWA_GROUP * BLOCK
    q_spec = pl.BlockSpec((None, SWA_GROUP, BLOCK, SWA_DH), lambda g, n: (g, 0, n, 0))
    kv_prev = pl.BlockSpec((None, BLOCK, SWA_DH), lambda g, n: (g, jnp.maximum(n - 1, 0), 0))
    kv_cur = pl.BlockSpec((None, BLOCK, SWA_DH), lambda g, n: (g, n, 0))
    bias_spec = pl.BlockSpec((None, rows, 2 * BLOCK), lambda g, n: (g, 0, 0))
    col_spec = pl.BlockSpec((None, rows, 1), lambda g, n: (g, 0, 0))
    lse_spec = pl.BlockSpec((None, None, rows, 1), lambda g, n: (g, n, 0, 0))
    return rows, q_spec, kv_prev, kv_cur, bias_spec, col_spec, lse_spec


def _swa_scores(q_ref, kp_ref, kc_ref, bias_ref, n):
    rows = SWA_GROUP * BLOCK
    q = q_ref[...].reshape(rows, SWA_DH)
    kb = jnp.concatenate([kp_ref[...], kc_ref[...]], axis=0)
    s = lax.dot_general(q, kb, _DIMS["nt"], preferred_element_type=F32) * SWA_SCALE + bias_ref[...]
    col = lax.broadcasted_iota(jnp.int32, s.shape, 1)
    s = jnp.where(jnp.logical_and(n == 0, col < BLOCK), -jnp.inf, s)
    return q, kb, s


def _swa_fwd(q, k, v, bias, sink_rows):
    s = q.shape[2]
    nb = s // BLOCK
    rows, q_spec, kv_prev, kv_cur, bias_spec, col_spec, lse_spec = _swa_specs(s)

    def body(q_ref, kp_ref, kc_ref, vp_ref, vc_ref, bias_ref, sink_ref, o_ref, lse_ref):
        n = pl.program_id(1)
        _, _, sc = _swa_scores(q_ref, kp_ref, kc_ref, bias_ref, n)
        sink = sink_ref[...]
        m = jnp.maximum(jnp.max(sc, axis=-1, keepdims=True), sink)
        p = jnp.exp(sc - m)
        den = jnp.sum(p, axis=-1, keepdims=True) + jnp.exp(sink - m)
        p = p / den
        vb = jnp.concatenate([vp_ref[...], vc_ref[...]], axis=0)
        o = lax.dot_general(p.astype(BF16), vb, _DIMS["nn"], preferred_element_type=F32)
        o_ref[...] = o.reshape(SWA_GROUP, BLOCK, SWA_DH).astype(BF16)
        lse_ref[...] = m + jnp.log(den)

    return _pcall(
        body, name="swa_fwd", grid=(SWA_KV, nb),
        in_specs=[q_spec, kv_prev, kv_cur, kv_prev, kv_cur, bias_spec, col_spec],
        out_specs=[q_spec, lse_spec],
        out_shape=[jax.ShapeDtypeStruct(q.shape, BF16), jax.ShapeDtypeStruct((SWA_KV, nb, rows, 1), F32)],
        compiler_params=_cparams(("parallel", "parallel")))(q, k, k, v, v, bias, sink_rows)


def _swa_bwd(q, k, v, do, lse, bias, sink_rows):
    s = q.shape[2]
    nb = s // BLOCK
    rows, q_spec, kv_prev, kv_cur, bias_spec, col_spec, lse_spec = _swa_specs(s)

    def body(q_ref, kp_ref, kc_ref, vp_ref, vc_ref, do_ref, lse_ref, bias_ref, sink_ref,
             dq_ref, dkp_ref, dkc_ref, dvp_ref, dvc_ref, dbias_ref, dsink_ref):
        n = pl.program_id(1)

        @pl.when(n == 0)
        def _():
            dbias_ref[...] = jnp.zeros_like(dbias_ref)
            dsink_ref[...] = jnp.zeros_like(dsink_ref)

        qv, kb, sc = _swa_scores(q_ref, kp_ref, kc_ref, bias_ref, n)
        lse_v = lse_ref[...]
        p = jnp.exp(sc - lse_v)
        p_sink = jnp.exp(sink_ref[...] - lse_v)
        dov = do_ref[...].reshape(rows, SWA_DH)
        vb = jnp.concatenate([vp_ref[...], vc_ref[...]], axis=0)
        dp = lax.dot_general(dov, vb, _DIMS["nt"], preferred_element_type=F32)
        delta = jnp.sum(p * dp, axis=-1, keepdims=True)
        ds = p * (dp - delta)
        dbias_ref[...] += ds
        dsink_ref[...] += -p_sink * delta
        dsb = (ds * SWA_SCALE).astype(BF16)
        dq_ref[...] = lax.dot_general(dsb, kb, _DIMS["nn"], preferred_element_type=F32).reshape(SWA_GROUP, BLOCK, SWA_DH)
        dk = lax.dot_general(dsb, qv, _DIMS["tn"], preferred_element_type=F32)
        dv = lax.dot_general(p.astype(BF16), dov, _DIMS["tn"], preferred_element_type=F32)
        dkp_ref[...] = dk[:BLOCK]
        dkc_ref[...] = dk[BLOCK:]
        dvp_ref[...] = dv[:BLOCK]
        dvc_ref[...] = dv[BLOCK:]

    kv_out = jax.ShapeDtypeStruct((SWA_KV, s, SWA_DH), F32)
    return _pcall(
        body, name="swa_bwd", grid=(SWA_KV, nb),
        in_specs=[q_spec, kv_prev, kv_cur, kv_prev, kv_cur, q_spec, lse_spec, bias_spec, col_spec],
        out_specs=[q_spec, kv_cur, kv_cur, kv_cur, kv_cur, bias_spec, col_spec],
        out_shape=[jax.ShapeDtypeStruct(q.shape, F32), kv_out, kv_out, kv_out, kv_out,
                   jax.ShapeDtypeStruct(bias.shape, F32), jax.ShapeDtypeStruct(sink_rows.shape, F32)],
        compiler_params=_cparams(("arbitrary", "arbitrary")))(q, k, k, v, v, do, lse, bias, sink_rows)


def _rope_slab(slab, table):
    t = slab * table
    return t + pltpu.roll(t, ROPE, 1)


def _low_lanes(v):
    lane = lax.broadcasted_iota(jnp.int32, v.shape, 1)
    return jnp.where(lane < ROPE, v, 0.0)


def _rms(xv, g):
    r = lax.rsqrt(jnp.mean(xv * xv, axis=-1, keepdims=True) + EPS)
    return xv * r, r


def _mla_prep(proj, gq, gkv, table):
    s = proj.shape[0]
    ts = _pick(s, ROW_TILE, 16)

    def body(p_ref, gq_ref, gkv_ref, t_ref, cq_ref, ckv_ref, kr_ref):
        xq, _ = _rms(p_ref[:, 0:Q_RANK], None)
        cq_ref[...] = (xq * gq_ref[...]).astype(BF16)
        xkv, _ = _rms(p_ref[:, Q_RANK:Q_RANK + KV_RANK], None)
        ckv_ref[...] = (xkv * gkv_ref[...]).astype(BF16)
        kr_ref[...] = _low_lanes(_rope_slab(p_ref[:, Q_RANK + KV_RANK:TAIL], t_ref[...]))

    return _pcall(
        body, name="mla_prep", grid=(s // ts,),
        in_specs=[pl.BlockSpec((ts, TAIL), lambda i: (i, TAIL0 // TAIL)), _vec(Q_RANK), _vec(KV_RANK), _row(ts, 2 * ROPE)],
        out_specs=[_row(ts, Q_RANK), _row(ts, KV_RANK), _row(ts, 2 * ROPE)],
        out_shape=[jax.ShapeDtypeStruct((s, Q_RANK), BF16), jax.ShapeDtypeStruct((s, KV_RANK), BF16),
                   jax.ShapeDtypeStruct((s, 2 * ROPE), F32)],
        compiler_params=_cparams(("parallel",)))(proj, gq, gkv, table)


def _mla_prep_bwd(proj, dcq, dckv, dkr, gq, gkv, table):
    s = proj.shape[0]
    ts = _pick(s, ROW_TILE, 16)

    def norm_bwd(xv, dy, g):
        xn, r = _rms(xv, None)
        dg = jnp.sum(dy * xn, axis=0, keepdims=True)
        dxn = dy * g
        return r * (dxn - xn * jnp.mean(dxn * xn, axis=-1, keepdims=True)), dg

    def body(p_ref, dcq_ref, dckv_ref, dkr_ref, gq_ref, gkv_ref, t_ref, dt_ref, dgq_ref, dgkv_ref):
        @pl.when(pl.program_id(0) == 0)
        def _():
            dgq_ref[...] = jnp.zeros_like(dgq_ref)
            dgkv_ref[...] = jnp.zeros_like(dgkv_ref)

        dxq, dgq = norm_bwd(p_ref[:, 0:Q_RANK], dcq_ref[...], gq_ref[...])
        dxkv, dgkv = norm_bwd(p_ref[:, Q_RANK:Q_RANK + KV_RANK], dckv_ref[...], gkv_ref[...])
        dgq_ref[...] += dgq
        dgkv_ref[...] += dgkv
        d = _low_lanes(dkr_ref[...])
        dslab = (d + pltpu.roll(d, ROPE, 1)) * t_ref[...]
        dt_ref[:, 0:Q_RANK] = dxq.astype(BF16)
        dt_ref[:, Q_RANK:Q_RANK + KV_RANK] = dxkv.astype(BF16)
        dt_ref[:, Q_RANK + KV_RANK:TAIL] = dslab.astype(BF16)

    return _pcall(
        body, name="mla_prep_bwd", grid=(s // ts,),
        in_specs=[pl.BlockSpec((ts, TAIL), lambda i: (i, TAIL0 // TAIL)), _row(ts, Q_RANK), _row(ts, KV_RANK),
                  _row(ts, 2 * ROPE), _vec(Q_RANK), _vec(KV_RANK), _row(ts, 2 * ROPE)],
        out_specs=[_row(ts, TAIL), _vec(Q_RANK), _vec(KV_RANK)],
        out_shape=[jax.ShapeDtypeStruct((s, TAIL), BF16), jax.ShapeDtypeStruct((1, Q_RANK), F32),
                   jax.ShapeDtypeStruct((1, KV_RANK), F32)],
        compiler_params=_cparams(("arbitrary",)))(proj, dcq, dckv, dkr, gq, gkv, table)


def _head_specs(ts):
    tok = lambda w: pl.BlockSpec((ts, w), lambda h, i: (i, 0))
    head = lambda w: pl.BlockSpec((None, ts, w), lambda h, i: (h, i, 0))
    wgt = lambda r, c: pl.BlockSpec((None, r, c), lambda h, i: (h, 0, 0))
    return tok, head, wgt


def _mla_qkv(cq, ckv, kr, wq, wkv, table):
    s = cq.shape[0]
    ts = _pick(s, ROW_TILE, 16)
    tok, head, wgt = _head_specs(ts)

    def body(cq_ref, ckv_ref, kr_ref, wq_ref, wkv_ref, t_ref, q_ref, k_ref, v_ref):
        qf = lax.dot_general(cq_ref[...], wq_ref[...], _DIMS["nn"], preferred_element_type=F32)
        q_ref[:, 0:NOPE] = qf[:, 0:NOPE].astype(BF16)
        q_ref[:, NOPE:QW] = _rope_slab(qf[:, NOPE:QW], t_ref[...]).astype(BF16)
        kv = lax.dot_general(ckv_ref[...], wkv_ref[...], _DIMS["nn"], preferred_element_type=F32)
        k_ref[:, 0:NOPE] = kv[:, 0:NOPE].astype(BF16)
        k_ref[:, NOPE:QW] = kr_ref[...].astype(BF16)
        v_ref[...] = kv[:, NOPE:NOPE + VDIM].astype(BF16)

    return _pcall(
        body, name="mla_qkv", grid=(MLA_H, s // ts),
        in_specs=[tok(Q_RANK), tok(KV_RANK), tok(2 * ROPE), wgt(Q_RANK, QW), wgt(KV_RANK, NOPE + VDIM), tok(2 * ROPE)],
        out_specs=[head(QW), head(QW), head(VDIM)],
        out_shape=[jax.ShapeDtypeStruct((MLA_H, s, QW), BF16), jax.ShapeDtypeStruct((MLA_H, s, QW), BF16),
                   jax.ShapeDtypeStruct((MLA_H, s, VDIM), BF16)],
        compiler_params=_cparams(("parallel", "parallel")))(cq, ckv, kr, wq, wkv, table)


def _mla_qkv_bwd(dq, dk, dv, cq, ckv, wq, wkv, table):
    s = cq.shape[0]
    ts = _pick(s, ROW_TILE, 16)
    tok, head, wgt = _head_specs(ts)
    whole = lambda w: pl.BlockSpec((s, w), lambda h, i: (0, 0))

    def body(dq_ref, dk_ref, dv_ref, cq_ref, ckv_ref, wq_ref, wkv_ref, t_ref,
             dcq_ref, dckv_ref, dkr_ref, gwq_ref, gwkv_ref):
        h, i = pl.program_id(0), pl.program_id(1)
        rows = pl.ds(pl.multiple_of(i * ts, ts), ts)
        d = dq_ref[:, NOPE:QW]
        dslab = (d + pltpu.roll(d, ROPE, 1)) * t_ref[...]
        dqe = jnp.concatenate([dq_ref[:, 0:NOPE], dslab], axis=1).astype(BF16)
        dkv = jnp.concatenate([dk_ref[:, 0:NOPE], dv_ref[...]], axis=1).astype(BF16)
        dcq = lax.dot_general(dqe, wq_ref[...], _DIMS["nt"], preferred_element_type=F32)
        dckv = lax.dot_general(dkv, wkv_ref[...], _DIMS["nt"], preferred_element_type=F32)
        gwq = lax.dot_general(cq_ref[...], dqe, _DIMS["tn"], preferred_element_type=F32)
        gwkv = lax.dot_general(ckv_ref[...], dkv, _DIMS["tn"], preferred_element_type=F32)
        dkr = dk_ref[:, NOPE:QW]

        @pl.when(h == 0)
        def _():
            dcq_ref[rows, :] = dcq
            dckv_ref[rows, :] = dckv
            dkr_ref[rows, :] = dkr

        @pl.when(h > 0)
        def _():
            dcq_ref[rows, :] += dcq
            dckv_ref[rows, :] += dckv
            dkr_ref[rows, :] += dkr

        @pl.when(i == 0)
        def _():
            gwq_ref[...] = gwq
            gwkv_ref[...] = gwkv

        @pl.when(i > 0)
        def _():
            gwq_ref[...] += gwq
            gwkv_ref[...] += gwkv

    return _pcall(
        body, name="mla_qkv_bwd", grid=(MLA_H, s // ts),
        in_specs=[head(QW), head(QW), head(VDIM), tok(Q_RANK), tok(KV_RANK), wgt(Q_RANK, QW),
                  wgt(KV_RANK, NOPE + VDIM), tok(2 * ROPE)],
        out_specs=[whole(Q_RANK), whole(KV_RANK), whole(2 * ROPE), wgt(Q_RANK, QW), wgt(KV_RANK, NOPE + VDIM)],
        out_shape=[jax.ShapeDtypeStruct((s, Q_RANK), F32), jax.ShapeDtypeStruct((s, KV_RANK), F32),
                   jax.ShapeDtypeStruct((s, 2 * ROPE), F32), jax.ShapeDtypeStruct((MLA_H, Q_RANK, QW), F32),
                   jax.ShapeDtypeStruct((MLA_H, KV_RANK, NOPE + VDIM), F32)],
        compiler_params=_cparams(("arbitrary", "arbitrary")))(dq, dk, dv, cq, ckv, wq, wkv, table)


def _causal(i, j, t):
    row = i * t + lax.broadcasted_iota(jnp.int32, (t, t), 0)
    col = j * t + lax.broadcasted_iota(jnp.int32, (t, t), 1)
    return col <= row


def _mla_fwd(q, k, v):
    s = q.shape[1]
    t = _pick(s, ATT_T, 128)
    nt = s // t

    def body(q_ref, k_ref, v_ref, o_ref, lse_ref, m_ref, l_ref, acc_ref):
        i, j = pl.program_id(1), pl.program_id(2)

        @pl.when(j == 0)
        def _():
            m_ref[...] = jnp.full_like(m_ref, -jnp.inf)
            l_ref[...] = jnp.zeros_like(l_ref)
            acc_ref[...] = jnp.zeros_like(acc_ref)

        @pl.when(j <= i)
        def _():
            sc = lax.dot_general(q_ref[...], k_ref[...], _DIMS["nt"], preferred_element_type=F32) * MLA_SCALE
            sc = jnp.where(_causal(i, j, t), sc, -jnp.inf)
            m_old = m_ref[...]
            m_new = jnp.maximum(m_old, jnp.max(sc, axis=-1, keepdims=True))
            alpha = jnp.exp(m_old - m_new)
            p = jnp.exp(sc - m_new)
            l_ref[...] = alpha * l_ref[...] + jnp.sum(p, axis=-1, keepdims=True)
            acc_ref[...] = alpha * acc_ref[...] + lax.dot_general(p.astype(BF16), v_ref[...], _DIMS["nn"],
                                                                   preferred_element_type=F32)
            m_ref[...] = m_new

        @pl.when(j == nt - 1)
        def _():
            o_ref[...] = acc_ref[...] / l_ref[...]
            lse_ref[...] = m_ref[...] + jnp.log(l_ref[...])

    kv = lambda w: pl.BlockSpec((None, t, w), lambda h, i, j: (h, jnp.minimum(i, j), 0))
    return _pcall(
        body, name="mla_fwd", grid=(MLA_H, nt, nt),
        in_specs=[pl.BlockSpec((None, t, QW), lambda h, i, j: (h, i, 0)), kv(QW), kv(VDIM)],
        out_specs=[pl.BlockSpec((t, VDIM), lambda h, i, j: (i, h)), pl.BlockSpec((None, t, 1), lambda h, i, j: (h, i, 0))],
        out_shape=[jax.ShapeDtypeStruct((s, MLA_H * VDIM), F32), jax.ShapeDtypeStruct((MLA_H, s, 1), F32)],
        scratch_shapes=[pltpu.VMEM((t, 1), F32), pltpu.VMEM((t, 1), F32), pltpu.VMEM((t, VDIM), F32)],
        compiler_params=_cparams(("parallel", "parallel", "arbitrary")))(q, k, v)


def _mla_bwd(q, k, v, dmix, o, lse):
    s = q.shape[1]
    t = _pick(s, ATT_T, 128)
    nt = s // t
    o_blk0 = SWA_HEADS * SWA_DH // VDIM

    def body(q_ref, k_ref, v_ref, do_ref, o_ref, lse_ref, dq_ref, dk_ref, dv_ref, dk_acc, dv_acc):
        j, i = pl.program_id(1), pl.program_id(2)
        rows = pl.ds(pl.multiple_of(i * t, t), t)

        @pl.when(i == 0)
        def _():
            dk_acc[...] = jnp.zeros_like(dk_acc)
            dv_acc[...] = jnp.zeros_like(dv_acc)

        @pl.when(i >= j)
        def _():
            qv, kv_, dov = q_ref[...], k_ref[...], do_ref[...]
            sc = lax.dot_general(qv, kv_, _DIMS["nt"], preferred_element_type=F32) * MLA_SCALE
            p = jnp.where(_causal(i, j, t), jnp.exp(sc - lse_ref[...]), 0.0)
            dob = dov.astype(BF16)
            dp = lax.dot_general(dob, v_ref[...], _DIMS["nt"], preferred_element_type=F32)
            delta = jnp.sum(dov * o_ref[...], axis=-1, keepdims=True)
            ds = (p * (dp - delta) * MLA_SCALE).astype(BF16)
            dv_acc[...] += lax.dot_general(p.astype(BF16), dob, _DIMS["tn"], preferred_element_type=F32)
            dk_acc[...] += lax.dot_general(ds, qv, _DIMS["tn"], preferred_element_type=F32)
            dqv = lax.dot_general(ds, kv_, _DIMS["nn"], preferred_element_type=F32)

            @pl.when(j == 0)
            def _():
                dq_ref[rows, :] = dqv

            @pl.when(j > 0)
            def _():
                dq_ref[rows, :] += dqv

        @pl.when(i == nt - 1)
        def _():
            dk_ref[...] = dk_acc[...]
            dv_ref[...] = dv_acc[...]

    qi = lambda h, j, i: (h, jnp.maximum(i, j), 0)
    return _pcall(
        body, name="mla_bwd", grid=(MLA_H, nt, nt),
        in_specs=[pl.BlockSpec((None, t, QW), qi),
                  pl.BlockSpec((None, t, QW), lambda h, j, i: (h, j, 0)),
                  pl.BlockSpec((None, t, VDIM), lambda h, j, i: (h, j, 0)),
                  pl.BlockSpec((t, VDIM), lambda h, j, i: (jnp.maximum(i, j), o_blk0 + h)),
                  pl.BlockSpec((t, VDIM), lambda h, j, i: (jnp.maximum(i, j), h)),
                  pl.BlockSpec((None, t, 1), qi)],
        out_specs=[pl.BlockSpec((None, s, QW), lambda h, j, i: (h, 0, 0)),
                   pl.BlockSpec((None, t, QW), lambda h, j, i: (h, j, 0)),
                   pl.BlockSpec((None, t, VDIM), lambda h, j, i: (h, j, 0))],
        out_shape=[jax.ShapeDtypeStruct((MLA_H, s, QW), F32), jax.ShapeDtypeStruct((MLA_H, s, QW), F32),
                   jax.ShapeDtypeStruct((MLA_H, s, VDIM), F32)],
        scratch_shapes=[pltpu.VMEM((t, QW), F32), pltpu.VMEM((t, VDIM), F32)],
        compiler_params=_cparams(("arbitrary", "arbitrary", "arbitrary")))(q, k, v, dmix, o, lse)


def _adamw(name, w, g, m, v, parts):
    r, c = w.shape
    tr = r if r * c <= ADAM_ELEMS else _pick(r, max(8, ADAM_ELEMS // c // 8 * 8), 8)
    c1 = 1.0 - ADAM_B1 ** ADAM_STEP
    c2 = 1.0 - ADAM_B2 ** ADAM_STEP

    def body(w_ref, g_ref, m_ref, v_ref, go_ref, d_ref, mo_ref, vo_ref):
        if parts:
            gv = g_ref[0]
            for j in range(1, NDEV):
                gv = gv + g_ref[j]
        else:
            gv = g_ref[...]
        mv = ADAM_B1 * m_ref[...] + (1.0 - ADAM_B1) * gv
        vv = ADAM_B2 * v_ref[...] + (1.0 - ADAM_B2) * (gv * gv)
        go_ref[...] = gv
        mo_ref[...] = mv
        vo_ref[...] = vv
        d_ref[...] = -ADAM_LR * ((mv / c1) / (jnp.sqrt(vv / c2) + ADAM_EPS) + ADAM_WD * w_ref[...])

    blk = pl.BlockSpec((tr, c), lambda i: (i, 0))
    g_spec = pl.BlockSpec((NDEV, tr, c), lambda i: (0, i, 0)) if parts else blk
    out = jax.ShapeDtypeStruct((r, c), F32)
    return _pcall(body, name=name, grid=(r // tr,), in_specs=[blk, g_spec, blk, blk], out_specs=[blk] * 4,
                  out_shape=[out] * 4, compiler_params=_cparams(("parallel",)))(w, g, m, v)


def _t5_bucket(dist):
    n = jnp.maximum(dist, 0)
    max_exact = REL_BUCKETS // 2
    nf = jnp.maximum(n, 1).astype(F32)
    large = max_exact + (jnp.log(nf / max_exact) / math.log(REL_MAX_DIST / max_exact)
                         * (REL_BUCKETS - max_exact)).astype(jnp.int32)
    return jnp.where(n < max_exact, n, jnp.minimum(large, REL_BUCKETS - 1))


def _swap_halves(w, r0):
    return jnp.concatenate([w[:, r0 + ROPE // 2:r0 + ROPE], w[:, r0:r0 + ROPE // 2]], axis=1)


def _fold_swapped(g, r0, width):
    sw = g[..., width:width + ROPE]
    half = ROPE // 2
    return jnp.concatenate([g[..., :r0], g[..., r0:r0 + half] + sw[..., half:], g[..., r0 + half:r0 + ROPE] + sw[..., :half],
                            g[..., r0 + ROPE:width]], axis=-1)


def _heads_major(a, heads):
    s = a.shape[0]
    return a.reshape(s, heads, SWA_DH).transpose(1, 0, 2)


def _tokens_major(a):
    h, s, d = a.shape
    return a.transpose(1, 0, 2).reshape(s, h * d)


def kernel(x, c, w_mod, b_mod, attn_norm_g, w_in, swa_sinks, rel_bias, mla_q_norm_g, w_uq, mla_kv_norm_g, w_ukv, w_out, mlp_norm_g, w_ff1, w_ff2, final_norm_g, loss_target, m_w_mod, m_b_mod, m_attn_norm_g, m_w_in, m_swa_sinks, m_rel_bias, m_mla_q_norm_g, m_w_uq, m_mla_kv_norm_g, m_w_ukv, m_w_out, m_mlp_norm_g, m_w_ff1, m_w_ff2, m_final_norm_g, v_w_mod, v_b_mod, v_attn_norm_g, v_w_in, v_swa_sinks, v_rel_bias, v_mla_q_norm_g, v_w_uq, v_mla_kv_norm_g, v_w_ukv, v_w_out, v_mlp_norm_g, v_w_ff1, v_w_ff2, v_final_norm_g):
    s, d = x.shape[1], x.shape[2]
    ffs = w_ff1.shape[2]
    ff = ffs * NDEV
    nmod = w_mod.shape[2]
    me = 4 * lax.axis_index("x") + 2 * lax.axis_index("y") + lax.axis_index("c")
    x2d, tgt = x[0], loss_target[0]
    final_g = final_norm_g.reshape(1, d)

    w_in_l = jnp.concatenate([w_in[0], _swap_halves(w_in[0], OFF_KR)], axis=1).astype(BF16)
    w_uq_l = jnp.concatenate([w_uq[0], _swap_halves(w_uq[0], NOPE)], axis=1).astype(BF16)
    c_all, w_in_g, w_uq_g, w_ukv_g, w_out_g, w_ff1_g, w_ff2_g = _exchange(
        "gather_weights", [c, w_in_l, w_uq_l, w_ukv[0].astype(BF16), w_out[0].astype(BF16),
                           w_ff1[0].astype(BF16), w_ff2[0].astype(BF16)], gather=True)
    w_in_e = w_in_g.reshape(d, IN_EXT)
    w_out_f = w_out_g.reshape(MIX, d)
    w_ff2_f = w_ff2_g.reshape(ff, d)

    b_cols = lax.dynamic_slice(b_mod, (0, me * nmod), (1, nmod))
    act_all, mod_cols = _mod_fwd(c_all.reshape(NDEV, d), w_mod[0], b_cols)
    (mod_g,) = _exchange("gather_mod", [mod_cols], gather=True)
    mod = lax.dynamic_index_in_dim(mod_g, me, axis=1, keepdims=False).reshape(1, 6 * d)
    sh1, sc1, g1, sh2, sc2, g2 = [mod[:, i * d:(i + 1) * d] for i in range(6)]

    pos = jnp.arange(s, dtype=F32)
    inv_freq = ROPE_THETA ** (-jnp.arange(ROPE // 2, dtype=F32) / (ROPE // 2))
    ang = pos[:, None] * inv_freq[None, :]
    cos, sin = jnp.cos(ang), jnp.sin(ang)
    table = jnp.concatenate([cos, cos, -sin, sin], axis=1)
    q_loc = jnp.arange(BLOCK)[:, None]
    k_loc = jnp.arange(2 * BLOCK)[None, :]
    dist = q_loc + BLOCK - k_loc
    in_window = (dist >= 0) & (dist < BLOCK)
    onehot = (_t5_bucket(dist).reshape(-1, 1) == jnp.arange(REL_BUCKETS)[None, :]).astype(BF16)
    bias = _bias_expand(rel_bias.T, onehot.T).reshape(SWA_HEADS, BLOCK, 2 * BLOCK)
    bias = jnp.where(in_window[None], bias, -jnp.inf).reshape(SWA_KV, SWA_GROUP * BLOCK, 2 * BLOCK)
    sink_rows = jnp.broadcast_to(swa_sinks.reshape(SWA_HEADS, 1), (SWA_HEADS, BLOCK)).reshape(SWA_KV, SWA_GROUP * BLOCK, 1)

    h1 = _norm_mod("norm1", x2d, attn_norm_g, sc1, sh1)
    proj = _mm_plain("proj", h1, w_in_e, "nn", s, IN_EXT, d, F32, tn=IN_EXT)
    q_a = _heads_major(proj[:, :OFF_K].astype(BF16), SWA_HEADS).reshape(SWA_KV, SWA_GROUP, s, SWA_DH)
    k_a = _heads_major(proj[:, OFF_K:OFF_V].astype(BF16), SWA_KV)
    v_a = _heads_major(proj[:, OFF_V:OFF_CQ].astype(BF16), SWA_KV)
    o_a, lse_a = _swa_fwd(q_a, k_a, v_a, bias, sink_rows)

    cq, ckv, kr = _mla_prep(proj, mla_q_norm_g, mla_kv_norm_g, table)
    q_b, k_b, v_b = _mla_qkv(cq, ckv, kr, w_uq_g, w_ukv_g, table)
    o_b, lse_b = _mla_fwd(q_b, k_b, v_b)
    mix = jnp.concatenate([_tokens_major(o_a.reshape(SWA_HEADS, s, SWA_DH)), o_b.astype(BF16)], axis=1)

    tm, tn, tk = _pick(s, MM_TM, 128), _pick(d, MM_TN, 128), _pick(MIX, MM_TK, 128)
    row_blk = pl.BlockSpec((tm, tn), lambda i, j, q: (i, j))
    gate_blk = pl.BlockSpec((1, tn), lambda i, j, q: (0, j))

    def gated_residual(acc, ex, outs):
        outs[0][...] = acc
        outs[1][...] = ex[0][...] + ex[1][...] * acc

    y1, x2 = _mm("out_proj", mix, w_out_f, "nn", (s // tm, d // tn, MIX // tk),
                 pl.BlockSpec((tm, tk), lambda i, j, q: (i, q)), pl.BlockSpec((tk, tn), lambda i, j, q: (q, j)),
                 [jax.ShapeDtypeStruct((s, d), F32)] * 2, [row_blk, row_blk], (tm, tn), gated_residual,
                 extras=(x2d, g1), extra_specs=(row_blk, gate_blk))

    h2 = _norm_mod("norm2", x2, mlp_norm_g, sc2, sh2)
    tnf, tkd = _pick(ffs, MM_TN, 128), _pick(d, MM_TK, 128)
    rf = ffs // tnf
    ff_blk = pl.BlockSpec((tm, tnf), lambda i, j, q: (i, j))

    def relu_sq(acc, ex, outs):
        u = jnp.maximum(acc, 0.0)
        outs[0][...] = u
        outs[1][...] = (u * u).astype(BF16)

    u, uu = _mm("ff1", h2, w_ff1_g, "nn", (s // tm, ff // tnf, d // tkd),
                pl.BlockSpec((tm, tkd), lambda i, j, q: (i, q)),
                pl.BlockSpec((None, tkd, tnf), lambda i, j, q: (j // rf, q, j % rf)),
                [jax.ShapeDtypeStruct((s, ff), F32), jax.ShapeDtypeStruct((s, ff), BF16)], [ff_blk, ff_blk],
                (tm, tnf), relu_sq)
    tkf = _pick(ff, MM_TK, 128)
    y2, x3 = _mm("ff2", uu, w_ff2_f, "nn", (s // tm, d // tn, ff // tkf),
                 pl.BlockSpec((tm, tkf), lambda i, j, q: (i, q)), pl.BlockSpec((tkf, tn), lambda i, j, q: (q, j)),
                 [jax.ShapeDtypeStruct((s, d), F32)] * 2, [row_blk, row_blk], (tm, tn), gated_residual,
                 extras=(x2, g2), extra_specs=(row_blk, gate_blk))

    dx3, dy2, loss_p, dgf, dg2 = _loss_head(x3, tgt, y2, final_g, g2)
    loss = lax.psum(loss_p[0, 0], ("x", "y", "c"))

    def relu_sq_bwd(acc, ex, outs):
        outs[0][...] = (acc * (2.0 * ex[0][...])).astype(BF16)

    tnf2 = _pick(ff, MM_TN, 128)
    du = _mm("ff2_dx", dy2, w_ff2_f, "nt", (s // tm, ff // tnf2, d // tkd),
             pl.BlockSpec((tm, tkd), lambda i, j, q: (i, q)), pl.BlockSpec((tnf2, tkd), lambda i, j, q: (j, q)),
             [jax.ShapeDtypeStruct((s, ff), BF16)], [pl.BlockSpec((tm, tnf2), lambda i, j, q: (i, j))],
             (tm, tnf2), relu_sq_bwd, extras=(u,), extra_specs=(pl.BlockSpec((tm, tnf2), lambda i, j, q: (i, j)),))[0]
    gw_ff2 = _mm_plain("ff2_dw", uu, dy2, "tn", ff, d, s, F32)
    tmd, tks = _pick(d, MM_TM, 128), _pick(s, MM_TK, 128)
    gw_ff1 = _mm("ff1_dw", h2, du, "tn", (d // tmd, ff // tnf, s // tks),
                 pl.BlockSpec((tks, tmd), lambda i, j, q: (q, i)), pl.BlockSpec((tks, tnf), lambda i, j, q: (q, j)),
                 [jax.ShapeDtypeStruct((NDEV, d, ffs), F32)],
                 [pl.BlockSpec((None, tmd, tnf), lambda i, j, q: (j // rf, i, j % rf))], (tmd, tnf), _store(F32))[0]
    tkf1 = _pick(ffs, MM_TK, 128)
    rk = ffs // tkf1
    dh2 = _mm("ff1_dx", du, w_ff1_g, "nt", (s // tm, d // tn, ff // tkf1),
              pl.BlockSpec((tm, tkf1), lambda i, j, q: (i, q)),
              pl.BlockSpec((None, tn, tkf1), lambda i, j, q: (q // rk, j, q % rk)),
              [jax.ShapeDtypeStruct((s, d), F32)], [row_blk], (tm, tn), _store(F32))[0]
    dx2, dy1, dsc2, dsh2, dgm, dg1 = _norm_mod_bwd("norm2_bwd", x2, dh2, dx3, mlp_norm_g, sc2, y1, g1)

    dmix = _mm_plain("out_proj_dx", dy1, w_out_f, "nt", s, MIX, d, F32)
    gw_out = _mm_plain("out_proj_dw", mix, dy1, "tn", MIX, d, s, F32)

    dq_b, dk_b, dv_b = _mla_bwd(q_b, k_b, v_b, dmix, o_b, lse_b)
    dcq, dckv, dkr, gw_uq_e, gw_ukv = _mla_qkv_bwd(dq_b, dk_b, dv_b, cq, ckv, w_uq_g, w_ukv_g, table)
    dtail, dgq, dgkv = _mla_prep_bwd(proj, dcq, dckv, dkr, mla_q_norm_g, mla_kv_norm_g, table)

    do_a = _heads_major(dmix[:, :OFF_K].astype(BF16), SWA_HEADS).reshape(SWA_KV, SWA_GROUP, s, SWA_DH)
    dq_a, dkp, dkc, dvp, dvc, dbias, dsink = _swa_bwd(q_a, k_a, v_a, do_a, lse_a, bias, sink_rows)
    shift = lambda p: jnp.concatenate([p[:, BLOCK:], jnp.zeros_like(p[:, :BLOCK])], axis=1)
    dk_a, dv_a = dkc + shift(dkp), dvc + shift(dvp)
    drel_t, dsinks = _bias_reduce(dbias.reshape(SWA_HEADS, BLOCK * 2 * BLOCK), onehot, dsink.reshape(SWA_HEADS, BLOCK))
    dproj = jnp.concatenate([_tokens_major(dq_a.reshape(SWA_HEADS, s, SWA_DH)).astype(BF16),
                             _tokens_major(dk_a).astype(BF16), _tokens_major(dv_a).astype(BF16), dtail], axis=1)
    gw_in_e = _mm_plain("proj_dw", h1, dproj, "tn", d, IN_EXT, s, F32, tn=IN_EXT)
    dh1 = _mm_plain("proj_dx", dproj, w_in_e, "nt", s, d, IN_EXT, F32, tk=TAIL)
    gx, dsc1, dsh1, dga = _norm_mod_bwd("norm1_bwd", x2d, dh1, dx2, attn_norm_g, sc1)

    small = [jnp.concatenate([dsh1, dsc1, dg1, dsh2, dsc2, dg2], axis=1), dga, dgm, dgf, dgq, dgkv,
             dsinks.reshape(1, SWA_HEADS), drel_t.T.reshape(1, REL_BUCKETS * SWA_HEADS)]
    n_small = sum(a.shape[1] for a in small)
    n_pad = -n_small % 1024
    rows_small = (n_small + n_pad) // 128
    pad = jnp.zeros((1, n_pad), F32)
    pack = lambda parts: jnp.concatenate([p.reshape(1, -1) for p in parts] + [pad], axis=1).reshape(rows_small, 128)
    (small_g,) = _exchange("gather_small", [pack(small)], gather=True)
    small_names = (b_mod, attn_norm_g, mlp_norm_g, final_norm_g, mla_q_norm_g, mla_kv_norm_g, swa_sinks, rel_bias)
    small_m = (m_b_mod, m_attn_norm_g, m_mlp_norm_g, m_final_norm_g, m_mla_q_norm_g, m_mla_kv_norm_g, m_swa_sinks, m_rel_bias)
    small_v = (v_b_mod, v_attn_norm_g, v_mlp_norm_g, v_final_norm_g, v_mla_q_norm_g, v_mla_kv_norm_g, v_swa_sinks, v_rel_bias)
    small_out = _adamw("adamw_small", pack(small_names), small_g, pack(small_m), pack(small_v), parts=True)

    def unpack(flat):
        flat = flat.reshape(1, -1)
        out, off = [], 0
        for a in small_names:
            out.append(flat[:, off:off + a.size].reshape(a.shape))
            off += a.size
        return out

    sg, sd, sm, sv = [unpack(o) for o in small_out]

    dmod_cols = lax.dynamic_slice(small_g.reshape(NDEV, -1), (0, me * nmod), (NDEV, nmod))
    gw_mod = _mod_wgrad(act_all, dmod_cols)
    big = {"w_mod": _adamw("adamw_w_mod", w_mod[0], gw_mod, m_w_mod[0], v_w_mod[0], parts=False)}

    gw_in = _fold_swapped(gw_in_e, OFF_KR, IN_COLS).reshape(NDEV, d // NDEV, IN_COLS)
    gw_uq = _fold_swapped(gw_uq_e, NOPE, NOPE + ROPE)
    parts = _exchange("exchange_grads", [gw_in, gw_uq, gw_ukv, gw_out.reshape(NDEV, MIX // NDEV, d), gw_ff1,
                                         gw_ff2.reshape(NDEV, ffs, d)], gather=False)
    for name, w, p, m, v in (("w_in", w_in, parts[0], m_w_in, v_w_in), ("w_uq", w_uq, parts[1], m_w_uq, v_w_uq),
                             ("w_ukv", w_ukv, parts[2], m_w_ukv, v_w_ukv), ("w_out", w_out, parts[3], m_w_out, v_w_out),
                             ("w_ff1", w_ff1, parts[4], m_w_ff1, v_w_ff1), ("w_ff2", w_ff2, parts[5], m_w_ff2, v_w_ff2)):
        big[name] = _adamw("adamw_" + name, w[0], p, m[0], v[0], parts=True)

    order = ("w_mod", "b_mod", "attn_norm_g", "w_in", "swa_sinks", "rel_bias", "mla_q_norm_g", "w_uq", "mla_kv_norm_g",
             "w_ukv", "w_out", "mlp_norm_g", "w_ff1", "w_ff2", "final_norm_g")
    small_idx = {"b_mod": 0, "attn_norm_g": 1, "mlp_norm_g": 2, "final_norm_g": 3, "mla_q_norm_g": 4,
                 "mla_kv_norm_g": 5, "swa_sinks": 6, "rel_bias": 7}
    outs = []
    for kind, small_list in enumerate((sg, sd, sm, sv)):
        for name in order:
            outs.append(small_list[small_idx[name]] if name in small_idx else big[name][kind][None])
    return (loss, gx[None], *outs)
```

```python
import functools
import math

import jax
import jax.numpy as jnp
from jax import lax
from jax.experimental import pallas as pl
from jax.experimental.pallas import tpu as pltpu

F32 = jnp.float32
BF16 = jnp.bfloat16

NDEV = 8
EPS = 1e-6
BLOCK = 128
SWA_HEADS, SWA_KV, SWA_DH, SWA_GROUP = 16, 2, 64, 8
REL_BUCKETS, REL_MAX_DIST = 32, 128
MLA_H, Q_RANK, KV_RANK, NOPE, ROPE, VDIM = 8, 384, 128, 128, 64, 128
ROPE_THETA = 10000.0
OFF_K, OFF_V, OFF_CQ, OFF_CKV, OFF_KR, IN_COLS = 1024, 1152, 1280, 1664, 1792, 1856
IN_EXT = IN_COLS + ROPE
TAIL0, TAIL = OFF_CQ, IN_EXT - OFF_CQ
QW = NOPE + 2 * ROPE
MIX = SWA_HEADS * SWA_DH + MLA_H * VDIM
MLA_SCALE = (NOPE + ROPE) ** -0.5
SWA_SCALE = SWA_DH ** -0.5

ADAM_LR, ADAM_B1, ADAM_B2, ADAM_EPS, ADAM_WD, ADAM_STEP = 0.001, 0.9, 0.999, 1e-08, 0.01, 10

VMEM_LIMIT = 52 * 1024 * 1024
ROW_TILE = 256
MM_TM, MM_TN, MM_TK = 1024, 1024, 512
ATT_T = 512
ADAM_ELEMS = 128 * 1024


def _peer(x, y, c, k):
    return (1 - x if k & 4 else x, 1 - y if k & 2 else y, 1 - c if k & 1 else c)


def _comm_shapes(arrays, gather):
    if gather:
        return [jax.ShapeDtypeStruct((NDEV,) + a.shape, a.dtype) for a in arrays]
    return [jax.ShapeDtypeStruct(a.shape, a.dtype) for a in arrays]


def _comm_scratch(n):
    return [pltpu.SemaphoreType.DMA((7 * n,)), pltpu.SemaphoreType.DMA((7 * n,)), pltpu.SemaphoreType.DMA((n,))]


def _comm_copies(ins, outs, sems, gather):
    send, recv, loc = sems
    x, y, c = lax.axis_index("x"), lax.axis_index("y"), lax.axis_index("c")
    me = 4 * x + 2 * y + c

    def src(a, dest):
        return ins[a] if gather else ins[a].at[dest]

    local, sends, recvs = [], [], []
    for a in range(len(ins)):
        local.append(pltpu.make_async_copy(src(a, me), outs[a].at[me], loc.at[a]))
    for k in (1, 2, 4, 3, 5, 6, 7):
        px, py, pc = _peer(x, y, c, k)
        pid = 4 * px + 2 * py + pc
        for a in range(len(ins)):
            sem = dict(send_sem=send.at[a * 7 + k - 1], recv_sem=recv.at[a * 7 + k - 1],
                       device_id=(px, py, pc), device_id_type=pl.DeviceIdType.MESH)
            sends.append(pltpu.make_async_remote_copy(src_ref=src(a, pid), dst_ref=outs[a].at[me], **sem))
            recvs.append(pltpu.make_async_remote_copy(src_ref=src(a, pid), dst_ref=outs[a].at[pid], **sem))
    return local, sends, recvs


def _comm_start(ins, outs, sems, gather):
    local, sends, _ = _comm_copies(ins, outs, sems, gather)
    for cp in local + sends:
        cp.start()


def _comm_finish(ins, outs, sems, gather):
    local, sends, recvs = _comm_copies(ins, outs, sems, gather)
    for cp in recvs:
        cp.wait_recv()
    for cp in sends:
        cp.wait_send()
    for cp in local:
        cp.wait()


def _call(body, **kw):
    return pl.pallas_call(body, **kw)


def _pcall(body, comm=None, **kw):
    if comm is None:
        return _call(body, **kw)
    arrays, gather = comm
    n = len(arrays)
    grid = kw["grid"]
    in_specs, out_specs, out_shape = list(kw["in_specs"]), list(kw["out_specs"]), list(kw["out_shape"])
    scratch = list(kw.get("scratch_shapes", ()))
    n_in, n_out, n_scr = len(in_specs), len(out_shape), len(scratch)
    hbm = pl.BlockSpec(memory_space=pltpu.HBM)

    def carried(*refs):
        ins, cins = refs[:n_in], refs[n_in:n_in + n]
        outs, couts = refs[n_in + n:n_in + n + n_out], refs[n_in + n + n_out:n_in + 2 * n + n_out]
        scr, sems = refs[n_in + 2 * n + n_out:n_in + 2 * n + n_out + n_scr], refs[n_in + 2 * n + n_out + n_scr:]
        ids = [pl.program_id(ax) for ax in range(len(grid))]
        first = functools.reduce(jnp.logical_and, [i == 0 for i in ids])
        last = functools.reduce(jnp.logical_and, [i == g - 1 for i, g in zip(ids, grid)])

        @pl.when(first)
        def _():
            _comm_start(cins, couts, sems, gather)

        body(*ins, *outs, *scr)

        @pl.when(last)
        def _():
            _comm_finish(cins, couts, sems, gather)

    kw.update(in_specs=in_specs + [hbm] * n, out_specs=out_specs + [hbm] * n,
              out_shape=out_shape + _comm_shapes(arrays, gather), scratch_shapes=scratch + _comm_scratch(n),
              compiler_params=_cparams(("arbitrary",) * len(grid)))
    call = _call(carried, **kw)
    return lambda *args: call(*args, *arrays)


def _cparams(sem):
    return pltpu.CompilerParams(dimension_semantics=sem, vmem_limit_bytes=VMEM_LIMIT)


def _pick(n, pref, align):
    if n <= pref:
        return n
    t = (pref // align) * align
    while t >= align:
        if n % t == 0:
            return t
        t -= align
    return n


def _split3(x):
    a = x.astype(BF16)
    r = x - a.astype(F32)
    b = r.astype(BF16)
    c = (r - b.astype(F32)).astype(BF16)
    return a, b, c


def _exchange(name, arrays, gather):
    n = len(arrays)

    def body(*refs):
        ins, outs, sems = refs[:n], refs[n:2 * n], refs[2 * n:]
        _comm_start(ins, outs, sems, gather)
        _comm_finish(ins, outs, sems, gather)

    hbm = pl.BlockSpec(memory_space=pltpu.HBM)
    return _call(body, name=name, out_shape=_comm_shapes(arrays, gather), in_specs=[hbm] * n,
                 out_specs=[hbm] * n, scratch_shapes=_comm_scratch(n))(*arrays)


_DIMS = {"nn": (((1,), (0,)), ((), ())), "nt": (((1,), (1,)), ((), ())), "tn": (((0,), (0,)), ((), ()))}


def _mm(name, a, b, kind, grid, a_spec, b_spec, out_shape, out_specs, acc_shape, epilogue,
        extras=(), extra_specs=(), comm=None):
    nk, ne, no = grid[2], len(extras), len(out_shape)

    def body(*refs):
        a_ref, b_ref = refs[0], refs[1]
        ex, outs, acc = refs[2:2 + ne], refs[2 + ne:2 + ne + no], refs[-1]
        k = pl.program_id(2)

        @pl.when(k == 0)
        def _():
            acc[...] = jnp.zeros_like(acc)

        acc[...] += lax.dot_general(a_ref[...].astype(BF16), b_ref[...].astype(BF16), _DIMS[kind],
                                    preferred_element_type=F32)

        @pl.when(k == nk - 1)
        def _():
            epilogue(acc[...], ex, outs)

    return _pcall(
        body, comm=comm, name=name, grid=grid, in_specs=[a_spec, b_spec, *extra_specs], out_specs=out_specs,
        out_shape=out_shape, scratch_shapes=[pltpu.VMEM(acc_shape, F32)],
        compiler_params=_cparams(("parallel", "parallel", "arbitrary")),
    )(a, b, *extras)


def _store(dtype):
    def epi(acc, ex, outs):
        outs[0][...] = acc.astype(dtype)
    return epi


def _mm_plain(name, a, b, kind, m, n, k, out_dtype, tm=None, tn=None, tk=None):
    tm = _pick(m, tm or MM_TM, 128)
    tn = _pick(n, tn or MM_TN, 128)
    tk = _pick(k, tk or MM_TK, 128)
    a_spec = pl.BlockSpec((tk, tm), lambda i, j, q: (q, i)) if kind == "tn" else pl.BlockSpec((tm, tk), lambda i, j, q: (i, q))
    b_spec = pl.BlockSpec((tn, tk), lambda i, j, q: (j, q)) if kind == "nt" else pl.BlockSpec((tk, tn), lambda i, j, q: (q, j))
    return _mm(name, a, b, kind, (m // tm, n // tn, k // tk), a_spec, b_spec,
               [jax.ShapeDtypeStruct((m, n), out_dtype)], [pl.BlockSpec((tm, tn), lambda i, j, q: (i, j))],
               (tm, tn), _store(out_dtype))[0]


def _row(ts, d):
    return pl.BlockSpec((ts, d), lambda i: (i, 0))


def _vec(d):
    return pl.BlockSpec((1, d), lambda i: (0, 0))


def _norm_mod(name, x, gain, sc, sh):
    s, d = x.shape
    ts = _pick(s, ROW_TILE, 16)

    def body(x_ref, g_ref, sc_ref, sh_ref, h_ref):
        xv = x_ref[...]
        r = lax.rsqrt(jnp.mean(xv * xv, axis=-1, keepdims=True) + EPS)
        h_ref[...] = ((xv * r) * g_ref[...] * (1.0 + sc_ref[...]) + sh_ref[...]).astype(BF16)

    return _pcall(body, name=name, grid=(s // ts,), in_specs=[_row(ts, d), _vec(d), _vec(d), _vec(d)],
                  out_specs=_row(ts, d), out_shape=jax.ShapeDtypeStruct((s, d), BF16),
                  compiler_params=_cparams(("parallel",)))(x, gain, sc, sh)


def _loss_head(x3, tgt, y2, gf, g2):
    s, d = x3.shape
    ts = _pick(s, ROW_TILE, 16)

    def body(x_ref, t_ref, y_ref, gf_ref, g2_ref, dx_ref, dy_ref, loss_ref, dgf_ref, dg2_ref):
        @pl.when(pl.program_id(0) == 0)
        def _():
            loss_ref[...] = jnp.zeros_like(loss_ref)
            dgf_ref[...] = jnp.zeros_like(dgf_ref)
            dg2_ref[...] = jnp.zeros_like(dg2_ref)

        xv = x_ref[...]
        r = lax.rsqrt(jnp.mean(xv * xv, axis=-1, keepdims=True) + EPS)
        xn = xv * r
        err = xn * gf_ref[...] - t_ref[...]
        loss_ref[...] += 0.5 * jnp.sum(jnp.mean(err * err, axis=-1, keepdims=True), axis=0, keepdims=True)
        dout = err * (1.0 / d)
        dgf_ref[...] += jnp.sum(dout * xn, axis=0, keepdims=True)
        dxn = dout * gf_ref[...]
        dx = r * (dxn - xn * jnp.mean(dxn * xn, axis=-1, keepdims=True))
        dx_ref[...] = dx
        dy_ref[...] = (dx * g2_ref[...]).astype(BF16)
        dg2_ref[...] += jnp.sum(dx * y_ref[...], axis=0, keepdims=True)

    one = pl.BlockSpec((1, 1), lambda i: (0, 0))
    return _pcall(
        body, name="loss_head", grid=(s // ts,),
        in_specs=[_row(ts, d), _row(ts, d), _row(ts, d), _vec(d), _vec(d)],
        out_specs=[_row(ts, d), _row(ts, d), one, _vec(d), _vec(d)],
        out_shape=[jax.ShapeDtypeStruct((s, d), F32), jax.ShapeDtypeStruct((s, d), BF16),
                   jax.ShapeDtypeStruct((1, 1), F32), jax.ShapeDtypeStruct((1, d), F32),
                   jax.ShapeDtypeStruct((1, d), F32)],
        compiler_params=_cparams(("arbitrary",)))(x3, tgt, y2, gf, g2)


def _norm_mod_bwd(name, x, dh, dres, gain, sc, y_prev=None, gate=None):
    s, d = x.shape
    ts = _pick(s, ROW_TILE, 16)
    gated = y_prev is not None

    def body(*refs):
        if gated:
            x_ref, dh_ref, dr_ref, g_ref, sc_ref, y_ref, gt_ref, dx_ref, dy_ref, dsc_ref, dsh_ref, dg_ref, dgt_ref = refs
        else:
            x_ref, dh_ref, dr_ref, g_ref, sc_ref, dx_ref, dsc_ref, dsh_ref, dg_ref = refs

        @pl.when(pl.program_id(0) == 0)
        def _():
            dsc_ref[...] = jnp.zeros_like(dsc_ref)
            dsh_ref[...] = jnp.zeros_like(dsh_ref)
            dg_ref[...] = jnp.zeros_like(dg_ref)
            if gated:
                dgt_ref[...] = jnp.zeros_like(dgt_ref)

        xv, dhv = x_ref[...], dh_ref[...]
        r = lax.rsqrt(jnp.mean(xv * xv, axis=-1, keepdims=True) + EPS)
        xn = xv * r
        dsc_ref[...] += jnp.sum(dhv * (xn * g_ref[...]), axis=0, keepdims=True)
        dsh_ref[...] += jnp.sum(dhv, axis=0, keepdims=True)
        da = dhv * (1.0 + sc_ref[...])
        dg_ref[...] += jnp.sum(da * xn, axis=0, keepdims=True)
        dxn = da * g_ref[...]
        dx = dr_ref[...] + r * (dxn - xn * jnp.mean(dxn * xn, axis=-1, keepdims=True))
        dx_ref[...] = dx
        if gated:
            dy_ref[...] = (dx * gt_ref[...]).astype(BF16)
            dgt_ref[...] += jnp.sum(dx * y_ref[...], axis=0, keepdims=True)

    ins = [x, dh, dres, gain, sc] + ([y_prev, gate] if gated else [])
    in_specs = [_row(ts, d)] * 3 + [_vec(d)] * 2 + ([_row(ts, d), _vec(d)] if gated else [])
    vec_out = jax.ShapeDtypeStruct((1, d), F32)
    out_shape = [jax.ShapeDtypeStruct((s, d), F32)] + ([jax.ShapeDtypeStruct((s, d), BF16)] if gated else [])
    out_shape += [vec_out] * (4 if gated else 3)
    out_specs = [_row(ts, d)] * (2 if gated else 1) + [_vec(d)] * (4 if gated else 3)
    return _pcall(body, name=name, grid=(s // ts,), in_specs=in_specs, out_specs=out_specs, out_shape=out_shape,
                  compiler_params=_cparams(("arbitrary",)))(*ins)


def _dot3(a, b, dims):
    a1, a2, _ = _split3(a)
    b1, b2, _ = _split3(b)
    dot = functools.partial(lax.dot_general, dimension_numbers=dims, preferred_element_type=F32)
    return dot(a1, b1) + (dot(a1, b2) + dot(a2, b1))


def _mod_fwd(c_all, w, b_cols, comm=None):
    nb, d = c_all.shape
    n = w.shape[1]
    tk = _pick(d, 512, 128)
    nk = d // tk

    def body(c_ref, w_ref, b_ref, act_ref, out_ref):
        k = pl.program_id(0)
        cv = c_ref[...]
        act = cv * (1.0 / (1.0 + jnp.exp(-cv)))
        act_ref[...] = act

        @pl.when(k == 0)
        def _():
            out_ref[...] = jnp.broadcast_to(b_ref[...], out_ref.shape)

        out_ref[...] += _dot3(act, w_ref[...], _DIMS["nn"])

    return _pcall(
        body, comm=comm, name="mod_fwd", grid=(nk,),
        in_specs=[pl.BlockSpec((nb, tk), lambda k: (0, k)), pl.BlockSpec((tk, n), lambda k: (k, 0)),
                  pl.BlockSpec((1, n), lambda k: (0, 0))],
        out_specs=[pl.BlockSpec((nb, tk), lambda k: (0, k)), pl.BlockSpec((nb, n), lambda k: (0, 0))],
        out_shape=[jax.ShapeDtypeStruct((nb, d), F32), jax.ShapeDtypeStruct((nb, n), F32)],
        compiler_params=_cparams(("arbitrary",)))(c_all, w, b_cols)


def _mod_wgrad(act_all, dmod_cols):
    nb, d = act_all.shape
    n = dmod_cols.shape[1]
    tm = _pick(d, 512, 128)

    def body(a_ref, d_ref, o_ref):
        o_ref[...] = _dot3(a_ref[...], d_ref[...], _DIMS["tn"])

    return _pcall(
        body, name="mod_wgrad", grid=(d // tm,),
        in_specs=[pl.BlockSpec((nb, tm), lambda i: (0, i)), pl.BlockSpec((nb, n), lambda i: (0, 0))],
        out_specs=pl.BlockSpec((tm, n), lambda i: (i, 0)), out_shape=jax.ShapeDtypeStruct((d, n), F32),
        compiler_params=_cparams(("parallel",)))(act_all, dmod_cols)


def _bias_expand(rel_t, onehot_t):
    h, _ = rel_t.shape
    n = onehot_t.shape[1]

    def body(r_ref, o_ref, out_ref):
        a, b, c = _split3(r_ref[...])
        dot = functools.partial(lax.dot_general, dimension_numbers=_DIMS["nn"], preferred_element_type=F32)
        oh = o_ref[...]
        out_ref[...] = dot(a, oh) + (dot(b, oh) + dot(c, oh))

    full = lambda shp: pl.BlockSpec(shp, lambda: (0,) * len(shp))
    return _pcall(body, name="bias_expand", in_specs=[full(rel_t.shape), full(onehot_t.shape)],
                  out_specs=full((h, n)), out_shape=jax.ShapeDtypeStruct((h, n), F32),
                  compiler_params=pltpu.CompilerParams(vmem_limit_bytes=VMEM_LIMIT))(rel_t, onehot_t)


def _bias_reduce(dbias, onehot, dsink_rows):
    h, n = dbias.shape

    def body(d_ref, o_ref, s_ref, out_ref, so_ref):
        a, b, c = _split3(d_ref[...])
        dot = functools.partial(lax.dot_general, dimension_numbers=_DIMS["nn"], preferred_element_type=F32)
        oh = o_ref[...]
        out_ref[...] = dot(a, oh) + (dot(b, oh) + dot(c, oh))
        so_ref[...] = jnp.sum(s_ref[...], axis=-1, keepdims=True)

    full = lambda shp: pl.BlockSpec(shp, lambda: (0,) * len(shp))
    return _pcall(body, name="bias_reduce", in_specs=[full(dbias.shape), full(onehot.shape), full(dsink_rows.shape)],
                  out_specs=[full((h, REL_BUCKETS)), full((h, 1))],
                  out_shape=[jax.ShapeDtypeStruct((h, REL_BUCKETS), F32), jax.ShapeDtypeStruct((h, 1), F32)],
                  compiler_params=pltpu.CompilerParams(vmem_limit_bytes=VMEM_LIMIT))(dbias, onehot, dsink_rows)


def _swa_specs(s):
    rows = SWA_GROUP * BLOCK
    q_spec = pl.BlockSpec((None, SWA_GROUP, BLOCK, SWA_DH), lambda g, n: (g, 0, n, 0))
    kv_prev = pl.BlockSpec((None, BLOCK, SWA_DH), lambda g, n: (g, jnp.maximum(n - 1, 0), 0))
    kv_cur = pl.BlockSpec((None, BLOCK, SWA_DH), lambda g, n: (g, n, 0))
    bias_spec = pl.BlockSpec((None, rows, 2 * BLOCK), lambda g, n: (g, 0, 0))
    col_spec = pl.BlockSpec((None, rows, 1), lambda g, n: (g, 0, 0))
    lse_spec = pl.BlockSpec((None, None, rows, 1), lambda g, n: (g, n, 0, 0))
    return rows, q_spec, kv_prev, kv_cur, bias_spec, col_spec, lse_spec


def _swa_scores(q_ref, kp_ref, kc_ref, bias_ref, n):
    rows = SWA_GROUP * BLOCK
    q = q_ref[...].reshape(rows, SWA_DH)
    kb = jnp.concatenate([kp_ref[...], kc_ref[...]], axis=0)
    s = lax.dot_general(q, kb, _DIMS["nt"], preferred_element_type=F32) * SWA_SCALE + bias_ref[...]
    col = lax.broadcasted_iota(jnp.int32, s.shape, 1)
    s = jnp.where(jnp.logical_and(n == 0, col < BLOCK), -jnp.inf, s)
    return q, kb, s


def _swa_fwd(q, k, v, bias, sink_rows, comm=None):
    s = q.shape[2]
    nb = s // BLOCK
    rows, q_spec, kv_prev, kv_cur, bias_spec, col_spec, lse_spec = _swa_specs(s)

    def body(q_ref, kp_ref, kc_ref, vp_ref, vc_ref, bias_ref, sink_ref, o_ref, lse_ref):
        n = pl.program_id(1)
        _, _, sc = _swa_scores(q_ref, kp_ref, kc_ref, bias_ref, n)
        sink = sink_ref[...]
        m = jnp.maximum(jnp.max(sc, axis=-1, keepdims=True), sink)
        p = jnp.exp(sc - m)
        den = jnp.sum(p, axis=-1, keepdims=True) + jnp.exp(sink - m)
        p = p / den
        vb = jnp.concatenate([vp_ref[...], vc_ref[...]], axis=0)
        o = lax.dot_general(p.astype(BF16), vb, _DIMS["nn"], preferred_element_type=F32)
        o_ref[...] = o.reshape(SWA_GROUP, BLOCK, SWA_DH).astype(BF16)
        lse_ref[...] = m + jnp.log(den)

    return _pcall(
        body, comm=comm, name="swa_fwd", grid=(SWA_KV, nb),
        in_specs=[q_spec, kv_prev, kv_cur, kv_prev, kv_cur, bias_spec, col_spec],
        out_specs=[q_spec, lse_spec],
        out_shape=[jax.ShapeDtypeStruct(q.shape, BF16), jax.ShapeDtypeStruct((SWA_KV, nb, rows, 1), F32)],
        compiler_params=_cparams(("parallel", "parallel")))(q, k, k, v, v, bias, sink_rows)


def _swa_bwd(q, k, v, do, lse, bias, sink_rows, comm=None):
    s = q.shape[2]
    nb = s // BLOCK
    rows, q_spec, kv_prev, kv_cur, bias_spec, col_spec, lse_spec = _swa_specs(s)

    def body(q_ref, kp_ref, kc_ref, vp_ref, vc_ref, do_ref, lse_ref, bias_ref, sink_ref,
             dq_ref, dkp_ref, dkc_ref, dvp_ref, dvc_ref, dbias_ref, dsink_ref):
        n = pl.program_id(1)

        @pl.when(n == 0)
        def _():
            dbias_ref[...] = jnp.zeros_like(dbias_ref)
            dsink_ref[...] = jnp.zeros_like(dsink_ref)

        qv, kb, sc = _swa_scores(q_ref, kp_ref, kc_ref, bias_ref, n)
        lse_v = lse_ref[...]
        p = jnp.exp(sc - lse_v)
        p_sink = jnp.exp(sink_ref[...] - lse_v)
        dov = do_ref[...].reshape(rows, SWA_DH)
        vb = jnp.concatenate([vp_ref[...], vc_ref[...]], axis=0)
        dp = lax.dot_general(dov, vb, _DIMS["nt"], preferred_element_type=F32)
        delta = jnp.sum(p * dp, axis=-1, keepdims=True)
        ds = p * (dp - delta)
        dbias_ref[...] += ds
        dsink_ref[...] += -p_sink * delta
        dsb = (ds * SWA_SCALE).astype(BF16)
        dq_ref[...] = lax.dot_general(dsb, kb, _DIMS["nn"], preferred_element_type=F32).reshape(SWA_GROUP, BLOCK, SWA_DH)
        dk = lax.dot_general(dsb, qv, _DIMS["tn"], preferred_element_type=F32)
        dv = lax.dot_general(p.astype(BF16), dov, _DIMS["tn"], preferred_element_type=F32)
        dkp_ref[...] = dk[:BLOCK]
        dkc_ref[...] = dk[BLOCK:]
        dvp_ref[...] = dv[:BLOCK]
        dvc_ref[...] = dv[BLOCK:]

    kv_out = jax.ShapeDtypeStruct((SWA_KV, s, SWA_DH), F32)
    return _pcall(
        body, comm=comm, name="swa_bwd", grid=(SWA_KV, nb),
        in_specs=[q_spec, kv_prev, kv_cur, kv_prev, kv_cur, q_spec, lse_spec, bias_spec, col_spec],
        out_specs=[q_spec, kv_cur, kv_cur, kv_cur, kv_cur, bias_spec, col_spec],
        out_shape=[jax.ShapeDtypeStruct(q.shape, F32), kv_out, kv_out, kv_out, kv_out,
                   jax.ShapeDtypeStruct(bias.shape, F32), jax.ShapeDtypeStruct(sink_rows.shape, F32)],
        compiler_params=_cparams(("arbitrary", "arbitrary")))(q, k, k, v, v, do, lse, bias, sink_rows)


def _rope_slab(slab, table):
    t = slab * table
    return t + pltpu.roll(t, ROPE, 1)


def _low_lanes(v):
    lane = lax.broadcasted_iota(jnp.int32, v.shape, 1)
    return jnp.where(lane < ROPE, v, 0.0)


def _rms(xv, g):
    r = lax.rsqrt(jnp.mean(xv * xv, axis=-1, keepdims=True) + EPS)
    return xv * r, r


def _mla_prep(proj, gq, gkv, table):
    s = proj.shape[0]
    ts = _pick(s, ROW_TILE, 16)

    def body(p_ref, gq_ref, gkv_ref, t_ref, cq_ref, ckv_ref, kr_ref):
        xq, _ = _rms(p_ref[:, 0:Q_RANK], None)
        cq_ref[...] = (xq * gq_ref[...]).astype(BF16)
        xkv, _ = _rms(p_ref[:, Q_RANK:Q_RANK + KV_RANK], None)
        ckv_ref[...] = (xkv * gkv_ref[...]).astype(BF16)
        kr_ref[...] = _low_lanes(_rope_slab(p_ref[:, Q_RANK + KV_RANK:TAIL], t_ref[...]))

    return _pcall(
        body, name="mla_prep", grid=(s // ts,),
        in_specs=[pl.BlockSpec((ts, TAIL), lambda i: (i, TAIL0 // TAIL)), _vec(Q_RANK), _vec(KV_RANK), _row(ts, 2 * ROPE)],
        out_specs=[_row(ts, Q_RANK), _row(ts, KV_RANK), _row(ts, 2 * ROPE)],
        out_shape=[jax.ShapeDtypeStruct((s, Q_RANK), BF16), jax.ShapeDtypeStruct((s, KV_RANK), BF16),
                   jax.ShapeDtypeStruct((s, 2 * ROPE), F32)],
        compiler_params=_cparams(("parallel",)))(proj, gq, gkv, table)


def _mla_prep_bwd(proj, dcq, dckv, dkr, gq, gkv, table):
    s = proj.shape[0]
    ts = _pick(s, ROW_TILE, 16)

    def norm_bwd(xv, dy, g):
        xn, r = _rms(xv, None)
        dg = jnp.sum(dy * xn, axis=0, keepdims=True)
        dxn = dy * g
        return r * (dxn - xn * jnp.mean(dxn * xn, axis=-1, keepdims=True)), dg

    def body(p_ref, dcq_ref, dckv_ref, dkr_ref, gq_ref, gkv_ref, t_ref, dt_ref, dgq_ref, dgkv_ref):
        @pl.when(pl.program_id(0) == 0)
        def _():
            dgq_ref[...] = jnp.zeros_like(dgq_ref)
            dgkv_ref[...] = jnp.zeros_like(dgkv_ref)

        dxq, dgq = norm_bwd(p_ref[:, 0:Q_RANK], dcq_ref[...], gq_ref[...])
        dxkv, dgkv = norm_bwd(p_ref[:, Q_RANK:Q_RANK + KV_RANK], dckv_ref[...], gkv_ref[...])
        dgq_ref[...] += dgq
        dgkv_ref[...] += dgkv
        d = _low_lanes(dkr_ref[...])
        dslab = (d + pltpu.roll(d, ROPE, 1)) * t_ref[...]
        dt_ref[:, 0:Q_RANK] = dxq.astype(BF16)
        dt_ref[:, Q_RANK:Q_RANK + KV_RANK] = dxkv.astype(BF16)
        dt_ref[:, Q_RANK + KV_RANK:TAIL] = dslab.astype(BF16)

    return _pcall(
        body, name="mla_prep_bwd", grid=(s // ts,),
        in_specs=[pl.BlockSpec((ts, TAIL), lambda i: (i, TAIL0 // TAIL)), _row(ts, Q_RANK), _row(ts, KV_RANK),
                  _row(ts, 2 * ROPE), _vec(Q_RANK), _vec(KV_RANK), _row(ts, 2 * ROPE)],
        out_specs=[_row(ts, TAIL), _vec(Q_RANK), _vec(KV_RANK)],
        out_shape=[jax.ShapeDtypeStruct((s, TAIL), BF16), jax.ShapeDtypeStruct((1, Q_RANK), F32),
                   jax.ShapeDtypeStruct((1, KV_RANK), F32)],
        compiler_params=_cparams(("arbitrary",)))(proj, dcq, dckv, dkr, gq, gkv, table)


def _head_specs(ts):
    tok = lambda w: pl.BlockSpec((ts, w), lambda h, i: (i, 0))
    head = lambda w: pl.BlockSpec((None, ts, w), lambda h, i: (h, i, 0))
    wgt = lambda r, c: pl.BlockSpec((None, r, c), lambda h, i: (h, 0, 0))
    return tok, head, wgt


def _mla_qkv(cq, ckv, kr, wq, wkv, table):
    s = cq.shape[0]
    ts = _pick(s, ROW_TILE, 16)
    tok, head, wgt = _head_specs(ts)

    def body(cq_ref, ckv_ref, kr_ref, wq_ref, wkv_ref, t_ref, q_ref, k_ref, v_ref):
        qf = lax.dot_general(cq_ref[...], wq_ref[...], _DIMS["nn"], preferred_element_type=F32)
        q_ref[:, 0:NOPE] = qf[:, 0:NOPE].astype(BF16)
        q_ref[:, NOPE:QW] = _rope_slab(qf[:, NOPE:QW], t_ref[...]).astype(BF16)
        kv = lax.dot_general(ckv_ref[...], wkv_ref[...], _DIMS["nn"], preferred_element_type=F32)
        k_ref[:, 0:NOPE] = kv[:, 0:NOPE].astype(BF16)
        k_ref[:, NOPE:QW] = kr_ref[...].astype(BF16)
        v_ref[...] = kv[:, NOPE:NOPE + VDIM].astype(BF16)

    return _pcall(
        body, name="mla_qkv", grid=(MLA_H, s // ts),
        in_specs=[tok(Q_RANK), tok(KV_RANK), tok(2 * ROPE), wgt(Q_RANK, QW), wgt(KV_RANK, NOPE + VDIM), tok(2 * ROPE)],
        out_specs=[head(QW), head(QW), head(VDIM)],
        out_shape=[jax.ShapeDtypeStruct((MLA_H, s, QW), BF16), jax.ShapeDtypeStruct((MLA_H, s, QW), BF16),
                   jax.ShapeDtypeStruct((MLA_H, s, VDIM), BF16)],
        compiler_params=_cparams(("parallel", "parallel")))(cq, ckv, kr, wq, wkv, table)


def _mla_qkv_bwd(dq, dk, dv, cq, ckv, wq, wkv, table):
    s = cq.shape[0]
    ts = _pick(s, ROW_TILE, 16)
    tok, head, wgt = _head_specs(ts)
    whole = lambda w: pl.BlockSpec((s, w), lambda h, i: (0, 0))

    def body(dq_ref, dk_ref, dv_ref, cq_ref, ckv_ref, wq_ref, wkv_ref, t_ref,
             dcq_ref, dckv_ref, dkr_ref, gwq_ref, gwkv_ref):
        h, i = pl.program_id(0), pl.program_id(1)
        rows = pl.ds(pl.multiple_of(i * ts, ts), ts)
        d = dq_ref[:, NOPE:QW]
        dslab = (d + pltpu.roll(d, ROPE, 1)) * t_ref[...]
        dqe = jnp.concatenate([dq_ref[:, 0:NOPE], dslab], axis=1).astype(BF16)
        dkv = jnp.concatenate([dk_ref[:, 0:NOPE], dv_ref[...]], axis=1).astype(BF16)
        dcq = lax.dot_general(dqe, wq_ref[...], _DIMS["nt"], preferred_element_type=F32)
        dckv = lax.dot_general(dkv, wkv_ref[...], _DIMS["nt"], preferred_element_type=F32)
        gwq = lax.dot_general(cq_ref[...], dqe, _DIMS["tn"], preferred_element_type=F32)
        gwkv = lax.dot_general(ckv_ref[...], dkv, _DIMS["tn"], preferred_element_type=F32)
        dkr = dk_ref[:, NOPE:QW]

        @pl.when(h == 0)
        def _():
            dcq_ref[rows, :] = dcq
            dckv_ref[rows, :] = dckv
            dkr_ref[rows, :] = dkr

        @pl.when(h > 0)
        def _():
            dcq_ref[rows, :] += dcq
            dckv_ref[rows, :] += dckv
            dkr_ref[rows, :] += dkr

        @pl.when(i == 0)
        def _():
            gwq_ref[...] = gwq
            gwkv_ref[...] = gwkv

        @pl.when(i > 0)
        def _():
            gwq_ref[...] += gwq
            gwkv_ref[...] += gwkv

    return _pcall(
        body, name="mla_qkv_bwd", grid=(MLA_H, s // ts),
        in_specs=[head(QW), head(QW), head(VDIM), tok(Q_RANK), tok(KV_RANK), wgt(Q_RANK, QW),
                  wgt(KV_RANK, NOPE + VDIM), tok(2 * ROPE)],
        out_specs=[whole(Q_RANK), whole(KV_RANK), whole(2 * ROPE), wgt(Q_RANK, QW), wgt(KV_RANK, NOPE + VDIM)],
        out_shape=[jax.ShapeDtypeStruct((s, Q_RANK), F32), jax.ShapeDtypeStruct((s, KV_RANK), F32),
                   jax.ShapeDtypeStruct((s, 2 * ROPE), F32), jax.ShapeDtypeStruct((MLA_H, Q_RANK, QW), F32),
                   jax.ShapeDtypeStruct((MLA_H, KV_RANK, NOPE + VDIM), F32)],
        compiler_params=_cparams(("arbitrary", "arbitrary")))(dq, dk, dv, cq, ckv, wq, wkv, table)


def _causal(i, j, t):
    row = i * t + lax.broadcasted_iota(jnp.int32, (t, t), 0)
    col = j * t + lax.broadcasted_iota(jnp.int32, (t, t), 1)
    return col <= row


def _mla_fwd(q, k, v, comm=None):
    s = q.shape[1]
    t = _pick(s, ATT_T, 128)
    nt = s // t

    def body(q_ref, k_ref, v_ref, o_ref, lse_ref, m_ref, l_ref, acc_ref):
        i, j = pl.program_id(1), pl.program_id(2)

        @pl.when(j == 0)
        def _():
            m_ref[...] = jnp.full_like(m_ref, -jnp.inf)
            l_ref[...] = jnp.zeros_like(l_ref)
            acc_ref[...] = jnp.zeros_like(acc_ref)

        @pl.when(j <= i)
        def _():
            sc = lax.dot_general(q_ref[...], k_ref[...], _DIMS["nt"], preferred_element_type=F32) * MLA_SCALE
            sc = jnp.where(_causal(i, j, t), sc, -jnp.inf)
            m_old = m_ref[...]
            m_new = jnp.maximum(m_old, jnp.max(sc, axis=-1, keepdims=True))
            alpha = jnp.exp(m_old - m_new)
            p = jnp.exp(sc - m_new)
            l_ref[...] = alpha * l_ref[...] + jnp.sum(p, axis=-1, keepdims=True)
            acc_ref[...] = alpha * acc_ref[...] + lax.dot_general(p.astype(BF16), v_ref[...], _DIMS["nn"],
                                                                   preferred_element_type=F32)
            m_ref[...] = m_new

        @pl.when(j == nt - 1)
        def _():
            o_ref[...] = acc_ref[...] / l_ref[...]
            lse_ref[...] = m_ref[...] + jnp.log(l_ref[...])

    kv = lambda w: pl.BlockSpec((None, t, w), lambda h, i, j: (h, jnp.minimum(i, j), 0))
    return _pcall(
        body, comm=comm, name="mla_fwd", grid=(MLA_H, nt, nt),
        in_specs=[pl.BlockSpec((None, t, QW), lambda h, i, j: (h, i, 0)), kv(QW), kv(VDIM)],
        out_specs=[pl.BlockSpec((t, VDIM), lambda h, i, j: (i, h)), pl.BlockSpec((None, t, 1), lambda h, i, j: (h, i, 0))],
        out_shape=[jax.ShapeDtypeStruct((s, MLA_H * VDIM), F32), jax.ShapeDtypeStruct((MLA_H, s, 1), F32)],
        scratch_shapes=[pltpu.VMEM((t, 1), F32), pltpu.VMEM((t, 1), F32), pltpu.VMEM((t, VDIM), F32)],
        compiler_params=_cparams(("parallel", "parallel", "arbitrary")))(q, k, v)


def _mla_bwd(q, k, v, dmix, o, lse, comm=None):
    s = q.shape[1]
    t = _pick(s, ATT_T, 128)
    nt = s // t
    o_blk0 = SWA_HEADS * SWA_DH // VDIM

    def body(q_ref, k_ref, v_ref, do_ref, o_ref, lse_ref, dq_ref, dk_ref, dv_ref, dk_acc, dv_acc):
        j, i = pl.program_id(1), pl.program_id(2)
        rows = pl.ds(pl.multiple_of(i * t, t), t)

        @pl.when(i == 0)
        def _():
            dk_acc[...] = jnp.zeros_like(dk_acc)
            dv_acc[...] = jnp.zeros_like(dv_acc)

        @pl.when(i >= j)
        def _():
            qv, kv_, dov = q_ref[...], k_ref[...], do_ref[...]
            sc = lax.dot_general(qv, kv_, _DIMS["nt"], preferred_element_type=F32) * MLA_SCALE
            p = jnp.where(_causal(i, j, t), jnp.exp(sc - lse_ref[...]), 0.0)
            dob = dov.astype(BF16)
            dp = lax.dot_general(dob, v_ref[...], _DIMS["nt"], preferred_element_type=F32)
            delta = jnp.sum(dov * o_ref[...], axis=-1, keepdims=True)
            ds = (p * (dp - delta) * MLA_SCALE).astype(BF16)
            dv_acc[...] += lax.dot_general(p.astype(BF16), dob, _DIMS["tn"], preferred_element_type=F32)
            dk_acc[...] += lax.dot_general(ds, qv, _DIMS["tn"], preferred_element_type=F32)
            dqv = lax.dot_general(ds, kv_, _DIMS["nn"], preferred_element_type=F32)

            @pl.when(j == 0)
            def _():
                dq_ref[rows, :] = dqv

            @pl.when(j > 0)
            def _():
                dq_ref[rows, :] += dqv

        @pl.when(i == nt - 1)
        def _():
            dk_ref[...] = dk_acc[...]
            dv_ref[...] = dv_acc[...]

    qi = lambda h, j, i: (h, jnp.maximum(i, j), 0)
    return _pcall(
        body, comm=comm, name="mla_bwd", grid=(MLA_H, nt, nt),
        in_specs=[pl.BlockSpec((None, t, QW), qi),
                  pl.BlockSpec((None, t, QW), lambda h, j, i: (h, j, 0)),
                  pl.BlockSpec((None, t, VDIM), lambda h, j, i: (h, j, 0)),
                  pl.BlockSpec((t, VDIM), lambda h, j, i: (jnp.maximum(i, j), o_blk0 + h)),
                  pl.BlockSpec((t, VDIM), lambda h, j, i: (jnp.maximum(i, j), h)),
                  pl.BlockSpec((None, t, 1), qi)],
        out_specs=[pl.BlockSpec((None, s, QW), lambda h, j, i: (h, 0, 0)),
                   pl.BlockSpec((None, t, QW), lambda h, j, i: (h, j, 0)),
                   pl.BlockSpec((None, t, VDIM), lambda h, j, i: (h, j, 0))],
        out_shape=[jax.ShapeDtypeStruct((MLA_H, s, QW), F32), jax.ShapeDtypeStruct((MLA_H, s, QW), F32),
                   jax.ShapeDtypeStruct((MLA_H, s, VDIM), F32)],
        scratch_shapes=[pltpu.VMEM((t, QW), F32), pltpu.VMEM((t, VDIM), F32)],
        compiler_params=_cparams(("arbitrary", "arbitrary", "arbitrary")))(q, k, v, dmix, o, lse)


def _adamw(name, w, g, m, v, parts):
    r, c = w.shape
    tr = r if r * c <= ADAM_ELEMS else _pick(r, max(8, ADAM_ELEMS // c // 8 * 8), 8)
    c1 = 1.0 - ADAM_B1 ** ADAM_STEP
    c2 = 1.0 - ADAM_B2 ** ADAM_STEP

    def body(w_ref, g_ref, m_ref, v_ref, go_ref, d_ref, mo_ref, vo_ref):
        if parts:
            gv = g_ref[0].astype(F32)
            for j in range(1, NDEV):
                gv = gv + g_ref[j].astype(F32)
        else:
            gv = g_ref[...]
        mv = ADAM_B1 * m_ref[...] + (1.0 - ADAM_B1) * gv
        vv = ADAM_B2 * v_ref[...] + (1.0 - ADAM_B2) * (gv * gv)
        go_ref[...] = gv
        mo_ref[...] = mv
        vo_ref[...] = vv
        d_ref[...] = -ADAM_LR * ((mv / c1) / (jnp.sqrt(vv / c2) + ADAM_EPS) + ADAM_WD * w_ref[...])

    blk = pl.BlockSpec((tr, c), lambda i: (i, 0))
    g_spec = pl.BlockSpec((NDEV, tr, c), lambda i: (0, i, 0)) if parts else blk
    out = jax.ShapeDtypeStruct((r, c), F32)
    return _pcall(body, name=name, grid=(r // tr,), in_specs=[blk, g_spec, blk, blk], out_specs=[blk] * 4,
                  out_shape=[out] * 4, compiler_params=_cparams(("parallel",)))(w, g, m, v)


def _t5_bucket(dist):
    n = jnp.maximum(dist, 0)
    max_exact = REL_BUCKETS // 2
    nf = jnp.maximum(n, 1).astype(F32)
    large = max_exact + (jnp.log(nf / max_exact) / math.log(REL_MAX_DIST / max_exact)
                         * (REL_BUCKETS - max_exact)).astype(jnp.int32)
    return jnp.where(n < max_exact, n, jnp.minimum(large, REL_BUCKETS - 1))


def _swap_halves(w, r0):
    return jnp.concatenate([w[:, r0 + ROPE // 2:r0 + ROPE], w[:, r0:r0 + ROPE // 2]], axis=1)


def _fold_swapped(g, r0, width):
    sw = g[..., width:width + ROPE]
    half = ROPE // 2
    return jnp.concatenate([g[..., :r0], g[..., r0:r0 + half] + sw[..., half:], g[..., r0 + half:r0 + ROPE] + sw[..., :half],
                            g[..., r0 + ROPE:width]], axis=-1)


def _heads_major(a, heads):
    s = a.shape[0]
    return a.reshape(s, heads, SWA_DH).transpose(1, 0, 2)


def _tokens_major(a):
    h, s, d = a.shape
    return a.transpose(1, 0, 2).reshape(s, h * d)


def kernel(x, c, w_mod, b_mod, attn_norm_g, w_in, swa_sinks, rel_bias, mla_q_norm_g, w_uq, mla_kv_norm_g, w_ukv, w_out, mlp_norm_g, w_ff1, w_ff2, final_norm_g, loss_target, m_w_mod, m_b_mod, m_attn_norm_g, m_w_in, m_swa_sinks, m_rel_bias, m_mla_q_norm_g, m_w_uq, m_mla_kv_norm_g, m_w_ukv, m_w_out, m_mlp_norm_g, m_w_ff1, m_w_ff2, m_final_norm_g, v_w_mod, v_b_mod, v_attn_norm_g, v_w_in, v_swa_sinks, v_rel_bias, v_mla_q_norm_g, v_w_uq, v_mla_kv_norm_g, v_w_ukv, v_w_out, v_mlp_norm_g, v_w_ff1, v_w_ff2, v_final_norm_g):
    s, d = x.shape[1], x.shape[2]
    ffs = w_ff1.shape[2]
    ff = ffs * NDEV
    nmod = w_mod.shape[2]
    me = 4 * lax.axis_index("x") + 2 * lax.axis_index("y") + lax.axis_index("c")
    x2d, tgt = x[0], loss_target[0]
    final_g = final_norm_g.reshape(1, d)

    w_in_l = jnp.concatenate([w_in[0], _swap_halves(w_in[0], OFF_KR)], axis=1).astype(BF16)
    w_uq_l = jnp.concatenate([w_uq[0], _swap_halves(w_uq[0], NOPE)], axis=1).astype(BF16)
    (c_all,) = _exchange("gather_c", [c], gather=True)

    b_cols = lax.dynamic_slice(b_mod, (0, me * nmod), (1, nmod))
    act_all, mod_cols, w_in_g, w_uq_g, w_ukv_g = _mod_fwd(
        c_all.reshape(NDEV, d), w_mod[0], b_cols, comm=([w_in_l, w_uq_l, w_ukv[0].astype(BF16)], True))
    w_in_e = w_in_g.reshape(d, IN_EXT)
    (mod_g,) = _exchange("gather_mod", [mod_cols], gather=True)
    mod = lax.dynamic_index_in_dim(mod_g, me, axis=1, keepdims=False).reshape(1, 6 * d)
    sh1, sc1, g1, sh2, sc2, g2 = [mod[:, i * d:(i + 1) * d] for i in range(6)]

    pos = jnp.arange(s, dtype=F32)
    inv_freq = ROPE_THETA ** (-jnp.arange(ROPE // 2, dtype=F32) / (ROPE // 2))
    ang = pos[:, None] * inv_freq[None, :]
    cos, sin = jnp.cos(ang), jnp.sin(ang)
    table = jnp.concatenate([cos, cos, -sin, sin], axis=1)
    q_loc = jnp.arange(BLOCK)[:, None]
    k_loc = jnp.arange(2 * BLOCK)[None, :]
    dist = q_loc + BLOCK - k_loc
    in_window = (dist >= 0) & (dist < BLOCK)
    onehot = (_t5_bucket(dist).reshape(-1, 1) == jnp.arange(REL_BUCKETS)[None, :]).astype(BF16)
    bias = _bias_expand(rel_bias.T, onehot.T).reshape(SWA_HEADS, BLOCK, 2 * BLOCK)
    bias = jnp.where(in_window[None], bias, -jnp.inf).reshape(SWA_KV, SWA_GROUP * BLOCK, 2 * BLOCK)
    sink_rows = jnp.broadcast_to(swa_sinks.reshape(SWA_HEADS, 1), (SWA_HEADS, BLOCK)).reshape(SWA_KV, SWA_GROUP * BLOCK, 1)

    h1 = _norm_mod("norm1", x2d, attn_norm_g, sc1, sh1)
    proj = _mm_plain("proj", h1, w_in_e, "nn", s, IN_EXT, d, F32, tn=IN_EXT)
    q_a = _heads_major(proj[:, :OFF_K].astype(BF16), SWA_HEADS).reshape(SWA_KV, SWA_GROUP, s, SWA_DH)
    k_a = _heads_major(proj[:, OFF_K:OFF_V].astype(BF16), SWA_KV)
    v_a = _heads_major(proj[:, OFF_V:OFF_CQ].astype(BF16), SWA_KV)
    o_a, lse_a, w_out_g = _swa_fwd(q_a, k_a, v_a, bias, sink_rows, comm=([w_out[0].astype(BF16)], True))
    w_out_f = w_out_g.reshape(MIX, d)

    cq, ckv, kr = _mla_prep(proj, mla_q_norm_g, mla_kv_norm_g, table)
    q_b, k_b, v_b = _mla_qkv(cq, ckv, kr, w_uq_g, w_ukv_g, table)
    o_b, lse_b, w_ff1_g = _mla_fwd(q_b, k_b, v_b, comm=([w_ff1[0].astype(BF16)], True))
    mix = jnp.concatenate([_tokens_major(o_a.reshape(SWA_HEADS, s, SWA_DH)), o_b.astype(BF16)], axis=1)

    tm, tn, tk = _pick(s, MM_TM, 128), _pick(d, MM_TN, 128), _pick(MIX, MM_TK, 128)
    row_blk = pl.BlockSpec((tm, tn), lambda i, j, q: (i, j))
    gate_blk = pl.BlockSpec((1, tn), lambda i, j, q: (0, j))

    def gated_residual(acc, ex, outs):
        outs[0][...] = acc
        outs[1][...] = ex[0][...] + ex[1][...] * acc

    y1, x2 = _mm("out_proj", mix, w_out_f, "nn", (s // tm, d // tn, MIX // tk),
                 pl.BlockSpec((tm, tk), lambda i, j, q: (i, q)), pl.BlockSpec((tk, tn), lambda i, j, q: (q, j)),
                 [jax.ShapeDtypeStruct((s, d), F32)] * 2, [row_blk, row_blk], (tm, tn), gated_residual,
                 extras=(x2d, g1), extra_specs=(row_blk, gate_blk))

    h2 = _norm_mod("norm2", x2, mlp_norm_g, sc2, sh2)
    tnf, tkd = _pick(ffs, MM_TN, 128), _pick(d, MM_TK, 128)
    rf = ffs // tnf
    ff_blk = pl.BlockSpec((tm, tnf), lambda i, j, q: (i, j))

    def relu_sq(acc, ex, outs):
        u = jnp.maximum(acc, 0.0)
        outs[0][...] = u
        outs[1][...] = (u * u).astype(BF16)

    u, uu, w_ff2_g = _mm("ff1", h2, w_ff1_g, "nn", (s // tm, ff // tnf, d // tkd),
                         pl.BlockSpec((tm, tkd), lambda i, j, q: (i, q)),
                         pl.BlockSpec((None, tkd, tnf), lambda i, j, q: (j // rf, q, j % rf)),
                         [jax.ShapeDtypeStruct((s, ff), F32), jax.ShapeDtypeStruct((s, ff), BF16)], [ff_blk, ff_blk],
                         (tm, tnf), relu_sq, comm=([w_ff2[0].astype(BF16)], True))
    w_ff2_f = w_ff2_g.reshape(ff, d)
    tkf = _pick(ff, MM_TK, 128)
    y2, x3 = _mm("ff2", uu, w_ff2_f, "nn", (s // tm, d // tn, ff // tkf),
                 pl.BlockSpec((tm, tkf), lambda i, j, q: (i, q)), pl.BlockSpec((tkf, tn), lambda i, j, q: (q, j)),
                 [jax.ShapeDtypeStruct((s, d), F32)] * 2, [row_blk, row_blk], (tm, tn), gated_residual,
                 extras=(x2, g2), extra_specs=(row_blk, gate_blk))

    dx3, dy2, loss_p, dgf, dg2 = _loss_head(x3, tgt, y2, final_g, g2)
    loss = lax.psum(loss_p[0, 0], ("x", "y", "c"))

    def relu_sq_bwd(acc, ex, outs):
        outs[0][...] = (acc * (2.0 * ex[0][...])).astype(BF16)

    tnf2 = _pick(ff, MM_TN, 128)
    du = _mm("ff2_dx", dy2, w_ff2_f, "nt", (s // tm, ff // tnf2, d // tkd),
             pl.BlockSpec((tm, tkd), lambda i, j, q: (i, q)), pl.BlockSpec((tnf2, tkd), lambda i, j, q: (j, q)),
             [jax.ShapeDtypeStruct((s, ff), BF16)], [pl.BlockSpec((tm, tnf2), lambda i, j, q: (i, j))],
             (tm, tnf2), relu_sq_bwd, extras=(u,), extra_specs=(pl.BlockSpec((tm, tnf2), lambda i, j, q: (i, j)),))[0]
    gw_ff2 = _mm_plain("ff2_dw", uu, dy2, "tn", ff, d, s, BF16)
    tmd, tks = _pick(d, MM_TM, 128), _pick(s, MM_TK, 128)
    gw_ff1, p_ff2 = _mm("ff1_dw", h2, du, "tn", (d // tmd, ff // tnf, s // tks),
                        pl.BlockSpec((tks, tmd), lambda i, j, q: (q, i)), pl.BlockSpec((tks, tnf), lambda i, j, q: (q, j)),
                        [jax.ShapeDtypeStruct((NDEV, d, ffs), BF16)],
                        [pl.BlockSpec((None, tmd, tnf), lambda i, j, q: (j // rf, i, j % rf))], (tmd, tnf), _store(BF16),
                        comm=([gw_ff2.reshape(NDEV, ffs, d)], False))
    tkf1 = _pick(ffs, MM_TK, 128)
    rk = ffs // tkf1
    dh2 = _mm("ff1_dx", du, w_ff1_g, "nt", (s // tm, d // tn, ff // tkf1),
              pl.BlockSpec((tm, tkf1), lambda i, j, q: (i, q)),
              pl.BlockSpec((None, tn, tkf1), lambda i, j, q: (q // rk, j, q % rk)),
              [jax.ShapeDtypeStruct((s, d), F32)], [row_blk], (tm, tn), _store(F32))[0]
    dx2, dy1, dsc2, dsh2, dgm, dg1 = _norm_mod_bwd("norm2_bwd", x2, dh2, dx3, mlp_norm_g, sc2, y1, g1)

    dmix = _mm_plain("out_proj_dx", dy1, w_out_f, "nt", s, MIX, d, F32)
    gw_out = _mm_plain("out_proj_dw", mix, dy1, "tn", MIX, d, s, BF16)

    dq_b, dk_b, dv_b, p_ff1 = _mla_bwd(q_b, k_b, v_b, dmix, o_b, lse_b, comm=([gw_ff1], False))
    dcq, dckv, dkr, gw_uq_e, gw_ukv = _mla_qkv_bwd(dq_b, dk_b, dv_b, cq, ckv, w_uq_g, w_ukv_g, table)
    dtail, dgq, dgkv = _mla_prep_bwd(proj, dcq, dckv, dkr, mla_q_norm_g, mla_kv_norm_g, table)

    do_a = _heads_major(dmix[:, :OFF_K].astype(BF16), SWA_HEADS).reshape(SWA_KV, SWA_GROUP, s, SWA_DH)
    dq_a, dkp, dkc, dvp, dvc, dbias, dsink, p_out = _swa_bwd(
        q_a, k_a, v_a, do_a, lse_a, bias, sink_rows, comm=([gw_out.reshape(NDEV, MIX // NDEV, d)], False))
    shift = lambda p: jnp.concatenate([p[:, BLOCK:], jnp.zeros_like(p[:, :BLOCK])], axis=1)
    dk_a, dv_a = dkc + shift(dkp), dvc + shift(dvp)
    drel_t, dsinks = _bias_reduce(dbias.reshape(SWA_HEADS, BLOCK * 2 * BLOCK), onehot, dsink.reshape(SWA_HEADS, BLOCK))
    dproj = jnp.concatenate([_tokens_major(dq_a.reshape(SWA_HEADS, s, SWA_DH)).astype(BF16),
                             _tokens_major(dk_a).astype(BF16), _tokens_major(dv_a).astype(BF16), dtail], axis=1)
    gw_in_e = _mm_plain("proj_dw", h1, dproj, "tn", d, IN_EXT, s, F32, tn=IN_EXT)
    gw_in = _fold_swapped(gw_in_e, OFF_KR, IN_COLS).reshape(NDEV, d // NDEV, IN_COLS).astype(BF16)
    gw_uq = _fold_swapped(gw_uq_e, NOPE, NOPE + ROPE).astype(BF16)
    tkt = _pick(IN_EXT, TAIL, 128)
    dh1, p_in, p_uq, p_ukv = _mm(
        "proj_dx", dproj, w_in_e, "nt", (s // tm, d // tn, IN_EXT // tkt),
        pl.BlockSpec((tm, tkt), lambda i, j, q: (i, q)), pl.BlockSpec((tn, tkt), lambda i, j, q: (j, q)),
        [jax.ShapeDtypeStruct((s, d), F32)], [row_blk], (tm, tn), _store(F32),
        comm=([gw_in, gw_uq, gw_ukv.astype(BF16)], False))
    gx, dsc1, dsh1, dga = _norm_mod_bwd("norm1_bwd", x2d, dh1, dx2, attn_norm_g, sc1)

    small = [jnp.concatenate([dsh1, dsc1, dg1, dsh2, dsc2, dg2], axis=1), dga, dgm, dgf, dgq, dgkv,
             dsinks.reshape(1, SWA_HEADS), drel_t.T.reshape(1, REL_BUCKETS * SWA_HEADS)]
    n_small = sum(a.shape[1] for a in small)
    n_pad = -n_small % 1024
    rows_small = (n_small + n_pad) // 128
    pad = jnp.zeros((1, n_pad), F32)
    pack = lambda parts: jnp.concatenate([p.reshape(1, -1) for p in parts] + [pad], axis=1).reshape(rows_small, 128)
    (small_g,) = _exchange("gather_small", [pack(small)], gather=True)
    small_names = (b_mod, attn_norm_g, mlp_norm_g, final_norm_g, mla_q_norm_g, mla_kv_norm_g, swa_sinks, rel_bias)
    small_m = (m_b_mod, m_attn_norm_g, m_mlp_norm_g, m_final_norm_g, m_mla_q_norm_g, m_mla_kv_norm_g, m_swa_sinks, m_rel_bias)
    small_v = (v_b_mod, v_attn_norm_g, v_mlp_norm_g, v_final_norm_g, v_mla_q_norm_g, v_mla_kv_norm_g, v_swa_sinks, v_rel_bias)
    small_out = _adamw("adamw_small", pack(small_names), small_g, pack(small_m), pack(small_v), parts=True)

    def unpack(flat):
        flat = flat.reshape(1, -1)
        out, off = [], 0
        for a in small_names:
            out.append(flat[:, off:off + a.size].reshape(a.shape))
            off += a.size
        return out

    sg, sd, sm, sv = [unpack(o) for o in small_out]

    dmod_cols = lax.dynamic_slice(small_g.reshape(NDEV, -1), (0, me * nmod), (NDEV, nmod))
    gw_mod = _mod_wgrad(act_all, dmod_cols)
    big = {"w_mod": _adamw("adamw_w_mod", w_mod[0], gw_mod, m_w_mod[0], v_w_mod[0], parts=False)}

    for name, w, p, m, v in (("w_in", w_in, p_in, m_w_in, v_w_in), ("w_uq", w_uq, p_uq, m_w_uq, v_w_uq),
                             ("w_ukv", w_ukv, p_ukv, m_w_ukv, v_w_ukv), ("w_out", w_out, p_out, m_w_out, v_w_out),
                             ("w_ff1", w_ff1, p_ff1, m_w_ff1, v_w_ff1), ("w_ff2", w_ff2, p_ff2, m_w_ff2, v_w_ff2)):
        big[name] = _adamw("adamw_" + name, w[0], p, m[0], v[0], parts=True)

    order = ("w_mod", "b_mod", "attn_norm_g", "w_in", "swa_sinks", "rel_bias", "mla_q_norm_g", "w_uq", "mla_kv_norm_g",
             "w_ukv", "w_out", "mlp_norm_g", "w_ff1", "w_ff2", "final_norm_g")
    small_idx = {"b_mod": 0, "attn_norm_g": 1, "mlp_norm_g": 2, "final_norm_g": 3, "mla_q_norm_g": 4,
                 "mla_kv_norm_g": 5, "swa_sinks": 6, "rel_bias": 7}
    outs = []
    for kind, small_list in enumerate((sg, sd, sm, sv)):
        for name in order:
            outs.append(small_list[small_idx[name]] if name in small_idx else big[name][kind][None])
    return (loss, gx[None], *outs)
```

```python
import functools
import math

import jax
import jax.numpy as jnp
from jax import lax
from jax.experimental import pallas as pl
from jax.experimental.pallas import tpu as pltpu

F32 = jnp.float32
BF16 = jnp.bfloat16

NDEV = 8
EPS = 1e-6
BLOCK = 128
SWA_HEADS, SWA_KV, SWA_DH, SWA_GROUP = 16, 2, 64, 8
REL_BUCKETS, REL_MAX_DIST = 32, 128
MLA_H, Q_RANK, KV_RANK, NOPE, ROPE, VDIM = 8, 384, 128, 128, 64, 128
ROPE_THETA = 10000.0
OFF_K, OFF_V, OFF_CQ, OFF_CKV, OFF_KR, IN_COLS = 1024, 1152, 1280, 1664, 1792, 1856
IN_EXT = IN_COLS + ROPE
TAIL0, TAIL = OFF_CQ, IN_EXT - OFF_CQ
QW = NOPE + 2 * ROPE
MIX = SWA_HEADS * SWA_DH + MLA_H * VDIM
MLA_SCALE = (NOPE + ROPE) ** -0.5
SWA_SCALE = SWA_DH ** -0.5

ADAM_LR, ADAM_B1, ADAM_B2, ADAM_EPS, ADAM_WD, ADAM_STEP = 0.001, 0.9, 0.999, 1e-08, 0.01, 10

VMEM_LIMIT = 52 * 1024 * 1024
ROW_TILE = 256
MM_TM, MM_TN, MM_TK = 1024, 1024, 2048
ATT_T = 512
MLA_HB = 2
ADAM_ELEMS = 128 * 1024


def _peer(x, y, c, k):
    return (1 - x if k & 4 else x, 1 - y if k & 2 else y, 1 - c if k & 1 else c)


def _comm_shapes(arrays, gather):
    if gather:
        return [jax.ShapeDtypeStruct((NDEV,) + a.shape, a.dtype) for a in arrays]
    return [jax.ShapeDtypeStruct(a.shape, a.dtype) for a in arrays]


def _comm_scratch(n):
    return [pltpu.SemaphoreType.DMA((7 * n,)), pltpu.SemaphoreType.DMA((7 * n,)), pltpu.SemaphoreType.DMA((n,))]


def _comm_copies(ins, outs, sems, gather):
    send, recv, loc = sems
    x, y, c = lax.axis_index("x"), lax.axis_index("y"), lax.axis_index("c")
    me = 4 * x + 2 * y + c

    def src(a, dest):
        return ins[a] if gather else ins[a].at[dest]

    local, sends, recvs = [], [], []
    for a in range(len(ins)):
        local.append(functools.partial(pltpu.make_async_copy, src(a, me), outs[a].at[me], loc.at[a]))
    for k in (1, 2, 4, 3, 5, 6, 7):
        px, py, pc = _peer(x, y, c, k)
        pid = 4 * px + 2 * py + pc
        for a in range(len(ins)):
            sem = dict(send_sem=send.at[a * 7 + k - 1], recv_sem=recv.at[a * 7 + k - 1],
                       device_id=(px, py, pc), device_id_type=pl.DeviceIdType.MESH)
            sends.append(functools.partial(pltpu.make_async_remote_copy, src_ref=src(a, pid), dst_ref=outs[a].at[me], **sem))
            recvs.append(functools.partial(pltpu.make_async_remote_copy, src_ref=src(a, pid), dst_ref=outs[a].at[pid], **sem))
    return local, sends, recvs


def _comm_start(ins, outs, sems, gather):
    local, sends, _ = _comm_copies(ins, outs, sems, gather)
    for cp in local + sends:
        cp().start()


def _comm_finish(ins, outs, sems, gather):
    local, sends, recvs = _comm_copies(ins, outs, sems, gather)
    for cp in recvs:
        cp().wait_recv()
    for cp in sends:
        cp().wait_send()
    for cp in local:
        cp().wait()


def _call(body, **kw):
    return pl.pallas_call(body, **kw)


def _pcall(body, comm=None, **kw):
    if comm is None:
        return _call(body, **kw)
    arrays, gather = comm
    n = len(arrays)
    grid = kw["grid"]
    in_specs, out_specs, out_shape = list(kw["in_specs"]), list(kw["out_specs"]), list(kw["out_shape"])
    scratch = list(kw.get("scratch_shapes", ()))
    n_in, n_out, n_scr = len(in_specs), len(out_shape), len(scratch)
    hbm = pl.BlockSpec(memory_space=pltpu.HBM)

    def carried(*refs):
        ins, cins = refs[:n_in], refs[n_in:n_in + n]
        outs, couts = refs[n_in + n:n_in + n + n_out], refs[n_in + n + n_out:n_in + 2 * n + n_out]
        scr, sems = refs[n_in + 2 * n + n_out:n_in + 2 * n + n_out + n_scr], refs[n_in + 2 * n + n_out + n_scr:]
        ids = [pl.program_id(ax) for ax in range(len(grid))]
        first = functools.reduce(jnp.logical_and, [i == 0 for i in ids])
        last = functools.reduce(jnp.logical_and, [i == g - 1 for i, g in zip(ids, grid)])

        @pl.when(first)
        def _():
            _comm_start(cins, couts, sems, gather)

        body(*ins, *outs, *scr)

        @pl.when(last)
        def _():
            _comm_finish(cins, couts, sems, gather)

    kw.update(in_specs=in_specs + [hbm] * n, out_specs=out_specs + [hbm] * n,
              out_shape=out_shape + _comm_shapes(arrays, gather), scratch_shapes=scratch + _comm_scratch(n),
              compiler_params=_cparams(("arbitrary",) * len(grid)))
    call = _call(carried, **kw)
    return lambda *args: call(*args, *arrays)


def _cparams(sem):
    return pltpu.CompilerParams(dimension_semantics=sem, vmem_limit_bytes=VMEM_LIMIT)


def _pick(n, pref, align):
    if n <= pref:
        return n
    t = (pref // align) * align
    while t >= align:
        if n % t == 0:
            return t
        t -= align
    return n


def _split3(x):
    a = x.astype(BF16)
    r = x - a.astype(F32)
    b = r.astype(BF16)
    c = (r - b.astype(F32)).astype(BF16)
    return a, b, c


def _exchange(name, arrays, gather):
    n = len(arrays)

    def body(*refs):
        ins, outs, sems = refs[:n], refs[n:2 * n], refs[2 * n:]
        _comm_start(ins, outs, sems, gather)
        _comm_finish(ins, outs, sems, gather)

    hbm = pl.BlockSpec(memory_space=pltpu.HBM)
    return _call(body, name=name, out_shape=_comm_shapes(arrays, gather), in_specs=[hbm] * n,
                 out_specs=[hbm] * n, scratch_shapes=_comm_scratch(n))(*arrays)


_DIMS = {"nn": (((1,), (0,)), ((), ())), "nt": (((1,), (1,)), ((), ())), "tn": (((0,), (0,)), ((), ()))}


def _mm(name, a, b, kind, grid, a_spec, b_spec, out_shape, out_specs, acc_shape, epilogue,
        extras=(), extra_specs=(), comm=None):
    nk, ne, no = grid[2], len(extras), len(out_shape)

    def body(*refs):
        a_ref, b_ref = refs[0], refs[1]
        ex, outs = refs[2:2 + ne], refs[2 + ne:2 + ne + no]
        part = lax.dot_general(a_ref[...].astype(BF16), b_ref[...].astype(BF16), _DIMS[kind],
                               preferred_element_type=F32)
        if nk == 1:
            epilogue(part, ex, outs)
            return
        acc = refs[-1]
        k = pl.program_id(2)

        @pl.when(k == 0)
        def _():
            acc[...] = part

        @pl.when(jnp.logical_and(k > 0, k < nk - 1))
        def _():
            acc[...] += part

        @pl.when(k == nk - 1)
        def _():
            epilogue(acc[...] + part, ex, outs)

    return _pcall(
        body, comm=comm, name=name, grid=grid, in_specs=[a_spec, b_spec, *extra_specs], out_specs=out_specs,
        out_shape=out_shape, scratch_shapes=[pltpu.VMEM(acc_shape, F32)] if nk > 1 else [],
        compiler_params=_cparams(("parallel", "parallel", "arbitrary")),
    )(a, b, *extras)


def _store(dtype):
    def epi(acc, ex, outs):
        outs[0][...] = acc.astype(dtype)
    return epi


def _mm_plain(name, a, b, kind, m, n, k, out_dtype, tm=None, tn=None, tk=None):
    tm = _pick(m, tm or MM_TM, 128)
    tn = _pick(n, tn or MM_TN, 128)
    tk = _pick(k, tk or MM_TK, 128)
    a_spec = pl.BlockSpec((tk, tm), lambda i, j, q: (q, i)) if kind == "tn" else pl.BlockSpec((tm, tk), lambda i, j, q: (i, q))
    b_spec = pl.BlockSpec((tn, tk), lambda i, j, q: (j, q)) if kind == "nt" else pl.BlockSpec((tk, tn), lambda i, j, q: (q, j))
    return _mm(name, a, b, kind, (m // tm, n // tn, k // tk), a_spec, b_spec,
               [jax.ShapeDtypeStruct((m, n), out_dtype)], [pl.BlockSpec((tm, tn), lambda i, j, q: (i, j))],
               (tm, tn), _store(out_dtype))[0]


def _row(ts, d):
    return pl.BlockSpec((ts, d), lambda i: (i, 0))


def _vec(d):
    return pl.BlockSpec((1, d), lambda i: (0, 0))


def _norm_mod(name, x, gain, sc, sh):
    s, d = x.shape
    ts = _pick(s, ROW_TILE, 16)

    def body(x_ref, g_ref, sc_ref, sh_ref, h_ref):
        xv = x_ref[...]
        r = lax.rsqrt(jnp.mean(xv * xv, axis=-1, keepdims=True) + EPS)
        h_ref[...] = ((xv * r) * g_ref[...] * (1.0 + sc_ref[...]) + sh_ref[...]).astype(BF16)

    return _pcall(body, name=name, grid=(s // ts,), in_specs=[_row(ts, d), _vec(d), _vec(d), _vec(d)],
                  out_specs=_row(ts, d), out_shape=jax.ShapeDtypeStruct((s, d), BF16),
                  compiler_params=_cparams(("parallel",)))(x, gain, sc, sh)


def _loss_head(x3, tgt, y2, gf, g2):
    s, d = x3.shape
    ts = _pick(s, ROW_TILE, 16)

    def body(x_ref, t_ref, y_ref, gf_ref, g2_ref, dx_ref, dy_ref, loss_ref, dgf_ref, dg2_ref):
        @pl.when(pl.program_id(0) == 0)
        def _():
            loss_ref[...] = jnp.zeros_like(loss_ref)
            dgf_ref[...] = jnp.zeros_like(dgf_ref)
            dg2_ref[...] = jnp.zeros_like(dg2_ref)

        xv = x_ref[...]
        r = lax.rsqrt(jnp.mean(xv * xv, axis=-1, keepdims=True) + EPS)
        xn = xv * r
        err = xn * gf_ref[...] - t_ref[...]
        loss_ref[...] += 0.5 * jnp.sum(jnp.mean(err * err, axis=-1, keepdims=True), axis=0, keepdims=True)
        dout = err * (1.0 / d)
        dgf_ref[...] += jnp.sum(dout * xn, axis=0, keepdims=True)
        dxn = dout * gf_ref[...]
        dx = r * (dxn - xn * jnp.mean(dxn * xn, axis=-1, keepdims=True))
        dx_ref[...] = dx
        dy_ref[...] = (dx * g2_ref[...]).astype(BF16)
        dg2_ref[...] += jnp.sum(dx * y_ref[...], axis=0, keepdims=True)

    one = pl.BlockSpec((1, 1), lambda i: (0, 0))
    return _pcall(
        body, name="loss_head", grid=(s // ts,),
        in_specs=[_row(ts, d), _row(ts, d), _row(ts, d), _vec(d), _vec(d)],
        out_specs=[_row(ts, d), _row(ts, d), one, _vec(d), _vec(d)],
        out_shape=[jax.ShapeDtypeStruct((s, d), F32), jax.ShapeDtypeStruct((s, d), BF16),
                   jax.ShapeDtypeStruct((1, 1), F32), jax.ShapeDtypeStruct((1, d), F32),
                   jax.ShapeDtypeStruct((1, d), F32)],
        compiler_params=_cparams(("arbitrary",)))(x3, tgt, y2, gf, g2)


def _norm_mod_bwd(name, x, dh, dres, gain, sc, y_prev=None, gate=None):
    s, d = x.shape
    ts = _pick(s, ROW_TILE, 16)
    gated = y_prev is not None

    def body(*refs):
        if gated:
            x_ref, dh_ref, dr_ref, g_ref, sc_ref, y_ref, gt_ref, dx_ref, dy_ref, dsc_ref, dsh_ref, dg_ref, dgt_ref = refs
        else:
            x_ref, dh_ref, dr_ref, g_ref, sc_ref, dx_ref, dsc_ref, dsh_ref, dg_ref = refs

        @pl.when(pl.program_id(0) == 0)
        def _():
            dsc_ref[...] = jnp.zeros_like(dsc_ref)
            dsh_ref[...] = jnp.zeros_like(dsh_ref)
            dg_ref[...] = jnp.zeros_like(dg_ref)
            if gated:
                dgt_ref[...] = jnp.zeros_like(dgt_ref)

        xv, dhv = x_ref[...], dh_ref[...]
        r = lax.rsqrt(jnp.mean(xv * xv, axis=-1, keepdims=True) + EPS)
        xn = xv * r
        dsc_ref[...] += jnp.sum(dhv * (xn * g_ref[...]), axis=0, keepdims=True)
        dsh_ref[...] += jnp.sum(dhv, axis=0, keepdims=True)
        da = dhv * (1.0 + sc_ref[...])
        dg_ref[...] += jnp.sum(da * xn, axis=0, keepdims=True)
        dxn = da * g_ref[...]
        dx = dr_ref[...] + r * (dxn - xn * jnp.mean(dxn * xn, axis=-1, keepdims=True))
        dx_ref[...] = dx
        if gated:
            dy_ref[...] = (dx * gt_ref[...]).astype(BF16)
            dgt_ref[...] += jnp.sum(dx * y_ref[...], axis=0, keepdims=True)

    ins = [x, dh, dres, gain, sc] + ([y_prev, gate] if gated else [])
    in_specs = [_row(ts, d)] * 3 + [_vec(d)] * 2 + ([_row(ts, d), _vec(d)] if gated else [])
    vec_out = jax.ShapeDtypeStruct((1, d), F32)
    out_shape = [jax.ShapeDtypeStruct((s, d), F32)] + ([jax.ShapeDtypeStruct((s, d), BF16)] if gated else [])
    out_shape += [vec_out] * (4 if gated else 3)
    out_specs = [_row(ts, d)] * (2 if gated else 1) + [_vec(d)] * (4 if gated else 3)
    return _pcall(body, name=name, grid=(s // ts,), in_specs=in_specs, out_specs=out_specs, out_shape=out_shape,
                  compiler_params=_cparams(("arbitrary",)))(*ins)


def _dot3(a, b, dims):
    a1, a2, _ = _split3(a)
    b1, b2, _ = _split3(b)
    dot = functools.partial(lax.dot_general, dimension_numbers=dims, preferred_element_type=F32)
    return dot(a1, b1) + (dot(a1, b2) + dot(a2, b1))


def _mod_fwd(c_all, w, b_cols, comm=None):
    nb, d = c_all.shape
    n = w.shape[1]
    tk = _pick(d, 512, 128)
    nk = d // tk

    def body(c_ref, w_ref, b_ref, act_ref, out_ref):
        k = pl.program_id(0)
        cv = c_ref[...]
        act = cv * (1.0 / (1.0 + jnp.exp(-cv)))
        act_ref[...] = act

        @pl.when(k == 0)
        def _():
            out_ref[...] = jnp.broadcast_to(b_ref[...], out_ref.shape)

        out_ref[...] += _dot3(act, w_ref[...], _DIMS["nn"])

    return _pcall(
        body, comm=comm, name="mod_fwd", grid=(nk,),
        in_specs=[pl.BlockSpec((nb, tk), lambda k: (0, k)), pl.BlockSpec((tk, n), lambda k: (k, 0)),
                  pl.BlockSpec((1, n), lambda k: (0, 0))],
        out_specs=[pl.BlockSpec((nb, tk), lambda k: (0, k)), pl.BlockSpec((nb, n), lambda k: (0, 0))],
        out_shape=[jax.ShapeDtypeStruct((nb, d), F32), jax.ShapeDtypeStruct((nb, n), F32)],
        compiler_params=_cparams(("arbitrary",)))(c_all, w, b_cols)


def _mod_wgrad(act_all, dmod_cols):
    nb, d = act_all.shape
    n = dmod_cols.shape[1]
    tm = _pick(d, 512, 128)

    def body(a_ref, d_ref, o_ref):
        o_ref[...] = _dot3(a_ref[...], d_ref[...], _DIMS["tn"])

    return _pcall(
        body, name="mod_wgrad", grid=(d // tm,),
        in_specs=[pl.BlockSpec((nb, tm), lambda i: (0, i)), pl.BlockSpec((nb, n), lambda i: (0, 0))],
        out_specs=pl.BlockSpec((tm, n), lambda i: (i, 0)), out_shape=jax.ShapeDtypeStruct((d, n), F32),
        compiler_params=_cparams(("parallel",)))(act_all, dmod_cols)


def _bias_expand(rel_t, onehot_t):
    h, _ = rel_t.shape
    n = onehot_t.shape[1]

    def body(r_ref, o_ref, out_ref):
        a, b, c = _split3(r_ref[...])
        dot = functools.partial(lax.dot_general, dimension_numbers=_DIMS["nn"], preferred_element_type=F32)
        oh = o_ref[...]
        out_ref[...] = dot(a, oh) + (dot(b, oh) + dot(c, oh))

    full = lambda shp: pl.BlockSpec(shp, lambda: (0,) * len(shp))
    return _pcall(body, name="bias_expand", in_specs=[full(rel_t.shape), full(onehot_t.shape)],
                  out_specs=full((h, n)), out_shape=jax.ShapeDtypeStruct((h, n), F32),
                  compiler_params=pltpu.CompilerParams(vmem_limit_bytes=VMEM_LIMIT))(rel_t, onehot_t)


def _bias_reduce(dbias, onehot, dsink_rows):
    h, n = dbias.shape

    def body(d_ref, o_ref, s_ref, out_ref, so_ref):
        a, b, c = _split3(d_ref[...])
        dot = functools.partial(lax.dot_general, dimension_numbers=_DIMS["nn"], preferred_element_type=F32)
        oh = o_ref[...]
        out_ref[...] = dot(a, oh) + (dot(b, oh) + dot(c, oh))
        so_ref[...] = jnp.sum(s_ref[...], axis=-1, keepdims=True)

    full = lambda shp: pl.BlockSpec(shp, lambda: (0,) * len(shp))
    return _pcall(body, name="bias_reduce", in_specs=[full(dbias.shape), full(onehot.shape), full(dsink_rows.shape)],
                  out_specs=[full((h, REL_BUCKETS)), full((h, 1))],
                  out_shape=[jax.ShapeDtypeStruct((h, REL_BUCKETS), F32), jax.ShapeDtypeStruct((h, 1), F32)],
                  compiler_params=pltpu.CompilerParams(vmem_limit_bytes=VMEM_LIMIT))(dbias, onehot, dsink_rows)


def _swa_specs(s):
    rows = SWA_GROUP * BLOCK
    q_spec = pl.BlockSpec((None, SWA_GROUP, BLOCK, SWA_DH), lambda g, n: (g, 0, n, 0))
    kv_prev = pl.BlockSpec((None, BLOCK, SWA_DH), lambda g, n: (g, jnp.maximum(n - 1, 0), 0))
    kv_cur = pl.BlockSpec((None, BLOCK, SWA_DH), lambda g, n: (g, n, 0))
    bias_spec = pl.BlockSpec((None, rows, 2 * BLOCK), lambda g, n: (g, 0, 0))
    col_spec = pl.BlockSpec((None, rows, 1), lambda g, n: (g, 0, 0))
    lse_spec = pl.BlockSpec((None, None, rows, 1), lambda g, n: (g, n, 0, 0))
    return rows, q_spec, kv_prev, kv_cur, bias_spec, col_spec, lse_spec


def _swa_scores(q_ref, kp_ref, kc_ref, bias_ref, n):
    rows = SWA_GROUP * BLOCK
    q = q_ref[...].reshape(rows, SWA_DH)
    kb = jnp.concatenate([kp_ref[...], kc_ref[...]], axis=0)
    s = lax.dot_general(q, kb, _DIMS["nt"], preferred_element_type=F32) * SWA_SCALE + bias_ref[...]
    col = lax.broadcasted_iota(jnp.int32, s.shape, 1)
    s = jnp.where(jnp.logical_and(n == 0, col < BLOCK), -jnp.inf, s)
    return q, kb, s


def _swa_fwd(q, k, v, bias, sink_rows, comm=None):
    s = q.shape[2]
    nb = s // BLOCK
    rows, q_spec, kv_prev, kv_cur, bias_spec, col_spec, lse_spec = _swa_specs(s)

    def body(q_ref, kp_ref, kc_ref, vp_ref, vc_ref, bias_ref, sink_ref, o_ref, lse_ref):
        n = pl.program_id(1)
        _, _, sc = _swa_scores(q_ref, kp_ref, kc_ref, bias_ref, n)
        sink = sink_ref[...]
        m = jnp.maximum(jnp.max(sc, axis=-1, keepdims=True), sink)
        p = jnp.exp(sc - m)
        den = jnp.sum(p, axis=-1, keepdims=True) + jnp.exp(sink - m)
        p = p / den
        vb = jnp.concatenate([vp_ref[...], vc_ref[...]], axis=0)
        o = lax.dot_general(p.astype(BF16), vb, _DIMS["nn"], preferred_element_type=F32)
        o_ref[...] = o.reshape(SWA_GROUP, BLOCK, SWA_DH).astype(BF16)
        lse_ref[...] = m + jnp.log(den)

    return _pcall(
        body, comm=comm, name="swa_fwd", grid=(SWA_KV, nb),
        in_specs=[q_spec, kv_prev, kv_cur, kv_prev, kv_cur, bias_spec, col_spec],
        out_specs=[q_spec, lse_spec],
        out_shape=[jax.ShapeDtypeStruct(q.shape, BF16), jax.ShapeDtypeStruct((SWA_KV, nb, rows, 1), F32)],
        compiler_params=_cparams(("parallel", "parallel")))(q, k, k, v, v, bias, sink_rows)


def _swa_bwd(q, k, v, do, lse, bias, sink_rows, comm=None):
    s = q.shape[2]
    nb = s // BLOCK
    rows, q_spec, kv_prev, kv_cur, bias_spec, col_spec, lse_spec = _swa_specs(s)

    def body(q_ref, kp_ref, kc_ref, vp_ref, vc_ref, do_ref, lse_ref, bias_ref, sink_ref,
             dq_ref, dkp_ref, dkc_ref, dvp_ref, dvc_ref, dbias_ref, dsink_ref):
        n = pl.program_id(1)

        @pl.when(n == 0)
        def _():
            dbias_ref[...] = jnp.zeros_like(dbias_ref)
            dsink_ref[...] = jnp.zeros_like(dsink_ref)

        qv, kb, sc = _swa_scores(q_ref, kp_ref, kc_ref, bias_ref, n)
        lse_v = lse_ref[...]
        p = jnp.exp(sc - lse_v)
        p_sink = jnp.exp(sink_ref[...] - lse_v)
        dov = do_ref[...].reshape(rows, SWA_DH)
        vb = jnp.concatenate([vp_ref[...], vc_ref[...]], axis=0)
        dp = lax.dot_general(dov, vb, _DIMS["nt"], preferred_element_type=F32)
        delta = jnp.sum(p * dp, axis=-1, keepdims=True)
        ds = p * (dp - delta)
        dbias_ref[...] += ds
        dsink_ref[...] += -p_sink * delta
        dsb = (ds * SWA_SCALE).astype(BF16)
        dq_ref[...] = lax.dot_general(dsb, kb, _DIMS["nn"], preferred_element_type=F32).reshape(SWA_GROUP, BLOCK, SWA_DH)
        dk = lax.dot_general(dsb, qv, _DIMS["tn"], preferred_element_type=F32)
        dv = lax.dot_general(p.astype(BF16), dov, _DIMS["tn"], preferred_element_type=F32)
        dkp_ref[...] = dk[:BLOCK]
        dkc_ref[...] = dk[BLOCK:]
        dvp_ref[...] = dv[:BLOCK]
        dvc_ref[...] = dv[BLOCK:]

    kv_out = jax.ShapeDtypeStruct((SWA_KV, s, SWA_DH), F32)
    return _pcall(
        body, comm=comm, name="swa_bwd", grid=(SWA_KV, nb),
        in_specs=[q_spec, kv_prev, kv_cur, kv_prev, kv_cur, q_spec, lse_spec, bias_spec, col_spec],
        out_specs=[q_spec, kv_cur, kv_cur, kv_cur, kv_cur, bias_spec, col_spec],
        out_shape=[jax.ShapeDtypeStruct(q.shape, F32), kv_out, kv_out, kv_out, kv_out,
                   jax.ShapeDtypeStruct(bias.shape, F32), jax.ShapeDtypeStruct(sink_rows.shape, F32)],
        compiler_params=_cparams(("arbitrary", "arbitrary")))(q, k, k, v, v, do, lse, bias, sink_rows)


def _rope_slab(slab, table):
    t = slab * table
    return t + pltpu.roll(t, ROPE, 1)


def _low_lanes(v):
    lane = lax.broadcasted_iota(jnp.int32, v.shape, 1)
    return jnp.where(lane < ROPE, v, 0.0)


def _rms(xv, g):
    r = lax.rsqrt(jnp.mean(xv * xv, axis=-1, keepdims=True) + EPS)
    return xv * r, r


def _mla_prep(proj, gq, gkv, table):
    s = proj.shape[0]
    ts = _pick(s, ROW_TILE, 16)

    def body(p_ref, gq_ref, gkv_ref, t_ref, cq_ref, ckv_ref, kr_ref):
        xq, _ = _rms(p_ref[:, 0:Q_RANK], None)
        cq_ref[...] = (xq * gq_ref[...]).astype(BF16)
        xkv, _ = _rms(p_ref[:, Q_RANK:Q_RANK + KV_RANK], None)
        ckv_ref[...] = (xkv * gkv_ref[...]).astype(BF16)
        kr_ref[...] = _low_lanes(_rope_slab(p_ref[:, Q_RANK + KV_RANK:TAIL], t_ref[...]))

    return _pcall(
        body, name="mla_prep", grid=(s // ts,),
        in_specs=[pl.BlockSpec((ts, TAIL), lambda i: (i, TAIL0 // TAIL)), _vec(Q_RANK), _vec(KV_RANK), _row(ts, 2 * ROPE)],
        out_specs=[_row(ts, Q_RANK), _row(ts, KV_RANK), _row(ts, 2 * ROPE)],
        out_shape=[jax.ShapeDtypeStruct((s, Q_RANK), BF16), jax.ShapeDtypeStruct((s, KV_RANK), BF16),
                   jax.ShapeDtypeStruct((s, 2 * ROPE), F32)],
        compiler_params=_cparams(("parallel",)))(proj, gq, gkv, table)


def _mla_prep_bwd(proj, dcq, dckv, dkr, gq, gkv, table):
    s = proj.shape[0]
    ts = _pick(s, ROW_TILE, 16)

    def norm_bwd(xv, dy, g):
        xn, r = _rms(xv, None)
        dg = jnp.sum(dy * xn, axis=0, keepdims=True)
        dxn = dy * g
        return r * (dxn - xn * jnp.mean(dxn * xn, axis=-1, keepdims=True)), dg

    def body(p_ref, dcq_ref, dckv_ref, dkr_ref, gq_ref, gkv_ref, t_ref, dt_ref, dgq_ref, dgkv_ref):
        @pl.when(pl.program_id(0) == 0)
        def _():
            dgq_ref[...] = jnp.zeros_like(dgq_ref)
            dgkv_ref[...] = jnp.zeros_like(dgkv_ref)

        dxq, dgq = norm_bwd(p_ref[:, 0:Q_RANK], dcq_ref[...], gq_ref[...])
        dxkv, dgkv = norm_bwd(p_ref[:, Q_RANK:Q_RANK + KV_RANK], dckv_ref[...], gkv_ref[...])
        dgq_ref[...] += dgq
        dgkv_ref[...] += dgkv
        d = _low_lanes(dkr_ref[...])
        dslab = (d + pltpu.roll(d, ROPE, 1)) * t_ref[...]
        dt_ref[:, 0:Q_RANK] = dxq.astype(BF16)
        dt_ref[:, Q_RANK:Q_RANK + KV_RANK] = dxkv.astype(BF16)
        dt_ref[:, Q_RANK + KV_RANK:TAIL] = dslab.astype(BF16)

    return _pcall(
        body, name="mla_prep_bwd", grid=(s // ts,),
        in_specs=[pl.BlockSpec((ts, TAIL), lambda i: (i, TAIL0 // TAIL)), _row(ts, Q_RANK), _row(ts, KV_RANK),
                  _row(ts, 2 * ROPE), _vec(Q_RANK), _vec(KV_RANK), _row(ts, 2 * ROPE)],
        out_specs=[_row(ts, TAIL), _vec(Q_RANK), _vec(KV_RANK)],
        out_shape=[jax.ShapeDtypeStruct((s, TAIL), BF16), jax.ShapeDtypeStruct((1, Q_RANK), F32),
                   jax.ShapeDtypeStruct((1, KV_RANK), F32)],
        compiler_params=_cparams(("arbitrary",)))(proj, dcq, dckv, dkr, gq, gkv, table)


def _head_specs(ts):
    tok = lambda w: pl.BlockSpec((ts, w), lambda h, i: (i, 0))
    head = lambda w: pl.BlockSpec((None, ts, w), lambda h, i: (h, i, 0))
    wgt = lambda r, c: pl.BlockSpec((None, r, c), lambda h, i: (h, 0, 0))
    return tok, head, wgt


def _mla_qkv(cq, ckv, kr, wq, wkv, table):
    s = cq.shape[0]
    ts = _pick(s, ROW_TILE, 16)
    tok, head, wgt = _head_specs(ts)

    def body(cq_ref, ckv_ref, kr_ref, wq_ref, wkv_ref, t_ref, q_ref, k_ref, v_ref):
        qf = lax.dot_general(cq_ref[...], wq_ref[...], _DIMS["nn"], preferred_element_type=F32)
        q_ref[:, 0:NOPE] = qf[:, 0:NOPE].astype(BF16)
        q_ref[:, NOPE:QW] = _rope_slab(qf[:, NOPE:QW], t_ref[...]).astype(BF16)
        kv = lax.dot_general(ckv_ref[...], wkv_ref[...], _DIMS["nn"], preferred_element_type=F32)
        k_ref[:, 0:NOPE] = kv[:, 0:NOPE].astype(BF16)
        k_ref[:, NOPE:QW] = kr_ref[...].astype(BF16)
        v_ref[...] = kv[:, NOPE:NOPE + VDIM].astype(BF16)

    return _pcall(
        body, name="mla_qkv", grid=(MLA_H, s // ts),
        in_specs=[tok(Q_RANK), tok(KV_RANK), tok(2 * ROPE), wgt(Q_RANK, QW), wgt(KV_RANK, NOPE + VDIM), tok(2 * ROPE)],
        out_specs=[head(QW), head(QW), head(VDIM)],
        out_shape=[jax.ShapeDtypeStruct((MLA_H, s, QW), BF16), jax.ShapeDtypeStruct((MLA_H, s, QW), BF16),
                   jax.ShapeDtypeStruct((MLA_H, s, VDIM), BF16)],
        compiler_params=_cparams(("parallel", "parallel")))(cq, ckv, kr, wq, wkv, table)


def _mla_qkv_bwd(dq, dk, dv, cq, ckv, wq, wkv, table):
    s = cq.shape[0]
    ts = _pick(s, ROW_TILE, 16)
    tok, head, wgt = _head_specs(ts)
    whole = lambda w: pl.BlockSpec((s, w), lambda h, i: (0, 0))

    def body(dq_ref, dk_ref, dv_ref, cq_ref, ckv_ref, wq_ref, wkv_ref, t_ref,
             dcq_ref, dckv_ref, dkr_ref, gwq_ref, gwkv_ref):
        h, i = pl.program_id(0), pl.program_id(1)
        rows = pl.ds(pl.multiple_of(i * ts, ts), ts)
        d = dq_ref[:, NOPE:QW]
        dslab = (d + pltpu.roll(d, ROPE, 1)) * t_ref[...]
        dqe = jnp.concatenate([dq_ref[:, 0:NOPE], dslab], axis=1).astype(BF16)
        dkv = jnp.concatenate([dk_ref[:, 0:NOPE], dv_ref[...]], axis=1).astype(BF16)
        dcq = lax.dot_general(dqe, wq_ref[...], _DIMS["nt"], preferred_element_type=F32)
        dckv = lax.dot_general(dkv, wkv_ref[...], _DIMS["nt"], preferred_element_type=F32)
        gwq = lax.dot_general(cq_ref[...], dqe, _DIMS["tn"], preferred_element_type=F32)
        gwkv = lax.dot_general(ckv_ref[...], dkv, _DIMS["tn"], preferred_element_type=F32)
        dkr = dk_ref[:, NOPE:QW]

        @pl.when(h == 0)
        def _():
            dcq_ref[rows, :] = dcq
            dckv_ref[rows, :] = dckv
            dkr_ref[rows, :] = dkr

        @pl.when(h > 0)
        def _():
            dcq_ref[rows, :] += dcq
            dckv_ref[rows, :] += dckv
            dkr_ref[rows, :] += dkr

        @pl.when(i == 0)
        def _():
            gwq_ref[...] = gwq
            gwkv_ref[...] = gwkv

        @pl.when(i > 0)
        def _():
            gwq_ref[...] += gwq
            gwkv_ref[...] += gwkv

    return _pcall(
        body, name="mla_qkv_bwd", grid=(MLA_H, s // ts),
        in_specs=[head(QW), head(QW), head(VDIM), tok(Q_RANK), tok(KV_RANK), wgt(Q_RANK, QW),
                  wgt(KV_RANK, NOPE + VDIM), tok(2 * ROPE)],
        out_specs=[whole(Q_RANK), whole(KV_RANK), whole(2 * ROPE), wgt(Q_RANK, QW), wgt(KV_RANK, NOPE + VDIM)],
        out_shape=[jax.ShapeDtypeStruct((s, Q_RANK), F32), jax.ShapeDtypeStruct((s, KV_RANK), F32),
                   jax.ShapeDtypeStruct((s, 2 * ROPE), F32), jax.ShapeDtypeStruct((MLA_H, Q_RANK, QW), F32),
                   jax.ShapeDtypeStruct((MLA_H, KV_RANK, NOPE + VDIM), F32)],
        compiler_params=_cparams(("arbitrary", "arbitrary")))(dq, dk, dv, cq, ckv, wq, wkv, table)


def _diag_mask(t):
    return lax.broadcasted_iota(jnp.int32, (t, t), 1) <= lax.broadcasted_iota(jnp.int32, (t, t), 0)


def _mla_fwd(q, k, v, comm=None):
    s = q.shape[1]
    t = _pick(s, ATT_T, 128)
    nt = s // t
    hb = MLA_HB

    def body(q_ref, k_ref, v_ref, o_ref, lse_ref, m_ref, l_ref, acc_ref):
        i, j = pl.program_id(1), pl.program_id(2)

        @pl.when(j == 0)
        def _():
            m_ref[...] = jnp.full_like(m_ref, -jnp.inf)
            l_ref[...] = jnp.zeros_like(l_ref)
            acc_ref[...] = jnp.zeros_like(acc_ref)

        def step(diagonal):
            for h in range(hb):
                sc = lax.dot_general(q_ref[h], k_ref[h], _DIMS["nt"], preferred_element_type=F32) * MLA_SCALE
                if diagonal:
                    sc = jnp.where(_diag_mask(t), sc, -jnp.inf)
                m_old = m_ref[h]
                m_new = jnp.maximum(m_old, jnp.max(sc, axis=-1, keepdims=True))
                alpha = jnp.exp(m_old - m_new)
                p = jnp.exp(sc - m_new)
                l_ref[h] = alpha * l_ref[h] + jnp.sum(p, axis=-1, keepdims=True)
                acc_ref[h] = alpha * acc_ref[h] + lax.dot_general(p.astype(BF16), v_ref[h], _DIMS["nn"],
                                                                  preferred_element_type=F32)
                m_ref[h] = m_new

        @pl.when(j < i)
        def _():
            step(False)

        @pl.when(j == i)
        def _():
            step(True)

        @pl.when(j == nt - 1)
        def _():
            for h in range(hb):
                o_ref[:, h * VDIM:(h + 1) * VDIM] = acc_ref[h] / l_ref[h]
                lse_ref[h] = m_ref[h] + jnp.log(l_ref[h])

    kv = lambda w: pl.BlockSpec((hb, t, w), lambda h, i, j: (h, jnp.minimum(i, j), 0))
    return _pcall(
        body, comm=comm, name="mla_fwd", grid=(MLA_H // hb, nt, nt),
        in_specs=[pl.BlockSpec((hb, t, QW), lambda h, i, j: (h, i, 0)), kv(QW), kv(VDIM)],
        out_specs=[pl.BlockSpec((t, hb * VDIM), lambda h, i, j: (i, h)), pl.BlockSpec((hb, t, 1), lambda h, i, j: (h, i, 0))],
        out_shape=[jax.ShapeDtypeStruct((s, MLA_H * VDIM), F32), jax.ShapeDtypeStruct((MLA_H, s, 1), F32)],
        scratch_shapes=[pltpu.VMEM((hb, t, 1), F32), pltpu.VMEM((hb, t, 1), F32), pltpu.VMEM((hb, t, VDIM), F32)],
        compiler_params=_cparams(("parallel", "parallel", "arbitrary")))(q, k, v)


def _mla_bwd(q, k, v, dmix, o, lse, comm=None):
    s = q.shape[1]
    t = _pick(s, ATT_T, 128)
    nt = s // t
    hb = MLA_HB
    o_blk0 = SWA_HEADS * SWA_DH // (hb * VDIM)

    def body(q_ref, k_ref, v_ref, do_ref, o_ref, lse_ref, dq_ref, dk_ref, dv_ref, dk_acc, dv_acc):
        j, i = pl.program_id(1), pl.program_id(2)
        rows = pl.ds(pl.multiple_of(i * t, t), t)

        @pl.when(i == 0)
        def _():
            dk_acc[...] = jnp.zeros_like(dk_acc)
            dv_acc[...] = jnp.zeros_like(dv_acc)

        def step(diagonal):
            for h in range(hb):
                qv, kv_ = q_ref[h], k_ref[h]
                dov = do_ref[:, h * VDIM:(h + 1) * VDIM]
                sc = lax.dot_general(qv, kv_, _DIMS["nt"], preferred_element_type=F32) * MLA_SCALE
                p = jnp.exp(sc - lse_ref[h])
                if diagonal:
                    p = jnp.where(_diag_mask(t), p, 0.0)
                dob = dov.astype(BF16)
                dp = lax.dot_general(dob, v_ref[h], _DIMS["nt"], preferred_element_type=F32)
                delta = jnp.sum(dov * o_ref[:, h * VDIM:(h + 1) * VDIM], axis=-1, keepdims=True)
                ds = (p * (dp - delta) * MLA_SCALE).astype(BF16)
                dv_acc[h] += lax.dot_general(p.astype(BF16), dob, _DIMS["tn"], preferred_element_type=F32)
                dk_acc[h] += lax.dot_general(ds, qv, _DIMS["tn"], preferred_element_type=F32)
                dqv = lax.dot_general(ds, kv_, _DIMS["nn"], preferred_element_type=F32)

                @pl.when(j == 0)
                def _():
                    dq_ref[h, rows, :] = dqv

                @pl.when(j > 0)
                def _():
                    dq_ref[h, rows, :] += dqv

        @pl.when(i > j)
        def _():
            step(False)

        @pl.when(i == j)
        def _():
            step(True)

        @pl.when(i == nt - 1)
        def _():
            dk_ref[...] = dk_acc[...]
            dv_ref[...] = dv_acc[...]

    qi = lambda h, j, i: (h, jnp.maximum(i, j), 0)
    return _pcall(
        body, comm=comm, name="mla_bwd", grid=(MLA_H // hb, nt, nt),
        in_specs=[pl.BlockSpec((hb, t, QW), qi),
                  pl.BlockSpec((hb, t, QW), lambda h, j, i: (h, j, 0)),
                  pl.BlockSpec((hb, t, VDIM), lambda h, j, i: (h, j, 0)),
                  pl.BlockSpec((t, hb * VDIM), lambda h, j, i: (jnp.maximum(i, j), o_blk0 + h)),
                  pl.BlockSpec((t, hb * VDIM), lambda h, j, i: (jnp.maximum(i, j), h)),
                  pl.BlockSpec((hb, t, 1), qi)],
        out_specs=[pl.BlockSpec((hb, s, QW), lambda h, j, i: (h, 0, 0)),
                   pl.BlockSpec((hb, t, QW), lambda h, j, i: (h, j, 0)),
                   pl.BlockSpec((hb, t, VDIM), lambda h, j, i: (h, j, 0))],
        out_shape=[jax.ShapeDtypeStruct((MLA_H, s, QW), F32), jax.ShapeDtypeStruct((MLA_H, s, QW), F32),
                   jax.ShapeDtypeStruct((MLA_H, s, VDIM), F32)],
        scratch_shapes=[pltpu.VMEM((hb, t, QW), F32), pltpu.VMEM((hb, t, VDIM), F32)],
        compiler_params=_cparams(("arbitrary", "arbitrary", "arbitrary")))(q, k, v, dmix, o, lse)


def _adamw(name, w, g, m, v, parts):
    r, c = w.shape
    tr = r if r * c <= ADAM_ELEMS else _pick(r, max(8, ADAM_ELEMS // c // 8 * 8), 8)
    c1 = 1.0 - ADAM_B1 ** ADAM_STEP
    c2 = 1.0 - ADAM_B2 ** ADAM_STEP

    def body(w_ref, g_ref, m_ref, v_ref, go_ref, d_ref, mo_ref, vo_ref):
        if parts:
            gv = g_ref[0].astype(F32)
            for j in range(1, NDEV):
                gv = gv + g_ref[j].astype(F32)
        else:
            gv = g_ref[...]
        mv = ADAM_B1 * m_ref[...] + (1.0 - ADAM_B1) * gv
        vv = ADAM_B2 * v_ref[...] + (1.0 - ADAM_B2) * (gv * gv)
        go_ref[...] = gv
        mo_ref[...] = mv
        vo_ref[...] = vv
        d_ref[...] = -ADAM_LR * ((mv / c1) / (jnp.sqrt(vv / c2) + ADAM_EPS) + ADAM_WD * w_ref[...])

    blk = pl.BlockSpec((tr, c), lambda i: (i, 0))
    g_spec = pl.BlockSpec((NDEV, tr, c), lambda i: (0, i, 0)) if parts else blk
    out = jax.ShapeDtypeStruct((r, c), F32)
    return _pcall(body, name=name, grid=(r // tr,), in_specs=[blk, g_spec, blk, blk], out_specs=[blk] * 4,
                  out_shape=[out] * 4, compiler_params=_cparams(("parallel",)))(w, g, m, v)


def _t5_bucket(dist):
    n = jnp.maximum(dist, 0)
    max_exact = REL_BUCKETS // 2
    nf = jnp.maximum(n, 1).astype(F32)
    large = max_exact + (jnp.log(nf / max_exact) / math.log(REL_MAX_DIST / max_exact)
                         * (REL_BUCKETS - max_exact)).astype(jnp.int32)
    return jnp.where(n < max_exact, n, jnp.minimum(large, REL_BUCKETS - 1))


def _swap_halves(w, r0):
    return jnp.concatenate([w[:, r0 + ROPE // 2:r0 + ROPE], w[:, r0:r0 + ROPE // 2]], axis=1)


def _fold_swapped(g, r0, width):
    sw = g[..., width:width + ROPE]
    half = ROPE // 2
    return jnp.concatenate([g[..., :r0], g[..., r0:r0 + half] + sw[..., half:], g[..., r0 + half:r0 + ROPE] + sw[..., :half],
                            g[..., r0 + ROPE:width]], axis=-1)


def _heads_major(a, heads):
    s = a.shape[0]
    return a.reshape(s, heads, SWA_DH).transpose(1, 0, 2)


def _tokens_major(a):
    h, s, d = a.shape
    return a.transpose(1, 0, 2).reshape(s, h * d)


def kernel(x, c, w_mod, b_mod, attn_norm_g, w_in, swa_sinks, rel_bias, mla_q_norm_g, w_uq, mla_kv_norm_g, w_ukv, w_out, mlp_norm_g, w_ff1, w_ff2, final_norm_g, loss_target, m_w_mod, m_b_mod, m_attn_norm_g, m_w_in, m_swa_sinks, m_rel_bias, m_mla_q_norm_g, m_w_uq, m_mla_kv_norm_g, m_w_ukv, m_w_out, m_mlp_norm_g, m_w_ff1, m_w_ff2, m_final_norm_g, v_w_mod, v_b_mod, v_attn_norm_g, v_w_in, v_swa_sinks, v_rel_bias, v_mla_q_norm_g, v_w_uq, v_mla_kv_norm_g, v_w_ukv, v_w_out, v_mlp_norm_g, v_w_ff1, v_w_ff2, v_final_norm_g):
    s, d = x.shape[1], x.shape[2]
    ffs = w_ff1.shape[2]
    ff = ffs * NDEV
    nmod = w_mod.shape[2]
    me = 4 * lax.axis_index("x") + 2 * lax.axis_index("y") + lax.axis_index("c")
    x2d, tgt = x[0], loss_target[0]
    final_g = final_norm_g.reshape(1, d)

    w_in_l = jnp.concatenate([w_in[0], _swap_halves(w_in[0], OFF_KR)], axis=1).astype(BF16)
    w_uq_l = jnp.concatenate([w_uq[0], _swap_halves(w_uq[0], NOPE)], axis=1).astype(BF16)
    (c_all,) = _exchange("gather_c", [c], gather=True)

    b_cols = lax.dynamic_slice(b_mod, (0, me * nmod), (1, nmod))
    act_all, mod_cols, w_in_g, w_uq_g, w_ukv_g = _mod_fwd(
        c_all.reshape(NDEV, d), w_mod[0], b_cols, comm=([w_in_l, w_uq_l, w_ukv[0].astype(BF16)], True))
    w_in_e = w_in_g.reshape(d, IN_EXT)
    (mod_g,) = _exchange("gather_mod", [mod_cols], gather=True)
    mod = lax.dynamic_index_in_dim(mod_g, me, axis=1, keepdims=False).reshape(1, 6 * d)
    sh1, sc1, g1, sh2, sc2, g2 = [mod[:, i * d:(i + 1) * d] for i in range(6)]

    pos = jnp.arange(s, dtype=F32)
    inv_freq = ROPE_THETA ** (-jnp.arange(ROPE // 2, dtype=F32) / (ROPE // 2))
    ang = pos[:, None] * inv_freq[None, :]
    cos, sin = jnp.cos(ang), jnp.sin(ang)
    table = jnp.concatenate([cos, cos, -sin, sin], axis=1)
    q_loc = jnp.arange(BLOCK)[:, None]
    k_loc = jnp.arange(2 * BLOCK)[None, :]
    dist = q_loc + BLOCK - k_loc
    in_window = (dist >= 0) & (dist < BLOCK)
    onehot = (_t5_bucket(dist).reshape(-1, 1) == jnp.arange(REL_BUCKETS)[None, :]).astype(BF16)
    bias = _bias_expand(rel_bias.T, onehot.T).reshape(SWA_HEADS, BLOCK, 2 * BLOCK)
    bias = jnp.where(in_window[None], bias, -jnp.inf).reshape(SWA_KV, SWA_GROUP * BLOCK, 2 * BLOCK)
    sink_rows = jnp.broadcast_to(swa_sinks.reshape(SWA_HEADS, 1), (SWA_HEADS, BLOCK)).reshape(SWA_KV, SWA_GROUP * BLOCK, 1)

    h1 = _norm_mod("norm1", x2d, attn_norm_g, sc1, sh1)
    proj = _mm_plain("proj", h1, w_in_e, "nn", s, IN_EXT, d, F32, tn=IN_EXT)
    q_a = _heads_major(proj[:, :OFF_K].astype(BF16), SWA_HEADS).reshape(SWA_KV, SWA_GROUP, s, SWA_DH)
    k_a = _heads_major(proj[:, OFF_K:OFF_V].astype(BF16), SWA_KV)
    v_a = _heads_major(proj[:, OFF_V:OFF_CQ].astype(BF16), SWA_KV)
    o_a, lse_a, w_out_g = _swa_fwd(q_a, k_a, v_a, bias, sink_rows, comm=([w_out[0].astype(BF16)], True))
    w_out_f = w_out_g.reshape(MIX, d)

    cq, ckv, kr = _mla_prep(proj, mla_q_norm_g, mla_kv_norm_g, table)
    q_b, k_b, v_b = _mla_qkv(cq, ckv, kr, w_uq_g, w_ukv_g, table)
    o_b, lse_b, w_ff1_g = _mla_fwd(q_b, k_b, v_b, comm=([w_ff1[0].astype(BF16)], True))
    mix = jnp.concatenate([_tokens_major(o_a.reshape(SWA_HEADS, s, SWA_DH)), o_b.astype(BF16)], axis=1)

    tm, tn, tk = _pick(s, MM_TM, 128), _pick(d, MM_TN, 128), _pick(MIX, MM_TK, 128)
    row_blk = pl.BlockSpec((tm, tn), lambda i, j, q: (i, j))
    gate_blk = pl.BlockSpec((1, tn), lambda i, j, q: (0, j))

    def gated_residual(acc, ex, outs):
        outs[0][...] = acc
        outs[1][...] = ex[0][...] + ex[1][...] * acc

    y1, x2 = _mm("out_proj", mix, w_out_f, "nn", (s // tm, d // tn, MIX // tk),
                 pl.BlockSpec((tm, tk), lambda i, j, q: (i, q)), pl.BlockSpec((tk, tn), lambda i, j, q: (q, j)),
                 [jax.ShapeDtypeStruct((s, d), F32)] * 2, [row_blk, row_blk], (tm, tn), gated_residual,
                 extras=(x2d, g1), extra_specs=(row_blk, gate_blk))

    h2 = _norm_mod("norm2", x2, mlp_norm_g, sc2, sh2)
    tnf, tkd = _pick(ffs, MM_TN, 128), _pick(d, MM_TK, 128)
    rf = ffs // tnf
    ff_blk = pl.BlockSpec((tm, tnf), lambda i, j, q: (i, j))

    def relu_sq(acc, ex, outs):
        u = jnp.maximum(acc, 0.0)
        outs[0][...] = u
        outs[1][...] = (u * u).astype(BF16)

    u, uu, w_ff2_g = _mm("ff1", h2, w_ff1_g, "nn", (s // tm, ff // tnf, d // tkd),
                         pl.BlockSpec((tm, tkd), lambda i, j, q: (i, q)),
                         pl.BlockSpec((None, tkd, tnf), lambda i, j, q: (j // rf, q, j % rf)),
                         [jax.ShapeDtypeStruct((s, ff), F32), jax.ShapeDtypeStruct((s, ff), BF16)], [ff_blk, ff_blk],
                         (tm, tnf), relu_sq, comm=([w_ff2[0].astype(BF16)], True))
    w_ff2_f = w_ff2_g.reshape(ff, d)
    tkf = _pick(ff, MM_TK, 128)
    y2, x3 = _mm("ff2", uu, w_ff2_f, "nn", (s // tm, d // tn, ff // tkf),
                 pl.BlockSpec((tm, tkf), lambda i, j, q: (i, q)), pl.BlockSpec((tkf, tn), lambda i, j, q: (q, j)),
                 [jax.ShapeDtypeStruct((s, d), F32)] * 2, [row_blk, row_blk], (tm, tn), gated_residual,
                 extras=(x2, g2), extra_specs=(row_blk, gate_blk))

    dx3, dy2, loss_p, dgf, dg2 = _loss_head(x3, tgt, y2, final_g, g2)
    loss = lax.psum(loss_p[0, 0], ("x", "y", "c"))

    def relu_sq_bwd(acc, ex, outs):
        outs[0][...] = (acc * (2.0 * ex[0][...])).astype(BF16)

    tnf2 = _pick(ff, MM_TN, 128)
    du = _mm("ff2_dx", dy2, w_ff2_f, "nt", (s // tm, ff // tnf2, d // tkd),
             pl.BlockSpec((tm, tkd), lambda i, j, q: (i, q)), pl.BlockSpec((tnf2, tkd), lambda i, j, q: (j, q)),
             [jax.ShapeDtypeStruct((s, ff), BF16)], [pl.BlockSpec((tm, tnf2), lambda i, j, q: (i, j))],
             (tm, tnf2), relu_sq_bwd, extras=(u,), extra_specs=(pl.BlockSpec((tm, tnf2), lambda i, j, q: (i, j)),))[0]
    gw_ff2 = _mm_plain("ff2_dw", uu, dy2, "tn", ff, d, s, BF16)
    tmd, tks = _pick(d, MM_TM, 128), _pick(s, MM_TK, 128)
    gw_ff1, p_ff2 = _mm("ff1_dw", h2, du, "tn", (d // tmd, ff // tnf, s // tks),
                        pl.BlockSpec((tks, tmd), lambda i, j, q: (q, i)), pl.BlockSpec((tks, tnf), lambda i, j, q: (q, j)),
                        [jax.ShapeDtypeStruct((NDEV, d, ffs), BF16)],
                        [pl.BlockSpec((None, tmd, tnf), lambda i, j, q: (j // rf, i, j % rf))], (tmd, tnf), _store(BF16),
                        comm=([gw_ff2.reshape(NDEV, ffs, d)], False))
    tkf1 = _pick(ffs, MM_TK, 128)
    rk = ffs // tkf1
    dh2 = _mm("ff1_dx", du, w_ff1_g, "nt", (s // tm, d // tn, ff // tkf1),
              pl.BlockSpec((tm, tkf1), lambda i, j, q: (i, q)),
              pl.BlockSpec((None, tn, tkf1), lambda i, j, q: (q // rk, j, q % rk)),
              [jax.ShapeDtypeStruct((s, d), F32)], [row_blk], (tm, tn), _store(F32))[0]
    dx2, dy1, dsc2, dsh2, dgm, dg1 = _norm_mod_bwd("norm2_bwd", x2, dh2, dx3, mlp_norm_g, sc2, y1, g1)

    dmix = _mm_plain("out_proj_dx", dy1, w_out_f, "nt", s, MIX, d, F32)
    gw_out = _mm_plain("out_proj_dw", mix, dy1, "tn", MIX, d, s, BF16)

    dq_b, dk_b, dv_b, p_ff1 = _mla_bwd(q_b, k_b, v_b, dmix, o_b, lse_b, comm=([gw_ff1], False))
    dcq, dckv, dkr, gw_uq_e, gw_ukv = _mla_qkv_bwd(dq_b, dk_b, dv_b, cq, ckv, w_uq_g, w_ukv_g, table)
    dtail, dgq, dgkv = _mla_prep_bwd(proj, dcq, dckv, dkr, mla_q_norm_g, mla_kv_norm_g, table)

    do_a = _heads_major(dmix[:, :OFF_K].astype(BF16), SWA_HEADS).reshape(SWA_KV, SWA_GROUP, s, SWA_DH)
    dq_a, dkp, dkc, dvp, dvc, dbias, dsink, p_out = _swa_bwd(
        q_a, k_a, v_a, do_a, lse_a, bias, sink_rows, comm=([gw_out.reshape(NDEV, MIX // NDEV, d)], False))
    shift = lambda p: jnp.concatenate([p[:, BLOCK:], jnp.zeros_like(p[:, :BLOCK])], axis=1)
    dk_a, dv_a = dkc + shift(dkp), dvc + shift(dvp)
    drel_t, dsinks = _bias_reduce(dbias.reshape(SWA_HEADS, BLOCK * 2 * BLOCK), onehot, dsink.reshape(SWA_HEADS, BLOCK))
    dproj = jnp.concatenate([_tokens_major(dq_a.reshape(SWA_HEADS, s, SWA_DH)).astype(BF16),
                             _tokens_major(dk_a).astype(BF16), _tokens_major(dv_a).astype(BF16), dtail], axis=1)
    gw_in_e = _mm_plain("proj_dw", h1, dproj, "tn", d, IN_EXT, s, F32, tn=TAIL)
    gw_in = _fold_swapped(gw_in_e, OFF_KR, IN_COLS).reshape(NDEV, d // NDEV, IN_COLS).astype(BF16)
    gw_uq = _fold_swapped(gw_uq_e, NOPE, NOPE + ROPE).astype(BF16)
    tkt = IN_EXT
    dh1, p_in, p_uq, p_ukv = _mm(
        "proj_dx", dproj, w_in_e, "nt", (s // tm, d // tn, IN_EXT // tkt),
        pl.BlockSpec((tm, tkt), lambda i, j, q: (i, q)), pl.BlockSpec((tn, tkt), lambda i, j, q: (j, q)),
        [jax.ShapeDtypeStruct((s, d), F32)], [row_blk], (tm, tn), _store(F32),
        comm=([gw_in, gw_uq, gw_ukv.astype(BF16)], False))
    gx, dsc1, dsh1, dga = _norm_mod_bwd("norm1_bwd", x2d, dh1, dx2, attn_norm_g, sc1)

    small = [jnp.concatenate([dsh1, dsc1, dg1, dsh2, dsc2, dg2], axis=1), dga, dgm, dgf, dgq, dgkv,
             dsinks.reshape(1, SWA_HEADS), drel_t.T.reshape(1, REL_BUCKETS * SWA_HEADS)]
    n_small = sum(a.shape[1] for a in small)
    n_pad = -n_small % 1024
    rows_small = (n_small + n_pad) // 128
    pad = jnp.zeros((1, n_pad), F32)
    pack = lambda parts: jnp.concatenate([p.reshape(1, -1) for p in parts] + [pad], axis=1).reshape(rows_small, 128)
    (small_g,) = _exchange("gather_small", [pack(small)], gather=True)
    small_names = (b_mod, attn_norm_g, mlp_norm_g, final_norm_g, mla_q_norm_g, mla_kv_norm_g, swa_sinks, rel_bias)
    small_m = (m_b_mod, m_attn_norm_g, m_mlp_norm_g, m_final_norm_g, m_mla_q_norm_g, m_mla_kv_norm_g, m_swa_sinks, m_rel_bias)
    small_v = (v_b_mod, v_attn_norm_g, v_mlp_norm_g, v_final_norm_g, v_mla_q_norm_g, v_mla_kv_norm_g, v_swa_sinks, v_rel_bias)
    small_out = _adamw("adamw_small", pack(small_names), small_g, pack(small_m), pack(small_v), parts=True)

    def unpack(flat):
        flat = flat.reshape(1, -1)
        out, off = [], 0
        for a in small_names:
            out.append(flat[:, off:off + a.size].reshape(a.shape))
            off += a.size
        return out

    sg, sd, sm, sv = [unpack(o) for o in small_out]

    dmod_cols = lax.dynamic_slice(small_g.reshape(NDEV, -1), (0, me * nmod), (NDEV, nmod))
    gw_mod = _mod_wgrad(act_all, dmod_cols)
    big = {"w_mod": _adamw("adamw_w_mod", w_mod[0], gw_mod, m_w_mod[0], v_w_mod[0], parts=False)}

    for name, w, p, m, v in (("w_in", w_in, p_in, m_w_in, v_w_in), ("w_uq", w_uq, p_uq, m_w_uq, v_w_uq),
                             ("w_ukv", w_ukv, p_ukv, m_w_ukv, v_w_ukv), ("w_out", w_out, p_out, m_w_out, v_w_out),
                             ("w_ff1", w_ff1, p_ff1, m_w_ff1, v_w_ff1), ("w_ff2", w_ff2, p_ff2, m_w_ff2, v_w_ff2)):
        big[name] = _adamw("adamw_" + name, w[0], p, m[0], v[0], parts=True)

    order = ("w_mod", "b_mod", "attn_norm_g", "w_in", "swa_sinks", "rel_bias", "mla_q_norm_g", "w_uq", "mla_kv_norm_g",
             "w_ukv", "w_out", "mlp_norm_g", "w_ff1", "w_ff2", "final_norm_g")
    small_idx = {"b_mod": 0, "attn_norm_g": 1, "mlp_norm_g": 2, "final_norm_g": 3, "mla_q_norm_g": 4,
                 "mla_kv_norm_g": 5, "swa_sinks": 6, "rel_bias": 7}
    outs = []
    for kind, small_list in enumerate((sg, sd, sm, sv)):
        for name in order:
            outs.append(small_list[small_idx[name]] if name in small_idx else big[name][kind][None])
    return (loss, gx[None], *outs)
```

```python
import functools
import math

import jax
import jax.numpy as jnp
from jax import lax
from jax.experimental import pallas as pl
from jax.experimental.pallas import tpu as pltpu

F32 = jnp.float32
BF16 = jnp.bfloat16

NDEV = 8
EPS = 1e-6
BLOCK = 128
SWA_HEADS, SWA_KV, SWA_DH, SWA_GROUP = 16, 2, 64, 8
REL_BUCKETS, REL_MAX_DIST = 32, 128
MLA_H, Q_RANK, KV_RANK, NOPE, ROPE, VDIM = 8, 384, 128, 128, 64, 128
ROPE_THETA = 10000.0
OFF_K, OFF_V, OFF_CQ, OFF_CKV, OFF_KR, IN_COLS = 1024, 1152, 1280, 1664, 1792, 1856
IN_EXT = IN_COLS + ROPE
TAIL0, TAIL = OFF_CQ, IN_EXT - OFF_CQ
QW = NOPE + 2 * ROPE
MIX = SWA_HEADS * SWA_DH + MLA_H * VDIM
MLA_SCALE = (NOPE + ROPE) ** -0.5
SWA_SCALE = SWA_DH ** -0.5

ADAM_LR, ADAM_B1, ADAM_B2, ADAM_EPS, ADAM_WD, ADAM_STEP = 0.001, 0.9, 0.999, 1e-08, 0.01, 10

VMEM_LIMIT = 52 * 1024 * 1024
ROW_TILE = 256
MM_TM, MM_TN, MM_TK = 1024, 1024, 2048
ATT_T = 512
MLA_HB = 2
ADAM_ELEMS = 128 * 1024


MESH_ID = pl.DeviceIdType.MESH


def _place():
    x, y, c = lax.axis_index("x"), lax.axis_index("y"), lax.axis_index("c")
    return x, y, c, 2 * x + y


def _chip(x, y, k):
    return (1 - x if k & 2 else x, 1 - y if k & 1 else y)


def _dma_sems(*counts):
    return [pltpu.SemaphoreType.DMA((n,)) for n in counts]


class _Gather:
    def __init__(self, arrays):
        self.arrays = list(arrays)
        n = len(self.arrays)
        self.out_shape = [jax.ShapeDtypeStruct((NDEV,) + a.shape, a.dtype) for a in self.arrays]
        self.sems = _dma_sems(7 * n, 7 * n, n)

    def _copy(self, sems, a, k, src, dst, to):
        return pltpu.make_async_remote_copy(src_ref=src, dst_ref=dst, send_sem=sems[0].at[7 * a + k],
                                            recv_sem=sems[1].at[7 * a + k], device_id=to, device_id_type=MESH_ID)

    def start(self, ins, outs, sems):
        x, y, c, q = _place()
        me = 2 * q + c
        for a in range(len(ins)):
            pltpu.make_async_copy(ins[a], outs[a].at[me], sems[2].at[a]).start()
            self._copy(sems, a, 0, ins[a], outs[a].at[me], (x, y, 1 - c)).start()
            for k in (1, 2, 3):
                self._copy(sems, a, k, ins[a], outs[a].at[me], (*_chip(x, y, k), c)).start()

    def finish(self, ins, outs, sems):
        x, y, c, q = _place()
        me, sib = 2 * q + c, (x, y, 1 - c)
        n = len(ins)
        for k in (1, 2, 3):
            for a in range(n):
                blk = outs[a].at[2 * (q ^ k) + c]
                self._copy(sems, a, k, ins[a], blk, (*_chip(x, y, k), c)).wait_recv()
                self._copy(sems, a, 3 + k, blk, blk, sib).start()
        for a in range(n):
            self._copy(sems, a, 0, ins[a], outs[a].at[2 * q + 1 - c], sib).wait_recv()
            for k in (1, 2, 3):
                blk = outs[a].at[2 * (q ^ k) + 1 - c]
                self._copy(sems, a, 3 + k, blk, blk, sib).wait_recv()
        for a in range(n):
            for k in range(7):
                self._copy(sems, a, k, ins[a], outs[a].at[me], sib).wait_send()
            pltpu.make_async_copy(ins[a], outs[a].at[me], sems[2].at[a]).wait()


class _PairSwap:
    def __init__(self, arrays):
        self.arrays = list(arrays)
        n = len(self.arrays)
        self.out_shape = [jax.ShapeDtypeStruct((NDEV // 2,) + a.shape[1:], a.dtype) for a in self.arrays]
        self.sems = _dma_sems(4 * n, 4 * n)

    def _copy(self, sems, a, p, src, dst, to):
        return pltpu.make_async_remote_copy(src_ref=src, dst_ref=dst, send_sem=sems[0].at[4 * a + p],
                                            recv_sem=sems[1].at[4 * a + p], device_id=to, device_id_type=MESH_ID)

    def start(self, ins, outs, sems):
        x, y, c, _ = _place()
        for a in range(len(ins)):
            for p in range(4):
                self._copy(sems, a, p, ins[a].at[2 * p + 1 - c], outs[a].at[p], (x, y, 1 - c)).start()

    def finish(self, ins, outs, sems):
        x, y, c, _ = _place()
        for a in range(len(ins)):
            for p in range(4):
                cp = self._copy(sems, a, p, ins[a].at[2 * p + 1 - c], outs[a].at[p], (x, y, 1 - c))
                cp.wait_recv()
                cp.wait_send()


class _ChipScatter:
    def __init__(self, arrays):
        self.arrays = list(arrays)
        n = len(self.arrays)
        self.out_shape = [jax.ShapeDtypeStruct(a.shape, a.dtype) for a in self.arrays]
        self.sems = _dma_sems(3 * n, 3 * n, n)

    def _copy(self, sems, a, k, src, dst, to):
        return pltpu.make_async_remote_copy(src_ref=src, dst_ref=dst, send_sem=sems[0].at[3 * a + k - 1],
                                            recv_sem=sems[1].at[3 * a + k - 1], device_id=to, device_id_type=MESH_ID)

    def start(self, ins, outs, sems):
        x, y, c, q = _place()
        for a in range(len(ins)):
            pltpu.make_async_copy(ins[a].at[q], outs[a].at[q], sems[2].at[a]).start()
            for k in (1, 2, 3):
                self._copy(sems, a, k, ins[a].at[q ^ k], outs[a].at[q], (*_chip(x, y, k), c)).start()

    def finish(self, ins, outs, sems):
        x, y, c, q = _place()
        for a in range(len(ins)):
            for k in (1, 2, 3):
                cp = self._copy(sems, a, k, ins[a].at[q ^ k], outs[a].at[q ^ k], (*_chip(x, y, k), c))
                cp.wait_recv()
                cp.wait_send()
            pltpu.make_async_copy(ins[a].at[q], outs[a].at[q], sems[2].at[a]).wait()


def _call(body, **kw):
    return pl.pallas_call(body, **kw)


def _pcall(body, comm=None, **kw):
    if not comm:
        return _call(body, **kw)
    grid = kw["grid"]
    in_specs, out_specs, out_shape = list(kw["in_specs"]), list(kw["out_specs"]), list(kw["out_shape"])
    scratch = list(kw.get("scratch_shapes", ()))
    n_in, n_out, n_scr = len(in_specs), len(out_shape), len(scratch)
    n_cin = [len(j.arrays) for j in comm]
    n_sem = [len(j.sems) for j in comm]
    n = sum(n_cin)
    hbm = pl.BlockSpec(memory_space=pltpu.HBM)

    def carried(*refs):
        ins, cins = refs[:n_in], refs[n_in:n_in + n]
        outs, couts = refs[n_in + n:n_in + n + n_out], refs[n_in + n + n_out:n_in + 2 * n + n_out]
        scr, sems = refs[n_in + 2 * n + n_out:n_in + 2 * n + n_out + n_scr], refs[n_in + 2 * n + n_out + n_scr:]
        ids = [pl.program_id(ax) for ax in range(len(grid))]
        first = functools.reduce(jnp.logical_and, [i == 0 for i in ids])
        last = functools.reduce(jnp.logical_and, [i == g - 1 for i, g in zip(ids, grid)])

        def each(method):
            ai = si = 0
            for job, na, ns in zip(comm, n_cin, n_sem):
                getattr(job, method)(cins[ai:ai + na], couts[ai:ai + na], sems[si:si + ns])
                ai, si = ai + na, si + ns

        @pl.when(first)
        def _():
            each("start")

        body(*ins, *outs, *scr)

        @pl.when(last)
        def _():
            each("finish")

    kw.update(in_specs=in_specs + [hbm] * n, out_specs=out_specs + [hbm] * n,
              out_shape=out_shape + [o for j in comm for o in j.out_shape],
              scratch_shapes=scratch + [sm for j in comm for sm in j.sems],
              compiler_params=_cparams(("arbitrary",) * len(grid)))
    call = _call(carried, **kw)
    return lambda *args: call(*args, *[a for j in comm for a in j.arrays])


def _cparams(sem):
    return pltpu.CompilerParams(dimension_semantics=sem, vmem_limit_bytes=VMEM_LIMIT)


def _pick(n, pref, align):
    if n <= pref:
        return n
    t = (pref // align) * align
    while t >= align:
        if n % t == 0:
            return t
        t -= align
    return n


def _split3(x):
    a = x.astype(BF16)
    r = x - a.astype(F32)
    b = r.astype(BF16)
    c = (r - b.astype(F32)).astype(BF16)
    return a, b, c


def _exchange(name, job):
    n = len(job.arrays)

    def body(*refs):
        ins, outs, sems = refs[:n], refs[n:2 * n], refs[2 * n:]
        job.start(ins, outs, sems)
        job.finish(ins, outs, sems)

    hbm = pl.BlockSpec(memory_space=pltpu.HBM)
    return _call(body, name=name, out_shape=job.out_shape, in_specs=[hbm] * n, out_specs=[hbm] * n,
                 scratch_shapes=job.sems)(*job.arrays)


def _pair_sum(name, g, r, core):
    _, rr, cc = g.shape
    tr = rr if rr * cc <= 4 * ADAM_ELEMS else _pick(rr, max(16, 4 * ADAM_ELEMS // cc // 16 * 16), 16)

    def body(g_ref, r_ref, c_ref, o_ref):
        north = c_ref[:, 0:1] > 0.5
        mine = jnp.where(north, g_ref[1].astype(F32), g_ref[0].astype(F32))
        o_ref[...] = (mine + r_ref[...].astype(F32)).astype(o_ref.dtype)

    return _pcall(
        body, name=name, grid=(NDEV // 2, rr // tr),
        in_specs=[pl.BlockSpec((None, 2, tr, cc), lambda p, i: (p, 0, i, 0)),
                  pl.BlockSpec((None, tr, cc), lambda p, i: (p, i, 0)), pl.BlockSpec((1, 128), lambda p, i: (0, 0))],
        out_specs=pl.BlockSpec((None, tr, cc), lambda p, i: (p, i, 0)),
        out_shape=jax.ShapeDtypeStruct((NDEV // 2, rr, cc), g.dtype),
        compiler_params=_cparams(("parallel", "parallel")))(g.reshape(NDEV // 2, 2, rr, cc), r, core)


_DIMS = {"nn": (((1,), (0,)), ((), ())), "nt": (((1,), (1,)), ((), ())), "tn": (((0,), (0,)), ((), ()))}


def _mm(name, a, b, kind, grid, a_spec, b_spec, out_shape, out_specs, acc_shape, epilogue,
        extras=(), extra_specs=(), comm=None):
    nk, ne, no = grid[2], len(extras), len(out_shape)

    def body(*refs):
        a_ref, b_ref = refs[0], refs[1]
        ex, outs = refs[2:2 + ne], refs[2 + ne:2 + ne + no]
        part = lax.dot_general(a_ref[...].astype(BF16), b_ref[...].astype(BF16), _DIMS[kind],
                               preferred_element_type=F32)
        if nk == 1:
            epilogue(part, ex, outs)
            return
        acc = refs[-1]
        k = pl.program_id(2)

        @pl.when(k == 0)
        def _():
            acc[...] = part

        @pl.when(jnp.logical_and(k > 0, k < nk - 1))
        def _():
            acc[...] += part

        @pl.when(k == nk - 1)
        def _():
            epilogue(acc[...] + part, ex, outs)

    return _pcall(
        body, comm=comm, name=name, grid=grid, in_specs=[a_spec, b_spec, *extra_specs], out_specs=out_specs,
        out_shape=out_shape, scratch_shapes=[pltpu.VMEM(acc_shape, F32)] if nk > 1 else [],
        compiler_params=_cparams(("parallel", "parallel", "arbitrary")),
    )(a, b, *extras)


def _store(dtype):
    def epi(acc, ex, outs):
        outs[0][...] = acc.astype(dtype)
    return epi


def _mm_plain(name, a, b, kind, m, n, k, out_dtype, tm=None, tn=None, tk=None):
    tm = _pick(m, tm or MM_TM, 128)
    tn = _pick(n, tn or MM_TN, 128)
    tk = _pick(k, tk or MM_TK, 128)
    a_spec = pl.BlockSpec((tk, tm), lambda i, j, q: (q, i)) if kind == "tn" else pl.BlockSpec((tm, tk), lambda i, j, q: (i, q))
    b_spec = pl.BlockSpec((tn, tk), lambda i, j, q: (j, q)) if kind == "nt" else pl.BlockSpec((tk, tn), lambda i, j, q: (q, j))
    return _mm(name, a, b, kind, (m // tm, n // tn, k // tk), a_spec, b_spec,
               [jax.ShapeDtypeStruct((m, n), out_dtype)], [pl.BlockSpec((tm, tn), lambda i, j, q: (i, j))],
               (tm, tn), _store(out_dtype))[0]


def _row(ts, d):
    return pl.BlockSpec((ts, d), lambda i: (i, 0))


def _vec(d):
    return pl.BlockSpec((1, d), lambda i: (0, 0))


def _norm_mod(name, x, gain, sc, sh):
    s, d = x.shape
    ts = _pick(s, ROW_TILE, 16)

    def body(x_ref, g_ref, sc_ref, sh_ref, h_ref):
        xv = x_ref[...]
        r = lax.rsqrt(jnp.mean(xv * xv, axis=-1, keepdims=True) + EPS)
        h_ref[...] = ((xv * r) * g_ref[...] * (1.0 + sc_ref[...]) + sh_ref[...]).astype(BF16)

    return _pcall(body, name=name, grid=(s // ts,), in_specs=[_row(ts, d), _vec(d), _vec(d), _vec(d)],
                  out_specs=_row(ts, d), out_shape=jax.ShapeDtypeStruct((s, d), BF16),
                  compiler_params=_cparams(("parallel",)))(x, gain, sc, sh)


def _loss_head(x3, tgt, y2, gf, g2):
    s, d = x3.shape
    ts = _pick(s, ROW_TILE, 16)

    def body(x_ref, t_ref, y_ref, gf_ref, g2_ref, dx_ref, dy_ref, loss_ref, dgf_ref, dg2_ref):
        @pl.when(pl.program_id(0) == 0)
        def _():
            loss_ref[...] = jnp.zeros_like(loss_ref)
            dgf_ref[...] = jnp.zeros_like(dgf_ref)
            dg2_ref[...] = jnp.zeros_like(dg2_ref)

        xv = x_ref[...]
        r = lax.rsqrt(jnp.mean(xv * xv, axis=-1, keepdims=True) + EPS)
        xn = xv * r
        err = xn * gf_ref[...] - t_ref[...]
        loss_ref[...] += 0.5 * jnp.sum(jnp.mean(err * err, axis=-1, keepdims=True), axis=0, keepdims=True)
        dout = err * (1.0 / d)
        dgf_ref[...] += jnp.sum(dout * xn, axis=0, keepdims=True)
        dxn = dout * gf_ref[...]
        dx = r * (dxn - xn * jnp.mean(dxn * xn, axis=-1, keepdims=True))
        dx_ref[...] = dx
        dy_ref[...] = (dx * g2_ref[...]).astype(BF16)
        dg2_ref[...] += jnp.sum(dx * y_ref[...], axis=0, keepdims=True)

    one = pl.BlockSpec((1, 1), lambda i: (0, 0))
    return _pcall(
        body, name="loss_head", grid=(s // ts,),
        in_specs=[_row(ts, d), _row(ts, d), _row(ts, d), _vec(d), _vec(d)],
        out_specs=[_row(ts, d), _row(ts, d), one, _vec(d), _vec(d)],
        out_shape=[jax.ShapeDtypeStruct((s, d), F32), jax.ShapeDtypeStruct((s, d), BF16),
                   jax.ShapeDtypeStruct((1, 1), F32), jax.ShapeDtypeStruct((1, d), F32),
                   jax.ShapeDtypeStruct((1, d), F32)],
        compiler_params=_cparams(("arbitrary",)))(x3, tgt, y2, gf, g2)


def _norm_mod_bwd(name, x, dh, dres, gain, sc, y_prev=None, gate=None, comm=None):
    s, d = x.shape
    ts = _pick(s, ROW_TILE, 16)
    gated = y_prev is not None

    def body(*refs):
        if gated:
            x_ref, dh_ref, dr_ref, g_ref, sc_ref, y_ref, gt_ref, dx_ref, dy_ref, dsc_ref, dsh_ref, dg_ref, dgt_ref = refs
        else:
            x_ref, dh_ref, dr_ref, g_ref, sc_ref, dx_ref, dsc_ref, dsh_ref, dg_ref = refs

        @pl.when(pl.program_id(0) == 0)
        def _():
            dsc_ref[...] = jnp.zeros_like(dsc_ref)
            dsh_ref[...] = jnp.zeros_like(dsh_ref)
            dg_ref[...] = jnp.zeros_like(dg_ref)
            if gated:
                dgt_ref[...] = jnp.zeros_like(dgt_ref)

        xv, dhv = x_ref[...], dh_ref[...]
        r = lax.rsqrt(jnp.mean(xv * xv, axis=-1, keepdims=True) + EPS)
        xn = xv * r
        dsc_ref[...] += jnp.sum(dhv * (xn * g_ref[...]), axis=0, keepdims=True)
        dsh_ref[...] += jnp.sum(dhv, axis=0, keepdims=True)
        da = dhv * (1.0 + sc_ref[...])
        dg_ref[...] += jnp.sum(da * xn, axis=0, keepdims=True)
        dxn = da * g_ref[...]
        dx = dr_ref[...] + r * (dxn - xn * jnp.mean(dxn * xn, axis=-1, keepdims=True))
        dx_ref[...] = dx
        if gated:
            dy_ref[...] = (dx * gt_ref[...]).astype(BF16)
            dgt_ref[...] += jnp.sum(dx * y_ref[...], axis=0, keepdims=True)

    ins = [x, dh, dres, gain, sc] + ([y_prev, gate] if gated else [])
    in_specs = [_row(ts, d)] * 3 + [_vec(d)] * 2 + ([_row(ts, d), _vec(d)] if gated else [])
    vec_out = jax.ShapeDtypeStruct((1, d), F32)
    out_shape = [jax.ShapeDtypeStruct((s, d), F32)] + ([jax.ShapeDtypeStruct((s, d), BF16)] if gated else [])
    out_shape += [vec_out] * (4 if gated else 3)
    out_specs = [_row(ts, d)] * (2 if gated else 1) + [_vec(d)] * (4 if gated else 3)
    return _pcall(body, comm=comm, name=name, grid=(s // ts,), in_specs=in_specs, out_specs=out_specs,
                  out_shape=out_shape, compiler_params=_cparams(("arbitrary",)))(*ins)


def _dot3(a, b, dims):
    a1, a2, _ = _split3(a)
    b1, b2, _ = _split3(b)
    dot = functools.partial(lax.dot_general, dimension_numbers=dims, preferred_element_type=F32)
    return dot(a1, b1) + (dot(a1, b2) + dot(a2, b1))


def _mod_fwd(c_all, w, b_cols, comm=None):
    nb, d = c_all.shape
    n = w.shape[1]
    tk = _pick(d, 512, 128)
    nk = d // tk

    def body(c_ref, w_ref, b_ref, act_ref, out_ref):
        k = pl.program_id(0)
        cv = c_ref[...]
        act = cv * (1.0 / (1.0 + jnp.exp(-cv)))
        act_ref[...] = act

        @pl.when(k == 0)
        def _():
            out_ref[...] = jnp.broadcast_to(b_ref[...], out_ref.shape)

        out_ref[...] += _dot3(act, w_ref[...], _DIMS["nn"])

    return _pcall(
        body, comm=comm, name="mod_fwd", grid=(nk,),
        in_specs=[pl.BlockSpec((nb, tk), lambda k: (0, k)), pl.BlockSpec((tk, n), lambda k: (k, 0)),
                  pl.BlockSpec((1, n), lambda k: (0, 0))],
        out_specs=[pl.BlockSpec((nb, tk), lambda k: (0, k)), pl.BlockSpec((nb, n), lambda k: (0, 0))],
        out_shape=[jax.ShapeDtypeStruct((nb, d), F32), jax.ShapeDtypeStruct((nb, n), F32)],
        compiler_params=_cparams(("arbitrary",)))(c_all, w, b_cols)


def _mod_wgrad(act_all, dmod_cols):
    nb, d = act_all.shape
    n = dmod_cols.shape[1]
    tm = _pick(d, 512, 128)

    def body(a_ref, d_ref, o_ref):
        o_ref[...] = _dot3(a_ref[...], d_ref[...], _DIMS["tn"])

    return _pcall(
        body, name="mod_wgrad", grid=(d // tm,),
        in_specs=[pl.BlockSpec((nb, tm), lambda i: (0, i)), pl.BlockSpec((nb, n), lambda i: (0, 0))],
        out_specs=pl.BlockSpec((tm, n), lambda i: (i, 0)), out_shape=jax.ShapeDtypeStruct((d, n), F32),
        compiler_params=_cparams(("parallel",)))(act_all, dmod_cols)


def _bias_expand(rel_t, onehot_t):
    h, _ = rel_t.shape
    n = onehot_t.shape[1]

    def body(r_ref, o_ref, out_ref):
        a, b, c = _split3(r_ref[...])
        dot = functools.partial(lax.dot_general, dimension_numbers=_DIMS["nn"], preferred_element_type=F32)
        oh = o_ref[...]
        out_ref[...] = dot(a, oh) + (dot(b, oh) + dot(c, oh))

    full = lambda shp: pl.BlockSpec(shp, lambda: (0,) * len(shp))
    return _pcall(body, name="bias_expand", in_specs=[full(rel_t.shape), full(onehot_t.shape)],
                  out_specs=full((h, n)), out_shape=jax.ShapeDtypeStruct((h, n), F32),
                  compiler_params=pltpu.CompilerParams(vmem_limit_bytes=VMEM_LIMIT))(rel_t, onehot_t)


def _bias_reduce(dbias, onehot, dsink_rows):
    h, n = dbias.shape

    def body(d_ref, o_ref, s_ref, out_ref, so_ref):
        a, b, c = _split3(d_ref[...])
        dot = functools.partial(lax.dot_general, dimension_numbers=_DIMS["nn"], preferred_element_type=F32)
        oh = o_ref[...]
        out_ref[...] = dot(a, oh) + (dot(b, oh) + dot(c, oh))
        so_ref[...] = jnp.sum(s_ref[...], axis=-1, keepdims=True)

    full = lambda shp: pl.BlockSpec(shp, lambda: (0,) * len(shp))
    return _pcall(body, name="bias_reduce", in_specs=[full(dbias.shape), full(onehot.shape), full(dsink_rows.shape)],
                  out_specs=[full((h, REL_BUCKETS)), full((h, 1))],
                  out_shape=[jax.ShapeDtypeStruct((h, REL_BUCKETS), F32), jax.ShapeDtypeStruct((h, 1), F32)],
                  compiler_params=pltpu.CompilerParams(vmem_limit_bytes=VMEM_LIMIT))(dbias, onehot, dsink_rows)


def _swa_specs(s):
    rows = SWA_GROUP * BLOCK
    q_spec = pl.BlockSpec((None, SWA_GROUP, BLOCK, SWA_DH), lambda g, n: (g, 0, n, 0))
    kv_prev = pl.BlockSpec((None, BLOCK, SWA_DH), lambda g, n: (g, jnp.maximum(n - 1, 0), 0))
    kv_cur = pl.BlockSpec((None, BLOCK, SWA_DH), lambda g, n: (g, n, 0))
    bias_spec = pl.BlockSpec((None, rows, 2 * BLOCK), lambda g, n: (g, 0, 0))
    col_spec = pl.BlockSpec((None, rows, 1), lambda g, n: (g, 0, 0))
    lse_spec = pl.BlockSpec((None, None, rows, 1), lambda g, n: (g, n, 0, 0))
    return rows, q_spec, kv_prev, kv_cur, bias_spec, col_spec, lse_spec


def _swa_scores(q_ref, kp_ref, kc_ref, bias_ref, n):
    rows = SWA_GROUP * BLOCK
    q = q_ref[...].reshape(rows, SWA_DH)
    kb = jnp.concatenate([kp_ref[...], kc_ref[...]], axis=0)
    s = lax.dot_general(q, kb, _DIMS["nt"], preferred_element_type=F32) * SWA_SCALE + bias_ref[...]
    col = lax.broadcasted_iota(jnp.int32, s.shape, 1)
    s = jnp.where(jnp.logical_and(n == 0, col < BLOCK), -jnp.inf, s)
    return q, kb, s


def _swa_fwd(q, k, v, bias, sink_rows, comm=None):
    s = q.shape[2]
    nb = s // BLOCK
    rows, q_spec, kv_prev, kv_cur, bias_spec, col_spec, lse_spec = _swa_specs(s)

    def body(q_ref, kp_ref, kc_ref, vp_ref, vc_ref, bias_ref, sink_ref, o_ref, lse_ref):
        n = pl.program_id(1)
        _, _, sc = _swa_scores(q_ref, kp_ref, kc_ref, bias_ref, n)
        sink = sink_ref[...]
        m = jnp.maximum(jnp.max(sc, axis=-1, keepdims=True), sink)
        p = jnp.exp(sc - m)
        den = jnp.sum(p, axis=-1, keepdims=True) + jnp.exp(sink - m)
        p = p / den
        vb = jnp.concatenate([vp_ref[...], vc_ref[...]], axis=0)
        o = lax.dot_general(p.astype(BF16), vb, _DIMS["nn"], preferred_element_type=F32)
        o_ref[...] = o.reshape(SWA_GROUP, BLOCK, SWA_DH).astype(BF16)
        lse_ref[...] = m + jnp.log(den)

    return _pcall(
        body, comm=comm, name="swa_fwd", grid=(SWA_KV, nb),
        in_specs=[q_spec, kv_prev, kv_cur, kv_prev, kv_cur, bias_spec, col_spec],
        out_specs=[q_spec, lse_spec],
        out_shape=[jax.ShapeDtypeStruct(q.shape, BF16), jax.ShapeDtypeStruct((SWA_KV, nb, rows, 1), F32)],
        compiler_params=_cparams(("parallel", "parallel")))(q, k, k, v, v, bias, sink_rows)


def _swa_bwd(q, k, v, do, lse, bias, sink_rows, comm=None):
    s = q.shape[2]
    nb = s // BLOCK
    rows, q_spec, kv_prev, kv_cur, bias_spec, col_spec, lse_spec = _swa_specs(s)

    def body(q_ref, kp_ref, kc_ref, vp_ref, vc_ref, do_ref, lse_ref, bias_ref, sink_ref,
             dq_ref, dkp_ref, dkc_ref, dvp_ref, dvc_ref, dbias_ref, dsink_ref):
        n = pl.program_id(1)

        @pl.when(n == 0)
        def _():
            dbias_ref[...] = jnp.zeros_like(dbias_ref)
            dsink_ref[...] = jnp.zeros_like(dsink_ref)

        qv, kb, sc = _swa_scores(q_ref, kp_ref, kc_ref, bias_ref, n)
        lse_v = lse_ref[...]
        p = jnp.exp(sc - lse_v)
        p_sink = jnp.exp(sink_ref[...] - lse_v)
        dov = do_ref[...].reshape(rows, SWA_DH)
        vb = jnp.concatenate([vp_ref[...], vc_ref[...]], axis=0)
        dp = lax.dot_general(dov, vb, _DIMS["nt"], preferred_element_type=F32)
        delta = jnp.sum(p * dp, axis=-1, keepdims=True)
        ds = p * (dp - delta)
        dbias_ref[...] += ds
        dsink_ref[...] += -p_sink * delta
        dsb = (ds * SWA_SCALE).astype(BF16)
        dq_ref[...] = lax.dot_general(dsb, kb, _DIMS["nn"], preferred_element_type=F32).reshape(SWA_GROUP, BLOCK, SWA_DH)
        dk = lax.dot_general(dsb, qv, _DIMS["tn"], preferred_element_type=F32)
        dv = lax.dot_general(p.astype(BF16), dov, _DIMS["tn"], preferred_element_type=F32)
        dkp_ref[...] = dk[:BLOCK]
        dkc_ref[...] = dk[BLOCK:]
        dvp_ref[...] = dv[:BLOCK]
        dvc_ref[...] = dv[BLOCK:]

    kv_out = jax.ShapeDtypeStruct((SWA_KV, s, SWA_DH), F32)
    return _pcall(
        body, comm=comm, name="swa_bwd", grid=(SWA_KV, nb),
        in_specs=[q_spec, kv_prev, kv_cur, kv_prev, kv_cur, q_spec, lse_spec, bias_spec, col_spec],
        out_specs=[q_spec, kv_cur, kv_cur, kv_cur, kv_cur, bias_spec, col_spec],
        out_shape=[jax.ShapeDtypeStruct(q.shape, F32), kv_out, kv_out, kv_out, kv_out,
                   jax.ShapeDtypeStruct(bias.shape, F32), jax.ShapeDtypeStruct(sink_rows.shape, F32)],
        compiler_params=_cparams(("arbitrary", "arbitrary")))(q, k, k, v, v, do, lse, bias, sink_rows)


def _rope_slab(slab, table):
    t = slab * table
    return t + pltpu.roll(t, ROPE, 1)


def _low_lanes(v):
    lane = lax.broadcasted_iota(jnp.int32, v.shape, 1)
    return jnp.where(lane < ROPE, v, 0.0)


def _rms(xv, g):
    r = lax.rsqrt(jnp.mean(xv * xv, axis=-1, keepdims=True) + EPS)
    return xv * r, r


def _mla_prep(proj, gq, gkv, table):
    s = proj.shape[0]
    ts = _pick(s, ROW_TILE, 16)

    def body(p_ref, gq_ref, gkv_ref, t_ref, cq_ref, ckv_ref, kr_ref):
        xq, _ = _rms(p_ref[:, 0:Q_RANK], None)
        cq_ref[...] = (xq * gq_ref[...]).astype(BF16)
        xkv, _ = _rms(p_ref[:, Q_RANK:Q_RANK + KV_RANK], None)
        ckv_ref[...] = (xkv * gkv_ref[...]).astype(BF16)
        kr_ref[...] = _low_lanes(_rope_slab(p_ref[:, Q_RANK + KV_RANK:TAIL], t_ref[...]))

    return _pcall(
        body, name="mla_prep", grid=(s // ts,),
        in_specs=[pl.BlockSpec((ts, TAIL), lambda i: (i, TAIL0 // TAIL)), _vec(Q_RANK), _vec(KV_RANK), _row(ts, 2 * ROPE)],
        out_specs=[_row(ts, Q_RANK), _row(ts, KV_RANK), _row(ts, 2 * ROPE)],
        out_shape=[jax.ShapeDtypeStruct((s, Q_RANK), BF16), jax.ShapeDtypeStruct((s, KV_RANK), BF16),
                   jax.ShapeDtypeStruct((s, 2 * ROPE), F32)],
        compiler_params=_cparams(("parallel",)))(proj, gq, gkv, table)


def _mla_prep_bwd(proj, dcq, dckv, dkr, gq, gkv, table):
    s = proj.shape[0]
    ts = _pick(s, ROW_TILE, 16)

    def norm_bwd(xv, dy, g):
        xn, r = _rms(xv, None)
        dg = jnp.sum(dy * xn, axis=0, keepdims=True)
        dxn = dy * g
        return r * (dxn - xn * jnp.mean(dxn * xn, axis=-1, keepdims=True)), dg

    def body(p_ref, dcq_ref, dckv_ref, dkr_ref, gq_ref, gkv_ref, t_ref, dt_ref, dgq_ref, dgkv_ref):
        @pl.when(pl.program_id(0) == 0)
        def _():
            dgq_ref[...] = jnp.zeros_like(dgq_ref)
            dgkv_ref[...] = jnp.zeros_like(dgkv_ref)

        dxq, dgq = norm_bwd(p_ref[:, 0:Q_RANK], dcq_ref[...], gq_ref[...])
        dxkv, dgkv = norm_bwd(p_ref[:, Q_RANK:Q_RANK + KV_RANK], dckv_ref[...], gkv_ref[...])
        dgq_ref[...] += dgq
        dgkv_ref[...] += dgkv
        d = _low_lanes(dkr_ref[...])
        dslab = (d + pltpu.roll(d, ROPE, 1)) * t_ref[...]
        dt_ref[:, 0:Q_RANK] = dxq.astype(BF16)
        dt_ref[:, Q_RANK:Q_RANK + KV_RANK] = dxkv.astype(BF16)
        dt_ref[:, Q_RANK + KV_RANK:TAIL] = dslab.astype(BF16)

    return _pcall(
        body, name="mla_prep_bwd", grid=(s // ts,),
        in_specs=[pl.BlockSpec((ts, TAIL), lambda i: (i, TAIL0 // TAIL)), _row(ts, Q_RANK), _row(ts, KV_RANK),
                  _row(ts, 2 * ROPE), _vec(Q_RANK), _vec(KV_RANK), _row(ts, 2 * ROPE)],
        out_specs=[_row(ts, TAIL), _vec(Q_RANK), _vec(KV_RANK)],
        out_shape=[jax.ShapeDtypeStruct((s, TAIL), BF16), jax.ShapeDtypeStruct((1, Q_RANK), F32),
                   jax.ShapeDtypeStruct((1, KV_RANK), F32)],
        compiler_params=_cparams(("arbitrary",)))(proj, dcq, dckv, dkr, gq, gkv, table)


def _head_specs(ts):
    tok = lambda w: pl.BlockSpec((ts, w), lambda h, i: (i, 0))
    head = lambda w: pl.BlockSpec((None, ts, w), lambda h, i: (h, i, 0))
    wgt = lambda r, c: pl.BlockSpec((None, r, c), lambda h, i: (h, 0, 0))
    return tok, head, wgt


def _mla_qkv(cq, ckv, kr, wq, wkv, table):
    s = cq.shape[0]
    ts = _pick(s, ROW_TILE, 16)
    tok, head, wgt = _head_specs(ts)

    def body(cq_ref, ckv_ref, kr_ref, wq_ref, wkv_ref, t_ref, q_ref, k_ref, v_ref):
        qf = lax.dot_general(cq_ref[...], wq_ref[...], _DIMS["nn"], preferred_element_type=F32)
        q_ref[:, 0:NOPE] = qf[:, 0:NOPE].astype(BF16)
        q_ref[:, NOPE:QW] = _rope_slab(qf[:, NOPE:QW], t_ref[...]).astype(BF16)
        kv = lax.dot_general(ckv_ref[...], wkv_ref[...], _DIMS["nn"], preferred_element_type=F32)
        k_ref[:, 0:NOPE] = kv[:, 0:NOPE].astype(BF16)
        k_ref[:, NOPE:QW] = kr_ref[...].astype(BF16)
        v_ref[...] = kv[:, NOPE:NOPE + VDIM].astype(BF16)

    return _pcall(
        body, name="mla_qkv", grid=(MLA_H, s // ts),
        in_specs=[tok(Q_RANK), tok(KV_RANK), tok(2 * ROPE), wgt(Q_RANK, QW), wgt(KV_RANK, NOPE + VDIM), tok(2 * ROPE)],
        out_specs=[head(QW), head(QW), head(VDIM)],
        out_shape=[jax.ShapeDtypeStruct((MLA_H, s, QW), BF16), jax.ShapeDtypeStruct((MLA_H, s, QW), BF16),
                   jax.ShapeDtypeStruct((MLA_H, s, VDIM), BF16)],
        compiler_params=_cparams(("parallel", "parallel")))(cq, ckv, kr, wq, wkv, table)


def _mla_qkv_bwd(dq, dk, dv, cq, ckv, wq, wkv, table, comm=None):
    s = cq.shape[0]
    ts = _pick(s, ROW_TILE, 16)
    tok, head, wgt = _head_specs(ts)
    whole = lambda w: pl.BlockSpec((s, w), lambda h, i: (0, 0))

    def body(dq_ref, dk_ref, dv_ref, cq_ref, ckv_ref, wq_ref, wkv_ref, t_ref,
             dcq_ref, dckv_ref, dkr_ref, gwq_ref, gwkv_ref):
        h, i = pl.program_id(0), pl.program_id(1)
        rows = pl.ds(pl.multiple_of(i * ts, ts), ts)
        d = dq_ref[:, NOPE:QW]
        dslab = (d + pltpu.roll(d, ROPE, 1)) * t_ref[...]
        dqe = jnp.concatenate([dq_ref[:, 0:NOPE], dslab], axis=1).astype(BF16)
        dkv = jnp.concatenate([dk_ref[:, 0:NOPE], dv_ref[...]], axis=1).astype(BF16)
        dcq = lax.dot_general(dqe, wq_ref[...], _DIMS["nt"], preferred_element_type=F32)
        dckv = lax.dot_general(dkv, wkv_ref[...], _DIMS["nt"], preferred_element_type=F32)
        gwq = lax.dot_general(cq_ref[...], dqe, _DIMS["tn"], preferred_element_type=F32)
        gwkv = lax.dot_general(ckv_ref[...], dkv, _DIMS["tn"], preferred_element_type=F32)
        dkr = dk_ref[:, NOPE:QW]

        @pl.when(h == 0)
        def _():
            dcq_ref[rows, :] = dcq
            dckv_ref[rows, :] = dckv
            dkr_ref[rows, :] = dkr

        @pl.when(h > 0)
        def _():
            dcq_ref[rows, :] += dcq
            dckv_ref[rows, :] += dckv
            dkr_ref[rows, :] += dkr

        @pl.when(i == 0)
        def _():
            gwq_ref[...] = gwq
            gwkv_ref[...] = gwkv

        @pl.when(i > 0)
        def _():
            gwq_ref[...] += gwq
            gwkv_ref[...] += gwkv

    return _pcall(
        body, comm=comm, name="mla_qkv_bwd", grid=(MLA_H, s // ts),
        in_specs=[head(QW), head(QW), head(VDIM), tok(Q_RANK), tok(KV_RANK), wgt(Q_RANK, QW),
                  wgt(KV_RANK, NOPE + VDIM), tok(2 * ROPE)],
        out_specs=[whole(Q_RANK), whole(KV_RANK), whole(2 * ROPE), wgt(Q_RANK, QW), wgt(KV_RANK, NOPE + VDIM)],
        out_shape=[jax.ShapeDtypeStruct((s, Q_RANK), F32), jax.ShapeDtypeStruct((s, KV_RANK), F32),
                   jax.ShapeDtypeStruct((s, 2 * ROPE), F32), jax.ShapeDtypeStruct((MLA_H, Q_RANK, QW), F32),
                   jax.ShapeDtypeStruct((MLA_H, KV_RANK, NOPE + VDIM), F32)],
        compiler_params=_cparams(("arbitrary", "arbitrary")))(dq, dk, dv, cq, ckv, wq, wkv, table)


def _diag_mask(t):
    return lax.broadcasted_iota(jnp.int32, (t, t), 1) <= lax.broadcasted_iota(jnp.int32, (t, t), 0)


def _mla_fwd(q, k, v, comm=None):
    s = q.shape[1]
    t = _pick(s, ATT_T, 128)
    nt = s // t
    hb = MLA_HB

    def body(q_ref, k_ref, v_ref, o_ref, lse_ref, m_ref, l_ref, acc_ref):
        i, j = pl.program_id(1), pl.program_id(2)

        @pl.when(j == 0)
        def _():
            m_ref[...] = jnp.full_like(m_ref, -jnp.inf)
            l_ref[...] = jnp.zeros_like(l_ref)
            acc_ref[...] = jnp.zeros_like(acc_ref)

        def step(diagonal):
            for h in range(hb):
                sc = lax.dot_general(q_ref[h], k_ref[h], _DIMS["nt"], preferred_element_type=F32) * MLA_SCALE
                if diagonal:
                    sc = jnp.where(_diag_mask(t), sc, -jnp.inf)
                m_old = m_ref[h]
                m_new = jnp.maximum(m_old, jnp.max(sc, axis=-1, keepdims=True))
                alpha = jnp.exp(m_old - m_new)
                p = jnp.exp(sc - m_new)
                l_ref[h] = alpha * l_ref[h] + jnp.sum(p, axis=-1, keepdims=True)
                acc_ref[h] = alpha * acc_ref[h] + lax.dot_general(p.astype(BF16), v_ref[h], _DIMS["nn"],
                                                                  preferred_element_type=F32)
                m_ref[h] = m_new

        @pl.when(j < i)
        def _():
            step(False)

        @pl.when(j == i)
        def _():
            step(True)

        @pl.when(j == nt - 1)
        def _():
            for h in range(hb):
                o_ref[:, h * VDIM:(h + 1) * VDIM] = acc_ref[h] / l_ref[h]
                lse_ref[h] = m_ref[h] + jnp.log(l_ref[h])

    kv = lambda w: pl.BlockSpec((hb, t, w), lambda h, i, j: (h, jnp.minimum(i, j), 0))
    return _pcall(
        body, comm=comm, name="mla_fwd", grid=(MLA_H // hb, nt, nt),
        in_specs=[pl.BlockSpec((hb, t, QW), lambda h, i, j: (h, i, 0)), kv(QW), kv(VDIM)],
        out_specs=[pl.BlockSpec((t, hb * VDIM), lambda h, i, j: (i, h)), pl.BlockSpec((hb, t, 1), lambda h, i, j: (h, i, 0))],
        out_shape=[jax.ShapeDtypeStruct((s, MLA_H * VDIM), F32), jax.ShapeDtypeStruct((MLA_H, s, 1), F32)],
        scratch_shapes=[pltpu.VMEM((hb, t, 1), F32), pltpu.VMEM((hb, t, 1), F32), pltpu.VMEM((hb, t, VDIM), F32)],
        compiler_params=_cparams(("parallel", "parallel", "arbitrary")))(q, k, v)


def _mla_bwd(q, k, v, dmix, o, lse, comm=None):
    s = q.shape[1]
    t = _pick(s, ATT_T, 128)
    nt = s // t
    hb = MLA_HB
    o_blk0 = SWA_HEADS * SWA_DH // (hb * VDIM)

    def body(q_ref, k_ref, v_ref, do_ref, o_ref, lse_ref, dq_ref, dk_ref, dv_ref, dk_acc, dv_acc):
        j, i = pl.program_id(1), pl.program_id(2)
        rows = pl.ds(pl.multiple_of(i * t, t), t)

        @pl.when(i == 0)
        def _():
            dk_acc[...] = jnp.zeros_like(dk_acc)
            dv_acc[...] = jnp.zeros_like(dv_acc)

        def step(diagonal):
            for h in range(hb):
                qv, kv_ = q_ref[h], k_ref[h]
                dov = do_ref[:, h * VDIM:(h + 1) * VDIM]
                sc = lax.dot_general(qv, kv_, _DIMS["nt"], preferred_element_type=F32) * MLA_SCALE
                p = jnp.exp(sc - lse_ref[h])
                if diagonal:
                    p = jnp.where(_diag_mask(t), p, 0.0)
                dob = dov.astype(BF16)
                dp = lax.dot_general(dob, v_ref[h], _DIMS["nt"], preferred_element_type=F32)
                delta = jnp.sum(dov * o_ref[:, h * VDIM:(h + 1) * VDIM], axis=-1, keepdims=True)
                ds = (p * (dp - delta) * MLA_SCALE).astype(BF16)
                dv_acc[h] += lax.dot_general(p.astype(BF16), dob, _DIMS["tn"], preferred_element_type=F32)
                dk_acc[h] += lax.dot_general(ds, qv, _DIMS["tn"], preferred_element_type=F32)
                dqv = lax.dot_general(ds, kv_, _DIMS["nn"], preferred_element_type=F32)

                @pl.when(j == 0)
                def _():
                    dq_ref[h, rows, :] = dqv

                @pl.when(j > 0)
                def _():
                    dq_ref[h, rows, :] += dqv

        @pl.when(i > j)
        def _():
            step(False)

        @pl.when(i == j)
        def _():
            step(True)

        @pl.when(i == nt - 1)
        def _():
            dk_ref[...] = dk_acc[...]
            dv_ref[...] = dv_acc[...]

    qi = lambda h, j, i: (h, jnp.maximum(i, j), 0)
    return _pcall(
        body, comm=comm, name="mla_bwd", grid=(MLA_H // hb, nt, nt),
        in_specs=[pl.BlockSpec((hb, t, QW), qi),
                  pl.BlockSpec((hb, t, QW), lambda h, j, i: (h, j, 0)),
                  pl.BlockSpec((hb, t, VDIM), lambda h, j, i: (h, j, 0)),
                  pl.BlockSpec((t, hb * VDIM), lambda h, j, i: (jnp.maximum(i, j), o_blk0 + h)),
                  pl.BlockSpec((t, hb * VDIM), lambda h, j, i: (jnp.maximum(i, j), h)),
                  pl.BlockSpec((hb, t, 1), qi)],
        out_specs=[pl.BlockSpec((hb, s, QW), lambda h, j, i: (h, 0, 0)),
                   pl.BlockSpec((hb, t, QW), lambda h, j, i: (h, j, 0)),
                   pl.BlockSpec((hb, t, VDIM), lambda h, j, i: (h, j, 0))],
        out_shape=[jax.ShapeDtypeStruct((MLA_H, s, QW), F32), jax.ShapeDtypeStruct((MLA_H, s, QW), F32),
                   jax.ShapeDtypeStruct((MLA_H, s, VDIM), F32)],
        scratch_shapes=[pltpu.VMEM((hb, t, QW), F32), pltpu.VMEM((hb, t, VDIM), F32)],
        compiler_params=_cparams(("arbitrary", "arbitrary", "arbitrary")))(q, k, v, dmix, o, lse)


def _adamw(name, w, g, m, v, parts):
    r, c = w.shape
    n_parts = g.shape[0] if parts else 1
    tr = r if r * c <= ADAM_ELEMS else _pick(r, max(8, ADAM_ELEMS // c // 8 * 8), 8)
    c1 = 1.0 - ADAM_B1 ** ADAM_STEP
    c2 = 1.0 - ADAM_B2 ** ADAM_STEP

    def body(w_ref, g_ref, m_ref, v_ref, go_ref, d_ref, mo_ref, vo_ref):
        if parts:
            gv = g_ref[0].astype(F32)
            for j in range(1, n_parts):
                gv = gv + g_ref[j].astype(F32)
        else:
            gv = g_ref[...]
        mv = ADAM_B1 * m_ref[...] + (1.0 - ADAM_B1) * gv
        vv = ADAM_B2 * v_ref[...] + (1.0 - ADAM_B2) * (gv * gv)
        go_ref[...] = gv
        mo_ref[...] = mv
        vo_ref[...] = vv
        d_ref[...] = -ADAM_LR * ((mv / c1) / (jnp.sqrt(vv / c2) + ADAM_EPS) + ADAM_WD * w_ref[...])

    blk = pl.BlockSpec((tr, c), lambda i: (i, 0))
    g_spec = pl.BlockSpec((n_parts, tr, c), lambda i: (0, i, 0)) if parts else blk
    out = jax.ShapeDtypeStruct((r, c), F32)
    return _pcall(body, name=name, grid=(r // tr,), in_specs=[blk, g_spec, blk, blk], out_specs=[blk] * 4,
                  out_shape=[out] * 4, compiler_params=_cparams(("parallel",)))(w, g, m, v)


def _t5_bucket(dist):
    n = jnp.maximum(dist, 0)
    max_exact = REL_BUCKETS // 2
    nf = jnp.maximum(n, 1).astype(F32)
    large = max_exact + (jnp.log(nf / max_exact) / math.log(REL_MAX_DIST / max_exact)
                         * (REL_BUCKETS - max_exact)).astype(jnp.int32)
    return jnp.where(n < max_exact, n, jnp.minimum(large, REL_BUCKETS - 1))


def _swap_halves(w, r0):
    return jnp.concatenate([w[:, r0 + ROPE // 2:r0 + ROPE], w[:, r0:r0 + ROPE // 2]], axis=1)


def _fold_swapped(g, r0, width):
    sw = g[..., width:width + ROPE]
    half = ROPE // 2
    return jnp.concatenate([g[..., :r0], g[..., r0:r0 + half] + sw[..., half:], g[..., r0 + half:r0 + ROPE] + sw[..., :half],
                            g[..., r0 + ROPE:width]], axis=-1)


def _heads_major(a, heads):
    s = a.shape[0]
    return a.reshape(s, heads, SWA_DH).transpose(1, 0, 2)


def _tokens_major(a):
    h, s, d = a.shape
    return a.transpose(1, 0, 2).reshape(s, h * d)


def kernel(x, c, w_mod, b_mod, attn_norm_g, w_in, swa_sinks, rel_bias, mla_q_norm_g, w_uq, mla_kv_norm_g, w_ukv, w_out, mlp_norm_g, w_ff1, w_ff2, final_norm_g, loss_target, m_w_mod, m_b_mod, m_attn_norm_g, m_w_in, m_swa_sinks, m_rel_bias, m_mla_q_norm_g, m_w_uq, m_mla_kv_norm_g, m_w_ukv, m_w_out, m_mlp_norm_g, m_w_ff1, m_w_ff2, m_final_norm_g, v_w_mod, v_b_mod, v_attn_norm_g, v_w_in, v_swa_sinks, v_rel_bias, v_mla_q_norm_g, v_w_uq, v_mla_kv_norm_g, v_w_ukv, v_w_out, v_mlp_norm_g, v_w_ff1, v_w_ff2, v_final_norm_g):
    s, d = x.shape[1], x.shape[2]
    ffs = w_ff1.shape[2]
    ff = ffs * NDEV
    nmod = w_mod.shape[2]
    me = 4 * lax.axis_index("x") + 2 * lax.axis_index("y") + lax.axis_index("c")
    x2d, tgt = x[0], loss_target[0]
    final_g = final_norm_g.reshape(1, d)

    w_in_l = jnp.concatenate([w_in[0], _swap_halves(w_in[0], OFF_KR)], axis=1).astype(BF16)
    w_uq_l = jnp.concatenate([w_uq[0], _swap_halves(w_uq[0], NOPE)], axis=1).astype(BF16)
    core = jnp.full((1, 128), lax.axis_index("c"), F32)
    (c_all,) = _exchange("gather_c", _Gather([c]))

    b_cols = lax.dynamic_slice(b_mod, (0, me * nmod), (1, nmod))
    act_all, mod_cols, w_in_g, w_uq_g, w_ukv_g = _mod_fwd(
        c_all.reshape(NDEV, d), w_mod[0], b_cols, comm=[_Gather([w_in_l, w_uq_l, w_ukv[0].astype(BF16)])])
    w_in_e = w_in_g.reshape(d, IN_EXT)
    (mod_g,) = _exchange("gather_mod", _Gather([mod_cols]))
    mod = lax.dynamic_index_in_dim(mod_g, me, axis=1, keepdims=False).reshape(1, 6 * d)
    sh1, sc1, g1, sh2, sc2, g2 = [mod[:, i * d:(i + 1) * d] for i in range(6)]

    pos = jnp.arange(s, dtype=F32)
    inv_freq = ROPE_THETA ** (-jnp.arange(ROPE // 2, dtype=F32) / (ROPE // 2))
    ang = pos[:, None] * inv_freq[None, :]
    cos, sin = jnp.cos(ang), jnp.sin(ang)
    table = jnp.concatenate([cos, cos, -sin, sin], axis=1)
    q_loc = jnp.arange(BLOCK)[:, None]
    k_loc = jnp.arange(2 * BLOCK)[None, :]
    dist = q_loc + BLOCK - k_loc
    in_window = (dist >= 0) & (dist < BLOCK)
    onehot = (_t5_bucket(dist).reshape(-1, 1) == jnp.arange(REL_BUCKETS)[None, :]).astype(BF16)
    bias = _bias_expand(rel_bias.T, onehot.T).reshape(SWA_HEADS, BLOCK, 2 * BLOCK)
    bias = jnp.where(in_window[None], bias, -jnp.inf).reshape(SWA_KV, SWA_GROUP * BLOCK, 2 * BLOCK)
    sink_rows = jnp.broadcast_to(swa_sinks.reshape(SWA_HEADS, 1), (SWA_HEADS, BLOCK)).reshape(SWA_KV, SWA_GROUP * BLOCK, 1)

    h1 = _norm_mod("norm1", x2d, attn_norm_g, sc1, sh1)
    proj = _mm_plain("proj", h1, w_in_e, "nn", s, IN_EXT, d, F32, tn=IN_EXT)
    q_a = _heads_major(proj[:, :OFF_K].astype(BF16), SWA_HEADS).reshape(SWA_KV, SWA_GROUP, s, SWA_DH)
    k_a = _heads_major(proj[:, OFF_K:OFF_V].astype(BF16), SWA_KV)
    v_a = _heads_major(proj[:, OFF_V:OFF_CQ].astype(BF16), SWA_KV)
    o_a, lse_a, w_out_g = _swa_fwd(q_a, k_a, v_a, bias, sink_rows, comm=[_Gather([w_out[0].astype(BF16)])])
    w_out_f = w_out_g.reshape(MIX, d)

    cq, ckv, kr = _mla_prep(proj, mla_q_norm_g, mla_kv_norm_g, table)
    q_b, k_b, v_b = _mla_qkv(cq, ckv, kr, w_uq_g, w_ukv_g, table)
    o_b, lse_b, w_ff1_g, w_ff2_g = _mla_fwd(
        q_b, k_b, v_b, comm=[_Gather([w_ff1[0].astype(BF16), w_ff2[0].astype(BF16)])])
    mix = jnp.concatenate([_tokens_major(o_a.reshape(SWA_HEADS, s, SWA_DH)), o_b.astype(BF16)], axis=1)

    tm, tn, tk = _pick(s, MM_TM, 128), _pick(d, MM_TN, 128), _pick(MIX, MM_TK, 128)
    row_blk = pl.BlockSpec((tm, tn), lambda i, j, q: (i, j))
    gate_blk = pl.BlockSpec((1, tn), lambda i, j, q: (0, j))

    def gated_residual(acc, ex, outs):
        outs[0][...] = acc
        outs[1][...] = ex[0][...] + ex[1][...] * acc

    y1, x2 = _mm("out_proj", mix, w_out_f, "nn", (s // tm, d // tn, MIX // tk),
                 pl.BlockSpec((tm, tk), lambda i, j, q: (i, q)), pl.BlockSpec((tk, tn), lambda i, j, q: (q, j)),
                 [jax.ShapeDtypeStruct((s, d), F32)] * 2, [row_blk, row_blk], (tm, tn), gated_residual,
                 extras=(x2d, g1), extra_specs=(row_blk, gate_blk))

    h2 = _norm_mod("norm2", x2, mlp_norm_g, sc2, sh2)
    tnf, tkd = _pick(ffs, MM_TN, 128), _pick(d, MM_TK, 128)
    rf = ffs // tnf
    ff_blk = pl.BlockSpec((tm, tnf), lambda i, j, q: (i, j))

    def relu_sq(acc, ex, outs):
        u = jnp.maximum(acc, 0.0)
        outs[0][...] = u
        outs[1][...] = (u * u).astype(BF16)

    u, uu = _mm("ff1", h2, w_ff1_g, "nn", (s // tm, ff // tnf, d // tkd),
                pl.BlockSpec((tm, tkd), lambda i, j, q: (i, q)),
                pl.BlockSpec((None, tkd, tnf), lambda i, j, q: (j // rf, q, j % rf)),
                [jax.ShapeDtypeStruct((s, ff), F32), jax.ShapeDtypeStruct((s, ff), BF16)], [ff_blk, ff_blk],
                (tm, tnf), relu_sq)
    w_ff2_f = w_ff2_g.reshape(ff, d)
    tkf = _pick(ff, MM_TK, 128)
    y2, x3 = _mm("ff2", uu, w_ff2_f, "nn", (s // tm, d // tn, ff // tkf),
                 pl.BlockSpec((tm, tkf), lambda i, j, q: (i, q)), pl.BlockSpec((tkf, tn), lambda i, j, q: (q, j)),
                 [jax.ShapeDtypeStruct((s, d), F32)] * 2, [row_blk, row_blk], (tm, tn), gated_residual,
                 extras=(x2, g2), extra_specs=(row_blk, gate_blk))

    dx3, dy2, loss_p, dgf, dg2 = _loss_head(x3, tgt, y2, final_g, g2)
    loss = lax.psum(loss_p[0, 0], ("x", "y", "c"))

    def relu_sq_bwd(acc, ex, outs):
        outs[0][...] = (acc * (2.0 * ex[0][...])).astype(BF16)

    tnf2 = _pick(ff, MM_TN, 128)
    du = _mm("ff2_dx", dy2, w_ff2_f, "nt", (s // tm, ff // tnf2, d // tkd),
             pl.BlockSpec((tm, tkd), lambda i, j, q: (i, q)), pl.BlockSpec((tnf2, tkd), lambda i, j, q: (j, q)),
             [jax.ShapeDtypeStruct((s, ff), BF16)], [pl.BlockSpec((tm, tnf2), lambda i, j, q: (i, j))],
             (tm, tnf2), relu_sq_bwd, extras=(u,), extra_specs=(pl.BlockSpec((tm, tnf2), lambda i, j, q: (i, j)),))[0]
    gw_ff2 = _mm_plain("ff2_dw", uu, dy2, "tn", ff, d, s, BF16)
    tmd, tks = _pick(d, MM_TM, 128), _pick(s, MM_TK, 128)
    gw_ff2 = gw_ff2.reshape(NDEV, ffs, d)
    gw_ff1, s_ff2 = _mm("ff1_dw", h2, du, "tn", (d // tmd, ff // tnf, s // tks),
                        pl.BlockSpec((tks, tmd), lambda i, j, q: (q, i)), pl.BlockSpec((tks, tnf), lambda i, j, q: (q, j)),
                        [jax.ShapeDtypeStruct((NDEV, d, ffs), BF16)],
                        [pl.BlockSpec((None, tmd, tnf), lambda i, j, q: (j // rf, i, j % rf))], (tmd, tnf), _store(BF16),
                        comm=[_PairSwap([gw_ff2])])
    c_ff2 = _pair_sum("pair_ff2", gw_ff2, s_ff2, core)
    tkf1 = _pick(ffs, MM_TK, 128)
    rk = ffs // tkf1
    dh2, p_ff2, s_ff1 = _mm("ff1_dx", du, w_ff1_g, "nt", (s // tm, d // tn, ff // tkf1),
                            pl.BlockSpec((tm, tkf1), lambda i, j, q: (i, q)),
                            pl.BlockSpec((None, tn, tkf1), lambda i, j, q: (q // rk, j, q % rk)),
                            [jax.ShapeDtypeStruct((s, d), F32)], [row_blk], (tm, tn), _store(F32),
                            comm=[_ChipScatter([c_ff2]), _PairSwap([gw_ff1])])
    c_ff1 = _pair_sum("pair_ff1", gw_ff1, s_ff1, core)
    dx2, dy1, dsc2, dsh2, dgm, dg1 = _norm_mod_bwd("norm2_bwd", x2, dh2, dx3, mlp_norm_g, sc2, y1, g1)

    dmix = _mm_plain("out_proj_dx", dy1, w_out_f, "nt", s, MIX, d, F32)
    gw_out = _mm_plain("out_proj_dw", mix, dy1, "tn", MIX, d, s, BF16).reshape(NDEV, MIX // NDEV, d)

    dq_b, dk_b, dv_b, p_ff1, s_out = _mla_bwd(q_b, k_b, v_b, dmix, o_b, lse_b,
                                              comm=[_ChipScatter([c_ff1]), _PairSwap([gw_out])])
    c_out = _pair_sum("pair_out", gw_out, s_out, core)
    dcq, dckv, dkr, gw_uq_e, gw_ukv, p_out = _mla_qkv_bwd(dq_b, dk_b, dv_b, cq, ckv, w_uq_g, w_ukv_g, table,
                                                          comm=[_ChipScatter([c_out])])
    dtail, dgq, dgkv = _mla_prep_bwd(proj, dcq, dckv, dkr, mla_q_norm_g, mla_kv_norm_g, table)
    gw_uq = _fold_swapped(gw_uq_e, NOPE, NOPE + ROPE).astype(BF16)
    gw_ukv = gw_ukv.astype(BF16)

    do_a = _heads_major(dmix[:, :OFF_K].astype(BF16), SWA_HEADS).reshape(SWA_KV, SWA_GROUP, s, SWA_DH)
    dq_a, dkp, dkc, dvp, dvc, dbias, dsink, s_uq, s_ukv = _swa_bwd(
        q_a, k_a, v_a, do_a, lse_a, bias, sink_rows, comm=[_PairSwap([gw_uq, gw_ukv])])
    c_uq = _pair_sum("pair_uq", gw_uq, s_uq, core)
    c_ukv = _pair_sum("pair_ukv", gw_ukv, s_ukv, core)
    shift = lambda p: jnp.concatenate([p[:, BLOCK:], jnp.zeros_like(p[:, :BLOCK])], axis=1)
    dk_a, dv_a = dkc + shift(dkp), dvc + shift(dvp)
    drel_t, dsinks = _bias_reduce(dbias.reshape(SWA_HEADS, BLOCK * 2 * BLOCK), onehot, dsink.reshape(SWA_HEADS, BLOCK))
    dproj = jnp.concatenate([_tokens_major(dq_a.reshape(SWA_HEADS, s, SWA_DH)).astype(BF16),
                             _tokens_major(dk_a).astype(BF16), _tokens_major(dv_a).astype(BF16), dtail], axis=1)
    gw_in_e = _mm_plain("proj_dw", h1, dproj, "tn", d, IN_EXT, s, F32, tn=TAIL)
    gw_in = _fold_swapped(gw_in_e, OFF_KR, IN_COLS).reshape(NDEV, d // NDEV, IN_COLS).astype(BF16)
    tkt = IN_EXT
    dh1, s_in, p_uq, p_ukv = _mm(
        "proj_dx", dproj, w_in_e, "nt", (s // tm, d // tn, IN_EXT // tkt),
        pl.BlockSpec((tm, tkt), lambda i, j, q: (i, q)), pl.BlockSpec((tn, tkt), lambda i, j, q: (j, q)),
        [jax.ShapeDtypeStruct((s, d), F32)], [row_blk], (tm, tn), _store(F32),
        comm=[_PairSwap([gw_in]), _ChipScatter([c_uq, c_ukv])])
    c_in = _pair_sum("pair_in", gw_in, s_in, core)
    gx, dsc1, dsh1, dga, p_in = _norm_mod_bwd("norm1_bwd", x2d, dh1, dx2, attn_norm_g, sc1,
                                              comm=[_ChipScatter([c_in])])

    small = [jnp.concatenate([dsh1, dsc1, dg1, dsh2, dsc2, dg2], axis=1), dga, dgm, dgf, dgq, dgkv,
             dsinks.reshape(1, SWA_HEADS), drel_t.T.reshape(1, REL_BUCKETS * SWA_HEADS)]
    n_small = sum(a.shape[1] for a in small)
    n_pad = -n_small % 1024
    rows_small = (n_small + n_pad) // 128
    pad = jnp.zeros((1, n_pad), F32)
    pack = lambda parts: jnp.concatenate([p.reshape(1, -1) for p in parts] + [pad], axis=1).reshape(rows_small, 128)
    (small_g,) = _exchange("gather_small", _Gather([pack(small)]))
    small_names = (b_mod, attn_norm_g, mlp_norm_g, final_norm_g, mla_q_norm_g, mla_kv_norm_g, swa_sinks, rel_bias)
    small_m = (m_b_mod, m_attn_norm_g, m_mlp_norm_g, m_final_norm_g, m_mla_q_norm_g, m_mla_kv_norm_g, m_swa_sinks, m_rel_bias)
    small_v = (v_b_mod, v_attn_norm_g, v_mlp_norm_g, v_final_norm_g, v_mla_q_norm_g, v_mla_kv_norm_g, v_swa_sinks, v_rel_bias)
    small_out = _adamw("adamw_small", pack(small_names), small_g, pack(small_m), pack(small_v), parts=True)

    def unpack(flat):
        flat = flat.reshape(1, -1)
        out, off = [], 0
        for a in small_names:
            out.append(flat[:, off:off + a.size].reshape(a.shape))
            off += a.size
        return out

    sg, sd, sm, sv = [unpack(o) for o in small_out]

    dmod_cols = lax.dynamic_slice(small_g.reshape(NDEV, -1), (0, me * nmod), (NDEV, nmod))
    gw_mod = _mod_wgrad(act_all, dmod_cols)
    big = {"w_mod": _adamw("adamw_w_mod", w_mod[0], gw_mod, m_w_mod[0], v_w_mod[0], parts=False)}

    for name, w, p, m, v in (("w_in", w_in, p_in, m_w_in, v_w_in), ("w_uq", w_uq, p_uq, m_w_uq, v_w_uq),
                             ("w_ukv", w_ukv, p_ukv, m_w_ukv, v_w_ukv), ("w_out", w_out, p_out, m_w_out, v_w_out),
                             ("w_ff1", w_ff1, p_ff1, m_w_ff1, v_w_ff1), ("w_ff2", w_ff2, p_ff2, m_w_ff2, v_w_ff2)):
        big[name] = _adamw("adamw_" + name, w[0], p, m[0], v[0], parts=True)

    order = ("w_mod", "b_mod", "attn_norm_g", "w_in", "swa_sinks", "rel_bias", "mla_q_norm_g", "w_uq", "mla_kv_norm_g",
             "w_ukv", "w_out", "mlp_norm_g", "w_ff1", "w_ff2", "final_norm_g")
    small_idx = {"b_mod": 0, "attn_norm_g": 1, "mlp_norm_g": 2, "final_norm_g": 3, "mla_q_norm_g": 4,
                 "mla_kv_norm_g": 5, "swa_sinks": 6, "rel_bias": 7}
    outs = []
    for kind, small_list in enumerate((sg, sd, sm, sv)):
        for name in order:
            outs.append(small_list[small_idx[name]] if name in small_idx else big[name][kind][None])
    return (loss, gx[None], *outs)
```

```python
import functools
import math

import jax
import jax.numpy as jnp
from jax import lax
from jax.experimental import pallas as pl
from jax.experimental.pallas import tpu as pltpu

F32 = jnp.float32
BF16 = jnp.bfloat16

NDEV = 8
EPS = 1e-6
BLOCK = 128
SWA_HEADS, SWA_KV, SWA_DH, SWA_GROUP = 16, 2, 64, 8
REL_BUCKETS, REL_MAX_DIST = 32, 128
MLA_H, Q_RANK, KV_RANK, NOPE, ROPE, VDIM = 8, 384, 128, 128, 64, 128
ROPE_THETA = 10000.0
OFF_K, OFF_V, OFF_CQ, OFF_CKV, OFF_KR, IN_COLS = 1024, 1152, 1280, 1664, 1792, 1856
IN_EXT = IN_COLS + ROPE
TAIL0, TAIL = OFF_CQ, IN_EXT - OFF_CQ
QW = NOPE + 2 * ROPE
MIX = SWA_HEADS * SWA_DH + MLA_H * VDIM
MLA_SCALE = (NOPE + ROPE) ** -0.5
SWA_SCALE = SWA_DH ** -0.5

ADAM_LR, ADAM_B1, ADAM_B2, ADAM_EPS, ADAM_WD, ADAM_STEP = 0.001, 0.9, 0.999, 1e-08, 0.01, 10

VMEM_LIMIT = 52 * 1024 * 1024
ROW_TILE = 256
MM_TM, MM_TN, MM_TK = 1024, 1024, 2048
ATT_T = 512
MLA_HB = 2
ADAM_ELEMS = 128 * 1024


MESH_ID = pl.DeviceIdType.MESH


def _place():
    x, y, c = lax.axis_index("x"), lax.axis_index("y"), lax.axis_index("c")
    return x, y, c, 2 * x + y


def _chip(x, y, k):
    return (1 - x if k & 2 else x, 1 - y if k & 1 else y)


def _dma_sems(*counts):
    return [pltpu.SemaphoreType.DMA((n,)) for n in counts]


class _Gather:
    def __init__(self, arrays):
        self.arrays = list(arrays)
        n = len(self.arrays)
        self.out_shape = [jax.ShapeDtypeStruct((NDEV,) + a.shape, a.dtype) for a in self.arrays]
        self.sems = _dma_sems(7 * n, 7 * n, n)

    def _copy(self, sems, a, k, src, dst, to):
        return pltpu.make_async_remote_copy(src_ref=src, dst_ref=dst, send_sem=sems[0].at[7 * a + k],
                                            recv_sem=sems[1].at[7 * a + k], device_id=to, device_id_type=MESH_ID)

    def start(self, ins, outs, sems):
        x, y, c, q = _place()
        me = 2 * q + c
        for a in range(len(ins)):
            pltpu.make_async_copy(ins[a], outs[a].at[me], sems[2].at[a]).start()
            self._copy(sems, a, 0, ins[a], outs[a].at[me], (x, y, 1 - c)).start()
            for k in (1, 2, 3):
                self._copy(sems, a, k, ins[a], outs[a].at[me], (*_chip(x, y, k), c)).start()

    def finish(self, ins, outs, sems):
        x, y, c, q = _place()
        me, sib = 2 * q + c, (x, y, 1 - c)
        n = len(ins)
        for k in (1, 2, 3):
            for a in range(n):
                blk = outs[a].at[2 * (q ^ k) + c]
                self._copy(sems, a, k, ins[a], blk, (*_chip(x, y, k), c)).wait_recv()
                self._copy(sems, a, 3 + k, blk, blk, sib).start()
        for a in range(n):
            self._copy(sems, a, 0, ins[a], outs[a].at[2 * q + 1 - c], sib).wait_recv()
            for k in (1, 2, 3):
                blk = outs[a].at[2 * (q ^ k) + 1 - c]
                self._copy(sems, a, 3 + k, blk, blk, sib).wait_recv()
        for a in range(n):
            for k in range(7):
                self._copy(sems, a, k, ins[a], outs[a].at[me], sib).wait_send()
            pltpu.make_async_copy(ins[a], outs[a].at[me], sems[2].at[a]).wait()


class _PairSwap:
    def __init__(self, arrays):
        self.arrays = list(arrays)
        n = len(self.arrays)
        self.out_shape = [jax.ShapeDtypeStruct((NDEV // 2,) + a.shape[1:], a.dtype) for a in self.arrays]
        self.sems = _dma_sems(4 * n, 4 * n)

    def _copy(self, sems, a, p, src, dst, to):
        return pltpu.make_async_remote_copy(src_ref=src, dst_ref=dst, send_sem=sems[0].at[4 * a + p],
                                            recv_sem=sems[1].at[4 * a + p], device_id=to, device_id_type=MESH_ID)

    def start(self, ins, outs, sems):
        x, y, c, _ = _place()
        for a in range(len(ins)):
            for p in range(4):
                self._copy(sems, a, p, ins[a].at[2 * p + 1 - c], outs[a].at[p], (x, y, 1 - c)).start()

    def finish(self, ins, outs, sems):
        x, y, c, _ = _place()
        for a in range(len(ins)):
            for p in range(4):
                cp = self._copy(sems, a, p, ins[a].at[2 * p + 1 - c], outs[a].at[p], (x, y, 1 - c))
                cp.wait_recv()
                cp.wait_send()


class _ChipScatter:
    def __init__(self, arrays):
        self.arrays = list(arrays)
        n = len(self.arrays)
        self.out_shape = [jax.ShapeDtypeStruct(a.shape, a.dtype) for a in self.arrays]
        self.sems = _dma_sems(3 * n, 3 * n, n)

    def _copy(self, sems, a, k, src, dst, to):
        return pltpu.make_async_remote_copy(src_ref=src, dst_ref=dst, send_sem=sems[0].at[3 * a + k - 1],
                                            recv_sem=sems[1].at[3 * a + k - 1], device_id=to, device_id_type=MESH_ID)

    def start(self, ins, outs, sems):
        x, y, c, q = _place()
        for a in range(len(ins)):
            pltpu.make_async_copy(ins[a].at[q], outs[a].at[q], sems[2].at[a]).start()
            for k in (1, 2, 3):
                self._copy(sems, a, k, ins[a].at[q ^ k], outs[a].at[q], (*_chip(x, y, k), c)).start()

    def finish(self, ins, outs, sems):
        x, y, c, q = _place()
        for a in range(len(ins)):
            for k in (1, 2, 3):
                cp = self._copy(sems, a, k, ins[a].at[q ^ k], outs[a].at[q ^ k], (*_chip(x, y, k), c))
                cp.wait_recv()
                cp.wait_send()
            pltpu.make_async_copy(ins[a].at[q], outs[a].at[q], sems[2].at[a]).wait()


def _call(body, **kw):
    return pl.pallas_call(body, **kw)


def _pcall(body, comm=None, **kw):
    if not comm:
        return _call(body, **kw)
    grid = kw["grid"]
    in_specs, out_specs, out_shape = list(kw["in_specs"]), list(kw["out_specs"]), list(kw["out_shape"])
    scratch = list(kw.get("scratch_shapes", ()))
    n_in, n_out, n_scr = len(in_specs), len(out_shape), len(scratch)
    n_cin = [len(j.arrays) for j in comm]
    n_sem = [len(j.sems) for j in comm]
    n = sum(n_cin)
    hbm = pl.BlockSpec(memory_space=pltpu.HBM)

    def carried(*refs):
        ins, cins = refs[:n_in], refs[n_in:n_in + n]
        outs, couts = refs[n_in + n:n_in + n + n_out], refs[n_in + n + n_out:n_in + 2 * n + n_out]
        scr, sems = refs[n_in + 2 * n + n_out:n_in + 2 * n + n_out + n_scr], refs[n_in + 2 * n + n_out + n_scr:]
        ids = [pl.program_id(ax) for ax in range(len(grid))]
        first = functools.reduce(jnp.logical_and, [i == 0 for i in ids])
        last = functools.reduce(jnp.logical_and, [i == g - 1 for i, g in zip(ids, grid)])

        def each(method):
            ai = si = 0
            for job, na, ns in zip(comm, n_cin, n_sem):
                getattr(job, method)(cins[ai:ai + na], couts[ai:ai + na], sems[si:si + ns])
                ai, si = ai + na, si + ns

        @pl.when(first)
        def _():
            each("start")

        body(*ins, *outs, *scr)

        @pl.when(last)
        def _():
            each("finish")

    kw.update(in_specs=in_specs + [hbm] * n, out_specs=out_specs + [hbm] * n,
              out_shape=out_shape + [o for j in comm for o in j.out_shape],
              scratch_shapes=scratch + [sm for j in comm for sm in j.sems],
              compiler_params=_cparams(("arbitrary",) * len(grid)))
    call = _call(carried, **kw)
    return lambda *args: call(*args, *[a for j in comm for a in j.arrays])


def _cparams(sem):
    return pltpu.CompilerParams(dimension_semantics=sem, vmem_limit_bytes=VMEM_LIMIT)


def _pick(n, pref, align):
    if n <= pref:
        return n
    t = (pref // align) * align
    while t >= align:
        if n % t == 0:
            return t
        t -= align
    return n


def _split3(x):
    a = x.astype(BF16)
    r = x - a.astype(F32)
    b = r.astype(BF16)
    c = (r - b.astype(F32)).astype(BF16)
    return a, b, c


def _exchange(name, job):
    n = len(job.arrays)

    def body(*refs):
        ins, outs, sems = refs[:n], refs[n:2 * n], refs[2 * n:]
        job.start(ins, outs, sems)
        job.finish(ins, outs, sems)

    hbm = pl.BlockSpec(memory_space=pltpu.HBM)
    return _call(body, name=name, out_shape=job.out_shape, in_specs=[hbm] * n, out_specs=[hbm] * n,
                 scratch_shapes=job.sems)(*job.arrays)


def _pair_sum(name, g, r, core):
    _, rr, cc = g.shape
    tr = rr if rr * cc <= 4 * ADAM_ELEMS else _pick(rr, max(16, 4 * ADAM_ELEMS // cc // 16 * 16), 16)

    def body(g_ref, r_ref, c_ref, o_ref):
        north = c_ref[:, 0:1] > 0.5
        mine = jnp.where(north, g_ref[1].astype(F32), g_ref[0].astype(F32))
        o_ref[...] = (mine + r_ref[...].astype(F32)).astype(o_ref.dtype)

    return _pcall(
        body, name=name, grid=(NDEV // 2, rr // tr),
        in_specs=[pl.BlockSpec((None, 2, tr, cc), lambda p, i: (p, 0, i, 0)),
                  pl.BlockSpec((None, tr, cc), lambda p, i: (p, i, 0)), pl.BlockSpec((1, 128), lambda p, i: (0, 0))],
        out_specs=pl.BlockSpec((None, tr, cc), lambda p, i: (p, i, 0)),
        out_shape=jax.ShapeDtypeStruct((NDEV // 2, rr, cc), g.dtype),
        compiler_params=_cparams(("parallel", "parallel")))(g.reshape(NDEV // 2, 2, rr, cc), r, core)


_DIMS = {"nn": (((1,), (0,)), ((), ())), "nt": (((1,), (1,)), ((), ())), "tn": (((0,), (0,)), ((), ()))}


def _mm(name, a, b, kind, grid, a_spec, b_spec, out_shape, out_specs, acc_shape, epilogue,
        extras=(), extra_specs=(), comm=None):
    nk, ne, no = grid[2], len(extras), len(out_shape)

    def body(*refs):
        a_ref, b_ref = refs[0], refs[1]
        ex, outs = refs[2:2 + ne], refs[2 + ne:2 + ne + no]
        part = lax.dot_general(a_ref[...].astype(BF16), b_ref[...].astype(BF16), _DIMS[kind],
                               preferred_element_type=F32)
        if nk == 1:
            epilogue(part, ex, outs)
            return
        acc = refs[-1]
        k = pl.program_id(2)

        @pl.when(k == 0)
        def _():
            acc[...] = part

        @pl.when(jnp.logical_and(k > 0, k < nk - 1))
        def _():
            acc[...] += part

        @pl.when(k == nk - 1)
        def _():
            epilogue(acc[...] + part, ex, outs)

    return _pcall(
        body, comm=comm, name=name, grid=grid, in_specs=[a_spec, b_spec, *extra_specs], out_specs=out_specs,
        out_shape=out_shape, scratch_shapes=[pltpu.VMEM(acc_shape, F32)] if nk > 1 else [],
        compiler_params=_cparams(("parallel", "parallel", "arbitrary")),
    )(a, b, *extras)


def _store(dtype):
    def epi(acc, ex, outs):
        outs[0][...] = acc.astype(dtype)
    return epi


def _mm_plain(name, a, b, kind, m, n, k, out_dtype, tm=None, tn=None, tk=None):
    tm = _pick(m, tm or MM_TM, 128)
    tn = _pick(n, tn or MM_TN, 128)
    tk = _pick(k, tk or MM_TK, 128)
    a_spec = pl.BlockSpec((tk, tm), lambda i, j, q: (q, i)) if kind == "tn" else pl.BlockSpec((tm, tk), lambda i, j, q: (i, q))
    b_spec = pl.BlockSpec((tn, tk), lambda i, j, q: (j, q)) if kind == "nt" else pl.BlockSpec((tk, tn), lambda i, j, q: (q, j))
    return _mm(name, a, b, kind, (m // tm, n // tn, k // tk), a_spec, b_spec,
               [jax.ShapeDtypeStruct((m, n), out_dtype)], [pl.BlockSpec((tm, tn), lambda i, j, q: (i, j))],
               (tm, tn), _store(out_dtype))[0]


def _row(ts, d):
    return pl.BlockSpec((ts, d), lambda i: (i, 0))


def _vec(d):
    return pl.BlockSpec((1, d), lambda i: (0, 0))


def _norm_mod(name, x, gain, sc, sh):
    s, d = x.shape
    ts = _pick(s, ROW_TILE, 16)

    def body(x_ref, g_ref, sc_ref, sh_ref, h_ref):
        xv = x_ref[...]
        r = lax.rsqrt(jnp.mean(xv * xv, axis=-1, keepdims=True) + EPS)
        h_ref[...] = ((xv * r) * g_ref[...] * (1.0 + sc_ref[...]) + sh_ref[...]).astype(BF16)

    return _pcall(body, name=name, grid=(s // ts,), in_specs=[_row(ts, d), _vec(d), _vec(d), _vec(d)],
                  out_specs=_row(ts, d), out_shape=jax.ShapeDtypeStruct((s, d), BF16),
                  compiler_params=_cparams(("parallel",)))(x, gain, sc, sh)


def _loss_head(x3, tgt, y2, gf, g2):
    s, d = x3.shape
    ts = _pick(s, ROW_TILE, 16)

    def body(x_ref, t_ref, y_ref, gf_ref, g2_ref, dx_ref, dy_ref, loss_ref, dgf_ref, dg2_ref):
        @pl.when(pl.program_id(0) == 0)
        def _():
            loss_ref[...] = jnp.zeros_like(loss_ref)
            dgf_ref[...] = jnp.zeros_like(dgf_ref)
            dg2_ref[...] = jnp.zeros_like(dg2_ref)

        xv = x_ref[...]
        r = lax.rsqrt(jnp.mean(xv * xv, axis=-1, keepdims=True) + EPS)
        xn = xv * r
        err = xn * gf_ref[...] - t_ref[...]
        loss_ref[...] += 0.5 * jnp.sum(jnp.mean(err * err, axis=-1, keepdims=True), axis=0, keepdims=True)
        dout = err * (1.0 / d)
        dgf_ref[...] += jnp.sum(dout * xn, axis=0, keepdims=True)
        dxn = dout * gf_ref[...]
        dx = r * (dxn - xn * jnp.mean(dxn * xn, axis=-1, keepdims=True))
        dx_ref[...] = dx
        dy_ref[...] = (dx * g2_ref[...]).astype(BF16)
        dg2_ref[...] += jnp.sum(dx * y_ref[...], axis=0, keepdims=True)

    one = pl.BlockSpec((1, 1), lambda i: (0, 0))
    return _pcall(
        body, name="loss_head", grid=(s // ts,),
        in_specs=[_row(ts, d), _row(ts, d), _row(ts, d), _vec(d), _vec(d)],
        out_specs=[_row(ts, d), _row(ts, d), one, _vec(d), _vec(d)],
        out_shape=[jax.ShapeDtypeStruct((s, d), F32), jax.ShapeDtypeStruct((s, d), BF16),
                   jax.ShapeDtypeStruct((1, 1), F32), jax.ShapeDtypeStruct((1, d), F32),
                   jax.ShapeDtypeStruct((1, d), F32)],
        compiler_params=_cparams(("arbitrary",)))(x3, tgt, y2, gf, g2)


def _norm_mod_bwd(name, x, dh, dres, gain, sc, y_prev=None, gate=None, comm=None):
    s, d = x.shape
    ts = _pick(s, ROW_TILE, 16)
    gated = y_prev is not None

    def body(*refs):
        if gated:
            x_ref, dh_ref, dr_ref, g_ref, sc_ref, y_ref, gt_ref, dx_ref, dy_ref, dsc_ref, dsh_ref, dg_ref, dgt_ref = refs
        else:
            x_ref, dh_ref, dr_ref, g_ref, sc_ref, dx_ref, dsc_ref, dsh_ref, dg_ref = refs

        @pl.when(pl.program_id(0) == 0)
        def _():
            dsc_ref[...] = jnp.zeros_like(dsc_ref)
            dsh_ref[...] = jnp.zeros_like(dsh_ref)
            dg_ref[...] = jnp.zeros_like(dg_ref)
            if gated:
                dgt_ref[...] = jnp.zeros_like(dgt_ref)

        xv, dhv = x_ref[...], dh_ref[...]
        r = lax.rsqrt(jnp.mean(xv * xv, axis=-1, keepdims=True) + EPS)
        xn = xv * r
        dsc_ref[...] += jnp.sum(dhv * (xn * g_ref[...]), axis=0, keepdims=True)
        dsh_ref[...] += jnp.sum(dhv, axis=0, keepdims=True)
        da = dhv * (1.0 + sc_ref[...])
        dg_ref[...] += jnp.sum(da * xn, axis=0, keepdims=True)
        dxn = da * g_ref[...]
        dx = dr_ref[...] + r * (dxn - xn * jnp.mean(dxn * xn, axis=-1, keepdims=True))
        dx_ref[...] = dx
        if gated:
            dy_ref[...] = (dx * gt_ref[...]).astype(BF16)
            dgt_ref[...] += jnp.sum(dx * y_ref[...], axis=0, keepdims=True)

    ins = [x, dh, dres, gain, sc] + ([y_prev, gate] if gated else [])
    in_specs = [_row(ts, d)] * 3 + [_vec(d)] * 2 + ([_row(ts, d), _vec(d)] if gated else [])
    vec_out = jax.ShapeDtypeStruct((1, d), F32)
    out_shape = [jax.ShapeDtypeStruct((s, d), F32)] + ([jax.ShapeDtypeStruct((s, d), BF16)] if gated else [])
    out_shape += [vec_out] * (4 if gated else 3)
    out_specs = [_row(ts, d)] * (2 if gated else 1) + [_vec(d)] * (4 if gated else 3)
    return _pcall(body, comm=comm, name=name, grid=(s // ts,), in_specs=in_specs, out_specs=out_specs,
                  out_shape=out_shape, compiler_params=_cparams(("arbitrary",)))(*ins)


def _dot3(a, b, dims):
    a1, a2, _ = _split3(a)
    b1, b2, _ = _split3(b)
    dot = functools.partial(lax.dot_general, dimension_numbers=dims, preferred_element_type=F32)
    return dot(a1, b1) + (dot(a1, b2) + dot(a2, b1))


def _mod_fwd(c_all, w, b_cols, comm=None):
    nb, d = c_all.shape
    n = w.shape[1]
    tk = _pick(d, 512, 128)
    nk = d // tk

    def body(c_ref, w_ref, b_ref, act_ref, out_ref):
        k = pl.program_id(0)
        cv = c_ref[...]
        act = cv * (1.0 / (1.0 + jnp.exp(-cv)))
        act_ref[...] = act

        @pl.when(k == 0)
        def _():
            out_ref[...] = jnp.broadcast_to(b_ref[...], out_ref.shape)

        out_ref[...] += _dot3(act, w_ref[...], _DIMS["nn"])

    return _pcall(
        body, comm=comm, name="mod_fwd", grid=(nk,),
        in_specs=[pl.BlockSpec((nb, tk), lambda k: (0, k)), pl.BlockSpec((tk, n), lambda k: (k, 0)),
                  pl.BlockSpec((1, n), lambda k: (0, 0))],
        out_specs=[pl.BlockSpec((nb, tk), lambda k: (0, k)), pl.BlockSpec((nb, n), lambda k: (0, 0))],
        out_shape=[jax.ShapeDtypeStruct((nb, d), F32), jax.ShapeDtypeStruct((nb, n), F32)],
        compiler_params=_cparams(("arbitrary",)))(c_all, w, b_cols)


def _mod_wgrad(act_all, dmod_cols):
    nb, d = act_all.shape
    n = dmod_cols.shape[1]
    tm = _pick(d, 512, 128)

    def body(a_ref, d_ref, o_ref):
        o_ref[...] = _dot3(a_ref[...], d_ref[...], _DIMS["tn"])

    return _pcall(
        body, name="mod_wgrad", grid=(d // tm,),
        in_specs=[pl.BlockSpec((nb, tm), lambda i: (0, i)), pl.BlockSpec((nb, n), lambda i: (0, 0))],
        out_specs=pl.BlockSpec((tm, n), lambda i: (i, 0)), out_shape=jax.ShapeDtypeStruct((d, n), F32),
        compiler_params=_cparams(("parallel",)))(act_all, dmod_cols)


def _bias_expand(rel_t, onehot_t):
    h, _ = rel_t.shape
    n = onehot_t.shape[1]

    def body(r_ref, o_ref, out_ref):
        a, b, c = _split3(r_ref[...])
        dot = functools.partial(lax.dot_general, dimension_numbers=_DIMS["nn"], preferred_element_type=F32)
        oh = o_ref[...]
        out_ref[...] = dot(a, oh) + (dot(b, oh) + dot(c, oh))

    full = lambda shp: pl.BlockSpec(shp, lambda: (0,) * len(shp))
    return _pcall(body, name="bias_expand", in_specs=[full(rel_t.shape), full(onehot_t.shape)],
                  out_specs=full((h, n)), out_shape=jax.ShapeDtypeStruct((h, n), F32),
                  compiler_params=pltpu.CompilerParams(vmem_limit_bytes=VMEM_LIMIT))(rel_t, onehot_t)


def _bias_reduce(dbias, onehot, dsink_rows):
    h, n = dbias.shape

    def body(d_ref, o_ref, s_ref, out_ref, so_ref):
        a, b, c = _split3(d_ref[...])
        dot = functools.partial(lax.dot_general, dimension_numbers=_DIMS["nn"], preferred_element_type=F32)
        oh = o_ref[...]
        out_ref[...] = dot(a, oh) + (dot(b, oh) + dot(c, oh))
        so_ref[...] = jnp.sum(s_ref[...], axis=-1, keepdims=True)

    full = lambda shp: pl.BlockSpec(shp, lambda: (0,) * len(shp))
    return _pcall(body, name="bias_reduce", in_specs=[full(dbias.shape), full(onehot.shape), full(dsink_rows.shape)],
                  out_specs=[full((h, REL_BUCKETS)), full((h, 1))],
                  out_shape=[jax.ShapeDtypeStruct((h, REL_BUCKETS), F32), jax.ShapeDtypeStruct((h, 1), F32)],
                  compiler_params=pltpu.CompilerParams(vmem_limit_bytes=VMEM_LIMIT))(dbias, onehot, dsink_rows)


def _swa_specs(s):
    rows = SWA_GROUP * BLOCK
    q_spec = pl.BlockSpec((None, SWA_GROUP, BLOCK, SWA_DH), lambda g, n: (g, 0, n, 0))
    kv_prev = pl.BlockSpec((None, BLOCK, SWA_DH), lambda g, n: (g, jnp.maximum(n - 1, 0), 0))
    kv_cur = pl.BlockSpec((None, BLOCK, SWA_DH), lambda g, n: (g, n, 0))
    bias_spec = pl.BlockSpec((None, rows, 2 * BLOCK), lambda g, n: (g, 0, 0))
    col_spec = pl.BlockSpec((None, rows, 1), lambda g, n: (g, 0, 0))
    lse_spec = pl.BlockSpec((None, None, rows, 1), lambda g, n: (g, n, 0, 0))
    return rows, q_spec, kv_prev, kv_cur, bias_spec, col_spec, lse_spec


def _swa_scores(q_ref, kp_ref, kc_ref, bias_ref, n):
    rows = SWA_GROUP * BLOCK
    q = q_ref[...].reshape(rows, SWA_DH)
    kb = jnp.concatenate([kp_ref[...], kc_ref[...]], axis=0)
    s = lax.dot_general(q, kb, _DIMS["nt"], preferred_element_type=F32) * SWA_SCALE + bias_ref[...]
    col = lax.broadcasted_iota(jnp.int32, s.shape, 1)
    s = jnp.where(jnp.logical_and(n == 0, col < BLOCK), -jnp.inf, s)
    return q, kb, s


def _swa_fwd(q, k, v, bias, sink_rows, comm=None):
    s = q.shape[2]
    nb = s // BLOCK
    rows, q_spec, kv_prev, kv_cur, bias_spec, col_spec, lse_spec = _swa_specs(s)

    def body(q_ref, kp_ref, kc_ref, vp_ref, vc_ref, bias_ref, sink_ref, o_ref, lse_ref):
        n = pl.program_id(1)
        _, _, sc = _swa_scores(q_ref, kp_ref, kc_ref, bias_ref, n)
        sink = sink_ref[...]
        m = jnp.maximum(jnp.max(sc, axis=-1, keepdims=True), sink)
        p = jnp.exp(sc - m)
        den = jnp.sum(p, axis=-1, keepdims=True) + jnp.exp(sink - m)
        p = p / den
        vb = jnp.concatenate([vp_ref[...], vc_ref[...]], axis=0)
        o = lax.dot_general(p.astype(BF16), vb, _DIMS["nn"], preferred_element_type=F32)
        o_ref[...] = o.reshape(SWA_GROUP, BLOCK, SWA_DH).astype(BF16)
        lse_ref[...] = m + jnp.log(den)

    return _pcall(
        body, comm=comm, name="swa_fwd", grid=(SWA_KV, nb),
        in_specs=[q_spec, kv_prev, kv_cur, kv_prev, kv_cur, bias_spec, col_spec],
        out_specs=[q_spec, lse_spec],
        out_shape=[jax.ShapeDtypeStruct(q.shape, BF16), jax.ShapeDtypeStruct((SWA_KV, nb, rows, 1), F32)],
        compiler_params=_cparams(("parallel", "parallel")))(q, k, k, v, v, bias, sink_rows)


def _swa_bwd(q, k, v, do, lse, bias, sink_rows, comm=None):
    s = q.shape[2]
    nb = s // BLOCK
    rows, q_spec, kv_prev, kv_cur, bias_spec, col_spec, lse_spec = _swa_specs(s)

    def body(q_ref, kp_ref, kc_ref, vp_ref, vc_ref, do_ref, lse_ref, bias_ref, sink_ref,
             dq_ref, dkp_ref, dkc_ref, dvp_ref, dvc_ref, dbias_ref, dsink_ref):
        n = pl.program_id(1)

        @pl.when(n == 0)
        def _():
            dbias_ref[...] = jnp.zeros_like(dbias_ref)
            dsink_ref[...] = jnp.zeros_like(dsink_ref)

        qv, kb, sc = _swa_scores(q_ref, kp_ref, kc_ref, bias_ref, n)
        lse_v = lse_ref[...]
        p = jnp.exp(sc - lse_v)
        p_sink = jnp.exp(sink_ref[...] - lse_v)
        dov = do_ref[...].reshape(rows, SWA_DH)
        vb = jnp.concatenate([vp_ref[...], vc_ref[...]], axis=0)
        dp = lax.dot_general(dov, vb, _DIMS["nt"], preferred_element_type=F32)
        delta = jnp.sum(p * dp, axis=-1, keepdims=True)
        ds = p * (dp - delta)
        dbias_ref[...] += ds
        dsink_ref[...] += -p_sink * delta
        dsb = (ds * SWA_SCALE).astype(BF16)
        dq = lax.dot_general(dsb, kb, _DIMS["nn"], preferred_element_type=F32)
        dq_ref[...] = dq.reshape(SWA_GROUP, BLOCK, SWA_DH).astype(BF16)
        dk = lax.dot_general(dsb, qv, _DIMS["tn"], preferred_element_type=F32)
        dv = lax.dot_general(p.astype(BF16), dov, _DIMS["tn"], preferred_element_type=F32)
        dkp_ref[...] = dk[:BLOCK]
        dkc_ref[...] = dk[BLOCK:]
        dvp_ref[...] = dv[:BLOCK]
        dvc_ref[...] = dv[BLOCK:]

    kv_out = jax.ShapeDtypeStruct((SWA_KV, s, SWA_DH), F32)
    return _pcall(
        body, comm=comm, name="swa_bwd", grid=(SWA_KV, nb),
        in_specs=[q_spec, kv_prev, kv_cur, kv_prev, kv_cur, q_spec, lse_spec, bias_spec, col_spec],
        out_specs=[q_spec, kv_cur, kv_cur, kv_cur, kv_cur, bias_spec, col_spec],
        out_shape=[jax.ShapeDtypeStruct(q.shape, BF16), kv_out, kv_out, kv_out, kv_out,
                   jax.ShapeDtypeStruct(bias.shape, F32), jax.ShapeDtypeStruct(sink_rows.shape, F32)],
        compiler_params=_cparams(("arbitrary", "arbitrary")))(q, k, k, v, v, do, lse, bias, sink_rows)


def _rope_slab(slab, table):
    t = slab * table
    return t + pltpu.roll(t, ROPE, 1)


def _low_lanes(v):
    lane = lax.broadcasted_iota(jnp.int32, v.shape, 1)
    return jnp.where(lane < ROPE, v, 0.0)


def _rms(xv, g):
    r = lax.rsqrt(jnp.mean(xv * xv, axis=-1, keepdims=True) + EPS)
    return xv * r, r


def _mla_prep(proj, gq, gkv, table):
    s = proj.shape[0]
    ts = _pick(s, ROW_TILE, 16)

    def body(p_ref, gq_ref, gkv_ref, t_ref, cq_ref, ckv_ref, kr_ref):
        xq, _ = _rms(p_ref[:, 0:Q_RANK], None)
        cq_ref[...] = (xq * gq_ref[...]).astype(BF16)
        xkv, _ = _rms(p_ref[:, Q_RANK:Q_RANK + KV_RANK], None)
        ckv_ref[...] = (xkv * gkv_ref[...]).astype(BF16)
        kr_ref[...] = _low_lanes(_rope_slab(p_ref[:, Q_RANK + KV_RANK:TAIL], t_ref[...]))

    return _pcall(
        body, name="mla_prep", grid=(s // ts,),
        in_specs=[pl.BlockSpec((ts, TAIL), lambda i: (i, TAIL0 // TAIL)), _vec(Q_RANK), _vec(KV_RANK), _row(ts, 2 * ROPE)],
        out_specs=[_row(ts, Q_RANK), _row(ts, KV_RANK), _row(ts, 2 * ROPE)],
        out_shape=[jax.ShapeDtypeStruct((s, Q_RANK), BF16), jax.ShapeDtypeStruct((s, KV_RANK), BF16),
                   jax.ShapeDtypeStruct((s, 2 * ROPE), F32)],
        compiler_params=_cparams(("parallel",)))(proj, gq, gkv, table)


def _mla_prep_bwd(proj, dcq, dckv, dkr, gq, gkv, table):
    s = proj.shape[0]
    ts = _pick(s, ROW_TILE, 16)

    def norm_bwd(xv, dy, g):
        xn, r = _rms(xv, None)
        dg = jnp.sum(dy * xn, axis=0, keepdims=True)
        dxn = dy * g
        return r * (dxn - xn * jnp.mean(dxn * xn, axis=-1, keepdims=True)), dg

    def body(p_ref, dcq_ref, dckv_ref, dkr_ref, gq_ref, gkv_ref, t_ref, dt_ref, dgq_ref, dgkv_ref):
        @pl.when(pl.program_id(0) == 0)
        def _():
            dgq_ref[...] = jnp.zeros_like(dgq_ref)
            dgkv_ref[...] = jnp.zeros_like(dgkv_ref)

        dxq, dgq = norm_bwd(p_ref[:, 0:Q_RANK], dcq_ref[...], gq_ref[...])
        dxkv, dgkv = norm_bwd(p_ref[:, Q_RANK:Q_RANK + KV_RANK], dckv_ref[...], gkv_ref[...])
        dgq_ref[...] += dgq
        dgkv_ref[...] += dgkv
        d = _low_lanes(dkr_ref[...])
        dslab = (d + pltpu.roll(d, ROPE, 1)) * t_ref[...]
        dt_ref[:, 0:Q_RANK] = dxq.astype(BF16)
        dt_ref[:, Q_RANK:Q_RANK + KV_RANK] = dxkv.astype(BF16)
        dt_ref[:, Q_RANK + KV_RANK:TAIL] = dslab.astype(BF16)

    return _pcall(
        body, name="mla_prep_bwd", grid=(s // ts,),
        in_specs=[pl.BlockSpec((ts, TAIL), lambda i: (i, TAIL0 // TAIL)), _row(ts, Q_RANK), _row(ts, KV_RANK),
                  _row(ts, 2 * ROPE), _vec(Q_RANK), _vec(KV_RANK), _row(ts, 2 * ROPE)],
        out_specs=[_row(ts, TAIL), _vec(Q_RANK), _vec(KV_RANK)],
        out_shape=[jax.ShapeDtypeStruct((s, TAIL), BF16), jax.ShapeDtypeStruct((1, Q_RANK), F32),
                   jax.ShapeDtypeStruct((1, KV_RANK), F32)],
        compiler_params=_cparams(("arbitrary",)))(proj, dcq, dckv, dkr, gq, gkv, table)


def _head_specs(ts):
    tok = lambda w: pl.BlockSpec((ts, w), lambda h, i: (i, 0))
    head = lambda w: pl.BlockSpec((None, ts, w), lambda h, i: (h, i, 0))
    wgt = lambda r, c: pl.BlockSpec((None, r, c), lambda h, i: (h, 0, 0))
    return tok, head, wgt


def _mla_qkv(cq, ckv, kr, wq, wkv, table):
    s = cq.shape[0]
    ts = _pick(s, 2 * ROW_TILE, 16)
    tok, head, wgt = _head_specs(ts)

    def body(cq_ref, ckv_ref, kr_ref, wq_ref, wkv_ref, t_ref, q_ref, k_ref, v_ref):
        qf = lax.dot_general(cq_ref[...], wq_ref[...], _DIMS["nn"], preferred_element_type=F32)
        q_ref[:, 0:NOPE] = qf[:, 0:NOPE].astype(BF16)
        q_ref[:, NOPE:QW] = _rope_slab(qf[:, NOPE:QW], t_ref[...]).astype(BF16)
        kv = lax.dot_general(ckv_ref[...], wkv_ref[...], _DIMS["nn"], preferred_element_type=F32)
        k_ref[:, 0:NOPE] = kv[:, 0:NOPE].astype(BF16)
        k_ref[:, NOPE:QW] = kr_ref[...].astype(BF16)
        v_ref[...] = kv[:, NOPE:NOPE + VDIM].astype(BF16)

    return _pcall(
        body, name="mla_qkv", grid=(MLA_H, s // ts),
        in_specs=[tok(Q_RANK), tok(KV_RANK), tok(2 * ROPE), wgt(Q_RANK, QW), wgt(KV_RANK, NOPE + VDIM), tok(2 * ROPE)],
        out_specs=[head(QW), head(QW), head(VDIM)],
        out_shape=[jax.ShapeDtypeStruct((MLA_H, s, QW), BF16), jax.ShapeDtypeStruct((MLA_H, s, QW), BF16),
                   jax.ShapeDtypeStruct((MLA_H, s, VDIM), BF16)],
        compiler_params=_cparams(("parallel", "parallel")))(cq, ckv, kr, wq, wkv, table)


def _mla_qkv_bwd(dq, dk, dv, cq, ckv, wq, wkv, table, comm=None):
    s = cq.shape[0]
    ts = _pick(s, 2 * ROW_TILE, 16)
    tok, head, wgt = _head_specs(ts)
    whole = lambda w: pl.BlockSpec((s, w), lambda h, i: (0, 0))

    def body(dq_ref, dk_ref, dv_ref, cq_ref, ckv_ref, wq_ref, wkv_ref, t_ref,
             dcq_ref, dckv_ref, dkr_ref, gwq_ref, gwkv_ref):
        h, i = pl.program_id(0), pl.program_id(1)
        rows = pl.ds(pl.multiple_of(i * ts, ts), ts)
        d = dq_ref[:, NOPE:QW]
        dslab = (d + pltpu.roll(d, ROPE, 1)) * t_ref[...]
        dqe = jnp.concatenate([dq_ref[:, 0:NOPE], dslab], axis=1).astype(BF16)
        dkv = jnp.concatenate([dk_ref[:, 0:NOPE], dv_ref[...]], axis=1).astype(BF16)
        dcq = lax.dot_general(dqe, wq_ref[...], _DIMS["nt"], preferred_element_type=F32)
        dckv = lax.dot_general(dkv, wkv_ref[...], _DIMS["nt"], preferred_element_type=F32)
        gwq = lax.dot_general(cq_ref[...], dqe, _DIMS["tn"], preferred_element_type=F32)
        gwkv = lax.dot_general(ckv_ref[...], dkv, _DIMS["tn"], preferred_element_type=F32)
        dkr = dk_ref[:, NOPE:QW]

        @pl.when(h == 0)
        def _():
            dcq_ref[rows, :] = dcq
            dckv_ref[rows, :] = dckv
            dkr_ref[rows, :] = dkr

        @pl.when(h > 0)
        def _():
            dcq_ref[rows, :] += dcq
            dckv_ref[rows, :] += dckv
            dkr_ref[rows, :] += dkr

        @pl.when(i == 0)
        def _():
            gwq_ref[...] = gwq
            gwkv_ref[...] = gwkv

        @pl.when(i > 0)
        def _():
            gwq_ref[...] += gwq
            gwkv_ref[...] += gwkv

    return _pcall(
        body, comm=comm, name="mla_qkv_bwd", grid=(MLA_H, s // ts),
        in_specs=[head(QW), head(QW), head(VDIM), tok(Q_RANK), tok(KV_RANK), wgt(Q_RANK, QW),
                  wgt(KV_RANK, NOPE + VDIM), tok(2 * ROPE)],
        out_specs=[whole(Q_RANK), whole(KV_RANK), whole(2 * ROPE), wgt(Q_RANK, QW), wgt(KV_RANK, NOPE + VDIM)],
        out_shape=[jax.ShapeDtypeStruct((s, Q_RANK), F32), jax.ShapeDtypeStruct((s, KV_RANK), F32),
                   jax.ShapeDtypeStruct((s, 2 * ROPE), F32), jax.ShapeDtypeStruct((MLA_H, Q_RANK, QW), F32),
                   jax.ShapeDtypeStruct((MLA_H, KV_RANK, NOPE + VDIM), F32)],
        compiler_params=_cparams(("arbitrary", "arbitrary")))(dq, dk, dv, cq, ckv, wq, wkv, table)


def _diag_mask(t):
    return lax.broadcasted_iota(jnp.int32, (t, t), 1) <= lax.broadcasted_iota(jnp.int32, (t, t), 0)


def _mla_fwd(q, k, v, comm=None):
    s = q.shape[1]
    t = _pick(s, ATT_T, 128)
    nt = s // t
    assert nt % 2 == 0
    hb = MLA_HB

    def fold(p, u):
        first = u <= p
        return jnp.where(first, p, nt - 1 - p), jnp.where(first, u, u - p - 1)

    def body(q_ref, k_ref, v_ref, o_ref, oh_ref, lse_ref, m_ref, l_ref, acc_ref):
        i, j = fold(pl.program_id(1), pl.program_id(2))

        @pl.when(j == 0)
        def _():
            m_ref[...] = jnp.full_like(m_ref, -jnp.inf)
            l_ref[...] = jnp.zeros_like(l_ref)
            acc_ref[...] = jnp.zeros_like(acc_ref)

        def step(diagonal):
            for h in range(hb):
                sc = lax.dot_general(q_ref[h], k_ref[h], _DIMS["nt"], preferred_element_type=F32) * MLA_SCALE
                if diagonal:
                    sc = jnp.where(_diag_mask(t), sc, -jnp.inf)
                m_old = m_ref[h]
                m_new = jnp.maximum(m_old, jnp.max(sc, axis=-1, keepdims=True))
                alpha = jnp.exp(m_old - m_new)
                p = jnp.exp(sc - m_new)
                l_ref[h] = alpha * l_ref[h] + jnp.sum(p, axis=-1, keepdims=True)
                acc_ref[h] = alpha * acc_ref[h] + lax.dot_general(p.astype(BF16), v_ref[h], _DIMS["nn"],
                                                                  preferred_element_type=F32)
                m_ref[h] = m_new

        @pl.when(j < i)
        def _():
            step(False)

        @pl.when(j == i)
        def _():
            step(True)
            for h in range(hb):
                o = acc_ref[h] / l_ref[h]
                o_ref[:, h * VDIM:(h + 1) * VDIM] = o
                oh_ref[:, h * VDIM:(h + 1) * VDIM] = o.astype(BF16)
                lse_ref[h] = m_ref[h] + jnp.log(l_ref[h])

    kv = lambda w: pl.BlockSpec((hb, t, w), lambda h, p, u: (h, fold(p, u)[1], 0))
    o_spec = pl.BlockSpec((t, hb * VDIM), lambda h, p, u: (fold(p, u)[0], h))
    return _pcall(
        body, comm=comm, name="mla_fwd", grid=(MLA_H // hb, nt // 2, nt + 1),
        in_specs=[pl.BlockSpec((hb, t, QW), lambda h, p, u: (h, fold(p, u)[0], 0)), kv(QW), kv(VDIM)],
        out_specs=[o_spec, o_spec, pl.BlockSpec((hb, t, 1), lambda h, p, u: (h, fold(p, u)[0], 0))],
        out_shape=[jax.ShapeDtypeStruct((s, MLA_H * VDIM), F32), jax.ShapeDtypeStruct((s, MLA_H * VDIM), BF16),
                   jax.ShapeDtypeStruct((MLA_H, s, 1), F32)],
        scratch_shapes=[pltpu.VMEM((hb, t, 1), F32), pltpu.VMEM((hb, t, 1), F32), pltpu.VMEM((hb, t, VDIM), F32)],
        compiler_params=_cparams(("parallel", "parallel", "arbitrary")))(q, k, v)


def _mla_bwd(q, k, v, dmix, o, lse, comm=None):
    s = q.shape[1]
    t = _pick(s, ATT_T, 128)
    nt = s // t
    assert nt % 2 == 0
    hb = MLA_HB
    o_blk0 = SWA_HEADS * SWA_DH // (hb * VDIM)

    def fold(p, u):
        first = u < nt - p
        return jnp.where(first, p, nt - 1 - p), jnp.where(first, p + u, u - 1)

    def body(q_ref, k_ref, v_ref, do_ref, o_ref, lse_ref, dq_ref, dk_ref, dv_ref, dk_acc, dv_acc):
        j, i = fold(pl.program_id(1), pl.program_id(2))
        rows = pl.ds(pl.multiple_of(i * t, t), t)

        @pl.when(i == j)
        def _():
            dk_acc[...] = jnp.zeros_like(dk_acc)
            dv_acc[...] = jnp.zeros_like(dv_acc)

        def step(diagonal):
            for h in range(hb):
                qv, kv_ = q_ref[h], k_ref[h]
                dov = do_ref[:, h * VDIM:(h + 1) * VDIM]
                sc = lax.dot_general(qv, kv_, _DIMS["nt"], preferred_element_type=F32) * MLA_SCALE
                p = jnp.exp(sc - lse_ref[h])
                if diagonal:
                    p = jnp.where(_diag_mask(t), p, 0.0)
                dob = dov.astype(BF16)
                dp = lax.dot_general(dob, v_ref[h], _DIMS["nt"], preferred_element_type=F32)
                delta = jnp.sum(dov * o_ref[:, h * VDIM:(h + 1) * VDIM], axis=-1, keepdims=True)
                ds = (p * (dp - delta) * MLA_SCALE).astype(BF16)
                dv_acc[h] += lax.dot_general(p.astype(BF16), dob, _DIMS["tn"], preferred_element_type=F32)
                dk_acc[h] += lax.dot_general(ds, qv, _DIMS["tn"], preferred_element_type=F32)
                dqv = lax.dot_general(ds, kv_, _DIMS["nn"], preferred_element_type=F32)

                @pl.when(j == 0)
                def _():
                    dq_ref[h, rows, :] = dqv

                @pl.when(j > 0)
                def _():
                    dq_ref[h, rows, :] += dqv

        @pl.when(i > j)
        def _():
            step(False)

        @pl.when(i == j)
        def _():
            step(True)

        @pl.when(i == nt - 1)
        def _():
            dk_ref[...] = dk_acc[...]
            dv_ref[...] = dv_acc[...]

    qi = lambda h, p, u: (h, fold(p, u)[1], 0)
    kj = lambda h, p, u: (h, fold(p, u)[0], 0)
    return _pcall(
        body, comm=comm, name="mla_bwd", grid=(MLA_H // hb, nt // 2, nt + 1),
        in_specs=[pl.BlockSpec((hb, t, QW), qi), pl.BlockSpec((hb, t, QW), kj), pl.BlockSpec((hb, t, VDIM), kj),
                  pl.BlockSpec((t, hb * VDIM), lambda h, p, u: (fold(p, u)[1], o_blk0 + h)),
                  pl.BlockSpec((t, hb * VDIM), lambda h, p, u: (fold(p, u)[1], h)),
                  pl.BlockSpec((hb, t, 1), qi)],
        out_specs=[pl.BlockSpec((hb, s, QW), lambda h, p, u: (h, 0, 0)), pl.BlockSpec((hb, t, QW), kj),
                   pl.BlockSpec((hb, t, VDIM), kj)],
        out_shape=[jax.ShapeDtypeStruct((MLA_H, s, QW), F32), jax.ShapeDtypeStruct((MLA_H, s, QW), F32),
                   jax.ShapeDtypeStruct((MLA_H, s, VDIM), F32)],
        scratch_shapes=[pltpu.VMEM((hb, t, QW), F32), pltpu.VMEM((hb, t, VDIM), F32)],
        compiler_params=_cparams(("arbitrary", "arbitrary", "arbitrary")))(q, k, v, dmix, o, lse)


def _adamw(name, w, g, m, v, parts):
    r, c = w.shape
    n_parts = g.shape[0] if parts else 1
    tr = r if r * c <= ADAM_ELEMS else _pick(r, max(8, ADAM_ELEMS // c // 8 * 8), 8)
    c1 = 1.0 - ADAM_B1 ** ADAM_STEP
    c2 = 1.0 - ADAM_B2 ** ADAM_STEP

    def body(w_ref, g_ref, m_ref, v_ref, go_ref, d_ref, mo_ref, vo_ref):
        if parts:
            gv = g_ref[0].astype(F32)
            for j in range(1, n_parts):
                gv = gv + g_ref[j].astype(F32)
        else:
            gv = g_ref[...]
        mv = ADAM_B1 * m_ref[...] + (1.0 - ADAM_B1) * gv
        vv = ADAM_B2 * v_ref[...] + (1.0 - ADAM_B2) * (gv * gv)
        go_ref[...] = gv
        mo_ref[...] = mv
        vo_ref[...] = vv
        d_ref[...] = -ADAM_LR * ((mv / c1) / (jnp.sqrt(vv / c2) + ADAM_EPS) + ADAM_WD * w_ref[...])

    blk = pl.BlockSpec((tr, c), lambda i: (i, 0))
    g_spec = pl.BlockSpec((n_parts, tr, c), lambda i: (0, i, 0)) if parts else blk
    out = jax.ShapeDtypeStruct((r, c), F32)
    return _pcall(body, name=name, grid=(r // tr,), in_specs=[blk, g_spec, blk, blk], out_specs=[blk] * 4,
                  out_shape=[out] * 4, compiler_params=_cparams(("parallel",)))(w, g, m, v)


def _t5_bucket(dist):
    n = jnp.maximum(dist, 0)
    max_exact = REL_BUCKETS // 2
    nf = jnp.maximum(n, 1).astype(F32)
    large = max_exact + (jnp.log(nf / max_exact) / math.log(REL_MAX_DIST / max_exact)
                         * (REL_BUCKETS - max_exact)).astype(jnp.int32)
    return jnp.where(n < max_exact, n, jnp.minimum(large, REL_BUCKETS - 1))


def _swap_halves(w, r0):
    return jnp.concatenate([w[:, r0 + ROPE // 2:r0 + ROPE], w[:, r0:r0 + ROPE // 2]], axis=1)


def _fold_swapped(g, r0, width):
    sw = g[..., width:width + ROPE]
    half = ROPE // 2
    return jnp.concatenate([g[..., :r0], g[..., r0:r0 + half] + sw[..., half:], g[..., r0 + half:r0 + ROPE] + sw[..., :half],
                            g[..., r0 + ROPE:width]], axis=-1)


def _heads_major(a, heads):
    s = a.shape[0]
    return a.reshape(s, heads, SWA_DH).transpose(1, 0, 2)


def _tokens_major(a):
    h, s, d = a.shape
    return a.transpose(1, 0, 2).reshape(s, h * d)


def kernel(x, c, w_mod, b_mod, attn_norm_g, w_in, swa_sinks, rel_bias, mla_q_norm_g, w_uq, mla_kv_norm_g, w_ukv, w_out, mlp_norm_g, w_ff1, w_ff2, final_norm_g, loss_target, m_w_mod, m_b_mod, m_attn_norm_g, m_w_in, m_swa_sinks, m_rel_bias, m_mla_q_norm_g, m_w_uq, m_mla_kv_norm_g, m_w_ukv, m_w_out, m_mlp_norm_g, m_w_ff1, m_w_ff2, m_final_norm_g, v_w_mod, v_b_mod, v_attn_norm_g, v_w_in, v_swa_sinks, v_rel_bias, v_mla_q_norm_g, v_w_uq, v_mla_kv_norm_g, v_w_ukv, v_w_out, v_mlp_norm_g, v_w_ff1, v_w_ff2, v_final_norm_g):
    s, d = x.shape[1], x.shape[2]
    ffs = w_ff1.shape[2]
    ff = ffs * NDEV
    nmod = w_mod.shape[2]
    me = 4 * lax.axis_index("x") + 2 * lax.axis_index("y") + lax.axis_index("c")
    x2d, tgt = x[0], loss_target[0]
    final_g = final_norm_g.reshape(1, d)

    w_in_l = jnp.concatenate([w_in[0], _swap_halves(w_in[0], OFF_KR)], axis=1).astype(BF16)
    w_uq_l = jnp.concatenate([w_uq[0], _swap_halves(w_uq[0], NOPE)], axis=1).astype(BF16)
    core = jnp.full((1, 128), lax.axis_index("c"), F32)
    (c_all,) = _exchange("gather_c", _Gather([c]))

    b_cols = lax.dynamic_slice(b_mod, (0, me * nmod), (1, nmod))
    act_all, mod_cols, w_in_g, w_uq_g, w_ukv_g = _mod_fwd(
        c_all.reshape(NDEV, d), w_mod[0], b_cols, comm=[_Gather([w_in_l, w_uq_l, w_ukv[0].astype(BF16)])])
    w_in_e = w_in_g.reshape(d, IN_EXT)
    (mod_g,) = _exchange("gather_mod", _Gather([mod_cols]))
    mod = lax.dynamic_index_in_dim(mod_g, me, axis=1, keepdims=False).reshape(1, 6 * d)
    sh1, sc1, g1, sh2, sc2, g2 = [mod[:, i * d:(i + 1) * d] for i in range(6)]

    pos = jnp.arange(s, dtype=F32)
    inv_freq = ROPE_THETA ** (-jnp.arange(ROPE // 2, dtype=F32) / (ROPE // 2))
    ang = pos[:, None] * inv_freq[None, :]
    cos, sin = jnp.cos(ang), jnp.sin(ang)
    table = jnp.concatenate([cos, cos, -sin, sin], axis=1)
    q_loc = jnp.arange(BLOCK)[:, None]
    k_loc = jnp.arange(2 * BLOCK)[None, :]
    dist = q_loc + BLOCK - k_loc
    in_window = (dist >= 0) & (dist < BLOCK)
    onehot = (_t5_bucket(dist).reshape(-1, 1) == jnp.arange(REL_BUCKETS)[None, :]).astype(BF16)
    bias = _bias_expand(rel_bias.T, onehot.T).reshape(SWA_HEADS, BLOCK, 2 * BLOCK)
    bias = jnp.where(in_window[None], bias, -jnp.inf).reshape(SWA_KV, SWA_GROUP * BLOCK, 2 * BLOCK)
    sink_rows = jnp.broadcast_to(swa_sinks.reshape(SWA_HEADS, 1), (SWA_HEADS, BLOCK)).reshape(SWA_KV, SWA_GROUP * BLOCK, 1)

    h1 = _norm_mod("norm1", x2d, attn_norm_g, sc1, sh1)
    def both_dtypes(acc, ex, outs):
        outs[0][...] = acc
        outs[1][...] = acc.astype(BF16)

    tmp = _pick(s, MM_TM // 2, 128)
    proj_blk = pl.BlockSpec((tmp, IN_EXT), lambda i, j, q: (i, 0))
    proj, proj_h = _mm("proj", h1, w_in_e, "nn", (s // tmp, 1, 1), pl.BlockSpec((tmp, d), lambda i, j, q: (i, 0)),
                       pl.BlockSpec((d, IN_EXT), lambda i, j, q: (0, 0)),
                       [jax.ShapeDtypeStruct((s, IN_EXT), F32), jax.ShapeDtypeStruct((s, IN_EXT), BF16)],
                       [proj_blk, proj_blk], (tmp, IN_EXT), both_dtypes)
    q_a = _heads_major(proj_h[:, :OFF_K], SWA_HEADS).reshape(SWA_KV, SWA_GROUP, s, SWA_DH)
    k_a = _heads_major(proj_h[:, OFF_K:OFF_V], SWA_KV)
    v_a = _heads_major(proj_h[:, OFF_V:OFF_CQ], SWA_KV)
    o_a, lse_a, w_out_g = _swa_fwd(q_a, k_a, v_a, bias, sink_rows, comm=[_Gather([w_out[0].astype(BF16)])])
    w_out_f = w_out_g.reshape(MIX, d)

    cq, ckv, kr = _mla_prep(proj, mla_q_norm_g, mla_kv_norm_g, table)
    q_b, k_b, v_b = _mla_qkv(cq, ckv, kr, w_uq_g, w_ukv_g, table)
    o_b, o_bh, lse_b, w_ff1_g, w_ff2_g = _mla_fwd(
        q_b, k_b, v_b, comm=[_Gather([w_ff1[0].astype(BF16), w_ff2[0].astype(BF16)])])
    mix = jnp.concatenate([_tokens_major(o_a.reshape(SWA_HEADS, s, SWA_DH)), o_bh], axis=1)

    tm, tn, tk = _pick(s, MM_TM, 128), _pick(d, MM_TN, 128), _pick(MIX, MM_TK, 128)
    row_blk = pl.BlockSpec((tm, tn), lambda i, j, q: (i, j))
    gate_blk = pl.BlockSpec((1, tn), lambda i, j, q: (0, j))

    def gated_residual(acc, ex, outs):
        outs[0][...] = acc
        outs[1][...] = ex[0][...] + ex[1][...] * acc

    y1, x2 = _mm("out_proj", mix, w_out_f, "nn", (s // tm, d // tn, MIX // tk),
                 pl.BlockSpec((tm, tk), lambda i, j, q: (i, q)), pl.BlockSpec((tk, tn), lambda i, j, q: (q, j)),
                 [jax.ShapeDtypeStruct((s, d), F32)] * 2, [row_blk, row_blk], (tm, tn), gated_residual,
                 extras=(x2d, g1), extra_specs=(row_blk, gate_blk))

    h2 = _norm_mod("norm2", x2, mlp_norm_g, sc2, sh2)
    tnf, tkd = _pick(ffs, MM_TN, 128), _pick(d, MM_TK, 128)
    rf = ffs // tnf
    ff_blk = pl.BlockSpec((tm, tnf), lambda i, j, q: (i, j))

    def relu_sq(acc, ex, outs):
        u = jnp.maximum(acc, 0.0)
        outs[0][...] = u
        outs[1][...] = (u * u).astype(BF16)

    u, uu = _mm("ff1", h2, w_ff1_g, "nn", (s // tm, ff // tnf, d // tkd),
                pl.BlockSpec((tm, tkd), lambda i, j, q: (i, q)),
                pl.BlockSpec((None, tkd, tnf), lambda i, j, q: (j // rf, q, j % rf)),
                [jax.ShapeDtypeStruct((s, ff), F32), jax.ShapeDtypeStruct((s, ff), BF16)], [ff_blk, ff_blk],
                (tm, tnf), relu_sq)
    w_ff2_f = w_ff2_g.reshape(ff, d)
    tkf = _pick(ff, MM_TK, 128)
    y2, x3 = _mm("ff2", uu, w_ff2_f, "nn", (s // tm, d // tn, ff // tkf),
                 pl.BlockSpec((tm, tkf), lambda i, j, q: (i, q)), pl.BlockSpec((tkf, tn), lambda i, j, q: (q, j)),
                 [jax.ShapeDtypeStruct((s, d), F32)] * 2, [row_blk, row_blk], (tm, tn), gated_residual,
                 extras=(x2, g2), extra_specs=(row_blk, gate_blk))

    dx3, dy2, loss_p, dgf, dg2 = _loss_head(x3, tgt, y2, final_g, g2)
    loss = lax.psum(loss_p[0, 0], ("x", "y", "c"))

    def relu_sq_bwd(acc, ex, outs):
        outs[0][...] = (acc * (2.0 * ex[0][...])).astype(BF16)

    tnf2 = _pick(ff, MM_TN, 128)
    du = _mm("ff2_dx", dy2, w_ff2_f, "nt", (s // tm, ff // tnf2, d // tkd),
             pl.BlockSpec((tm, tkd), lambda i, j, q: (i, q)), pl.BlockSpec((tnf2, tkd), lambda i, j, q: (j, q)),
             [jax.ShapeDtypeStruct((s, ff), BF16)], [pl.BlockSpec((tm, tnf2), lambda i, j, q: (i, j))],
             (tm, tnf2), relu_sq_bwd, extras=(u,), extra_specs=(pl.BlockSpec((tm, tnf2), lambda i, j, q: (i, j)),))[0]
    gw_ff2 = _mm_plain("ff2_dw", uu, dy2, "tn", ff, d, s, BF16)
    tmd, tks = _pick(d, MM_TM, 128), _pick(s, MM_TK, 128)
    gw_ff2 = gw_ff2.reshape(NDEV, ffs, d)
    gw_ff1, s_ff2 = _mm("ff1_dw", h2, du, "tn", (d // tmd, ff // tnf, s // tks),
                        pl.BlockSpec((tks, tmd), lambda i, j, q: (q, i)), pl.BlockSpec((tks, tnf), lambda i, j, q: (q, j)),
                        [jax.ShapeDtypeStruct((NDEV, d, ffs), BF16)],
                        [pl.BlockSpec((None, tmd, tnf), lambda i, j, q: (j // rf, i, j % rf))], (tmd, tnf), _store(BF16),
                        comm=[_PairSwap([gw_ff2])])
    c_ff2 = _pair_sum("pair_ff2", gw_ff2, s_ff2, core)
    tkf1 = _pick(ffs, MM_TK, 128)
    rk = ffs // tkf1
    dh2, p_ff2, s_ff1 = _mm("ff1_dx", du, w_ff1_g, "nt", (s // tm, d // tn, ff // tkf1),
                            pl.BlockSpec((tm, tkf1), lambda i, j, q: (i, q)),
                            pl.BlockSpec((None, tn, tkf1), lambda i, j, q: (q // rk, j, q % rk)),
                            [jax.ShapeDtypeStruct((s, d), F32)], [row_blk], (tm, tn), _store(F32),
                            comm=[_ChipScatter([c_ff2]), _PairSwap([gw_ff1])])
    c_ff1 = _pair_sum("pair_ff1", gw_ff1, s_ff1, core)
    dx2, dy1, dsc2, dsh2, dgm, dg1 = _norm_mod_bwd("norm2_bwd", x2, dh2, dx3, mlp_norm_g, sc2, y1, g1)

    dmix = _mm_plain("out_proj_dx", dy1, w_out_f, "nt", s, MIX, d, F32)
    gw_out = _mm_plain("out_proj_dw", mix, dy1, "tn", MIX, d, s, BF16).reshape(NDEV, MIX // NDEV, d)

    dq_b, dk_b, dv_b, p_ff1, s_out = _mla_bwd(q_b, k_b, v_b, dmix, o_b, lse_b,
                                              comm=[_ChipScatter([c_ff1]), _PairSwap([gw_out])])
    c_out = _pair_sum("pair_out", gw_out, s_out, core)
    dcq, dckv, dkr, gw_uq_e, gw_ukv, p_out = _mla_qkv_bwd(dq_b, dk_b, dv_b, cq, ckv, w_uq_g, w_ukv_g, table,
                                                          comm=[_ChipScatter([c_out])])
    dtail, dgq, dgkv = _mla_prep_bwd(proj, dcq, dckv, dkr, mla_q_norm_g, mla_kv_norm_g, table)
    gw_uq = _fold_swapped(gw_uq_e, NOPE, NOPE + ROPE).astype(BF16)
    gw_ukv = gw_ukv.astype(BF16)

    do_a = _heads_major(dmix[:, :OFF_K].astype(BF16), SWA_HEADS).reshape(SWA_KV, SWA_GROUP, s, SWA_DH)
    dq_a, dkp, dkc, dvp, dvc, dbias, dsink, s_uq, s_ukv = _swa_bwd(
        q_a, k_a, v_a, do_a, lse_a, bias, sink_rows, comm=[_PairSwap([gw_uq, gw_ukv])])
    c_uq = _pair_sum("pair_uq", gw_uq, s_uq, core)
    c_ukv = _pair_sum("pair_ukv", gw_ukv, s_ukv, core)
    shift = lambda p: jnp.concatenate([p[:, BLOCK:], jnp.zeros_like(p[:, :BLOCK])], axis=1)
    dk_a, dv_a = dkc + shift(dkp), dvc + shift(dvp)
    drel_t, dsinks = _bias_reduce(dbias.reshape(SWA_HEADS, BLOCK * 2 * BLOCK), onehot, dsink.reshape(SWA_HEADS, BLOCK))
    dproj = jnp.concatenate([_tokens_major(dq_a.reshape(SWA_HEADS, s, SWA_DH)),
                             _tokens_major(dk_a).astype(BF16), _tokens_major(dv_a).astype(BF16), dtail], axis=1)
    gw_in_e = _mm_plain("proj_dw", h1, dproj, "tn", d, IN_EXT, s, F32, tn=TAIL)
    gw_in = _fold_swapped(gw_in_e, OFF_KR, IN_COLS).reshape(NDEV, d // NDEV, IN_COLS).astype(BF16)
    tkt = IN_EXT
    dh1, s_in, p_uq, p_ukv = _mm(
        "proj_dx", dproj, w_in_e, "nt", (s // tm, d // tn, IN_EXT // tkt),
        pl.BlockSpec((tm, tkt), lambda i, j, q: (i, q)), pl.BlockSpec((tn, tkt), lambda i, j, q: (j, q)),
        [jax.ShapeDtypeStruct((s, d), F32)], [row_blk], (tm, tn), _store(F32),
        comm=[_PairSwap([gw_in]), _ChipScatter([c_uq, c_ukv])])
    c_in = _pair_sum("pair_in", gw_in, s_in, core)
    gx, dsc1, dsh1, dga, p_in = _norm_mod_bwd("norm1_bwd", x2d, dh1, dx2, attn_norm_g, sc1,
                                              comm=[_ChipScatter([c_in])])

    small = [jnp.concatenate([dsh1, dsc1, dg1, dsh2, dsc2, dg2], axis=1), dga, dgm, dgf, dgq, dgkv,
             dsinks.reshape(1, SWA_HEADS), drel_t.T.reshape(1, REL_BUCKETS * SWA_HEADS)]
    n_small = sum(a.shape[1] for a in small)
    n_pad = -n_small % 1024
    rows_small = (n_small + n_pad) // 128
    pad = jnp.zeros((1, n_pad), F32)
    pack = lambda parts: jnp.concatenate([p.reshape(1, -1) for p in parts] + [pad], axis=1).reshape(rows_small, 128)
    (small_g,) = _exchange("gather_small", _Gather([pack(small)]))
    small_names = (b_mod, attn_norm_g, mlp_norm_g, final_norm_g, mla_q_norm_g, mla_kv_norm_g, swa_sinks, rel_bias)
    small_m = (m_b_mod, m_attn_norm_g, m_mlp_norm_g, m_final_norm_g, m_mla_q_norm_g, m_mla_kv_norm_g, m_swa_sinks, m_rel_bias)
    small_v = (v_b_mod, v_attn_norm_g, v_mlp_norm_g, v_final_norm_g, v_mla_q_norm_g, v_mla_kv_norm_g, v_swa_sinks, v_rel_bias)
    small_out = _adamw("adamw_small", pack(small_names), small_g, pack(small_m), pack(small_v), parts=True)

    def unpack(flat):
        flat = flat.reshape(1, -1)
        out, off = [], 0
        for a in small_names:
            out.append(flat[:, off:off + a.size].reshape(a.shape))
            off += a.size
        return out

    sg, sd, sm, sv = [unpack(o) for o in small_out]

    dmod_cols = lax.dynamic_slice(small_g.reshape(NDEV, -1), (0, me * nmod), (NDEV, nmod))
    gw_mod = _mod_wgrad(act_all, dmod_cols)
    big = {"w_mod": _adamw("adamw_w_mod", w_mod[0], gw_mod, m_w_mod[0], v_w_mod[0], parts=False)}

    for name, w, p, m, v in (("w_in", w_in, p_in, m_w_in, v_w_in), ("w_uq", w_uq, p_uq, m_w_uq, v_w_uq),
                             ("w_ukv", w_ukv, p_ukv, m_w_ukv, v_w_ukv), ("w_out", w_out, p_out, m_w_out, v_w_out),
                             ("w_ff1", w_ff1, p_ff1, m_w_ff1, v_w_ff1), ("w_ff2", w_ff2, p_ff2, m_w_ff2, v_w_ff2)):
        big[name] = _adamw("adamw_" + name, w[0], p, m[0], v[0], parts=True)

    order = ("w_mod", "b_mod", "attn_norm_g", "w_in", "swa_sinks", "rel_bias", "mla_q_norm_g", "w_uq", "mla_kv_norm_g",
             "w_ukv", "w_out", "mlp_norm_g", "w_ff1", "w_ff2", "final_norm_g")
    small_idx = {"b_mod": 0, "attn_norm_g": 1, "mlp_norm_g": 2, "final_norm_g": 3, "mla_q_norm_g": 4,
                 "mla_kv_norm_g": 5, "swa_sinks": 6, "rel_bias": 7}
    outs = []
    for kind, small_list in enumerate((sg, sd, sm, sv)):
        for name in order:
            outs.append(small_list[small_idx[name]] if name in small_idx else big[name][kind][None])
    return (loss, gx[None], *outs)
```

```python
import functools
import math

import jax
import jax.numpy as jnp
from jax import lax
from jax.experimental import pallas as pl
from jax.experimental.pallas import tpu as pltpu

F32 = jnp.float32
BF16 = jnp.bfloat16

NDEV = 8
EPS = 1e-6
BLOCK = 128
SWA_HEADS, SWA_KV, SWA_DH, SWA_GROUP = 16, 2, 64, 8
REL_BUCKETS, REL_MAX_DIST = 32, 128
MLA_H, Q_RANK, KV_RANK, NOPE, ROPE, VDIM = 8, 384, 128, 128, 64, 128
ROPE_THETA = 10000.0
OFF_K, OFF_V, OFF_CQ, OFF_CKV, OFF_KR, IN_COLS = 1024, 1152, 1280, 1664, 1792, 1856
IN_EXT = IN_COLS + ROPE
TAIL0, TAIL = OFF_CQ, IN_EXT - OFF_CQ
QW = NOPE + 2 * ROPE
MIX = SWA_HEADS * SWA_DH + MLA_H * VDIM
MLA_SCALE = (NOPE + ROPE) ** -0.5
SWA_SCALE = SWA_DH ** -0.5

ADAM_LR, ADAM_B1, ADAM_B2, ADAM_EPS, ADAM_WD, ADAM_STEP = 0.001, 0.9, 0.999, 1e-08, 0.01, 10

VMEM_LIMIT = 52 * 1024 * 1024
ROW_TILE = 256
MM_TM, MM_TN, MM_TK = 1024, 1024, 2048
ATT_T = 512
MLA_HB = 2
ADAM_ELEMS = 128 * 1024


MESH_ID = pl.DeviceIdType.MESH


def _place():
    x, y, c = lax.axis_index("x"), lax.axis_index("y"), lax.axis_index("c")
    return x, y, c, 2 * x + y


def _chip(x, y, k):
    return (1 - x if k & 2 else x, 1 - y if k & 1 else y)


def _dma_sems(*counts):
    return [pltpu.SemaphoreType.DMA((n,)) for n in counts]


class _Gather:
    def __init__(self, arrays):
        self.arrays = list(arrays)
        n = len(self.arrays)
        self.out_shape = [jax.ShapeDtypeStruct((NDEV,) + a.shape, a.dtype) for a in self.arrays]
        self.sems = _dma_sems(7 * n, 7 * n, n)

    def _copy(self, sems, a, k, src, dst, to):
        return pltpu.make_async_remote_copy(src_ref=src, dst_ref=dst, send_sem=sems[0].at[7 * a + k],
                                            recv_sem=sems[1].at[7 * a + k], device_id=to, device_id_type=MESH_ID)

    def start(self, ins, outs, sems):
        x, y, c, q = _place()
        me = 2 * q + c
        for a in range(len(ins)):
            pltpu.make_async_copy(ins[a], outs[a].at[me], sems[2].at[a]).start()
            self._copy(sems, a, 0, ins[a], outs[a].at[me], (x, y, 1 - c)).start()
            for k in (1, 2, 3):
                self._copy(sems, a, k, ins[a], outs[a].at[me], (*_chip(x, y, k), c)).start()

    def finish(self, ins, outs, sems):
        x, y, c, q = _place()
        me, sib = 2 * q + c, (x, y, 1 - c)
        n = len(ins)
        for k in (1, 2, 3):
            for a in range(n):
                blk = outs[a].at[2 * (q ^ k) + c]
                self._copy(sems, a, k, ins[a], blk, (*_chip(x, y, k), c)).wait_recv()
                self._copy(sems, a, 3 + k, blk, blk, sib).start()
        for a in range(n):
            self._copy(sems, a, 0, ins[a], outs[a].at[2 * q + 1 - c], sib).wait_recv()
            for k in (1, 2, 3):
                blk = outs[a].at[2 * (q ^ k) + 1 - c]
                self._copy(sems, a, 3 + k, blk, blk, sib).wait_recv()
        for a in range(n):
            for k in range(7):
                self._copy(sems, a, k, ins[a], outs[a].at[me], sib).wait_send()
            pltpu.make_async_copy(ins[a], outs[a].at[me], sems[2].at[a]).wait()


class _PairSwap:
    def __init__(self, arrays):
        self.arrays = list(arrays)
        n = len(self.arrays)
        self.out_shape = [jax.ShapeDtypeStruct((NDEV // 2,) + a.shape[1:], a.dtype) for a in self.arrays]
        self.sems = _dma_sems(4 * n, 4 * n)

    def _copy(self, sems, a, p, src, dst, to):
        return pltpu.make_async_remote_copy(src_ref=src, dst_ref=dst, send_sem=sems[0].at[4 * a + p],
                                            recv_sem=sems[1].at[4 * a + p], device_id=to, device_id_type=MESH_ID)

    def start(self, ins, outs, sems):
        x, y, c, _ = _place()
        for a in range(len(ins)):
            for p in range(4):
                self._copy(sems, a, p, ins[a].at[2 * p + 1 - c], outs[a].at[p], (x, y, 1 - c)).start()

    def finish(self, ins, outs, sems):
        x, y, c, _ = _place()
        for a in range(len(ins)):
            for p in range(4):
                cp = self._copy(sems, a, p, ins[a].at[2 * p + 1 - c], outs[a].at[p], (x, y, 1 - c))
                cp.wait_recv()
                cp.wait_send()


class _ChipScatter:
    def __init__(self, arrays):
        self.arrays = list(arrays)
        n = len(self.arrays)
        self.out_shape = [jax.ShapeDtypeStruct(a.shape, a.dtype) for a in self.arrays]
        self.sems = _dma_sems(3 * n, 3 * n, n)

    def _copy(self, sems, a, k, src, dst, to):
        return pltpu.make_async_remote_copy(src_ref=src, dst_ref=dst, send_sem=sems[0].at[3 * a + k - 1],
                                            recv_sem=sems[1].at[3 * a + k - 1], device_id=to, device_id_type=MESH_ID)

    def start(self, ins, outs, sems):
        x, y, c, q = _place()
        for a in range(len(ins)):
            pltpu.make_async_copy(ins[a].at[q], outs[a].at[q], sems[2].at[a]).start()
            for k in (1, 2, 3):
                self._copy(sems, a, k, ins[a].at[q ^ k], outs[a].at[q], (*_chip(x, y, k), c)).start()

    def finish(self, ins, outs, sems):
        x, y, c, q = _place()
        for a in range(len(ins)):
            for k in (1, 2, 3):
                cp = self._copy(sems, a, k, ins[a].at[q ^ k], outs[a].at[q ^ k], (*_chip(x, y, k), c))
                cp.wait_recv()
                cp.wait_send()
            pltpu.make_async_copy(ins[a].at[q], outs[a].at[q], sems[2].at[a]).wait()


def _call(body, **kw):
    return pl.pallas_call(body, **kw)


def _pcall(body, comm=None, **kw):
    if not comm:
        return _call(body, **kw)
    grid = kw["grid"]
    in_specs, out_specs, out_shape = list(kw["in_specs"]), list(kw["out_specs"]), list(kw["out_shape"])
    scratch = list(kw.get("scratch_shapes", ()))
    n_in, n_out, n_scr = len(in_specs), len(out_shape), len(scratch)
    n_cin = [len(j.arrays) for j in comm]
    n_sem = [len(j.sems) for j in comm]
    n = sum(n_cin)
    hbm = pl.BlockSpec(memory_space=pltpu.HBM)

    def carried(*refs):
        ins, cins = refs[:n_in], refs[n_in:n_in + n]
        outs, couts = refs[n_in + n:n_in + n + n_out], refs[n_in + n + n_out:n_in + 2 * n + n_out]
        scr, sems = refs[n_in + 2 * n + n_out:n_in + 2 * n + n_out + n_scr], refs[n_in + 2 * n + n_out + n_scr:]
        ids = [pl.program_id(ax) for ax in range(len(grid))]
        first = functools.reduce(jnp.logical_and, [i == 0 for i in ids])
        last = functools.reduce(jnp.logical_and, [i == g - 1 for i, g in zip(ids, grid)])

        def each(method):
            ai = si = 0
            for job, na, ns in zip(comm, n_cin, n_sem):
                getattr(job, method)(cins[ai:ai + na], couts[ai:ai + na], sems[si:si + ns])
                ai, si = ai + na, si + ns

        @pl.when(first)
        def _():
            each("start")

        body(*ins, *outs, *scr)

        @pl.when(last)
        def _():
            each("finish")

    kw.update(in_specs=in_specs + [hbm] * n, out_specs=out_specs + [hbm] * n,
              out_shape=out_shape + [o for j in comm for o in j.out_shape],
              scratch_shapes=scratch + [sm for j in comm for sm in j.sems],
              compiler_params=_cparams(("arbitrary",) * len(grid)))
    call = _call(carried, **kw)
    return lambda *args: call(*args, *[a for j in comm for a in j.arrays])


def _cparams(sem):
    return pltpu.CompilerParams(dimension_semantics=sem, vmem_limit_bytes=VMEM_LIMIT)


def _pick(n, pref, align):
    if n <= pref:
        return n
    t = (pref // align) * align
    while t >= align:
        if n % t == 0:
            return t
        t -= align
    return n


def _split3(x):
    a = x.astype(BF16)
    r = x - a.astype(F32)
    b = r.astype(BF16)
    c = (r - b.astype(F32)).astype(BF16)
    return a, b, c


def _exchange(name, job):
    n = len(job.arrays)

    def body(*refs):
        ins, outs, sems = refs[:n], refs[n:2 * n], refs[2 * n:]
        job.start(ins, outs, sems)
        job.finish(ins, outs, sems)

    hbm = pl.BlockSpec(memory_space=pltpu.HBM)
    return _call(body, name=name, out_shape=job.out_shape, in_specs=[hbm] * n, out_specs=[hbm] * n,
                 scratch_shapes=job.sems)(*job.arrays)


def _pair_sum(name, g, r, core):
    _, rr, cc = g.shape
    tr = rr if rr * cc <= 4 * ADAM_ELEMS else _pick(rr, max(16, 4 * ADAM_ELEMS // cc // 16 * 16), 16)

    def body(g_ref, r_ref, c_ref, o_ref):
        north = c_ref[:, 0:1] > 0.5
        mine = jnp.where(north, g_ref[1].astype(F32), g_ref[0].astype(F32))
        o_ref[...] = (mine + r_ref[...].astype(F32)).astype(o_ref.dtype)

    return _pcall(
        body, name=name, grid=(NDEV // 2, rr // tr),
        in_specs=[pl.BlockSpec((None, 2, tr, cc), lambda p, i: (p, 0, i, 0)),
                  pl.BlockSpec((None, tr, cc), lambda p, i: (p, i, 0)), pl.BlockSpec((1, 128), lambda p, i: (0, 0))],
        out_specs=pl.BlockSpec((None, tr, cc), lambda p, i: (p, i, 0)),
        out_shape=jax.ShapeDtypeStruct((NDEV // 2, rr, cc), g.dtype),
        compiler_params=_cparams(("parallel", "parallel")))(g.reshape(NDEV // 2, 2, rr, cc), r, core)


_DIMS = {"nn": (((1,), (0,)), ((), ())), "nt": (((1,), (1,)), ((), ())), "tn": (((0,), (0,)), ((), ()))}


def _mm(name, a, b, kind, grid, a_spec, b_spec, out_shape, out_specs, acc_shape, epilogue,
        extras=(), extra_specs=(), comm=None):
    nk, ne, no = grid[2], len(extras), len(out_shape)

    def body(*refs):
        a_ref, b_ref = refs[0], refs[1]
        ex, outs = refs[2:2 + ne], refs[2 + ne:2 + ne + no]
        part = lax.dot_general(a_ref[...].astype(BF16), b_ref[...].astype(BF16), _DIMS[kind],
                               preferred_element_type=F32)
        if nk == 1:
            epilogue(part, ex, outs)
            return
        acc = refs[-1]
        k = pl.program_id(2)

        @pl.when(k == 0)
        def _():
            acc[...] = part

        @pl.when(jnp.logical_and(k > 0, k < nk - 1))
        def _():
            acc[...] += part

        @pl.when(k == nk - 1)
        def _():
            epilogue(acc[...] + part, ex, outs)

    return _pcall(
        body, comm=comm, name=name, grid=grid, in_specs=[a_spec, b_spec, *extra_specs], out_specs=out_specs,
        out_shape=out_shape, scratch_shapes=[pltpu.VMEM(acc_shape, F32)] if nk > 1 else [],
        compiler_params=_cparams(("parallel", "parallel", "arbitrary")),
    )(a, b, *extras)


def _store(dtype):
    def epi(acc, ex, outs):
        outs[0][...] = acc.astype(dtype)
    return epi


def _mm_plain(name, a, b, kind, m, n, k, out_dtype, tm=None, tn=None, tk=None):
    tm = _pick(m, tm or MM_TM, 128)
    tn = _pick(n, tn or MM_TN, 128)
    tk = _pick(k, tk or MM_TK, 128)
    a_spec = pl.BlockSpec((tk, tm), lambda i, j, q: (q, i)) if kind == "tn" else pl.BlockSpec((tm, tk), lambda i, j, q: (i, q))
    b_spec = pl.BlockSpec((tn, tk), lambda i, j, q: (j, q)) if kind == "nt" else pl.BlockSpec((tk, tn), lambda i, j, q: (q, j))
    return _mm(name, a, b, kind, (m // tm, n // tn, k // tk), a_spec, b_spec,
               [jax.ShapeDtypeStruct((m, n), out_dtype)], [pl.BlockSpec((tm, tn), lambda i, j, q: (i, j))],
               (tm, tn), _store(out_dtype))[0]


def _row(ts, d):
    return pl.BlockSpec((ts, d), lambda i: (i, 0))


def _vec(d):
    return pl.BlockSpec((1, d), lambda i: (0, 0))


def _norm_mod(name, x, gain, sc, sh):
    s, d = x.shape
    ts = _pick(s, ROW_TILE, 16)

    def body(x_ref, g_ref, sc_ref, sh_ref, h_ref):
        xv = x_ref[...]
        r = lax.rsqrt(jnp.mean(xv * xv, axis=-1, keepdims=True) + EPS)
        h_ref[...] = ((xv * r) * g_ref[...] * (1.0 + sc_ref[...]) + sh_ref[...]).astype(BF16)

    return _pcall(body, name=name, grid=(s // ts,), in_specs=[_row(ts, d), _vec(d), _vec(d), _vec(d)],
                  out_specs=_row(ts, d), out_shape=jax.ShapeDtypeStruct((s, d), BF16),
                  compiler_params=_cparams(("parallel",)))(x, gain, sc, sh)


def _loss_head(x3, tgt, y2, gf, g2):
    s, d = x3.shape
    ts = _pick(s, ROW_TILE, 16)

    def body(x_ref, t_ref, y_ref, gf_ref, g2_ref, dx_ref, dy_ref, loss_ref, dgf_ref, dg2_ref):
        @pl.when(pl.program_id(0) == 0)
        def _():
            loss_ref[...] = jnp.zeros_like(loss_ref)
            dgf_ref[...] = jnp.zeros_like(dgf_ref)
            dg2_ref[...] = jnp.zeros_like(dg2_ref)

        xv = x_ref[...]
        r = lax.rsqrt(jnp.mean(xv * xv, axis=-1, keepdims=True) + EPS)
        xn = xv * r
        err = xn * gf_ref[...] - t_ref[...]
        loss_ref[...] += 0.5 * jnp.sum(jnp.mean(err * err, axis=-1, keepdims=True), axis=0, keepdims=True)
        dout = err * (1.0 / d)
        dgf_ref[...] += jnp.sum(dout * xn, axis=0, keepdims=True)
        dxn = dout * gf_ref[...]
        dx = r * (dxn - xn * jnp.mean(dxn * xn, axis=-1, keepdims=True))
        dx_ref[...] = dx
        dy_ref[...] = (dx * g2_ref[...]).astype(BF16)
        dg2_ref[...] += jnp.sum(dx * y_ref[...], axis=0, keepdims=True)

    one = pl.BlockSpec((1, 1), lambda i: (0, 0))
    return _pcall(
        body, name="loss_head", grid=(s // ts,),
        in_specs=[_row(ts, d), _row(ts, d), _row(ts, d), _vec(d), _vec(d)],
        out_specs=[_row(ts, d), _row(ts, d), one, _vec(d), _vec(d)],
        out_shape=[jax.ShapeDtypeStruct((s, d), F32), jax.ShapeDtypeStruct((s, d), BF16),
                   jax.ShapeDtypeStruct((1, 1), F32), jax.ShapeDtypeStruct((1, d), F32),
                   jax.ShapeDtypeStruct((1, d), F32)],
        compiler_params=_cparams(("arbitrary",)))(x3, tgt, y2, gf, g2)


def _norm_mod_bwd(name, x, dh, dres, gain, sc, y_prev=None, gate=None, comm=None):
    s, d = x.shape
    ts = _pick(s, ROW_TILE, 16)
    gated = y_prev is not None

    def body(*refs):
        if gated:
            x_ref, dh_ref, dr_ref, g_ref, sc_ref, y_ref, gt_ref, dx_ref, dy_ref, dsc_ref, dsh_ref, dg_ref, dgt_ref = refs
        else:
            x_ref, dh_ref, dr_ref, g_ref, sc_ref, dx_ref, dsc_ref, dsh_ref, dg_ref = refs

        @pl.when(pl.program_id(0) == 0)
        def _():
            dsc_ref[...] = jnp.zeros_like(dsc_ref)
            dsh_ref[...] = jnp.zeros_like(dsh_ref)
            dg_ref[...] = jnp.zeros_like(dg_ref)
            if gated:
                dgt_ref[...] = jnp.zeros_like(dgt_ref)

        xv, dhv = x_ref[...], dh_ref[...]
        r = lax.rsqrt(jnp.mean(xv * xv, axis=-1, keepdims=True) + EPS)
        xn = xv * r
        dsc_ref[...] += jnp.sum(dhv * (xn * g_ref[...]), axis=0, keepdims=True)
        dsh_ref[...] += jnp.sum(dhv, axis=0, keepdims=True)
        da = dhv * (1.0 + sc_ref[...])
        dg_ref[...] += jnp.sum(da * xn, axis=0, keepdims=True)
        dxn = da * g_ref[...]
        dx = dr_ref[...] + r * (dxn - xn * jnp.mean(dxn * xn, axis=-1, keepdims=True))
        dx_ref[...] = dx
        if gated:
            dy_ref[...] = (dx * gt_ref[...]).astype(BF16)
            dgt_ref[...] += jnp.sum(dx * y_ref[...], axis=0, keepdims=True)

    ins = [x, dh, dres, gain, sc] + ([y_prev, gate] if gated else [])
    in_specs = [_row(ts, d)] * 3 + [_vec(d)] * 2 + ([_row(ts, d), _vec(d)] if gated else [])
    vec_out = jax.ShapeDtypeStruct((1, d), F32)
    out_shape = [jax.ShapeDtypeStruct((s, d), F32)] + ([jax.ShapeDtypeStruct((s, d), BF16)] if gated else [])
    out_shape += [vec_out] * (4 if gated else 3)
    out_specs = [_row(ts, d)] * (2 if gated else 1) + [_vec(d)] * (4 if gated else 3)
    return _pcall(body, comm=comm, name=name, grid=(s // ts,), in_specs=in_specs, out_specs=out_specs,
                  out_shape=out_shape, compiler_params=_cparams(("arbitrary",)))(*ins)


def _dot3(a, b, dims):
    a1, a2, _ = _split3(a)
    b1, b2, _ = _split3(b)
    dot = functools.partial(lax.dot_general, dimension_numbers=dims, preferred_element_type=F32)
    return dot(a1, b1) + (dot(a1, b2) + dot(a2, b1))


def _mod_fwd(c_all, w, b_cols, comm=None):
    nb, d = c_all.shape
    n = w.shape[1]
    tk = _pick(d, 512, 128)
    nk = d // tk

    def body(c_ref, w_ref, b_ref, act_ref, out_ref):
        k = pl.program_id(0)
        cv = c_ref[...]
        act = cv * (1.0 / (1.0 + jnp.exp(-cv)))
        act_ref[...] = act

        @pl.when(k == 0)
        def _():
            out_ref[...] = jnp.broadcast_to(b_ref[...], out_ref.shape)

        out_ref[...] += _dot3(act, w_ref[...], _DIMS["nn"])

    return _pcall(
        body, comm=comm, name="mod_fwd", grid=(nk,),
        in_specs=[pl.BlockSpec((nb, tk), lambda k: (0, k)), pl.BlockSpec((tk, n), lambda k: (k, 0)),
                  pl.BlockSpec((1, n), lambda k: (0, 0))],
        out_specs=[pl.BlockSpec((nb, tk), lambda k: (0, k)), pl.BlockSpec((nb, n), lambda k: (0, 0))],
        out_shape=[jax.ShapeDtypeStruct((nb, d), F32), jax.ShapeDtypeStruct((nb, n), F32)],
        compiler_params=_cparams(("arbitrary",)))(c_all, w, b_cols)


def _mod_wgrad(act_all, dmod_cols):
    nb, d = act_all.shape
    n = dmod_cols.shape[1]
    tm = _pick(d, 512, 128)

    def body(a_ref, d_ref, o_ref):
        o_ref[...] = _dot3(a_ref[...], d_ref[...], _DIMS["tn"])

    return _pcall(
        body, name="mod_wgrad", grid=(d // tm,),
        in_specs=[pl.BlockSpec((nb, tm), lambda i: (0, i)), pl.BlockSpec((nb, n), lambda i: (0, 0))],
        out_specs=pl.BlockSpec((tm, n), lambda i: (i, 0)), out_shape=jax.ShapeDtypeStruct((d, n), F32),
        compiler_params=_cparams(("parallel",)))(act_all, dmod_cols)


def _bias_expand(rel_t, onehot_t):
    h, _ = rel_t.shape
    n = onehot_t.shape[1]

    def body(r_ref, o_ref, out_ref):
        a, b, c = _split3(r_ref[...])
        dot = functools.partial(lax.dot_general, dimension_numbers=_DIMS["nn"], preferred_element_type=F32)
        oh = o_ref[...]
        out_ref[...] = dot(a, oh) + (dot(b, oh) + dot(c, oh))

    full = lambda shp: pl.BlockSpec(shp, lambda: (0,) * len(shp))
    return _pcall(body, name="bias_expand", in_specs=[full(rel_t.shape), full(onehot_t.shape)],
                  out_specs=full((h, n)), out_shape=jax.ShapeDtypeStruct((h, n), F32),
                  compiler_params=pltpu.CompilerParams(vmem_limit_bytes=VMEM_LIMIT))(rel_t, onehot_t)


def _bias_reduce(dbias, onehot, dsink_rows):
    h, n = dbias.shape

    def body(d_ref, o_ref, s_ref, out_ref, so_ref):
        a, b, c = _split3(d_ref[...])
        dot = functools.partial(lax.dot_general, dimension_numbers=_DIMS["nn"], preferred_element_type=F32)
        oh = o_ref[...]
        out_ref[...] = dot(a, oh) + (dot(b, oh) + dot(c, oh))
        so_ref[...] = jnp.sum(s_ref[...], axis=-1, keepdims=True)

    full = lambda shp: pl.BlockSpec(shp, lambda: (0,) * len(shp))
    return _pcall(body, name="bias_reduce", in_specs=[full(dbias.shape), full(onehot.shape), full(dsink_rows.shape)],
                  out_specs=[full((h, REL_BUCKETS)), full((h, 1))],
                  out_shape=[jax.ShapeDtypeStruct((h, REL_BUCKETS), F32), jax.ShapeDtypeStruct((h, 1), F32)],
                  compiler_params=pltpu.CompilerParams(vmem_limit_bytes=VMEM_LIMIT))(dbias, onehot, dsink_rows)


def _swa_specs(s):
    rows = SWA_GROUP * BLOCK
    q_spec = pl.BlockSpec((None, SWA_GROUP, BLOCK, SWA_DH), lambda g, n: (g, 0, n, 0))
    kv_prev = pl.BlockSpec((None, BLOCK, SWA_DH), lambda g, n: (g, jnp.maximum(n - 1, 0), 0))
    kv_cur = pl.BlockSpec((None, BLOCK, SWA_DH), lambda g, n: (g, n, 0))
    bias_spec = pl.BlockSpec((None, rows, 2 * BLOCK), lambda g, n: (g, 0, 0))
    col_spec = pl.BlockSpec((None, rows, 1), lambda g, n: (g, 0, 0))
    lse_spec = pl.BlockSpec((None, None, rows, 1), lambda g, n: (g, n, 0, 0))
    return rows, q_spec, kv_prev, kv_cur, bias_spec, col_spec, lse_spec


def _swa_scores(q_ref, kp_ref, kc_ref, bias_ref, n):
    rows = SWA_GROUP * BLOCK
    q = q_ref[...].reshape(rows, SWA_DH)
    kb = jnp.concatenate([kp_ref[...], kc_ref[...]], axis=0)
    s = lax.dot_general(q, kb, _DIMS["nt"], preferred_element_type=F32) * SWA_SCALE + bias_ref[...]
    col = lax.broadcasted_iota(jnp.int32, s.shape, 1)
    s = jnp.where(jnp.logical_and(n == 0, col < BLOCK), -jnp.inf, s)
    return q, kb, s


def _swa_fwd(q, k, v, bias, sink_rows, comm=None):
    s = q.shape[2]
    nb = s // BLOCK
    rows, q_spec, kv_prev, kv_cur, bias_spec, col_spec, lse_spec = _swa_specs(s)
    v_prev = pl.BlockSpec((None, BLOCK, 2 * SWA_DH), lambda g, n: (g, jnp.maximum(n - 1, 0), 0))
    v_cur = pl.BlockSpec((None, BLOCK, 2 * SWA_DH), lambda g, n: (g, n, 0))

    def body(q_ref, kp_ref, kc_ref, vp_ref, vc_ref, bias_ref, sink_ref, o_ref, lse_ref):
        n = pl.program_id(1)
        _, _, sc = _swa_scores(q_ref, kp_ref, kc_ref, bias_ref, n)
        sink = sink_ref[...]
        m = jnp.maximum(jnp.max(sc, axis=-1, keepdims=True), sink)
        p = jnp.exp(sc - m)
        vb = jnp.concatenate([vp_ref[...], vc_ref[...]], axis=0)
        acc = lax.dot_general(p.astype(BF16), vb, _DIMS["nn"], preferred_element_type=F32)
        den = acc[:, SWA_DH:SWA_DH + 1] + jnp.exp(sink - m)
        o_ref[...] = (acc[:, 0:SWA_DH] / den).reshape(SWA_GROUP, BLOCK, SWA_DH).astype(BF16)
        lse_ref[...] = m + jnp.log(den)

    return _pcall(
        body, comm=comm, name="swa_fwd", grid=(SWA_KV, nb),
        in_specs=[q_spec, kv_prev, kv_cur, v_prev, v_cur, bias_spec, col_spec],
        out_specs=[q_spec, lse_spec],
        out_shape=[jax.ShapeDtypeStruct(q.shape, BF16), jax.ShapeDtypeStruct((SWA_KV, nb, rows, 1), F32)],
        compiler_params=_cparams(("parallel", "parallel")))(q, k, k, v, v, bias, sink_rows)


def _swa_bwd(q, k, v, do, lse, bias, sink_rows, comm=None):
    s = q.shape[2]
    nb = s // BLOCK
    rows, q_spec, kv_prev, kv_cur, bias_spec, col_spec, lse_spec = _swa_specs(s)

    def body(q_ref, kp_ref, kc_ref, vp_ref, vc_ref, do_ref, lse_ref, bias_ref, sink_ref,
             dq_ref, dkp_ref, dkc_ref, dvp_ref, dvc_ref, dbias_ref, dsink_ref):
        n = pl.program_id(1)

        @pl.when(n == 0)
        def _():
            dbias_ref[...] = jnp.zeros_like(dbias_ref)
            dsink_ref[...] = jnp.zeros_like(dsink_ref)

        qv, kb, sc = _swa_scores(q_ref, kp_ref, kc_ref, bias_ref, n)
        lse_v = lse_ref[...]
        p = jnp.exp(sc - lse_v)
        p_sink = jnp.exp(sink_ref[...] - lse_v)
        dov = do_ref[...].reshape(rows, SWA_DH)
        vb = jnp.concatenate([vp_ref[...], vc_ref[...]], axis=0)
        dp = lax.dot_general(dov, vb, _DIMS["nt"], preferred_element_type=F32)
        delta = jnp.sum(p * dp, axis=-1, keepdims=True)
        ds = p * (dp - delta)
        dbias_ref[...] += ds
        dsink_ref[...] += -p_sink * delta
        dsb = (ds * SWA_SCALE).astype(BF16)
        dq = lax.dot_general(dsb, kb, _DIMS["nn"], preferred_element_type=F32)
        dq_ref[...] = dq.reshape(SWA_GROUP, BLOCK, SWA_DH).astype(BF16)
        dk = lax.dot_general(dsb, qv, _DIMS["tn"], preferred_element_type=F32)
        dv = lax.dot_general(p.astype(BF16), dov, _DIMS["tn"], preferred_element_type=F32)
        dkp_ref[...] = dk[:BLOCK]
        dkc_ref[...] = dk[BLOCK:]
        dvp_ref[...] = dv[:BLOCK]
        dvc_ref[...] = dv[BLOCK:]

    kv_out = jax.ShapeDtypeStruct((SWA_KV, s, SWA_DH), F32)
    return _pcall(
        body, comm=comm, name="swa_bwd", grid=(SWA_KV, nb),
        in_specs=[q_spec, kv_prev, kv_cur, kv_prev, kv_cur, q_spec, lse_spec, bias_spec, col_spec],
        out_specs=[q_spec, kv_cur, kv_cur, kv_cur, kv_cur, bias_spec, col_spec],
        out_shape=[jax.ShapeDtypeStruct(q.shape, BF16), kv_out, kv_out, kv_out, kv_out,
                   jax.ShapeDtypeStruct(bias.shape, F32), jax.ShapeDtypeStruct(sink_rows.shape, F32)],
        compiler_params=_cparams(("arbitrary", "arbitrary")))(q, k, k, v, v, do, lse, bias, sink_rows)


def _rope_slab(slab, table):
    t = slab * table
    return t + pltpu.roll(t, ROPE, 1)


def _low_lanes(v):
    lane = lax.broadcasted_iota(jnp.int32, v.shape, 1)
    return jnp.where(lane < ROPE, v, 0.0)


def _rms(xv, g):
    r = lax.rsqrt(jnp.mean(xv * xv, axis=-1, keepdims=True) + EPS)
    return xv * r, r


def _mla_prep(proj, gq, gkv, table):
    s = proj.shape[0]
    ts = _pick(s, ROW_TILE, 16)

    def body(p_ref, gq_ref, gkv_ref, t_ref, cq_ref, ckv_ref, kr_ref):
        xq, _ = _rms(p_ref[:, 0:Q_RANK], None)
        cq_ref[...] = (xq * gq_ref[...]).astype(BF16)
        xkv, _ = _rms(p_ref[:, Q_RANK:Q_RANK + KV_RANK], None)
        ckv_ref[...] = (xkv * gkv_ref[...]).astype(BF16)
        kr_ref[...] = _low_lanes(_rope_slab(p_ref[:, Q_RANK + KV_RANK:TAIL], t_ref[...]))

    return _pcall(
        body, name="mla_prep", grid=(s // ts,),
        in_specs=[pl.BlockSpec((ts, TAIL), lambda i: (i, TAIL0 // TAIL)), _vec(Q_RANK), _vec(KV_RANK), _row(ts, 2 * ROPE)],
        out_specs=[_row(ts, Q_RANK), _row(ts, KV_RANK), _row(ts, 2 * ROPE)],
        out_shape=[jax.ShapeDtypeStruct((s, Q_RANK), BF16), jax.ShapeDtypeStruct((s, KV_RANK), BF16),
                   jax.ShapeDtypeStruct((s, 2 * ROPE), F32)],
        compiler_params=_cparams(("parallel",)))(proj, gq, gkv, table)


def _mla_prep_bwd(proj, dcq, dckv, dkr, gq, gkv, table):
    s = proj.shape[0]
    ts = _pick(s, ROW_TILE, 16)

    def norm_bwd(xv, dy, g):
        xn, r = _rms(xv, None)
        dg = jnp.sum(dy * xn, axis=0, keepdims=True)
        dxn = dy * g
        return r * (dxn - xn * jnp.mean(dxn * xn, axis=-1, keepdims=True)), dg

    def body(p_ref, dcq_ref, dckv_ref, dkr_ref, gq_ref, gkv_ref, t_ref, dt_ref, dgq_ref, dgkv_ref):
        @pl.when(pl.program_id(0) == 0)
        def _():
            dgq_ref[...] = jnp.zeros_like(dgq_ref)
            dgkv_ref[...] = jnp.zeros_like(dgkv_ref)

        dxq, dgq = norm_bwd(p_ref[:, 0:Q_RANK], dcq_ref[...], gq_ref[...])
        dxkv, dgkv = norm_bwd(p_ref[:, Q_RANK:Q_RANK + KV_RANK], dckv_ref[...], gkv_ref[...])
        dgq_ref[...] += dgq
        dgkv_ref[...] += dgkv
        d = _low_lanes(dkr_ref[...])
        dslab = (d + pltpu.roll(d, ROPE, 1)) * t_ref[...]
        dt_ref[:, 0:Q_RANK] = dxq.astype(BF16)
        dt_ref[:, Q_RANK:Q_RANK + KV_RANK] = dxkv.astype(BF16)
        dt_ref[:, Q_RANK + KV_RANK:TAIL] = dslab.astype(BF16)

    return _pcall(
        body, name="mla_prep_bwd", grid=(s // ts,),
        in_specs=[pl.BlockSpec((ts, TAIL), lambda i: (i, TAIL0 // TAIL)), _row(ts, Q_RANK), _row(ts, KV_RANK),
                  _row(ts, 2 * ROPE), _vec(Q_RANK), _vec(KV_RANK), _row(ts, 2 * ROPE)],
        out_specs=[_row(ts, TAIL), _vec(Q_RANK), _vec(KV_RANK)],
        out_shape=[jax.ShapeDtypeStruct((s, TAIL), BF16), jax.ShapeDtypeStruct((1, Q_RANK), F32),
                   jax.ShapeDtypeStruct((1, KV_RANK), F32)],
        compiler_params=_cparams(("arbitrary",)))(proj, dcq, dckv, dkr, gq, gkv, table)


def _head_specs(ts):
    tok = lambda w: pl.BlockSpec((ts, w), lambda h, i: (i, 0))
    head = lambda w: pl.BlockSpec((None, ts, w), lambda h, i: (h, i, 0))
    wgt = lambda r, c: pl.BlockSpec((None, r, c), lambda h, i: (h, 0, 0))
    return tok, head, wgt


def _mla_qkv(cq, ckv, kr, wq, wkv, table):
    s = cq.shape[0]
    ts = _pick(s, 2 * ROW_TILE, 16)
    tok, head, wgt = _head_specs(ts)

    def body(cq_ref, ckv_ref, kr_ref, wq_ref, wkv_ref, t_ref, q_ref, k_ref, v_ref):
        qf = lax.dot_general(cq_ref[...], wq_ref[...], _DIMS["nn"], preferred_element_type=F32)
        q_ref[:, 0:NOPE] = qf[:, 0:NOPE].astype(BF16)
        q_ref[:, NOPE:QW] = _rope_slab(qf[:, NOPE:QW], t_ref[...]).astype(BF16)
        kv = lax.dot_general(ckv_ref[...], wkv_ref[...], _DIMS["nn"], preferred_element_type=F32)
        k_ref[:, 0:NOPE] = kv[:, 0:NOPE].astype(BF16)
        k_ref[:, NOPE:QW] = kr_ref[...].astype(BF16)
        v_ref[:, 0:VDIM] = kv[:, NOPE:NOPE + VDIM].astype(BF16)
        lane = lax.broadcasted_iota(jnp.int32, (ts, VDIM), 1)
        v_ref[:, VDIM:2 * VDIM] = jnp.where(lane == 0, 1.0, 0.0).astype(BF16)

    return _pcall(
        body, name="mla_qkv", grid=(MLA_H, s // ts),
        in_specs=[tok(Q_RANK), tok(KV_RANK), tok(2 * ROPE), wgt(Q_RANK, QW), wgt(KV_RANK, NOPE + VDIM), tok(2 * ROPE)],
        out_specs=[head(QW), head(QW), head(2 * VDIM)],
        out_shape=[jax.ShapeDtypeStruct((MLA_H, s, QW), BF16), jax.ShapeDtypeStruct((MLA_H, s, QW), BF16),
                   jax.ShapeDtypeStruct((MLA_H, s, 2 * VDIM), BF16)],
        compiler_params=_cparams(("parallel", "parallel")))(cq, ckv, kr, wq, wkv, table)


def _mla_qkv_bwd(dq, dk, dv, cq, ckv, wq, wkv, table, comm=None):
    s = cq.shape[0]
    ts = _pick(s, 2 * ROW_TILE, 16)
    tok, head, wgt = _head_specs(ts)
    whole = lambda w: pl.BlockSpec((s, w), lambda h, i: (0, 0))

    def body(dq_ref, dk_ref, dv_ref, cq_ref, ckv_ref, wq_ref, wkv_ref, t_ref,
             dcq_ref, dckv_ref, dkr_ref, gwq_ref, gwkv_ref):
        h, i = pl.program_id(0), pl.program_id(1)
        rows = pl.ds(pl.multiple_of(i * ts, ts), ts)
        d = dq_ref[:, NOPE:QW]
        dslab = (d + pltpu.roll(d, ROPE, 1)) * t_ref[...]
        dqe = jnp.concatenate([dq_ref[:, 0:NOPE], dslab], axis=1).astype(BF16)
        dkv = jnp.concatenate([dk_ref[:, 0:NOPE], dv_ref[...]], axis=1).astype(BF16)
        dcq = lax.dot_general(dqe, wq_ref[...], _DIMS["nt"], preferred_element_type=F32)
        dckv = lax.dot_general(dkv, wkv_ref[...], _DIMS["nt"], preferred_element_type=F32)
        gwq = lax.dot_general(cq_ref[...], dqe, _DIMS["tn"], preferred_element_type=F32)
        gwkv = lax.dot_general(ckv_ref[...], dkv, _DIMS["tn"], preferred_element_type=F32)
        dkr = dk_ref[:, NOPE:QW]

        @pl.when(h == 0)
        def _():
            dcq_ref[rows, :] = dcq
            dckv_ref[rows, :] = dckv
            dkr_ref[rows, :] = dkr

        @pl.when(h > 0)
        def _():
            dcq_ref[rows, :] += dcq
            dckv_ref[rows, :] += dckv
            dkr_ref[rows, :] += dkr

        @pl.when(i == 0)
        def _():
            gwq_ref[...] = gwq
            gwkv_ref[...] = gwkv

        @pl.when(i > 0)
        def _():
            gwq_ref[...] += gwq
            gwkv_ref[...] += gwkv

    return _pcall(
        body, comm=comm, name="mla_qkv_bwd", grid=(MLA_H, s // ts),
        in_specs=[head(QW), head(QW), head(VDIM), tok(Q_RANK), tok(KV_RANK), wgt(Q_RANK, QW),
                  wgt(KV_RANK, NOPE + VDIM), tok(2 * ROPE)],
        out_specs=[whole(Q_RANK), whole(KV_RANK), whole(2 * ROPE), wgt(Q_RANK, QW), wgt(KV_RANK, NOPE + VDIM)],
        out_shape=[jax.ShapeDtypeStruct((s, Q_RANK), F32), jax.ShapeDtypeStruct((s, KV_RANK), F32),
                   jax.ShapeDtypeStruct((s, 2 * ROPE), F32), jax.ShapeDtypeStruct((MLA_H, Q_RANK, QW), F32),
                   jax.ShapeDtypeStruct((MLA_H, KV_RANK, NOPE + VDIM), F32)],
        compiler_params=_cparams(("arbitrary", "arbitrary")))(dq, dk, dv, cq, ckv, wq, wkv, table)


def _diag_mask(t):
    return lax.broadcasted_iota(jnp.int32, (t, t), 1) <= lax.broadcasted_iota(jnp.int32, (t, t), 0)


def _mla_fwd(q, k, v, comm=None):
    s = q.shape[1]
    t = _pick(s, ATT_T, 128)
    nt = s // t
    assert nt % 2 == 0
    hb = MLA_HB

    def fold(p, u):
        first = u <= p
        return jnp.where(first, p, nt - 1 - p), jnp.where(first, u, u - p - 1)

    to_log2 = MLA_SCALE * math.log2(math.e)

    def body(q_ref, k_ref, v_ref, o_ref, oh_ref, lse_ref, m_ref, acc_ref):
        i, j = fold(pl.program_id(1), pl.program_id(2))

        @pl.when(j == 0)
        def _():
            m_ref[...] = jnp.full_like(m_ref, -jnp.inf)
            acc_ref[...] = jnp.zeros_like(acc_ref)

        def step(diagonal):
            for h in range(hb):
                sc = lax.dot_general(q_ref[h], k_ref[h], _DIMS["nt"], preferred_element_type=F32)
                if diagonal:
                    sc = jnp.where(_diag_mask(t), sc, -jnp.inf)
                m_old = m_ref[h]
                m_new = jnp.maximum(m_old, jnp.max(sc, axis=-1, keepdims=True))
                alpha = jnp.exp2((m_old - m_new) * to_log2)
                p = jnp.exp2((sc - m_new) * to_log2)
                acc_ref[h] = alpha * acc_ref[h] + lax.dot_general(p.astype(BF16), v_ref[h], _DIMS["nn"],
                                                                  preferred_element_type=F32)
                m_ref[h] = m_new

        @pl.when(j < i)
        def _():
            step(False)

        @pl.when(j == i)
        def _():
            step(True)
            for h in range(hb):
                den = acc_ref[h, :, VDIM:VDIM + 1]
                o = acc_ref[h, :, 0:VDIM] / den
                o_ref[:, h * VDIM:(h + 1) * VDIM] = o
                oh_ref[:, h * VDIM:(h + 1) * VDIM] = o.astype(BF16)
                lse_ref[h] = m_ref[h] * MLA_SCALE + jnp.log(den)

    o_spec = pl.BlockSpec((t, hb * VDIM), lambda h, p, u: (fold(p, u)[0], h))
    return _pcall(
        body, comm=comm, name="mla_fwd", grid=(MLA_H // hb, nt // 2, nt + 1),
        in_specs=[pl.BlockSpec((hb, t, QW), lambda h, p, u: (h, fold(p, u)[0], 0)),
                  pl.BlockSpec((hb, t, QW), lambda h, p, u: (h, fold(p, u)[1], 0)),
                  pl.BlockSpec((hb, t, 2 * VDIM), lambda h, p, u: (h, fold(p, u)[1], 0))],
        out_specs=[o_spec, o_spec, pl.BlockSpec((hb, t, 1), lambda h, p, u: (h, fold(p, u)[0], 0))],
        out_shape=[jax.ShapeDtypeStruct((s, MLA_H * VDIM), F32), jax.ShapeDtypeStruct((s, MLA_H * VDIM), BF16),
                   jax.ShapeDtypeStruct((MLA_H, s, 1), F32)],
        scratch_shapes=[pltpu.VMEM((hb, t, 1), F32), pltpu.VMEM((hb, t, 2 * VDIM), F32)],
        compiler_params=_cparams(("parallel", "parallel", "arbitrary")))(q, k, v)


def _mla_bwd(q, k, v, dmix, o, lse, comm=None):
    s = q.shape[1]
    t = _pick(s, ATT_T, 128)
    nt = s // t
    assert nt % 2 == 0
    hb = MLA_HB
    o_blk0 = SWA_HEADS * SWA_DH // (hb * VDIM)

    def fold(p, u):
        first = u < nt - p
        return jnp.where(first, p, nt - 1 - p), jnp.where(first, p + u, u - 1)

    def body(q_ref, k_ref, v_ref, do_ref, o_ref, lse_ref, dq_ref, dk_ref, dv_ref, dk_acc, dv_acc):
        j, i = fold(pl.program_id(1), pl.program_id(2))
        rows = pl.ds(pl.multiple_of(i * t, t), t)

        @pl.when(i == j)
        def _():
            dk_acc[...] = jnp.zeros_like(dk_acc)
            dv_acc[...] = jnp.zeros_like(dv_acc)

        def step(diagonal):
            for h in range(hb):
                qv, kv_ = q_ref[h], k_ref[h]
                dov = do_ref[:, h * VDIM:(h + 1) * VDIM]
                sc = lax.dot_general(qv, kv_, _DIMS["nt"], preferred_element_type=F32) * MLA_SCALE
                p = jnp.exp(sc - lse_ref[h])
                if diagonal:
                    p = jnp.where(_diag_mask(t), p, 0.0)
                dob = dov.astype(BF16)
                dp = lax.dot_general(dob, v_ref[h], _DIMS["nt"], preferred_element_type=F32)
                delta = jnp.sum(dov * o_ref[:, h * VDIM:(h + 1) * VDIM], axis=-1, keepdims=True)
                ds = (p * (dp - delta) * MLA_SCALE).astype(BF16)
                dv_acc[h] += lax.dot_general(p.astype(BF16), dob, _DIMS["tn"], preferred_element_type=F32)
                dk_acc[h] += lax.dot_general(ds, qv, _DIMS["tn"], preferred_element_type=F32)
                dqv = lax.dot_general(ds, kv_, _DIMS["nn"], preferred_element_type=F32)

                @pl.when(j == 0)
                def _():
                    dq_ref[h, rows, :] = dqv

                @pl.when(j > 0)
                def _():
                    dq_ref[h, rows, :] += dqv

        @pl.when(i > j)
        def _():
            step(False)

        @pl.when(i == j)
        def _():
            step(True)

        @pl.when(i == nt - 1)
        def _():
            dk_ref[...] = dk_acc[...]
            dv_ref[...] = dv_acc[...]

    qi = lambda h, p, u: (h, fold(p, u)[1], 0)
    kj = lambda h, p, u: (h, fold(p, u)[0], 0)
    return _pcall(
        body, comm=comm, name="mla_bwd", grid=(MLA_H // hb, nt // 2, nt + 1),
        in_specs=[pl.BlockSpec((hb, t, QW), qi), pl.BlockSpec((hb, t, QW), kj), pl.BlockSpec((hb, t, VDIM), kj),
                  pl.BlockSpec((t, hb * VDIM), lambda h, p, u: (fold(p, u)[1], o_blk0 + h)),
                  pl.BlockSpec((t, hb * VDIM), lambda h, p, u: (fold(p, u)[1], h)),
                  pl.BlockSpec((hb, t, 1), qi)],
        out_specs=[pl.BlockSpec((hb, s, QW), lambda h, p, u: (h, 0, 0)), pl.BlockSpec((hb, t, QW), kj),
                   pl.BlockSpec((hb, t, VDIM), kj)],
        out_shape=[jax.ShapeDtypeStruct((MLA_H, s, QW), F32), jax.ShapeDtypeStruct((MLA_H, s, QW), F32),
                   jax.ShapeDtypeStruct((MLA_H, s, VDIM), F32)],
        scratch_shapes=[pltpu.VMEM((hb, t, QW), F32), pltpu.VMEM((hb, t, VDIM), F32)],
        compiler_params=_cparams(("arbitrary", "arbitrary", "arbitrary")))(q, k, v, dmix, o, lse)


def _adamw(name, w, g, m, v, parts):
    r, c = w.shape
    n_parts = g.shape[0] if parts else 1
    tr = r if r * c <= ADAM_ELEMS else _pick(r, max(8, ADAM_ELEMS // c // 8 * 8), 8)
    c1 = 1.0 - ADAM_B1 ** ADAM_STEP
    c2 = 1.0 - ADAM_B2 ** ADAM_STEP

    def body(w_ref, g_ref, m_ref, v_ref, go_ref, d_ref, mo_ref, vo_ref):
        if parts:
            gv = g_ref[0].astype(F32)
            for j in range(1, n_parts):
                gv = gv + g_ref[j].astype(F32)
        else:
            gv = g_ref[...]
        mv = ADAM_B1 * m_ref[...] + (1.0 - ADAM_B1) * gv
        vv = ADAM_B2 * v_ref[...] + (1.0 - ADAM_B2) * (gv * gv)
        go_ref[...] = gv
        mo_ref[...] = mv
        vo_ref[...] = vv
        d_ref[...] = -ADAM_LR * ((mv / c1) / (jnp.sqrt(vv / c2) + ADAM_EPS) + ADAM_WD * w_ref[...])

    blk = pl.BlockSpec((tr, c), lambda i: (i, 0))
    g_spec = pl.BlockSpec((n_parts, tr, c), lambda i: (0, i, 0)) if parts else blk
    out = jax.ShapeDtypeStruct((r, c), F32)
    return _pcall(body, name=name, grid=(r // tr,), in_specs=[blk, g_spec, blk, blk], out_specs=[blk] * 4,
                  out_shape=[out] * 4, compiler_params=_cparams(("parallel",)))(w, g, m, v)


def _t5_bucket(dist):
    n = jnp.maximum(dist, 0)
    max_exact = REL_BUCKETS // 2
    nf = jnp.maximum(n, 1).astype(F32)
    large = max_exact + (jnp.log(nf / max_exact) / math.log(REL_MAX_DIST / max_exact)
                         * (REL_BUCKETS - max_exact)).astype(jnp.int32)
    return jnp.where(n < max_exact, n, jnp.minimum(large, REL_BUCKETS - 1))


def _swap_halves(w, r0):
    return jnp.concatenate([w[:, r0 + ROPE // 2:r0 + ROPE], w[:, r0:r0 + ROPE // 2]], axis=1)


def _fold_swapped(g, r0, width):
    sw = g[..., width:width + ROPE]
    half = ROPE // 2
    return jnp.concatenate([g[..., :r0], g[..., r0:r0 + half] + sw[..., half:], g[..., r0 + half:r0 + ROPE] + sw[..., :half],
                            g[..., r0 + ROPE:width]], axis=-1)


def _heads_major(a, heads):
    s = a.shape[0]
    return a.reshape(s, heads, SWA_DH).transpose(1, 0, 2)


def _tokens_major(a):
    h, s, d = a.shape
    return a.transpose(1, 0, 2).reshape(s, h * d)


def kernel(x, c, w_mod, b_mod, attn_norm_g, w_in, swa_sinks, rel_bias, mla_q_norm_g, w_uq, mla_kv_norm_g, w_ukv, w_out, mlp_norm_g, w_ff1, w_ff2, final_norm_g, loss_target, m_w_mod, m_b_mod, m_attn_norm_g, m_w_in, m_swa_sinks, m_rel_bias, m_mla_q_norm_g, m_w_uq, m_mla_kv_norm_g, m_w_ukv, m_w_out, m_mlp_norm_g, m_w_ff1, m_w_ff2, m_final_norm_g, v_w_mod, v_b_mod, v_attn_norm_g, v_w_in, v_swa_sinks, v_rel_bias, v_mla_q_norm_g, v_w_uq, v_mla_kv_norm_g, v_w_ukv, v_w_out, v_mlp_norm_g, v_w_ff1, v_w_ff2, v_final_norm_g):
    s, d = x.shape[1], x.shape[2]
    ffs = w_ff1.shape[2]
    ff = ffs * NDEV
    nmod = w_mod.shape[2]
    me = 4 * lax.axis_index("x") + 2 * lax.axis_index("y") + lax.axis_index("c")
    x2d, tgt = x[0], loss_target[0]
    final_g = final_norm_g.reshape(1, d)

    w_in_l = jnp.concatenate([w_in[0], _swap_halves(w_in[0], OFF_KR)], axis=1).astype(BF16)
    w_uq_l = jnp.concatenate([w_uq[0], _swap_halves(w_uq[0], NOPE)], axis=1).astype(BF16)
    core = jnp.full((1, 128), lax.axis_index("c"), F32)
    (c_all,) = _exchange("gather_c", _Gather([c]))

    b_cols = lax.dynamic_slice(b_mod, (0, me * nmod), (1, nmod))
    act_all, mod_cols, w_in_g, w_uq_g, w_ukv_g = _mod_fwd(
        c_all.reshape(NDEV, d), w_mod[0], b_cols, comm=[_Gather([w_in_l, w_uq_l, w_ukv[0].astype(BF16)])])
    w_in_e = w_in_g.reshape(d, IN_EXT)
    (mod_g,) = _exchange("gather_mod", _Gather([mod_cols]))
    mod = lax.dynamic_index_in_dim(mod_g, me, axis=1, keepdims=False).reshape(1, 6 * d)
    sh1, sc1, g1, sh2, sc2, g2 = [mod[:, i * d:(i + 1) * d] for i in range(6)]

    pos = jnp.arange(s, dtype=F32)
    inv_freq = ROPE_THETA ** (-jnp.arange(ROPE // 2, dtype=F32) / (ROPE // 2))
    ang = pos[:, None] * inv_freq[None, :]
    cos, sin = jnp.cos(ang), jnp.sin(ang)
    table = jnp.concatenate([cos, cos, -sin, sin], axis=1)
    q_loc = jnp.arange(BLOCK)[:, None]
    k_loc = jnp.arange(2 * BLOCK)[None, :]
    dist = q_loc + BLOCK - k_loc
    in_window = (dist >= 0) & (dist < BLOCK)
    onehot = (_t5_bucket(dist).reshape(-1, 1) == jnp.arange(REL_BUCKETS)[None, :]).astype(BF16)
    bias = _bias_expand(rel_bias.T, onehot.T).reshape(SWA_HEADS, BLOCK, 2 * BLOCK)
    bias = jnp.where(in_window[None], bias, -jnp.inf).reshape(SWA_KV, SWA_GROUP * BLOCK, 2 * BLOCK)
    sink_rows = jnp.broadcast_to(swa_sinks.reshape(SWA_HEADS, 1), (SWA_HEADS, BLOCK)).reshape(SWA_KV, SWA_GROUP * BLOCK, 1)

    h1 = _norm_mod("norm1", x2d, attn_norm_g, sc1, sh1)
    def both_dtypes(acc, ex, outs):
        outs[0][...] = acc
        outs[1][...] = acc.astype(BF16)

    tmp = _pick(s, MM_TM // 2, 128)
    proj_blk = pl.BlockSpec((tmp, IN_EXT), lambda i, j, q: (i, 0))
    proj, proj_h = _mm("proj", h1, w_in_e, "nn", (s // tmp, 1, 1), pl.BlockSpec((tmp, d), lambda i, j, q: (i, 0)),
                       pl.BlockSpec((d, IN_EXT), lambda i, j, q: (0, 0)),
                       [jax.ShapeDtypeStruct((s, IN_EXT), F32), jax.ShapeDtypeStruct((s, IN_EXT), BF16)],
                       [proj_blk, proj_blk], (tmp, IN_EXT), both_dtypes)
    q_a = _heads_major(proj_h[:, :OFF_K], SWA_HEADS).reshape(SWA_KV, SWA_GROUP, s, SWA_DH)
    k_a = _heads_major(proj_h[:, OFF_K:OFF_V], SWA_KV)
    v_a = _heads_major(proj_h[:, OFF_V:OFF_CQ], SWA_KV)
    ones_col = (jnp.arange(SWA_DH) == 0).astype(BF16)
    v_a1 = jnp.concatenate([v_a, jnp.broadcast_to(ones_col, v_a.shape)], axis=2)
    o_a, lse_a, w_out_g = _swa_fwd(q_a, k_a, v_a1, bias, sink_rows, comm=[_Gather([w_out[0].astype(BF16)])])
    w_out_f = w_out_g.reshape(MIX, d)

    cq, ckv, kr = _mla_prep(proj, mla_q_norm_g, mla_kv_norm_g, table)
    q_b, k_b, v_b = _mla_qkv(cq, ckv, kr, w_uq_g, w_ukv_g, table)
    o_b, o_bh, lse_b, w_ff1_g = _mla_fwd(q_b, k_b, v_b, comm=[_Gather([w_ff1[0].astype(BF16)])])
    mix = jnp.concatenate([_tokens_major(o_a.reshape(SWA_HEADS, s, SWA_DH)), o_bh], axis=1)

    tm, tn, tk = _pick(s, MM_TM, 128), _pick(d, MM_TN, 128), _pick(MIX, MM_TK, 128)
    row_blk = pl.BlockSpec((tm, tn), lambda i, j, q: (i, j))
    gate_blk = pl.BlockSpec((1, tn), lambda i, j, q: (0, j))

    def gated_residual(acc, ex, outs):
        outs[0][...] = acc
        outs[1][...] = ex[0][...] + ex[1][...] * acc

    y1, x2 = _mm("out_proj", mix, w_out_f, "nn", (s // tm, d // tn, MIX // tk),
                 pl.BlockSpec((tm, tk), lambda i, j, q: (i, q)), pl.BlockSpec((tk, tn), lambda i, j, q: (q, j)),
                 [jax.ShapeDtypeStruct((s, d), F32)] * 2, [row_blk, row_blk], (tm, tn), gated_residual,
                 extras=(x2d, g1), extra_specs=(row_blk, gate_blk))

    h2 = _norm_mod("norm2", x2, mlp_norm_g, sc2, sh2)
    tnf, tkd = _pick(ffs, MM_TN, 128), _pick(d, MM_TK, 128)
    rf = ffs // tnf
    ff_blk = pl.BlockSpec((tm, tnf), lambda i, j, q: (i, j))

    def relu_sq(acc, ex, outs):
        u = jnp.maximum(acc, 0.0)
        outs[0][...] = u
        outs[1][...] = (u * u).astype(BF16)

    u, uu, w_ff2_g = _mm("ff1", h2, w_ff1_g, "nn", (s // tm, ff // tnf, d // tkd),
                         pl.BlockSpec((tm, tkd), lambda i, j, q: (i, q)),
                         pl.BlockSpec((None, tkd, tnf), lambda i, j, q: (j // rf, q, j % rf)),
                         [jax.ShapeDtypeStruct((s, ff), F32), jax.ShapeDtypeStruct((s, ff), BF16)], [ff_blk, ff_blk],
                         (tm, tnf), relu_sq, comm=[_Gather([w_ff2[0].astype(BF16)])])
    w_ff2_f = w_ff2_g.reshape(ff, d)
    tkf = _pick(ff, MM_TK, 128)
    y2, x3 = _mm("ff2", uu, w_ff2_f, "nn", (s // tm, d // tn, ff // tkf),
                 pl.BlockSpec((tm, tkf), lambda i, j, q: (i, q)), pl.BlockSpec((tkf, tn), lambda i, j, q: (q, j)),
                 [jax.ShapeDtypeStruct((s, d), F32)] * 2, [row_blk, row_blk], (tm, tn), gated_residual,
                 extras=(x2, g2), extra_specs=(row_blk, gate_blk))

    dx3, dy2, loss_p, dgf, dg2 = _loss_head(x3, tgt, y2, final_g, g2)
    loss = lax.psum(loss_p[0, 0], ("x", "y", "c"))

    def relu_sq_bwd(acc, ex, outs):
        outs[0][...] = (acc * (2.0 * ex[0][...])).astype(BF16)

    tnf2 = _pick(ff, MM_TN, 128)
    du = _mm("ff2_dx", dy2, w_ff2_f, "nt", (s // tm, ff // tnf2, d // tkd),
             pl.BlockSpec((tm, tkd), lambda i, j, q: (i, q)), pl.BlockSpec((tnf2, tkd), lambda i, j, q: (j, q)),
             [jax.ShapeDtypeStruct((s, ff), BF16)], [pl.BlockSpec((tm, tnf2), lambda i, j, q: (i, j))],
             (tm, tnf2), relu_sq_bwd, extras=(u,), extra_specs=(pl.BlockSpec((tm, tnf2), lambda i, j, q: (i, j)),))[0]
    gw_ff2 = _mm_plain("ff2_dw", uu, dy2, "tn", ff, d, s, BF16)
    tmd, tks = _pick(d, MM_TM, 128), _pick(s, MM_TK, 128)
    gw_ff2 = gw_ff2.reshape(NDEV, ffs, d)
    gw_ff1, s_ff2 = _mm("ff1_dw", h2, du, "tn", (d // tmd, ff // tnf, s // tks),
                        pl.BlockSpec((tks, tmd), lambda i, j, q: (q, i)), pl.BlockSpec((tks, tnf), lambda i, j, q: (q, j)),
                        [jax.ShapeDtypeStruct((NDEV, d, ffs), BF16)],
                        [pl.BlockSpec((None, tmd, tnf), lambda i, j, q: (j // rf, i, j % rf))], (tmd, tnf), _store(BF16),
                        comm=[_PairSwap([gw_ff2])])
    c_ff2 = _pair_sum("pair_ff2", gw_ff2, s_ff2, core)
    tkf1 = _pick(ffs, MM_TK, 128)
    rk = ffs // tkf1
    dh2, p_ff2, s_ff1 = _mm("ff1_dx", du, w_ff1_g, "nt", (s // tm, d // tn, ff // tkf1),
                            pl.BlockSpec((tm, tkf1), lambda i, j, q: (i, q)),
                            pl.BlockSpec((None, tn, tkf1), lambda i, j, q: (q // rk, j, q % rk)),
                            [jax.ShapeDtypeStruct((s, d), F32)], [row_blk], (tm, tn), _store(F32),
                            comm=[_ChipScatter([c_ff2]), _PairSwap([gw_ff1])])
    c_ff1 = _pair_sum("pair_ff1", gw_ff1, s_ff1, core)
    dx2, dy1, dsc2, dsh2, dgm, dg1 = _norm_mod_bwd("norm2_bwd", x2, dh2, dx3, mlp_norm_g, sc2, y1, g1)

    dmix = _mm_plain("out_proj_dx", dy1, w_out_f, "nt", s, MIX, d, F32)
    gw_out = _mm_plain("out_proj_dw", mix, dy1, "tn", MIX, d, s, BF16).reshape(NDEV, MIX // NDEV, d)

    dq_b, dk_b, dv_b, p_ff1, s_out = _mla_bwd(q_b, k_b, v_b, dmix, o_b, lse_b,
                                              comm=[_ChipScatter([c_ff1]), _PairSwap([gw_out])])
    c_out = _pair_sum("pair_out", gw_out, s_out, core)
    dcq, dckv, dkr, gw_uq_e, gw_ukv, p_out = _mla_qkv_bwd(dq_b, dk_b, dv_b, cq, ckv, w_uq_g, w_ukv_g, table,
                                                          comm=[_ChipScatter([c_out])])
    dtail, dgq, dgkv = _mla_prep_bwd(proj, dcq, dckv, dkr, mla_q_norm_g, mla_kv_norm_g, table)
    gw_uq = _fold_swapped(gw_uq_e, NOPE, NOPE + ROPE).astype(BF16)
    gw_ukv = gw_ukv.astype(BF16)

    do_a = _heads_major(dmix[:, :OFF_K].astype(BF16), SWA_HEADS).reshape(SWA_KV, SWA_GROUP, s, SWA_DH)
    dq_a, dkp, dkc, dvp, dvc, dbias, dsink, s_uq, s_ukv = _swa_bwd(
        q_a, k_a, v_a, do_a, lse_a, bias, sink_rows, comm=[_PairSwap([gw_uq, gw_ukv])])
    c_uq = _pair_sum("pair_uq", gw_uq, s_uq, core)
    c_ukv = _pair_sum("pair_ukv", gw_ukv, s_ukv, core)
    shift = lambda p: jnp.concatenate([p[:, BLOCK:], jnp.zeros_like(p[:, :BLOCK])], axis=1)
    dk_a, dv_a = dkc + shift(dkp), dvc + shift(dvp)
    drel_t, dsinks = _bias_reduce(dbias.reshape(SWA_HEADS, BLOCK * 2 * BLOCK), onehot, dsink.reshape(SWA_HEADS, BLOCK))
    dproj = jnp.concatenate([_tokens_major(dq_a.reshape(SWA_HEADS, s, SWA_DH)),
                             _tokens_major(dk_a).astype(BF16), _tokens_major(dv_a).astype(BF16), dtail], axis=1)
    gw_in_e = _mm_plain("proj_dw", h1, dproj, "tn", d, IN_EXT, s, F32, tn=TAIL)
    gw_in = _fold_swapped(gw_in_e, OFF_KR, IN_COLS).reshape(NDEV, d // NDEV, IN_COLS).astype(BF16)
    tkt = IN_EXT
    dh1, s_in, p_uq, p_ukv = _mm(
        "proj_dx", dproj, w_in_e, "nt", (s // tm, d // tn, IN_EXT // tkt),
        pl.BlockSpec((tm, tkt), lambda i, j, q: (i, q)), pl.BlockSpec((tn, tkt), lambda i, j, q: (j, q)),
        [jax.ShapeDtypeStruct((s, d), F32)], [row_blk], (tm, tn), _store(F32),
        comm=[_PairSwap([gw_in]), _ChipScatter([c_uq, c_ukv])])
    c_in = _pair_sum("pair_in", gw_in, s_in, core)
    gx, dsc1, dsh1, dga, p_in = _norm_mod_bwd("norm1_bwd", x2d, dh1, dx2, attn_norm_g, sc1,
                                              comm=[_ChipScatter([c_in])])

    small = [jnp.concatenate([dsh1, dsc1, dg1, dsh2, dsc2, dg2], axis=1), dga, dgm, dgf, dgq, dgkv,
             dsinks.reshape(1, SWA_HEADS), drel_t.T.reshape(1, REL_BUCKETS * SWA_HEADS)]
    n_small = sum(a.shape[1] for a in small)
    n_pad = -n_small % 1024
    rows_small = (n_small + n_pad) // 128
    pad = jnp.zeros((1, n_pad), F32)
    pack = lambda parts: jnp.concatenate([p.reshape(1, -1) for p in parts] + [pad], axis=1).reshape(rows_small, 128)
    (small_g,) = _exchange("gather_small", _Gather([pack(small)]))
    small_names = (b_mod, attn_norm_g, mlp_norm_g, final_norm_g, mla_q_norm_g, mla_kv_norm_g, swa_sinks, rel_bias)
    small_m = (m_b_mod, m_attn_norm_g, m_mlp_norm_g, m_final_norm_g, m_mla_q_norm_g, m_mla_kv_norm_g, m_swa_sinks, m_rel_bias)
    small_v = (v_b_mod, v_attn_norm_g, v_mlp_norm_g, v_final_norm_g, v_mla_q_norm_g, v_mla_kv_norm_g, v_swa_sinks, v_rel_bias)
    small_out = _adamw("adamw_small", pack(small_names), small_g, pack(small_m), pack(small_v), parts=True)

    def unpack(flat):
        flat = flat.reshape(1, -1)
        out, off = [], 0
        for a in small_names:
            out.append(flat[:, off:off + a.size].reshape(a.shape))
            off += a.size
        return out

    sg, sd, sm, sv = [unpack(o) for o in small_out]

    dmod_cols = lax.dynamic_slice(small_g.reshape(NDEV, -1), (0, me * nmod), (NDEV, nmod))
    gw_mod = _mod_wgrad(act_all, dmod_cols)
    big = {"w_mod": _adamw("adamw_w_mod", w_mod[0], gw_mod, m_w_mod[0], v_w_mod[0], parts=False)}

    for name, w, p, m, v in (("w_in", w_in, p_in, m_w_in, v_w_in), ("w_uq", w_uq, p_uq, m_w_uq, v_w_uq),
                             ("w_ukv", w_ukv, p_ukv, m_w_ukv, v_w_ukv), ("w_out", w_out, p_out, m_w_out, v_w_out),
                             ("w_ff1", w_ff1, p_ff1, m_w_ff1, v_w_ff1), ("w_ff2", w_ff2, p_ff2, m_w_ff2, v_w_ff2)):
        big[name] = _adamw("adamw_" + name, w[0], p, m[0], v[0], parts=True)

    order = ("w_mod", "b_mod", "attn_norm_g", "w_in", "swa_sinks", "rel_bias", "mla_q_norm_g", "w_uq", "mla_kv_norm_g",
             "w_ukv", "w_out", "mlp_norm_g", "w_ff1", "w_ff2", "final_norm_g")
    small_idx = {"b_mod": 0, "attn_norm_g": 1, "mlp_norm_g": 2, "final_norm_g": 3, "mla_q_norm_g": 4,
                 "mla_kv_norm_g": 5, "swa_sinks": 6, "rel_bias": 7}
    outs = []
    for kind, small_list in enumerate((sg, sd, sm, sv)):
        for name in order:
            outs.append(small_list[small_idx[name]] if name in small_idx else big[name][kind][None])
    return (loss, gx[None], *outs)
```

```python
import functools
import math

import jax
import jax.numpy as jnp
from jax import lax
from jax.experimental import pallas as pl
from jax.experimental.pallas import tpu as pltpu

F32 = jnp.float32
BF16 = jnp.bfloat16

NDEV = 8
EPS = 1e-6
BLOCK = 128
SWA_HEADS, SWA_KV, SWA_DH, SWA_GROUP = 16, 2, 64, 8
REL_BUCKETS, REL_MAX_DIST = 32, 128
MLA_H, Q_RANK, KV_RANK, NOPE, ROPE, VDIM = 8, 384, 128, 128, 64, 128
ROPE_THETA = 10000.0
OFF_K, OFF_V, OFF_CQ, OFF_CKV, OFF_KR, IN_COLS = 1024, 1152, 1280, 1664, 1792, 1856
IN_EXT = IN_COLS + ROPE
TAIL0, TAIL = OFF_CQ, IN_EXT - OFF_CQ
QW = NOPE + 2 * ROPE
MIX = SWA_HEADS * SWA_DH + MLA_H * VDIM
MLA_SCALE = (NOPE + ROPE) ** -0.5
SWA_SCALE = SWA_DH ** -0.5

ADAM_LR, ADAM_B1, ADAM_B2, ADAM_EPS, ADAM_WD, ADAM_STEP = 0.001, 0.9, 0.999, 1e-08, 0.01, 10

VMEM_LIMIT = 52 * 1024 * 1024
ROW_TILE = 256
MM_TM, MM_TN, MM_TK = 1024, 1024, 2048
ATT_T = 512
MLA_HB = 2
ADAM_ELEMS = 128 * 1024


MESH_ID = pl.DeviceIdType.MESH


def _place():
    x, y, c = lax.axis_index("x"), lax.axis_index("y"), lax.axis_index("c")
    return x, y, c, 2 * x + y


def _chip(x, y, k):
    return (1 - x if k & 2 else x, 1 - y if k & 1 else y)


def _dma_sems(*counts):
    return [pltpu.SemaphoreType.DMA((n,)) for n in counts]


class _Gather:
    def __init__(self, arrays):
        self.arrays = list(arrays)
        n = len(self.arrays)
        self.out_shape = [jax.ShapeDtypeStruct((NDEV,) + a.shape, a.dtype) for a in self.arrays]
        self.sems = _dma_sems(7 * n, 7 * n, n)

    def _copy(self, sems, a, k, src, dst, to):
        return pltpu.make_async_remote_copy(src_ref=src, dst_ref=dst, send_sem=sems[0].at[7 * a + k],
                                            recv_sem=sems[1].at[7 * a + k], device_id=to, device_id_type=MESH_ID)

    def start(self, ins, outs, sems):
        x, y, c, q = _place()
        me = 2 * q + c
        for a in range(len(ins)):
            pltpu.make_async_copy(ins[a], outs[a].at[me], sems[2].at[a]).start()
            self._copy(sems, a, 0, ins[a], outs[a].at[me], (x, y, 1 - c)).start()
            for k in (1, 2, 3):
                self._copy(sems, a, k, ins[a], outs[a].at[me], (*_chip(x, y, k), c)).start()

    def finish(self, ins, outs, sems):
        x, y, c, q = _place()
        me, sib = 2 * q + c, (x, y, 1 - c)
        n = len(ins)
        for k in (1, 2, 3):
            for a in range(n):
                blk = outs[a].at[2 * (q ^ k) + c]
                self._copy(sems, a, k, ins[a], blk, (*_chip(x, y, k), c)).wait_recv()
                self._copy(sems, a, 3 + k, blk, blk, sib).start()
        for a in range(n):
            self._copy(sems, a, 0, ins[a], outs[a].at[2 * q + 1 - c], sib).wait_recv()
            for k in (1, 2, 3):
                blk = outs[a].at[2 * (q ^ k) + 1 - c]
                self._copy(sems, a, 3 + k, blk, blk, sib).wait_recv()
        for a in range(n):
            for k in range(7):
                self._copy(sems, a, k, ins[a], outs[a].at[me], sib).wait_send()
            pltpu.make_async_copy(ins[a], outs[a].at[me], sems[2].at[a]).wait()


class _PairSwap:
    def __init__(self, arrays):
        self.arrays = list(arrays)
        n = len(self.arrays)
        self.out_shape = [jax.ShapeDtypeStruct((NDEV // 2,) + a.shape[1:], a.dtype) for a in self.arrays]
        self.sems = _dma_sems(4 * n, 4 * n)

    def _copy(self, sems, a, p, src, dst, to):
        return pltpu.make_async_remote_copy(src_ref=src, dst_ref=dst, send_sem=sems[0].at[4 * a + p],
                                            recv_sem=sems[1].at[4 * a + p], device_id=to, device_id_type=MESH_ID)

    def start(self, ins, outs, sems):
        x, y, c, _ = _place()
        for a in range(len(ins)):
            for p in range(4):
                self._copy(sems, a, p, ins[a].at[2 * p + 1 - c], outs[a].at[p], (x, y, 1 - c)).start()

    def finish(self, ins, outs, sems):
        x, y, c, _ = _place()
        for a in range(len(ins)):
            for p in range(4):
                cp = self._copy(sems, a, p, ins[a].at[2 * p + 1 - c], outs[a].at[p], (x, y, 1 - c))
                cp.wait_recv()
                cp.wait_send()


class _ChipScatter:
    def __init__(self, arrays):
        self.arrays = list(arrays)
        n = len(self.arrays)
        self.out_shape = [jax.ShapeDtypeStruct(a.shape, a.dtype) for a in self.arrays]
        self.sems = _dma_sems(3 * n, 3 * n, n)

    def _copy(self, sems, a, k, src, dst, to):
        return pltpu.make_async_remote_copy(src_ref=src, dst_ref=dst, send_sem=sems[0].at[3 * a + k - 1],
                                            recv_sem=sems[1].at[3 * a + k - 1], device_id=to, device_id_type=MESH_ID)

    def start(self, ins, outs, sems):
        x, y, c, q = _place()
        for a in range(len(ins)):
            pltpu.make_async_copy(ins[a].at[q], outs[a].at[q], sems[2].at[a]).start()
            for k in (1, 2, 3):
                self._copy(sems, a, k, ins[a].at[q ^ k], outs[a].at[q], (*_chip(x, y, k), c)).start()

    def finish(self, ins, outs, sems):
        x, y, c, q = _place()
        for a in range(len(ins)):
            for k in (1, 2, 3):
                cp = self._copy(sems, a, k, ins[a].at[q ^ k], outs[a].at[q ^ k], (*_chip(x, y, k), c))
                cp.wait_recv()
                cp.wait_send()
            pltpu.make_async_copy(ins[a].at[q], outs[a].at[q], sems[2].at[a]).wait()


def _call(body, **kw):
    return pl.pallas_call(body, **kw)


def _pcall(body, comm=None, **kw):
    if not comm:
        return _call(body, **kw)
    grid = kw["grid"]
    in_specs, out_specs, out_shape = list(kw["in_specs"]), list(kw["out_specs"]), list(kw["out_shape"])
    scratch = list(kw.get("scratch_shapes", ()))
    n_in, n_out, n_scr = len(in_specs), len(out_shape), len(scratch)
    n_cin = [len(j.arrays) for j in comm]
    n_sem = [len(j.sems) for j in comm]
    n = sum(n_cin)
    hbm = pl.BlockSpec(memory_space=pltpu.HBM)

    def carried(*refs):
        ins, cins = refs[:n_in], refs[n_in:n_in + n]
        outs, couts = refs[n_in + n:n_in + n + n_out], refs[n_in + n + n_out:n_in + 2 * n + n_out]
        scr, sems = refs[n_in + 2 * n + n_out:n_in + 2 * n + n_out + n_scr], refs[n_in + 2 * n + n_out + n_scr:]
        ids = [pl.program_id(ax) for ax in range(len(grid))]
        first = functools.reduce(jnp.logical_and, [i == 0 for i in ids])
        last = functools.reduce(jnp.logical_and, [i == g - 1 for i, g in zip(ids, grid)])

        def each(method):
            ai = si = 0
            for job, na, ns in zip(comm, n_cin, n_sem):
                getattr(job, method)(cins[ai:ai + na], couts[ai:ai + na], sems[si:si + ns])
                ai, si = ai + na, si + ns

        @pl.when(first)
        def _():
            each("start")

        body(*ins, *outs, *scr)

        @pl.when(last)
        def _():
            each("finish")

    kw.update(in_specs=in_specs + [hbm] * n, out_specs=out_specs + [hbm] * n,
              out_shape=out_shape + [o for j in comm for o in j.out_shape],
              scratch_shapes=scratch + [sm for j in comm for sm in j.sems],
              compiler_params=_cparams(("arbitrary",) * len(grid)))
    call = _call(carried, **kw)
    return lambda *args: call(*args, *[a for j in comm for a in j.arrays])


def _cparams(sem):
    return pltpu.CompilerParams(dimension_semantics=sem, vmem_limit_bytes=VMEM_LIMIT)


def _pick(n, pref, align):
    if n <= pref:
        return n
    t = (pref // align) * align
    while t >= align:
        if n % t == 0:
            return t
        t -= align
    return n


def _split3(x):
    a = x.astype(BF16)
    r = x - a.astype(F32)
    b = r.astype(BF16)
    c = (r - b.astype(F32)).astype(BF16)
    return a, b, c


def _exchange(name, job):
    n = len(job.arrays)

    def body(*refs):
        ins, outs, sems = refs[:n], refs[n:2 * n], refs[2 * n:]
        job.start(ins, outs, sems)
        job.finish(ins, outs, sems)

    hbm = pl.BlockSpec(memory_space=pltpu.HBM)
    return _call(body, name=name, out_shape=job.out_shape, in_specs=[hbm] * n, out_specs=[hbm] * n,
                 scratch_shapes=job.sems)(*job.arrays)


def _pair_sum(name, g, r, core):
    _, rr, cc = g.shape
    tr = rr if rr * cc <= 4 * ADAM_ELEMS else _pick(rr, max(16, 4 * ADAM_ELEMS // cc // 16 * 16), 16)

    def body(g_ref, r_ref, c_ref, o_ref):
        north = c_ref[:, 0:1] > 0.5
        mine = jnp.where(north, g_ref[1].astype(F32), g_ref[0].astype(F32))
        o_ref[...] = (mine + r_ref[...].astype(F32)).astype(o_ref.dtype)

    return _pcall(
        body, name=name, grid=(NDEV // 2, rr // tr),
        in_specs=[pl.BlockSpec((None, 2, tr, cc), lambda p, i: (p, 0, i, 0)),
                  pl.BlockSpec((None, tr, cc), lambda p, i: (p, i, 0)), pl.BlockSpec((1, 128), lambda p, i: (0, 0))],
        out_specs=pl.BlockSpec((None, tr, cc), lambda p, i: (p, i, 0)),
        out_shape=jax.ShapeDtypeStruct((NDEV // 2, rr, cc), g.dtype),
        compiler_params=_cparams(("parallel", "parallel")))(g.reshape(NDEV // 2, 2, rr, cc), r, core)


_DIMS = {"nn": (((1,), (0,)), ((), ())), "nt": (((1,), (1,)), ((), ())), "tn": (((0,), (0,)), ((), ()))}


def _mm(name, a, b, kind, grid, a_spec, b_spec, out_shape, out_specs, acc_shape, epilogue,
        extras=(), extra_specs=(), comm=None):
    nk, ne, no = grid[2], len(extras), len(out_shape)

    def body(*refs):
        a_ref, b_ref = refs[0], refs[1]
        ex, outs = refs[2:2 + ne], refs[2 + ne:2 + ne + no]
        part = lax.dot_general(a_ref[...].astype(BF16), b_ref[...].astype(BF16), _DIMS[kind],
                               preferred_element_type=F32)
        if nk == 1:
            epilogue(part, ex, outs)
            return
        acc = refs[-1]
        k = pl.program_id(2)

        @pl.when(k == 0)
        def _():
            acc[...] = part

        @pl.when(jnp.logical_and(k > 0, k < nk - 1))
        def _():
            acc[...] += part

        @pl.when(k == nk - 1)
        def _():
            epilogue(acc[...] + part, ex, outs)

    return _pcall(
        body, comm=comm, name=name, grid=grid, in_specs=[a_spec, b_spec, *extra_specs], out_specs=out_specs,
        out_shape=out_shape, scratch_shapes=[pltpu.VMEM(acc_shape, F32)] if nk > 1 else [],
        compiler_params=_cparams(("parallel", "parallel", "arbitrary")),
    )(a, b, *extras)


def _store(dtype):
    def epi(acc, ex, outs):
        outs[0][...] = acc.astype(dtype)
    return epi


def _mm_plain(name, a, b, kind, m, n, k, out_dtype, tm=None, tn=None, tk=None):
    tm = _pick(m, tm or MM_TM, 128)
    tn = _pick(n, tn or MM_TN, 128)
    tk = _pick(k, tk or MM_TK, 128)
    a_spec = pl.BlockSpec((tk, tm), lambda i, j, q: (q, i)) if kind == "tn" else pl.BlockSpec((tm, tk), lambda i, j, q: (i, q))
    b_spec = pl.BlockSpec((tn, tk), lambda i, j, q: (j, q)) if kind == "nt" else pl.BlockSpec((tk, tn), lambda i, j, q: (q, j))
    return _mm(name, a, b, kind, (m // tm, n // tn, k // tk), a_spec, b_spec,
               [jax.ShapeDtypeStruct((m, n), out_dtype)], [pl.BlockSpec((tm, tn), lambda i, j, q: (i, j))],
               (tm, tn), _store(out_dtype))[0]


def _row(ts, d):
    return pl.BlockSpec((ts, d), lambda i: (i, 0))


def _vec(d):
    return pl.BlockSpec((1, d), lambda i: (0, 0))


def _norm_mod(name, x, gain, sc, sh):
    s, d = x.shape
    ts = _pick(s, ROW_TILE, 16)

    def body(x_ref, g_ref, sc_ref, sh_ref, h_ref):
        xv = x_ref[...]
        r = lax.rsqrt(jnp.mean(xv * xv, axis=-1, keepdims=True) + EPS)
        h_ref[...] = ((xv * r) * g_ref[...] * (1.0 + sc_ref[...]) + sh_ref[...]).astype(BF16)

    return _pcall(body, name=name, grid=(s // ts,), in_specs=[_row(ts, d), _vec(d), _vec(d), _vec(d)],
                  out_specs=_row(ts, d), out_shape=jax.ShapeDtypeStruct((s, d), BF16),
                  compiler_params=_cparams(("parallel",)))(x, gain, sc, sh)


def _loss_head(x3, tgt, y2, gf, g2):
    s, d = x3.shape
    ts = _pick(s, ROW_TILE, 16)

    def body(x_ref, t_ref, y_ref, gf_ref, g2_ref, dx_ref, dy_ref, loss_ref, dgf_ref, dg2_ref):
        @pl.when(pl.program_id(0) == 0)
        def _():
            loss_ref[...] = jnp.zeros_like(loss_ref)
            dgf_ref[...] = jnp.zeros_like(dgf_ref)
            dg2_ref[...] = jnp.zeros_like(dg2_ref)

        xv = x_ref[...]
        r = lax.rsqrt(jnp.mean(xv * xv, axis=-1, keepdims=True) + EPS)
        xn = xv * r
        err = xn * gf_ref[...] - t_ref[...]
        loss_ref[...] += 0.5 * jnp.sum(jnp.mean(err * err, axis=-1, keepdims=True), axis=0, keepdims=True)
        dout = err * (1.0 / d)
        dgf_ref[...] += jnp.sum(dout * xn, axis=0, keepdims=True)
        dxn = dout * gf_ref[...]
        dx = r * (dxn - xn * jnp.mean(dxn * xn, axis=-1, keepdims=True))
        dx_ref[...] = dx
        dy_ref[...] = (dx * g2_ref[...]).astype(BF16)
        dg2_ref[...] += jnp.sum(dx * y_ref[...], axis=0, keepdims=True)

    one = pl.BlockSpec((1, 1), lambda i: (0, 0))
    return _pcall(
        body, name="loss_head", grid=(s // ts,),
        in_specs=[_row(ts, d), _row(ts, d), _row(ts, d), _vec(d), _vec(d)],
        out_specs=[_row(ts, d), _row(ts, d), one, _vec(d), _vec(d)],
        out_shape=[jax.ShapeDtypeStruct((s, d), F32), jax.ShapeDtypeStruct((s, d), BF16),
                   jax.ShapeDtypeStruct((1, 1), F32), jax.ShapeDtypeStruct((1, d), F32),
                   jax.ShapeDtypeStruct((1, d), F32)],
        compiler_params=_cparams(("arbitrary",)))(x3, tgt, y2, gf, g2)


def _norm_mod_bwd(name, x, dh, dres, gain, sc, y_prev=None, gate=None, comm=None):
    s, d = x.shape
    ts = _pick(s, ROW_TILE, 16)
    gated = y_prev is not None

    def body(*refs):
        if gated:
            x_ref, dh_ref, dr_ref, g_ref, sc_ref, y_ref, gt_ref, dx_ref, dy_ref, dsc_ref, dsh_ref, dg_ref, dgt_ref = refs
        else:
            x_ref, dh_ref, dr_ref, g_ref, sc_ref, dx_ref, dsc_ref, dsh_ref, dg_ref = refs

        @pl.when(pl.program_id(0) == 0)
        def _():
            dsc_ref[...] = jnp.zeros_like(dsc_ref)
            dsh_ref[...] = jnp.zeros_like(dsh_ref)
            dg_ref[...] = jnp.zeros_like(dg_ref)
            if gated:
                dgt_ref[...] = jnp.zeros_like(dgt_ref)

        xv, dhv = x_ref[...], dh_ref[...]
        r = lax.rsqrt(jnp.mean(xv * xv, axis=-1, keepdims=True) + EPS)
        xn = xv * r
        dsc_ref[...] += jnp.sum(dhv * (xn * g_ref[...]), axis=0, keepdims=True)
        dsh_ref[...] += jnp.sum(dhv, axis=0, keepdims=True)
        da = dhv * (1.0 + sc_ref[...])
        dg_ref[...] += jnp.sum(da * xn, axis=0, keepdims=True)
        dxn = da * g_ref[...]
        dx = dr_ref[...] + r * (dxn - xn * jnp.mean(dxn * xn, axis=-1, keepdims=True))
        dx_ref[...] = dx
        if gated:
            dy_ref[...] = (dx * gt_ref[...]).astype(BF16)
            dgt_ref[...] += jnp.sum(dx * y_ref[...], axis=0, keepdims=True)

    ins = [x, dh, dres, gain, sc] + ([y_prev, gate] if gated else [])
    in_specs = [_row(ts, d)] * 3 + [_vec(d)] * 2 + ([_row(ts, d), _vec(d)] if gated else [])
    vec_out = jax.ShapeDtypeStruct((1, d), F32)
    out_shape = [jax.ShapeDtypeStruct((s, d), F32)] + ([jax.ShapeDtypeStruct((s, d), BF16)] if gated else [])
    out_shape += [vec_out] * (4 if gated else 3)
    out_specs = [_row(ts, d)] * (2 if gated else 1) + [_vec(d)] * (4 if gated else 3)
    return _pcall(body, comm=comm, name=name, grid=(s // ts,), in_specs=in_specs, out_specs=out_specs,
                  out_shape=out_shape, compiler_params=_cparams(("arbitrary",)))(*ins)


def _dot3(a, b, dims):
    a1, a2, _ = _split3(a)
    b1, b2, _ = _split3(b)
    dot = functools.partial(lax.dot_general, dimension_numbers=dims, preferred_element_type=F32)
    return dot(a1, b1) + (dot(a1, b2) + dot(a2, b1))


def _mod_fwd(c_all, w, b_cols, comm=None):
    nb, d = c_all.shape
    n = w.shape[1]
    tk = _pick(d, 512, 128)
    nk = d // tk

    def body(c_ref, w_ref, b_ref, act_ref, out_ref):
        k = pl.program_id(0)
        cv = c_ref[...]
        act = cv * (1.0 / (1.0 + jnp.exp(-cv)))
        act_ref[...] = act

        @pl.when(k == 0)
        def _():
            out_ref[...] = jnp.broadcast_to(b_ref[...], out_ref.shape)

        out_ref[...] += _dot3(act, w_ref[...], _DIMS["nn"])

    return _pcall(
        body, comm=comm, name="mod_fwd", grid=(nk,),
        in_specs=[pl.BlockSpec((nb, tk), lambda k: (0, k)), pl.BlockSpec((tk, n), lambda k: (k, 0)),
                  pl.BlockSpec((1, n), lambda k: (0, 0))],
        out_specs=[pl.BlockSpec((nb, tk), lambda k: (0, k)), pl.BlockSpec((nb, n), lambda k: (0, 0))],
        out_shape=[jax.ShapeDtypeStruct((nb, d), F32), jax.ShapeDtypeStruct((nb, n), F32)],
        compiler_params=_cparams(("arbitrary",)))(c_all, w, b_cols)


def _mod_wgrad(act_all, dmod_cols):
    nb, d = act_all.shape
    n = dmod_cols.shape[1]
    tm = _pick(d, 512, 128)

    def body(a_ref, d_ref, o_ref):
        o_ref[...] = _dot3(a_ref[...], d_ref[...], _DIMS["tn"])

    return _pcall(
        body, name="mod_wgrad", grid=(d // tm,),
        in_specs=[pl.BlockSpec((nb, tm), lambda i: (0, i)), pl.BlockSpec((nb, n), lambda i: (0, 0))],
        out_specs=pl.BlockSpec((tm, n), lambda i: (i, 0)), out_shape=jax.ShapeDtypeStruct((d, n), F32),
        compiler_params=_cparams(("parallel",)))(act_all, dmod_cols)


def _bias_expand(rel_t, onehot_t):
    h, _ = rel_t.shape
    n = onehot_t.shape[1]

    def body(r_ref, o_ref, out_ref):
        a, b, c = _split3(r_ref[...])
        dot = functools.partial(lax.dot_general, dimension_numbers=_DIMS["nn"], preferred_element_type=F32)
        oh = o_ref[...]
        out_ref[...] = dot(a, oh) + (dot(b, oh) + dot(c, oh))

    full = lambda shp: pl.BlockSpec(shp, lambda: (0,) * len(shp))
    return _pcall(body, name="bias_expand", in_specs=[full(rel_t.shape), full(onehot_t.shape)],
                  out_specs=full((h, n)), out_shape=jax.ShapeDtypeStruct((h, n), F32),
                  compiler_params=pltpu.CompilerParams(vmem_limit_bytes=VMEM_LIMIT))(rel_t, onehot_t)


def _bias_reduce(dbias, onehot, dsink_rows):
    h, n = dbias.shape

    def body(d_ref, o_ref, s_ref, out_ref, so_ref):
        a, b, c = _split3(d_ref[...])
        dot = functools.partial(lax.dot_general, dimension_numbers=_DIMS["nn"], preferred_element_type=F32)
        oh = o_ref[...]
        out_ref[...] = dot(a, oh) + (dot(b, oh) + dot(c, oh))
        so_ref[...] = jnp.sum(s_ref[...], axis=-1, keepdims=True)

    full = lambda shp: pl.BlockSpec(shp, lambda: (0,) * len(shp))
    return _pcall(body, name="bias_reduce", in_specs=[full(dbias.shape), full(onehot.shape), full(dsink_rows.shape)],
                  out_specs=[full((h, REL_BUCKETS)), full((h, 1))],
                  out_shape=[jax.ShapeDtypeStruct((h, REL_BUCKETS), F32), jax.ShapeDtypeStruct((h, 1), F32)],
                  compiler_params=pltpu.CompilerParams(vmem_limit_bytes=VMEM_LIMIT))(dbias, onehot, dsink_rows)


def _swa_specs(s):
    rows = SWA_GROUP * BLOCK
    q_spec = pl.BlockSpec((None, SWA_GROUP, BLOCK, SWA_DH), lambda g, n: (g, 0, n, 0))
    kv_prev = pl.BlockSpec((None, BLOCK, SWA_DH), lambda g, n: (g, jnp.maximum(n - 1, 0), 0))
    kv_cur = pl.BlockSpec((None, BLOCK, SWA_DH), lambda g, n: (g, n, 0))
    bias_spec = pl.BlockSpec((None, rows, 2 * BLOCK), lambda g, n: (g, 0, 0))
    col_spec = pl.BlockSpec((None, rows, 1), lambda g, n: (g, 0, 0))
    lse_spec = pl.BlockSpec((None, None, rows, 1), lambda g, n: (g, n, 0, 0))
    return rows, q_spec, kv_prev, kv_cur, bias_spec, col_spec, lse_spec


def _swa_scores(q_ref, kp_ref, kc_ref, bias_ref, n):
    rows = SWA_GROUP * BLOCK
    q = q_ref[...].reshape(rows, SWA_DH)
    kb = jnp.concatenate([kp_ref[...], kc_ref[...]], axis=0)
    s = lax.dot_general(q, kb, _DIMS["nt"], preferred_element_type=F32) * SWA_SCALE + bias_ref[...]
    col = lax.broadcasted_iota(jnp.int32, s.shape, 1)
    s = jnp.where(jnp.logical_and(n == 0, col < BLOCK), -jnp.inf, s)
    return q, kb, s


def _swa_fwd(q, k, v, bias, sink_rows, comm=None):
    s = q.shape[2]
    nb = s // BLOCK
    rows, q_spec, kv_prev, kv_cur, bias_spec, col_spec, lse_spec = _swa_specs(s)

    def body(q_ref, kp_ref, kc_ref, vp_ref, vc_ref, bias_ref, sink_ref, o_ref, lse_ref):
        n = pl.program_id(1)
        _, _, sc = _swa_scores(q_ref, kp_ref, kc_ref, bias_ref, n)
        sink = sink_ref[...]
        m = jnp.maximum(jnp.max(sc, axis=-1, keepdims=True), sink)
        p = jnp.exp(sc - m)
        den = jnp.sum(p, axis=-1, keepdims=True) + jnp.exp(sink - m)
        p = p / den
        vb = jnp.concatenate([vp_ref[...], vc_ref[...]], axis=0)
        o = lax.dot_general(p.astype(BF16), vb, _DIMS["nn"], preferred_element_type=F32)
        o_ref[...] = o.reshape(SWA_GROUP, BLOCK, SWA_DH).astype(BF16)
        lse_ref[...] = m + jnp.log(den)

    return _pcall(
        body, comm=comm, name="swa_fwd", grid=(SWA_KV, nb),
        in_specs=[q_spec, kv_prev, kv_cur, kv_prev, kv_cur, bias_spec, col_spec],
        out_specs=[q_spec, lse_spec],
        out_shape=[jax.ShapeDtypeStruct(q.shape, BF16), jax.ShapeDtypeStruct((SWA_KV, nb, rows, 1), F32)],
        compiler_params=_cparams(("parallel", "parallel")))(q, k, k, v, v, bias, sink_rows)


def _swa_bwd(q, k, v, do, lse, bias, sink_rows, comm=None):
    s = q.shape[2]
    nb = s // BLOCK
    rows, q_spec, kv_prev, kv_cur, bias_spec, col_spec, lse_spec = _swa_specs(s)

    def body(q_ref, kp_ref, kc_ref, vp_ref, vc_ref, do_ref, lse_ref, bias_ref, sink_ref,
             dq_ref, dkp_ref, dkc_ref, dvp_ref, dvc_ref, dbias_ref, dsink_ref):
        n = pl.program_id(1)

        @pl.when(n == 0)
        def _():
            dbias_ref[...] = jnp.zeros_like(dbias_ref)
            dsink_ref[...] = jnp.zeros_like(dsink_ref)

        qv, kb, sc = _swa_scores(q_ref, kp_ref, kc_ref, bias_ref, n)
        lse_v = lse_ref[...]
        p = jnp.exp(sc - lse_v)
        p_sink = jnp.exp(sink_ref[...] - lse_v)
        dov = do_ref[...].reshape(rows, SWA_DH)
        vb = jnp.concatenate([vp_ref[...], vc_ref[...]], axis=0)
        dp = lax.dot_general(dov, vb, _DIMS["nt"], preferred_element_type=F32)
        delta = jnp.sum(p * dp, axis=-1, keepdims=True)
        ds = p * (dp - delta)
        dbias_ref[...] += ds
        dsink_ref[...] += -p_sink * delta
        dsb = (ds * SWA_SCALE).astype(BF16)
        dq = lax.dot_general(dsb, kb, _DIMS["nn"], preferred_element_type=F32)
        dq_ref[...] = dq.reshape(SWA_GROUP, BLOCK, SWA_DH).astype(BF16)
        dk = lax.dot_general(dsb, qv, _DIMS["tn"], preferred_element_type=F32)
        dv = lax.dot_general(p.astype(BF16), dov, _DIMS["tn"], preferred_element_type=F32)
        dkp_ref[...] = dk[:BLOCK]
        dkc_ref[...] = dk[BLOCK:]
        dvp_ref[...] = dv[:BLOCK]
        dvc_ref[...] = dv[BLOCK:]

    kv_out = jax.ShapeDtypeStruct((SWA_KV, s, SWA_DH), F32)
    return _pcall(
        body, comm=comm, name="swa_bwd", grid=(SWA_KV, nb),
        in_specs=[q_spec, kv_prev, kv_cur, kv_prev, kv_cur, q_spec, lse_spec, bias_spec, col_spec],
        out_specs=[q_spec, kv_cur, kv_cur, kv_cur, kv_cur, bias_spec, col_spec],
        out_shape=[jax.ShapeDtypeStruct(q.shape, BF16), kv_out, kv_out, kv_out, kv_out,
                   jax.ShapeDtypeStruct(bias.shape, F32), jax.ShapeDtypeStruct(sink_rows.shape, F32)],
        compiler_params=_cparams(("arbitrary", "arbitrary")))(q, k, k, v, v, do, lse, bias, sink_rows)


def _rope_slab(slab, table):
    t = slab * table
    return t + pltpu.roll(t, ROPE, 1)


def _low_lanes(v):
    lane = lax.broadcasted_iota(jnp.int32, v.shape, 1)
    return jnp.where(lane < ROPE, v, 0.0)


def _rms(xv, g):
    r = lax.rsqrt(jnp.mean(xv * xv, axis=-1, keepdims=True) + EPS)
    return xv * r, r


def _mla_prep(proj, gq, gkv, table):
    s = proj.shape[0]
    ts = _pick(s, ROW_TILE, 16)

    def body(p_ref, gq_ref, gkv_ref, t_ref, cq_ref, ckv_ref, kr_ref):
        xq, _ = _rms(p_ref[:, 0:Q_RANK], None)
        cq_ref[...] = (xq * gq_ref[...]).astype(BF16)
        xkv, _ = _rms(p_ref[:, Q_RANK:Q_RANK + KV_RANK], None)
        ckv_ref[...] = (xkv * gkv_ref[...]).astype(BF16)
        kr_ref[...] = _low_lanes(_rope_slab(p_ref[:, Q_RANK + KV_RANK:TAIL], t_ref[...]))

    return _pcall(
        body, name="mla_prep", grid=(s // ts,),
        in_specs=[pl.BlockSpec((ts, TAIL), lambda i: (i, TAIL0 // TAIL)), _vec(Q_RANK), _vec(KV_RANK), _row(ts, 2 * ROPE)],
        out_specs=[_row(ts, Q_RANK), _row(ts, KV_RANK), _row(ts, 2 * ROPE)],
        out_shape=[jax.ShapeDtypeStruct((s, Q_RANK), BF16), jax.ShapeDtypeStruct((s, KV_RANK), BF16),
                   jax.ShapeDtypeStruct((s, 2 * ROPE), F32)],
        compiler_params=_cparams(("parallel",)))(proj, gq, gkv, table)


def _mla_prep_bwd(proj, dcq, dckv, dkr, gq, gkv, table):
    s = proj.shape[0]
    ts = _pick(s, ROW_TILE, 16)

    def norm_bwd(xv, dy, g):
        xn, r = _rms(xv, None)
        dg = jnp.sum(dy * xn, axis=0, keepdims=True)
        dxn = dy * g
        return r * (dxn - xn * jnp.mean(dxn * xn, axis=-1, keepdims=True)), dg

    def body(p_ref, dcq_ref, dckv_ref, dkr_ref, gq_ref, gkv_ref, t_ref, dt_ref, dgq_ref, dgkv_ref):
        @pl.when(pl.program_id(0) == 0)
        def _():
            dgq_ref[...] = jnp.zeros_like(dgq_ref)
            dgkv_ref[...] = jnp.zeros_like(dgkv_ref)

        dxq, dgq = norm_bwd(p_ref[:, 0:Q_RANK], dcq_ref[...], gq_ref[...])
        dxkv, dgkv = norm_bwd(p_ref[:, Q_RANK:Q_RANK + KV_RANK], dckv_ref[...], gkv_ref[...])
        dgq_ref[...] += dgq
        dgkv_ref[...] += dgkv
        d = _low_lanes(dkr_ref[...])
        dslab = (d + pltpu.roll(d, ROPE, 1)) * t_ref[...]
        dt_ref[:, 0:Q_RANK] = dxq.astype(BF16)
        dt_ref[:, Q_RANK:Q_RANK + KV_RANK] = dxkv.astype(BF16)
        dt_ref[:, Q_RANK + KV_RANK:TAIL] = dslab.astype(BF16)

    return _pcall(
        body, name="mla_prep_bwd", grid=(s // ts,),
        in_specs=[pl.BlockSpec((ts, TAIL), lambda i: (i, TAIL0 // TAIL)), _row(ts, Q_RANK), _row(ts, KV_RANK),
                  _row(ts, 2 * ROPE), _vec(Q_RANK), _vec(KV_RANK), _row(ts, 2 * ROPE)],
        out_specs=[_row(ts, TAIL), _vec(Q_RANK), _vec(KV_RANK)],
        out_shape=[jax.ShapeDtypeStruct((s, TAIL), BF16), jax.ShapeDtypeStruct((1, Q_RANK), F32),
                   jax.ShapeDtypeStruct((1, KV_RANK), F32)],
        compiler_params=_cparams(("arbitrary",)))(proj, dcq, dckv, dkr, gq, gkv, table)


def _head_specs(ts):
    tok = lambda w: pl.BlockSpec((ts, w), lambda h, i: (i, 0))
    head = lambda w: pl.BlockSpec((None, ts, w), lambda h, i: (h, i, 0))
    wgt = lambda r, c: pl.BlockSpec((None, r, c), lambda h, i: (h, 0, 0))
    return tok, head, wgt


def _mla_qkv(cq, ckv, kr, wq, wkv, table):
    s = cq.shape[0]
    ts = _pick(s, 2 * ROW_TILE, 16)
    tok, head, wgt = _head_specs(ts)

    def body(cq_ref, ckv_ref, kr_ref, wq_ref, wkv_ref, t_ref, q_ref, k_ref, v_ref):
        qf = lax.dot_general(cq_ref[...], wq_ref[...], _DIMS["nn"], preferred_element_type=F32)
        q_ref[:, 0:NOPE] = qf[:, 0:NOPE].astype(BF16)
        q_ref[:, NOPE:QW] = _rope_slab(qf[:, NOPE:QW], t_ref[...]).astype(BF16)
        kv = lax.dot_general(ckv_ref[...], wkv_ref[...], _DIMS["nn"], preferred_element_type=F32)
        k_ref[:, 0:NOPE] = kv[:, 0:NOPE].astype(BF16)
        k_ref[:, NOPE:QW] = kr_ref[...].astype(BF16)
        v_ref[:, 0:VDIM] = kv[:, NOPE:NOPE + VDIM].astype(BF16)
        lane = lax.broadcasted_iota(jnp.int32, (ts, VDIM), 1)
        v_ref[:, VDIM:2 * VDIM] = jnp.where(lane == 0, 1.0, 0.0).astype(BF16)

    return _pcall(
        body, name="mla_qkv", grid=(MLA_H, s // ts),
        in_specs=[tok(Q_RANK), tok(KV_RANK), tok(2 * ROPE), wgt(Q_RANK, QW), wgt(KV_RANK, NOPE + VDIM), tok(2 * ROPE)],
        out_specs=[head(QW), head(QW), head(2 * VDIM)],
        out_shape=[jax.ShapeDtypeStruct((MLA_H, s, QW), BF16), jax.ShapeDtypeStruct((MLA_H, s, QW), BF16),
                   jax.ShapeDtypeStruct((MLA_H, s, 2 * VDIM), BF16)],
        compiler_params=_cparams(("parallel", "parallel")))(cq, ckv, kr, wq, wkv, table)


def _mla_qkv_bwd(dq, dk, dv, cq, ckv, wq, wkv, table, comm=None):
    s = cq.shape[0]
    ts = _pick(s, 2 * ROW_TILE, 16)
    tok, head, wgt = _head_specs(ts)
    whole = lambda w: pl.BlockSpec((s, w), lambda h, i: (0, 0))

    def body(dq_ref, dk_ref, dv_ref, cq_ref, ckv_ref, wq_ref, wkv_ref, t_ref,
             dcq_ref, dckv_ref, dkr_ref, gwq_ref, gwkv_ref):
        h, i = pl.program_id(0), pl.program_id(1)
        rows = pl.ds(pl.multiple_of(i * ts, ts), ts)
        d = dq_ref[:, NOPE:QW]
        dslab = (d + pltpu.roll(d, ROPE, 1)) * t_ref[...]
        dqe = jnp.concatenate([dq_ref[:, 0:NOPE], dslab], axis=1).astype(BF16)
        dkv = jnp.concatenate([dk_ref[:, 0:NOPE], dv_ref[...]], axis=1).astype(BF16)
        dcq = lax.dot_general(dqe, wq_ref[...], _DIMS["nt"], preferred_element_type=F32)
        dckv = lax.dot_general(dkv, wkv_ref[...], _DIMS["nt"], preferred_element_type=F32)
        gwq = lax.dot_general(cq_ref[...], dqe, _DIMS["tn"], preferred_element_type=F32)
        gwkv = lax.dot_general(ckv_ref[...], dkv, _DIMS["tn"], preferred_element_type=F32)
        dkr = dk_ref[:, NOPE:QW]

        @pl.when(h == 0)
        def _():
            dcq_ref[rows, :] = dcq
            dckv_ref[rows, :] = dckv
            dkr_ref[rows, :] = dkr

        @pl.when(h > 0)
        def _():
            dcq_ref[rows, :] += dcq
            dckv_ref[rows, :] += dckv
            dkr_ref[rows, :] += dkr

        @pl.when(i == 0)
        def _():
            gwq_ref[...] = gwq
            gwkv_ref[...] = gwkv

        @pl.when(i > 0)
        def _():
            gwq_ref[...] += gwq
            gwkv_ref[...] += gwkv

    return _pcall(
        body, comm=comm, name="mla_qkv_bwd", grid=(MLA_H, s // ts),
        in_specs=[head(QW), head(QW), head(VDIM), tok(Q_RANK), tok(KV_RANK), wgt(Q_RANK, QW),
                  wgt(KV_RANK, NOPE + VDIM), tok(2 * ROPE)],
        out_specs=[whole(Q_RANK), whole(KV_RANK), whole(2 * ROPE), wgt(Q_RANK, QW), wgt(KV_RANK, NOPE + VDIM)],
        out_shape=[jax.ShapeDtypeStruct((s, Q_RANK), F32), jax.ShapeDtypeStruct((s, KV_RANK), F32),
                   jax.ShapeDtypeStruct((s, 2 * ROPE), F32), jax.ShapeDtypeStruct((MLA_H, Q_RANK, QW), F32),
                   jax.ShapeDtypeStruct((MLA_H, KV_RANK, NOPE + VDIM), F32)],
        compiler_params=_cparams(("arbitrary", "arbitrary")))(dq, dk, dv, cq, ckv, wq, wkv, table)


def _diag_mask(t):
    return lax.broadcasted_iota(jnp.int32, (t, t), 1) <= lax.broadcasted_iota(jnp.int32, (t, t), 0)


def _mla_fwd(q, k, v, comm=None):
    s = q.shape[1]
    t = _pick(s, ATT_T, 128)
    nt = s // t
    assert nt % 2 == 0
    hb = 2 * MLA_HB

    def fold(p, u):
        first = u <= p
        return jnp.where(first, p, nt - 1 - p), jnp.where(first, u, u - p - 1)

    to_log2 = MLA_SCALE * math.log2(math.e)

    def body(q_ref, k_ref, v_ref, o_ref, oh_ref, lse_ref, m_ref, acc_ref):
        i, j = fold(pl.program_id(1), pl.program_id(2))

        @pl.when(j == 0)
        def _():
            m_ref[...] = jnp.full_like(m_ref, -jnp.inf)
            acc_ref[...] = jnp.zeros_like(acc_ref)

        def step(diagonal):
            for h in range(hb):
                sc = lax.dot_general(q_ref[h], k_ref[h], _DIMS["nt"], preferred_element_type=F32)
                if diagonal:
                    sc = jnp.where(_diag_mask(t), sc, -jnp.inf)
                m_old = m_ref[h]
                m_new = jnp.maximum(m_old, jnp.max(sc, axis=-1, keepdims=True))
                alpha = jnp.exp2((m_old - m_new) * to_log2)
                p = jnp.exp2((sc - m_new) * to_log2)
                acc_ref[h] = alpha * acc_ref[h] + lax.dot_general(p.astype(BF16), v_ref[h], _DIMS["nn"],
                                                                  preferred_element_type=F32)
                m_ref[h] = m_new

        @pl.when(j < i)
        def _():
            step(False)

        @pl.when(j == i)
        def _():
            step(True)
            for h in range(hb):
                den = acc_ref[h, :, VDIM:VDIM + 1]
                o = acc_ref[h, :, 0:VDIM] / den
                o_ref[:, h * VDIM:(h + 1) * VDIM] = o
                oh_ref[:, h * VDIM:(h + 1) * VDIM] = o.astype(BF16)
                lse_ref[h] = m_ref[h] * MLA_SCALE + jnp.log(den)

    o_spec = pl.BlockSpec((t, hb * VDIM), lambda h, p, u: (fold(p, u)[0], h))
    return _pcall(
        body, comm=comm, name="mla_fwd", grid=(MLA_H // hb, nt // 2, nt + 1),
        in_specs=[pl.BlockSpec((hb, t, QW), lambda h, p, u: (h, fold(p, u)[0], 0)),
                  pl.BlockSpec((hb, t, QW), lambda h, p, u: (h, fold(p, u)[1], 0)),
                  pl.BlockSpec((hb, t, 2 * VDIM), lambda h, p, u: (h, fold(p, u)[1], 0))],
        out_specs=[o_spec, o_spec, pl.BlockSpec((hb, t, 1), lambda h, p, u: (h, fold(p, u)[0], 0))],
        out_shape=[jax.ShapeDtypeStruct((s, MLA_H * VDIM), F32), jax.ShapeDtypeStruct((s, MLA_H * VDIM), BF16),
                   jax.ShapeDtypeStruct((MLA_H, s, 1), F32)],
        scratch_shapes=[pltpu.VMEM((hb, t, 1), F32), pltpu.VMEM((hb, t, 2 * VDIM), F32)],
        compiler_params=_cparams(("parallel", "parallel", "arbitrary")))(q, k, v)


def _mla_delta(dmix, o):
    s = o.shape[0]
    ts = _pick(s, ROW_TILE, 16)
    w = MLA_H * VDIM

    def body(d_ref, o_ref, out_ref):
        prod = d_ref[...] * o_ref[...]
        for h in range(MLA_H):
            out_ref[h] = jnp.sum(prod[:, h * VDIM:(h + 1) * VDIM], axis=-1, keepdims=True)

    return _pcall(body, name="mla_delta", grid=(s // ts,),
                  in_specs=[pl.BlockSpec((ts, w), lambda i: (i, SWA_HEADS * SWA_DH // w)), pl.BlockSpec((ts, w), lambda i: (i, 0))],
                  out_specs=pl.BlockSpec((MLA_H, ts, 1), lambda i: (0, i, 0)),
                  out_shape=jax.ShapeDtypeStruct((MLA_H, s, 1), F32), compiler_params=_cparams(("parallel",)))(dmix, o)


def _mla_bwd(q, k, v, dmix, delta, lse, comm=None):
    s = q.shape[1]
    t = _pick(s, ATT_T, 128)
    nt = s // t
    assert nt % 2 == 0
    hb = MLA_HB
    o_blk0 = SWA_HEADS * SWA_DH // (hb * VDIM)

    def fold(p, u):
        first = u < nt - p
        return jnp.where(first, p, nt - 1 - p), jnp.where(first, p + u, u - 1)

    log2e = math.log2(math.e)

    def body(q_ref, k_ref, v_ref, do_ref, delta_ref, lse_ref, dq_ref, dk_ref, dv_ref, dk_acc, dv_acc):
        j, i = fold(pl.program_id(1), pl.program_id(2))
        rows = pl.ds(pl.multiple_of(i * t, t), t)

        @pl.when(i == j)
        def _():
            dk_acc[...] = jnp.zeros_like(dk_acc)
            dv_acc[...] = jnp.zeros_like(dv_acc)

        def step(diagonal):
            for h in range(hb):
                qv, kv_ = q_ref[h], k_ref[h]
                dob = do_ref[:, h * VDIM:(h + 1) * VDIM].astype(BF16)
                st = lax.dot_general(kv_, qv, _DIMS["nt"], preferred_element_type=F32)
                pt = jnp.exp2(st * (MLA_SCALE * log2e) - lse_ref[h] * log2e)
                if diagonal:
                    keep = lax.broadcasted_iota(jnp.int32, (t, t), 0) <= lax.broadcasted_iota(jnp.int32, (t, t), 1)
                    pt = jnp.where(keep, pt, 0.0)
                dpt = lax.dot_general(v_ref[h], dob, _DIMS["nt"], preferred_element_type=F32)
                dst = (pt * (dpt - delta_ref[h]) * MLA_SCALE).astype(BF16)
                dv_acc[h] += lax.dot_general(pt.astype(BF16), dob, _DIMS["nn"], preferred_element_type=F32)
                dk_acc[h] += lax.dot_general(dst, qv, _DIMS["nn"], preferred_element_type=F32)
                dqv = lax.dot_general(dst, kv_, _DIMS["tn"], preferred_element_type=F32)

                @pl.when(j == 0)
                def _():
                    dq_ref[h, rows, :] = dqv

                @pl.when(j > 0)
                def _():
                    dq_ref[h, rows, :] += dqv

        @pl.when(i > j)
        def _():
            step(False)

        @pl.when(i == j)
        def _():
            step(True)

        @pl.when(i == nt - 1)
        def _():
            dk_ref[...] = dk_acc[...]
            dv_ref[...] = dv_acc[...]

    qi = lambda h, p, u: (h, fold(p, u)[1], 0)
    kj = lambda h, p, u: (h, fold(p, u)[0], 0)
    row = pl.BlockSpec((hb, 1, t), lambda h, p, u: (h, 0, fold(p, u)[1]))
    return _pcall(
        body, comm=comm, name="mla_bwd", grid=(MLA_H // hb, nt // 2, nt + 1),
        in_specs=[pl.BlockSpec((hb, t, QW), qi), pl.BlockSpec((hb, t, QW), kj), pl.BlockSpec((hb, t, VDIM), kj),
                  pl.BlockSpec((t, hb * VDIM), lambda h, p, u: (fold(p, u)[1], o_blk0 + h)), row, row],
        out_specs=[pl.BlockSpec((hb, s, QW), lambda h, p, u: (h, 0, 0)), pl.BlockSpec((hb, t, QW), kj),
                   pl.BlockSpec((hb, t, VDIM), kj)],
        out_shape=[jax.ShapeDtypeStruct((MLA_H, s, QW), F32), jax.ShapeDtypeStruct((MLA_H, s, QW), F32),
                   jax.ShapeDtypeStruct((MLA_H, s, VDIM), F32)],
        scratch_shapes=[pltpu.VMEM((hb, t, QW), F32), pltpu.VMEM((hb, t, VDIM), F32)],
        compiler_params=_cparams(("arbitrary", "arbitrary", "arbitrary")))(q, k, v, dmix, delta, lse)


def _adamw(name, w, g, m, v, parts):
    r, c = w.shape
    n_parts = g.shape[0] if parts else 1
    tr = r if r * c <= ADAM_ELEMS else _pick(r, max(8, ADAM_ELEMS // c // 8 * 8), 8)
    c1 = 1.0 - ADAM_B1 ** ADAM_STEP
    c2 = 1.0 - ADAM_B2 ** ADAM_STEP

    def body(w_ref, g_ref, m_ref, v_ref, go_ref, d_ref, mo_ref, vo_ref):
        if parts:
            gv = g_ref[0].astype(F32)
            for j in range(1, n_parts):
                gv = gv + g_ref[j].astype(F32)
        else:
            gv = g_ref[...]
        mv = ADAM_B1 * m_ref[...] + (1.0 - ADAM_B1) * gv
        vv = ADAM_B2 * v_ref[...] + (1.0 - ADAM_B2) * (gv * gv)
        go_ref[...] = gv
        mo_ref[...] = mv
        vo_ref[...] = vv
        d_ref[...] = -ADAM_LR * ((mv / c1) / (jnp.sqrt(vv / c2) + ADAM_EPS) + ADAM_WD * w_ref[...])

    blk = pl.BlockSpec((tr, c), lambda i: (i, 0))
    g_spec = pl.BlockSpec((n_parts, tr, c), lambda i: (0, i, 0)) if parts else blk
    out = jax.ShapeDtypeStruct((r, c), F32)
    return _pcall(body, name=name, grid=(r // tr,), in_specs=[blk, g_spec, blk, blk], out_specs=[blk] * 4,
                  out_shape=[out] * 4, compiler_params=_cparams(("parallel",)))(w, g, m, v)


def _t5_bucket(dist):
    n = jnp.maximum(dist, 0)
    max_exact = REL_BUCKETS // 2
    nf = jnp.maximum(n, 1).astype(F32)
    large = max_exact + (jnp.log(nf / max_exact) / math.log(REL_MAX_DIST / max_exact)
                         * (REL_BUCKETS - max_exact)).astype(jnp.int32)
    return jnp.where(n < max_exact, n, jnp.minimum(large, REL_BUCKETS - 1))


def _swap_halves(w, r0):
    return jnp.concatenate([w[:, r0 + ROPE // 2:r0 + ROPE], w[:, r0:r0 + ROPE // 2]], axis=1)


def _fold_swapped(g, r0, width):
    sw = g[..., width:width + ROPE]
    half = ROPE // 2
    return jnp.concatenate([g[..., :r0], g[..., r0:r0 + half] + sw[..., half:], g[..., r0 + half:r0 + ROPE] + sw[..., :half],
                            g[..., r0 + ROPE:width]], axis=-1)


def _heads_major(a, heads):
    s = a.shape[0]
    return a.reshape(s, heads, SWA_DH).transpose(1, 0, 2)


def _tokens_major(a):
    h, s, d = a.shape
    return a.transpose(1, 0, 2).reshape(s, h * d)


def kernel(x, c, w_mod, b_mod, attn_norm_g, w_in, swa_sinks, rel_bias, mla_q_norm_g, w_uq, mla_kv_norm_g, w_ukv, w_out, mlp_norm_g, w_ff1, w_ff2, final_norm_g, loss_target, m_w_mod, m_b_mod, m_attn_norm_g, m_w_in, m_swa_sinks, m_rel_bias, m_mla_q_norm_g, m_w_uq, m_mla_kv_norm_g, m_w_ukv, m_w_out, m_mlp_norm_g, m_w_ff1, m_w_ff2, m_final_norm_g, v_w_mod, v_b_mod, v_attn_norm_g, v_w_in, v_swa_sinks, v_rel_bias, v_mla_q_norm_g, v_w_uq, v_mla_kv_norm_g, v_w_ukv, v_w_out, v_mlp_norm_g, v_w_ff1, v_w_ff2, v_final_norm_g):
    s, d = x.shape[1], x.shape[2]
    ffs = w_ff1.shape[2]
    ff = ffs * NDEV
    nmod = w_mod.shape[2]
    me = 4 * lax.axis_index("x") + 2 * lax.axis_index("y") + lax.axis_index("c")
    x2d, tgt = x[0], loss_target[0]
    final_g = final_norm_g.reshape(1, d)

    w_in_l = jnp.concatenate([w_in[0], _swap_halves(w_in[0], OFF_KR)], axis=1).astype(BF16)
    w_uq_l = jnp.concatenate([w_uq[0], _swap_halves(w_uq[0], NOPE)], axis=1).astype(BF16)
    core = jnp.full((1, 128), lax.axis_index("c"), F32)
    (c_all,) = _exchange("gather_c", _Gather([c]))

    b_cols = lax.dynamic_slice(b_mod, (0, me * nmod), (1, nmod))
    act_all, mod_cols, w_in_g, w_uq_g, w_ukv_g = _mod_fwd(
        c_all.reshape(NDEV, d), w_mod[0], b_cols, comm=[_Gather([w_in_l, w_uq_l, w_ukv[0].astype(BF16)])])
    w_in_e = w_in_g.reshape(d, IN_EXT)
    (mod_g,) = _exchange("gather_mod", _Gather([mod_cols]))
    mod = lax.dynamic_index_in_dim(mod_g, me, axis=1, keepdims=False).reshape(1, 6 * d)
    sh1, sc1, g1, sh2, sc2, g2 = [mod[:, i * d:(i + 1) * d] for i in range(6)]

    pos = jnp.arange(s, dtype=F32)
    inv_freq = ROPE_THETA ** (-jnp.arange(ROPE // 2, dtype=F32) / (ROPE // 2))
    ang = pos[:, None] * inv_freq[None, :]
    cos, sin = jnp.cos(ang), jnp.sin(ang)
    table = jnp.concatenate([cos, cos, -sin, sin], axis=1)
    q_loc = jnp.arange(BLOCK)[:, None]
    k_loc = jnp.arange(2 * BLOCK)[None, :]
    dist = q_loc + BLOCK - k_loc
    in_window = (dist >= 0) & (dist < BLOCK)
    onehot = (_t5_bucket(dist).reshape(-1, 1) == jnp.arange(REL_BUCKETS)[None, :]).astype(BF16)
    bias = _bias_expand(rel_bias.T, onehot.T).reshape(SWA_HEADS, BLOCK, 2 * BLOCK)
    bias = jnp.where(in_window[None], bias, -jnp.inf).reshape(SWA_KV, SWA_GROUP * BLOCK, 2 * BLOCK)
    sink_rows = jnp.broadcast_to(swa_sinks.reshape(SWA_HEADS, 1), (SWA_HEADS, BLOCK)).reshape(SWA_KV, SWA_GROUP * BLOCK, 1)

    h1 = _norm_mod("norm1", x2d, attn_norm_g, sc1, sh1)
    def both_dtypes(acc, ex, outs):
        outs[0][...] = acc
        outs[1][...] = acc.astype(BF16)

    tmp = _pick(s, MM_TM // 2, 128)
    proj_blk = pl.BlockSpec((tmp, IN_EXT), lambda i, j, q: (i, 0))
    proj, proj_h = _mm("proj", h1, w_in_e, "nn", (s // tmp, 1, 1), pl.BlockSpec((tmp, d), lambda i, j, q: (i, 0)),
                       pl.BlockSpec((d, IN_EXT), lambda i, j, q: (0, 0)),
                       [jax.ShapeDtypeStruct((s, IN_EXT), F32), jax.ShapeDtypeStruct((s, IN_EXT), BF16)],
                       [proj_blk, proj_blk], (tmp, IN_EXT), both_dtypes)
    q_a = _heads_major(proj_h[:, :OFF_K], SWA_HEADS).reshape(SWA_KV, SWA_GROUP, s, SWA_DH)
    k_a = _heads_major(proj_h[:, OFF_K:OFF_V], SWA_KV)
    v_a = _heads_major(proj_h[:, OFF_V:OFF_CQ], SWA_KV)
    o_a, lse_a, w_out_g = _swa_fwd(q_a, k_a, v_a, bias, sink_rows, comm=[_Gather([w_out[0].astype(BF16)])])
    w_out_f = w_out_g.reshape(MIX, d)

    cq, ckv, kr = _mla_prep(proj, mla_q_norm_g, mla_kv_norm_g, table)
    q_b, k_b, v_b = _mla_qkv(cq, ckv, kr, w_uq_g, w_ukv_g, table)
    o_b, o_bh, lse_b, w_ff1_g = _mla_fwd(q_b, k_b, v_b, comm=[_Gather([w_ff1[0].astype(BF16)])])
    mix = jnp.concatenate([_tokens_major(o_a.reshape(SWA_HEADS, s, SWA_DH)), o_bh], axis=1)

    tm, tn, tk = _pick(s, MM_TM, 128), _pick(d, MM_TN, 128), _pick(MIX, MM_TK, 128)
    row_blk = pl.BlockSpec((tm, tn), lambda i, j, q: (i, j))
    gate_blk = pl.BlockSpec((1, tn), lambda i, j, q: (0, j))

    def gated_residual(acc, ex, outs):
        outs[0][...] = acc
        outs[1][...] = ex[0][...] + ex[1][...] * acc

    y1, x2 = _mm("out_proj", mix, w_out_f, "nn", (s // tm, d // tn, MIX // tk),
                 pl.BlockSpec((tm, tk), lambda i, j, q: (i, q)), pl.BlockSpec((tk, tn), lambda i, j, q: (q, j)),
                 [jax.ShapeDtypeStruct((s, d), F32)] * 2, [row_blk, row_blk], (tm, tn), gated_residual,
                 extras=(x2d, g1), extra_specs=(row_blk, gate_blk))

    h2 = _norm_mod("norm2", x2, mlp_norm_g, sc2, sh2)
    tnf, tkd = _pick(ffs, MM_TN, 128), _pick(d, MM_TK, 128)
    rf = ffs // tnf
    ff_blk = pl.BlockSpec((tm, tnf), lambda i, j, q: (i, j))

    def relu_sq(acc, ex, outs):
        u = jnp.maximum(acc, 0.0)
        outs[0][...] = u
        outs[1][...] = (u * u).astype(BF16)

    u, uu, w_ff2_g = _mm("ff1", h2, w_ff1_g, "nn", (s // tm, ff // tnf, d // tkd),
                         pl.BlockSpec((tm, tkd), lambda i, j, q: (i, q)),
                         pl.BlockSpec((None, tkd, tnf), lambda i, j, q: (j // rf, q, j % rf)),
                         [jax.ShapeDtypeStruct((s, ff), F32), jax.ShapeDtypeStruct((s, ff), BF16)], [ff_blk, ff_blk],
                         (tm, tnf), relu_sq, comm=[_Gather([w_ff2[0].astype(BF16)])])
    w_ff2_f = w_ff2_g.reshape(ff, d)
    tkf = _pick(ff, MM_TK, 128)
    y2, x3 = _mm("ff2", uu, w_ff2_f, "nn", (s // tm, d // tn, ff // tkf),
                 pl.BlockSpec((tm, tkf), lambda i, j, q: (i, q)), pl.BlockSpec((tkf, tn), lambda i, j, q: (q, j)),
                 [jax.ShapeDtypeStruct((s, d), F32)] * 2, [row_blk, row_blk], (tm, tn), gated_residual,
                 extras=(x2, g2), extra_specs=(row_blk, gate_blk))

    dx3, dy2, loss_p, dgf, dg2 = _loss_head(x3, tgt, y2, final_g, g2)
    loss = lax.psum(loss_p[0, 0], ("x", "y", "c"))

    def relu_sq_bwd(acc, ex, outs):
        outs[0][...] = (acc * (2.0 * ex[0][...])).astype(BF16)

    tnf2 = _pick(ff, MM_TN, 128)
    du = _mm("ff2_dx", dy2, w_ff2_f, "nt", (s // tm, ff // tnf2, d // tkd),
             pl.BlockSpec((tm, tkd), lambda i, j, q: (i, q)), pl.BlockSpec((tnf2, tkd), lambda i, j, q: (j, q)),
             [jax.ShapeDtypeStruct((s, ff), BF16)], [pl.BlockSpec((tm, tnf2), lambda i, j, q: (i, j))],
             (tm, tnf2), relu_sq_bwd, extras=(u,), extra_specs=(pl.BlockSpec((tm, tnf2), lambda i, j, q: (i, j)),))[0]
    gw_ff2 = _mm_plain("ff2_dw", uu, dy2, "tn", ff, d, s, BF16)
    tmd, tks = _pick(d, MM_TM, 128), _pick(s, MM_TK, 128)
    gw_ff2 = gw_ff2.reshape(NDEV, ffs, d)
    tkf1 = _pick(ffs, MM_TK, 128)
    rk = ffs // tkf1
    dh2, s_ff2 = _mm("ff1_dx", du, w_ff1_g, "nt", (s // tm, d // tn, ff // tkf1),
                     pl.BlockSpec((tm, tkf1), lambda i, j, q: (i, q)),
                     pl.BlockSpec((None, tn, tkf1), lambda i, j, q: (q // rk, j, q % rk)),
                     [jax.ShapeDtypeStruct((s, d), F32)], [row_blk], (tm, tn), _store(F32),
                     comm=[_PairSwap([gw_ff2])])
    c_ff2 = _pair_sum("pair_ff2", gw_ff2, s_ff2, core)
    gw_ff1, p_ff2 = _mm("ff1_dw", h2, du, "tn", (d // tmd, ff // tnf, s // tks),
                        pl.BlockSpec((tks, tmd), lambda i, j, q: (q, i)), pl.BlockSpec((tks, tnf), lambda i, j, q: (q, j)),
                        [jax.ShapeDtypeStruct((NDEV, d, ffs), BF16)],
                        [pl.BlockSpec((None, tmd, tnf), lambda i, j, q: (j // rf, i, j % rf))], (tmd, tnf), _store(BF16),
                        comm=[_ChipScatter([c_ff2])])
    dx2, dy1, dsc2, dsh2, dgm, dg1, s_ff1 = _norm_mod_bwd("norm2_bwd", x2, dh2, dx3, mlp_norm_g, sc2, y1, g1,
                                                          comm=[_PairSwap([gw_ff1])])
    c_ff1 = _pair_sum("pair_ff1", gw_ff1, s_ff1, core)

    dmix = _mm_plain("out_proj_dx", dy1, w_out_f, "nt", s, MIX, d, F32)
    gw_out = _mm_plain("out_proj_dw", mix, dy1, "tn", MIX, d, s, BF16).reshape(NDEV, MIX // NDEV, d)

    delta_b = _mla_delta(dmix, o_b).reshape(MLA_H, 1, s)
    dq_b, dk_b, dv_b, p_ff1, s_out = _mla_bwd(q_b, k_b, v_b, dmix, delta_b, lse_b.reshape(MLA_H, 1, s),
                                              comm=[_ChipScatter([c_ff1]), _PairSwap([gw_out])])
    c_out = _pair_sum("pair_out", gw_out, s_out, core)
    dcq, dckv, dkr, gw_uq_e, gw_ukv, p_out = _mla_qkv_bwd(dq_b, dk_b, dv_b, cq, ckv, w_uq_g, w_ukv_g, table,
                                                          comm=[_ChipScatter([c_out])])
    dtail, dgq, dgkv = _mla_prep_bwd(proj, dcq, dckv, dkr, mla_q_norm_g, mla_kv_norm_g, table)
    gw_uq = _fold_swapped(gw_uq_e, NOPE, NOPE + ROPE).astype(BF16)
    gw_ukv = gw_ukv.astype(BF16)

    do_a = _heads_major(dmix[:, :OFF_K].astype(BF16), SWA_HEADS).reshape(SWA_KV, SWA_GROUP, s, SWA_DH)
    dq_a, dkp, dkc, dvp, dvc, dbias, dsink, s_uq, s_ukv = _swa_bwd(
        q_a, k_a, v_a, do_a, lse_a, bias, sink_rows, comm=[_PairSwap([gw_uq, gw_ukv])])
    c_uq = _pair_sum("pair_uq", gw_uq, s_uq, core)
    c_ukv = _pair_sum("pair_ukv", gw_ukv, s_ukv, core)
    shift = lambda p: jnp.concatenate([p[:, BLOCK:], jnp.zeros_like(p[:, :BLOCK])], axis=1)
    dk_a, dv_a = dkc + shift(dkp), dvc + shift(dvp)
    drel_t, dsinks = _bias_reduce(dbias.reshape(SWA_HEADS, BLOCK * 2 * BLOCK), onehot, dsink.reshape(SWA_HEADS, BLOCK))
    dproj = jnp.concatenate([_tokens_major(dq_a.reshape(SWA_HEADS, s, SWA_DH)),
                             _tokens_major(dk_a).astype(BF16), _tokens_major(dv_a).astype(BF16), dtail], axis=1)
    gw_in_e = _mm_plain("proj_dw", h1, dproj, "tn", d, IN_EXT, s, F32, tn=TAIL)
    gw_in = _fold_swapped(gw_in_e, OFF_KR, IN_COLS).reshape(NDEV, d // NDEV, IN_COLS).astype(BF16)
    tkt = IN_EXT
    dh1, s_in, p_uq, p_ukv = _mm(
        "proj_dx", dproj, w_in_e, "nt", (s // tm, d // tn, IN_EXT // tkt),
        pl.BlockSpec((tm, tkt), lambda i, j, q: (i, q)), pl.BlockSpec((tn, tkt), lambda i, j, q: (j, q)),
        [jax.ShapeDtypeStruct((s, d), F32)], [row_blk], (tm, tn), _store(F32),
        comm=[_PairSwap([gw_in]), _ChipScatter([c_uq, c_ukv])])
    c_in = _pair_sum("pair_in", gw_in, s_in, core)
    gx, dsc1, dsh1, dga, p_in = _norm_mod_bwd("norm1_bwd", x2d, dh1, dx2, attn_norm_g, sc1,
                                              comm=[_ChipScatter([c_in])])

    small = [jnp.concatenate([dsh1, dsc1, dg1, dsh2, dsc2, dg2], axis=1), dga, dgm, dgf, dgq, dgkv,
             dsinks.reshape(1, SWA_HEADS), drel_t.T.reshape(1, REL_BUCKETS * SWA_HEADS)]
    n_small = sum(a.shape[1] for a in small)
    n_pad = -n_small % 1024
    rows_small = (n_small + n_pad) // 128
    pad = jnp.zeros((1, n_pad), F32)
    pack = lambda parts: jnp.concatenate([p.reshape(1, -1) for p in parts] + [pad], axis=1).reshape(rows_small, 128)
    (small_g,) = _exchange("gather_small", _Gather([pack(small)]))
    small_names = (b_mod, attn_norm_g, mlp_norm_g, final_norm_g, mla_q_norm_g, mla_kv_norm_g, swa_sinks, rel_bias)
    small_m = (m_b_mod, m_attn_norm_g, m_mlp_norm_g, m_final_norm_g, m_mla_q_norm_g, m_mla_kv_norm_g, m_swa_sinks, m_rel_bias)
    small_v = (v_b_mod, v_attn_norm_g, v_mlp_norm_g, v_final_norm_g, v_mla_q_norm_g, v_mla_kv_norm_g, v_swa_sinks, v_rel_bias)
    small_out = _adamw("adamw_small", pack(small_names), small_g, pack(small_m), pack(small_v), parts=True)

    def unpack(flat):
        flat = flat.reshape(1, -1)
        out, off = [], 0
        for a in small_names:
            out.append(flat[:, off:off + a.size].reshape(a.shape))
            off += a.size
        return out

    sg, sd, sm, sv = [unpack(o) for o in small_out]

    dmod_cols = lax.dynamic_slice(small_g.reshape(NDEV, -1), (0, me * nmod), (NDEV, nmod))
    gw_mod = _mod_wgrad(act_all, dmod_cols)
    big = {"w_mod": _adamw("adamw_w_mod", w_mod[0], gw_mod, m_w_mod[0], v_w_mod[0], parts=False)}

    for name, w, p, m, v in (("w_in", w_in, p_in, m_w_in, v_w_in), ("w_uq", w_uq, p_uq, m_w_uq, v_w_uq),
                             ("w_ukv", w_ukv, p_ukv, m_w_ukv, v_w_ukv), ("w_out", w_out, p_out, m_w_out, v_w_out),
                             ("w_ff1", w_ff1, p_ff1, m_w_ff1, v_w_ff1), ("w_ff2", w_ff2, p_ff2, m_w_ff2, v_w_ff2)):
        big[name] = _adamw("adamw_" + name, w[0], p, m[0], v[0], parts=True)

    order = ("w_mod", "b_mod", "attn_norm_g", "w_in", "swa_sinks", "rel_bias", "mla_q_norm_g", "w_uq", "mla_kv_norm_g",
             "w_ukv", "w_out", "mlp_norm_g", "w_ff1", "w_ff2", "final_norm_g")
    small_idx = {"b_mod": 0, "attn_norm_g": 1, "mlp_norm_g": 2, "final_norm_g": 3, "mla_q_norm_g": 4,
                 "mla_kv_norm_g": 5, "swa_sinks": 6, "rel_bias": 7}
    outs = []
    for kind, small_list in enumerate((sg, sd, sm, sv)):
        for name in order:
            outs.append(small_list[small_idx[name]] if name in small_idx else big[name][kind][None])
    return (loss, gx[None], *outs)
```

```python
import functools
import math

import jax
import jax.numpy as jnp
from jax import lax
from jax.experimental import pallas as pl
from jax.experimental.pallas import tpu as pltpu

F32 = jnp.float32
BF16 = jnp.bfloat16

NDEV = 8
EPS = 1e-6
BLOCK = 128
SWA_HEADS, SWA_KV, SWA_DH, SWA_GROUP = 16, 2, 64, 8
REL_BUCKETS, REL_MAX_DIST = 32, 128
MLA_H, Q_RANK, KV_RANK, NOPE, ROPE, VDIM = 8, 384, 128, 128, 64, 128
ROPE_THETA = 10000.0
OFF_K, OFF_V, OFF_CQ, OFF_CKV, OFF_KR, IN_COLS = 1024, 1152, 1280, 1664, 1792, 1856
IN_EXT = IN_COLS + ROPE
TAIL0, TAIL = OFF_CQ, IN_EXT - OFF_CQ
QW = NOPE + 2 * ROPE
MIX = SWA_HEADS * SWA_DH + MLA_H * VDIM
MLA_SCALE = (NOPE + ROPE) ** -0.5
SWA_SCALE = SWA_DH ** -0.5

ADAM_LR, ADAM_B1, ADAM_B2, ADAM_EPS, ADAM_WD, ADAM_STEP = 0.001, 0.9, 0.999, 1e-08, 0.01, 10

VMEM_LIMIT = 52 * 1024 * 1024
ROW_TILE = 256
MM_TM, MM_TN, MM_TK = 1024, 1024, 2048
ATT_T = 512
MLA_HB = 2
ADAM_ELEMS = 128 * 1024


MESH_ID = pl.DeviceIdType.MESH


def _place():
    x, y, c = lax.axis_index("x"), lax.axis_index("y"), lax.axis_index("c")
    return x, y, c, 2 * x + y


def _chip(x, y, k):
    return (1 - x if k & 2 else x, 1 - y if k & 1 else y)


def _dma_sems(*counts):
    return [pltpu.SemaphoreType.DMA((n,)) for n in counts]


class _Gather:
    def __init__(self, arrays):
        self.arrays = list(arrays)
        n = len(self.arrays)
        self.out_shape = [jax.ShapeDtypeStruct((NDEV,) + a.shape, a.dtype) for a in self.arrays]
        self.sems = _dma_sems(7 * n, 7 * n, n)

    def _copy(self, sems, a, k, src, dst, to):
        return pltpu.make_async_remote_copy(src_ref=src, dst_ref=dst, send_sem=sems[0].at[7 * a + k],
                                            recv_sem=sems[1].at[7 * a + k], device_id=to, device_id_type=MESH_ID)

    def start(self, ins, outs, sems):
        x, y, c, q = _place()
        me = 2 * q + c
        for a in range(len(ins)):
            pltpu.make_async_copy(ins[a], outs[a].at[me], sems[2].at[a]).start()
            self._copy(sems, a, 0, ins[a], outs[a].at[me], (x, y, 1 - c)).start()
            for k in (1, 2, 3):
                self._copy(sems, a, k, ins[a], outs[a].at[me], (*_chip(x, y, k), c)).start()

    def finish(self, ins, outs, sems):
        x, y, c, q = _place()
        me, sib = 2 * q + c, (x, y, 1 - c)
        n = len(ins)
        for k in (1, 2, 3):
            for a in range(n):
                blk = outs[a].at[2 * (q ^ k) + c]
                self._copy(sems, a, k, ins[a], blk, (*_chip(x, y, k), c)).wait_recv()
                self._copy(sems, a, 3 + k, blk, blk, sib).start()
        for a in range(n):
            self._copy(sems, a, 0, ins[a], outs[a].at[2 * q + 1 - c], sib).wait_recv()
            for k in (1, 2, 3):
                blk = outs[a].at[2 * (q ^ k) + 1 - c]
                self._copy(sems, a, 3 + k, blk, blk, sib).wait_recv()
        for a in range(n):
            for k in range(7):
                self._copy(sems, a, k, ins[a], outs[a].at[me], sib).wait_send()
            pltpu.make_async_copy(ins[a], outs[a].at[me], sems[2].at[a]).wait()


class _PairSwap:
    def __init__(self, arrays):
        self.arrays = list(arrays)
        n = len(self.arrays)
        self.out_shape = [jax.ShapeDtypeStruct((NDEV // 2,) + a.shape[1:], a.dtype) for a in self.arrays]
        self.sems = _dma_sems(4 * n, 4 * n)

    def _copy(self, sems, a, p, src, dst, to):
        return pltpu.make_async_remote_copy(src_ref=src, dst_ref=dst, send_sem=sems[0].at[4 * a + p],
                                            recv_sem=sems[1].at[4 * a + p], device_id=to, device_id_type=MESH_ID)

    def start(self, ins, outs, sems):
        x, y, c, _ = _place()
        for a in range(len(ins)):
            for p in range(4):
                self._copy(sems, a, p, ins[a].at[2 * p + 1 - c], outs[a].at[p], (x, y, 1 - c)).start()

    def finish(self, ins, outs, sems):
        x, y, c, _ = _place()
        for a in range(len(ins)):
            for p in range(4):
                cp = self._copy(sems, a, p, ins[a].at[2 * p + 1 - c], outs[a].at[p], (x, y, 1 - c))
                cp.wait_recv()
                cp.wait_send()


class _ChipScatter:
    def __init__(self, arrays):
        self.arrays = list(arrays)
        n = len(self.arrays)
        self.out_shape = [jax.ShapeDtypeStruct(a.shape, a.dtype) for a in self.arrays]
        self.sems = _dma_sems(3 * n, 3 * n, n)

    def _copy(self, sems, a, k, src, dst, to):
        return pltpu.make_async_remote_copy(src_ref=src, dst_ref=dst, send_sem=sems[0].at[3 * a + k - 1],
                                            recv_sem=sems[1].at[3 * a + k - 1], device_id=to, device_id_type=MESH_ID)

    def start(self, ins, outs, sems):
        x, y, c, q = _place()
        for a in range(len(ins)):
            pltpu.make_async_copy(ins[a].at[q], outs[a].at[q], sems[2].at[a]).start()
            for k in (1, 2, 3):
                self._copy(sems, a, k, ins[a].at[q ^ k], outs[a].at[q], (*_chip(x, y, k), c)).start()

    def finish(self, ins, outs, sems):
        x, y, c, q = _place()
        for a in range(len(ins)):
            for k in (1, 2, 3):
                cp = self._copy(sems, a, k, ins[a].at[q ^ k], outs[a].at[q ^ k], (*_chip(x, y, k), c))
                cp.wait_recv()
                cp.wait_send()
            pltpu.make_async_copy(ins[a].at[q], outs[a].at[q], sems[2].at[a]).wait()


def _call(body, **kw):
    return pl.pallas_call(body, **kw)


def _pcall(body, comm=None, **kw):
    if not comm:
        return _call(body, **kw)
    grid = kw["grid"]
    in_specs, out_specs, out_shape = list(kw["in_specs"]), list(kw["out_specs"]), list(kw["out_shape"])
    scratch = list(kw.get("scratch_shapes", ()))
    n_in, n_out, n_scr = len(in_specs), len(out_shape), len(scratch)
    n_cin = [len(j.arrays) for j in comm]
    n_sem = [len(j.sems) for j in comm]
    n = sum(n_cin)
    hbm = pl.BlockSpec(memory_space=pltpu.HBM)

    def carried(*refs):
        ins, cins = refs[:n_in], refs[n_in:n_in + n]
        outs, couts = refs[n_in + n:n_in + n + n_out], refs[n_in + n + n_out:n_in + 2 * n + n_out]
        scr, sems = refs[n_in + 2 * n + n_out:n_in + 2 * n + n_out + n_scr], refs[n_in + 2 * n + n_out + n_scr:]
        ids = [pl.program_id(ax) for ax in range(len(grid))]
        first = functools.reduce(jnp.logical_and, [i == 0 for i in ids])
        last = functools.reduce(jnp.logical_and, [i == g - 1 for i, g in zip(ids, grid)])

        def each(method):
            ai = si = 0
            for job, na, ns in zip(comm, n_cin, n_sem):
                getattr(job, method)(cins[ai:ai + na], couts[ai:ai + na], sems[si:si + ns])
                ai, si = ai + na, si + ns

        @pl.when(first)
        def _():
            each("start")

        body(*ins, *outs, *scr)

        @pl.when(last)
        def _():
            each("finish")

    kw.update(in_specs=in_specs + [hbm] * n, out_specs=out_specs + [hbm] * n,
              out_shape=out_shape + [o for j in comm for o in j.out_shape],
              scratch_shapes=scratch + [sm for j in comm for sm in j.sems],
              compiler_params=_cparams(("arbitrary",) * len(grid)))
    call = _call(carried, **kw)
    return lambda *args: call(*args, *[a for j in comm for a in j.arrays])


def _cparams(sem):
    return pltpu.CompilerParams(dimension_semantics=sem, vmem_limit_bytes=VMEM_LIMIT)


def _pick(n, pref, align):
    if n <= pref:
        return n
    t = (pref // align) * align
    while t >= align:
        if n % t == 0:
            return t
        t -= align
    return n


def _split3(x):
    a = x.astype(BF16)
    r = x - a.astype(F32)
    b = r.astype(BF16)
    c = (r - b.astype(F32)).astype(BF16)
    return a, b, c


def _exchange(name, job):
    n = len(job.arrays)

    def body(*refs):
        ins, outs, sems = refs[:n], refs[n:2 * n], refs[2 * n:]
        job.start(ins, outs, sems)
        job.finish(ins, outs, sems)

    hbm = pl.BlockSpec(memory_space=pltpu.HBM)
    return _call(body, name=name, out_shape=job.out_shape, in_specs=[hbm] * n, out_specs=[hbm] * n,
                 scratch_shapes=job.sems)(*job.arrays)


def _pair_sum(name, g, r, core):
    _, rr, cc = g.shape
    tr = rr if rr * cc <= 4 * ADAM_ELEMS else _pick(rr, max(16, 4 * ADAM_ELEMS // cc // 16 * 16), 16)

    def body(g_ref, r_ref, c_ref, o_ref):
        north = c_ref[:, 0:1] > 0.5
        mine = jnp.where(north, g_ref[1].astype(F32), g_ref[0].astype(F32))
        o_ref[...] = (mine + r_ref[...].astype(F32)).astype(o_ref.dtype)

    return _pcall(
        body, name=name, grid=(NDEV // 2, rr // tr),
        in_specs=[pl.BlockSpec((None, 2, tr, cc), lambda p, i: (p, 0, i, 0)),
                  pl.BlockSpec((None, tr, cc), lambda p, i: (p, i, 0)), pl.BlockSpec((1, 128), lambda p, i: (0, 0))],
        out_specs=pl.BlockSpec((None, tr, cc), lambda p, i: (p, i, 0)),
        out_shape=jax.ShapeDtypeStruct((NDEV // 2, rr, cc), g.dtype),
        compiler_params=_cparams(("parallel", "parallel")))(g.reshape(NDEV // 2, 2, rr, cc), r, core)


_DIMS = {"nn": (((1,), (0,)), ((), ())), "nt": (((1,), (1,)), ((), ())), "tn": (((0,), (0,)), ((), ()))}


def _mm(name, a, b, kind, grid, a_spec, b_spec, out_shape, out_specs, acc_shape, epilogue,
        extras=(), extra_specs=(), comm=None, b_parts=1):
    nk, ne, no = grid[2], len(extras), len(out_shape)

    def body(*refs):
        a_ref, b_ref = refs[0], refs[1]
        ex, outs = refs[2:2 + ne], refs[2 + ne:2 + ne + no]
        if b_parts == 1:
            part = lax.dot_general(a_ref[...].astype(BF16), b_ref[...].astype(BF16), _DIMS[kind],
                                   preferred_element_type=F32)
        else:
            kp = a_ref.shape[1] // b_parts
            part = sum(lax.dot_general(a_ref[:, p * kp:(p + 1) * kp].astype(BF16), b_ref[p].astype(BF16), _DIMS[kind],
                                       preferred_element_type=F32) for p in range(b_parts))
        if nk == 1:
            epilogue(part, ex, outs)
            return
        acc = refs[-1]
        k = pl.program_id(2)

        @pl.when(k == 0)
        def _():
            acc[...] = part

        @pl.when(jnp.logical_and(k > 0, k < nk - 1))
        def _():
            acc[...] += part

        @pl.when(k == nk - 1)
        def _():
            epilogue(acc[...] + part, ex, outs)

    return _pcall(
        body, comm=comm, name=name, grid=grid, in_specs=[a_spec, b_spec, *extra_specs], out_specs=out_specs,
        out_shape=out_shape, scratch_shapes=[pltpu.VMEM(acc_shape, F32)] if nk > 1 else [],
        compiler_params=_cparams(("parallel", "parallel", "arbitrary")),
    )(a, b, *extras)


def _store(dtype):
    def epi(acc, ex, outs):
        outs[0][...] = acc.astype(dtype)
    return epi


def _mm_plain(name, a, b, kind, m, n, k, out_dtype, tm=None, tn=None, tk=None):
    tm = _pick(m, tm or MM_TM, 128)
    tn = _pick(n, tn or MM_TN, 128)
    tk = _pick(k, tk or MM_TK, 128)
    a_spec = pl.BlockSpec((tk, tm), lambda i, j, q: (q, i)) if kind == "tn" else pl.BlockSpec((tm, tk), lambda i, j, q: (i, q))
    b_spec = pl.BlockSpec((tn, tk), lambda i, j, q: (j, q)) if kind == "nt" else pl.BlockSpec((tk, tn), lambda i, j, q: (q, j))
    return _mm(name, a, b, kind, (m // tm, n // tn, k // tk), a_spec, b_spec,
               [jax.ShapeDtypeStruct((m, n), out_dtype)], [pl.BlockSpec((tm, tn), lambda i, j, q: (i, j))],
               (tm, tn), _store(out_dtype))[0]


def _row(ts, d):
    return pl.BlockSpec((ts, d), lambda i: (i, 0))


def _vec(d):
    return pl.BlockSpec((1, d), lambda i: (0, 0))


def _norm_mod(name, x, gain, sc, sh):
    s, d = x.shape
    ts = _pick(s, ROW_TILE, 16)

    def body(x_ref, g_ref, sc_ref, sh_ref, h_ref):
        xv = x_ref[...]
        r = lax.rsqrt(jnp.mean(xv * xv, axis=-1, keepdims=True) + EPS)
        h_ref[...] = ((xv * r) * g_ref[...] * (1.0 + sc_ref[...]) + sh_ref[...]).astype(BF16)

    return _pcall(body, name=name, grid=(s // ts,), in_specs=[_row(ts, d), _vec(d), _vec(d), _vec(d)],
                  out_specs=_row(ts, d), out_shape=jax.ShapeDtypeStruct((s, d), BF16),
                  compiler_params=_cparams(("parallel",)))(x, gain, sc, sh)


def _loss_head(x3, tgt, y2, gf, g2):
    s, d = x3.shape
    ts = _pick(s, ROW_TILE, 16)

    def body(x_ref, t_ref, y_ref, gf_ref, g2_ref, dx_ref, dy_ref, loss_ref, dgf_ref, dg2_ref):
        @pl.when(pl.program_id(0) == 0)
        def _():
            loss_ref[...] = jnp.zeros_like(loss_ref)
            dgf_ref[...] = jnp.zeros_like(dgf_ref)
            dg2_ref[...] = jnp.zeros_like(dg2_ref)

        xv = x_ref[...]
        r = lax.rsqrt(jnp.mean(xv * xv, axis=-1, keepdims=True) + EPS)
        xn = xv * r
        err = xn * gf_ref[...] - t_ref[...]
        loss_ref[...] += 0.5 * jnp.sum(jnp.mean(err * err, axis=-1, keepdims=True), axis=0, keepdims=True)
        dout = err * (1.0 / d)
        dgf_ref[...] += jnp.sum(dout * xn, axis=0, keepdims=True)
        dxn = dout * gf_ref[...]
        dx = r * (dxn - xn * jnp.mean(dxn * xn, axis=-1, keepdims=True))
        dx_ref[...] = dx
        dy_ref[...] = (dx * g2_ref[...]).astype(BF16)
        dg2_ref[...] += jnp.sum(dx * y_ref[...], axis=0, keepdims=True)

    one = pl.BlockSpec((1, 1), lambda i: (0, 0))
    return _pcall(
        body, name="loss_head", grid=(s // ts,),
        in_specs=[_row(ts, d), _row(ts, d), _row(ts, d), _vec(d), _vec(d)],
        out_specs=[_row(ts, d), _row(ts, d), one, _vec(d), _vec(d)],
        out_shape=[jax.ShapeDtypeStruct((s, d), F32), jax.ShapeDtypeStruct((s, d), BF16),
                   jax.ShapeDtypeStruct((1, 1), F32), jax.ShapeDtypeStruct((1, d), F32),
                   jax.ShapeDtypeStruct((1, d), F32)],
        compiler_params=_cparams(("arbitrary",)))(x3, tgt, y2, gf, g2)


def _norm_mod_bwd(name, x, dh, dres, gain, sc, y_prev=None, gate=None, comm=None):
    s, d = x.shape
    ts = _pick(s, ROW_TILE, 16)
    gated = y_prev is not None

    def body(*refs):
        if gated:
            x_ref, dh_ref, dr_ref, g_ref, sc_ref, y_ref, gt_ref, dx_ref, dy_ref, dsc_ref, dsh_ref, dg_ref, dgt_ref = refs
        else:
            x_ref, dh_ref, dr_ref, g_ref, sc_ref, dx_ref, dsc_ref, dsh_ref, dg_ref = refs

        @pl.when(pl.program_id(0) == 0)
        def _():
            dsc_ref[...] = jnp.zeros_like(dsc_ref)
            dsh_ref[...] = jnp.zeros_like(dsh_ref)
            dg_ref[...] = jnp.zeros_like(dg_ref)
            if gated:
                dgt_ref[...] = jnp.zeros_like(dgt_ref)

        xv, dhv = x_ref[...], dh_ref[...]
        r = lax.rsqrt(jnp.mean(xv * xv, axis=-1, keepdims=True) + EPS)
        xn = xv * r
        dsc_ref[...] += jnp.sum(dhv * (xn * g_ref[...]), axis=0, keepdims=True)
        dsh_ref[...] += jnp.sum(dhv, axis=0, keepdims=True)
        da = dhv * (1.0 + sc_ref[...])
        dg_ref[...] += jnp.sum(da * xn, axis=0, keepdims=True)
        dxn = da * g_ref[...]
        dx = dr_ref[...] + r * (dxn - xn * jnp.mean(dxn * xn, axis=-1, keepdims=True))
        dx_ref[...] = dx
        if gated:
            dy_ref[...] = (dx * gt_ref[...]).astype(BF16)
            dgt_ref[...] += jnp.sum(dx * y_ref[...], axis=0, keepdims=True)

    ins = [x, dh, dres, gain, sc] + ([y_prev, gate] if gated else [])
    in_specs = [_row(ts, d)] * 3 + [_vec(d)] * 2 + ([_row(ts, d), _vec(d)] if gated else [])
    vec_out = jax.ShapeDtypeStruct((1, d), F32)
    out_shape = [jax.ShapeDtypeStruct((s, d), F32)] + ([jax.ShapeDtypeStruct((s, d), BF16)] if gated else [])
    out_shape += [vec_out] * (4 if gated else 3)
    out_specs = [_row(ts, d)] * (2 if gated else 1) + [_vec(d)] * (4 if gated else 3)
    return _pcall(body, comm=comm, name=name, grid=(s // ts,), in_specs=in_specs, out_specs=out_specs,
                  out_shape=out_shape, compiler_params=_cparams(("arbitrary",)))(*ins)


def _dot3(a, b, dims):
    a1, a2, _ = _split3(a)
    b1, b2, _ = _split3(b)
    dot = functools.partial(lax.dot_general, dimension_numbers=dims, preferred_element_type=F32)
    return dot(a1, b1) + (dot(a1, b2) + dot(a2, b1))


def _mod_fwd(c_all, w, b_cols, comm=None):
    nb, d = c_all.shape
    n = w.shape[1]
    tk = _pick(d, 512, 128)
    nk = d // tk

    def body(c_ref, w_ref, b_ref, act_ref, out_ref):
        k = pl.program_id(0)
        cv = c_ref[...]
        act = cv * (1.0 / (1.0 + jnp.exp(-cv)))
        act_ref[...] = act

        @pl.when(k == 0)
        def _():
            out_ref[...] = jnp.broadcast_to(b_ref[...], out_ref.shape)

        out_ref[...] += _dot3(act, w_ref[...], _DIMS["nn"])

    return _pcall(
        body, comm=comm, name="mod_fwd", grid=(nk,),
        in_specs=[pl.BlockSpec((nb, tk), lambda k: (0, k)), pl.BlockSpec((tk, n), lambda k: (k, 0)),
                  pl.BlockSpec((1, n), lambda k: (0, 0))],
        out_specs=[pl.BlockSpec((nb, tk), lambda k: (0, k)), pl.BlockSpec((nb, n), lambda k: (0, 0))],
        out_shape=[jax.ShapeDtypeStruct((nb, d), F32), jax.ShapeDtypeStruct((nb, n), F32)],
        compiler_params=_cparams(("arbitrary",)))(c_all, w, b_cols)


def _mod_wgrad(act_all, dmod_cols):
    nb, d = act_all.shape
    n = dmod_cols.shape[1]
    tm = _pick(d, 512, 128)

    def body(a_ref, d_ref, o_ref):
        o_ref[...] = _dot3(a_ref[...], d_ref[...], _DIMS["tn"])

    return _pcall(
        body, name="mod_wgrad", grid=(d // tm,),
        in_specs=[pl.BlockSpec((nb, tm), lambda i: (0, i)), pl.BlockSpec((nb, n), lambda i: (0, 0))],
        out_specs=pl.BlockSpec((tm, n), lambda i: (i, 0)), out_shape=jax.ShapeDtypeStruct((d, n), F32),
        compiler_params=_cparams(("parallel",)))(act_all, dmod_cols)


def _bias_expand(rel_t, onehot_t):
    h, _ = rel_t.shape
    n = onehot_t.shape[1]

    def body(r_ref, o_ref, out_ref):
        a, b, c = _split3(r_ref[...])
        dot = functools.partial(lax.dot_general, dimension_numbers=_DIMS["nn"], preferred_element_type=F32)
        oh = o_ref[...]
        out_ref[...] = dot(a, oh) + (dot(b, oh) + dot(c, oh))

    full = lambda shp: pl.BlockSpec(shp, lambda: (0,) * len(shp))
    return _pcall(body, name="bias_expand", in_specs=[full(rel_t.shape), full(onehot_t.shape)],
                  out_specs=full((h, n)), out_shape=jax.ShapeDtypeStruct((h, n), F32),
                  compiler_params=pltpu.CompilerParams(vmem_limit_bytes=VMEM_LIMIT))(rel_t, onehot_t)


def _bias_reduce(dbias, onehot, dsink_rows):
    h, n = dbias.shape

    def body(d_ref, o_ref, s_ref, out_ref, so_ref):
        a, b, c = _split3(d_ref[...])
        dot = functools.partial(lax.dot_general, dimension_numbers=_DIMS["nn"], preferred_element_type=F32)
        oh = o_ref[...]
        out_ref[...] = dot(a, oh) + (dot(b, oh) + dot(c, oh))
        so_ref[...] = jnp.sum(s_ref[...], axis=-1, keepdims=True)

    full = lambda shp: pl.BlockSpec(shp, lambda: (0,) * len(shp))
    return _pcall(body, name="bias_reduce", in_specs=[full(dbias.shape), full(onehot.shape), full(dsink_rows.shape)],
                  out_specs=[full((h, REL_BUCKETS)), full((h, 1))],
                  out_shape=[jax.ShapeDtypeStruct((h, REL_BUCKETS), F32), jax.ShapeDtypeStruct((h, 1), F32)],
                  compiler_params=pltpu.CompilerParams(vmem_limit_bytes=VMEM_LIMIT))(dbias, onehot, dsink_rows)


def _swa_specs(s):
    rows = SWA_GROUP * BLOCK
    q_spec = pl.BlockSpec((None, SWA_GROUP, BLOCK, SWA_DH), lambda g, n: (g, 0, n, 0))
    kv_prev = pl.BlockSpec((None, BLOCK, SWA_DH), lambda g, n: (g, jnp.maximum(n - 1, 0), 0))
    kv_cur = pl.BlockSpec((None, BLOCK, SWA_DH), lambda g, n: (g, n, 0))
    bias_spec = pl.BlockSpec((None, rows, 2 * BLOCK), lambda g, n: (g, 0, 0))
    col_spec = pl.BlockSpec((None, rows, 1), lambda g, n: (g, 0, 0))
    lse_spec = pl.BlockSpec((None, None, rows, 1), lambda g, n: (g, n, 0, 0))
    return rows, q_spec, kv_prev, kv_cur, bias_spec, col_spec, lse_spec


def _swa_scores(q_ref, kp_ref, kc_ref, bias_ref, n):
    rows = SWA_GROUP * BLOCK
    q = q_ref[...].reshape(rows, SWA_DH)
    kb = jnp.concatenate([kp_ref[...], kc_ref[...]], axis=0)
    s = lax.dot_general(q, kb, _DIMS["nt"], preferred_element_type=F32) * SWA_SCALE + bias_ref[...]
    col = lax.broadcasted_iota(jnp.int32, s.shape, 1)
    s = jnp.where(jnp.logical_and(n == 0, col < BLOCK), -jnp.inf, s)
    return q, kb, s


def _swa_fwd(q, k, v, bias, sink_rows, comm=None):
    s = q.shape[2]
    nb = s // BLOCK
    rows, q_spec, kv_prev, kv_cur, bias_spec, col_spec, lse_spec = _swa_specs(s)

    def body(q_ref, kp_ref, kc_ref, vp_ref, vc_ref, bias_ref, sink_ref, o_ref, lse_ref):
        n = pl.program_id(1)
        _, _, sc = _swa_scores(q_ref, kp_ref, kc_ref, bias_ref, n)
        sink = sink_ref[...]
        m = jnp.maximum(jnp.max(sc, axis=-1, keepdims=True), sink)
        p = jnp.exp(sc - m)
        den = jnp.sum(p, axis=-1, keepdims=True) + jnp.exp(sink - m)
        vb = jnp.concatenate([vp_ref[...], vc_ref[...]], axis=0)
        o = lax.dot_general(p.astype(BF16), vb, _DIMS["nn"], preferred_element_type=F32) / den
        o_ref[...] = o.reshape(SWA_GROUP, BLOCK, SWA_DH).astype(BF16)
        lse_ref[...] = m + jnp.log(den)

    return _pcall(
        body, comm=comm, name="swa_fwd", grid=(SWA_KV, nb),
        in_specs=[q_spec, kv_prev, kv_cur, kv_prev, kv_cur, bias_spec, col_spec],
        out_specs=[q_spec, lse_spec],
        out_shape=[jax.ShapeDtypeStruct(q.shape, BF16), jax.ShapeDtypeStruct((SWA_KV, nb, rows, 1), F32)],
        compiler_params=_cparams(("parallel", "parallel")))(q, k, k, v, v, bias, sink_rows)


def _swa_bwd(q, k, v, do, lse, bias, sink_rows, comm=None):
    s = q.shape[2]
    nb = s // BLOCK
    rows, q_spec, kv_prev, kv_cur, bias_spec, col_spec, lse_spec = _swa_specs(s)

    def body(q_ref, kp_ref, kc_ref, vp_ref, vc_ref, do_ref, lse_ref, bias_ref, sink_ref,
             dq_ref, dkp_ref, dkc_ref, dvp_ref, dvc_ref, dbias_ref, dsink_ref):
        n = pl.program_id(1)

        @pl.when(n == 0)
        def _():
            dbias_ref[...] = jnp.zeros_like(dbias_ref)
            dsink_ref[...] = jnp.zeros_like(dsink_ref)

        qv, kb, sc = _swa_scores(q_ref, kp_ref, kc_ref, bias_ref, n)
        lse_v = lse_ref[...]
        p = jnp.exp(sc - lse_v)
        p_sink = jnp.exp(sink_ref[...] - lse_v)
        dov = do_ref[...].reshape(rows, SWA_DH)
        vb = jnp.concatenate([vp_ref[...], vc_ref[...]], axis=0)
        dp = lax.dot_general(dov, vb, _DIMS["nt"], preferred_element_type=F32)
        delta = jnp.sum(p * dp, axis=-1, keepdims=True)
        ds = p * (dp - delta)
        dbias_ref[...] += ds
        dsink_ref[...] += -p_sink * delta
        dsb = (ds * SWA_SCALE).astype(BF16)
        dq = lax.dot_general(dsb, kb, _DIMS["nn"], preferred_element_type=F32)
        dq_ref[...] = dq.reshape(SWA_GROUP, BLOCK, SWA_DH).astype(BF16)
        dk = lax.dot_general(dsb, qv, _DIMS["tn"], preferred_element_type=F32)
        dv = lax.dot_general(p.astype(BF16), dov, _DIMS["tn"], preferred_element_type=F32)
        dkp_ref[...] = dk[:BLOCK]
        dkc_ref[...] = dk[BLOCK:]
        dvp_ref[...] = dv[:BLOCK]
        dvc_ref[...] = dv[BLOCK:]

    kv_out = jax.ShapeDtypeStruct((SWA_KV, s, SWA_DH), F32)
    return _pcall(
        body, comm=comm, name="swa_bwd", grid=(SWA_KV, nb),
        in_specs=[q_spec, kv_prev, kv_cur, kv_prev, kv_cur, q_spec, lse_spec, bias_spec, col_spec],
        out_specs=[q_spec, kv_cur, kv_cur, kv_cur, kv_cur, bias_spec, col_spec],
        out_shape=[jax.ShapeDtypeStruct(q.shape, BF16), kv_out, kv_out, kv_out, kv_out,
                   jax.ShapeDtypeStruct(bias.shape, F32), jax.ShapeDtypeStruct(sink_rows.shape, F32)],
        compiler_params=_cparams(("arbitrary", "arbitrary")))(q, k, k, v, v, do, lse, bias, sink_rows)


def _rope_slab(slab, table):
    t = slab * table
    return t + pltpu.roll(t, ROPE, 1)


def _low_lanes(v):
    lane = lax.broadcasted_iota(jnp.int32, v.shape, 1)
    return jnp.where(lane < ROPE, v, 0.0)


def _rms(xv, g):
    r = lax.rsqrt(jnp.mean(xv * xv, axis=-1, keepdims=True) + EPS)
    return xv * r, r


def _mla_prep(proj, gq, gkv, table):
    s = proj.shape[0]
    ts = _pick(s, ROW_TILE, 16)

    def body(p_ref, gq_ref, gkv_ref, t_ref, cq_ref, ckv_ref, kr_ref):
        xq, _ = _rms(p_ref[:, 0:Q_RANK], None)
        cq_ref[...] = (xq * gq_ref[...]).astype(BF16)
        xkv, _ = _rms(p_ref[:, Q_RANK:Q_RANK + KV_RANK], None)
        ckv_ref[...] = (xkv * gkv_ref[...]).astype(BF16)
        kr_ref[...] = _low_lanes(_rope_slab(p_ref[:, Q_RANK + KV_RANK:TAIL], t_ref[...]))

    return _pcall(
        body, name="mla_prep", grid=(s // ts,),
        in_specs=[pl.BlockSpec((ts, TAIL), lambda i: (i, TAIL0 // TAIL)), _vec(Q_RANK), _vec(KV_RANK), _row(ts, 2 * ROPE)],
        out_specs=[_row(ts, Q_RANK), _row(ts, KV_RANK), _row(ts, 2 * ROPE)],
        out_shape=[jax.ShapeDtypeStruct((s, Q_RANK), BF16), jax.ShapeDtypeStruct((s, KV_RANK), BF16),
                   jax.ShapeDtypeStruct((s, 2 * ROPE), F32)],
        compiler_params=_cparams(("parallel",)))(proj, gq, gkv, table)


def _mla_prep_bwd(proj, dcq, dckv, dkr, gq, gkv, table):
    s = proj.shape[0]
    ts = _pick(s, ROW_TILE, 16)

    def norm_bwd(xv, dy, g):
        xn, r = _rms(xv, None)
        dg = jnp.sum(dy * xn, axis=0, keepdims=True)
        dxn = dy * g
        return r * (dxn - xn * jnp.mean(dxn * xn, axis=-1, keepdims=True)), dg

    def body(p_ref, dcq_ref, dckv_ref, dkr_ref, gq_ref, gkv_ref, t_ref, dt_ref, dgq_ref, dgkv_ref):
        @pl.when(pl.program_id(0) == 0)
        def _():
            dgq_ref[...] = jnp.zeros_like(dgq_ref)
            dgkv_ref[...] = jnp.zeros_like(dgkv_ref)

        dxq, dgq = norm_bwd(p_ref[:, 0:Q_RANK], dcq_ref[...], gq_ref[...])
        dxkv, dgkv = norm_bwd(p_ref[:, Q_RANK:Q_RANK + KV_RANK], dckv_ref[...], gkv_ref[...])
        dgq_ref[...] += dgq
        dgkv_ref[...] += dgkv
        d = _low_lanes(dkr_ref[...])
        dslab = (d + pltpu.roll(d, ROPE, 1)) * t_ref[...]
        dt_ref[:, 0:Q_RANK] = dxq.astype(BF16)
        dt_ref[:, Q_RANK:Q_RANK + KV_RANK] = dxkv.astype(BF16)
        dt_ref[:, Q_RANK + KV_RANK:TAIL] = dslab.astype(BF16)

    return _pcall(
        body, name="mla_prep_bwd", grid=(s // ts,),
        in_specs=[pl.BlockSpec((ts, TAIL), lambda i: (i, TAIL0 // TAIL)), _row(ts, Q_RANK), _row(ts, KV_RANK),
                  _row(ts, 2 * ROPE), _vec(Q_RANK), _vec(KV_RANK), _row(ts, 2 * ROPE)],
        out_specs=[_row(ts, TAIL), _vec(Q_RANK), _vec(KV_RANK)],
        out_shape=[jax.ShapeDtypeStruct((s, TAIL), BF16), jax.ShapeDtypeStruct((1, Q_RANK), F32),
                   jax.ShapeDtypeStruct((1, KV_RANK), F32)],
        compiler_params=_cparams(("arbitrary",)))(proj, dcq, dckv, dkr, gq, gkv, table)


def _head_specs(ts):
    tok = lambda w: pl.BlockSpec((ts, w), lambda h, i: (i, 0))
    head = lambda w: pl.BlockSpec((None, ts, w), lambda h, i: (h, i, 0))
    wgt = lambda r, c: pl.BlockSpec((None, r, c), lambda h, i: (h, 0, 0))
    return tok, head, wgt


def _mla_qkv(cq, ckv, kr, wq, wkv, table):
    s = cq.shape[0]
    ts = _pick(s, 2 * ROW_TILE, 16)
    tok, head, wgt = _head_specs(ts)

    def body(cq_ref, ckv_ref, kr_ref, wq_ref, wkv_ref, t_ref, q_ref, k_ref, v_ref):
        qf = lax.dot_general(cq_ref[...], wq_ref[...], _DIMS["nn"], preferred_element_type=F32)
        q_ref[:, 0:NOPE] = qf[:, 0:NOPE].astype(BF16)
        q_ref[:, NOPE:QW] = _rope_slab(qf[:, NOPE:QW], t_ref[...]).astype(BF16)
        kv = lax.dot_general(ckv_ref[...], wkv_ref[...], _DIMS["nn"], preferred_element_type=F32)
        k_ref[:, 0:NOPE] = kv[:, 0:NOPE].astype(BF16)
        k_ref[:, NOPE:QW] = kr_ref[...].astype(BF16)
        v_ref[:, 0:VDIM] = kv[:, NOPE:NOPE + VDIM].astype(BF16)
        lane = lax.broadcasted_iota(jnp.int32, (ts, VDIM), 1)
        v_ref[:, VDIM:2 * VDIM] = jnp.where(lane == 0, 1.0, 0.0).astype(BF16)

    return _pcall(
        body, name="mla_qkv", grid=(MLA_H, s // ts),
        in_specs=[tok(Q_RANK), tok(KV_RANK), tok(2 * ROPE), wgt(Q_RANK, QW), wgt(KV_RANK, NOPE + VDIM), tok(2 * ROPE)],
        out_specs=[head(QW), head(QW), head(2 * VDIM)],
        out_shape=[jax.ShapeDtypeStruct((MLA_H, s, QW), BF16), jax.ShapeDtypeStruct((MLA_H, s, QW), BF16),
                   jax.ShapeDtypeStruct((MLA_H, s, 2 * VDIM), BF16)],
        compiler_params=_cparams(("parallel", "parallel")))(cq, ckv, kr, wq, wkv, table)


def _mla_qkv_bwd(dq, dk, dv, cq, ckv, wq, wkv, table, comm=None):
    s = cq.shape[0]
    ts = _pick(s, 2 * ROW_TILE, 16)
    tok, head, wgt = _head_specs(ts)
    whole = lambda w: pl.BlockSpec((s, w), lambda h, i: (0, 0))

    def body(dq_ref, dk_ref, dv_ref, cq_ref, ckv_ref, wq_ref, wkv_ref, t_ref,
             dcq_ref, dckv_ref, dkr_ref, gwq_ref, gwkv_ref):
        h, i = pl.program_id(0), pl.program_id(1)
        rows = pl.ds(pl.multiple_of(i * ts, ts), ts)
        d = dq_ref[:, NOPE:QW]
        dslab = (d + pltpu.roll(d, ROPE, 1)) * t_ref[...]
        dqe = jnp.concatenate([dq_ref[:, 0:NOPE], dslab], axis=1).astype(BF16)
        dkv = jnp.concatenate([dk_ref[:, 0:NOPE], dv_ref[...]], axis=1).astype(BF16)
        dcq = lax.dot_general(dqe, wq_ref[...], _DIMS["nt"], preferred_element_type=F32)
        dckv = lax.dot_general(dkv, wkv_ref[...], _DIMS["nt"], preferred_element_type=F32)
        gwq = lax.dot_general(cq_ref[...], dqe, _DIMS["tn"], preferred_element_type=F32)
        gwkv = lax.dot_general(ckv_ref[...], dkv, _DIMS["tn"], preferred_element_type=F32)
        dkr = dk_ref[:, NOPE:QW].astype(F32)

        @pl.when(h == 0)
        def _():
            dcq_ref[rows, :] = dcq
            dckv_ref[rows, :] = dckv
            dkr_ref[rows, :] = dkr

        @pl.when(h > 0)
        def _():
            dcq_ref[rows, :] += dcq
            dckv_ref[rows, :] += dckv
            dkr_ref[rows, :] += dkr

        @pl.when(i == 0)
        def _():
            gwq_ref[...] = gwq
            gwkv_ref[...] = gwkv

        @pl.when(i > 0)
        def _():
            gwq_ref[...] += gwq
            gwkv_ref[...] += gwkv

    return _pcall(
        body, comm=comm, name="mla_qkv_bwd", grid=(MLA_H, s // ts),
        in_specs=[head(QW), head(QW), head(VDIM), tok(Q_RANK), tok(KV_RANK), wgt(Q_RANK, QW),
                  wgt(KV_RANK, NOPE + VDIM), tok(2 * ROPE)],
        out_specs=[whole(Q_RANK), whole(KV_RANK), whole(2 * ROPE), wgt(Q_RANK, QW), wgt(KV_RANK, NOPE + VDIM)],
        out_shape=[jax.ShapeDtypeStruct((s, Q_RANK), F32), jax.ShapeDtypeStruct((s, KV_RANK), F32),
                   jax.ShapeDtypeStruct((s, 2 * ROPE), F32), jax.ShapeDtypeStruct((MLA_H, Q_RANK, QW), F32),
                   jax.ShapeDtypeStruct((MLA_H, KV_RANK, NOPE + VDIM), F32)],
        compiler_params=_cparams(("arbitrary", "arbitrary")))(dq, dk, dv, cq, ckv, wq, wkv, table)


def _diag_mask(t):
    return lax.broadcasted_iota(jnp.int32, (t, t), 1) <= lax.broadcasted_iota(jnp.int32, (t, t), 0)


def _mla_fwd(q, k, v, comm=None):
    s = q.shape[1]
    t = _pick(s, ATT_T, 128)
    nt = s // t
    assert nt % 2 == 0
    hb = 2 * MLA_HB

    def fold(p, u):
        first = u <= p
        return jnp.where(first, p, nt - 1 - p), jnp.where(first, u, u - p - 1)

    to_log2 = MLA_SCALE * math.log2(math.e)

    def body(q_ref, k_ref, v_ref, o_ref, oh_ref, lse_ref, m_ref, acc_ref):
        i, j = fold(pl.program_id(1), pl.program_id(2))

        @pl.when(j == 0)
        def _():
            m_ref[...] = jnp.full_like(m_ref, -jnp.inf)
            acc_ref[...] = jnp.zeros_like(acc_ref)

        def step(diagonal):
            for h in range(hb):
                sc = lax.dot_general(q_ref[h], k_ref[h], _DIMS["nt"], preferred_element_type=F32)
                if diagonal:
                    sc = jnp.where(_diag_mask(t), sc, -jnp.inf)
                m_old = m_ref[h]
                m_new = jnp.maximum(m_old, jnp.max(sc, axis=-1, keepdims=True))
                alpha = jnp.exp2((m_old - m_new) * to_log2)
                p = jnp.exp2((sc - m_new) * to_log2)
                acc_ref[h] = alpha * acc_ref[h] + lax.dot_general(p.astype(BF16), v_ref[h], _DIMS["nn"],
                                                                  preferred_element_type=F32)
                m_ref[h] = m_new

        @pl.when(j < i)
        def _():
            step(False)

        @pl.when(j == i)
        def _():
            step(True)
            for h in range(hb):
                den = acc_ref[h, :, VDIM:VDIM + 1]
                o = acc_ref[h, :, 0:VDIM] / den
                o_ref[:, h * VDIM:(h + 1) * VDIM] = o
                oh_ref[:, h * VDIM:(h + 1) * VDIM] = o.astype(BF16)
                lse_ref[h] = m_ref[h] * MLA_SCALE + jnp.log(den)

    o_spec = pl.BlockSpec((t, hb * VDIM), lambda h, p, u: (fold(p, u)[0], h))
    return _pcall(
        body, comm=comm, name="mla_fwd", grid=(MLA_H // hb, nt // 2, nt + 1),
        in_specs=[pl.BlockSpec((hb, t, QW), lambda h, p, u: (h, fold(p, u)[0], 0)),
                  pl.BlockSpec((hb, t, QW), lambda h, p, u: (h, fold(p, u)[1], 0)),
                  pl.BlockSpec((hb, t, 2 * VDIM), lambda h, p, u: (h, fold(p, u)[1], 0))],
        out_specs=[o_spec, o_spec, pl.BlockSpec((hb, t, 1), lambda h, p, u: (h, fold(p, u)[0], 0))],
        out_shape=[jax.ShapeDtypeStruct((s, MLA_H * VDIM), F32), jax.ShapeDtypeStruct((s, MLA_H * VDIM), BF16),
                   jax.ShapeDtypeStruct((MLA_H, s, 1), F32)],
        scratch_shapes=[pltpu.VMEM((hb, t, 1), F32), pltpu.VMEM((hb, t, 2 * VDIM), F32)],
        compiler_params=_cparams(("parallel", "parallel", "arbitrary")))(q, k, v)


def _mla_delta(dmix, o):
    s = o.shape[0]
    ts = _pick(s, ROW_TILE, 16)
    w = MLA_H * VDIM

    def body(d_ref, o_ref, out_ref):
        prod = d_ref[...] * o_ref[...]
        for h in range(MLA_H):
            out_ref[h] = jnp.sum(prod[:, h * VDIM:(h + 1) * VDIM], axis=-1, keepdims=True)

    return _pcall(body, name="mla_delta", grid=(s // ts,),
                  in_specs=[pl.BlockSpec((ts, w), lambda i: (i, SWA_HEADS * SWA_DH // w)), pl.BlockSpec((ts, w), lambda i: (i, 0))],
                  out_specs=pl.BlockSpec((MLA_H, ts, 1), lambda i: (0, i, 0)),
                  out_shape=jax.ShapeDtypeStruct((MLA_H, s, 1), F32), compiler_params=_cparams(("parallel",)))(dmix, o)


def _mla_bwd(q, k, v, dmix, delta, lse, comm=None):
    s = q.shape[1]
    t = _pick(s, ATT_T, 128)
    nt = s // t
    assert nt % 2 == 0
    hb = MLA_HB
    o_blk0 = SWA_HEADS * SWA_DH // (hb * VDIM)

    def fold(p, u):
        first = u < nt - p
        return jnp.where(first, p, nt - 1 - p), jnp.where(first, p + u, u - 1)

    log2e = math.log2(math.e)

    def body(q_ref, k_ref, v_ref, do_ref, delta_ref, lse_ref, dq_ref, dk_ref, dv_ref, dk_acc, dv_acc):
        j, i = fold(pl.program_id(1), pl.program_id(2))
        rows = pl.ds(pl.multiple_of(i * t, t), t)

        @pl.when(i == j)
        def _():
            dk_acc[...] = jnp.zeros_like(dk_acc)
            dv_acc[...] = jnp.zeros_like(dv_acc)

        def step(diagonal):
            for h in range(hb):
                qv, kv_ = q_ref[h], k_ref[h]
                dob = do_ref[:, h * VDIM:(h + 1) * VDIM].astype(BF16)
                st = lax.dot_general(kv_, qv, _DIMS["nt"], preferred_element_type=F32)
                pt = jnp.exp2(st * (MLA_SCALE * log2e) - lse_ref[h] * log2e)
                if diagonal:
                    keep = lax.broadcasted_iota(jnp.int32, (t, t), 0) <= lax.broadcasted_iota(jnp.int32, (t, t), 1)
                    pt = jnp.where(keep, pt, 0.0)
                dpt = lax.dot_general(v_ref[h], dob, _DIMS["nt"], preferred_element_type=F32)
                dst = (pt * (dpt - delta_ref[h]) * MLA_SCALE).astype(BF16)
                dv_acc[h] += lax.dot_general(pt.astype(BF16), dob, _DIMS["nn"], preferred_element_type=F32)
                dk_acc[h] += lax.dot_general(dst, qv, _DIMS["nn"], preferred_element_type=F32)
                dqv = lax.dot_general(dst, kv_, _DIMS["tn"], preferred_element_type=F32)

                @pl.when(j == 0)
                def _():
                    dq_ref[h, rows, :] = dqv

                @pl.when(j > 0)
                def _():
                    dq_ref[h, rows, :] += dqv

        @pl.when(i > j)
        def _():
            step(False)

        @pl.when(i == j)
        def _():
            step(True)

        @pl.when(i == nt - 1)
        def _():
            dk_ref[...] = dk_acc[...].astype(BF16)
            dv_ref[...] = dv_acc[...].astype(BF16)

    qi = lambda h, p, u: (h, fold(p, u)[1], 0)
    kj = lambda h, p, u: (h, fold(p, u)[0], 0)
    row = pl.BlockSpec((hb, 1, t), lambda h, p, u: (h, 0, fold(p, u)[1]))
    return _pcall(
        body, comm=comm, name="mla_bwd", grid=(MLA_H // hb, nt // 2, nt + 1),
        in_specs=[pl.BlockSpec((hb, t, QW), qi), pl.BlockSpec((hb, t, QW), kj), pl.BlockSpec((hb, t, VDIM), kj),
                  pl.BlockSpec((t, hb * VDIM), lambda h, p, u: (fold(p, u)[1], o_blk0 + h)), row, row],
        out_specs=[pl.BlockSpec((hb, s, QW), lambda h, p, u: (h, 0, 0)), pl.BlockSpec((hb, t, QW), kj),
                   pl.BlockSpec((hb, t, VDIM), kj)],
        out_shape=[jax.ShapeDtypeStruct((MLA_H, s, QW), F32), jax.ShapeDtypeStruct((MLA_H, s, QW), BF16),
                   jax.ShapeDtypeStruct((MLA_H, s, VDIM), BF16)],
        scratch_shapes=[pltpu.VMEM((hb, t, QW), F32), pltpu.VMEM((hb, t, VDIM), F32)],
        compiler_params=_cparams(("arbitrary", "arbitrary", "arbitrary")))(q, k, v, dmix, delta, lse)


def _adamw(name, w, g, m, v, parts):
    r, c = w.shape
    n_parts = g.shape[0] if parts else 1
    tr = r if r * c <= ADAM_ELEMS else _pick(r, max(8, ADAM_ELEMS // c // 8 * 8), 8)
    c1 = 1.0 - ADAM_B1 ** ADAM_STEP
    c2 = 1.0 - ADAM_B2 ** ADAM_STEP

    def body(w_ref, g_ref, m_ref, v_ref, go_ref, d_ref, mo_ref, vo_ref):
        if parts:
            gv = g_ref[0].astype(F32)
            for j in range(1, n_parts):
                gv = gv + g_ref[j].astype(F32)
        else:
            gv = g_ref[...]
        mv = ADAM_B1 * m_ref[...] + (1.0 - ADAM_B1) * gv
        vv = ADAM_B2 * v_ref[...] + (1.0 - ADAM_B2) * (gv * gv)
        go_ref[...] = gv
        mo_ref[...] = mv
        vo_ref[...] = vv
        d_ref[...] = -ADAM_LR * ((mv / c1) / (jnp.sqrt(vv / c2) + ADAM_EPS) + ADAM_WD * w_ref[...])

    blk = pl.BlockSpec((tr, c), lambda i: (i, 0))
    g_spec = pl.BlockSpec((n_parts, tr, c), lambda i: (0, i, 0)) if parts else blk
    out = jax.ShapeDtypeStruct((r, c), F32)
    return _pcall(body, name=name, grid=(r // tr,), in_specs=[blk, g_spec, blk, blk], out_specs=[blk] * 4,
                  out_shape=[out] * 4, compiler_params=_cparams(("parallel",)))(w, g, m, v)


def _t5_bucket(dist):
    n = jnp.maximum(dist, 0)
    max_exact = REL_BUCKETS // 2
    nf = jnp.maximum(n, 1).astype(F32)
    large = max_exact + (jnp.log(nf / max_exact) / math.log(REL_MAX_DIST / max_exact)
                         * (REL_BUCKETS - max_exact)).astype(jnp.int32)
    return jnp.where(n < max_exact, n, jnp.minimum(large, REL_BUCKETS - 1))


def _swap_halves(w, r0):
    return jnp.concatenate([w[:, r0 + ROPE // 2:r0 + ROPE], w[:, r0:r0 + ROPE // 2]], axis=1)


def _fold_swapped(g, r0, width):
    sw = g[..., width:width + ROPE]
    half = ROPE // 2
    return jnp.concatenate([g[..., :r0], g[..., r0:r0 + half] + sw[..., half:], g[..., r0 + half:r0 + ROPE] + sw[..., :half],
                            g[..., r0 + ROPE:width]], axis=-1)


def _heads_major(a, heads):
    s = a.shape[0]
    return a.reshape(s, heads, SWA_DH).transpose(1, 0, 2)


def _tokens_major(a):
    h, s, d = a.shape
    return a.transpose(1, 0, 2).reshape(s, h * d)


def kernel(x, c, w_mod, b_mod, attn_norm_g, w_in, swa_sinks, rel_bias, mla_q_norm_g, w_uq, mla_kv_norm_g, w_ukv, w_out, mlp_norm_g, w_ff1, w_ff2, final_norm_g, loss_target, m_w_mod, m_b_mod, m_attn_norm_g, m_w_in, m_swa_sinks, m_rel_bias, m_mla_q_norm_g, m_w_uq, m_mla_kv_norm_g, m_w_ukv, m_w_out, m_mlp_norm_g, m_w_ff1, m_w_ff2, m_final_norm_g, v_w_mod, v_b_mod, v_attn_norm_g, v_w_in, v_swa_sinks, v_rel_bias, v_mla_q_norm_g, v_w_uq, v_mla_kv_norm_g, v_w_ukv, v_w_out, v_mlp_norm_g, v_w_ff1, v_w_ff2, v_final_norm_g):
    s, d = x.shape[1], x.shape[2]
    ffs = w_ff1.shape[2]
    ff = ffs * NDEV
    nmod = w_mod.shape[2]
    me = 4 * lax.axis_index("x") + 2 * lax.axis_index("y") + lax.axis_index("c")
    x2d, tgt = x[0], loss_target[0]
    final_g = final_norm_g.reshape(1, d)

    w_in_l = jnp.concatenate([w_in[0], _swap_halves(w_in[0], OFF_KR)], axis=1).astype(BF16)
    w_uq_l = jnp.concatenate([w_uq[0], _swap_halves(w_uq[0], NOPE)], axis=1).astype(BF16)
    core = jnp.full((1, 128), lax.axis_index("c"), F32)
    (c_all,) = _exchange("gather_c", _Gather([c]))

    b_cols = lax.dynamic_slice(b_mod, (0, me * nmod), (1, nmod))
    act_all, mod_cols, w_in_g, w_uq_g, w_ukv_g = _mod_fwd(
        c_all.reshape(NDEV, d), w_mod[0], b_cols, comm=[_Gather([w_in_l, w_uq_l, w_ukv[0].astype(BF16)])])
    w_in_e = w_in_g.reshape(d, IN_EXT)
    (mod_g,) = _exchange("gather_mod", _Gather([mod_cols]))
    mod = lax.dynamic_index_in_dim(mod_g, me, axis=1, keepdims=False).reshape(1, 6 * d)
    sh1, sc1, g1, sh2, sc2, g2 = [mod[:, i * d:(i + 1) * d] for i in range(6)]

    pos = jnp.arange(s, dtype=F32)
    inv_freq = ROPE_THETA ** (-jnp.arange(ROPE // 2, dtype=F32) / (ROPE // 2))
    ang = pos[:, None] * inv_freq[None, :]
    cos, sin = jnp.cos(ang), jnp.sin(ang)
    table = jnp.concatenate([cos, cos, -sin, sin], axis=1)
    q_loc = jnp.arange(BLOCK)[:, None]
    k_loc = jnp.arange(2 * BLOCK)[None, :]
    dist = q_loc + BLOCK - k_loc
    in_window = (dist >= 0) & (dist < BLOCK)
    onehot = (_t5_bucket(dist).reshape(-1, 1) == jnp.arange(REL_BUCKETS)[None, :]).astype(BF16)
    bias = _bias_expand(rel_bias.T, onehot.T).reshape(SWA_HEADS, BLOCK, 2 * BLOCK)
    bias = jnp.where(in_window[None], bias, -jnp.inf).reshape(SWA_KV, SWA_GROUP * BLOCK, 2 * BLOCK)
    sink_rows = jnp.broadcast_to(swa_sinks.reshape(SWA_HEADS, 1), (SWA_HEADS, BLOCK)).reshape(SWA_KV, SWA_GROUP * BLOCK, 1)

    h1 = _norm_mod("norm1", x2d, attn_norm_g, sc1, sh1)
    def both_dtypes(acc, ex, outs):
        outs[0][...] = acc
        outs[1][...] = acc.astype(BF16)

    tmp = _pick(s, MM_TM // 2, 128)
    proj_blk = pl.BlockSpec((tmp, IN_EXT), lambda i, j, q: (i, 0))
    proj, proj_h = _mm("proj", h1, w_in_e, "nn", (s // tmp, 1, 1), pl.BlockSpec((tmp, d), lambda i, j, q: (i, 0)),
                       pl.BlockSpec((d, IN_EXT), lambda i, j, q: (0, 0)),
                       [jax.ShapeDtypeStruct((s, IN_EXT), F32), jax.ShapeDtypeStruct((s, IN_EXT), BF16)],
                       [proj_blk, proj_blk], (tmp, IN_EXT), both_dtypes)
    q_a = _heads_major(proj_h[:, :OFF_K], SWA_HEADS).reshape(SWA_KV, SWA_GROUP, s, SWA_DH)
    k_a = _heads_major(proj_h[:, OFF_K:OFF_V], SWA_KV)
    v_a = _heads_major(proj_h[:, OFF_V:OFF_CQ], SWA_KV)
    o_a, lse_a, w_out_g = _swa_fwd(q_a, k_a, v_a, bias, sink_rows, comm=[_Gather([w_out[0].astype(BF16)])])
    w_out_f = w_out_g.reshape(MIX, d)

    cq, ckv, kr = _mla_prep(proj, mla_q_norm_g, mla_kv_norm_g, table)
    q_b, k_b, v_b = _mla_qkv(cq, ckv, kr, w_uq_g, w_ukv_g, table)
    o_b, o_bh, lse_b, w_ff1_g = _mla_fwd(q_b, k_b, v_b, comm=[_Gather([w_ff1[0].astype(BF16)])])
    mix = jnp.concatenate([_tokens_major(o_a.reshape(SWA_HEADS, s, SWA_DH)), o_bh], axis=1)

    tm, tn, tk = _pick(s, MM_TM, 128), _pick(d, MM_TN, 128), _pick(MIX, MM_TK, 128)
    row_blk = pl.BlockSpec((tm, tn), lambda i, j, q: (i, j))
    gate_blk = pl.BlockSpec((1, tn), lambda i, j, q: (0, j))

    def gated_residual(acc, ex, outs):
        outs[0][...] = acc
        outs[1][...] = ex[0][...] + ex[1][...] * acc

    y1, x2 = _mm("out_proj", mix, w_out_f, "nn", (s // tm, d // tn, MIX // tk),
                 pl.BlockSpec((tm, tk), lambda i, j, q: (i, q)), pl.BlockSpec((tk, tn), lambda i, j, q: (q, j)),
                 [jax.ShapeDtypeStruct((s, d), F32)] * 2, [row_blk, row_blk], (tm, tn), gated_residual,
                 extras=(x2d, g1), extra_specs=(row_blk, gate_blk))

    h2 = _norm_mod("norm2", x2, mlp_norm_g, sc2, sh2)
    tnf, tkd = _pick(ffs, MM_TN, 128), _pick(d, MM_TK, 128)
    rf = ffs // tnf
    ff_blk = pl.BlockSpec((tm, tnf), lambda i, j, q: (i, j))

    def relu_sq(acc, ex, outs):
        u = jnp.maximum(acc, 0.0)
        outs[0][...] = u
        outs[1][...] = (u * u).astype(BF16)

    u, uu, w_ff2_g = _mm("ff1", h2, w_ff1_g, "nn", (s // tm, ff // tnf, d // tkd),
                         pl.BlockSpec((tm, tkd), lambda i, j, q: (i, q)),
                         pl.BlockSpec((None, tkd, tnf), lambda i, j, q: (j // rf, q, j % rf)),
                         [jax.ShapeDtypeStruct((s, ff), F32), jax.ShapeDtypeStruct((s, ff), BF16)], [ff_blk, ff_blk],
                         (tm, tnf), relu_sq, comm=[_Gather([w_ff2[0].astype(BF16)])])
    w_ff2_f = w_ff2_g.reshape(ff, d)
    tkf = _pick(ff, MM_TK, 128)
    y2, x3 = _mm("ff2", uu, w_ff2_f, "nn", (s // tm, d // tn, ff // tkf),
                 pl.BlockSpec((tm, tkf), lambda i, j, q: (i, q)), pl.BlockSpec((tkf, tn), lambda i, j, q: (q, j)),
                 [jax.ShapeDtypeStruct((s, d), F32)] * 2, [row_blk, row_blk], (tm, tn), gated_residual,
                 extras=(x2, g2), extra_specs=(row_blk, gate_blk))

    dx3, dy2, loss_p, dgf, dg2 = _loss_head(x3, tgt, y2, final_g, g2)
    loss = lax.psum(loss_p[0, 0], ("x", "y", "c"))

    def relu_sq_bwd(acc, ex, outs):
        outs[0][...] = (acc * (2.0 * ex[0][...])).astype(BF16)

    tnf2 = _pick(ff, MM_TN, 128)
    du = _mm("ff2_dx", dy2, w_ff2_f, "nt", (s // tm, ff // tnf2, d // tkd),
             pl.BlockSpec((tm, tkd), lambda i, j, q: (i, q)), pl.BlockSpec((tnf2, tkd), lambda i, j, q: (j, q)),
             [jax.ShapeDtypeStruct((s, ff), BF16)], [pl.BlockSpec((tm, tnf2), lambda i, j, q: (i, j))],
             (tm, tnf2), relu_sq_bwd, extras=(u,), extra_specs=(pl.BlockSpec((tm, tnf2), lambda i, j, q: (i, j)),))[0]
    gw_ff2 = _mm_plain("ff2_dw", uu, dy2, "tn", ff, d, s, BF16)
    tmd, tks = _pick(d, MM_TM, 128), _pick(s, MM_TK, 128)
    gw_ff2 = gw_ff2.reshape(NDEV, ffs, d)
    dh2, s_ff2 = _mm("ff1_dx", du, w_ff1_g, "nt", (s // tm, d // tn, NDEV // 2),
                     pl.BlockSpec((tm, 2 * ffs), lambda i, j, q: (i, q)),
                     pl.BlockSpec((2, tn, ffs), lambda i, j, q: (q, j, 0)),
                     [jax.ShapeDtypeStruct((s, d), F32)], [row_blk], (tm, tn), _store(F32),
                     comm=[_PairSwap([gw_ff2])], b_parts=2)
    c_ff2 = _pair_sum("pair_ff2", gw_ff2, s_ff2, core)
    gw_ff1, p_ff2 = _mm("ff1_dw", h2, du, "tn", (d // tmd, ff // tnf, s // tks),
                        pl.BlockSpec((tks, tmd), lambda i, j, q: (q, i)), pl.BlockSpec((tks, tnf), lambda i, j, q: (q, j)),
                        [jax.ShapeDtypeStruct((NDEV, d, ffs), BF16)],
                        [pl.BlockSpec((None, tmd, tnf), lambda i, j, q: (j // rf, i, j % rf))], (tmd, tnf), _store(BF16),
                        comm=[_ChipScatter([c_ff2])])
    dx2, dy1, dsc2, dsh2, dgm, dg1, s_ff1 = _norm_mod_bwd("norm2_bwd", x2, dh2, dx3, mlp_norm_g, sc2, y1, g1,
                                                          comm=[_PairSwap([gw_ff1])])
    c_ff1 = _pair_sum("pair_ff1", gw_ff1, s_ff1, core)

    dmix = _mm_plain("out_proj_dx", dy1, w_out_f, "nt", s, MIX, d, F32)
    gw_out = _mm_plain("out_proj_dw", mix, dy1, "tn", MIX, d, s, BF16).reshape(NDEV, MIX // NDEV, d)

    delta_b = _mla_delta(dmix, o_b).reshape(MLA_H, 1, s)
    dq_b, dk_b, dv_b, p_ff1, s_out = _mla_bwd(q_b, k_b, v_b, dmix, delta_b, lse_b.reshape(MLA_H, 1, s),
                                              comm=[_ChipScatter([c_ff1]), _PairSwap([gw_out])])
    c_out = _pair_sum("pair_out", gw_out, s_out, core)
    dcq, dckv, dkr, gw_uq_e, gw_ukv, p_out = _mla_qkv_bwd(dq_b, dk_b, dv_b, cq, ckv, w_uq_g, w_ukv_g, table,
                                                          comm=[_ChipScatter([c_out])])
    dtail, dgq, dgkv = _mla_prep_bwd(proj, dcq, dckv, dkr, mla_q_norm_g, mla_kv_norm_g, table)
    gw_uq = _fold_swapped(gw_uq_e, NOPE, NOPE + ROPE).astype(BF16)
    gw_ukv = gw_ukv.astype(BF16)

    do_a = _heads_major(dmix[:, :OFF_K].astype(BF16), SWA_HEADS).reshape(SWA_KV, SWA_GROUP, s, SWA_DH)
    dq_a, dkp, dkc, dvp, dvc, dbias, dsink, s_uq, s_ukv = _swa_bwd(
        q_a, k_a, v_a, do_a, lse_a, bias, sink_rows, comm=[_PairSwap([gw_uq, gw_ukv])])
    c_uq = _pair_sum("pair_uq", gw_uq, s_uq, core)
    c_ukv = _pair_sum("pair_ukv", gw_ukv, s_ukv, core)
    shift = lambda p: jnp.concatenate([p[:, BLOCK:], jnp.zeros_like(p[:, :BLOCK])], axis=1)
    dk_a, dv_a = dkc + shift(dkp), dvc + shift(dvp)
    drel_t, dsinks = _bias_reduce(dbias.reshape(SWA_HEADS, BLOCK * 2 * BLOCK), onehot, dsink.reshape(SWA_HEADS, BLOCK))
    dproj = jnp.concatenate([_tokens_major(dq_a.reshape(SWA_HEADS, s, SWA_DH)),
                             _tokens_major(dk_a).astype(BF16), _tokens_major(dv_a).astype(BF16), dtail], axis=1)
    gw_in_e = _mm_plain("proj_dw", h1, dproj, "tn", d, IN_EXT, s, F32, tn=TAIL)
    gw_in = _fold_swapped(gw_in_e, OFF_KR, IN_COLS).reshape(NDEV, d // NDEV, IN_COLS).astype(BF16)
    tkt = IN_EXT
    dh1, s_in, p_uq, p_ukv = _mm(
        "proj_dx", dproj, w_in_e, "nt", (s // tm, d // tn, IN_EXT // tkt),
        pl.BlockSpec((tm, tkt), lambda i, j, q: (i, q)), pl.BlockSpec((tn, tkt), lambda i, j, q: (j, q)),
        [jax.ShapeDtypeStruct((s, d), F32)], [row_blk], (tm, tn), _store(F32),
        comm=[_PairSwap([gw_in]), _ChipScatter([c_uq, c_ukv])])
    c_in = _pair_sum("pair_in", gw_in, s_in, core)
    gx, dsc1, dsh1, dga, p_in = _norm_mod_bwd("norm1_bwd", x2d, dh1, dx2, attn_norm_g, sc1,
                                              comm=[_ChipScatter([c_in])])

    small = [jnp.concatenate([dsh1, dsc1, dg1, dsh2, dsc2, dg2], axis=1), dga, dgm, dgf, dgq, dgkv,
             dsinks.reshape(1, SWA_HEADS), drel_t.T.reshape(1, REL_BUCKETS * SWA_HEADS)]
    n_small = sum(a.shape[1] for a in small)
    n_pad = -n_small % 1024
    rows_small = (n_small + n_pad) // 128
    pad = jnp.zeros((1, n_pad), F32)
    pack = lambda parts: jnp.concatenate([p.reshape(1, -1) for p in parts] + [pad], axis=1).reshape(rows_small, 128)
    (small_g,) = _exchange("gather_small", _Gather([pack(small)]))
    small_names = (b_mod, attn_norm_g, mlp_norm_g, final_norm_g, mla_q_norm_g, mla_kv_norm_g, swa_sinks, rel_bias)
    small_m = (m_b_mod, m_attn_norm_g, m_mlp_norm_g, m_final_norm_g, m_mla_q_norm_g, m_mla_kv_norm_g, m_swa_sinks, m_rel_bias)
    small_v = (v_b_mod, v_attn_norm_g, v_mlp_norm_g, v_final_norm_g, v_mla_q_norm_g, v_mla_kv_norm_g, v_swa_sinks, v_rel_bias)
    small_out = _adamw("adamw_small", pack(small_names), small_g, pack(small_m), pack(small_v), parts=True)

    def unpack(flat):
        flat = flat.reshape(1, -1)
        out, off = [], 0
        for a in small_names:
            out.append(flat[:, off:off + a.size].reshape(a.shape))
            off += a.size
        return out

    sg, sd, sm, sv = [unpack(o) for o in small_out]

    dmod_cols = lax.dynamic_slice(small_g.reshape(NDEV, -1), (0, me * nmod), (NDEV, nmod))
    gw_mod = _mod_wgrad(act_all, dmod_cols)
    big = {"w_mod": _adamw("adamw_w_mod", w_mod[0], gw_mod, m_w_mod[0], v_w_mod[0], parts=False)}

    for name, w, p, m, v in (("w_in", w_in, p_in, m_w_in, v_w_in), ("w_uq", w_uq, p_uq, m_w_uq, v_w_uq),
                             ("w_ukv", w_ukv, p_ukv, m_w_ukv, v_w_ukv), ("w_out", w_out, p_out, m_w_out, v_w_out),
                             ("w_ff1", w_ff1, p_ff1, m_w_ff1, v_w_ff1), ("w_ff2", w_ff2, p_ff2, m_w_ff2, v_w_ff2)):
        big[name] = _adamw("adamw_" + name, w[0], p, m[0], v[0], parts=True)

    order = ("w_mod", "b_mod", "attn_norm_g", "w_in", "swa_sinks", "rel_bias", "mla_q_norm_g", "w_uq", "mla_kv_norm_g",
             "w_ukv", "w_out", "mlp_norm_g", "w_ff1", "w_ff2", "final_norm_g")
    small_idx = {"b_mod": 0, "attn_norm_g": 1, "mlp_norm_g": 2, "final_norm_g": 3, "mla_q_norm_g": 4,
                 "mla_kv_norm_g": 5, "swa_sinks": 6, "rel_bias": 7}
    outs = []
    for kind, small_list in enumerate((sg, sd, sm, sv)):
        for name in order:
            outs.append(small_list[small_idx[name]] if name in small_idx else big[name][kind][None])
    return (loss, gx[None], *outs)
```

```python
import functools
import math

import jax
import jax.numpy as jnp
from jax import lax
from jax.experimental import pallas as pl
from jax.experimental.pallas import tpu as pltpu

F32 = jnp.float32
BF16 = jnp.bfloat16

NDEV = 8
EPS = 1e-6
BLOCK = 128
SWA_HEADS, SWA_KV, SWA_DH, SWA_GROUP = 16, 2, 64, 8
REL_BUCKETS, REL_MAX_DIST = 32, 128
MLA_H, Q_RANK, KV_RANK, NOPE, ROPE, VDIM = 8, 384, 128, 128, 64, 128
ROPE_THETA = 10000.0
OFF_K, OFF_V, OFF_CQ, OFF_CKV, OFF_KR, IN_COLS = 1024, 1152, 1280, 1664, 1792, 1856
IN_EXT = IN_COLS + ROPE
TAIL0, TAIL = OFF_CQ, IN_EXT - OFF_CQ
QW = NOPE + 2 * ROPE
MIX = SWA_HEADS * SWA_DH + MLA_H * VDIM
MLA_SCALE = (NOPE + ROPE) ** -0.5
SWA_SCALE = SWA_DH ** -0.5

ADAM_LR, ADAM_B1, ADAM_B2, ADAM_EPS, ADAM_WD, ADAM_STEP = 0.001, 0.9, 0.999, 1e-08, 0.01, 10

VMEM_LIMIT = 52 * 1024 * 1024
ROW_TILE = 256
MM_TM, MM_TN, MM_TK = 1024, 1024, 2048
ATT_T = 512
MLA_HB = 2
ADAM_ELEMS = 128 * 1024


MESH_ID = pl.DeviceIdType.MESH


def _place():
    x, y, c = lax.axis_index("x"), lax.axis_index("y"), lax.axis_index("c")
    return x, y, c, 2 * x + y


def _chip(x, y, k):
    return (1 - x if k & 2 else x, 1 - y if k & 1 else y)


def _dma_sems(*counts):
    return [pltpu.SemaphoreType.DMA((n,)) for n in counts]


class _Gather:
    def __init__(self, arrays):
        self.arrays = list(arrays)
        n = len(self.arrays)
        self.out_shape = [jax.ShapeDtypeStruct((NDEV,) + a.shape, a.dtype) for a in self.arrays]
        self.sems = _dma_sems(7 * n, 7 * n, n)

    def _copy(self, sems, a, k, src, dst, to):
        return pltpu.make_async_remote_copy(src_ref=src, dst_ref=dst, send_sem=sems[0].at[7 * a + k],
                                            recv_sem=sems[1].at[7 * a + k], device_id=to, device_id_type=MESH_ID)

    def start(self, ins, outs, sems):
        x, y, c, q = _place()
        me = 2 * q + c
        for a in range(len(ins)):
            pltpu.make_async_copy(ins[a], outs[a].at[me], sems[2].at[a]).start()
            self._copy(sems, a, 0, ins[a], outs[a].at[me], (x, y, 1 - c)).start()
            for k in (1, 2, 3):
                self._copy(sems, a, k, ins[a], outs[a].at[me], (*_chip(x, y, k), c)).start()

    def finish(self, ins, outs, sems):
        x, y, c, q = _place()
        me, sib = 2 * q + c, (x, y, 1 - c)
        n = len(ins)
        for k in (1, 2, 3):
            for a in range(n):
                blk = outs[a].at[2 * (q ^ k) + c]
                self._copy(sems, a, k, ins[a], blk, (*_chip(x, y, k), c)).wait_recv()
                self._copy(sems, a, 3 + k, blk, blk, sib).start()
        for a in range(n):
            self._copy(sems, a, 0, ins[a], outs[a].at[2 * q + 1 - c], sib).wait_recv()
            for k in (1, 2, 3):
                blk = outs[a].at[2 * (q ^ k) + 1 - c]
                self._copy(sems, a, 3 + k, blk, blk, sib).wait_recv()
        for a in range(n):
            for k in range(7):
                self._copy(sems, a, k, ins[a], outs[a].at[me], sib).wait_send()
            pltpu.make_async_copy(ins[a], outs[a].at[me], sems[2].at[a]).wait()


class _PairSwap:
    def __init__(self, arrays):
        self.arrays = list(arrays)
        n = len(self.arrays)
        self.out_shape = [jax.ShapeDtypeStruct((NDEV // 2,) + a.shape[1:], a.dtype) for a in self.arrays]
        self.sems = _dma_sems(4 * n, 4 * n)

    def _copy(self, sems, a, p, src, dst, to):
        return pltpu.make_async_remote_copy(src_ref=src, dst_ref=dst, send_sem=sems[0].at[4 * a + p],
                                            recv_sem=sems[1].at[4 * a + p], device_id=to, device_id_type=MESH_ID)

    def start(self, ins, outs, sems):
        x, y, c, _ = _place()
        for a in range(len(ins)):
            for p in range(4):
                self._copy(sems, a, p, ins[a].at[2 * p + 1 - c], outs[a].at[p], (x, y, 1 - c)).start()

    def finish(self, ins, outs, sems):
        x, y, c, _ = _place()
        for a in range(len(ins)):
            for p in range(4):
                cp = self._copy(sems, a, p, ins[a].at[2 * p + 1 - c], outs[a].at[p], (x, y, 1 - c))
                cp.wait_recv()
                cp.wait_send()


class _ChipScatter:
    def __init__(self, arrays):
        self.arrays = list(arrays)
        n = len(self.arrays)
        self.out_shape = [jax.ShapeDtypeStruct(a.shape, a.dtype) for a in self.arrays]
        self.sems = _dma_sems(3 * n, 3 * n, n)

    def _copy(self, sems, a, k, src, dst, to):
        return pltpu.make_async_remote_copy(src_ref=src, dst_ref=dst, send_sem=sems[0].at[3 * a + k - 1],
                                            recv_sem=sems[1].at[3 * a + k - 1], device_id=to, device_id_type=MESH_ID)

    def start(self, ins, outs, sems):
        x, y, c, q = _place()
        for a in range(len(ins)):
            pltpu.make_async_copy(ins[a].at[q], outs[a].at[q], sems[2].at[a]).start()
            for k in (1, 2, 3):
                self._copy(sems, a, k, ins[a].at[q ^ k], outs[a].at[q], (*_chip(x, y, k), c)).start()

    def finish(self, ins, outs, sems):
        x, y, c, q = _place()
        for a in range(len(ins)):
            for k in (1, 2, 3):
                cp = self._copy(sems, a, k, ins[a].at[q ^ k], outs[a].at[q ^ k], (*_chip(x, y, k), c))
                cp.wait_recv()
                cp.wait_send()
            pltpu.make_async_copy(ins[a].at[q], outs[a].at[q], sems[2].at[a]).wait()


def _call(body, **kw):
    return pl.pallas_call(body, **kw)


def _pcall(body, comm=None, **kw):
    if not comm:
        return _call(body, **kw)
    grid = kw["grid"]
    in_specs, out_specs, out_shape = list(kw["in_specs"]), list(kw["out_specs"]), list(kw["out_shape"])
    scratch = list(kw.get("scratch_shapes", ()))
    n_in, n_out, n_scr = len(in_specs), len(out_shape), len(scratch)
    n_cin = [len(j.arrays) for j in comm]
    n_sem = [len(j.sems) for j in comm]
    n = sum(n_cin)
    hbm = pl.BlockSpec(memory_space=pltpu.HBM)

    def carried(*refs):
        ins, cins = refs[:n_in], refs[n_in:n_in + n]
        outs, couts = refs[n_in + n:n_in + n + n_out], refs[n_in + n + n_out:n_in + 2 * n + n_out]
        scr, sems = refs[n_in + 2 * n + n_out:n_in + 2 * n + n_out + n_scr], refs[n_in + 2 * n + n_out + n_scr:]
        ids = [pl.program_id(ax) for ax in range(len(grid))]
        first = functools.reduce(jnp.logical_and, [i == 0 for i in ids])
        last = functools.reduce(jnp.logical_and, [i == g - 1 for i, g in zip(ids, grid)])

        def each(method):
            ai = si = 0
            for job, na, ns in zip(comm, n_cin, n_sem):
                getattr(job, method)(cins[ai:ai + na], couts[ai:ai + na], sems[si:si + ns])
                ai, si = ai + na, si + ns

        @pl.when(first)
        def _():
            each("start")

        body(*ins, *outs, *scr)

        @pl.when(last)
        def _():
            each("finish")

    kw.update(in_specs=in_specs + [hbm] * n, out_specs=out_specs + [hbm] * n,
              out_shape=out_shape + [o for j in comm for o in j.out_shape],
              scratch_shapes=scratch + [sm for j in comm for sm in j.sems],
              compiler_params=_cparams(("arbitrary",) * len(grid)))
    call = _call(carried, **kw)
    return lambda *args: call(*args, *[a for j in comm for a in j.arrays])


def _cparams(sem):
    return pltpu.CompilerParams(dimension_semantics=sem, vmem_limit_bytes=VMEM_LIMIT)


def _pick(n, pref, align):
    if n <= pref:
        return n
    t = (pref // align) * align
    while t >= align:
        if n % t == 0:
            return t
        t -= align
    return n


def _split3(x):
    a = x.astype(BF16)
    r = x - a.astype(F32)
    b = r.astype(BF16)
    c = (r - b.astype(F32)).astype(BF16)
    return a, b, c


def _exchange(name, job):
    n = len(job.arrays)

    def body(*refs):
        ins, outs, sems = refs[:n], refs[n:2 * n], refs[2 * n:]
        job.start(ins, outs, sems)
        job.finish(ins, outs, sems)

    hbm = pl.BlockSpec(memory_space=pltpu.HBM)
    return _call(body, name=name, out_shape=job.out_shape, in_specs=[hbm] * n, out_specs=[hbm] * n,
                 scratch_shapes=job.sems)(*job.arrays)


def _pair_sum(name, g, r, core):
    _, rr, cc = g.shape
    tr = rr if rr * cc <= 4 * ADAM_ELEMS else _pick(rr, max(16, 4 * ADAM_ELEMS // cc // 16 * 16), 16)

    def body(g_ref, r_ref, c_ref, o_ref):
        north = c_ref[:, 0:1] > 0.5
        mine = jnp.where(north, g_ref[1].astype(F32), g_ref[0].astype(F32))
        o_ref[...] = (mine + r_ref[...].astype(F32)).astype(o_ref.dtype)

    return _pcall(
        body, name=name, grid=(NDEV // 2, rr // tr),
        in_specs=[pl.BlockSpec((None, 2, tr, cc), lambda p, i: (p, 0, i, 0)),
                  pl.BlockSpec((None, tr, cc), lambda p, i: (p, i, 0)), pl.BlockSpec((1, 128), lambda p, i: (0, 0))],
        out_specs=pl.BlockSpec((None, tr, cc), lambda p, i: (p, i, 0)),
        out_shape=jax.ShapeDtypeStruct((NDEV // 2, rr, cc), g.dtype),
        compiler_params=_cparams(("parallel", "parallel")))(g.reshape(NDEV // 2, 2, rr, cc), r, core)


_DIMS = {"nn": (((1,), (0,)), ((), ())), "nt": (((1,), (1,)), ((), ())), "tn": (((0,), (0,)), ((), ()))}


def _mm(name, a, b, kind, grid, a_spec, b_spec, out_shape, out_specs, acc_shape, epilogue,
        extras=(), extra_specs=(), comm=None, b_parts=1):
    nk, ne, no = grid[2], len(extras), len(out_shape)

    def body(*refs):
        a_ref, b_ref = refs[0], refs[1]
        ex, outs = refs[2:2 + ne], refs[2 + ne:2 + ne + no]
        if b_parts == 1:
            part = lax.dot_general(a_ref[...].astype(BF16), b_ref[...].astype(BF16), _DIMS[kind],
                                   preferred_element_type=F32)
        else:
            kp = a_ref.shape[1] // b_parts
            part = sum(lax.dot_general(a_ref[:, p * kp:(p + 1) * kp].astype(BF16), b_ref[p].astype(BF16), _DIMS[kind],
                                       preferred_element_type=F32) for p in range(b_parts))
        if nk == 1:
            epilogue(part, ex, outs)
            return
        acc = refs[-1]
        k = pl.program_id(2)

        @pl.when(k == 0)
        def _():
            acc[...] = part

        @pl.when(jnp.logical_and(k > 0, k < nk - 1))
        def _():
            acc[...] += part

        @pl.when(k == nk - 1)
        def _():
            epilogue(acc[...] + part, ex, outs)

    return _pcall(
        body, comm=comm, name=name, grid=grid, in_specs=[a_spec, b_spec, *extra_specs], out_specs=out_specs,
        out_shape=out_shape, scratch_shapes=[pltpu.VMEM(acc_shape, F32)] if nk > 1 else [],
        compiler_params=_cparams(("parallel", "parallel", "arbitrary")),
    )(a, b, *extras)


def _store(dtype):
    def epi(acc, ex, outs):
        outs[0][...] = acc.astype(dtype)
    return epi


def _mm_plain(name, a, b, kind, m, n, k, out_dtype, tm=None, tn=None, tk=None):
    tm = _pick(m, tm or MM_TM, 128)
    tn = _pick(n, tn or MM_TN, 128)
    tk = _pick(k, tk or MM_TK, 128)
    a_spec = pl.BlockSpec((tk, tm), lambda i, j, q: (q, i)) if kind == "tn" else pl.BlockSpec((tm, tk), lambda i, j, q: (i, q))
    b_spec = pl.BlockSpec((tn, tk), lambda i, j, q: (j, q)) if kind == "nt" else pl.BlockSpec((tk, tn), lambda i, j, q: (q, j))
    return _mm(name, a, b, kind, (m // tm, n // tn, k // tk), a_spec, b_spec,
               [jax.ShapeDtypeStruct((m, n), out_dtype)], [pl.BlockSpec((tm, tn), lambda i, j, q: (i, j))],
               (tm, tn), _store(out_dtype))[0]


def _row(ts, d):
    return pl.BlockSpec((ts, d), lambda i: (i, 0))


def _vec(d):
    return pl.BlockSpec((1, d), lambda i: (0, 0))


def _norm_mod(name, x, gain, sc, sh):
    s, d = x.shape
    ts = _pick(s, ROW_TILE, 16)

    def body(x_ref, g_ref, sc_ref, sh_ref, h_ref):
        xv = x_ref[...]
        r = lax.rsqrt(jnp.mean(xv * xv, axis=-1, keepdims=True) + EPS)
        h_ref[...] = ((xv * r) * g_ref[...] * (1.0 + sc_ref[...]) + sh_ref[...]).astype(BF16)

    return _pcall(body, name=name, grid=(s // ts,), in_specs=[_row(ts, d), _vec(d), _vec(d), _vec(d)],
                  out_specs=_row(ts, d), out_shape=jax.ShapeDtypeStruct((s, d), BF16),
                  compiler_params=_cparams(("parallel",)))(x, gain, sc, sh)


def _loss_head(x3, tgt, y2, gf, g2):
    s, d = x3.shape
    ts = _pick(s, ROW_TILE, 16)

    def body(x_ref, t_ref, y_ref, gf_ref, g2_ref, dx_ref, dy_ref, loss_ref, dgf_ref, dg2_ref):
        @pl.when(pl.program_id(0) == 0)
        def _():
            loss_ref[...] = jnp.zeros_like(loss_ref)
            dgf_ref[...] = jnp.zeros_like(dgf_ref)
            dg2_ref[...] = jnp.zeros_like(dg2_ref)

        xv = x_ref[...]
        r = lax.rsqrt(jnp.mean(xv * xv, axis=-1, keepdims=True) + EPS)
        xn = xv * r
        err = xn * gf_ref[...] - t_ref[...]
        loss_ref[...] += 0.5 * jnp.sum(jnp.mean(err * err, axis=-1, keepdims=True), axis=0, keepdims=True)
        dout = err * (1.0 / d)
        dgf_ref[...] += jnp.sum(dout * xn, axis=0, keepdims=True)
        dxn = dout * gf_ref[...]
        dx = r * (dxn - xn * jnp.mean(dxn * xn, axis=-1, keepdims=True))
        dx_ref[...] = dx
        dy_ref[...] = (dx * g2_ref[...]).astype(BF16)
        dg2_ref[...] += jnp.sum(dx * y_ref[...], axis=0, keepdims=True)

    one = pl.BlockSpec((1, 1), lambda i: (0, 0))
    return _pcall(
        body, name="loss_head", grid=(s // ts,),
        in_specs=[_row(ts, d), _row(ts, d), _row(ts, d), _vec(d), _vec(d)],
        out_specs=[_row(ts, d), _row(ts, d), one, _vec(d), _vec(d)],
        out_shape=[jax.ShapeDtypeStruct((s, d), F32), jax.ShapeDtypeStruct((s, d), BF16),
                   jax.ShapeDtypeStruct((1, 1), F32), jax.ShapeDtypeStruct((1, d), F32),
                   jax.ShapeDtypeStruct((1, d), F32)],
        compiler_params=_cparams(("arbitrary",)))(x3, tgt, y2, gf, g2)


def _norm_mod_bwd(name, x, dh, dres, gain, sc, y_prev=None, gate=None, comm=None):
    s, d = x.shape
    ts = _pick(s, ROW_TILE, 16)
    gated = y_prev is not None

    def body(*refs):
        if gated:
            x_ref, dh_ref, dr_ref, g_ref, sc_ref, y_ref, gt_ref, dx_ref, dy_ref, dsc_ref, dsh_ref, dg_ref, dgt_ref = refs
        else:
            x_ref, dh_ref, dr_ref, g_ref, sc_ref, dx_ref, dsc_ref, dsh_ref, dg_ref = refs

        @pl.when(pl.program_id(0) == 0)
        def _():
            dsc_ref[...] = jnp.zeros_like(dsc_ref)
            dsh_ref[...] = jnp.zeros_like(dsh_ref)
            dg_ref[...] = jnp.zeros_like(dg_ref)
            if gated:
                dgt_ref[...] = jnp.zeros_like(dgt_ref)

        xv, dhv = x_ref[...], dh_ref[...]
        r = lax.rsqrt(jnp.mean(xv * xv, axis=-1, keepdims=True) + EPS)
        xn = xv * r
        dsc_ref[...] += jnp.sum(dhv * (xn * g_ref[...]), axis=0, keepdims=True)
        dsh_ref[...] += jnp.sum(dhv, axis=0, keepdims=True)
        da = dhv * (1.0 + sc_ref[...])
        dg_ref[...] += jnp.sum(da * xn, axis=0, keepdims=True)
        dxn = da * g_ref[...]
        dx = dr_ref[...] + r * (dxn - xn * jnp.mean(dxn * xn, axis=-1, keepdims=True))
        dx_ref[...] = dx
        if gated:
            dy_ref[...] = (dx * gt_ref[...]).astype(BF16)
            dgt_ref[...] += jnp.sum(dx * y_ref[...], axis=0, keepdims=True)

    ins = [x, dh, dres, gain, sc] + ([y_prev, gate] if gated else [])
    in_specs = [_row(ts, d)] * 3 + [_vec(d)] * 2 + ([_row(ts, d), _vec(d)] if gated else [])
    vec_out = jax.ShapeDtypeStruct((1, d), F32)
    out_shape = [jax.ShapeDtypeStruct((s, d), F32)] + ([jax.ShapeDtypeStruct((s, d), BF16)] if gated else [])
    out_shape += [vec_out] * (4 if gated else 3)
    out_specs = [_row(ts, d)] * (2 if gated else 1) + [_vec(d)] * (4 if gated else 3)
    return _pcall(body, comm=comm, name=name, grid=(s // ts,), in_specs=in_specs, out_specs=out_specs,
                  out_shape=out_shape, compiler_params=_cparams(("arbitrary",)))(*ins)


def _dot3(a, b, dims):
    a1, a2, _ = _split3(a)
    b1, b2, _ = _split3(b)
    dot = functools.partial(lax.dot_general, dimension_numbers=dims, preferred_element_type=F32)
    return dot(a1, b1) + (dot(a1, b2) + dot(a2, b1))


def _mod_fwd(c_all, w, b_cols, comm=None):
    nb, d = c_all.shape
    n = w.shape[1]
    tk = _pick(d, 512, 128)
    nk = d // tk

    def body(c_ref, w_ref, b_ref, act_ref, out_ref):
        k = pl.program_id(0)
        cv = c_ref[...]
        act = cv * (1.0 / (1.0 + jnp.exp(-cv)))
        act_ref[...] = act

        @pl.when(k == 0)
        def _():
            out_ref[...] = jnp.broadcast_to(b_ref[...], out_ref.shape)

        out_ref[...] += _dot3(act, w_ref[...], _DIMS["nn"])

    return _pcall(
        body, comm=comm, name="mod_fwd", grid=(nk,),
        in_specs=[pl.BlockSpec((nb, tk), lambda k: (0, k)), pl.BlockSpec((tk, n), lambda k: (k, 0)),
                  pl.BlockSpec((1, n), lambda k: (0, 0))],
        out_specs=[pl.BlockSpec((nb, tk), lambda k: (0, k)), pl.BlockSpec((nb, n), lambda k: (0, 0))],
        out_shape=[jax.ShapeDtypeStruct((nb, d), F32), jax.ShapeDtypeStruct((nb, n), F32)],
        compiler_params=_cparams(("arbitrary",)))(c_all, w, b_cols)


def _mod_wgrad(act_all, dmod_cols):
    nb, d = act_all.shape
    n = dmod_cols.shape[1]
    tm = _pick(d, 512, 128)

    def body(a_ref, d_ref, o_ref):
        o_ref[...] = _dot3(a_ref[...], d_ref[...], _DIMS["tn"])

    return _pcall(
        body, name="mod_wgrad", grid=(d // tm,),
        in_specs=[pl.BlockSpec((nb, tm), lambda i: (0, i)), pl.BlockSpec((nb, n), lambda i: (0, 0))],
        out_specs=pl.BlockSpec((tm, n), lambda i: (i, 0)), out_shape=jax.ShapeDtypeStruct((d, n), F32),
        compiler_params=_cparams(("parallel",)))(act_all, dmod_cols)


def _bias_expand(rel_t, onehot_t):
    h, _ = rel_t.shape
    n = onehot_t.shape[1]

    def body(r_ref, o_ref, out_ref):
        a, b, c = _split3(r_ref[...])
        dot = functools.partial(lax.dot_general, dimension_numbers=_DIMS["nn"], preferred_element_type=F32)
        oh = o_ref[...]
        out_ref[...] = dot(a, oh) + (dot(b, oh) + dot(c, oh))

    full = lambda shp: pl.BlockSpec(shp, lambda: (0,) * len(shp))
    return _pcall(body, name="bias_expand", in_specs=[full(rel_t.shape), full(onehot_t.shape)],
                  out_specs=full((h, n)), out_shape=jax.ShapeDtypeStruct((h, n), F32),
                  compiler_params=pltpu.CompilerParams(vmem_limit_bytes=VMEM_LIMIT))(rel_t, onehot_t)


def _bias_reduce(dbias, onehot, dsink_rows):
    h, n = dbias.shape

    def body(d_ref, o_ref, s_ref, out_ref, so_ref):
        a, b, c = _split3(d_ref[...])
        dot = functools.partial(lax.dot_general, dimension_numbers=_DIMS["nn"], preferred_element_type=F32)
        oh = o_ref[...]
        out_ref[...] = dot(a, oh) + (dot(b, oh) + dot(c, oh))
        so_ref[...] = jnp.sum(s_ref[...], axis=-1, keepdims=True)

    full = lambda shp: pl.BlockSpec(shp, lambda: (0,) * len(shp))
    return _pcall(body, name="bias_reduce", in_specs=[full(dbias.shape), full(onehot.shape), full(dsink_rows.shape)],
                  out_specs=[full((h, REL_BUCKETS)), full((h, 1))],
                  out_shape=[jax.ShapeDtypeStruct((h, REL_BUCKETS), F32), jax.ShapeDtypeStruct((h, 1), F32)],
                  compiler_params=pltpu.CompilerParams(vmem_limit_bytes=VMEM_LIMIT))(dbias, onehot, dsink_rows)


def _swa_specs(s):
    rows = SWA_GROUP * BLOCK
    q_spec = pl.BlockSpec((None, SWA_GROUP, BLOCK, SWA_DH), lambda g, n: (g, 0, n, 0))
    kv_prev = pl.BlockSpec((None, BLOCK, SWA_DH), lambda g, n: (g, jnp.maximum(n - 1, 0), 0))
    kv_cur = pl.BlockSpec((None, BLOCK, SWA_DH), lambda g, n: (g, n, 0))
    bias_spec = pl.BlockSpec((None, rows, 2 * BLOCK), lambda g, n: (g, 0, 0))
    col_spec = pl.BlockSpec((None, rows, 1), lambda g, n: (g, 0, 0))
    lse_spec = pl.BlockSpec((None, None, rows, 1), lambda g, n: (g, n, 0, 0))
    return rows, q_spec, kv_prev, kv_cur, bias_spec, col_spec, lse_spec


def _swa_scores(q_ref, kp_ref, kc_ref, bias_ref, n):
    rows = SWA_GROUP * BLOCK
    q = q_ref[...].reshape(rows, SWA_DH)
    kb = jnp.concatenate([kp_ref[...], kc_ref[...]], axis=0)
    s = lax.dot_general(q, kb, _DIMS["nt"], preferred_element_type=F32) * SWA_SCALE + bias_ref[...]
    col = lax.broadcasted_iota(jnp.int32, s.shape, 1)
    s = jnp.where(jnp.logical_and(n == 0, col < BLOCK), -jnp.inf, s)
    return q, kb, s


def _swa_fwd(q, k, v, bias, sink_rows, comm=None):
    s = q.shape[2]
    nb = s // BLOCK
    rows, q_spec, kv_prev, kv_cur, bias_spec, col_spec, lse_spec = _swa_specs(s)

    def body(q_ref, kp_ref, kc_ref, vp_ref, vc_ref, bias_ref, sink_ref, o_ref, lse_ref):
        n = pl.program_id(1)
        _, _, sc = _swa_scores(q_ref, kp_ref, kc_ref, bias_ref, n)
        sink = sink_ref[...]
        m = jnp.maximum(jnp.max(sc, axis=-1, keepdims=True), sink)
        p = jnp.exp(sc - m)
        den = jnp.sum(p, axis=-1, keepdims=True) + jnp.exp(sink - m)
        p = p / den
        vb = jnp.concatenate([vp_ref[...], vc_ref[...]], axis=0)
        o = lax.dot_general(p.astype(BF16), vb, _DIMS["nn"], preferred_element_type=F32)
        o_ref[...] = o.reshape(SWA_GROUP, BLOCK, SWA_DH).astype(BF16)
        lse_ref[...] = m + jnp.log(den)

    return _pcall(
        body, comm=comm, name="swa_fwd", grid=(SWA_KV, nb),
        in_specs=[q_spec, kv_prev, kv_cur, kv_prev, kv_cur, bias_spec, col_spec],
        out_specs=[q_spec, lse_spec],
        out_shape=[jax.ShapeDtypeStruct(q.shape, BF16), jax.ShapeDtypeStruct((SWA_KV, nb, rows, 1), F32)],
        compiler_params=_cparams(("parallel", "parallel")))(q, k, k, v, v, bias, sink_rows)


def _swa_bwd(q, k, v, do, lse, bias, sink_rows, comm=None):
    s = q.shape[2]
    nb = s // BLOCK
    rows, q_spec, kv_prev, kv_cur, bias_spec, col_spec, lse_spec = _swa_specs(s)

    def body(q_ref, kp_ref, kc_ref, vp_ref, vc_ref, do_ref, lse_ref, bias_ref, sink_ref,
             dq_ref, dkp_ref, dkc_ref, dvp_ref, dvc_ref, dbias_ref, dsink_ref):
        n = pl.program_id(1)

        @pl.when(n == 0)
        def _():
            dbias_ref[...] = jnp.zeros_like(dbias_ref)
            dsink_ref[...] = jnp.zeros_like(dsink_ref)

        qv, kb, sc = _swa_scores(q_ref, kp_ref, kc_ref, bias_ref, n)
        lse_v = lse_ref[...]
        p = jnp.exp(sc - lse_v)
        p_sink = jnp.exp(sink_ref[...] - lse_v)
        dov = do_ref[...].reshape(rows, SWA_DH)
        vb = jnp.concatenate([vp_ref[...], vc_ref[...]], axis=0)
        dp = lax.dot_general(dov, vb, _DIMS["nt"], preferred_element_type=F32)
        delta = jnp.sum(p * dp, axis=-1, keepdims=True)
        ds = p * (dp - delta)
        dbias_ref[...] += ds
        dsink_ref[...] += -p_sink * delta
        dsb = (ds * SWA_SCALE).astype(BF16)
        dq = lax.dot_general(dsb, kb, _DIMS["nn"], preferred_element_type=F32)
        dq_ref[...] = dq.reshape(SWA_GROUP, BLOCK, SWA_DH).astype(BF16)
        dk = lax.dot_general(dsb, qv, _DIMS["tn"], preferred_element_type=F32)
        dv = lax.dot_general(p.astype(BF16), dov, _DIMS["tn"], preferred_element_type=F32)
        dkp_ref[...] = dk[:BLOCK]
        dkc_ref[...] = dk[BLOCK:]
        dvp_ref[...] = dv[:BLOCK]
        dvc_ref[...] = dv[BLOCK:]

    kv_out = jax.ShapeDtypeStruct((SWA_KV, s, SWA_DH), F32)
    return _pcall(
        body, comm=comm, name="swa_bwd", grid=(SWA_KV, nb),
        in_specs=[q_spec, kv_prev, kv_cur, kv_prev, kv_cur, q_spec, lse_spec, bias_spec, col_spec],
        out_specs=[q_spec, kv_cur, kv_cur, kv_cur, kv_cur, bias_spec, col_spec],
        out_shape=[jax.ShapeDtypeStruct(q.shape, BF16), kv_out, kv_out, kv_out, kv_out,
                   jax.ShapeDtypeStruct(bias.shape, F32), jax.ShapeDtypeStruct(sink_rows.shape, F32)],
        compiler_params=_cparams(("arbitrary", "arbitrary")))(q, k, k, v, v, do, lse, bias, sink_rows)


def _rope_slab(slab, table):
    t = slab * table
    return t + pltpu.roll(t, ROPE, 1)


def _low_lanes(v):
    lane = lax.broadcasted_iota(jnp.int32, v.shape, 1)
    return jnp.where(lane < ROPE, v, 0.0)


def _rms(xv, g):
    r = lax.rsqrt(jnp.mean(xv * xv, axis=-1, keepdims=True) + EPS)
    return xv * r, r


def _mla_prep(proj, gq, gkv, table):
    s = proj.shape[0]
    ts = _pick(s, ROW_TILE, 16)

    def body(p_ref, gq_ref, gkv_ref, t_ref, cq_ref, ckv_ref, kr_ref):
        xq, _ = _rms(p_ref[:, 0:Q_RANK], None)
        cq_ref[...] = (xq * gq_ref[...]).astype(BF16)
        xkv, _ = _rms(p_ref[:, Q_RANK:Q_RANK + KV_RANK], None)
        ckv_ref[...] = (xkv * gkv_ref[...]).astype(BF16)
        kr_ref[...] = _low_lanes(_rope_slab(p_ref[:, Q_RANK + KV_RANK:TAIL], t_ref[...]))

    return _pcall(
        body, name="mla_prep", grid=(s // ts,),
        in_specs=[pl.BlockSpec((ts, TAIL), lambda i: (i, TAIL0 // TAIL)), _vec(Q_RANK), _vec(KV_RANK), _row(ts, 2 * ROPE)],
        out_specs=[_row(ts, Q_RANK), _row(ts, KV_RANK), _row(ts, 2 * ROPE)],
        out_shape=[jax.ShapeDtypeStruct((s, Q_RANK), BF16), jax.ShapeDtypeStruct((s, KV_RANK), BF16),
                   jax.ShapeDtypeStruct((s, 2 * ROPE), F32)],
        compiler_params=_cparams(("parallel",)))(proj, gq, gkv, table)


def _mla_prep_bwd(proj, dcq, dckv, dkr, gq, gkv, table):
    s = proj.shape[0]
    ts = _pick(s, ROW_TILE, 16)

    def norm_bwd(xv, dy, g):
        xn, r = _rms(xv, None)
        dg = jnp.sum(dy * xn, axis=0, keepdims=True)
        dxn = dy * g
        return r * (dxn - xn * jnp.mean(dxn * xn, axis=-1, keepdims=True)), dg

    def body(p_ref, dcq_ref, dckv_ref, dkr_ref, gq_ref, gkv_ref, t_ref, dt_ref, dgq_ref, dgkv_ref):
        @pl.when(pl.program_id(0) == 0)
        def _():
            dgq_ref[...] = jnp.zeros_like(dgq_ref)
            dgkv_ref[...] = jnp.zeros_like(dgkv_ref)

        dxq, dgq = norm_bwd(p_ref[:, 0:Q_RANK], dcq_ref[...], gq_ref[...])
        dxkv, dgkv = norm_bwd(p_ref[:, Q_RANK:Q_RANK + KV_RANK], dckv_ref[...], gkv_ref[...])
        dgq_ref[...] += dgq
        dgkv_ref[...] += dgkv
        d = _low_lanes(dkr_ref[...])
        dslab = (d + pltpu.roll(d, ROPE, 1)) * t_ref[...]
        dt_ref[:, 0:Q_RANK] = dxq.astype(BF16)
        dt_ref[:, Q_RANK:Q_RANK + KV_RANK] = dxkv.astype(BF16)
        dt_ref[:, Q_RANK + KV_RANK:TAIL] = dslab.astype(BF16)

    return _pcall(
        body, name="mla_prep_bwd", grid=(s // ts,),
        in_specs=[pl.BlockSpec((ts, TAIL), lambda i: (i, TAIL0 // TAIL)), _row(ts, Q_RANK), _row(ts, KV_RANK),
                  _row(ts, 2 * ROPE), _vec(Q_RANK), _vec(KV_RANK), _row(ts, 2 * ROPE)],
        out_specs=[_row(ts, TAIL), _vec(Q_RANK), _vec(KV_RANK)],
        out_shape=[jax.ShapeDtypeStruct((s, TAIL), BF16), jax.ShapeDtypeStruct((1, Q_RANK), F32),
                   jax.ShapeDtypeStruct((1, KV_RANK), F32)],
        compiler_params=_cparams(("arbitrary",)))(proj, dcq, dckv, dkr, gq, gkv, table)


def _head_specs(ts):
    tok = lambda w: pl.BlockSpec((ts, w), lambda h, i: (i, 0))
    head = lambda w: pl.BlockSpec((None, ts, w), lambda h, i: (h, i, 0))
    wgt = lambda r, c: pl.BlockSpec((None, r, c), lambda h, i: (h, 0, 0))
    return tok, head, wgt


def _mla_qkv(cq, ckv, kr, wq, wkv, table):
    s = cq.shape[0]
    ts = _pick(s, 2 * ROW_TILE, 16)
    tok, head, wgt = _head_specs(ts)

    def body(cq_ref, ckv_ref, kr_ref, wq_ref, wkv_ref, t_ref, q_ref, k_ref, v_ref):
        qf = lax.dot_general(cq_ref[...], wq_ref[...], _DIMS["nn"], preferred_element_type=F32)
        q_ref[:, 0:NOPE] = qf[:, 0:NOPE].astype(BF16)
        q_ref[:, NOPE:QW] = _rope_slab(qf[:, NOPE:QW], t_ref[...]).astype(BF16)
        kv = lax.dot_general(ckv_ref[...], wkv_ref[...], _DIMS["nn"], preferred_element_type=F32)
        k_ref[:, 0:NOPE] = kv[:, 0:NOPE].astype(BF16)
        k_ref[:, NOPE:QW] = kr_ref[...].astype(BF16)
        v_ref[:, 0:VDIM] = kv[:, NOPE:NOPE + VDIM].astype(BF16)
        lane = lax.broadcasted_iota(jnp.int32, (ts, VDIM), 1)
        v_ref[:, VDIM:2 * VDIM] = jnp.where(lane == 0, 1.0, 0.0).astype(BF16)

    return _pcall(
        body, name="mla_qkv", grid=(MLA_H, s // ts),
        in_specs=[tok(Q_RANK), tok(KV_RANK), tok(2 * ROPE), wgt(Q_RANK, QW), wgt(KV_RANK, NOPE + VDIM), tok(2 * ROPE)],
        out_specs=[head(QW), head(QW), head(2 * VDIM)],
        out_shape=[jax.ShapeDtypeStruct((MLA_H, s, QW), BF16), jax.ShapeDtypeStruct((MLA_H, s, QW), BF16),
                   jax.ShapeDtypeStruct((MLA_H, s, 2 * VDIM), BF16)],
        compiler_params=_cparams(("parallel", "parallel")))(cq, ckv, kr, wq, wkv, table)


def _mla_qkv_bwd(dq, dk, dv, cq, ckv, wq, wkv, table, comm=None):
    s = cq.shape[0]
    ts = _pick(s, 2 * ROW_TILE, 16)
    tok, head, wgt = _head_specs(ts)
    whole = lambda w: pl.BlockSpec((s, w), lambda h, i: (0, 0))

    def body(dq_ref, dk_ref, dv_ref, cq_ref, ckv_ref, wq_ref, wkv_ref, t_ref,
             dcq_ref, dckv_ref, dkr_ref, gwq_ref, gwkv_ref):
        h, i = pl.program_id(0), pl.program_id(1)
        rows = pl.ds(pl.multiple_of(i * ts, ts), ts)
        d = dq_ref[:, NOPE:QW]
        dslab = (d + pltpu.roll(d, ROPE, 1)) * t_ref[...]
        dqe = jnp.concatenate([dq_ref[:, 0:NOPE], dslab], axis=1).astype(BF16)
        dkv = jnp.concatenate([dk_ref[:, 0:NOPE], dv_ref[...]], axis=1).astype(BF16)
        dcq = lax.dot_general(dqe, wq_ref[...], _DIMS["nt"], preferred_element_type=F32)
        dckv = lax.dot_general(dkv, wkv_ref[...], _DIMS["nt"], preferred_element_type=F32)
        gwq = lax.dot_general(cq_ref[...], dqe, _DIMS["tn"], preferred_element_type=F32)
        gwkv = lax.dot_general(ckv_ref[...], dkv, _DIMS["tn"], preferred_element_type=F32)
        dkr = dk_ref[:, NOPE:QW].astype(F32)

        @pl.when(h == 0)
        def _():
            dcq_ref[rows, :] = dcq
            dckv_ref[rows, :] = dckv
            dkr_ref[rows, :] = dkr

        @pl.when(h > 0)
        def _():
            dcq_ref[rows, :] += dcq
            dckv_ref[rows, :] += dckv
            dkr_ref[rows, :] += dkr

        @pl.when(i == 0)
        def _():
            gwq_ref[...] = gwq
            gwkv_ref[...] = gwkv

        @pl.when(i > 0)
        def _():
            gwq_ref[...] += gwq
            gwkv_ref[...] += gwkv

    return _pcall(
        body, comm=comm, name="mla_qkv_bwd", grid=(MLA_H, s // ts),
        in_specs=[head(QW), head(QW), head(VDIM), tok(Q_RANK), tok(KV_RANK), wgt(Q_RANK, QW),
                  wgt(KV_RANK, NOPE + VDIM), tok(2 * ROPE)],
        out_specs=[whole(Q_RANK), whole(KV_RANK), whole(2 * ROPE), wgt(Q_RANK, QW), wgt(KV_RANK, NOPE + VDIM)],
        out_shape=[jax.ShapeDtypeStruct((s, Q_RANK), F32), jax.ShapeDtypeStruct((s, KV_RANK), F32),
                   jax.ShapeDtypeStruct((s, 2 * ROPE), F32), jax.ShapeDtypeStruct((MLA_H, Q_RANK, QW), F32),
                   jax.ShapeDtypeStruct((MLA_H, KV_RANK, NOPE + VDIM), F32)],
        compiler_params=_cparams(("arbitrary", "arbitrary")))(dq, dk, dv, cq, ckv, wq, wkv, table)


def _diag_mask(t):
    return lax.broadcasted_iota(jnp.int32, (t, t), 1) <= lax.broadcasted_iota(jnp.int32, (t, t), 0)


def _mla_fwd(q, k, v, comm=None):
    s = q.shape[1]
    t = _pick(s, ATT_T, 128)
    nt = s // t
    assert nt % 2 == 0
    hb = 2 * MLA_HB

    def fold(p, u):
        first = u <= p
        return jnp.where(first, p, nt - 1 - p), jnp.where(first, u, u - p - 1)

    to_log2 = MLA_SCALE * math.log2(math.e)

    def body(q_ref, k_ref, v_ref, o_ref, oh_ref, lse_ref, m_ref, acc_ref):
        i, j = fold(pl.program_id(1), pl.program_id(2))

        @pl.when(j == 0)
        def _():
            m_ref[...] = jnp.full_like(m_ref, -jnp.inf)
            acc_ref[...] = jnp.zeros_like(acc_ref)

        def step(diagonal):
            for h in range(hb):
                sc = lax.dot_general(q_ref[h], k_ref[h], _DIMS["nt"], preferred_element_type=F32)
                if diagonal:
                    sc = jnp.where(_diag_mask(t), sc, -jnp.inf)
                m_old = m_ref[h]
                m_new = jnp.maximum(m_old, jnp.max(sc, axis=-1, keepdims=True))
                alpha = jnp.exp2((m_old - m_new) * to_log2)
                p = jnp.exp2((sc - m_new) * to_log2)
                acc_ref[h] = alpha * acc_ref[h] + lax.dot_general(p.astype(BF16), v_ref[h], _DIMS["nn"],
                                                                  preferred_element_type=F32)
                m_ref[h] = m_new

        @pl.when(j < i)
        def _():
            step(False)

        @pl.when(j == i)
        def _():
            step(True)
            for h in range(hb):
                den = acc_ref[h, :, VDIM:VDIM + 1]
                o = acc_ref[h, :, 0:VDIM] / den
                o_ref[:, h * VDIM:(h + 1) * VDIM] = o
                oh_ref[:, h * VDIM:(h + 1) * VDIM] = o.astype(BF16)
                lse_ref[h] = m_ref[h] * MLA_SCALE + jnp.log(den)

    o_spec = pl.BlockSpec((t, hb * VDIM), lambda h, p, u: (fold(p, u)[0], h))
    return _pcall(
        body, comm=comm, name="mla_fwd", grid=(MLA_H // hb, nt // 2, nt + 1),
        in_specs=[pl.BlockSpec((hb, t, QW), lambda h, p, u: (h, fold(p, u)[0], 0)),
                  pl.BlockSpec((hb, t, QW), lambda h, p, u: (h, fold(p, u)[1], 0)),
                  pl.BlockSpec((hb, t, 2 * VDIM), lambda h, p, u: (h, fold(p, u)[1], 0))],
        out_specs=[o_spec, o_spec, pl.BlockSpec((hb, t, 1), lambda h, p, u: (h, fold(p, u)[0], 0))],
        out_shape=[jax.ShapeDtypeStruct((s, MLA_H * VDIM), F32), jax.ShapeDtypeStruct((s, MLA_H * VDIM), BF16),
                   jax.ShapeDtypeStruct((MLA_H, s, 1), F32)],
        scratch_shapes=[pltpu.VMEM((hb, t, 1), F32), pltpu.VMEM((hb, t, 2 * VDIM), F32)],
        compiler_params=_cparams(("parallel", "parallel", "arbitrary")))(q, k, v)


def _mla_delta(dmix, o):
    s = o.shape[0]
    ts = _pick(s, ROW_TILE, 16)
    w = MLA_H * VDIM

    def body(d_ref, o_ref, out_ref):
        prod = d_ref[...] * o_ref[...]
        for h in range(MLA_H):
            out_ref[h] = jnp.sum(prod[:, h * VDIM:(h + 1) * VDIM], axis=-1, keepdims=True)

    return _pcall(body, name="mla_delta", grid=(s // ts,),
                  in_specs=[pl.BlockSpec((ts, w), lambda i: (i, SWA_HEADS * SWA_DH // w)), pl.BlockSpec((ts, w), lambda i: (i, 0))],
                  out_specs=pl.BlockSpec((MLA_H, ts, 1), lambda i: (0, i, 0)),
                  out_shape=jax.ShapeDtypeStruct((MLA_H, s, 1), F32), compiler_params=_cparams(("parallel",)))(dmix, o)


def _mla_bwd(q, k, v, dmix, delta, lse, comm=None):
    s = q.shape[1]
    t = _pick(s, ATT_T, 128)
    nt = s // t
    assert nt % 2 == 0
    hb = MLA_HB
    o_blk0 = SWA_HEADS * SWA_DH // (hb * VDIM)

    def fold(p, u):
        first = u < nt - p
        return jnp.where(first, p, nt - 1 - p), jnp.where(first, p + u, u - 1)

    log2e = math.log2(math.e)

    def body(q_ref, k_ref, v_ref, do_ref, delta_ref, lse_ref, dq_ref, dk_ref, dv_ref, dk_acc, dv_acc):
        j, i = fold(pl.program_id(1), pl.program_id(2))
        rows = pl.ds(pl.multiple_of(i * t, t), t)

        @pl.when(i == j)
        def _():
            dk_acc[...] = jnp.zeros_like(dk_acc)
            dv_acc[...] = jnp.zeros_like(dv_acc)

        def step(diagonal):
            for h in range(hb):
                qv, kv_ = q_ref[h], k_ref[h]
                dob = do_ref[:, h * VDIM:(h + 1) * VDIM].astype(BF16)
                st = lax.dot_general(kv_, qv, _DIMS["nt"], preferred_element_type=F32)
                pt = jnp.exp2(st * (MLA_SCALE * log2e) - lse_ref[h] * log2e)
                if diagonal:
                    keep = lax.broadcasted_iota(jnp.int32, (t, t), 0) <= lax.broadcasted_iota(jnp.int32, (t, t), 1)
                    pt = jnp.where(keep, pt, 0.0)
                dpt = lax.dot_general(v_ref[h], dob, _DIMS["nt"], preferred_element_type=F32)
                dst = (pt * (dpt - delta_ref[h]) * MLA_SCALE).astype(BF16)
                dv_acc[h] += lax.dot_general(pt.astype(BF16), dob, _DIMS["nn"], preferred_element_type=F32)
                dk_acc[h] += lax.dot_general(dst, qv, _DIMS["nn"], preferred_element_type=F32)
                dqv = lax.dot_general(dst, kv_, _DIMS["tn"], preferred_element_type=F32)

                @pl.when(j == 0)
                def _():
                    dq_ref[h, rows, :] = dqv

                @pl.when(j > 0)
                def _():
                    dq_ref[h, rows, :] += dqv

        @pl.when(i > j)
        def _():
            step(False)

        @pl.when(i == j)
        def _():
            step(True)

        @pl.when(i == nt - 1)
        def _():
            dk_ref[...] = dk_acc[...].astype(BF16)
            dv_ref[...] = dv_acc[...].astype(BF16)

    qi = lambda h, p, u: (h, fold(p, u)[1], 0)
    kj = lambda h, p, u: (h, fold(p, u)[0], 0)
    row = pl.BlockSpec((hb, 1, t), lambda h, p, u: (h, 0, fold(p, u)[1]))
    return _pcall(
        body, comm=comm, name="mla_bwd", grid=(MLA_H // hb, nt // 2, nt + 1),
        in_specs=[pl.BlockSpec((hb, t, QW), qi), pl.BlockSpec((hb, t, QW), kj), pl.BlockSpec((hb, t, VDIM), kj),
                  pl.BlockSpec((t, hb * VDIM), lambda h, p, u: (fold(p, u)[1], o_blk0 + h)), row, row],
        out_specs=[pl.BlockSpec((hb, s, QW), lambda h, p, u: (h, 0, 0)), pl.BlockSpec((hb, t, QW), kj),
                   pl.BlockSpec((hb, t, VDIM), kj)],
        out_shape=[jax.ShapeDtypeStruct((MLA_H, s, QW), F32), jax.ShapeDtypeStruct((MLA_H, s, QW), BF16),
                   jax.ShapeDtypeStruct((MLA_H, s, VDIM), BF16)],
        scratch_shapes=[pltpu.VMEM((hb, t, QW), F32), pltpu.VMEM((hb, t, VDIM), F32)],
        compiler_params=_cparams(("arbitrary", "arbitrary", "arbitrary")))(q, k, v, dmix, delta, lse)


def _adamw(name, w, g, m, v, parts):
    r, c = w.shape
    n_parts = g.shape[0] if parts else 1
    tr = r if r * c <= ADAM_ELEMS else _pick(r, max(8, ADAM_ELEMS // c // 8 * 8), 8)
    c1 = 1.0 - ADAM_B1 ** ADAM_STEP
    c2 = 1.0 - ADAM_B2 ** ADAM_STEP

    def body(w_ref, g_ref, m_ref, v_ref, go_ref, d_ref, mo_ref, vo_ref):
        if parts:
            gv = g_ref[0].astype(F32)
            for j in range(1, n_parts):
                gv = gv + g_ref[j].astype(F32)
        else:
            gv = g_ref[...]
        mv = ADAM_B1 * m_ref[...] + (1.0 - ADAM_B1) * gv
        vv = ADAM_B2 * v_ref[...] + (1.0 - ADAM_B2) * (gv * gv)
        go_ref[...] = gv
        mo_ref[...] = mv
        vo_ref[...] = vv
        d_ref[...] = -ADAM_LR * ((mv / c1) / (jnp.sqrt(vv / c2) + ADAM_EPS) + ADAM_WD * w_ref[...])

    blk = pl.BlockSpec((tr, c), lambda i: (i, 0))
    g_spec = pl.BlockSpec((n_parts, tr, c), lambda i: (0, i, 0)) if parts else blk
    out = jax.ShapeDtypeStruct((r, c), F32)
    return _pcall(body, name=name, grid=(r // tr,), in_specs=[blk, g_spec, blk, blk], out_specs=[blk] * 4,
                  out_shape=[out] * 4, compiler_params=_cparams(("parallel",)))(w, g, m, v)


def _t5_bucket(dist):
    n = jnp.maximum(dist, 0)
    max_exact = REL_BUCKETS // 2
    nf = jnp.maximum(n, 1).astype(F32)
    large = max_exact + (jnp.log(nf / max_exact) / math.log(REL_MAX_DIST / max_exact)
                         * (REL_BUCKETS - max_exact)).astype(jnp.int32)
    return jnp.where(n < max_exact, n, jnp.minimum(large, REL_BUCKETS - 1))


def _swap_halves(w, r0):
    return jnp.concatenate([w[:, r0 + ROPE // 2:r0 + ROPE], w[:, r0:r0 + ROPE // 2]], axis=1)


def _fold_swapped(g, r0, width):
    sw = g[..., width:width + ROPE]
    half = ROPE // 2
    return jnp.concatenate([g[..., :r0], g[..., r0:r0 + half] + sw[..., half:], g[..., r0 + half:r0 + ROPE] + sw[..., :half],
                            g[..., r0 + ROPE:width]], axis=-1)


def _heads_major(a, heads):
    s = a.shape[0]
    return a.reshape(s, heads, SWA_DH).transpose(1, 0, 2)


def _tokens_major(a):
    h, s, d = a.shape
    return a.transpose(1, 0, 2).reshape(s, h * d)


def kernel(x, c, w_mod, b_mod, attn_norm_g, w_in, swa_sinks, rel_bias, mla_q_norm_g, w_uq, mla_kv_norm_g, w_ukv, w_out, mlp_norm_g, w_ff1, w_ff2, final_norm_g, loss_target, m_w_mod, m_b_mod, m_attn_norm_g, m_w_in, m_swa_sinks, m_rel_bias, m_mla_q_norm_g, m_w_uq, m_mla_kv_norm_g, m_w_ukv, m_w_out, m_mlp_norm_g, m_w_ff1, m_w_ff2, m_final_norm_g, v_w_mod, v_b_mod, v_attn_norm_g, v_w_in, v_swa_sinks, v_rel_bias, v_mla_q_norm_g, v_w_uq, v_mla_kv_norm_g, v_w_ukv, v_w_out, v_mlp_norm_g, v_w_ff1, v_w_ff2, v_final_norm_g):
    s, d = x.shape[1], x.shape[2]
    ffs = w_ff1.shape[2]
    ff = ffs * NDEV
    nmod = w_mod.shape[2]
    me = 4 * lax.axis_index("x") + 2 * lax.axis_index("y") + lax.axis_index("c")
    x2d, tgt = x[0], loss_target[0]
    final_g = final_norm_g.reshape(1, d)

    w_in_l = jnp.concatenate([w_in[0], _swap_halves(w_in[0], OFF_KR)], axis=1).astype(BF16)
    w_uq_l = jnp.concatenate([w_uq[0], _swap_halves(w_uq[0], NOPE)], axis=1).astype(BF16)
    core = jnp.full((1, 128), lax.axis_index("c"), F32)
    (c_all,) = _exchange("gather_c", _Gather([c]))

    b_cols = lax.dynamic_slice(b_mod, (0, me * nmod), (1, nmod))
    act_all, mod_cols, w_in_g, w_uq_g, w_ukv_g = _mod_fwd(
        c_all.reshape(NDEV, d), w_mod[0], b_cols, comm=[_Gather([w_in_l, w_uq_l, w_ukv[0].astype(BF16)])])
    w_in_e = w_in_g.reshape(d, IN_EXT)
    (mod_g,) = _exchange("gather_mod", _Gather([mod_cols]))
    mod = lax.dynamic_index_in_dim(mod_g, me, axis=1, keepdims=False).reshape(1, 6 * d)
    sh1, sc1, g1, sh2, sc2, g2 = [mod[:, i * d:(i + 1) * d] for i in range(6)]

    pos = jnp.arange(s, dtype=F32)
    inv_freq = ROPE_THETA ** (-jnp.arange(ROPE // 2, dtype=F32) / (ROPE // 2))
    ang = pos[:, None] * inv_freq[None, :]
    cos, sin = jnp.cos(ang), jnp.sin(ang)
    table = jnp.concatenate([cos, cos, -sin, sin], axis=1)
    q_loc = jnp.arange(BLOCK)[:, None]
    k_loc = jnp.arange(2 * BLOCK)[None, :]
    dist = q_loc + BLOCK - k_loc
    in_window = (dist >= 0) & (dist < BLOCK)
    onehot = (_t5_bucket(dist).reshape(-1, 1) == jnp.arange(REL_BUCKETS)[None, :]).astype(BF16)
    bias = _bias_expand(rel_bias.T, onehot.T).reshape(SWA_HEADS, BLOCK, 2 * BLOCK)
    bias = jnp.where(in_window[None], bias, -jnp.inf).reshape(SWA_KV, SWA_GROUP * BLOCK, 2 * BLOCK)
    sink_rows = jnp.broadcast_to(swa_sinks.reshape(SWA_HEADS, 1), (SWA_HEADS, BLOCK)).reshape(SWA_KV, SWA_GROUP * BLOCK, 1)

    h1 = _norm_mod("norm1", x2d, attn_norm_g, sc1, sh1)
    def both_dtypes(acc, ex, outs):
        outs[0][...] = acc
        outs[1][...] = acc.astype(BF16)

    tmp = _pick(s, MM_TM // 2, 128)
    proj_blk = pl.BlockSpec((tmp, IN_EXT), lambda i, j, q: (i, 0))
    proj, proj_h = _mm("proj", h1, w_in_e, "nn", (s // tmp, 1, 1), pl.BlockSpec((tmp, d), lambda i, j, q: (i, 0)),
                       pl.BlockSpec((d, IN_EXT), lambda i, j, q: (0, 0)),
                       [jax.ShapeDtypeStruct((s, IN_EXT), F32), jax.ShapeDtypeStruct((s, IN_EXT), BF16)],
                       [proj_blk, proj_blk], (tmp, IN_EXT), both_dtypes)
    q_a = _heads_major(proj_h[:, :OFF_K], SWA_HEADS).reshape(SWA_KV, SWA_GROUP, s, SWA_DH)
    k_a = _heads_major(proj_h[:, OFF_K:OFF_V], SWA_KV)
    v_a = _heads_major(proj_h[:, OFF_V:OFF_CQ], SWA_KV)
    o_a, lse_a, w_out_g = _swa_fwd(q_a, k_a, v_a, bias, sink_rows, comm=[_Gather([w_out[0].astype(BF16)])])
    w_out_f = w_out_g.reshape(MIX, d)

    cq, ckv, kr = _mla_prep(proj, mla_q_norm_g, mla_kv_norm_g, table)
    q_b, k_b, v_b = _mla_qkv(cq, ckv, kr, w_uq_g, w_ukv_g, table)
    o_b, o_bh, lse_b, w_ff1_g = _mla_fwd(q_b, k_b, v_b, comm=[_Gather([w_ff1[0].astype(BF16)])])
    mix = jnp.concatenate([_tokens_major(o_a.reshape(SWA_HEADS, s, SWA_DH)), o_bh], axis=1)

    tm, tn, tk = _pick(s, MM_TM, 128), _pick(d, MM_TN, 128), _pick(MIX, MM_TK, 128)
    row_blk = pl.BlockSpec((tm, tn), lambda i, j, q: (i, j))
    gate_blk = pl.BlockSpec((1, tn), lambda i, j, q: (0, j))

    def gated_residual(acc, ex, outs):
        outs[0][...] = acc
        outs[1][...] = ex[0][...] + ex[1][...] * acc

    y1, x2 = _mm("out_proj", mix, w_out_f, "nn", (s // tm, d // tn, MIX // tk),
                 pl.BlockSpec((tm, tk), lambda i, j, q: (i, q)), pl.BlockSpec((tk, tn), lambda i, j, q: (q, j)),
                 [jax.ShapeDtypeStruct((s, d), F32)] * 2, [row_blk, row_blk], (tm, tn), gated_residual,
                 extras=(x2d, g1), extra_specs=(row_blk, gate_blk))

    h2 = _norm_mod("norm2", x2, mlp_norm_g, sc2, sh2)
    tnf, tkd = _pick(ffs, MM_TN, 128), _pick(d, MM_TK, 128)
    rf = ffs // tnf
    ff_blk = pl.BlockSpec((tm, tnf), lambda i, j, q: (i, j))

    def relu_sq(acc, ex, outs):
        u = jnp.maximum(acc, 0.0)
        outs[0][...] = u
        outs[1][...] = (u * u).astype(BF16)

    u, uu, w_ff2_g = _mm("ff1", h2, w_ff1_g, "nn", (s // tm, ff // tnf, d // tkd),
                         pl.BlockSpec((tm, tkd), lambda i, j, q: (i, q)),
                         pl.BlockSpec((None, tkd, tnf), lambda i, j, q: (j // rf, q, j % rf)),
                         [jax.ShapeDtypeStruct((s, ff), F32), jax.ShapeDtypeStruct((s, ff), BF16)], [ff_blk, ff_blk],
                         (tm, tnf), relu_sq, comm=[_Gather([w_ff2[0].astype(BF16)])])
    w_ff2_f = w_ff2_g.reshape(ff, d)
    tkf = _pick(ff, MM_TK, 128)
    y2, x3 = _mm("ff2", uu, w_ff2_f, "nn", (s // tm, d // tn, ff // tkf),
                 pl.BlockSpec((tm, tkf), lambda i, j, q: (i, q)), pl.BlockSpec((tkf, tn), lambda i, j, q: (q, j)),
                 [jax.ShapeDtypeStruct((s, d), F32)] * 2, [row_blk, row_blk], (tm, tn), gated_residual,
                 extras=(x2, g2), extra_specs=(row_blk, gate_blk))

    dx3, dy2, loss_p, dgf, dg2 = _loss_head(x3, tgt, y2, final_g, g2)
    loss = lax.psum(loss_p[0, 0], ("x", "y", "c"))

    def relu_sq_bwd(acc, ex, outs):
        outs[0][...] = (acc * (2.0 * ex[0][...])).astype(BF16)

    tnf2 = _pick(ff, MM_TN, 128)
    du = _mm("ff2_dx", dy2, w_ff2_f, "nt", (s // tm, ff // tnf2, d // tkd),
             pl.BlockSpec((tm, tkd), lambda i, j, q: (i, q)), pl.BlockSpec((tnf2, tkd), lambda i, j, q: (j, q)),
             [jax.ShapeDtypeStruct((s, ff), BF16)], [pl.BlockSpec((tm, tnf2), lambda i, j, q: (i, j))],
             (tm, tnf2), relu_sq_bwd, extras=(u,), extra_specs=(pl.BlockSpec((tm, tnf2), lambda i, j, q: (i, j)),))[0]
    gw_ff2 = _mm_plain("ff2_dw", uu, dy2, "tn", ff, d, s, BF16)
    tmd, tks = _pick(d, MM_TM, 128), _pick(s, MM_TK, 128)
    gw_ff2 = gw_ff2.reshape(NDEV, ffs, d)
    dh2, s_ff2 = _mm("ff1_dx", du, w_ff1_g, "nt", (s // tm, d // tn, NDEV // 2),
                     pl.BlockSpec((tm, 2 * ffs), lambda i, j, q: (i, q)),
                     pl.BlockSpec((2, tn, ffs), lambda i, j, q: (q, j, 0)),
                     [jax.ShapeDtypeStruct((s, d), F32)], [row_blk], (tm, tn), _store(F32),
                     comm=[_PairSwap([gw_ff2])], b_parts=2)
    c_ff2 = _pair_sum("pair_ff2", gw_ff2, s_ff2, core)
    gw_ff1, p_ff2 = _mm("ff1_dw", h2, du, "tn", (d // tmd, ff // tnf, s // tks),
                        pl.BlockSpec((tks, tmd), lambda i, j, q: (q, i)), pl.BlockSpec((tks, tnf), lambda i, j, q: (q, j)),
                        [jax.ShapeDtypeStruct((NDEV, d, ffs), BF16)],
                        [pl.BlockSpec((None, tmd, tnf), lambda i, j, q: (j // rf, i, j % rf))], (tmd, tnf), _store(BF16),
                        comm=[_ChipScatter([c_ff2])])
    dx2, dy1, dsc2, dsh2, dgm, dg1, s_ff1 = _norm_mod_bwd("norm2_bwd", x2, dh2, dx3, mlp_norm_g, sc2, y1, g1,
                                                          comm=[_PairSwap([gw_ff1])])
    c_ff1 = _pair_sum("pair_ff1", gw_ff1, s_ff1, core)

    dmix = _mm_plain("out_proj_dx", dy1, w_out_f, "nt", s, MIX, d, F32)
    gw_out = _mm_plain("out_proj_dw", mix, dy1, "tn", MIX, d, s, BF16).reshape(NDEV, MIX // NDEV, d)

    delta_b = _mla_delta(dmix, o_b).reshape(MLA_H, 1, s)
    dq_b, dk_b, dv_b, p_ff1, s_out = _mla_bwd(q_b, k_b, v_b, dmix, delta_b, lse_b.reshape(MLA_H, 1, s),
                                              comm=[_ChipScatter([c_ff1]), _PairSwap([gw_out])])
    c_out = _pair_sum("pair_out", gw_out, s_out, core)
    dcq, dckv, dkr, gw_uq_e, gw_ukv, p_out = _mla_qkv_bwd(dq_b, dk_b, dv_b, cq, ckv, w_uq_g, w_ukv_g, table,
                                                          comm=[_ChipScatter([c_out])])
    dtail, dgq, dgkv = _mla_prep_bwd(proj, dcq, dckv, dkr, mla_q_norm_g, mla_kv_norm_g, table)
    gw_uq = _fold_swapped(gw_uq_e, NOPE, NOPE + ROPE).astype(BF16)
    gw_ukv = gw_ukv.astype(BF16)

    do_a = _heads_major(dmix[:, :OFF_K].astype(BF16), SWA_HEADS).reshape(SWA_KV, SWA_GROUP, s, SWA_DH)
    dq_a, dkp, dkc, dvp, dvc, dbias, dsink, s_uq, s_ukv = _swa_bwd(
        q_a, k_a, v_a, do_a, lse_a, bias, sink_rows, comm=[_PairSwap([gw_uq, gw_ukv])])
    c_uq = _pair_sum("pair_uq", gw_uq, s_uq, core)
    c_ukv = _pair_sum("pair_ukv", gw_ukv, s_ukv, core)
    shift = lambda p: jnp.concatenate([p[:, BLOCK:], jnp.zeros_like(p[:, :BLOCK])], axis=1)
    dk_a, dv_a = dkc + shift(dkp), dvc + shift(dvp)
    drel_t, dsinks = _bias_reduce(dbias.reshape(SWA_HEADS, BLOCK * 2 * BLOCK), onehot, dsink.reshape(SWA_HEADS, BLOCK))
    dproj = jnp.concatenate([_tokens_major(dq_a.reshape(SWA_HEADS, s, SWA_DH)),
                             _tokens_major(dk_a).astype(BF16), _tokens_major(dv_a).astype(BF16), dtail], axis=1)
    gw_in_e = _mm_plain("proj_dw", h1, dproj, "tn", d, IN_EXT, s, F32, tn=TAIL)
    gw_in = _fold_swapped(gw_in_e, OFF_KR, IN_COLS).reshape(NDEV, d // NDEV, IN_COLS).astype(BF16)
    tkt = IN_EXT
    dh1, s_in, p_uq, p_ukv = _mm(
        "proj_dx", dproj, w_in_e, "nt", (s // tm, d // tn, IN_EXT // tkt),
        pl.BlockSpec((tm, tkt), lambda i, j, q: (i, q)), pl.BlockSpec((tn, tkt), lambda i, j, q: (j, q)),
        [jax.ShapeDtypeStruct((s, d), F32)], [row_blk], (tm, tn), _store(F32),
        comm=[_PairSwap([gw_in]), _ChipScatter([c_uq, c_ukv])])
    c_in = _pair_sum("pair_in", gw_in, s_in, core)
    gx, dsc1, dsh1, dga, p_in = _norm_mod_bwd("norm1_bwd", x2d, dh1, dx2, attn_norm_g, sc1,
                                              comm=[_ChipScatter([c_in])])

    small = [jnp.concatenate([dsh1, dsc1, dg1, dsh2, dsc2, dg2], axis=1), dga, dgm, dgf, dgq, dgkv,
             dsinks.reshape(1, SWA_HEADS), drel_t.T.reshape(1, REL_BUCKETS * SWA_HEADS)]
    n_small = sum(a.shape[1] for a in small)
    n_pad = -n_small % 1024
    rows_small = (n_small + n_pad) // 128
    pad = jnp.zeros((1, n_pad), F32)
    pack = lambda parts: jnp.concatenate([p.reshape(1, -1) for p in parts] + [pad], axis=1).reshape(rows_small, 128)
    (small_g,) = _exchange("gather_small", _Gather([pack(small)]))
    small_names = (b_mod, attn_norm_g, mlp_norm_g, final_norm_g, mla_q_norm_g, mla_kv_norm_g, swa_sinks, rel_bias)
    small_m = (m_b_mod, m_attn_norm_g, m_mlp_norm_g, m_final_norm_g, m_mla_q_norm_g, m_mla_kv_norm_g, m_swa_sinks, m_rel_bias)
    small_v = (v_b_mod, v_attn_norm_g, v_mlp_norm_g, v_final_norm_g, v_mla_q_norm_g, v_mla_kv_norm_g, v_swa_sinks, v_rel_bias)
    small_out = _adamw("adamw_small", pack(small_names), small_g, pack(small_m), pack(small_v), parts=True)

    def unpack(flat):
        flat = flat.reshape(1, -1)
        out, off = [], 0
        for a in small_names:
            out.append(flat[:, off:off + a.size].reshape(a.shape))
            off += a.size
        return out

    sg, sd, sm, sv = [unpack(o) for o in small_out]

    dmod_cols = lax.dynamic_slice(small_g.reshape(NDEV, -1), (0, me * nmod), (NDEV, nmod))
    gw_mod = _mod_wgrad(act_all, dmod_cols)
    big = {"w_mod": _adamw("adamw_w_mod", w_mod[0], gw_mod, m_w_mod[0], v_w_mod[0], parts=False)}

    for name, w, p, m, v in (("w_in", w_in, p_in, m_w_in, v_w_in), ("w_uq", w_uq, p_uq, m_w_uq, v_w_uq),
                             ("w_ukv", w_ukv, p_ukv, m_w_ukv, v_w_ukv), ("w_out", w_out, p_out, m_w_out, v_w_out),
                             ("w_ff1", w_ff1, p_ff1, m_w_ff1, v_w_ff1), ("w_ff2", w_ff2, p_ff2, m_w_ff2, v_w_ff2)):
        big[name] = _adamw("adamw_" + name, w[0], p, m[0], v[0], parts=True)

    order = ("w_mod", "b_mod", "attn_norm_g", "w_in", "swa_sinks", "rel_bias", "mla_q_norm_g", "w_uq", "mla_kv_norm_g",
             "w_ukv", "w_out", "mlp_norm_g", "w_ff1", "w_ff2", "final_norm_g")
    small_idx = {"b_mod": 0, "attn_norm_g": 1, "mlp_norm_g": 2, "final_norm_g": 3, "mla_q_norm_g": 4,
                 "mla_kv_norm_g": 5, "swa_sinks": 6, "rel_bias": 7}
    outs = []
    for kind, small_list in enumerate((sg, sd, sm, sv)):
        for name in order:
            outs.append(small_list[small_idx[name]] if name in small_idx else big[name][kind][None])
    return (loss, gx[None], *outs)
```

```python
import functools
import math

import jax
import jax.numpy as jnp
from jax import lax
from jax.experimental import pallas as pl
from jax.experimental.pallas import tpu as pltpu

F32 = jnp.float32
BF16 = jnp.bfloat16

NDEV = 8
EPS = 1e-6
BLOCK = 128
SWA_HEADS, SWA_KV, SWA_DH, SWA_GROUP = 16, 2, 64, 8
REL_BUCKETS, REL_MAX_DIST = 32, 128
MLA_H, Q_RANK, KV_RANK, NOPE, ROPE, VDIM = 8, 384, 128, 128, 64, 128
ROPE_THETA = 10000.0
OFF_K, OFF_V, OFF_CQ, OFF_CKV, OFF_KR, IN_COLS = 1024, 1152, 1280, 1664, 1792, 1856
IN_EXT = IN_COLS + ROPE
TAIL0, TAIL = OFF_CQ, IN_EXT - OFF_CQ
QW = NOPE + 2 * ROPE
MIX = SWA_HEADS * SWA_DH + MLA_H * VDIM
MLA_SCALE = (NOPE + ROPE) ** -0.5
SWA_SCALE = SWA_DH ** -0.5

ADAM_LR, ADAM_B1, ADAM_B2, ADAM_EPS, ADAM_WD, ADAM_STEP = 0.001, 0.9, 0.999, 1e-08, 0.01, 10

VMEM_LIMIT = 52 * 1024 * 1024
ROW_TILE = 256
MM_TM, MM_TN, MM_TK = 1024, 1024, 2048
ATT_T = 512
MLA_HB = 2
ADAM_ELEMS = 128 * 1024


MESH_ID = pl.DeviceIdType.MESH


def _place():
    x, y, c = lax.axis_index("x"), lax.axis_index("y"), lax.axis_index("c")
    return x, y, c, 2 * x + y


def _chip(x, y, k):
    return (1 - x if k & 2 else x, 1 - y if k & 1 else y)


def _dma_sems(*counts):
    return [pltpu.SemaphoreType.DMA((n,)) for n in counts]


class _Gather:
    def __init__(self, arrays):
        self.arrays = list(arrays)
        n = len(self.arrays)
        self.out_shape = [jax.ShapeDtypeStruct((NDEV,) + a.shape, a.dtype) for a in self.arrays]
        self.sems = _dma_sems(7 * n, 7 * n, n)

    def _copy(self, sems, a, k, src, dst, to):
        return pltpu.make_async_remote_copy(src_ref=src, dst_ref=dst, send_sem=sems[0].at[7 * a + k],
                                            recv_sem=sems[1].at[7 * a + k], device_id=to, device_id_type=MESH_ID)

    def start(self, ins, outs, sems):
        x, y, c, q = _place()
        me = 2 * q + c
        for a in range(len(ins)):
            pltpu.make_async_copy(ins[a], outs[a].at[me], sems[2].at[a]).start()
            self._copy(sems, a, 0, ins[a], outs[a].at[me], (x, y, 1 - c)).start()
            for k in (1, 2, 3):
                self._copy(sems, a, k, ins[a], outs[a].at[me], (*_chip(x, y, k), c)).start()

    def finish(self, ins, outs, sems):
        x, y, c, q = _place()
        me, sib = 2 * q + c, (x, y, 1 - c)
        n = len(ins)
        for k in (1, 2, 3):
            for a in range(n):
                blk = outs[a].at[2 * (q ^ k) + c]
                self._copy(sems, a, k, ins[a], blk, (*_chip(x, y, k), c)).wait_recv()
                self._copy(sems, a, 3 + k, blk, blk, sib).start()
        for a in range(n):
            self._copy(sems, a, 0, ins[a], outs[a].at[2 * q + 1 - c], sib).wait_recv()
            for k in (1, 2, 3):
                blk = outs[a].at[2 * (q ^ k) + 1 - c]
                self._copy(sems, a, 3 + k, blk, blk, sib).wait_recv()
        for a in range(n):
            for k in range(7):
                self._copy(sems, a, k, ins[a], outs[a].at[me], sib).wait_send()
            pltpu.make_async_copy(ins[a], outs[a].at[me], sems[2].at[a]).wait()


class _PairSwap:
    def __init__(self, arrays):
        self.arrays = list(arrays)
        n = len(self.arrays)
        self.out_shape = [jax.ShapeDtypeStruct((NDEV // 2,) + a.shape[1:], a.dtype) for a in self.arrays]
        self.sems = _dma_sems(4 * n, 4 * n)

    def _copy(self, sems, a, p, src, dst, to):
        return pltpu.make_async_remote_copy(src_ref=src, dst_ref=dst, send_sem=sems[0].at[4 * a + p],
                                            recv_sem=sems[1].at[4 * a + p], device_id=to, device_id_type=MESH_ID)

    def start(self, ins, outs, sems):
        x, y, c, _ = _place()
        for a in range(len(ins)):
            for p in range(4):
                self._copy(sems, a, p, ins[a].at[2 * p + 1 - c], outs[a].at[p], (x, y, 1 - c)).start()

    def finish(self, ins, outs, sems):
        x, y, c, _ = _place()
        for a in range(len(ins)):
            for p in range(4):
                cp = self._copy(sems, a, p, ins[a].at[2 * p + 1 - c], outs[a].at[p], (x, y, 1 - c))
                cp.wait_recv()
                cp.wait_send()


class _ChipScatter:
    def __init__(self, arrays):
        self.arrays = list(arrays)
        n = len(self.arrays)
        self.out_shape = [jax.ShapeDtypeStruct(a.shape, a.dtype) for a in self.arrays]
        self.sems = _dma_sems(3 * n, 3 * n, n)

    def _copy(self, sems, a, k, src, dst, to):
        return pltpu.make_async_remote_copy(src_ref=src, dst_ref=dst, send_sem=sems[0].at[3 * a + k - 1],
                                            recv_sem=sems[1].at[3 * a + k - 1], device_id=to, device_id_type=MESH_ID)

    def start(self, ins, outs, sems):
        x, y, c, q = _place()
        for a in range(len(ins)):
            pltpu.make_async_copy(ins[a].at[q], outs[a].at[q], sems[2].at[a]).start()
            for k in (1, 2, 3):
                self._copy(sems, a, k, ins[a].at[q ^ k], outs[a].at[q], (*_chip(x, y, k), c)).start()

    def finish(self, ins, outs, sems):
        x, y, c, q = _place()
        for a in range(len(ins)):
            for k in (1, 2, 3):
                cp = self._copy(sems, a, k, ins[a].at[q ^ k], outs[a].at[q ^ k], (*_chip(x, y, k), c))
                cp.wait_recv()
                cp.wait_send()
            pltpu.make_async_copy(ins[a].at[q], outs[a].at[q], sems[2].at[a]).wait()


def _call(body, **kw):
    return pl.pallas_call(body, **kw)


def _pcall(body, comm=None, **kw):
    if not comm:
        return _call(body, **kw)
    grid = kw["grid"]
    in_specs, out_specs, out_shape = list(kw["in_specs"]), list(kw["out_specs"]), list(kw["out_shape"])
    scratch = list(kw.get("scratch_shapes", ()))
    n_in, n_out, n_scr = len(in_specs), len(out_shape), len(scratch)
    n_cin = [len(j.arrays) for j in comm]
    n_sem = [len(j.sems) for j in comm]
    n = sum(n_cin)
    hbm = pl.BlockSpec(memory_space=pltpu.HBM)

    def carried(*refs):
        ins, cins = refs[:n_in], refs[n_in:n_in + n]
        outs, couts = refs[n_in + n:n_in + n + n_out], refs[n_in + n + n_out:n_in + 2 * n + n_out]
        scr, sems = refs[n_in + 2 * n + n_out:n_in + 2 * n + n_out + n_scr], refs[n_in + 2 * n + n_out + n_scr:]
        ids = [pl.program_id(ax) for ax in range(len(grid))]
        first = functools.reduce(jnp.logical_and, [i == 0 for i in ids])
        last = functools.reduce(jnp.logical_and, [i == g - 1 for i, g in zip(ids, grid)])

        def each(method):
            ai = si = 0
            for job, na, ns in zip(comm, n_cin, n_sem):
                getattr(job, method)(cins[ai:ai + na], couts[ai:ai + na], sems[si:si + ns])
                ai, si = ai + na, si + ns

        @pl.when(first)
        def _():
            each("start")

        body(*ins, *outs, *scr)

        @pl.when(last)
        def _():
            each("finish")

    kw.update(in_specs=in_specs + [hbm] * n, out_specs=out_specs + [hbm] * n,
              out_shape=out_shape + [o for j in comm for o in j.out_shape],
              scratch_shapes=scratch + [sm for j in comm for sm in j.sems],
              compiler_params=_cparams(("arbitrary",) * len(grid)))
    call = _call(carried, **kw)
    return lambda *args: call(*args, *[a for j in comm for a in j.arrays])


def _cparams(sem):
    return pltpu.CompilerParams(dimension_semantics=sem, vmem_limit_bytes=VMEM_LIMIT)


def _pick(n, pref, align):
    if n <= pref:
        return n
    t = (pref // align) * align
    while t >= align:
        if n % t == 0:
            return t
        t -= align
    return n


def _split3(x):
    a = x.astype(BF16)
    r = x - a.astype(F32)
    b = r.astype(BF16)
    c = (r - b.astype(F32)).astype(BF16)
    return a, b, c


def _exchange(name, job):
    n = len(job.arrays)

    def body(*refs):
        ins, outs, sems = refs[:n], refs[n:2 * n], refs[2 * n:]
        job.start(ins, outs, sems)
        job.finish(ins, outs, sems)

    hbm = pl.BlockSpec(memory_space=pltpu.HBM)
    return _call(body, name=name, out_shape=job.out_shape, in_specs=[hbm] * n, out_specs=[hbm] * n,
                 scratch_shapes=job.sems)(*job.arrays)


def _pair_sum(name, g, r, core):
    _, rr, cc = g.shape
    tr = rr if rr * cc <= 4 * ADAM_ELEMS else _pick(rr, max(16, 4 * ADAM_ELEMS // cc // 16 * 16), 16)

    def body(g_ref, r_ref, c_ref, o_ref):
        north = c_ref[:, 0:1] > 0.5
        mine = jnp.where(north, g_ref[1].astype(F32), g_ref[0].astype(F32))
        o_ref[...] = (mine + r_ref[...].astype(F32)).astype(o_ref.dtype)

    return _pcall(
        body, name=name, grid=(NDEV // 2, rr // tr),
        in_specs=[pl.BlockSpec((None, 2, tr, cc), lambda p, i: (p, 0, i, 0)),
                  pl.BlockSpec((None, tr, cc), lambda p, i: (p, i, 0)), pl.BlockSpec((1, 128), lambda p, i: (0, 0))],
        out_specs=pl.BlockSpec((None, tr, cc), lambda p, i: (p, i, 0)),
        out_shape=jax.ShapeDtypeStruct((NDEV // 2, rr, cc), g.dtype),
        compiler_params=_cparams(("parallel", "parallel")))(g.reshape(NDEV // 2, 2, rr, cc), r, core)


_DIMS = {"nn": (((1,), (0,)), ((), ())), "nt": (((1,), (1,)), ((), ())), "tn": (((0,), (0,)), ((), ()))}


def _mm(name, a, b, kind, grid, a_spec, b_spec, out_shape, out_specs, acc_shape, epilogue,
        extras=(), extra_specs=(), comm=None, b_parts=1):
    nk, ne, no = grid[2], len(extras), len(out_shape)

    def body(*refs):
        a_ref, b_ref = refs[0], refs[1]
        ex, outs = refs[2:2 + ne], refs[2 + ne:2 + ne + no]
        if b_parts == 1:
            part = lax.dot_general(a_ref[...].astype(BF16), b_ref[...].astype(BF16), _DIMS[kind],
                                   preferred_element_type=F32)
        else:
            kp = a_ref.shape[1] // b_parts
            part = sum(lax.dot_general(a_ref[:, p * kp:(p + 1) * kp].astype(BF16), b_ref[p].astype(BF16), _DIMS[kind],
                                       preferred_element_type=F32) for p in range(b_parts))
        if nk == 1:
            epilogue(part, ex, outs)
            return
        acc = refs[-1]
        k = pl.program_id(2)

        @pl.when(k == 0)
        def _():
            acc[...] = part

        @pl.when(jnp.logical_and(k > 0, k < nk - 1))
        def _():
            acc[...] += part

        @pl.when(k == nk - 1)
        def _():
            epilogue(acc[...] + part, ex, outs)

    return _pcall(
        body, comm=comm, name=name, grid=grid, in_specs=[a_spec, b_spec, *extra_specs], out_specs=out_specs,
        out_shape=out_shape, scratch_shapes=[pltpu.VMEM(acc_shape, F32)] if nk > 1 else [],
        compiler_params=_cparams(("parallel", "parallel", "arbitrary")),
    )(a, b, *extras)


def _store(dtype):
    def epi(acc, ex, outs):
        outs[0][...] = acc.astype(dtype)
    return epi


def _mm_plain(name, a, b, kind, m, n, k, out_dtype, tm=None, tn=None, tk=None):
    tm = _pick(m, tm or MM_TM, 128)
    tn = _pick(n, tn or MM_TN, 128)
    tk = _pick(k, tk or MM_TK, 128)
    a_spec = pl.BlockSpec((tk, tm), lambda i, j, q: (q, i)) if kind == "tn" else pl.BlockSpec((tm, tk), lambda i, j, q: (i, q))
    b_spec = pl.BlockSpec((tn, tk), lambda i, j, q: (j, q)) if kind == "nt" else pl.BlockSpec((tk, tn), lambda i, j, q: (q, j))
    return _mm(name, a, b, kind, (m // tm, n // tn, k // tk), a_spec, b_spec,
               [jax.ShapeDtypeStruct((m, n), out_dtype)], [pl.BlockSpec((tm, tn), lambda i, j, q: (i, j))],
               (tm, tn), _store(out_dtype))[0]


def _row(ts, d):
    return pl.BlockSpec((ts, d), lambda i: (i, 0))


def _vec(d):
    return pl.BlockSpec((1, d), lambda i: (0, 0))


def _norm_mod(name, x, gain, sc, sh):
    s, d = x.shape
    ts = _pick(s, ROW_TILE, 16)

    def body(x_ref, g_ref, sc_ref, sh_ref, h_ref):
        xv = x_ref[...]
        r = lax.rsqrt(jnp.mean(xv * xv, axis=-1, keepdims=True) + EPS)
        h_ref[...] = ((xv * r) * g_ref[...] * (1.0 + sc_ref[...]) + sh_ref[...]).astype(BF16)

    return _pcall(body, name=name, grid=(s // ts,), in_specs=[_row(ts, d), _vec(d), _vec(d), _vec(d)],
                  out_specs=_row(ts, d), out_shape=jax.ShapeDtypeStruct((s, d), BF16),
                  compiler_params=_cparams(("parallel",)))(x, gain, sc, sh)


def _loss_head(x3, tgt, y2, gf, g2):
    s, d = x3.shape
    ts = _pick(s, ROW_TILE, 16)

    def body(x_ref, t_ref, y_ref, gf_ref, g2_ref, dx_ref, dy_ref, loss_ref, dgf_ref, dg2_ref):
        @pl.when(pl.program_id(0) == 0)
        def _():
            loss_ref[...] = jnp.zeros_like(loss_ref)
            dgf_ref[...] = jnp.zeros_like(dgf_ref)
            dg2_ref[...] = jnp.zeros_like(dg2_ref)

        xv = x_ref[...]
        r = lax.rsqrt(jnp.mean(xv * xv, axis=-1, keepdims=True) + EPS)
        xn = xv * r
        err = xn * gf_ref[...] - t_ref[...]
        loss_ref[...] += 0.5 * jnp.sum(jnp.mean(err * err, axis=-1, keepdims=True), axis=0, keepdims=True)
        dout = err * (1.0 / d)
        dgf_ref[...] += jnp.sum(dout * xn, axis=0, keepdims=True)
        dxn = dout * gf_ref[...]
        dx = r * (dxn - xn * jnp.mean(dxn * xn, axis=-1, keepdims=True))
        dx_ref[...] = dx
        dy_ref[...] = (dx * g2_ref[...]).astype(BF16)
        dg2_ref[...] += jnp.sum(dx * y_ref[...], axis=0, keepdims=True)

    one = pl.BlockSpec((1, 1), lambda i: (0, 0))
    return _pcall(
        body, name="loss_head", grid=(s // ts,),
        in_specs=[_row(ts, d), _row(ts, d), _row(ts, d), _vec(d), _vec(d)],
        out_specs=[_row(ts, d), _row(ts, d), one, _vec(d), _vec(d)],
        out_shape=[jax.ShapeDtypeStruct((s, d), F32), jax.ShapeDtypeStruct((s, d), BF16),
                   jax.ShapeDtypeStruct((1, 1), F32), jax.ShapeDtypeStruct((1, d), F32),
                   jax.ShapeDtypeStruct((1, d), F32)],
        compiler_params=_cparams(("arbitrary",)))(x3, tgt, y2, gf, g2)


def _norm_mod_bwd(name, x, dh, dres, gain, sc, y_prev=None, gate=None, comm=None):
    s, d = x.shape
    ts = _pick(s, ROW_TILE, 16)
    gated = y_prev is not None

    def body(*refs):
        if gated:
            x_ref, dh_ref, dr_ref, g_ref, sc_ref, y_ref, gt_ref, dx_ref, dy_ref, dsc_ref, dsh_ref, dg_ref, dgt_ref = refs
        else:
            x_ref, dh_ref, dr_ref, g_ref, sc_ref, dx_ref, dsc_ref, dsh_ref, dg_ref = refs

        @pl.when(pl.program_id(0) == 0)
        def _():
            dsc_ref[...] = jnp.zeros_like(dsc_ref)
            dsh_ref[...] = jnp.zeros_like(dsh_ref)
            dg_ref[...] = jnp.zeros_like(dg_ref)
            if gated:
                dgt_ref[...] = jnp.zeros_like(dgt_ref)

        xv, dhv = x_ref[...], dh_ref[...]
        r = lax.rsqrt(jnp.mean(xv * xv, axis=-1, keepdims=True) + EPS)
        xn = xv * r
        dsc_ref[...] += jnp.sum(dhv * (xn * g_ref[...]), axis=0, keepdims=True)
        dsh_ref[...] += jnp.sum(dhv, axis=0, keepdims=True)
        da = dhv * (1.0 + sc_ref[...])
        dg_ref[...] += jnp.sum(da * xn, axis=0, keepdims=True)
        dxn = da * g_ref[...]
        dx = dr_ref[...] + r * (dxn - xn * jnp.mean(dxn * xn, axis=-1, keepdims=True))
        dx_ref[...] = dx
        if gated:
            dy_ref[...] = (dx * gt_ref[...]).astype(BF16)
            dgt_ref[...] += jnp.sum(dx * y_ref[...], axis=0, keepdims=True)

    ins = [x, dh, dres, gain, sc] + ([y_prev, gate] if gated else [])
    in_specs = [_row(ts, d)] * 3 + [_vec(d)] * 2 + ([_row(ts, d), _vec(d)] if gated else [])
    vec_out = jax.ShapeDtypeStruct((1, d), F32)
    out_shape = [jax.ShapeDtypeStruct((s, d), F32)] + ([jax.ShapeDtypeStruct((s, d), BF16)] if gated else [])
    out_shape += [vec_out] * (4 if gated else 3)
    out_specs = [_row(ts, d)] * (2 if gated else 1) + [_vec(d)] * (4 if gated else 3)
    return _pcall(body, comm=comm, name=name, grid=(s // ts,), in_specs=in_specs, out_specs=out_specs,
                  out_shape=out_shape, compiler_params=_cparams(("arbitrary",)))(*ins)


def _dot3(a, b, dims):
    a1, a2, _ = _split3(a)
    b1, b2, _ = _split3(b)
    dot = functools.partial(lax.dot_general, dimension_numbers=dims, preferred_element_type=F32)
    return dot(a1, b1) + (dot(a1, b2) + dot(a2, b1))


def _mod_fwd(c_all, w, b_cols, comm=None):
    nb, d = c_all.shape
    n = w.shape[1]
    tk = _pick(d, 512, 128)
    nk = d // tk

    def body(c_ref, w_ref, b_ref, act_ref, out_ref):
        k = pl.program_id(0)
        cv = c_ref[...]
        act = cv * (1.0 / (1.0 + jnp.exp(-cv)))
        act_ref[...] = act

        @pl.when(k == 0)
        def _():
            out_ref[...] = jnp.broadcast_to(b_ref[...], out_ref.shape)

        out_ref[...] += _dot3(act, w_ref[...], _DIMS["nn"])

    return _pcall(
        body, comm=comm, name="mod_fwd", grid=(nk,),
        in_specs=[pl.BlockSpec((nb, tk), lambda k: (0, k)), pl.BlockSpec((tk, n), lambda k: (k, 0)),
                  pl.BlockSpec((1, n), lambda k: (0, 0))],
        out_specs=[pl.BlockSpec((nb, tk), lambda k: (0, k)), pl.BlockSpec((nb, n), lambda k: (0, 0))],
        out_shape=[jax.ShapeDtypeStruct((nb, d), F32), jax.ShapeDtypeStruct((nb, n), F32)],
        compiler_params=_cparams(("arbitrary",)))(c_all, w, b_cols)


def _mod_wgrad(act_all, dmod_cols):
    nb, d = act_all.shape
    n = dmod_cols.shape[1]
    tm = _pick(d, 512, 128)

    def body(a_ref, d_ref, o_ref):
        o_ref[...] = _dot3(a_ref[...], d_ref[...], _DIMS["tn"])

    return _pcall(
        body, name="mod_wgrad", grid=(d // tm,),
        in_specs=[pl.BlockSpec((nb, tm), lambda i: (0, i)), pl.BlockSpec((nb, n), lambda i: (0, 0))],
        out_specs=pl.BlockSpec((tm, n), lambda i: (i, 0)), out_shape=jax.ShapeDtypeStruct((d, n), F32),
        compiler_params=_cparams(("parallel",)))(act_all, dmod_cols)


def _bias_expand(rel_t, onehot_t):
    h, _ = rel_t.shape
    n = onehot_t.shape[1]

    def body(r_ref, o_ref, out_ref):
        a, b, c = _split3(r_ref[...])
        dot = functools.partial(lax.dot_general, dimension_numbers=_DIMS["nn"], preferred_element_type=F32)
        oh = o_ref[...]
        out_ref[...] = dot(a, oh) + (dot(b, oh) + dot(c, oh))

    full = lambda shp: pl.BlockSpec(shp, lambda: (0,) * len(shp))
    return _pcall(body, name="bias_expand", in_specs=[full(rel_t.shape), full(onehot_t.shape)],
                  out_specs=full((h, n)), out_shape=jax.ShapeDtypeStruct((h, n), F32),
                  compiler_params=pltpu.CompilerParams(vmem_limit_bytes=VMEM_LIMIT))(rel_t, onehot_t)


def _bias_reduce(dbias, onehot, dsink_rows):
    h, n = dbias.shape

    def body(d_ref, o_ref, s_ref, out_ref, so_ref):
        a, b, c = _split3(d_ref[...])
        dot = functools.partial(lax.dot_general, dimension_numbers=_DIMS["nn"], preferred_element_type=F32)
        oh = o_ref[...]
        out_ref[...] = dot(a, oh) + (dot(b, oh) + dot(c, oh))
        so_ref[...] = jnp.sum(s_ref[...], axis=-1, keepdims=True)

    full = lambda shp: pl.BlockSpec(shp, lambda: (0,) * len(shp))
    return _pcall(body, name="bias_reduce", in_specs=[full(dbias.shape), full(onehot.shape), full(dsink_rows.shape)],
                  out_specs=[full((h, REL_BUCKETS)), full((h, 1))],
                  out_shape=[jax.ShapeDtypeStruct((h, REL_BUCKETS), F32), jax.ShapeDtypeStruct((h, 1), F32)],
                  compiler_params=pltpu.CompilerParams(vmem_limit_bytes=VMEM_LIMIT))(dbias, onehot, dsink_rows)


PAIRS = SWA_GROUP // 2
PROWS = PAIRS * BLOCK
PCOLS = 2 * 2 * BLOCK


def _swa2_specs():
    tok = lambda width: pl.BlockSpec((BLOCK, width), lambda g, n: (n, g))
    prev = pl.BlockSpec((None, BLOCK, 2 * SWA_DH), lambda g, n: (g, jnp.maximum(n - 1, 0), 0))
    cur = pl.BlockSpec((None, BLOCK, 2 * SWA_DH), lambda g, n: (g, n, 0))
    bias_spec = pl.BlockSpec((None, PROWS, PCOLS), lambda g, n: (g, 0, 0))
    col_spec = pl.BlockSpec((None, PROWS, 1), lambda g, n: (g, 0, 0))
    lse_spec = pl.BlockSpec((None, None, PROWS, 1), lambda g, n: (g, n, 0, 0))
    return tok, prev, cur, bias_spec, col_spec, lse_spec


def _stack_pairs(blk):
    return jnp.concatenate([blk[:, p * 2 * SWA_DH:(p + 1) * 2 * SWA_DH] for p in range(PAIRS)], axis=0)


def _band(tp, tc, bp, bc):
    return jnp.concatenate([tp[...], tc[...], bp[...], bc[...]], axis=0)


def _swa2_scores(q_ref, kd, bias_ref, n):
    q2 = _stack_pairs(q_ref[...])
    s2 = lax.dot_general(q2, kd, _DIMS["nt"], preferred_element_type=F32) * SWA_SCALE + bias_ref[...]
    col = lax.broadcasted_iota(jnp.int32, s2.shape, 1)
    before_start = jnp.logical_and(n == 0, (col & (2 * BLOCK - 1)) < BLOCK)
    return q2, jnp.where(before_start, -jnp.inf, s2)


def _swa2_fwd(src, ktop, kbot, vtop, vbot, bias, sinks, comm=None):
    s = src.shape[0]
    nb = s // BLOCK
    tok, prev, cur, bias_spec, col_spec, lse_spec = _swa2_specs()

    def body(q_ref, ktp, ktc, kbp, kbc, vtp, vtc, vbp, vbc, bias_ref, sa_ref, sb_ref, o_ref, la_ref, lb_ref):
        n = pl.program_id(1)
        _, s2 = _swa2_scores(q_ref, _band(ktp, ktc, kbp, kbc), bias_ref, n)
        row = lax.broadcasted_iota(jnp.int32, (PCOLS, 2 * SWA_DH), 0)
        lane = lax.broadcasted_iota(jnp.int32, (PCOLS, 2 * SWA_DH), 1)
        ones = jnp.where(lane == row // (2 * BLOCK), 1.0, 0.0).astype(BF16)
        ps, ms, sinks_ = [], [], []
        for half, sink_ref in enumerate((sa_ref, sb_ref)):
            sc = s2[:, half * 2 * BLOCK:(half + 1) * 2 * BLOCK]
            m = jnp.maximum(jnp.max(sc, axis=-1, keepdims=True), sink_ref[...])
            ps.append(jnp.exp(sc - m).astype(BF16))
            ms.append(m)
        acc = lax.dot_general(jnp.concatenate(ps, axis=1), jnp.concatenate([_band(vtp, vtc, vbp, vbc), ones], axis=1),
                              _DIMS["nn"], preferred_element_type=F32)
        dens = []
        for half, (sink_ref, lse_ref) in enumerate(((sa_ref, la_ref), (sb_ref, lb_ref))):
            den = acc[:, 2 * SWA_DH + half:2 * SWA_DH + half + 1] + jnp.exp(sink_ref[...] - ms[half])
            lse_ref[...] = ms[half] + jnp.log(den)
            dens.append(den)
        lo = lax.broadcasted_iota(jnp.int32, (PROWS, 2 * SWA_DH), 1) < SWA_DH
        o2 = acc[:, 0:2 * SWA_DH] / jnp.where(lo, dens[0], dens[1])
        for p in range(PAIRS):
            o_ref[:, p * 2 * SWA_DH:(p + 1) * 2 * SWA_DH] = o2[p * BLOCK:(p + 1) * BLOCK].astype(BF16)

    lse_shape = jax.ShapeDtypeStruct((SWA_KV, nb, PROWS, 1), F32)
    return _pcall(
        body, comm=comm, name="swa_fwd", grid=(SWA_KV, nb),
        in_specs=[tok(PROWS), prev, cur, prev, cur, prev, cur, prev, cur, bias_spec, col_spec, col_spec],
        out_specs=[tok(PROWS), lse_spec, lse_spec],
        out_shape=[jax.ShapeDtypeStruct((s, SWA_HEADS * SWA_DH), BF16), lse_shape, lse_shape],
        compiler_params=_cparams(("parallel", "parallel")))(
            src, ktop, ktop, kbot, kbot, vtop, vtop, vbot, vbot, bias, sinks[0], sinks[1])


def _swa2_bwd(src, dsrc, ktop, kbot, vtop, vbot, lses, bias, sinks, comm=None):
    s = src.shape[0]
    nb = s // BLOCK
    tok, prev, cur, bias_spec, col_spec, lse_spec = _swa2_specs()
    lane_lo = lambda shape: lax.broadcasted_iota(jnp.int32, shape, 1) < SWA_DH

    def body(q_ref, do_ref, ktp, ktc, kbp, kbc, vtp, vtc, vbp, vbc, la_ref, lb_ref, bias_ref, sa_ref, sb_ref,
             dq_ref, dkp_ref, dkc_ref, dvp_ref, dvc_ref, dbias_ref, dsa_ref, dsb_ref):
        n = pl.program_id(1)

        @pl.when(n == 0)
        def _():
            dbias_ref[...] = jnp.zeros_like(dbias_ref)
            dsa_ref[...] = jnp.zeros_like(dsa_ref)
            dsb_ref[...] = jnp.zeros_like(dsb_ref)

        kd = _band(ktp, ktc, kbp, kbc)
        q2, s2 = _swa2_scores(q_ref, kd, bias_ref, n)
        do2 = _stack_pairs(do_ref[...]).astype(BF16)
        dp2 = lax.dot_general(do2, _band(vtp, vtc, vbp, vbc), _DIMS["nt"], preferred_element_type=F32)
        ps, dss = [], []
        for half, (sink_ref, lse_ref, dsink_ref) in enumerate(((sa_ref, la_ref, dsa_ref), (sb_ref, lb_ref, dsb_ref))):
            cols = slice(half * 2 * BLOCK, (half + 1) * 2 * BLOCK)
            lse_v = lse_ref[...]
            p = jnp.exp(s2[:, cols] - lse_v)
            dp = dp2[:, cols]
            delta = jnp.sum(p * dp, axis=-1, keepdims=True)
            ds = p * (dp - delta)
            dsink_ref[...] += -jnp.exp(sink_ref[...] - lse_v) * delta
            ps.append(p.astype(BF16))
            dss.append(ds)
        ds2 = jnp.concatenate(dss, axis=1)
        dbias_ref[...] += ds2
        dsb2 = (ds2 * SWA_SCALE).astype(BF16)
        dq2 = lax.dot_general(dsb2, kd, _DIMS["nn"], preferred_element_type=F32)
        for p in range(PAIRS):
            dq_ref[:, p * 2 * SWA_DH:(p + 1) * 2 * SWA_DH] = dq2[p * BLOCK:(p + 1) * BLOCK].astype(BF16)
        dk = lax.dot_general(dsb2, q2, _DIMS["tn"], preferred_element_type=F32)
        dv = lax.dot_general(jnp.concatenate(ps, axis=1), do2, _DIMS["tn"], preferred_element_type=F32)
        for full, prev_ref, cur_ref in ((dk, dkp_ref, dkc_ref), (dv, dvp_ref, dvc_ref)):
            own = jnp.where(lane_lo((2 * BLOCK, 2 * SWA_DH)), full[:2 * BLOCK], full[2 * BLOCK:])
            prev_ref[...] = own[:BLOCK]
            cur_ref[...] = own[BLOCK:]

    kv_out = jax.ShapeDtypeStruct((SWA_KV, s, 2 * SWA_DH), F32)
    col_out = jax.ShapeDtypeStruct((SWA_KV, PROWS, 1), F32)
    return _pcall(
        body, comm=comm, name="swa_bwd", grid=(SWA_KV, nb),
        in_specs=[tok(PROWS), tok(PROWS), prev, cur, prev, cur, prev, cur, prev, cur, lse_spec, lse_spec, bias_spec,
                  col_spec, col_spec],
        out_specs=[tok(PROWS), cur, cur, cur, cur, bias_spec, col_spec, col_spec],
        out_shape=[jax.ShapeDtypeStruct((s, SWA_HEADS * SWA_DH), BF16), kv_out, kv_out, kv_out, kv_out,
                   jax.ShapeDtypeStruct(bias.shape, F32), col_out, col_out],
        compiler_params=_cparams(("arbitrary", "arbitrary")))(
            src, dsrc, ktop, ktop, kbot, kbot, vtop, vtop, vbot, vbot, lses[0], lses[1], bias, sinks[0], sinks[1])


def _rope_slab(slab, table):
    t = slab * table
    return t + pltpu.roll(t, ROPE, 1)


def _low_lanes(v):
    lane = lax.broadcasted_iota(jnp.int32, v.shape, 1)
    return jnp.where(lane < ROPE, v, 0.0)


def _rms(xv, g):
    r = lax.rsqrt(jnp.mean(xv * xv, axis=-1, keepdims=True) + EPS)
    return xv * r, r


def _mla_prep(proj, gq, gkv, table):
    s = proj.shape[0]
    ts = _pick(s, ROW_TILE, 16)

    def body(p_ref, gq_ref, gkv_ref, t_ref, cq_ref, ckv_ref, kr_ref):
        xq, _ = _rms(p_ref[:, 0:Q_RANK], None)
        cq_ref[...] = (xq * gq_ref[...]).astype(BF16)
        xkv, _ = _rms(p_ref[:, Q_RANK:Q_RANK + KV_RANK], None)
        ckv_ref[...] = (xkv * gkv_ref[...]).astype(BF16)
        kr_ref[...] = _low_lanes(_rope_slab(p_ref[:, Q_RANK + KV_RANK:TAIL], t_ref[...]))

    return _pcall(
        body, name="mla_prep", grid=(s // ts,),
        in_specs=[pl.BlockSpec((ts, TAIL), lambda i: (i, TAIL0 // TAIL)), _vec(Q_RANK), _vec(KV_RANK), _row(ts, 2 * ROPE)],
        out_specs=[_row(ts, Q_RANK), _row(ts, KV_RANK), _row(ts, 2 * ROPE)],
        out_shape=[jax.ShapeDtypeStruct((s, Q_RANK), BF16), jax.ShapeDtypeStruct((s, KV_RANK), BF16),
                   jax.ShapeDtypeStruct((s, 2 * ROPE), F32)],
        compiler_params=_cparams(("parallel",)))(proj, gq, gkv, table)


def _mla_prep_bwd(proj, dcq, dckv, dkr, gq, gkv, table):
    s = proj.shape[0]
    ts = _pick(s, ROW_TILE, 16)

    def norm_bwd(xv, dy, g):
        xn, r = _rms(xv, None)
        dg = jnp.sum(dy * xn, axis=0, keepdims=True)
        dxn = dy * g
        return r * (dxn - xn * jnp.mean(dxn * xn, axis=-1, keepdims=True)), dg

    def body(p_ref, dcq_ref, dckv_ref, dkr_ref, gq_ref, gkv_ref, t_ref, dt_ref, dgq_ref, dgkv_ref):
        @pl.when(pl.program_id(0) == 0)
        def _():
            dgq_ref[...] = jnp.zeros_like(dgq_ref)
            dgkv_ref[...] = jnp.zeros_like(dgkv_ref)

        dxq, dgq = norm_bwd(p_ref[:, 0:Q_RANK], dcq_ref[...], gq_ref[...])
        dxkv, dgkv = norm_bwd(p_ref[:, Q_RANK:Q_RANK + KV_RANK], dckv_ref[...], gkv_ref[...])
        dgq_ref[...] += dgq
        dgkv_ref[...] += dgkv
        d = _low_lanes(dkr_ref[...])
        dslab = (d + pltpu.roll(d, ROPE, 1)) * t_ref[...]
        dt_ref[:, 0:Q_RANK] = dxq.astype(BF16)
        dt_ref[:, Q_RANK:Q_RANK + KV_RANK] = dxkv.astype(BF16)
        dt_ref[:, Q_RANK + KV_RANK:TAIL] = dslab.astype(BF16)

    return _pcall(
        body, name="mla_prep_bwd", grid=(s // ts,),
        in_specs=[pl.BlockSpec((ts, TAIL), lambda i: (i, TAIL0 // TAIL)), _row(ts, Q_RANK), _row(ts, KV_RANK),
                  _row(ts, 2 * ROPE), _vec(Q_RANK), _vec(KV_RANK), _row(ts, 2 * ROPE)],
        out_specs=[_row(ts, TAIL), _vec(Q_RANK), _vec(KV_RANK)],
        out_shape=[jax.ShapeDtypeStruct((s, TAIL), BF16), jax.ShapeDtypeStruct((1, Q_RANK), F32),
                   jax.ShapeDtypeStruct((1, KV_RANK), F32)],
        compiler_params=_cparams(("arbitrary",)))(proj, dcq, dckv, dkr, gq, gkv, table)


def _head_specs(ts):
    tok = lambda w: pl.BlockSpec((ts, w), lambda h, i: (i, 0))
    head = lambda w: pl.BlockSpec((None, ts, w), lambda h, i: (h, i, 0))
    wgt = lambda r, c: pl.BlockSpec((None, r, c), lambda h, i: (h, 0, 0))
    return tok, head, wgt


def _mla_qkv(cq, ckv, kr, wq, wkv, table):
    s = cq.shape[0]
    ts = _pick(s, 2 * ROW_TILE, 16)
    tok, head, wgt = _head_specs(ts)

    def body(cq_ref, ckv_ref, kr_ref, wq_ref, wkv_ref, t_ref, q_ref, k_ref, v_ref):
        qf = lax.dot_general(cq_ref[...], wq_ref[...], _DIMS["nn"], preferred_element_type=F32)
        q_ref[:, 0:NOPE] = qf[:, 0:NOPE].astype(BF16)
        q_ref[:, NOPE:QW] = _rope_slab(qf[:, NOPE:QW], t_ref[...]).astype(BF16)
        kv = lax.dot_general(ckv_ref[...], wkv_ref[...], _DIMS["nn"], preferred_element_type=F32)
        k_ref[:, 0:NOPE] = kv[:, 0:NOPE].astype(BF16)
        k_ref[:, NOPE:QW] = kr_ref[...].astype(BF16)
        v_ref[:, 0:VDIM] = kv[:, NOPE:NOPE + VDIM].astype(BF16)
        lane = lax.broadcasted_iota(jnp.int32, (ts, VDIM), 1)
        v_ref[:, VDIM:2 * VDIM] = jnp.where(lane == 0, 1.0, 0.0).astype(BF16)

    return _pcall(
        body, name="mla_qkv", grid=(MLA_H, s // ts),
        in_specs=[tok(Q_RANK), tok(KV_RANK), tok(2 * ROPE), wgt(Q_RANK, QW), wgt(KV_RANK, NOPE + VDIM), tok(2 * ROPE)],
        out_specs=[head(QW), head(QW), head(2 * VDIM)],
        out_shape=[jax.ShapeDtypeStruct((MLA_H, s, QW), BF16), jax.ShapeDtypeStruct((MLA_H, s, QW), BF16),
                   jax.ShapeDtypeStruct((MLA_H, s, 2 * VDIM), BF16)],
        compiler_params=_cparams(("parallel", "parallel")))(cq, ckv, kr, wq, wkv, table)


def _mla_qkv_bwd(dq, dk, dv, cq, ckv, wq, wkv, table, comm=None):
    s = cq.shape[0]
    ts = _pick(s, 2 * ROW_TILE, 16)
    tok, head, wgt = _head_specs(ts)
    whole = lambda w: pl.BlockSpec((s, w), lambda h, i: (0, 0))

    def body(dq_ref, dk_ref, dv_ref, cq_ref, ckv_ref, wq_ref, wkv_ref, t_ref,
             dcq_ref, dckv_ref, dkr_ref, gwq_ref, gwkv_ref):
        h, i = pl.program_id(0), pl.program_id(1)
        rows = pl.ds(pl.multiple_of(i * ts, ts), ts)
        d = dq_ref[:, NOPE:QW]
        dslab = (d + pltpu.roll(d, ROPE, 1)) * t_ref[...]
        dqe = jnp.concatenate([dq_ref[:, 0:NOPE], dslab], axis=1).astype(BF16)
        dkv = jnp.concatenate([dk_ref[:, 0:NOPE], dv_ref[...]], axis=1).astype(BF16)
        dcq = lax.dot_general(dqe, wq_ref[...], _DIMS["nt"], preferred_element_type=F32)
        dckv = lax.dot_general(dkv, wkv_ref[...], _DIMS["nt"], preferred_element_type=F32)
        gwq = lax.dot_general(cq_ref[...], dqe, _DIMS["tn"], preferred_element_type=F32)
        gwkv = lax.dot_general(ckv_ref[...], dkv, _DIMS["tn"], preferred_element_type=F32)
        dkr = dk_ref[:, NOPE:QW].astype(F32)

        @pl.when(h == 0)
        def _():
            dcq_ref[rows, :] = dcq
            dckv_ref[rows, :] = dckv
            dkr_ref[rows, :] = dkr

        @pl.when(h > 0)
        def _():
            dcq_ref[rows, :] += dcq
            dckv_ref[rows, :] += dckv
            dkr_ref[rows, :] += dkr

        @pl.when(i == 0)
        def _():
            gwq_ref[...] = gwq
            gwkv_ref[...] = gwkv

        @pl.when(i > 0)
        def _():
            gwq_ref[...] += gwq
            gwkv_ref[...] += gwkv

    return _pcall(
        body, comm=comm, name="mla_qkv_bwd", grid=(MLA_H, s // ts),
        in_specs=[head(QW), head(QW), head(VDIM), tok(Q_RANK), tok(KV_RANK), wgt(Q_RANK, QW),
                  wgt(KV_RANK, NOPE + VDIM), tok(2 * ROPE)],
        out_specs=[whole(Q_RANK), whole(KV_RANK), whole(2 * ROPE), wgt(Q_RANK, QW), wgt(KV_RANK, NOPE + VDIM)],
        out_shape=[jax.ShapeDtypeStruct((s, Q_RANK), F32), jax.ShapeDtypeStruct((s, KV_RANK), F32),
                   jax.ShapeDtypeStruct((s, 2 * ROPE), F32), jax.ShapeDtypeStruct((MLA_H, Q_RANK, QW), F32),
                   jax.ShapeDtypeStruct((MLA_H, KV_RANK, NOPE + VDIM), F32)],
        compiler_params=_cparams(("arbitrary", "arbitrary")))(dq, dk, dv, cq, ckv, wq, wkv, table)


def _diag_mask(t):
    return lax.broadcasted_iota(jnp.int32, (t, t), 1) <= lax.broadcasted_iota(jnp.int32, (t, t), 0)


def _mla_fwd(q, k, v, comm=None):
    s = q.shape[1]
    t = _pick(s, ATT_T, 128)
    nt = s // t
    assert nt % 2 == 0
    hb = 2 * MLA_HB

    def fold(p, u):
        first = u <= p
        return jnp.where(first, p, nt - 1 - p), jnp.where(first, u, u - p - 1)

    to_log2 = MLA_SCALE * math.log2(math.e)

    def body(q_ref, k_ref, v_ref, o_ref, oh_ref, lse_ref, m_ref, acc_ref):
        i, j = fold(pl.program_id(1), pl.program_id(2))

        @pl.when(j == 0)
        def _():
            m_ref[...] = jnp.full_like(m_ref, -jnp.inf)
            acc_ref[...] = jnp.zeros_like(acc_ref)

        def step(diagonal):
            for h in range(hb):
                sc = lax.dot_general(q_ref[h], k_ref[h], _DIMS["nt"], preferred_element_type=F32)
                if diagonal:
                    sc = jnp.where(_diag_mask(t), sc, -jnp.inf)
                m_old = m_ref[h]
                m_new = jnp.maximum(m_old, jnp.max(sc, axis=-1, keepdims=True))
                alpha = jnp.exp2((m_old - m_new) * to_log2)
                p = jnp.exp2((sc - m_new) * to_log2)
                acc_ref[h] = alpha * acc_ref[h] + lax.dot_general(p.astype(BF16), v_ref[h], _DIMS["nn"],
                                                                  preferred_element_type=F32)
                m_ref[h] = m_new

        @pl.when(j < i)
        def _():
            step(False)

        @pl.when(j == i)
        def _():
            step(True)
            for h in range(hb):
                den = acc_ref[h, :, VDIM:VDIM + 1]
                o = acc_ref[h, :, 0:VDIM] / den
                o_ref[:, h * VDIM:(h + 1) * VDIM] = o
                oh_ref[:, h * VDIM:(h + 1) * VDIM] = o.astype(BF16)
                lse_ref[h] = m_ref[h] * MLA_SCALE + jnp.log(den)

    o_spec = pl.BlockSpec((t, hb * VDIM), lambda h, p, u: (fold(p, u)[0], h))
    return _pcall(
        body, comm=comm, name="mla_fwd", grid=(MLA_H // hb, nt // 2, nt + 1),
        in_specs=[pl.BlockSpec((hb, t, QW), lambda h, p, u: (h, fold(p, u)[0], 0)),
                  pl.BlockSpec((hb, t, QW), lambda h, p, u: (h, fold(p, u)[1], 0)),
                  pl.BlockSpec((hb, t, 2 * VDIM), lambda h, p, u: (h, fold(p, u)[1], 0))],
        out_specs=[o_spec, o_spec, pl.BlockSpec((hb, t, 1), lambda h, p, u: (h, fold(p, u)[0], 0))],
        out_shape=[jax.ShapeDtypeStruct((s, MLA_H * VDIM), F32), jax.ShapeDtypeStruct((s, MLA_H * VDIM), BF16),
                   jax.ShapeDtypeStruct((MLA_H, s, 1), F32)],
        scratch_shapes=[pltpu.VMEM((hb, t, 1), F32), pltpu.VMEM((hb, t, 2 * VDIM), F32)],
        compiler_params=_cparams(("parallel", "parallel", "arbitrary")))(q, k, v)


def _mla_delta(dmix, o):
    s = o.shape[0]
    ts = _pick(s, ROW_TILE, 16)
    w = MLA_H * VDIM

    def body(d_ref, o_ref, out_ref):
        prod = d_ref[...] * o_ref[...]
        for h in range(MLA_H):
            out_ref[h] = jnp.sum(prod[:, h * VDIM:(h + 1) * VDIM], axis=-1, keepdims=True)

    return _pcall(body, name="mla_delta", grid=(s // ts,),
                  in_specs=[pl.BlockSpec((ts, w), lambda i: (i, SWA_HEADS * SWA_DH // w)), pl.BlockSpec((ts, w), lambda i: (i, 0))],
                  out_specs=pl.BlockSpec((MLA_H, ts, 1), lambda i: (0, i, 0)),
                  out_shape=jax.ShapeDtypeStruct((MLA_H, s, 1), F32), compiler_params=_cparams(("parallel",)))(dmix, o)


def _mla_bwd(q, k, v, dmix, delta, lse, comm=None):
    s = q.shape[1]
    t = _pick(s, ATT_T, 128)
    nt = s // t
    assert nt % 2 == 0
    hb = MLA_HB
    o_blk0 = SWA_HEADS * SWA_DH // (hb * VDIM)

    def fold(p, u):
        first = u < nt - p
        return jnp.where(first, p, nt - 1 - p), jnp.where(first, p + u, u - 1)

    log2e = math.log2(math.e)

    def body(q_ref, k_ref, v_ref, do_ref, delta_ref, lse_ref, dq_ref, dk_ref, dv_ref, dk_acc, dv_acc):
        j, i = fold(pl.program_id(1), pl.program_id(2))
        rows = pl.ds(pl.multiple_of(i * t, t), t)

        @pl.when(i == j)
        def _():
            dk_acc[...] = jnp.zeros_like(dk_acc)
            dv_acc[...] = jnp.zeros_like(dv_acc)

        def step(diagonal):
            for h in range(hb):
                qv, kv_ = q_ref[h], k_ref[h]
                dob = do_ref[:, h * VDIM:(h + 1) * VDIM].astype(BF16)
                st = lax.dot_general(kv_, qv, _DIMS["nt"], preferred_element_type=F32)
                pt = jnp.exp2(st * (MLA_SCALE * log2e) - lse_ref[h] * log2e)
                if diagonal:
                    keep = lax.broadcasted_iota(jnp.int32, (t, t), 0) <= lax.broadcasted_iota(jnp.int32, (t, t), 1)
                    pt = jnp.where(keep, pt, 0.0)
                dpt = lax.dot_general(v_ref[h], dob, _DIMS["nt"], preferred_element_type=F32)
                dst = (pt * (dpt - delta_ref[h]) * MLA_SCALE).astype(BF16)
                dv_acc[h] += lax.dot_general(pt.astype(BF16), dob, _DIMS["nn"], preferred_element_type=F32)
                dk_acc[h] += lax.dot_general(dst, qv, _DIMS["nn"], preferred_element_type=F32)
                dqv = lax.dot_general(dst, kv_, _DIMS["tn"], preferred_element_type=F32)

                @pl.when(j == 0)
                def _():
                    dq_ref[h, rows, :] = dqv

                @pl.when(j > 0)
                def _():
                    dq_ref[h, rows, :] += dqv

        @pl.when(i > j)
        def _():
            step(False)

        @pl.when(i == j)
        def _():
            step(True)

        @pl.when(i == nt - 1)
        def _():
            dk_ref[...] = dk_acc[...].astype(BF16)
            dv_ref[...] = dv_acc[...].astype(BF16)

    qi = lambda h, p, u: (h, fold(p, u)[1], 0)
    kj = lambda h, p, u: (h, fold(p, u)[0], 0)
    row = pl.BlockSpec((hb, 1, t), lambda h, p, u: (h, 0, fold(p, u)[1]))
    return _pcall(
        body, comm=comm, name="mla_bwd", grid=(MLA_H // hb, nt // 2, nt + 1),
        in_specs=[pl.BlockSpec((hb, t, QW), qi), pl.BlockSpec((hb, t, QW), kj), pl.BlockSpec((hb, t, VDIM), kj),
                  pl.BlockSpec((t, hb * VDIM), lambda h, p, u: (fold(p, u)[1], o_blk0 + h)), row, row],
        out_specs=[pl.BlockSpec((hb, s, QW), lambda h, p, u: (h, 0, 0)), pl.BlockSpec((hb, t, QW), kj),
                   pl.BlockSpec((hb, t, VDIM), kj)],
        out_shape=[jax.ShapeDtypeStruct((MLA_H, s, QW), F32), jax.ShapeDtypeStruct((MLA_H, s, QW), BF16),
                   jax.ShapeDtypeStruct((MLA_H, s, VDIM), BF16)],
        scratch_shapes=[pltpu.VMEM((hb, t, QW), F32), pltpu.VMEM((hb, t, VDIM), F32)],
        compiler_params=_cparams(("arbitrary", "arbitrary", "arbitrary")))(q, k, v, dmix, delta, lse)


def _adamw(name, w, g, m, v, parts):
    r, c = w.shape
    n_parts = g.shape[0] if parts else 1
    tr = r if r * c <= ADAM_ELEMS else _pick(r, max(8, ADAM_ELEMS // c // 8 * 8), 8)
    c1 = 1.0 - ADAM_B1 ** ADAM_STEP
    c2 = 1.0 - ADAM_B2 ** ADAM_STEP

    def body(w_ref, g_ref, m_ref, v_ref, go_ref, d_ref, mo_ref, vo_ref):
        if parts:
            gv = g_ref[0].astype(F32)
            for j in range(1, n_parts):
                gv = gv + g_ref[j].astype(F32)
        else:
            gv = g_ref[...]
        mv = ADAM_B1 * m_ref[...] + (1.0 - ADAM_B1) * gv
        vv = ADAM_B2 * v_ref[...] + (1.0 - ADAM_B2) * (gv * gv)
        go_ref[...] = gv
        mo_ref[...] = mv
        vo_ref[...] = vv
        d_ref[...] = -ADAM_LR * ((mv / c1) / (jnp.sqrt(vv / c2) + ADAM_EPS) + ADAM_WD * w_ref[...])

    blk = pl.BlockSpec((tr, c), lambda i: (i, 0))
    g_spec = pl.BlockSpec((n_parts, tr, c), lambda i: (0, i, 0)) if parts else blk
    out = jax.ShapeDtypeStruct((r, c), F32)
    return _pcall(body, name=name, grid=(r // tr,), in_specs=[blk, g_spec, blk, blk], out_specs=[blk] * 4,
                  out_shape=[out] * 4, compiler_params=_cparams(("parallel",)))(w, g, m, v)


def _t5_bucket(dist):
    n = jnp.maximum(dist, 0)
    max_exact = REL_BUCKETS // 2
    nf = jnp.maximum(n, 1).astype(F32)
    large = max_exact + (jnp.log(nf / max_exact) / math.log(REL_MAX_DIST / max_exact)
                         * (REL_BUCKETS - max_exact)).astype(jnp.int32)
    return jnp.where(n < max_exact, n, jnp.minimum(large, REL_BUCKETS - 1))


def _swap_halves(w, r0):
    return jnp.concatenate([w[:, r0 + ROPE // 2:r0 + ROPE], w[:, r0:r0 + ROPE // 2]], axis=1)


def _fold_swapped(g, r0, width):
    sw = g[..., width:width + ROPE]
    half = ROPE // 2
    return jnp.concatenate([g[..., :r0], g[..., r0:r0 + half] + sw[..., half:], g[..., r0 + half:r0 + ROPE] + sw[..., :half],
                            g[..., r0 + ROPE:width]], axis=-1)


def kernel(x, c, w_mod, b_mod, attn_norm_g, w_in, swa_sinks, rel_bias, mla_q_norm_g, w_uq, mla_kv_norm_g, w_ukv, w_out, mlp_norm_g, w_ff1, w_ff2, final_norm_g, loss_target, m_w_mod, m_b_mod, m_attn_norm_g, m_w_in, m_swa_sinks, m_rel_bias, m_mla_q_norm_g, m_w_uq, m_mla_kv_norm_g, m_w_ukv, m_w_out, m_mlp_norm_g, m_w_ff1, m_w_ff2, m_final_norm_g, v_w_mod, v_b_mod, v_attn_norm_g, v_w_in, v_swa_sinks, v_rel_bias, v_mla_q_norm_g, v_w_uq, v_mla_kv_norm_g, v_w_ukv, v_w_out, v_mlp_norm_g, v_w_ff1, v_w_ff2, v_final_norm_g):
    s, d = x.shape[1], x.shape[2]
    ffs = w_ff1.shape[2]
    ff = ffs * NDEV
    nmod = w_mod.shape[2]
    me = 4 * lax.axis_index("x") + 2 * lax.axis_index("y") + lax.axis_index("c")
    x2d, tgt = x[0], loss_target[0]
    final_g = final_norm_g.reshape(1, d)

    w_in_l = jnp.concatenate([w_in[0], _swap_halves(w_in[0], OFF_KR)], axis=1).astype(BF16)
    w_uq_l = jnp.concatenate([w_uq[0], _swap_halves(w_uq[0], NOPE)], axis=1).astype(BF16)
    core = jnp.full((1, 128), lax.axis_index("c"), F32)
    (c_all,) = _exchange("gather_c", _Gather([c]))

    b_cols = lax.dynamic_slice(b_mod, (0, me * nmod), (1, nmod))
    act_all, mod_cols, w_in_g, w_uq_g, w_ukv_g = _mod_fwd(
        c_all.reshape(NDEV, d), w_mod[0], b_cols, comm=[_Gather([w_in_l, w_uq_l, w_ukv[0].astype(BF16)])])
    w_in_e = w_in_g.reshape(d, IN_EXT)
    (mod_g,) = _exchange("gather_mod", _Gather([mod_cols]))
    mod = lax.dynamic_index_in_dim(mod_g, me, axis=1, keepdims=False).reshape(1, 6 * d)
    sh1, sc1, g1, sh2, sc2, g2 = [mod[:, i * d:(i + 1) * d] for i in range(6)]

    pos = jnp.arange(s, dtype=F32)
    inv_freq = ROPE_THETA ** (-jnp.arange(ROPE // 2, dtype=F32) / (ROPE // 2))
    ang = pos[:, None] * inv_freq[None, :]
    cos, sin = jnp.cos(ang), jnp.sin(ang)
    table = jnp.concatenate([cos, cos, -sin, sin], axis=1)
    q_loc = jnp.arange(BLOCK)[:, None]
    k_loc = jnp.arange(2 * BLOCK)[None, :]
    dist = q_loc + BLOCK - k_loc
    in_window = (dist >= 0) & (dist < BLOCK)
    onehot = (_t5_bucket(dist).reshape(-1, 1) == jnp.arange(REL_BUCKETS)[None, :]).astype(BF16)
    bias = _bias_expand(rel_bias.T, onehot.T).reshape(SWA_HEADS, BLOCK, 2 * BLOCK)
    bias = jnp.where(in_window[None], bias, -jnp.inf).reshape(SWA_KV, PAIRS, 2, BLOCK, 2 * BLOCK)
    bias = bias.transpose(0, 1, 3, 2, 4).reshape(SWA_KV, PROWS, PCOLS)
    sinks = jnp.broadcast_to(swa_sinks.reshape(SWA_KV, PAIRS, 1, 2), (SWA_KV, PAIRS, BLOCK, 2)).reshape(SWA_KV, PROWS, 2)
    sinks = (sinks[:, :, 0:1], sinks[:, :, 1:2])

    h1 = _norm_mod("norm1", x2d, attn_norm_g, sc1, sh1)
    def both_dtypes(acc, ex, outs):
        outs[0][...] = acc
        outs[1][...] = acc.astype(BF16)

    tmp = _pick(s, MM_TM // 2, 128)
    proj_blk = pl.BlockSpec((tmp, IN_EXT), lambda i, j, q: (i, 0))
    proj, proj_h = _mm("proj", h1, w_in_e, "nn", (s // tmp, 1, 1), pl.BlockSpec((tmp, d), lambda i, j, q: (i, 0)),
                       pl.BlockSpec((d, IN_EXT), lambda i, j, q: (0, 0)),
                       [jax.ShapeDtypeStruct((s, IN_EXT), F32), jax.ShapeDtypeStruct((s, IN_EXT), BF16)],
                       [proj_blk, proj_blk], (tmp, IN_EXT), both_dtypes)
    def diag_pair(tok):
        x = jnp.stack([tok[:, :SWA_DH], tok[:, SWA_DH:]])
        zero = jnp.zeros_like(x)
        return jnp.concatenate([x, zero], axis=2), jnp.concatenate([zero, x], axis=2)

    k_top, k_bot = diag_pair(proj_h[:, OFF_K:OFF_V])
    v_top, v_bot = diag_pair(proj_h[:, OFF_V:OFF_CQ])
    o_a, lse_a0, lse_a1, w_out_g = _swa2_fwd(proj_h, k_top, k_bot, v_top, v_bot, bias, sinks,
                                             comm=[_Gather([w_out[0].astype(BF16)])])
    w_out_f = w_out_g.reshape(MIX, d)

    cq, ckv, kr = _mla_prep(proj, mla_q_norm_g, mla_kv_norm_g, table)
    q_b, k_b, v_b = _mla_qkv(cq, ckv, kr, w_uq_g, w_ukv_g, table)
    o_b, o_bh, lse_b, w_ff1_g = _mla_fwd(q_b, k_b, v_b, comm=[_Gather([w_ff1[0].astype(BF16)])])
    mix = jnp.concatenate([o_a, o_bh], axis=1)

    tm, tn, tk = _pick(s, MM_TM, 128), _pick(d, MM_TN, 128), _pick(MIX, MM_TK, 128)
    row_blk = pl.BlockSpec((tm, tn), lambda i, j, q: (i, j))
    gate_blk = pl.BlockSpec((1, tn), lambda i, j, q: (0, j))

    def gated_residual(acc, ex, outs):
        outs[0][...] = acc
        outs[1][...] = ex[0][...] + ex[1][...] * acc

    y1, x2 = _mm("out_proj", mix, w_out_f, "nn", (s // tm, d // tn, MIX // tk),
                 pl.BlockSpec((tm, tk), lambda i, j, q: (i, q)), pl.BlockSpec((tk, tn), lambda i, j, q: (q, j)),
                 [jax.ShapeDtypeStruct((s, d), F32)] * 2, [row_blk, row_blk], (tm, tn), gated_residual,
                 extras=(x2d, g1), extra_specs=(row_blk, gate_blk))

    h2 = _norm_mod("norm2", x2, mlp_norm_g, sc2, sh2)
    tnf, tkd = _pick(ffs, MM_TN, 128), _pick(d, MM_TK, 128)
    rf = ffs // tnf
    ff_blk = pl.BlockSpec((tm, tnf), lambda i, j, q: (i, j))

    def relu_sq(acc, ex, outs):
        u = jnp.maximum(acc, 0.0)
        outs[0][...] = u
        outs[1][...] = (u * u).astype(BF16)

    u, uu, w_ff2_g = _mm("ff1", h2, w_ff1_g, "nn", (s // tm, ff // tnf, d // tkd),
                         pl.BlockSpec((tm, tkd), lambda i, j, q: (i, q)),
                         pl.BlockSpec((None, tkd, tnf), lambda i, j, q: (j // rf, q, j % rf)),
                         [jax.ShapeDtypeStruct((s, ff), F32), jax.ShapeDtypeStruct((s, ff), BF16)], [ff_blk, ff_blk],
                         (tm, tnf), relu_sq, comm=[_Gather([w_ff2[0].astype(BF16)])])
    w_ff2_f = w_ff2_g.reshape(ff, d)
    tkf = _pick(ff, MM_TK, 128)
    y2, x3 = _mm("ff2", uu, w_ff2_f, "nn", (s // tm, d // tn, ff // tkf),
                 pl.BlockSpec((tm, tkf), lambda i, j, q: (i, q)), pl.BlockSpec((tkf, tn), lambda i, j, q: (q, j)),
                 [jax.ShapeDtypeStruct((s, d), F32)] * 2, [row_blk, row_blk], (tm, tn), gated_residual,
                 extras=(x2, g2), extra_specs=(row_blk, gate_blk))

    dx3, dy2, loss_p, dgf, dg2 = _loss_head(x3, tgt, y2, final_g, g2)
    loss = lax.psum(loss_p[0, 0], ("x", "y", "c"))

    def relu_sq_bwd(acc, ex, outs):
        outs[0][...] = (acc * (2.0 * ex[0][...])).astype(BF16)

    tnf2 = _pick(ff, MM_TN, 128)
    du = _mm("ff2_dx", dy2, w_ff2_f, "nt", (s // tm, ff // tnf2, d // tkd),
             pl.BlockSpec((tm, tkd), lambda i, j, q: (i, q)), pl.BlockSpec((tnf2, tkd), lambda i, j, q: (j, q)),
             [jax.ShapeDtypeStruct((s, ff), BF16)], [pl.BlockSpec((tm, tnf2), lambda i, j, q: (i, j))],
             (tm, tnf2), relu_sq_bwd, extras=(u,), extra_specs=(pl.BlockSpec((tm, tnf2), lambda i, j, q: (i, j)),))[0]
    gw_ff2 = _mm_plain("ff2_dw", uu, dy2, "tn", ff, d, s, BF16)
    tmd, tks = _pick(d, MM_TM, 128), _pick(s, MM_TK, 128)
    gw_ff2 = gw_ff2.reshape(NDEV, ffs, d)
    dh2, s_ff2 = _mm("ff1_dx", du, w_ff1_g, "nt", (s // tm, d // tn, NDEV // 2),
                     pl.BlockSpec((tm, 2 * ffs), lambda i, j, q: (i, q)),
                     pl.BlockSpec((2, tn, ffs), lambda i, j, q: (q, j, 0)),
                     [jax.ShapeDtypeStruct((s, d), F32)], [row_blk], (tm, tn), _store(F32),
                     comm=[_PairSwap([gw_ff2])], b_parts=2)
    c_ff2 = _pair_sum("pair_ff2", gw_ff2, s_ff2, core)
    gw_ff1, p_ff2 = _mm("ff1_dw", h2, du, "tn", (d // tmd, ff // tnf, s // tks),
                        pl.BlockSpec((tks, tmd), lambda i, j, q: (q, i)), pl.BlockSpec((tks, tnf), lambda i, j, q: (q, j)),
                        [jax.ShapeDtypeStruct((NDEV, d, ffs), BF16)],
                        [pl.BlockSpec((None, tmd, tnf), lambda i, j, q: (j // rf, i, j % rf))], (tmd, tnf), _store(BF16),
                        comm=[_ChipScatter([c_ff2])])
    dx2, dy1, dsc2, dsh2, dgm, dg1, s_ff1 = _norm_mod_bwd("norm2_bwd", x2, dh2, dx3, mlp_norm_g, sc2, y1, g1,
                                                          comm=[_PairSwap([gw_ff1])])
    c_ff1 = _pair_sum("pair_ff1", gw_ff1, s_ff1, core)

    dmix = _mm_plain("out_proj_dx", dy1, w_out_f, "nt", s, MIX, d, F32)
    gw_out = _mm_plain("out_proj_dw", mix, dy1, "tn", MIX, d, s, BF16).reshape(NDEV, MIX // NDEV, d)

    delta_b = _mla_delta(dmix, o_b).reshape(MLA_H, 1, s)
    dq_b, dk_b, dv_b, p_ff1, s_out = _mla_bwd(q_b, k_b, v_b, dmix, delta_b, lse_b.reshape(MLA_H, 1, s),
                                              comm=[_ChipScatter([c_ff1]), _PairSwap([gw_out])])
    c_out = _pair_sum("pair_out", gw_out, s_out, core)
    dcq, dckv, dkr, gw_uq_e, gw_ukv, p_out = _mla_qkv_bwd(dq_b, dk_b, dv_b, cq, ckv, w_uq_g, w_ukv_g, table,
                                                          comm=[_ChipScatter([c_out])])
    dtail, dgq, dgkv = _mla_prep_bwd(proj, dcq, dckv, dkr, mla_q_norm_g, mla_kv_norm_g, table)
    gw_uq = _fold_swapped(gw_uq_e, NOPE, NOPE + ROPE).astype(BF16)
    gw_ukv = gw_ukv.astype(BF16)

    dq_a, dkp, dkc, dvp, dvc, dbias, dsink0, dsink1, s_uq, s_ukv = _swa2_bwd(
        proj_h, dmix, k_top, k_bot, v_top, v_bot, (lse_a0, lse_a1), bias, sinks, comm=[_PairSwap([gw_uq, gw_ukv])])
    c_uq = _pair_sum("pair_uq", gw_uq, s_uq, core)
    c_ukv = _pair_sum("pair_ukv", gw_ukv, s_ukv, core)

    def band_grad(cur, prv):
        g = cur + jnp.concatenate([prv[:, BLOCK:], jnp.zeros_like(prv[:, :BLOCK])], axis=1)
        g = g[:, :, :SWA_DH] + g[:, :, SWA_DH:]
        return jnp.concatenate([g[0], g[1]], axis=1)

    dbias = dbias.reshape(SWA_KV, PAIRS, BLOCK, 2, 2 * BLOCK).transpose(0, 1, 3, 2, 4)
    dsink = jnp.stack([dsink0.reshape(SWA_KV, PAIRS, BLOCK), dsink1.reshape(SWA_KV, PAIRS, BLOCK)], axis=2)
    drel_t, dsinks = _bias_reduce(dbias.reshape(SWA_HEADS, BLOCK * 2 * BLOCK), onehot, dsink.reshape(SWA_HEADS, BLOCK))
    dproj = jnp.concatenate([dq_a, band_grad(dkc, dkp).astype(BF16), band_grad(dvc, dvp).astype(BF16), dtail], axis=1)
    gw_in_e = _mm_plain("proj_dw", h1, dproj, "tn", d, IN_EXT, s, F32, tn=TAIL)
    gw_in = _fold_swapped(gw_in_e, OFF_KR, IN_COLS).reshape(NDEV, d // NDEV, IN_COLS).astype(BF16)
    tkt = IN_EXT
    dh1, s_in, p_uq, p_ukv = _mm(
        "proj_dx", dproj, w_in_e, "nt", (s // tm, d // tn, IN_EXT // tkt),
        pl.BlockSpec((tm, tkt), lambda i, j, q: (i, q)), pl.BlockSpec((tn, tkt), lambda i, j, q: (j, q)),
        [jax.ShapeDtypeStruct((s, d), F32)], [row_blk], (tm, tn), _store(F32),
        comm=[_PairSwap([gw_in]), _ChipScatter([c_uq, c_ukv])])
    c_in = _pair_sum("pair_in", gw_in, s_in, core)
    gx, dsc1, dsh1, dga, p_in = _norm_mod_bwd("norm1_bwd", x2d, dh1, dx2, attn_norm_g, sc1,
                                              comm=[_ChipScatter([c_in])])

    small = [jnp.concatenate([dsh1, dsc1, dg1, dsh2, dsc2, dg2], axis=1), dga, dgm, dgf, dgq, dgkv,
             dsinks.reshape(1, SWA_HEADS), drel_t.T.reshape(1, REL_BUCKETS * SWA_HEADS)]
    n_small = sum(a.shape[1] for a in small)
    n_pad = -n_small % 1024
    rows_small = (n_small + n_pad) // 128
    pad = jnp.zeros((1, n_pad), F32)
    pack = lambda parts: jnp.concatenate([p.reshape(1, -1) for p in parts] + [pad], axis=1).reshape(rows_small, 128)
    (small_g,) = _exchange("gather_small", _Gather([pack(small)]))
    small_names = (b_mod, attn_norm_g, mlp_norm_g, final_norm_g, mla_q_norm_g, mla_kv_norm_g, swa_sinks, rel_bias)
    small_m = (m_b_mod, m_attn_norm_g, m_mlp_norm_g, m_final_norm_g, m_mla_q_norm_g, m_mla_kv_norm_g, m_swa_sinks, m_rel_bias)
    small_v = (v_b_mod, v_attn_norm_g, v_mlp_norm_g, v_final_norm_g, v_mla_q_norm_g, v_mla_kv_norm_g, v_swa_sinks, v_rel_bias)
    small_out = _adamw("adamw_small", pack(small_names), small_g, pack(small_m), pack(small_v), parts=True)

    def unpack(flat):
        flat = flat.reshape(1, -1)
        out, off = [], 0
        for a in small_names:
            out.append(flat[:, off:off + a.size].reshape(a.shape))
            off += a.size
        return out

    sg, sd, sm, sv = [unpack(o) for o in small_out]

    dmod_cols = lax.dynamic_slice(small_g.reshape(NDEV, -1), (0, me * nmod), (NDEV, nmod))
    gw_mod = _mod_wgrad(act_all, dmod_cols)
    big = {"w_mod": _adamw("adamw_w_mod", w_mod[0], gw_mod, m_w_mod[0], v_w_mod[0], parts=False)}

    for name, w, p, m, v in (("w_in", w_in, p_in, m_w_in, v_w_in), ("w_uq", w_uq, p_uq, m_w_uq, v_w_uq),
                             ("w_ukv", w_ukv, p_ukv, m_w_ukv, v_w_ukv), ("w_out", w_out, p_out, m_w_out, v_w_out),
                             ("w_ff1", w_ff1, p_ff1, m_w_ff1, v_w_ff1), ("w_ff2", w_ff2, p_ff2, m_w_ff2, v_w_ff2)):
        big[name] = _adamw("adamw_" + name, w[0], p, m[0], v[0], parts=True)

    order = ("w_mod", "b_mod", "attn_norm_g", "w_in", "swa_sinks", "rel_bias", "mla_q_norm_g", "w_uq", "mla_kv_norm_g",
             "w_ukv", "w_out", "mlp_norm_g", "w_ff1", "w_ff2", "final_norm_g")
    small_idx = {"b_mod": 0, "attn_norm_g": 1, "mlp_norm_g": 2, "final_norm_g": 3, "mla_q_norm_g": 4,
                 "mla_kv_norm_g": 5, "swa_sinks": 6, "rel_bias": 7}
    outs = []
    for kind, small_list in enumerate((sg, sd, sm, sv)):
        for name in order:
            outs.append(small_list[small_idx[name]] if name in small_idx else big[name][kind][None])
    return (loss, gx[None], *outs)
```

```python
import functools
import math

import jax
import jax.numpy as jnp
from jax import lax
from jax.experimental import pallas as pl
from jax.experimental.pallas import tpu as pltpu

F32 = jnp.float32
BF16 = jnp.bfloat16

NDEV = 8
EPS = 1e-6
BLOCK = 128
SWA_HEADS, SWA_KV, SWA_DH, SWA_GROUP = 16, 2, 64, 8
REL_BUCKETS, REL_MAX_DIST = 32, 128
MLA_H, Q_RANK, KV_RANK, NOPE, ROPE, VDIM = 8, 384, 128, 128, 64, 128
ROPE_THETA = 10000.0
OFF_K, OFF_V, OFF_CQ, OFF_CKV, OFF_KR, IN_COLS = 1024, 1152, 1280, 1664, 1792, 1856
IN_EXT = IN_COLS + ROPE
TAIL0, TAIL = OFF_CQ, IN_EXT - OFF_CQ
QW = NOPE + 2 * ROPE
MIX = SWA_HEADS * SWA_DH + MLA_H * VDIM
MLA_SCALE = (NOPE + ROPE) ** -0.5
SWA_SCALE = SWA_DH ** -0.5

ADAM_LR, ADAM_B1, ADAM_B2, ADAM_EPS, ADAM_WD, ADAM_STEP = 0.001, 0.9, 0.999, 1e-08, 0.01, 10

VMEM_LIMIT = 52 * 1024 * 1024
ROW_TILE = 256
MM_TM, MM_TN, MM_TK = 1024, 1024, 2048
ATT_T = 512
MLA_HB = 2
ADAM_ELEMS = 256 * 1024


MESH_ID = pl.DeviceIdType.MESH


def _place():
    x, y, c = lax.axis_index("x"), lax.axis_index("y"), lax.axis_index("c")
    return x, y, c, 2 * x + y


def _chip(x, y, k):
    return (1 - x if k & 2 else x, 1 - y if k & 1 else y)


def _dma_sems(*counts):
    return [pltpu.SemaphoreType.DMA((n,)) for n in counts]


class _Gather:
    def __init__(self, arrays):
        self.arrays = list(arrays)
        n = len(self.arrays)
        self.out_shape = [jax.ShapeDtypeStruct((NDEV,) + a.shape, a.dtype) for a in self.arrays]
        self.sems = _dma_sems(7 * n, 7 * n, n)

    def _copy(self, sems, a, k, src, dst, to):
        return pltpu.make_async_remote_copy(src_ref=src, dst_ref=dst, send_sem=sems[0].at[7 * a + k],
                                            recv_sem=sems[1].at[7 * a + k], device_id=to, device_id_type=MESH_ID)

    def start(self, ins, outs, sems):
        x, y, c, q = _place()
        me = 2 * q + c
        for a in range(len(ins)):
            pltpu.make_async_copy(ins[a], outs[a].at[me], sems[2].at[a]).start()
            self._copy(sems, a, 0, ins[a], outs[a].at[me], (x, y, 1 - c)).start()
            for k in (1, 2, 3):
                self._copy(sems, a, k, ins[a], outs[a].at[me], (*_chip(x, y, k), c)).start()

    def finish(self, ins, outs, sems):
        x, y, c, q = _place()
        me, sib = 2 * q + c, (x, y, 1 - c)
        n = len(ins)
        for k in (1, 2, 3):
            for a in range(n):
                blk = outs[a].at[2 * (q ^ k) + c]
                self._copy(sems, a, k, ins[a], blk, (*_chip(x, y, k), c)).wait_recv()
                self._copy(sems, a, 3 + k, blk, blk, sib).start()
        for a in range(n):
            self._copy(sems, a, 0, ins[a], outs[a].at[2 * q + 1 - c], sib).wait_recv()
            for k in (1, 2, 3):
                blk = outs[a].at[2 * (q ^ k) + 1 - c]
                self._copy(sems, a, 3 + k, blk, blk, sib).wait_recv()
        for a in range(n):
            for k in range(7):
                self._copy(sems, a, k, ins[a], outs[a].at[me], sib).wait_send()
            pltpu.make_async_copy(ins[a], outs[a].at[me], sems[2].at[a]).wait()


class _PairSwap:
    def __init__(self, arrays):
        self.arrays = list(arrays)
        n = len(self.arrays)
        self.out_shape = [jax.ShapeDtypeStruct((NDEV // 2,) + a.shape[1:], a.dtype) for a in self.arrays]
        self.sems = _dma_sems(4 * n, 4 * n)

    def _copy(self, sems, a, p, src, dst, to):
        return pltpu.make_async_remote_copy(src_ref=src, dst_ref=dst, send_sem=sems[0].at[4 * a + p],
                                            recv_sem=sems[1].at[4 * a + p], device_id=to, device_id_type=MESH_ID)

    def start(self, ins, outs, sems):
        x, y, c, _ = _place()
        for a in range(len(ins)):
            for p in range(4):
                self._copy(sems, a, p, ins[a].at[2 * p + 1 - c], outs[a].at[p], (x, y, 1 - c)).start()

    def finish(self, ins, outs, sems):
        x, y, c, _ = _place()
        for a in range(len(ins)):
            for p in range(4):
                cp = self._copy(sems, a, p, ins[a].at[2 * p + 1 - c], outs[a].at[p], (x, y, 1 - c))
                cp.wait_recv()
                cp.wait_send()


class _ChipScatter:
    def __init__(self, arrays):
        self.arrays = list(arrays)
        n = len(self.arrays)
        self.out_shape = [jax.ShapeDtypeStruct(a.shape, a.dtype) for a in self.arrays]
        self.sems = _dma_sems(3 * n, 3 * n, n)

    def _copy(self, sems, a, k, src, dst, to):
        return pltpu.make_async_remote_copy(src_ref=src, dst_ref=dst, send_sem=sems[0].at[3 * a + k - 1],
                                            recv_sem=sems[1].at[3 * a + k - 1], device_id=to, device_id_type=MESH_ID)

    def start(self, ins, outs, sems):
        x, y, c, q = _place()
        for a in range(len(ins)):
            pltpu.make_async_copy(ins[a].at[q], outs[a].at[q], sems[2].at[a]).start()
            for k in (1, 2, 3):
                self._copy(sems, a, k, ins[a].at[q ^ k], outs[a].at[q], (*_chip(x, y, k), c)).start()

    def finish(self, ins, outs, sems):
        x, y, c, q = _place()
        for a in range(len(ins)):
            for k in (1, 2, 3):
                cp = self._copy(sems, a, k, ins[a].at[q ^ k], outs[a].at[q ^ k], (*_chip(x, y, k), c))
                cp.wait_recv()
                cp.wait_send()
            pltpu.make_async_copy(ins[a].at[q], outs[a].at[q], sems[2].at[a]).wait()


def _call(body, **kw):
    return pl.pallas_call(body, **kw)


def _pcall(body, comm=None, **kw):
    if not comm:
        return _call(body, **kw)
    grid = kw["grid"]
    in_specs, out_specs, out_shape = list(kw["in_specs"]), list(kw["out_specs"]), list(kw["out_shape"])
    scratch = list(kw.get("scratch_shapes", ()))
    n_in, n_out, n_scr = len(in_specs), len(out_shape), len(scratch)
    n_cin = [len(j.arrays) for j in comm]
    n_sem = [len(j.sems) for j in comm]
    n = sum(n_cin)
    hbm = pl.BlockSpec(memory_space=pltpu.HBM)

    def carried(*refs):
        ins, cins = refs[:n_in], refs[n_in:n_in + n]
        outs, couts = refs[n_in + n:n_in + n + n_out], refs[n_in + n + n_out:n_in + 2 * n + n_out]
        scr, sems = refs[n_in + 2 * n + n_out:n_in + 2 * n + n_out + n_scr], refs[n_in + 2 * n + n_out + n_scr:]
        ids = [pl.program_id(ax) for ax in range(len(grid))]
        first = functools.reduce(jnp.logical_and, [i == 0 for i in ids])
        last = functools.reduce(jnp.logical_and, [i == g - 1 for i, g in zip(ids, grid)])

        def each(method):
            ai = si = 0
            for job, na, ns in zip(comm, n_cin, n_sem):
                getattr(job, method)(cins[ai:ai + na], couts[ai:ai + na], sems[si:si + ns])
                ai, si = ai + na, si + ns

        @pl.when(first)
        def _():
            each("start")

        body(*ins, *outs, *scr)

        @pl.when(last)
        def _():
            each("finish")

    kw.update(in_specs=in_specs + [hbm] * n, out_specs=out_specs + [hbm] * n,
              out_shape=out_shape + [o for j in comm for o in j.out_shape],
              scratch_shapes=scratch + [sm for j in comm for sm in j.sems],
              compiler_params=_cparams(("arbitrary",) * len(grid)))
    call = _call(carried, **kw)
    return lambda *args: call(*args, *[a for j in comm for a in j.arrays])


def _cparams(sem):
    return pltpu.CompilerParams(dimension_semantics=sem, vmem_limit_bytes=VMEM_LIMIT)


def _pick(n, pref, align):
    if n <= pref:
        return n
    t = (pref // align) * align
    while t >= align:
        if n % t == 0:
            return t
        t -= align
    return n


def _split3(x):
    a = x.astype(BF16)
    r = x - a.astype(F32)
    b = r.astype(BF16)
    c = (r - b.astype(F32)).astype(BF16)
    return a, b, c


def _exchange(name, job):
    n = len(job.arrays)

    def body(*refs):
        ins, outs, sems = refs[:n], refs[n:2 * n], refs[2 * n:]
        job.start(ins, outs, sems)
        job.finish(ins, outs, sems)

    hbm = pl.BlockSpec(memory_space=pltpu.HBM)
    return _call(body, name=name, out_shape=job.out_shape, in_specs=[hbm] * n, out_specs=[hbm] * n,
                 scratch_shapes=job.sems)(*job.arrays)


def _pair_sum(name, g, r, core):
    _, rr, cc = g.shape
    tr = rr if rr * cc <= 4 * ADAM_ELEMS else _pick(rr, max(16, 4 * ADAM_ELEMS // cc // 16 * 16), 16)

    def body(g_ref, r_ref, c_ref, o_ref):
        north = c_ref[:, 0:1] > 0.5
        mine = jnp.where(north, g_ref[1].astype(F32), g_ref[0].astype(F32))
        o_ref[...] = (mine + r_ref[...].astype(F32)).astype(o_ref.dtype)

    return _pcall(
        body, name=name, grid=(NDEV // 2, rr // tr),
        in_specs=[pl.BlockSpec((None, 2, tr, cc), lambda p, i: (p, 0, i, 0)),
                  pl.BlockSpec((None, tr, cc), lambda p, i: (p, i, 0)), pl.BlockSpec((1, 128), lambda p, i: (0, 0))],
        out_specs=pl.BlockSpec((None, tr, cc), lambda p, i: (p, i, 0)),
        out_shape=jax.ShapeDtypeStruct((NDEV // 2, rr, cc), g.dtype),
        compiler_params=_cparams(("parallel", "parallel")))(g.reshape(NDEV // 2, 2, rr, cc), r, core)


_DIMS = {"nn": (((1,), (0,)), ((), ())), "nt": (((1,), (1,)), ((), ())), "tn": (((0,), (0,)), ((), ()))}


def _mm(name, a, b, kind, grid, a_spec, b_spec, out_shape, out_specs, acc_shape, epilogue,
        extras=(), extra_specs=(), comm=None, b_parts=1):
    nk, ne, no = grid[2], len(extras), len(out_shape)

    def body(*refs):
        a_ref, b_ref = refs[0], refs[1]
        ex, outs = refs[2:2 + ne], refs[2 + ne:2 + ne + no]
        if b_parts == 1:
            part = lax.dot_general(a_ref[...].astype(BF16), b_ref[...].astype(BF16), _DIMS[kind],
                                   preferred_element_type=F32)
        else:
            kp = a_ref.shape[1] // b_parts
            part = sum(lax.dot_general(a_ref[:, p * kp:(p + 1) * kp].astype(BF16), b_ref[p].astype(BF16), _DIMS[kind],
                                       preferred_element_type=F32) for p in range(b_parts))
        if nk == 1:
            epilogue(part, ex, outs)
            return
        acc = refs[-1]
        k = pl.program_id(2)

        @pl.when(k == 0)
        def _():
            acc[...] = part

        @pl.when(jnp.logical_and(k > 0, k < nk - 1))
        def _():
            acc[...] += part

        @pl.when(k == nk - 1)
        def _():
            epilogue(acc[...] + part, ex, outs)

    return _pcall(
        body, comm=comm, name=name, grid=grid, in_specs=[a_spec, b_spec, *extra_specs], out_specs=out_specs,
        out_shape=out_shape, scratch_shapes=[pltpu.VMEM(acc_shape, F32)] if nk > 1 else [],
        compiler_params=_cparams(("parallel", "parallel", "arbitrary")),
    )(a, b, *extras)


def _store(dtype):
    def epi(acc, ex, outs):
        outs[0][...] = acc.astype(dtype)
    return epi


def _mm_plain(name, a, b, kind, m, n, k, out_dtype, tm=None, tn=None, tk=None):
    tm = _pick(m, tm or MM_TM, 128)
    tn = _pick(n, tn or MM_TN, 128)
    tk = _pick(k, tk or MM_TK, 128)
    a_spec = pl.BlockSpec((tk, tm), lambda i, j, q: (q, i)) if kind == "tn" else pl.BlockSpec((tm, tk), lambda i, j, q: (i, q))
    b_spec = pl.BlockSpec((tn, tk), lambda i, j, q: (j, q)) if kind == "nt" else pl.BlockSpec((tk, tn), lambda i, j, q: (q, j))
    return _mm(name, a, b, kind, (m // tm, n // tn, k // tk), a_spec, b_spec,
               [jax.ShapeDtypeStruct((m, n), out_dtype)], [pl.BlockSpec((tm, tn), lambda i, j, q: (i, j))],
               (tm, tn), _store(out_dtype))[0]


def _row(ts, d):
    return pl.BlockSpec((ts, d), lambda i: (i, 0))


def _vec(d):
    return pl.BlockSpec((1, d), lambda i: (0, 0))


def _norm_mod(name, x, gain, sc, sh, comm=None):
    s, d = x.shape
    ts = _pick(s, ROW_TILE, 16)

    def body(x_ref, g_ref, sc_ref, sh_ref, h_ref):
        xv = x_ref[...]
        r = lax.rsqrt(jnp.mean(xv * xv, axis=-1, keepdims=True) + EPS)
        h_ref[...] = ((xv * r) * g_ref[...] * (1.0 + sc_ref[...]) + sh_ref[...]).astype(BF16)

    return _pcall(body, comm=comm, name=name, grid=(s // ts,), in_specs=[_row(ts, d), _vec(d), _vec(d), _vec(d)],
                  out_specs=[_row(ts, d)], out_shape=[jax.ShapeDtypeStruct((s, d), BF16)],
                  compiler_params=_cparams(("parallel",)))(x, gain, sc, sh)


def _loss_head(x3, tgt, y2, gf, g2):
    s, d = x3.shape
    ts = _pick(s, ROW_TILE, 16)

    def body(x_ref, t_ref, y_ref, gf_ref, g2_ref, dx_ref, dy_ref, loss_ref, dgf_ref, dg2_ref):
        @pl.when(pl.program_id(0) == 0)
        def _():
            loss_ref[...] = jnp.zeros_like(loss_ref)
            dgf_ref[...] = jnp.zeros_like(dgf_ref)
            dg2_ref[...] = jnp.zeros_like(dg2_ref)

        xv = x_ref[...]
        r = lax.rsqrt(jnp.mean(xv * xv, axis=-1, keepdims=True) + EPS)
        xn = xv * r
        err = xn * gf_ref[...] - t_ref[...]
        loss_ref[...] += 0.5 * jnp.sum(jnp.mean(err * err, axis=-1, keepdims=True), axis=0, keepdims=True)
        dout = err * (1.0 / d)
        dgf_ref[...] += jnp.sum(dout * xn, axis=0, keepdims=True)
        dxn = dout * gf_ref[...]
        dx = r * (dxn - xn * jnp.mean(dxn * xn, axis=-1, keepdims=True))
        dx_ref[...] = dx
        dy_ref[...] = (dx * g2_ref[...]).astype(BF16)
        dg2_ref[...] += jnp.sum(dx * y_ref[...], axis=0, keepdims=True)

    one = pl.BlockSpec((1, 1), lambda i: (0, 0))
    return _pcall(
        body, name="loss_head", grid=(s // ts,),
        in_specs=[_row(ts, d), _row(ts, d), _row(ts, d), _vec(d), _vec(d)],
        out_specs=[_row(ts, d), _row(ts, d), one, _vec(d), _vec(d)],
        out_shape=[jax.ShapeDtypeStruct((s, d), F32), jax.ShapeDtypeStruct((s, d), BF16),
                   jax.ShapeDtypeStruct((1, 1), F32), jax.ShapeDtypeStruct((1, d), F32),
                   jax.ShapeDtypeStruct((1, d), F32)],
        compiler_params=_cparams(("arbitrary",)))(x3, tgt, y2, gf, g2)


def _norm_mod_bwd(name, x, dh, dres, gain, sc, y_prev=None, gate=None, comm=None):
    s, d = x.shape
    ts = _pick(s, ROW_TILE, 16)
    gated = y_prev is not None

    def body(*refs):
        if gated:
            x_ref, dh_ref, dr_ref, g_ref, sc_ref, y_ref, gt_ref, dx_ref, dy_ref, dsc_ref, dsh_ref, dg_ref, dgt_ref = refs
        else:
            x_ref, dh_ref, dr_ref, g_ref, sc_ref, dx_ref, dsc_ref, dsh_ref, dg_ref = refs

        @pl.when(pl.program_id(0) == 0)
        def _():
            dsc_ref[...] = jnp.zeros_like(dsc_ref)
            dsh_ref[...] = jnp.zeros_like(dsh_ref)
            dg_ref[...] = jnp.zeros_like(dg_ref)
            if gated:
                dgt_ref[...] = jnp.zeros_like(dgt_ref)

        xv, dhv = x_ref[...], dh_ref[...]
        r = lax.rsqrt(jnp.mean(xv * xv, axis=-1, keepdims=True) + EPS)
        xn = xv * r
        dsc_ref[...] += jnp.sum(dhv * (xn * g_ref[...]), axis=0, keepdims=True)
        dsh_ref[...] += jnp.sum(dhv, axis=0, keepdims=True)
        da = dhv * (1.0 + sc_ref[...])
        dg_ref[...] += jnp.sum(da * xn, axis=0, keepdims=True)
        dxn = da * g_ref[...]
        dx = dr_ref[...] + r * (dxn - xn * jnp.mean(dxn * xn, axis=-1, keepdims=True))
        dx_ref[...] = dx
        if gated:
            dy_ref[...] = (dx * gt_ref[...]).astype(BF16)
            dgt_ref[...] += jnp.sum(dx * y_ref[...], axis=0, keepdims=True)

    ins = [x, dh, dres, gain, sc] + ([y_prev, gate] if gated else [])
    in_specs = [_row(ts, d)] * 3 + [_vec(d)] * 2 + ([_row(ts, d), _vec(d)] if gated else [])
    vec_out = jax.ShapeDtypeStruct((1, d), F32)
    out_shape = [jax.ShapeDtypeStruct((s, d), F32)] + ([jax.ShapeDtypeStruct((s, d), BF16)] if gated else [])
    out_shape += [vec_out] * (4 if gated else 3)
    out_specs = [_row(ts, d)] * (2 if gated else 1) + [_vec(d)] * (4 if gated else 3)
    return _pcall(body, comm=comm, name=name, grid=(s // ts,), in_specs=in_specs, out_specs=out_specs,
                  out_shape=out_shape, compiler_params=_cparams(("arbitrary",)))(*ins)


def _dot3(a, b, dims):
    a1, a2, _ = _split3(a)
    b1, b2, _ = _split3(b)
    dot = functools.partial(lax.dot_general, dimension_numbers=dims, preferred_element_type=F32)
    return dot(a1, b1) + (dot(a1, b2) + dot(a2, b1))


def _mod_fwd(c_all, w, b_cols, comm=None):
    nb, d = c_all.shape
    n = w.shape[1]
    tk = _pick(d, 512, 128)
    nk = d // tk

    def body(c_ref, w_ref, b_ref, act_ref, out_ref):
        k = pl.program_id(0)
        cv = c_ref[...]
        act = cv * (1.0 / (1.0 + jnp.exp(-cv)))
        act_ref[...] = act

        @pl.when(k == 0)
        def _():
            out_ref[...] = jnp.broadcast_to(b_ref[...], out_ref.shape)

        out_ref[...] += _dot3(act, w_ref[...], _DIMS["nn"])

    return _pcall(
        body, comm=comm, name="mod_fwd", grid=(nk,),
        in_specs=[pl.BlockSpec((nb, tk), lambda k: (0, k)), pl.BlockSpec((tk, n), lambda k: (k, 0)),
                  pl.BlockSpec((1, n), lambda k: (0, 0))],
        out_specs=[pl.BlockSpec((nb, tk), lambda k: (0, k)), pl.BlockSpec((nb, n), lambda k: (0, 0))],
        out_shape=[jax.ShapeDtypeStruct((nb, d), F32), jax.ShapeDtypeStruct((nb, n), F32)],
        compiler_params=_cparams(("arbitrary",)))(c_all, w, b_cols)


def _mod_wgrad(act_all, dmod_cols):
    nb, d = act_all.shape
    n = dmod_cols.shape[1]
    tm = _pick(d, 512, 128)

    def body(a_ref, d_ref, o_ref):
        o_ref[...] = _dot3(a_ref[...], d_ref[...], _DIMS["tn"])

    return _pcall(
        body, name="mod_wgrad", grid=(d // tm,),
        in_specs=[pl.BlockSpec((nb, tm), lambda i: (0, i)), pl.BlockSpec((nb, n), lambda i: (0, 0))],
        out_specs=pl.BlockSpec((tm, n), lambda i: (i, 0)), out_shape=jax.ShapeDtypeStruct((d, n), F32),
        compiler_params=_cparams(("parallel",)))(act_all, dmod_cols)


def _bias_expand(rel_t, onehot_t):
    h, _ = rel_t.shape
    n = onehot_t.shape[1]

    def body(r_ref, o_ref, out_ref):
        a, b, c = _split3(r_ref[...])
        dot = functools.partial(lax.dot_general, dimension_numbers=_DIMS["nn"], preferred_element_type=F32)
        oh = o_ref[...]
        out_ref[...] = dot(a, oh) + (dot(b, oh) + dot(c, oh))

    full = lambda shp: pl.BlockSpec(shp, lambda: (0,) * len(shp))
    return _pcall(body, name="bias_expand", in_specs=[full(rel_t.shape), full(onehot_t.shape)],
                  out_specs=full((h, n)), out_shape=jax.ShapeDtypeStruct((h, n), F32),
                  compiler_params=pltpu.CompilerParams(vmem_limit_bytes=VMEM_LIMIT))(rel_t, onehot_t)


def _bias_reduce(dbias, onehot, dsink_rows):
    h, n = dbias.shape

    def body(d_ref, o_ref, s_ref, out_ref, so_ref):
        a, b, c = _split3(d_ref[...])
        dot = functools.partial(lax.dot_general, dimension_numbers=_DIMS["nn"], preferred_element_type=F32)
        oh = o_ref[...]
        out_ref[...] = dot(a, oh) + (dot(b, oh) + dot(c, oh))
        so_ref[...] = jnp.sum(s_ref[...], axis=-1, keepdims=True)

    full = lambda shp: pl.BlockSpec(shp, lambda: (0,) * len(shp))
    return _pcall(body, name="bias_reduce", in_specs=[full(dbias.shape), full(onehot.shape), full(dsink_rows.shape)],
                  out_specs=[full((h, REL_BUCKETS)), full((h, 1))],
                  out_shape=[jax.ShapeDtypeStruct((h, REL_BUCKETS), F32), jax.ShapeDtypeStruct((h, 1), F32)],
                  compiler_params=pltpu.CompilerParams(vmem_limit_bytes=VMEM_LIMIT))(dbias, onehot, dsink_rows)


PAIRS = SWA_GROUP // 2
PROWS = PAIRS * BLOCK
PCOLS = 2 * 2 * BLOCK


def _swa2_specs():
    tok = lambda width: pl.BlockSpec((BLOCK, width), lambda g, n: (n, g))
    prev = pl.BlockSpec((None, BLOCK, 2 * SWA_DH), lambda g, n: (g, jnp.maximum(n - 1, 0), 0))
    cur = pl.BlockSpec((None, BLOCK, 2 * SWA_DH), lambda g, n: (g, n, 0))
    bias_spec = pl.BlockSpec((None, PROWS, PCOLS), lambda g, n: (g, 0, 0))
    col_spec = pl.BlockSpec((None, PROWS, 1), lambda g, n: (g, 0, 0))
    lse_spec = pl.BlockSpec((None, None, PROWS, 1), lambda g, n: (g, n, 0, 0))
    return tok, prev, cur, bias_spec, col_spec, lse_spec


def _stack_pairs(blk):
    return jnp.concatenate([blk[:, p * 2 * SWA_DH:(p + 1) * 2 * SWA_DH] for p in range(PAIRS)], axis=0)


def _band(tp, tc, bp, bc):
    return jnp.concatenate([tp[...], tc[...], bp[...], bc[...]], axis=0)


def _swa2_scores(q_ref, kd, bias_ref, n):
    q2 = _stack_pairs(q_ref[...])
    s2 = lax.dot_general(q2, kd, _DIMS["nt"], preferred_element_type=F32) * SWA_SCALE + bias_ref[...]
    col = lax.broadcasted_iota(jnp.int32, s2.shape, 1)
    before_start = jnp.logical_and(n == 0, (col & (2 * BLOCK - 1)) < BLOCK)
    return q2, jnp.where(before_start, -jnp.inf, s2)


def _swa2_fwd(src, ktop, kbot, vtop, vbot, bias, sinks, comm=None):
    s = src.shape[0]
    nb = s // BLOCK
    tok, prev, cur, bias_spec, col_spec, lse_spec = _swa2_specs()

    def body(q_ref, ktp, ktc, kbp, kbc, vtp, vtc, vbp, vbc, bias_ref, sa_ref, sb_ref, o_ref, la_ref, lb_ref):
        n = pl.program_id(1)
        _, s2 = _swa2_scores(q_ref, _band(ktp, ktc, kbp, kbc), bias_ref, n)
        row = lax.broadcasted_iota(jnp.int32, (PCOLS, 2 * SWA_DH), 0)
        lane = lax.broadcasted_iota(jnp.int32, (PCOLS, 2 * SWA_DH), 1)
        ones = jnp.where(lane == row // (2 * BLOCK), 1.0, 0.0).astype(BF16)
        ps, ms, sinks_ = [], [], []
        for half, sink_ref in enumerate((sa_ref, sb_ref)):
            sc = s2[:, half * 2 * BLOCK:(half + 1) * 2 * BLOCK]
            m = jnp.maximum(jnp.max(sc, axis=-1, keepdims=True), sink_ref[...])
            ps.append(jnp.exp(sc - m).astype(BF16))
            ms.append(m)
        acc = lax.dot_general(jnp.concatenate(ps, axis=1), jnp.concatenate([_band(vtp, vtc, vbp, vbc), ones], axis=1),
                              _DIMS["nn"], preferred_element_type=F32)
        dens = []
        for half, (sink_ref, lse_ref) in enumerate(((sa_ref, la_ref), (sb_ref, lb_ref))):
            den = acc[:, 2 * SWA_DH + half:2 * SWA_DH + half + 1] + jnp.exp(sink_ref[...] - ms[half])
            lse_ref[...] = ms[half] + jnp.log(den)
            dens.append(den)
        lo = lax.broadcasted_iota(jnp.int32, (PROWS, 2 * SWA_DH), 1) < SWA_DH
        o2 = acc[:, 0:2 * SWA_DH] / jnp.where(lo, dens[0], dens[1])
        for p in range(PAIRS):
            o_ref[:, p * 2 * SWA_DH:(p + 1) * 2 * SWA_DH] = o2[p * BLOCK:(p + 1) * BLOCK].astype(BF16)

    lse_shape = jax.ShapeDtypeStruct((SWA_KV, nb, PROWS, 1), F32)
    return _pcall(
        body, comm=comm, name="swa_fwd", grid=(SWA_KV, nb),
        in_specs=[tok(PROWS), prev, cur, prev, cur, prev, cur, prev, cur, bias_spec, col_spec, col_spec],
        out_specs=[tok(PROWS), lse_spec, lse_spec],
        out_shape=[jax.ShapeDtypeStruct((s, SWA_HEADS * SWA_DH), BF16), lse_shape, lse_shape],
        compiler_params=_cparams(("parallel", "parallel")))(
            src, ktop, ktop, kbot, kbot, vtop, vtop, vbot, vbot, bias, sinks[0], sinks[1])


def _swa2_bwd(src, dsrc, ktop, kbot, vtop, vbot, lses, bias, sinks, comm=None):
    s = src.shape[0]
    nb = s // BLOCK
    tok, prev, cur, bias_spec, col_spec, lse_spec = _swa2_specs()
    lane_lo = lambda shape: lax.broadcasted_iota(jnp.int32, shape, 1) < SWA_DH

    def body(q_ref, do_ref, ktp, ktc, kbp, kbc, vtp, vtc, vbp, vbc, la_ref, lb_ref, bias_ref, sa_ref, sb_ref,
             dq_ref, dkp_ref, dkc_ref, dvp_ref, dvc_ref, dbias_ref, dsa_ref, dsb_ref):
        n = pl.program_id(1)

        @pl.when(n == 0)
        def _():
            dbias_ref[...] = jnp.zeros_like(dbias_ref)
            dsa_ref[...] = jnp.zeros_like(dsa_ref)
            dsb_ref[...] = jnp.zeros_like(dsb_ref)

        kd = _band(ktp, ktc, kbp, kbc)
        q2, s2 = _swa2_scores(q_ref, kd, bias_ref, n)
        do2 = _stack_pairs(do_ref[...]).astype(BF16)
        dp2 = lax.dot_general(do2, _band(vtp, vtc, vbp, vbc), _DIMS["nt"], preferred_element_type=F32)
        ps, dss = [], []
        for half, (sink_ref, lse_ref, dsink_ref) in enumerate(((sa_ref, la_ref, dsa_ref), (sb_ref, lb_ref, dsb_ref))):
            cols = slice(half * 2 * BLOCK, (half + 1) * 2 * BLOCK)
            lse_v = lse_ref[...]
            p = jnp.exp(s2[:, cols] - lse_v)
            dp = dp2[:, cols]
            delta = jnp.sum(p * dp, axis=-1, keepdims=True)
            ds = p * (dp - delta)
            dsink_ref[...] += -jnp.exp(sink_ref[...] - lse_v) * delta
            ps.append(p.astype(BF16))
            dss.append(ds)
        ds2 = jnp.concatenate(dss, axis=1)
        dbias_ref[...] += ds2
        dsb2 = (ds2 * SWA_SCALE).astype(BF16)
        dq2 = lax.dot_general(dsb2, kd, _DIMS["nn"], preferred_element_type=F32)
        for p in range(PAIRS):
            dq_ref[:, p * 2 * SWA_DH:(p + 1) * 2 * SWA_DH] = dq2[p * BLOCK:(p + 1) * BLOCK].astype(BF16)
        dk = lax.dot_general(dsb2, q2, _DIMS["tn"], preferred_element_type=F32)
        dv = lax.dot_general(jnp.concatenate(ps, axis=1), do2, _DIMS["tn"], preferred_element_type=F32)
        for full, prev_ref, cur_ref in ((dk, dkp_ref, dkc_ref), (dv, dvp_ref, dvc_ref)):
            own = jnp.where(lane_lo((2 * BLOCK, 2 * SWA_DH)), full[:2 * BLOCK], full[2 * BLOCK:])
            prev_ref[...] = own[:BLOCK]
            cur_ref[...] = own[BLOCK:]

    kv_out = jax.ShapeDtypeStruct((SWA_KV, s, 2 * SWA_DH), F32)
    col_out = jax.ShapeDtypeStruct((SWA_KV, PROWS, 1), F32)
    return _pcall(
        body, comm=comm, name="swa_bwd", grid=(SWA_KV, nb),
        in_specs=[tok(PROWS), tok(PROWS), prev, cur, prev, cur, prev, cur, prev, cur, lse_spec, lse_spec, bias_spec,
                  col_spec, col_spec],
        out_specs=[tok(PROWS), cur, cur, cur, cur, bias_spec, col_spec, col_spec],
        out_shape=[jax.ShapeDtypeStruct((s, SWA_HEADS * SWA_DH), BF16), kv_out, kv_out, kv_out, kv_out,
                   jax.ShapeDtypeStruct(bias.shape, F32), col_out, col_out],
        compiler_params=_cparams(("arbitrary", "arbitrary")))(
            src, dsrc, ktop, ktop, kbot, kbot, vtop, vtop, vbot, vbot, lses[0], lses[1], bias, sinks[0], sinks[1])


def _rope_slab(slab, table):
    t = slab * table
    return t + pltpu.roll(t, ROPE, 1)


def _low_lanes(v):
    lane = lax.broadcasted_iota(jnp.int32, v.shape, 1)
    return jnp.where(lane < ROPE, v, 0.0)


def _rms(xv, g):
    r = lax.rsqrt(jnp.mean(xv * xv, axis=-1, keepdims=True) + EPS)
    return xv * r, r


def _mla_prep(proj, gq, gkv, table):
    s = proj.shape[0]
    ts = _pick(s, ROW_TILE, 16)

    def body(p_ref, gq_ref, gkv_ref, t_ref, cq_ref, ckv_ref, kr_ref):
        xq, _ = _rms(p_ref[:, 0:Q_RANK], None)
        cq_ref[...] = (xq * gq_ref[...]).astype(BF16)
        xkv, _ = _rms(p_ref[:, Q_RANK:Q_RANK + KV_RANK], None)
        ckv_ref[...] = (xkv * gkv_ref[...]).astype(BF16)
        kr_ref[...] = _low_lanes(_rope_slab(p_ref[:, Q_RANK + KV_RANK:TAIL], t_ref[...]))

    return _pcall(
        body, name="mla_prep", grid=(s // ts,),
        in_specs=[pl.BlockSpec((ts, TAIL), lambda i: (i, TAIL0 // TAIL)), _vec(Q_RANK), _vec(KV_RANK), _row(ts, 2 * ROPE)],
        out_specs=[_row(ts, Q_RANK), _row(ts, KV_RANK), _row(ts, 2 * ROPE)],
        out_shape=[jax.ShapeDtypeStruct((s, Q_RANK), BF16), jax.ShapeDtypeStruct((s, KV_RANK), BF16),
                   jax.ShapeDtypeStruct((s, 2 * ROPE), F32)],
        compiler_params=_cparams(("parallel",)))(proj, gq, gkv, table)


def _mla_prep_bwd(proj, dcq, dckv, dkr, gq, gkv, table):
    s = proj.shape[0]
    ts = _pick(s, ROW_TILE, 16)

    def norm_bwd(xv, dy, g):
        xn, r = _rms(xv, None)
        dg = jnp.sum(dy * xn, axis=0, keepdims=True)
        dxn = dy * g
        return r * (dxn - xn * jnp.mean(dxn * xn, axis=-1, keepdims=True)), dg

    def body(p_ref, dcq_ref, dckv_ref, dkr_ref, gq_ref, gkv_ref, t_ref, dt_ref, dgq_ref, dgkv_ref):
        @pl.when(pl.program_id(0) == 0)
        def _():
            dgq_ref[...] = jnp.zeros_like(dgq_ref)
            dgkv_ref[...] = jnp.zeros_like(dgkv_ref)

        dxq, dgq = norm_bwd(p_ref[:, 0:Q_RANK], dcq_ref[...], gq_ref[...])
        dxkv, dgkv = norm_bwd(p_ref[:, Q_RANK:Q_RANK + KV_RANK], dckv_ref[...], gkv_ref[...])
        dgq_ref[...] += dgq
        dgkv_ref[...] += dgkv
        d = _low_lanes(dkr_ref[...])
        dslab = (d + pltpu.roll(d, ROPE, 1)) * t_ref[...]
        dt_ref[:, 0:Q_RANK] = dxq.astype(BF16)
        dt_ref[:, Q_RANK:Q_RANK + KV_RANK] = dxkv.astype(BF16)
        dt_ref[:, Q_RANK + KV_RANK:TAIL] = dslab.astype(BF16)

    return _pcall(
        body, name="mla_prep_bwd", grid=(s // ts,),
        in_specs=[pl.BlockSpec((ts, TAIL), lambda i: (i, TAIL0 // TAIL)), _row(ts, Q_RANK), _row(ts, KV_RANK),
                  _row(ts, 2 * ROPE), _vec(Q_RANK), _vec(KV_RANK), _row(ts, 2 * ROPE)],
        out_specs=[_row(ts, TAIL), _vec(Q_RANK), _vec(KV_RANK)],
        out_shape=[jax.ShapeDtypeStruct((s, TAIL), BF16), jax.ShapeDtypeStruct((1, Q_RANK), F32),
                   jax.ShapeDtypeStruct((1, KV_RANK), F32)],
        compiler_params=_cparams(("arbitrary",)))(proj, dcq, dckv, dkr, gq, gkv, table)


def _head_specs(ts):
    tok = lambda w: pl.BlockSpec((ts, w), lambda h, i: (i, 0))
    head = lambda w: pl.BlockSpec((None, ts, w), lambda h, i: (h, i, 0))
    wgt = lambda r, c: pl.BlockSpec((None, r, c), lambda h, i: (h, 0, 0))
    return tok, head, wgt


def _mla_qkv(cq, ckv, kr, wq, wkv, table):
    s = cq.shape[0]
    ts = _pick(s, 4 * ROW_TILE, 16)
    tok, head, wgt = _head_specs(ts)

    def body(cq_ref, ckv_ref, kr_ref, wq_ref, wkv_ref, t_ref, q_ref, k_ref, v_ref):
        qf = lax.dot_general(cq_ref[...], wq_ref[...], _DIMS["nn"], preferred_element_type=F32)
        q_ref[:, 0:NOPE] = qf[:, 0:NOPE].astype(BF16)
        q_ref[:, NOPE:QW] = _rope_slab(qf[:, NOPE:QW], t_ref[...]).astype(BF16)
        kv = lax.dot_general(ckv_ref[...], wkv_ref[...], _DIMS["nn"], preferred_element_type=F32)
        k_ref[:, 0:NOPE] = kv[:, 0:NOPE].astype(BF16)
        k_ref[:, NOPE:QW] = kr_ref[...].astype(BF16)
        v_ref[:, 0:VDIM] = kv[:, NOPE:NOPE + VDIM].astype(BF16)
        lane = lax.broadcasted_iota(jnp.int32, (ts, VDIM), 1)
        v_ref[:, VDIM:2 * VDIM] = jnp.where(lane == 0, 1.0, 0.0).astype(BF16)

    return _pcall(
        body, name="mla_qkv", grid=(MLA_H, s // ts),
        in_specs=[tok(Q_RANK), tok(KV_RANK), tok(2 * ROPE), wgt(Q_RANK, QW), wgt(KV_RANK, NOPE + VDIM), tok(2 * ROPE)],
        out_specs=[head(QW), head(QW), head(2 * VDIM)],
        out_shape=[jax.ShapeDtypeStruct((MLA_H, s, QW), BF16), jax.ShapeDtypeStruct((MLA_H, s, QW), BF16),
                   jax.ShapeDtypeStruct((MLA_H, s, 2 * VDIM), BF16)],
        compiler_params=_cparams(("parallel", "parallel")))(cq, ckv, kr, wq, wkv, table)


def _mla_qkv_bwd(dq, dk, dv, cq, ckv, wq, wkv, table, comm=None):
    s = cq.shape[0]
    ts = _pick(s, 4 * ROW_TILE, 16)
    tok, head, wgt = _head_specs(ts)
    whole = lambda w: pl.BlockSpec((s, w), lambda h, i: (0, 0))

    def body(dq_ref, dk_ref, dv_ref, cq_ref, ckv_ref, wq_ref, wkv_ref, t_ref,
             dcq_ref, dckv_ref, dkr_ref, gwq_ref, gwkv_ref):
        h, i = pl.program_id(0), pl.program_id(1)
        rows = pl.ds(pl.multiple_of(i * ts, ts), ts)
        d = dq_ref[:, NOPE:QW]
        dslab = (d + pltpu.roll(d, ROPE, 1)) * t_ref[...]
        dqe = jnp.concatenate([dq_ref[:, 0:NOPE], dslab], axis=1).astype(BF16)
        dkv = jnp.concatenate([dk_ref[:, 0:NOPE], dv_ref[...]], axis=1).astype(BF16)
        dcq = lax.dot_general(dqe, wq_ref[...], _DIMS["nt"], preferred_element_type=F32)
        dckv = lax.dot_general(dkv, wkv_ref[...], _DIMS["nt"], preferred_element_type=F32)
        gwq = lax.dot_general(cq_ref[...], dqe, _DIMS["tn"], preferred_element_type=F32)
        gwkv = lax.dot_general(ckv_ref[...], dkv, _DIMS["tn"], preferred_element_type=F32)
        dkr = dk_ref[:, NOPE:QW].astype(F32)

        @pl.when(h == 0)
        def _():
            dcq_ref[rows, :] = dcq
            dckv_ref[rows, :] = dckv
            dkr_ref[rows, :] = dkr

        @pl.when(h > 0)
        def _():
            dcq_ref[rows, :] += dcq
            dckv_ref[rows, :] += dckv
            dkr_ref[rows, :] += dkr

        @pl.when(i == 0)
        def _():
            gwq_ref[...] = gwq
            gwkv_ref[...] = gwkv

        @pl.when(i > 0)
        def _():
            gwq_ref[...] += gwq
            gwkv_ref[...] += gwkv

    return _pcall(
        body, comm=comm, name="mla_qkv_bwd", grid=(MLA_H, s // ts),
        in_specs=[head(QW), head(QW), head(VDIM), tok(Q_RANK), tok(KV_RANK), wgt(Q_RANK, QW),
                  wgt(KV_RANK, NOPE + VDIM), tok(2 * ROPE)],
        out_specs=[whole(Q_RANK), whole(KV_RANK), whole(2 * ROPE), wgt(Q_RANK, QW), wgt(KV_RANK, NOPE + VDIM)],
        out_shape=[jax.ShapeDtypeStruct((s, Q_RANK), F32), jax.ShapeDtypeStruct((s, KV_RANK), F32),
                   jax.ShapeDtypeStruct((s, 2 * ROPE), F32), jax.ShapeDtypeStruct((MLA_H, Q_RANK, QW), F32),
                   jax.ShapeDtypeStruct((MLA_H, KV_RANK, NOPE + VDIM), F32)],
        compiler_params=_cparams(("arbitrary", "arbitrary")))(dq, dk, dv, cq, ckv, wq, wkv, table)


def _diag_mask(t):
    return lax.broadcasted_iota(jnp.int32, (t, t), 1) <= lax.broadcasted_iota(jnp.int32, (t, t), 0)


def _mla_fwd(q, k, v, comm=None):
    s = q.shape[1]
    t = _pick(s, ATT_T, 128)
    nt = s // t
    assert nt % 2 == 0
    hb = 2 * MLA_HB

    def fold(p, u):
        first = u <= p
        return jnp.where(first, p, nt - 1 - p), jnp.where(first, u, u - p - 1)

    to_log2 = MLA_SCALE * math.log2(math.e)

    def body(q_ref, k_ref, v_ref, o_ref, oh_ref, lse_ref, m_ref, acc_ref):
        i, j = fold(pl.program_id(1), pl.program_id(2))

        @pl.when(j == 0)
        def _():
            m_ref[...] = jnp.full_like(m_ref, -jnp.inf)
            acc_ref[...] = jnp.zeros_like(acc_ref)

        def step(diagonal):
            for h in range(hb):
                sc = lax.dot_general(q_ref[h], k_ref[h], _DIMS["nt"], preferred_element_type=F32)
                if diagonal:
                    sc = jnp.where(_diag_mask(t), sc, -jnp.inf)
                m_old = m_ref[h]
                m_new = jnp.maximum(m_old, jnp.max(sc, axis=-1, keepdims=True))
                alpha = jnp.exp2((m_old - m_new) * to_log2)
                p = jnp.exp2((sc - m_new) * to_log2)
                acc_ref[h] = alpha * acc_ref[h] + lax.dot_general(p.astype(BF16), v_ref[h], _DIMS["nn"],
                                                                  preferred_element_type=F32)
                m_ref[h] = m_new

        @pl.when(j < i)
        def _():
            step(False)

        @pl.when(j == i)
        def _():
            step(True)
            for h in range(hb):
                den = acc_ref[h, :, VDIM:VDIM + 1]
                o = acc_ref[h, :, 0:VDIM] / den
                o_ref[:, h * VDIM:(h + 1) * VDIM] = o
                oh_ref[:, h * VDIM:(h + 1) * VDIM] = o.astype(BF16)
                lse_ref[h] = m_ref[h] * MLA_SCALE + jnp.log(den)

    o_spec = pl.BlockSpec((t, hb * VDIM), lambda h, p, u: (fold(p, u)[0], h))
    return _pcall(
        body, comm=comm, name="mla_fwd", grid=(MLA_H // hb, nt // 2, nt + 1),
        in_specs=[pl.BlockSpec((hb, t, QW), lambda h, p, u: (h, fold(p, u)[0], 0)),
                  pl.BlockSpec((hb, t, QW), lambda h, p, u: (h, fold(p, u)[1], 0)),
                  pl.BlockSpec((hb, t, 2 * VDIM), lambda h, p, u: (h, fold(p, u)[1], 0))],
        out_specs=[o_spec, o_spec, pl.BlockSpec((hb, t, 1), lambda h, p, u: (h, fold(p, u)[0], 0))],
        out_shape=[jax.ShapeDtypeStruct((s, MLA_H * VDIM), F32), jax.ShapeDtypeStruct((s, MLA_H * VDIM), BF16),
                   jax.ShapeDtypeStruct((MLA_H, s, 1), F32)],
        scratch_shapes=[pltpu.VMEM((hb, t, 1), F32), pltpu.VMEM((hb, t, 2 * VDIM), F32)],
        compiler_params=_cparams(("parallel", "parallel", "arbitrary")))(q, k, v)


def _mla_delta(dmix, o):
    s = o.shape[0]
    ts = _pick(s, 2 * ROW_TILE, 16)
    w = MLA_H * VDIM

    def body(d_ref, o_ref, out_ref):
        prod = d_ref[...] * o_ref[...]
        for h in range(MLA_H):
            out_ref[h] = jnp.sum(prod[:, h * VDIM:(h + 1) * VDIM], axis=-1, keepdims=True)

    return _pcall(body, name="mla_delta", grid=(s // ts,),
                  in_specs=[pl.BlockSpec((ts, w), lambda i: (i, SWA_HEADS * SWA_DH // w)), pl.BlockSpec((ts, w), lambda i: (i, 0))],
                  out_specs=pl.BlockSpec((MLA_H, ts, 1), lambda i: (0, i, 0)),
                  out_shape=jax.ShapeDtypeStruct((MLA_H, s, 1), F32), compiler_params=_cparams(("parallel",)))(dmix, o)


def _mla_bwd(q, k, v, dmix, delta, lse, comm=None):
    s = q.shape[1]
    t = _pick(s, ATT_T, 128)
    nt = s // t
    assert nt % 2 == 0
    hb = MLA_HB
    o_blk0 = SWA_HEADS * SWA_DH // (hb * VDIM)

    def fold(p, u):
        first = u < nt - p
        return jnp.where(first, p, nt - 1 - p), jnp.where(first, p + u, u - 1)

    log2e = math.log2(math.e)

    def body(q_ref, k_ref, v_ref, do_ref, delta_ref, lse_ref, dq_ref, dk_ref, dv_ref, dk_acc, dv_acc):
        j, i = fold(pl.program_id(1), pl.program_id(2))
        rows = pl.ds(pl.multiple_of(i * t, t), t)

        @pl.when(i == j)
        def _():
            dk_acc[...] = jnp.zeros_like(dk_acc)
            dv_acc[...] = jnp.zeros_like(dv_acc)

        def step(diagonal):
            for h in range(hb):
                qv, kv_ = q_ref[h], k_ref[h]
                dob = do_ref[:, h * VDIM:(h + 1) * VDIM].astype(BF16)
                st = lax.dot_general(kv_, qv, _DIMS["nt"], preferred_element_type=F32)
                pt = jnp.exp2(st * (MLA_SCALE * log2e) - lse_ref[h] * log2e)
                if diagonal:
                    keep = lax.broadcasted_iota(jnp.int32, (t, t), 0) <= lax.broadcasted_iota(jnp.int32, (t, t), 1)
                    pt = jnp.where(keep, pt, 0.0)
                dpt = lax.dot_general(v_ref[h], dob, _DIMS["nt"], preferred_element_type=F32)
                dst = (pt * (dpt - delta_ref[h]) * MLA_SCALE).astype(BF16)
                dv_acc[h] += lax.dot_general(pt.astype(BF16), dob, _DIMS["nn"], preferred_element_type=F32)
                dk_acc[h] += lax.dot_general(dst, qv, _DIMS["nn"], preferred_element_type=F32)
                dqv = lax.dot_general(dst, kv_, _DIMS["tn"], preferred_element_type=F32)

                @pl.when(j == 0)
                def _():
                    dq_ref[h, rows, :] = dqv

                @pl.when(j > 0)
                def _():
                    dq_ref[h, rows, :] += dqv

        @pl.when(i > j)
        def _():
            step(False)

        @pl.when(i == j)
        def _():
            step(True)

        @pl.when(i == nt - 1)
        def _():
            dk_ref[...] = dk_acc[...].astype(BF16)
            dv_ref[...] = dv_acc[...].astype(BF16)

    qi = lambda h, p, u: (h, fold(p, u)[1], 0)
    kj = lambda h, p, u: (h, fold(p, u)[0], 0)
    row = pl.BlockSpec((hb, 1, t), lambda h, p, u: (h, 0, fold(p, u)[1]))
    return _pcall(
        body, comm=comm, name="mla_bwd", grid=(MLA_H // hb, nt // 2, nt + 1),
        in_specs=[pl.BlockSpec((hb, t, QW), qi), pl.BlockSpec((hb, t, QW), kj), pl.BlockSpec((hb, t, VDIM), kj),
                  pl.BlockSpec((t, hb * VDIM), lambda h, p, u: (fold(p, u)[1], o_blk0 + h)), row, row],
        out_specs=[pl.BlockSpec((hb, s, QW), lambda h, p, u: (h, 0, 0)), pl.BlockSpec((hb, t, QW), kj),
                   pl.BlockSpec((hb, t, VDIM), kj)],
        out_shape=[jax.ShapeDtypeStruct((MLA_H, s, QW), F32), jax.ShapeDtypeStruct((MLA_H, s, QW), BF16),
                   jax.ShapeDtypeStruct((MLA_H, s, VDIM), BF16)],
        scratch_shapes=[pltpu.VMEM((hb, t, QW), F32), pltpu.VMEM((hb, t, VDIM), F32)],
        compiler_params=_cparams(("arbitrary", "arbitrary", "arbitrary")))(q, k, v, dmix, delta, lse)


def _adamw(name, w, g, m, v, parts):
    r, c = w.shape
    n_parts = g.shape[0] if parts else 1
    tr = r if r * c <= ADAM_ELEMS else _pick(r, max(8, ADAM_ELEMS // c // 8 * 8), 8)
    c1 = 1.0 - ADAM_B1 ** ADAM_STEP
    c2 = 1.0 - ADAM_B2 ** ADAM_STEP

    def body(w_ref, g_ref, m_ref, v_ref, go_ref, d_ref, mo_ref, vo_ref):
        if parts:
            gv = g_ref[0].astype(F32)
            for j in range(1, n_parts):
                gv = gv + g_ref[j].astype(F32)
        else:
            gv = g_ref[...]
        mv = ADAM_B1 * m_ref[...] + (1.0 - ADAM_B1) * gv
        vv = ADAM_B2 * v_ref[...] + (1.0 - ADAM_B2) * (gv * gv)
        go_ref[...] = gv
        mo_ref[...] = mv
        vo_ref[...] = vv
        d_ref[...] = -ADAM_LR * ((mv / c1) / (jnp.sqrt(vv / c2) + ADAM_EPS) + ADAM_WD * w_ref[...])

    blk = pl.BlockSpec((tr, c), lambda i: (i, 0))
    g_spec = pl.BlockSpec((n_parts, tr, c), lambda i: (0, i, 0)) if parts else blk
    out = jax.ShapeDtypeStruct((r, c), F32)
    return _pcall(body, name=name, grid=(r // tr,), in_specs=[blk, g_spec, blk, blk], out_specs=[blk] * 4,
                  out_shape=[out] * 4, compiler_params=_cparams(("parallel",)))(w, g, m, v)


def _t5_bucket(dist):
    n = jnp.maximum(dist, 0)
    max_exact = REL_BUCKETS // 2
    nf = jnp.maximum(n, 1).astype(F32)
    large = max_exact + (jnp.log(nf / max_exact) / math.log(REL_MAX_DIST / max_exact)
                         * (REL_BUCKETS - max_exact)).astype(jnp.int32)
    return jnp.where(n < max_exact, n, jnp.minimum(large, REL_BUCKETS - 1))


def _swap_halves(w, r0):
    return jnp.concatenate([w[:, r0 + ROPE // 2:r0 + ROPE], w[:, r0:r0 + ROPE // 2]], axis=1)


def _fold_swapped(g, r0, width):
    sw = g[..., width:width + ROPE]
    half = ROPE // 2
    return jnp.concatenate([g[..., :r0], g[..., r0:r0 + half] + sw[..., half:], g[..., r0 + half:r0 + ROPE] + sw[..., :half],
                            g[..., r0 + ROPE:width]], axis=-1)


def kernel(x, c, w_mod, b_mod, attn_norm_g, w_in, swa_sinks, rel_bias, mla_q_norm_g, w_uq, mla_kv_norm_g, w_ukv, w_out, mlp_norm_g, w_ff1, w_ff2, final_norm_g, loss_target, m_w_mod, m_b_mod, m_attn_norm_g, m_w_in, m_swa_sinks, m_rel_bias, m_mla_q_norm_g, m_w_uq, m_mla_kv_norm_g, m_w_ukv, m_w_out, m_mlp_norm_g, m_w_ff1, m_w_ff2, m_final_norm_g, v_w_mod, v_b_mod, v_attn_norm_g, v_w_in, v_swa_sinks, v_rel_bias, v_mla_q_norm_g, v_w_uq, v_mla_kv_norm_g, v_w_ukv, v_w_out, v_mlp_norm_g, v_w_ff1, v_w_ff2, v_final_norm_g):
    s, d = x.shape[1], x.shape[2]
    ffs = w_ff1.shape[2]
    ff = ffs * NDEV
    nmod = w_mod.shape[2]
    me = 4 * lax.axis_index("x") + 2 * lax.axis_index("y") + lax.axis_index("c")
    x2d, tgt = x[0], loss_target[0]
    final_g = final_norm_g.reshape(1, d)

    w_in_l = jnp.concatenate([w_in[0], _swap_halves(w_in[0], OFF_KR)], axis=1).astype(BF16)
    w_uq_l = jnp.concatenate([w_uq[0], _swap_halves(w_uq[0], NOPE)], axis=1).astype(BF16)
    core = jnp.full((1, 128), lax.axis_index("c"), F32)
    (c_all,) = _exchange("gather_c", _Gather([c]))

    b_cols = lax.dynamic_slice(b_mod, (0, me * nmod), (1, nmod))
    act_all, mod_cols = _mod_fwd(c_all.reshape(NDEV, d), w_mod[0], b_cols)
    (mod_g,) = _exchange("gather_mod", _Gather([mod_cols]))
    mod = lax.dynamic_index_in_dim(mod_g, me, axis=1, keepdims=False).reshape(1, 6 * d)
    sh1, sc1, g1, sh2, sc2, g2 = [mod[:, i * d:(i + 1) * d] for i in range(6)]

    pos = jnp.arange(s, dtype=F32)
    inv_freq = ROPE_THETA ** (-jnp.arange(ROPE // 2, dtype=F32) / (ROPE // 2))
    ang = pos[:, None] * inv_freq[None, :]
    cos, sin = jnp.cos(ang), jnp.sin(ang)
    table = jnp.concatenate([cos, cos, -sin, sin], axis=1)
    q_loc = jnp.arange(BLOCK)[:, None]
    k_loc = jnp.arange(2 * BLOCK)[None, :]
    dist = q_loc + BLOCK - k_loc
    in_window = (dist >= 0) & (dist < BLOCK)
    onehot = (_t5_bucket(dist).reshape(-1, 1) == jnp.arange(REL_BUCKETS)[None, :]).astype(BF16)
    bias = _bias_expand(rel_bias.T, onehot.T).reshape(SWA_HEADS, BLOCK, 2 * BLOCK)
    bias = jnp.where(in_window[None], bias, -jnp.inf).reshape(SWA_KV, PAIRS, 2, BLOCK, 2 * BLOCK)
    bias = bias.transpose(0, 1, 3, 2, 4).reshape(SWA_KV, PROWS, PCOLS)
    sinks = jnp.broadcast_to(swa_sinks.reshape(SWA_KV, PAIRS, 1, 2), (SWA_KV, PAIRS, BLOCK, 2)).reshape(SWA_KV, PROWS, 2)
    sinks = (sinks[:, :, 0:1], sinks[:, :, 1:2])

    h1, w_in_g, w_uq_g, w_ukv_g = _norm_mod("norm1", x2d, attn_norm_g, sc1, sh1,
                                            comm=[_Gather([w_in_l, w_uq_l, w_ukv[0].astype(BF16)])])
    w_in_e = w_in_g.reshape(d, IN_EXT)

    def both_dtypes(acc, ex, outs):
        outs[0][...] = acc
        outs[1][...] = acc.astype(BF16)

    tmp = _pick(s, MM_TM // 2, 128)
    proj_blk = pl.BlockSpec((tmp, IN_EXT), lambda i, j, q: (i, 0))
    proj, proj_h = _mm("proj", h1, w_in_e, "nn", (s // tmp, 1, 1), pl.BlockSpec((tmp, d), lambda i, j, q: (i, 0)),
                       pl.BlockSpec((d, IN_EXT), lambda i, j, q: (0, 0)),
                       [jax.ShapeDtypeStruct((s, IN_EXT), F32), jax.ShapeDtypeStruct((s, IN_EXT), BF16)],
                       [proj_blk, proj_blk], (tmp, IN_EXT), both_dtypes)
    def diag_pair(tok):
        x = jnp.stack([tok[:, :SWA_DH], tok[:, SWA_DH:]])
        zero = jnp.zeros_like(x)
        return jnp.concatenate([x, zero], axis=2), jnp.concatenate([zero, x], axis=2)

    k_top, k_bot = diag_pair(proj_h[:, OFF_K:OFF_V])
    v_top, v_bot = diag_pair(proj_h[:, OFF_V:OFF_CQ])
    o_a, lse_a0, lse_a1, w_out_g = _swa2_fwd(proj_h, k_top, k_bot, v_top, v_bot, bias, sinks,
                                             comm=[_Gather([w_out[0].astype(BF16)])])
    w_out_f = w_out_g.reshape(MIX, d)

    cq, ckv, kr = _mla_prep(proj, mla_q_norm_g, mla_kv_norm_g, table)
    q_b, k_b, v_b = _mla_qkv(cq, ckv, kr, w_uq_g, w_ukv_g, table)
    o_b, o_bh, lse_b, w_ff1_g = _mla_fwd(q_b, k_b, v_b, comm=[_Gather([w_ff1[0].astype(BF16)])])
    mix = jnp.concatenate([o_a, o_bh], axis=1)

    tm, tn, tk = _pick(s, MM_TM, 128), _pick(d, MM_TN, 128), _pick(MIX, MM_TK, 128)
    row_blk = pl.BlockSpec((tm, tn), lambda i, j, q: (i, j))
    gate_blk = pl.BlockSpec((1, tn), lambda i, j, q: (0, j))

    def gated_residual(acc, ex, outs):
        outs[0][...] = acc
        outs[1][...] = ex[0][...] + ex[1][...] * acc

    y1, x2 = _mm("out_proj", mix, w_out_f, "nn", (s // tm, d // tn, MIX // tk),
                 pl.BlockSpec((tm, tk), lambda i, j, q: (i, q)), pl.BlockSpec((tk, tn), lambda i, j, q: (q, j)),
                 [jax.ShapeDtypeStruct((s, d), F32)] * 2, [row_blk, row_blk], (tm, tn), gated_residual,
                 extras=(x2d, g1), extra_specs=(row_blk, gate_blk))

    (h2,) = _norm_mod("norm2", x2, mlp_norm_g, sc2, sh2)
    tnf, tkd = _pick(ffs, MM_TN, 128), _pick(d, MM_TK, 128)
    rf = ffs // tnf
    ff_blk = pl.BlockSpec((tm, tnf), lambda i, j, q: (i, j))

    def relu_sq(acc, ex, outs):
        u = jnp.maximum(acc, 0.0)
        outs[0][...] = u
        outs[1][...] = (u * u).astype(BF16)

    u, uu, w_ff2_g = _mm("ff1", h2, w_ff1_g, "nn", (s // tm, ff // tnf, d // tkd),
                         pl.BlockSpec((tm, tkd), lambda i, j, q: (i, q)),
                         pl.BlockSpec((None, tkd, tnf), lambda i, j, q: (j // rf, q, j % rf)),
                         [jax.ShapeDtypeStruct((s, ff), F32), jax.ShapeDtypeStruct((s, ff), BF16)], [ff_blk, ff_blk],
                         (tm, tnf), relu_sq, comm=[_Gather([w_ff2[0].astype(BF16)])])
    w_ff2_f = w_ff2_g.reshape(ff, d)
    tkf = _pick(ff, MM_TK, 128)
    y2, x3 = _mm("ff2", uu, w_ff2_f, "nn", (s // tm, d // tn, ff // tkf),
                 pl.BlockSpec((tm, tkf), lambda i, j, q: (i, q)), pl.BlockSpec((tkf, tn), lambda i, j, q: (q, j)),
                 [jax.ShapeDtypeStruct((s, d), F32)] * 2, [row_blk, row_blk], (tm, tn), gated_residual,
                 extras=(x2, g2), extra_specs=(row_blk, gate_blk))

    dx3, dy2, loss_p, dgf, dg2 = _loss_head(x3, tgt, y2, final_g, g2)

    def relu_sq_bwd(acc, ex, outs):
        outs[0][...] = (acc * (2.0 * ex[0][...])).astype(BF16)

    tnf2 = _pick(ff, MM_TN, 128)
    du = _mm("ff2_dx", dy2, w_ff2_f, "nt", (s // tm, ff // tnf2, d // tkd),
             pl.BlockSpec((tm, tkd), lambda i, j, q: (i, q)), pl.BlockSpec((tnf2, tkd), lambda i, j, q: (j, q)),
             [jax.ShapeDtypeStruct((s, ff), BF16)], [pl.BlockSpec((tm, tnf2), lambda i, j, q: (i, j))],
             (tm, tnf2), relu_sq_bwd, extras=(u,), extra_specs=(pl.BlockSpec((tm, tnf2), lambda i, j, q: (i, j)),))[0]
    gw_ff2 = _mm_plain("ff2_dw", uu, dy2, "tn", ff, d, s, BF16)
    tmd, tks = _pick(d, MM_TM, 128), _pick(s, MM_TK, 128)
    gw_ff2 = gw_ff2.reshape(NDEV, ffs, d)
    dh2, s_ff2 = _mm("ff1_dx", du, w_ff1_g, "nt", (s // tm, d // tn, NDEV // 2),
                     pl.BlockSpec((tm, 2 * ffs), lambda i, j, q: (i, q)),
                     pl.BlockSpec((2, tn, ffs), lambda i, j, q: (q, j, 0)),
                     [jax.ShapeDtypeStruct((s, d), F32)], [row_blk], (tm, tn), _store(F32),
                     comm=[_PairSwap([gw_ff2])], b_parts=2)
    c_ff2 = _pair_sum("pair_ff2", gw_ff2, s_ff2, core)
    gw_ff1, p_ff2 = _mm("ff1_dw", h2, du, "tn", (d // tmd, ff // tnf, s // tks),
                        pl.BlockSpec((tks, tmd), lambda i, j, q: (q, i)), pl.BlockSpec((tks, tnf), lambda i, j, q: (q, j)),
                        [jax.ShapeDtypeStruct((NDEV, d, ffs), BF16)],
                        [pl.BlockSpec((None, tmd, tnf), lambda i, j, q: (j // rf, i, j % rf))], (tmd, tnf), _store(BF16),
                        comm=[_ChipScatter([c_ff2])])
    dx2, dy1, dsc2, dsh2, dgm, dg1, s_ff1 = _norm_mod_bwd("norm2_bwd", x2, dh2, dx3, mlp_norm_g, sc2, y1, g1,
                                                          comm=[_PairSwap([gw_ff1])])
    c_ff1 = _pair_sum("pair_ff1", gw_ff1, s_ff1, core)

    dmix = _mm_plain("out_proj_dx", dy1, w_out_f, "nt", s, MIX, d, F32)
    gw_out = _mm_plain("out_proj_dw", mix, dy1, "tn", MIX, d, s, BF16).reshape(NDEV, MIX // NDEV, d)

    delta_b = _mla_delta(dmix, o_b).reshape(MLA_H, 1, s)
    dq_b, dk_b, dv_b, p_ff1, s_out = _mla_bwd(q_b, k_b, v_b, dmix, delta_b, lse_b.reshape(MLA_H, 1, s),
                                              comm=[_ChipScatter([c_ff1]), _PairSwap([gw_out])])
    c_out = _pair_sum("pair_out", gw_out, s_out, core)
    dcq, dckv, dkr, gw_uq_e, gw_ukv, p_out = _mla_qkv_bwd(dq_b, dk_b, dv_b, cq, ckv, w_uq_g, w_ukv_g, table,
                                                          comm=[_ChipScatter([c_out])])
    dtail, dgq, dgkv = _mla_prep_bwd(proj, dcq, dckv, dkr, mla_q_norm_g, mla_kv_norm_g, table)
    gw_uq = _fold_swapped(gw_uq_e, NOPE, NOPE + ROPE).astype(BF16)
    gw_ukv = gw_ukv.astype(BF16)

    dq_a, dkp, dkc, dvp, dvc, dbias, dsink0, dsink1, s_uq, s_ukv = _swa2_bwd(
        proj_h, dmix, k_top, k_bot, v_top, v_bot, (lse_a0, lse_a1), bias, sinks, comm=[_PairSwap([gw_uq, gw_ukv])])
    c_uq = _pair_sum("pair_uq", gw_uq, s_uq, core)
    c_ukv = _pair_sum("pair_ukv", gw_ukv, s_ukv, core)

    def band_grad(cur, prv):
        g = cur + jnp.concatenate([prv[:, BLOCK:], jnp.zeros_like(prv[:, :BLOCK])], axis=1)
        g = g[:, :, :SWA_DH] + g[:, :, SWA_DH:]
        return jnp.concatenate([g[0], g[1]], axis=1)

    dbias = dbias.reshape(SWA_KV, PAIRS, BLOCK, 2, 2 * BLOCK).transpose(0, 1, 3, 2, 4)
    dsink = jnp.stack([dsink0.reshape(SWA_KV, PAIRS, BLOCK), dsink1.reshape(SWA_KV, PAIRS, BLOCK)], axis=2)
    drel_t, dsinks = _bias_reduce(dbias.reshape(SWA_HEADS, BLOCK * 2 * BLOCK), onehot, dsink.reshape(SWA_HEADS, BLOCK))
    dproj = jnp.concatenate([dq_a, band_grad(dkc, dkp).astype(BF16), band_grad(dvc, dvp).astype(BF16), dtail], axis=1)
    gw_in_e = _mm_plain("proj_dw", h1, dproj, "tn", d, IN_EXT, s, F32, tn=TAIL)
    gw_in = _fold_swapped(gw_in_e, OFF_KR, IN_COLS).reshape(NDEV, d // NDEV, IN_COLS).astype(BF16)
    tkt = IN_EXT
    dh1, s_in, p_uq, p_ukv = _mm(
        "proj_dx", dproj, w_in_e, "nt", (s // tm, d // tn, IN_EXT // tkt),
        pl.BlockSpec((tm, tkt), lambda i, j, q: (i, q)), pl.BlockSpec((tn, tkt), lambda i, j, q: (j, q)),
        [jax.ShapeDtypeStruct((s, d), F32)], [row_blk], (tm, tn), _store(F32),
        comm=[_PairSwap([gw_in]), _ChipScatter([c_uq, c_ukv])])
    c_in = _pair_sum("pair_in", gw_in, s_in, core)
    gx, dsc1, dsh1, dga, p_in = _norm_mod_bwd("norm1_bwd", x2d, dh1, dx2, attn_norm_g, sc1,
                                              comm=[_ChipScatter([c_in])])

    small = [jnp.concatenate([dsh1, dsc1, dg1, dsh2, dsc2, dg2], axis=1), dga, dgm, dgf, dgq, dgkv,
             dsinks.reshape(1, SWA_HEADS), drel_t.T.reshape(1, REL_BUCKETS * SWA_HEADS)]
    n_small = sum(a.shape[1] for a in small)
    n_pad = -(n_small + 1) % 1024 + 1
    rows_small = (n_small + n_pad) // 128
    pad = jnp.zeros((1, n_pad), F32)
    pack = lambda parts, tail=pad: jnp.concatenate([p.reshape(1, -1) for p in parts] + [tail], axis=1).reshape(rows_small, 128)
    (small_g,) = _exchange("gather_small", _Gather([pack(small, jnp.concatenate([loss_p, pad[:, 1:]], axis=1))]))
    small_names = (b_mod, attn_norm_g, mlp_norm_g, final_norm_g, mla_q_norm_g, mla_kv_norm_g, swa_sinks, rel_bias)
    small_m = (m_b_mod, m_attn_norm_g, m_mlp_norm_g, m_final_norm_g, m_mla_q_norm_g, m_mla_kv_norm_g, m_swa_sinks, m_rel_bias)
    small_v = (v_b_mod, v_attn_norm_g, v_mlp_norm_g, v_final_norm_g, v_mla_q_norm_g, v_mla_kv_norm_g, v_swa_sinks, v_rel_bias)
    small_out = _adamw("adamw_small", pack(small_names), small_g, pack(small_m), pack(small_v), parts=True)

    def unpack(flat):
        flat = flat.reshape(1, -1)
        out, off = [], 0
        for a in small_names:
            out.append(flat[:, off:off + a.size].reshape(a.shape))
            off += a.size
        return out

    sg, sd, sm, sv = [unpack(o) for o in small_out]
    loss = small_out[0].reshape(-1)[n_small]

    dmod_cols = lax.dynamic_slice(small_g.reshape(NDEV, -1), (0, me * nmod), (NDEV, nmod))
    gw_mod = _mod_wgrad(act_all, dmod_cols)
    big = {"w_mod": _adamw("adamw_w_mod", w_mod[0], gw_mod, m_w_mod[0], v_w_mod[0], parts=False)}

    for name, w, p, m, v in (("w_in", w_in, p_in, m_w_in, v_w_in), ("w_uq", w_uq, p_uq, m_w_uq, v_w_uq),
                             ("w_ukv", w_ukv, p_ukv, m_w_ukv, v_w_ukv), ("w_out", w_out, p_out, m_w_out, v_w_out),
                             ("w_ff1", w_ff1, p_ff1, m_w_ff1, v_w_ff1), ("w_ff2", w_ff2, p_ff2, m_w_ff2, v_w_ff2)):
        big[name] = _adamw("adamw_" + name, w[0], p, m[0], v[0], parts=True)

    order = ("w_mod", "b_mod", "attn_norm_g", "w_in", "swa_sinks", "rel_bias", "mla_q_norm_g", "w_uq", "mla_kv_norm_g",
             "w_ukv", "w_out", "mlp_norm_g", "w_ff1", "w_ff2", "final_norm_g")
    small_idx = {"b_mod": 0, "attn_norm_g": 1, "mlp_norm_g": 2, "final_norm_g": 3, "mla_q_norm_g": 4,
                 "mla_kv_norm_g": 5, "swa_sinks": 6, "rel_bias": 7}
    outs = []
    for kind, small_list in enumerate((sg, sd, sm, sv)):
        for name in order:
            outs.append(small_list[small_idx[name]] if name in small_idx else big[name][kind][None])
    return (loss, gx[None], *outs)
```

```python
import functools
import math

import jax
import jax.numpy as jnp
from jax import lax
from jax.experimental import pallas as pl
from jax.experimental.pallas import tpu as pltpu

F32 = jnp.float32
BF16 = jnp.bfloat16

NDEV = 8
EPS = 1e-6
BLOCK = 128
SWA_HEADS, SWA_KV, SWA_DH, SWA_GROUP = 16, 2, 64, 8
REL_BUCKETS, REL_MAX_DIST = 32, 128
MLA_H, Q_RANK, KV_RANK, NOPE, ROPE, VDIM = 8, 384, 128, 128, 64, 128
ROPE_THETA = 10000.0
OFF_K, OFF_V, OFF_CQ, OFF_CKV, OFF_KR, IN_COLS = 1024, 1152, 1280, 1664, 1792, 1856
IN_EXT = IN_COLS + ROPE
TAIL0, TAIL = OFF_CQ, IN_EXT - OFF_CQ
QW = NOPE + 2 * ROPE
MIX = SWA_HEADS * SWA_DH + MLA_H * VDIM
MLA_SCALE = (NOPE + ROPE) ** -0.5
SWA_SCALE = SWA_DH ** -0.5

ADAM_LR, ADAM_B1, ADAM_B2, ADAM_EPS, ADAM_WD, ADAM_STEP = 0.001, 0.9, 0.999, 1e-08, 0.01, 10

VMEM_LIMIT = 52 * 1024 * 1024
ROW_TILE = 256
MM_TM, MM_TN, MM_TK = 1024, 1024, 2048
ATT_T = 512
MLA_HB = 2
ADAM_ELEMS = 256 * 1024


MESH_ID = pl.DeviceIdType.MESH


def _place():
    x, y, c = lax.axis_index("x"), lax.axis_index("y"), lax.axis_index("c")
    return x, y, c, 2 * x + y


def _chip(x, y, k):
    return (1 - x if k & 2 else x, 1 - y if k & 1 else y)


def _dma_sems(*counts):
    return [pltpu.SemaphoreType.DMA((n,)) for n in counts]


class _Gather:
    def __init__(self, arrays):
        self.arrays = list(arrays)
        n = len(self.arrays)
        self.out_shape = [jax.ShapeDtypeStruct((NDEV,) + a.shape, a.dtype) for a in self.arrays]
        self.sems = _dma_sems(7 * n, 7 * n, n)

    def _copy(self, sems, a, k, src, dst, to):
        return pltpu.make_async_remote_copy(src_ref=src, dst_ref=dst, send_sem=sems[0].at[7 * a + k],
                                            recv_sem=sems[1].at[7 * a + k], device_id=to, device_id_type=MESH_ID)

    def start(self, ins, outs, sems):
        x, y, c, q = _place()
        me = 2 * q + c
        for a in range(len(ins)):
            pltpu.make_async_copy(ins[a], outs[a].at[me], sems[2].at[a]).start()
            self._copy(sems, a, 0, ins[a], outs[a].at[me], (x, y, 1 - c)).start()
            for k in (1, 2, 3):
                self._copy(sems, a, k, ins[a], outs[a].at[me], (*_chip(x, y, k), c)).start()

    def finish(self, ins, outs, sems):
        x, y, c, q = _place()
        me, sib = 2 * q + c, (x, y, 1 - c)
        n = len(ins)
        for k in (1, 2, 3):
            for a in range(n):
                blk = outs[a].at[2 * (q ^ k) + c]
                self._copy(sems, a, k, ins[a], blk, (*_chip(x, y, k), c)).wait_recv()
                self._copy(sems, a, 3 + k, blk, blk, sib).start()
        for a in range(n):
            self._copy(sems, a, 0, ins[a], outs[a].at[2 * q + 1 - c], sib).wait_recv()
            for k in (1, 2, 3):
                blk = outs[a].at[2 * (q ^ k) + 1 - c]
                self._copy(sems, a, 3 + k, blk, blk, sib).wait_recv()
        for a in range(n):
            for k in range(7):
                self._copy(sems, a, k, ins[a], outs[a].at[me], sib).wait_send()
            pltpu.make_async_copy(ins[a], outs[a].at[me], sems[2].at[a]).wait()


class _PairSwap:
    def __init__(self, arrays):
        self.arrays = list(arrays)
        n = len(self.arrays)
        self.out_shape = [jax.ShapeDtypeStruct((NDEV // 2,) + a.shape[1:], a.dtype) for a in self.arrays]
        self.sems = _dma_sems(4 * n, 4 * n)

    def _copy(self, sems, a, p, src, dst, to):
        return pltpu.make_async_remote_copy(src_ref=src, dst_ref=dst, send_sem=sems[0].at[4 * a + p],
                                            recv_sem=sems[1].at[4 * a + p], device_id=to, device_id_type=MESH_ID)

    def start(self, ins, outs, sems):
        x, y, c, _ = _place()
        for a in range(len(ins)):
            for p in range(4):
                self._copy(sems, a, p, ins[a].at[2 * p + 1 - c], outs[a].at[p], (x, y, 1 - c)).start()

    def finish(self, ins, outs, sems):
        x, y, c, _ = _place()
        for a in range(len(ins)):
            for p in range(4):
                cp = self._copy(sems, a, p, ins[a].at[2 * p + 1 - c], outs[a].at[p], (x, y, 1 - c))
                cp.wait_recv()
                cp.wait_send()


class _ChipScatter:
    def __init__(self, arrays):
        self.arrays = list(arrays)
        n = len(self.arrays)
        self.out_shape = [jax.ShapeDtypeStruct(a.shape, a.dtype) for a in self.arrays]
        self.sems = _dma_sems(3 * n, 3 * n, n)

    def _copy(self, sems, a, k, src, dst, to):
        return pltpu.make_async_remote_copy(src_ref=src, dst_ref=dst, send_sem=sems[0].at[3 * a + k - 1],
                                            recv_sem=sems[1].at[3 * a + k - 1], device_id=to, device_id_type=MESH_ID)

    def start(self, ins, outs, sems):
        x, y, c, q = _place()
        for a in range(len(ins)):
            pltpu.make_async_copy(ins[a].at[q], outs[a].at[q], sems[2].at[a]).start()
            for k in (1, 2, 3):
                self._copy(sems, a, k, ins[a].at[q ^ k], outs[a].at[q], (*_chip(x, y, k), c)).start()

    def finish(self, ins, outs, sems):
        x, y, c, q = _place()
        for a in range(len(ins)):
            for k in (1, 2, 3):
                cp = self._copy(sems, a, k, ins[a].at[q ^ k], outs[a].at[q ^ k], (*_chip(x, y, k), c))
                cp.wait_recv()
                cp.wait_send()
            pltpu.make_async_copy(ins[a].at[q], outs[a].at[q], sems[2].at[a]).wait()


def _call(body, **kw):
    return pl.pallas_call(body, **kw)


def _pcall(body, comm=None, **kw):
    if not comm:
        return _call(body, **kw)
    grid = kw["grid"]
    in_specs, out_specs, out_shape = list(kw["in_specs"]), list(kw["out_specs"]), list(kw["out_shape"])
    scratch = list(kw.get("scratch_shapes", ()))
    n_in, n_out, n_scr = len(in_specs), len(out_shape), len(scratch)
    n_cin = [len(j.arrays) for j in comm]
    n_sem = [len(j.sems) for j in comm]
    n = sum(n_cin)
    hbm = pl.BlockSpec(memory_space=pltpu.HBM)

    def carried(*refs):
        ins, cins = refs[:n_in], refs[n_in:n_in + n]
        outs, couts = refs[n_in + n:n_in + n + n_out], refs[n_in + n + n_out:n_in + 2 * n + n_out]
        scr, sems = refs[n_in + 2 * n + n_out:n_in + 2 * n + n_out + n_scr], refs[n_in + 2 * n + n_out + n_scr:]
        ids = [pl.program_id(ax) for ax in range(len(grid))]
        first = functools.reduce(jnp.logical_and, [i == 0 for i in ids])
        last = functools.reduce(jnp.logical_and, [i == g - 1 for i, g in zip(ids, grid)])

        def each(method):
            ai = si = 0
            for job, na, ns in zip(comm, n_cin, n_sem):
                getattr(job, method)(cins[ai:ai + na], couts[ai:ai + na], sems[si:si + ns])
                ai, si = ai + na, si + ns

        @pl.when(first)
        def _():
            each("start")

        body(*ins, *outs, *scr)

        @pl.when(last)
        def _():
            each("finish")

    kw.update(in_specs=in_specs + [hbm] * n, out_specs=out_specs + [hbm] * n,
              out_shape=out_shape + [o for j in comm for o in j.out_shape],
              scratch_shapes=scratch + [sm for j in comm for sm in j.sems],
              compiler_params=_cparams(("arbitrary",) * len(grid)))
    call = _call(carried, **kw)
    return lambda *args: call(*args, *[a for j in comm for a in j.arrays])


def _cparams(sem):
    return pltpu.CompilerParams(dimension_semantics=sem, vmem_limit_bytes=VMEM_LIMIT)


def _pick(n, pref, align):
    if n <= pref:
        return n
    t = (pref // align) * align
    while t >= align:
        if n % t == 0:
            return t
        t -= align
    return n


def _split3(x):
    a = x.astype(BF16)
    r = x - a.astype(F32)
    b = r.astype(BF16)
    c = (r - b.astype(F32)).astype(BF16)
    return a, b, c


def _exchange(name, job):
    n = len(job.arrays)

    def body(*refs):
        ins, outs, sems = refs[:n], refs[n:2 * n], refs[2 * n:]
        job.start(ins, outs, sems)
        job.finish(ins, outs, sems)

    hbm = pl.BlockSpec(memory_space=pltpu.HBM)
    return _call(body, name=name, out_shape=job.out_shape, in_specs=[hbm] * n, out_specs=[hbm] * n,
                 scratch_shapes=job.sems)(*job.arrays)


def _pair_sum(name, g, r, core):
    _, rr, cc = g.shape
    tr = rr if rr * cc <= 4 * ADAM_ELEMS else _pick(rr, max(16, 4 * ADAM_ELEMS // cc // 16 * 16), 16)

    def body(g_ref, r_ref, c_ref, o_ref):
        north = c_ref[:, 0:1] > 0.5
        mine = jnp.where(north, g_ref[1].astype(F32), g_ref[0].astype(F32))
        o_ref[...] = (mine + r_ref[...].astype(F32)).astype(o_ref.dtype)

    return _pcall(
        body, name=name, grid=(NDEV // 2, rr // tr),
        in_specs=[pl.BlockSpec((None, 2, tr, cc), lambda p, i: (p, 0, i, 0)),
                  pl.BlockSpec((None, tr, cc), lambda p, i: (p, i, 0)), pl.BlockSpec((1, 128), lambda p, i: (0, 0))],
        out_specs=pl.BlockSpec((None, tr, cc), lambda p, i: (p, i, 0)),
        out_shape=jax.ShapeDtypeStruct((NDEV // 2, rr, cc), g.dtype),
        compiler_params=_cparams(("parallel", "parallel")))(g.reshape(NDEV // 2, 2, rr, cc), r, core)


_DIMS = {"nn": (((1,), (0,)), ((), ())), "nt": (((1,), (1,)), ((), ())), "tn": (((0,), (0,)), ((), ()))}


def _mm(name, a, b, kind, grid, a_spec, b_spec, out_shape, out_specs, acc_shape, epilogue,
        extras=(), extra_specs=(), comm=None, b_parts=1):
    nk, ne, no = grid[2], len(extras), len(out_shape)

    def body(*refs):
        a_ref, b_ref = refs[0], refs[1]
        ex, outs = refs[2:2 + ne], refs[2 + ne:2 + ne + no]
        if b_parts == 1:
            part = lax.dot_general(a_ref[...].astype(BF16), b_ref[...].astype(BF16), _DIMS[kind],
                                   preferred_element_type=F32)
        else:
            kp = a_ref.shape[1] // b_parts
            part = sum(lax.dot_general(a_ref[:, p * kp:(p + 1) * kp].astype(BF16), b_ref[p].astype(BF16), _DIMS[kind],
                                       preferred_element_type=F32) for p in range(b_parts))
        if nk == 1:
            epilogue(part, ex, outs)
            return
        acc = refs[-1]
        k = pl.program_id(2)

        @pl.when(k == 0)
        def _():
            acc[...] = part

        @pl.when(jnp.logical_and(k > 0, k < nk - 1))
        def _():
            acc[...] += part

        @pl.when(k == nk - 1)
        def _():
            epilogue(acc[...] + part, ex, outs)

    return _pcall(
        body, comm=comm, name=name, grid=grid, in_specs=[a_spec, b_spec, *extra_specs], out_specs=out_specs,
        out_shape=out_shape, scratch_shapes=[pltpu.VMEM(acc_shape, F32)] if nk > 1 else [],
        compiler_params=_cparams(("parallel", "parallel", "arbitrary")),
    )(a, b, *extras)


def _store(dtype):
    def epi(acc, ex, outs):
        outs[0][...] = acc.astype(dtype)
    return epi


def _mm_plain(name, a, b, kind, m, n, k, out_dtype, tm=None, tn=None, tk=None):
    tm = _pick(m, tm or MM_TM, 128)
    tn = _pick(n, tn or MM_TN, 128)
    tk = _pick(k, tk or MM_TK, 128)
    a_spec = pl.BlockSpec((tk, tm), lambda i, j, q: (q, i)) if kind == "tn" else pl.BlockSpec((tm, tk), lambda i, j, q: (i, q))
    b_spec = pl.BlockSpec((tn, tk), lambda i, j, q: (j, q)) if kind == "nt" else pl.BlockSpec((tk, tn), lambda i, j, q: (q, j))
    return _mm(name, a, b, kind, (m // tm, n // tn, k // tk), a_spec, b_spec,
               [jax.ShapeDtypeStruct((m, n), out_dtype)], [pl.BlockSpec((tm, tn), lambda i, j, q: (i, j))],
               (tm, tn), _store(out_dtype))[0]


def _row(ts, d):
    return pl.BlockSpec((ts, d), lambda i: (i, 0))


def _vec(d):
    return pl.BlockSpec((1, d), lambda i: (0, 0))


def _norm_mod(name, x, gain, sc, sh, comm=None):
    s, d = x.shape
    ts = _pick(s, ROW_TILE, 16)

    def body(x_ref, g_ref, sc_ref, sh_ref, h_ref):
        xv = x_ref[...]
        r = lax.rsqrt(jnp.mean(xv * xv, axis=-1, keepdims=True) + EPS)
        h_ref[...] = ((xv * r) * g_ref[...] * (1.0 + sc_ref[...]) + sh_ref[...]).astype(BF16)

    return _pcall(body, comm=comm, name=name, grid=(s // ts,), in_specs=[_row(ts, d), _vec(d), _vec(d), _vec(d)],
                  out_specs=[_row(ts, d)], out_shape=[jax.ShapeDtypeStruct((s, d), BF16)],
                  compiler_params=_cparams(("parallel",)))(x, gain, sc, sh)


def _loss_head(x3, tgt, y2, gf, g2):
    s, d = x3.shape
    ts = _pick(s, ROW_TILE, 16)

    def body(x_ref, t_ref, y_ref, gf_ref, g2_ref, dx_ref, dy_ref, loss_ref, dgf_ref, dg2_ref):
        @pl.when(pl.program_id(0) == 0)
        def _():
            loss_ref[...] = jnp.zeros_like(loss_ref)
            dgf_ref[...] = jnp.zeros_like(dgf_ref)
            dg2_ref[...] = jnp.zeros_like(dg2_ref)

        xv = x_ref[...]
        r = lax.rsqrt(jnp.mean(xv * xv, axis=-1, keepdims=True) + EPS)
        xn = xv * r
        err = xn * gf_ref[...] - t_ref[...]
        loss_ref[...] += 0.5 * jnp.sum(jnp.mean(err * err, axis=-1, keepdims=True), axis=0, keepdims=True)
        dout = err * (1.0 / d)
        dgf_ref[...] += jnp.sum(dout * xn, axis=0, keepdims=True)
        dxn = dout * gf_ref[...]
        dx = r * (dxn - xn * jnp.mean(dxn * xn, axis=-1, keepdims=True))
        dx_ref[...] = dx
        dy_ref[...] = (dx * g2_ref[...]).astype(BF16)
        dg2_ref[...] += jnp.sum(dx * y_ref[...], axis=0, keepdims=True)

    one = pl.BlockSpec((1, 1), lambda i: (0, 0))
    return _pcall(
        body, name="loss_head", grid=(s // ts,),
        in_specs=[_row(ts, d), _row(ts, d), _row(ts, d), _vec(d), _vec(d)],
        out_specs=[_row(ts, d), _row(ts, d), one, _vec(d), _vec(d)],
        out_shape=[jax.ShapeDtypeStruct((s, d), F32), jax.ShapeDtypeStruct((s, d), BF16),
                   jax.ShapeDtypeStruct((1, 1), F32), jax.ShapeDtypeStruct((1, d), F32),
                   jax.ShapeDtypeStruct((1, d), F32)],
        compiler_params=_cparams(("arbitrary",)))(x3, tgt, y2, gf, g2)


def _norm_mod_bwd(name, x, dh, dres, gain, sc, y_prev=None, gate=None, comm=None):
    s, d = x.shape
    ts = _pick(s, ROW_TILE, 16)
    gated = y_prev is not None

    def body(*refs):
        if gated:
            x_ref, dh_ref, dr_ref, g_ref, sc_ref, y_ref, gt_ref, dx_ref, dy_ref, dsc_ref, dsh_ref, dg_ref, dgt_ref = refs
        else:
            x_ref, dh_ref, dr_ref, g_ref, sc_ref, dx_ref, dsc_ref, dsh_ref, dg_ref = refs

        @pl.when(pl.program_id(0) == 0)
        def _():
            dsc_ref[...] = jnp.zeros_like(dsc_ref)
            dsh_ref[...] = jnp.zeros_like(dsh_ref)
            dg_ref[...] = jnp.zeros_like(dg_ref)
            if gated:
                dgt_ref[...] = jnp.zeros_like(dgt_ref)

        xv, dhv = x_ref[...], dh_ref[...]
        r = lax.rsqrt(jnp.mean(xv * xv, axis=-1, keepdims=True) + EPS)
        xn = xv * r
        dsc_ref[...] += jnp.sum(dhv * (xn * g_ref[...]), axis=0, keepdims=True)
        dsh_ref[...] += jnp.sum(dhv, axis=0, keepdims=True)
        da = dhv * (1.0 + sc_ref[...])
        dg_ref[...] += jnp.sum(da * xn, axis=0, keepdims=True)
        dxn = da * g_ref[...]
        dx = dr_ref[...] + r * (dxn - xn * jnp.mean(dxn * xn, axis=-1, keepdims=True))
        dx_ref[...] = dx
        if gated:
            dy_ref[...] = (dx * gt_ref[...]).astype(BF16)
            dgt_ref[...] += jnp.sum(dx * y_ref[...], axis=0, keepdims=True)

    ins = [x, dh, dres, gain, sc] + ([y_prev, gate] if gated else [])
    in_specs = [_row(ts, d)] * 3 + [_vec(d)] * 2 + ([_row(ts, d), _vec(d)] if gated else [])
    vec_out = jax.ShapeDtypeStruct((1, d), F32)
    out_shape = [jax.ShapeDtypeStruct((s, d), F32)] + ([jax.ShapeDtypeStruct((s, d), BF16)] if gated else [])
    out_shape += [vec_out] * (4 if gated else 3)
    out_specs = [_row(ts, d)] * (2 if gated else 1) + [_vec(d)] * (4 if gated else 3)
    return _pcall(body, comm=comm, name=name, grid=(s // ts,), in_specs=in_specs, out_specs=out_specs,
                  out_shape=out_shape, compiler_params=_cparams(("arbitrary",)))(*ins)


def _dot3(a, b, dims):
    a1, a2, _ = _split3(a)
    b1, b2, _ = _split3(b)
    dot = functools.partial(lax.dot_general, dimension_numbers=dims, preferred_element_type=F32)
    return dot(a1, b1) + (dot(a1, b2) + dot(a2, b1))


def _mod_fwd(c_all, w, b_cols, comm=None):
    nb, d = c_all.shape
    n = w.shape[1]
    tk = _pick(d, 512, 128)
    nk = d // tk

    def body(c_ref, w_ref, b_ref, act_ref, out_ref):
        k = pl.program_id(0)
        cv = c_ref[...]
        act = cv * (1.0 / (1.0 + jnp.exp(-cv)))
        act_ref[...] = act

        @pl.when(k == 0)
        def _():
            out_ref[...] = jnp.broadcast_to(b_ref[...], out_ref.shape)

        out_ref[...] += _dot3(act, w_ref[...], _DIMS["nn"])

    return _pcall(
        body, comm=comm, name="mod_fwd", grid=(nk,),
        in_specs=[pl.BlockSpec((nb, tk), lambda k: (0, k)), pl.BlockSpec((tk, n), lambda k: (k, 0)),
                  pl.BlockSpec((1, n), lambda k: (0, 0))],
        out_specs=[pl.BlockSpec((nb, tk), lambda k: (0, k)), pl.BlockSpec((nb, n), lambda k: (0, 0))],
        out_shape=[jax.ShapeDtypeStruct((nb, d), F32), jax.ShapeDtypeStruct((nb, n), F32)],
        compiler_params=_cparams(("arbitrary",)))(c_all, w, b_cols)


def _mod_wgrad(act_all, dmod_cols):
    nb, d = act_all.shape
    n = dmod_cols.shape[1]
    tm = _pick(d, 512, 128)

    def body(a_ref, d_ref, o_ref):
        o_ref[...] = _dot3(a_ref[...], d_ref[...], _DIMS["tn"])

    return _pcall(
        body, name="mod_wgrad", grid=(d // tm,),
        in_specs=[pl.BlockSpec((nb, tm), lambda i: (0, i)), pl.BlockSpec((nb, n), lambda i: (0, 0))],
        out_specs=pl.BlockSpec((tm, n), lambda i: (i, 0)), out_shape=jax.ShapeDtypeStruct((d, n), F32),
        compiler_params=_cparams(("parallel",)))(act_all, dmod_cols)


def _bias_expand(rel_t, onehot_t):
    h, _ = rel_t.shape
    n = onehot_t.shape[1]

    def body(r_ref, o_ref, out_ref):
        a, b, c = _split3(r_ref[...])
        dot = functools.partial(lax.dot_general, dimension_numbers=_DIMS["nn"], preferred_element_type=F32)
        oh = o_ref[...]
        out_ref[...] = dot(a, oh) + (dot(b, oh) + dot(c, oh))

    full = lambda shp: pl.BlockSpec(shp, lambda: (0,) * len(shp))
    return _pcall(body, name="bias_expand", in_specs=[full(rel_t.shape), full(onehot_t.shape)],
                  out_specs=full((h, n)), out_shape=jax.ShapeDtypeStruct((h, n), F32),
                  compiler_params=pltpu.CompilerParams(vmem_limit_bytes=VMEM_LIMIT))(rel_t, onehot_t)


def _bias_reduce(dbias, onehot, dsink_rows):
    h, n = dbias.shape

    def body(d_ref, o_ref, s_ref, out_ref, so_ref):
        a, b, c = _split3(d_ref[...])
        dot = functools.partial(lax.dot_general, dimension_numbers=_DIMS["nn"], preferred_element_type=F32)
        oh = o_ref[...]
        out_ref[...] = dot(a, oh) + (dot(b, oh) + dot(c, oh))
        so_ref[...] = jnp.sum(s_ref[...], axis=-1, keepdims=True)

    full = lambda shp: pl.BlockSpec(shp, lambda: (0,) * len(shp))
    return _pcall(body, name="bias_reduce", in_specs=[full(dbias.shape), full(onehot.shape), full(dsink_rows.shape)],
                  out_specs=[full((h, REL_BUCKETS)), full((h, 1))],
                  out_shape=[jax.ShapeDtypeStruct((h, REL_BUCKETS), F32), jax.ShapeDtypeStruct((h, 1), F32)],
                  compiler_params=pltpu.CompilerParams(vmem_limit_bytes=VMEM_LIMIT))(dbias, onehot, dsink_rows)


PAIRS = SWA_GROUP // 2
PROWS = PAIRS * BLOCK
PCOLS = 2 * 2 * BLOCK


def _swa2_specs():
    tok = lambda width: pl.BlockSpec((BLOCK, width), lambda g, n: (n, g))
    prev = pl.BlockSpec((None, BLOCK, 2 * SWA_DH), lambda g, n: (g, jnp.maximum(n - 1, 0), 0))
    cur = pl.BlockSpec((None, BLOCK, 2 * SWA_DH), lambda g, n: (g, n, 0))
    bias_spec = pl.BlockSpec((None, PROWS, PCOLS), lambda g, n: (g, 0, 0))
    col_spec = pl.BlockSpec((None, PROWS, 1), lambda g, n: (g, 0, 0))
    lse_spec = pl.BlockSpec((None, None, PROWS, 1), lambda g, n: (g, n, 0, 0))
    return tok, prev, cur, bias_spec, col_spec, lse_spec


def _stack_pairs(blk):
    return jnp.concatenate([blk[:, p * 2 * SWA_DH:(p + 1) * 2 * SWA_DH] for p in range(PAIRS)], axis=0)


def _band(tp, tc, bp, bc):
    return jnp.concatenate([tp[...], tc[...], bp[...], bc[...]], axis=0)


def _swa2_scores(q_ref, kd, bias_ref, n):
    q2 = _stack_pairs(q_ref[...])
    s2 = lax.dot_general(q2, kd, _DIMS["nt"], preferred_element_type=F32) * SWA_SCALE + bias_ref[...]
    col = lax.broadcasted_iota(jnp.int32, s2.shape, 1)
    before_start = jnp.logical_and(n == 0, (col & (2 * BLOCK - 1)) < BLOCK)
    return q2, jnp.where(before_start, -jnp.inf, s2)


def _swa2_fwd(src, ktop, kbot, vtop, vbot, bias, sinks, comm=None):
    s = src.shape[0]
    nb = s // BLOCK
    tok, prev, cur, bias_spec, col_spec, lse_spec = _swa2_specs()

    def body(q_ref, ktp, ktc, kbp, kbc, vtp, vtc, vbp, vbc, bias_ref, sa_ref, sb_ref, o_ref, la_ref, lb_ref):
        n = pl.program_id(1)
        _, s2 = _swa2_scores(q_ref, _band(ktp, ktc, kbp, kbc), bias_ref, n)
        row = lax.broadcasted_iota(jnp.int32, (PCOLS, 2 * SWA_DH), 0)
        lane = lax.broadcasted_iota(jnp.int32, (PCOLS, 2 * SWA_DH), 1)
        ones = jnp.where(lane == row // (2 * BLOCK), 1.0, 0.0).astype(BF16)
        ps, ms, sinks_ = [], [], []
        for half, sink_ref in enumerate((sa_ref, sb_ref)):
            sc = s2[:, half * 2 * BLOCK:(half + 1) * 2 * BLOCK]
            m = jnp.maximum(jnp.max(sc, axis=-1, keepdims=True), sink_ref[...])
            ps.append(jnp.exp(sc - m).astype(BF16))
            ms.append(m)
        acc = lax.dot_general(jnp.concatenate(ps, axis=1), jnp.concatenate([_band(vtp, vtc, vbp, vbc), ones], axis=1),
                              _DIMS["nn"], preferred_element_type=F32)
        dens = []
        for half, (sink_ref, lse_ref) in enumerate(((sa_ref, la_ref), (sb_ref, lb_ref))):
            den = acc[:, 2 * SWA_DH + half:2 * SWA_DH + half + 1] + jnp.exp(sink_ref[...] - ms[half])
            lse_ref[...] = ms[half] + jnp.log(den)
            dens.append(den)
        lo = lax.broadcasted_iota(jnp.int32, (PROWS, 2 * SWA_DH), 1) < SWA_DH
        o2 = acc[:, 0:2 * SWA_DH] / jnp.where(lo, dens[0], dens[1])
        for p in range(PAIRS):
            o_ref[:, p * 2 * SWA_DH:(p + 1) * 2 * SWA_DH] = o2[p * BLOCK:(p + 1) * BLOCK].astype(BF16)

    lse_shape = jax.ShapeDtypeStruct((SWA_KV, nb, PROWS, 1), F32)
    return _pcall(
        body, comm=comm, name="swa_fwd", grid=(SWA_KV, nb),
        in_specs=[tok(PROWS), prev, cur, prev, cur, prev, cur, prev, cur, bias_spec, col_spec, col_spec],
        out_specs=[tok(PROWS), lse_spec, lse_spec],
        out_shape=[jax.ShapeDtypeStruct((s, SWA_HEADS * SWA_DH), BF16), lse_shape, lse_shape],
        compiler_params=_cparams(("parallel", "parallel")))(
            src, ktop, ktop, kbot, kbot, vtop, vtop, vbot, vbot, bias, sinks[0], sinks[1])


def _swa2_bwd(src, dsrc, ktop, kbot, vtop, vbot, lses, bias, sinks, comm=None):
    s = src.shape[0]
    nb = s // BLOCK
    tok, prev, cur, bias_spec, col_spec, lse_spec = _swa2_specs()
    lane_lo = lambda shape: lax.broadcasted_iota(jnp.int32, shape, 1) < SWA_DH

    def body(q_ref, do_ref, ktp, ktc, kbp, kbc, vtp, vtc, vbp, vbc, la_ref, lb_ref, bias_ref, sa_ref, sb_ref,
             dq_ref, dkp_ref, dkc_ref, dvp_ref, dvc_ref, dbias_ref, dsa_ref, dsb_ref):
        n = pl.program_id(1)

        @pl.when(n == 0)
        def _():
            dbias_ref[...] = jnp.zeros_like(dbias_ref)
            dsa_ref[...] = jnp.zeros_like(dsa_ref)
            dsb_ref[...] = jnp.zeros_like(dsb_ref)

        kd = _band(ktp, ktc, kbp, kbc)
        q2, s2 = _swa2_scores(q_ref, kd, bias_ref, n)
        do2 = _stack_pairs(do_ref[...]).astype(BF16)
        dp2 = lax.dot_general(do2, _band(vtp, vtc, vbp, vbc), _DIMS["nt"], preferred_element_type=F32)
        ps, dss = [], []
        for half, (sink_ref, lse_ref, dsink_ref) in enumerate(((sa_ref, la_ref, dsa_ref), (sb_ref, lb_ref, dsb_ref))):
            cols = slice(half * 2 * BLOCK, (half + 1) * 2 * BLOCK)
            lse_v = lse_ref[...]
            p = jnp.exp(s2[:, cols] - lse_v)
            dp = dp2[:, cols]
            delta = jnp.sum(p * dp, axis=-1, keepdims=True)
            ds = p * (dp - delta)
            dsink_ref[...] += -jnp.exp(sink_ref[...] - lse_v) * delta
            ps.append(p.astype(BF16))
            dss.append(ds)
        ds2 = jnp.concatenate(dss, axis=1)
        dbias_ref[...] += ds2
        dsb2 = (ds2 * SWA_SCALE).astype(BF16)
        dq2 = lax.dot_general(dsb2, kd, _DIMS["nn"], preferred_element_type=F32)
        for p in range(PAIRS):
            dq_ref[:, p * 2 * SWA_DH:(p + 1) * 2 * SWA_DH] = dq2[p * BLOCK:(p + 1) * BLOCK].astype(BF16)
        dk = lax.dot_general(dsb2, q2, _DIMS["tn"], preferred_element_type=F32)
        dv = lax.dot_general(jnp.concatenate(ps, axis=1), do2, _DIMS["tn"], preferred_element_type=F32)
        for full, prev_ref, cur_ref in ((dk, dkp_ref, dkc_ref), (dv, dvp_ref, dvc_ref)):
            own = jnp.where(lane_lo((2 * BLOCK, 2 * SWA_DH)), full[:2 * BLOCK], full[2 * BLOCK:])
            prev_ref[...] = own[:BLOCK]
            cur_ref[...] = own[BLOCK:]

    kv_out = jax.ShapeDtypeStruct((SWA_KV, s, 2 * SWA_DH), F32)
    col_out = jax.ShapeDtypeStruct((SWA_KV, PROWS, 1), F32)
    return _pcall(
        body, comm=comm, name="swa_bwd", grid=(SWA_KV, nb),
        in_specs=[tok(PROWS), tok(PROWS), prev, cur, prev, cur, prev, cur, prev, cur, lse_spec, lse_spec, bias_spec,
                  col_spec, col_spec],
        out_specs=[tok(PROWS), cur, cur, cur, cur, bias_spec, col_spec, col_spec],
        out_shape=[jax.ShapeDtypeStruct((s, SWA_HEADS * SWA_DH), BF16), kv_out, kv_out, kv_out, kv_out,
                   jax.ShapeDtypeStruct(bias.shape, F32), col_out, col_out],
        compiler_params=_cparams(("arbitrary", "arbitrary")))(
            src, dsrc, ktop, ktop, kbot, kbot, vtop, vtop, vbot, vbot, lses[0], lses[1], bias, sinks[0], sinks[1])


def _rope_slab(slab, table):
    t = slab * table
    return t + pltpu.roll(t, ROPE, 1)


def _low_lanes(v):
    lane = lax.broadcasted_iota(jnp.int32, v.shape, 1)
    return jnp.where(lane < ROPE, v, 0.0)


def _rms(xv, g):
    r = lax.rsqrt(jnp.mean(xv * xv, axis=-1, keepdims=True) + EPS)
    return xv * r, r


def _mla_prep(proj, gq, gkv, table):
    s = proj.shape[0]
    ts = _pick(s, ROW_TILE, 16)

    def body(p_ref, gq_ref, gkv_ref, t_ref, cq_ref, ckv_ref, kr_ref):
        xq, _ = _rms(p_ref[:, 0:Q_RANK], None)
        cq_ref[...] = (xq * gq_ref[...]).astype(BF16)
        xkv, _ = _rms(p_ref[:, Q_RANK:Q_RANK + KV_RANK], None)
        ckv_ref[...] = (xkv * gkv_ref[...]).astype(BF16)
        kr_ref[...] = _low_lanes(_rope_slab(p_ref[:, Q_RANK + KV_RANK:TAIL], t_ref[...]))

    return _pcall(
        body, name="mla_prep", grid=(s // ts,),
        in_specs=[pl.BlockSpec((ts, TAIL), lambda i: (i, TAIL0 // TAIL)), _vec(Q_RANK), _vec(KV_RANK), _row(ts, 2 * ROPE)],
        out_specs=[_row(ts, Q_RANK), _row(ts, KV_RANK), _row(ts, 2 * ROPE)],
        out_shape=[jax.ShapeDtypeStruct((s, Q_RANK), BF16), jax.ShapeDtypeStruct((s, KV_RANK), BF16),
                   jax.ShapeDtypeStruct((s, 2 * ROPE), F32)],
        compiler_params=_cparams(("parallel",)))(proj, gq, gkv, table)


def _mla_prep_bwd(proj, dcq, dckv, dkr, gq, gkv, table):
    s = proj.shape[0]
    ts = _pick(s, ROW_TILE, 16)

    def norm_bwd(xv, dy, g):
        xn, r = _rms(xv, None)
        dg = jnp.sum(dy * xn, axis=0, keepdims=True)
        dxn = dy * g
        return r * (dxn - xn * jnp.mean(dxn * xn, axis=-1, keepdims=True)), dg

    def body(p_ref, dcq_ref, dckv_ref, dkr_ref, gq_ref, gkv_ref, t_ref, dt_ref, dgq_ref, dgkv_ref):
        @pl.when(pl.program_id(0) == 0)
        def _():
            dgq_ref[...] = jnp.zeros_like(dgq_ref)
            dgkv_ref[...] = jnp.zeros_like(dgkv_ref)

        dxq, dgq = norm_bwd(p_ref[:, 0:Q_RANK], dcq_ref[...], gq_ref[...])
        dxkv, dgkv = norm_bwd(p_ref[:, Q_RANK:Q_RANK + KV_RANK], dckv_ref[...], gkv_ref[...])
        dgq_ref[...] += dgq
        dgkv_ref[...] += dgkv
        d = _low_lanes(dkr_ref[...])
        dslab = (d + pltpu.roll(d, ROPE, 1)) * t_ref[...]
        dt_ref[:, 0:Q_RANK] = dxq.astype(BF16)
        dt_ref[:, Q_RANK:Q_RANK + KV_RANK] = dxkv.astype(BF16)
        dt_ref[:, Q_RANK + KV_RANK:TAIL] = dslab.astype(BF16)

    return _pcall(
        body, name="mla_prep_bwd", grid=(s // ts,),
        in_specs=[pl.BlockSpec((ts, TAIL), lambda i: (i, TAIL0 // TAIL)), _row(ts, Q_RANK), _row(ts, KV_RANK),
                  _row(ts, 2 * ROPE), _vec(Q_RANK), _vec(KV_RANK), _row(ts, 2 * ROPE)],
        out_specs=[_row(ts, TAIL), _vec(Q_RANK), _vec(KV_RANK)],
        out_shape=[jax.ShapeDtypeStruct((s, TAIL), BF16), jax.ShapeDtypeStruct((1, Q_RANK), F32),
                   jax.ShapeDtypeStruct((1, KV_RANK), F32)],
        compiler_params=_cparams(("arbitrary",)))(proj, dcq, dckv, dkr, gq, gkv, table)


def _head_specs(ts):
    tok = lambda w: pl.BlockSpec((ts, w), lambda h, i: (i, 0))
    head = lambda w: pl.BlockSpec((None, ts, w), lambda h, i: (h, i, 0))
    wgt = lambda r, c: pl.BlockSpec((None, r, c), lambda h, i: (h, 0, 0))
    return tok, head, wgt


def _mla_qkv(cq, ckv, kr, wq, wkv, table):
    s = cq.shape[0]
    ts = _pick(s, 4 * ROW_TILE, 16)
    tok, head, wgt = _head_specs(ts)

    def body(cq_ref, ckv_ref, kr_ref, wq_ref, wkv_ref, t_ref, q_ref, k_ref, v_ref):
        qf = lax.dot_general(cq_ref[...], wq_ref[...], _DIMS["nn"], preferred_element_type=F32)
        q_ref[:, 0:NOPE] = qf[:, 0:NOPE].astype(BF16)
        q_ref[:, NOPE:QW] = _rope_slab(qf[:, NOPE:QW], t_ref[...]).astype(BF16)
        kv = lax.dot_general(ckv_ref[...], wkv_ref[...], _DIMS["nn"], preferred_element_type=F32)
        k_ref[:, 0:NOPE] = kv[:, 0:NOPE].astype(BF16)
        k_ref[:, NOPE:QW] = kr_ref[...].astype(BF16)
        v_ref[:, 0:VDIM] = kv[:, NOPE:NOPE + VDIM].astype(BF16)
        lane = lax.broadcasted_iota(jnp.int32, (ts, VDIM), 1)
        v_ref[:, VDIM:2 * VDIM] = jnp.where(lane == 0, 1.0, 0.0).astype(BF16)

    return _pcall(
        body, name="mla_qkv", grid=(MLA_H, s // ts),
        in_specs=[tok(Q_RANK), tok(KV_RANK), tok(2 * ROPE), wgt(Q_RANK, QW), wgt(KV_RANK, NOPE + VDIM), tok(2 * ROPE)],
        out_specs=[head(QW), head(QW), head(2 * VDIM)],
        out_shape=[jax.ShapeDtypeStruct((MLA_H, s, QW), BF16), jax.ShapeDtypeStruct((MLA_H, s, QW), BF16),
                   jax.ShapeDtypeStruct((MLA_H, s, 2 * VDIM), BF16)],
        compiler_params=_cparams(("parallel", "parallel")))(cq, ckv, kr, wq, wkv, table)


def _mla_qkv_bwd(dq, dk, dv, cq, ckv, wq, wkv, table, comm=None):
    s = cq.shape[0]
    ts = _pick(s, 4 * ROW_TILE, 16)
    tok, head, wgt = _head_specs(ts)
    whole = lambda w: pl.BlockSpec((s, w), lambda h, i: (0, 0))

    def body(dq_ref, dk_ref, dv_ref, cq_ref, ckv_ref, wq_ref, wkv_ref, t_ref,
             dcq_ref, dckv_ref, dkr_ref, gwq_ref, gwkv_ref):
        h, i = pl.program_id(0), pl.program_id(1)
        rows = pl.ds(pl.multiple_of(i * ts, ts), ts)
        d = dq_ref[:, NOPE:QW]
        dslab = (d + pltpu.roll(d, ROPE, 1)) * t_ref[...]
        dqe = jnp.concatenate([dq_ref[:, 0:NOPE], dslab], axis=1).astype(BF16)
        dkv = jnp.concatenate([dk_ref[:, 0:NOPE], dv_ref[...]], axis=1).astype(BF16)
        dcq = lax.dot_general(dqe, wq_ref[...], _DIMS["nt"], preferred_element_type=F32)
        dckv = lax.dot_general(dkv, wkv_ref[...], _DIMS["nt"], preferred_element_type=F32)
        gwq = lax.dot_general(cq_ref[...], dqe, _DIMS["tn"], preferred_element_type=F32)
        gwkv = lax.dot_general(ckv_ref[...], dkv, _DIMS["tn"], preferred_element_type=F32)
        dkr = dk_ref[:, NOPE:QW].astype(F32)

        @pl.when(h == 0)
        def _():
            dcq_ref[rows, :] = dcq
            dckv_ref[rows, :] = dckv
            dkr_ref[rows, :] = dkr

        @pl.when(h > 0)
        def _():
            dcq_ref[rows, :] += dcq
            dckv_ref[rows, :] += dckv
            dkr_ref[rows, :] += dkr

        @pl.when(i == 0)
        def _():
            gwq_ref[...] = gwq
            gwkv_ref[...] = gwkv

        @pl.when(i > 0)
        def _():
            gwq_ref[...] += gwq
            gwkv_ref[...] += gwkv

    return _pcall(
        body, comm=comm, name="mla_qkv_bwd", grid=(MLA_H, s // ts),
        in_specs=[head(QW), head(QW), head(VDIM), tok(Q_RANK), tok(KV_RANK), wgt(Q_RANK, QW),
                  wgt(KV_RANK, NOPE + VDIM), tok(2 * ROPE)],
        out_specs=[whole(Q_RANK), whole(KV_RANK), whole(2 * ROPE), wgt(Q_RANK, QW), wgt(KV_RANK, NOPE + VDIM)],
        out_shape=[jax.ShapeDtypeStruct((s, Q_RANK), F32), jax.ShapeDtypeStruct((s, KV_RANK), F32),
                   jax.ShapeDtypeStruct((s, 2 * ROPE), F32), jax.ShapeDtypeStruct((MLA_H, Q_RANK, QW), F32),
                   jax.ShapeDtypeStruct((MLA_H, KV_RANK, NOPE + VDIM), F32)],
        compiler_params=_cparams(("arbitrary", "arbitrary")))(dq, dk, dv, cq, ckv, wq, wkv, table)


def _diag_mask(t):
    return lax.broadcasted_iota(jnp.int32, (t, t), 1) <= lax.broadcasted_iota(jnp.int32, (t, t), 0)


def _mla_fwd(q, k, v, comm=None):
    s = q.shape[1]
    t = _pick(s, ATT_T, 128)
    nt = s // t
    assert nt % 2 == 0
    hb = MLA_H

    def fold(p, u):
        first = u <= p
        return jnp.where(first, p, nt - 1 - p), jnp.where(first, u, u - p - 1)

    to_log2 = MLA_SCALE * math.log2(math.e)

    def body(q_ref, k_ref, v_ref, o_ref, oh_ref, lse_ref, m_ref, acc_ref):
        i, j = fold(pl.program_id(1), pl.program_id(2))

        @pl.when(j == 0)
        def _():
            m_ref[...] = jnp.full_like(m_ref, -jnp.inf)
            acc_ref[...] = jnp.zeros_like(acc_ref)

        def step(diagonal):
            for h in range(hb):
                sc = lax.dot_general(q_ref[h], k_ref[h], _DIMS["nt"], preferred_element_type=F32)
                if diagonal:
                    sc = jnp.where(_diag_mask(t), sc, -jnp.inf)
                m_old = m_ref[h]
                m_new = jnp.maximum(m_old, jnp.max(sc, axis=-1, keepdims=True))
                alpha = jnp.exp2((m_old - m_new) * to_log2)
                p = jnp.exp2((sc - m_new) * to_log2)
                acc_ref[h] = alpha * acc_ref[h] + lax.dot_general(p.astype(BF16), v_ref[h], _DIMS["nn"],
                                                                  preferred_element_type=F32)
                m_ref[h] = m_new

        @pl.when(j < i)
        def _():
            step(False)

        @pl.when(j == i)
        def _():
            step(True)
            for h in range(hb):
                den = acc_ref[h, :, VDIM:VDIM + 1]
                o = acc_ref[h, :, 0:VDIM] / den
                o_ref[:, h * VDIM:(h + 1) * VDIM] = o
                oh_ref[:, h * VDIM:(h + 1) * VDIM] = o.astype(BF16)
                lse_ref[h] = m_ref[h] * MLA_SCALE + jnp.log(den)

    o_spec = pl.BlockSpec((t, hb * VDIM), lambda h, p, u: (fold(p, u)[0], h))
    return _pcall(
        body, comm=comm, name="mla_fwd", grid=(MLA_H // hb, nt // 2, nt + 1),
        in_specs=[pl.BlockSpec((hb, t, QW), lambda h, p, u: (h, fold(p, u)[0], 0)),
                  pl.BlockSpec((hb, t, QW), lambda h, p, u: (h, fold(p, u)[1], 0)),
                  pl.BlockSpec((hb, t, 2 * VDIM), lambda h, p, u: (h, fold(p, u)[1], 0))],
        out_specs=[o_spec, o_spec, pl.BlockSpec((hb, t, 1), lambda h, p, u: (h, fold(p, u)[0], 0))],
        out_shape=[jax.ShapeDtypeStruct((s, MLA_H * VDIM), F32), jax.ShapeDtypeStruct((s, MLA_H * VDIM), BF16),
                   jax.ShapeDtypeStruct((MLA_H, s, 1), F32)],
        scratch_shapes=[pltpu.VMEM((hb, t, 1), F32), pltpu.VMEM((hb, t, 2 * VDIM), F32)],
        compiler_params=_cparams(("parallel", "parallel", "arbitrary")))(q, k, v)


def _mla_delta(dmix, o):
    s = o.shape[0]
    ts = _pick(s, 2 * ROW_TILE, 16)
    w = MLA_H * VDIM

    def body(d_ref, o_ref, out_ref):
        prod = d_ref[...] * o_ref[...]
        for h in range(MLA_H):
            out_ref[h] = jnp.sum(prod[:, h * VDIM:(h + 1) * VDIM], axis=-1, keepdims=True)

    return _pcall(body, name="mla_delta", grid=(s // ts,),
                  in_specs=[pl.BlockSpec((ts, w), lambda i: (i, SWA_HEADS * SWA_DH // w)), pl.BlockSpec((ts, w), lambda i: (i, 0))],
                  out_specs=pl.BlockSpec((MLA_H, ts, 1), lambda i: (0, i, 0)),
                  out_shape=jax.ShapeDtypeStruct((MLA_H, s, 1), F32), compiler_params=_cparams(("parallel",)))(dmix, o)


def _mla_bwd(q, k, v, dmix, delta, lse, comm=None):
    s = q.shape[1]
    t = _pick(s, ATT_T, 128)
    nt = s // t
    assert nt % 2 == 0
    hb = 2 * MLA_HB
    o_blk0 = SWA_HEADS * SWA_DH // (hb * VDIM)

    def fold(p, u):
        first = u < nt - p
        return jnp.where(first, p, nt - 1 - p), jnp.where(first, p + u, u - 1)

    log2e = math.log2(math.e)

    def body(q_ref, k_ref, v_ref, do_ref, delta_ref, lse_ref, dq_ref, dk_ref, dv_ref, dk_acc, dv_acc):
        j, i = fold(pl.program_id(1), pl.program_id(2))
        rows = pl.ds(pl.multiple_of(i * t, t), t)

        @pl.when(i == j)
        def _():
            dk_acc[...] = jnp.zeros_like(dk_acc)
            dv_acc[...] = jnp.zeros_like(dv_acc)

        def step(diagonal):
            for h in range(hb):
                qv, kv_ = q_ref[h], k_ref[h]
                dob = do_ref[:, h * VDIM:(h + 1) * VDIM].astype(BF16)
                st = lax.dot_general(kv_, qv, _DIMS["nt"], preferred_element_type=F32)
                pt = jnp.exp2(st * (MLA_SCALE * log2e) - lse_ref[h] * log2e)
                if diagonal:
                    keep = lax.broadcasted_iota(jnp.int32, (t, t), 0) <= lax.broadcasted_iota(jnp.int32, (t, t), 1)
                    pt = jnp.where(keep, pt, 0.0)
                dpt = lax.dot_general(v_ref[h], dob, _DIMS["nt"], preferred_element_type=F32)
                dst = (pt * (dpt - delta_ref[h]) * MLA_SCALE).astype(BF16)
                dv_acc[h] += lax.dot_general(pt.astype(BF16), dob, _DIMS["nn"], preferred_element_type=F32)
                dk_acc[h] += lax.dot_general(dst, qv, _DIMS["nn"], preferred_element_type=F32)
                dqv = lax.dot_general(dst, kv_, _DIMS["tn"], preferred_element_type=F32)

                @pl.when(j == 0)
                def _():
                    dq_ref[h, rows, :] = dqv

                @pl.when(j > 0)
                def _():
                    dq_ref[h, rows, :] += dqv

        @pl.when(i > j)
        def _():
            step(False)

        @pl.when(i == j)
        def _():
            step(True)

        @pl.when(i == nt - 1)
        def _():
            dk_ref[...] = dk_acc[...].astype(BF16)
            dv_ref[...] = dv_acc[...].astype(BF16)

    qi = lambda h, p, u: (h, fold(p, u)[1], 0)
    kj = lambda h, p, u: (h, fold(p, u)[0], 0)
    row = pl.BlockSpec((hb, 1, t), lambda h, p, u: (h, 0, fold(p, u)[1]))
    return _pcall(
        body, comm=comm, name="mla_bwd", grid=(MLA_H // hb, nt // 2, nt + 1),
        in_specs=[pl.BlockSpec((hb, t, QW), qi), pl.BlockSpec((hb, t, QW), kj), pl.BlockSpec((hb, t, VDIM), kj),
                  pl.BlockSpec((t, hb * VDIM), lambda h, p, u: (fold(p, u)[1], o_blk0 + h)), row, row],
        out_specs=[pl.BlockSpec((hb, s, QW), lambda h, p, u: (h, 0, 0)), pl.BlockSpec((hb, t, QW), kj),
                   pl.BlockSpec((hb, t, VDIM), kj)],
        out_shape=[jax.ShapeDtypeStruct((MLA_H, s, QW), F32), jax.ShapeDtypeStruct((MLA_H, s, QW), BF16),
                   jax.ShapeDtypeStruct((MLA_H, s, VDIM), BF16)],
        scratch_shapes=[pltpu.VMEM((hb, t, QW), F32), pltpu.VMEM((hb, t, VDIM), F32)],
        compiler_params=_cparams(("arbitrary", "arbitrary", "arbitrary")))(q, k, v, dmix, delta, lse)


def _adamw(name, w, g, m, v, parts):
    r, c = w.shape
    n_parts = g.shape[0] if parts else 1
    tr = r if r * c <= ADAM_ELEMS else _pick(r, max(8, ADAM_ELEMS // c // 8 * 8), 8)
    c1 = 1.0 - ADAM_B1 ** ADAM_STEP
    c2 = 1.0 - ADAM_B2 ** ADAM_STEP

    def body(w_ref, g_ref, m_ref, v_ref, go_ref, d_ref, mo_ref, vo_ref):
        if parts:
            gv = g_ref[0].astype(F32)
            for j in range(1, n_parts):
                gv = gv + g_ref[j].astype(F32)
        else:
            gv = g_ref[...]
        mv = ADAM_B1 * m_ref[...] + (1.0 - ADAM_B1) * gv
        vv = ADAM_B2 * v_ref[...] + (1.0 - ADAM_B2) * (gv * gv)
        go_ref[...] = gv
        mo_ref[...] = mv
        vo_ref[...] = vv
        d_ref[...] = -ADAM_LR * ((mv / c1) / (jnp.sqrt(vv / c2) + ADAM_EPS) + ADAM_WD * w_ref[...])

    blk = pl.BlockSpec((tr, c), lambda i: (i, 0))
    g_spec = pl.BlockSpec((n_parts, tr, c), lambda i: (0, i, 0)) if parts else blk
    out = jax.ShapeDtypeStruct((r, c), F32)
    return _pcall(body, name=name, grid=(r // tr,), in_specs=[blk, g_spec, blk, blk], out_specs=[blk] * 4,
                  out_shape=[out] * 4, compiler_params=_cparams(("parallel",)))(w, g, m, v)


def _t5_bucket(dist):
    n = jnp.maximum(dist, 0)
    max_exact = REL_BUCKETS // 2
    nf = jnp.maximum(n, 1).astype(F32)
    large = max_exact + (jnp.log(nf / max_exact) / math.log(REL_MAX_DIST / max_exact)
                         * (REL_BUCKETS - max_exact)).astype(jnp.int32)
    return jnp.where(n < max_exact, n, jnp.minimum(large, REL_BUCKETS - 1))


def _swap_halves(w, r0):
    return jnp.concatenate([w[:, r0 + ROPE // 2:r0 + ROPE], w[:, r0:r0 + ROPE // 2]], axis=1)


def _fold_swapped(g, r0, width):
    sw = g[..., width:width + ROPE]
    half = ROPE // 2
    return jnp.concatenate([g[..., :r0], g[..., r0:r0 + half] + sw[..., half:], g[..., r0 + half:r0 + ROPE] + sw[..., :half],
                            g[..., r0 + ROPE:width]], axis=-1)


def kernel(x, c, w_mod, b_mod, attn_norm_g, w_in, swa_sinks, rel_bias, mla_q_norm_g, w_uq, mla_kv_norm_g, w_ukv, w_out, mlp_norm_g, w_ff1, w_ff2, final_norm_g, loss_target, m_w_mod, m_b_mod, m_attn_norm_g, m_w_in, m_swa_sinks, m_rel_bias, m_mla_q_norm_g, m_w_uq, m_mla_kv_norm_g, m_w_ukv, m_w_out, m_mlp_norm_g, m_w_ff1, m_w_ff2, m_final_norm_g, v_w_mod, v_b_mod, v_attn_norm_g, v_w_in, v_swa_sinks, v_rel_bias, v_mla_q_norm_g, v_w_uq, v_mla_kv_norm_g, v_w_ukv, v_w_out, v_mlp_norm_g, v_w_ff1, v_w_ff2, v_final_norm_g):
    s, d = x.shape[1], x.shape[2]
    ffs = w_ff1.shape[2]
    ff = ffs * NDEV
    nmod = w_mod.shape[2]
    me = 4 * lax.axis_index("x") + 2 * lax.axis_index("y") + lax.axis_index("c")
    x2d, tgt = x[0], loss_target[0]
    final_g = final_norm_g.reshape(1, d)

    w_in_l = jnp.concatenate([w_in[0], _swap_halves(w_in[0], OFF_KR)], axis=1).astype(BF16)
    w_uq_l = jnp.concatenate([w_uq[0], _swap_halves(w_uq[0], NOPE)], axis=1).astype(BF16)
    core = jnp.full((1, 128), lax.axis_index("c"), F32)
    (c_all,) = _exchange("gather_c", _Gather([c]))

    b_cols = lax.dynamic_slice(b_mod, (0, me * nmod), (1, nmod))
    act_all, mod_cols = _mod_fwd(c_all.reshape(NDEV, d), w_mod[0], b_cols)
    (mod_g,) = _exchange("gather_mod", _Gather([mod_cols]))
    mod = lax.dynamic_index_in_dim(mod_g, me, axis=1, keepdims=False).reshape(1, 6 * d)
    sh1, sc1, g1, sh2, sc2, g2 = [mod[:, i * d:(i + 1) * d] for i in range(6)]

    pos = jnp.arange(s, dtype=F32)
    inv_freq = ROPE_THETA ** (-jnp.arange(ROPE // 2, dtype=F32) / (ROPE // 2))
    ang = pos[:, None] * inv_freq[None, :]
    cos, sin = jnp.cos(ang), jnp.sin(ang)
    table = jnp.concatenate([cos, cos, -sin, sin], axis=1)
    q_loc = jnp.arange(BLOCK)[:, None]
    k_loc = jnp.arange(2 * BLOCK)[None, :]
    dist = q_loc + BLOCK - k_loc
    in_window = (dist >= 0) & (dist < BLOCK)
    onehot = (_t5_bucket(dist).reshape(-1, 1) == jnp.arange(REL_BUCKETS)[None, :]).astype(BF16)
    bias = _bias_expand(rel_bias.T, onehot.T).reshape(SWA_HEADS, BLOCK, 2 * BLOCK)
    bias = jnp.where(in_window[None], bias, -jnp.inf).reshape(SWA_KV, PAIRS, 2, BLOCK, 2 * BLOCK)
    bias = bias.transpose(0, 1, 3, 2, 4).reshape(SWA_KV, PROWS, PCOLS)
    sinks = jnp.broadcast_to(swa_sinks.reshape(SWA_KV, PAIRS, 1, 2), (SWA_KV, PAIRS, BLOCK, 2)).reshape(SWA_KV, PROWS, 2)
    sinks = (sinks[:, :, 0:1], sinks[:, :, 1:2])

    h1, w_in_g, w_uq_g, w_ukv_g = _norm_mod("norm1", x2d, attn_norm_g, sc1, sh1,
                                            comm=[_Gather([w_in_l, w_uq_l, w_ukv[0].astype(BF16)])])
    w_in_e = w_in_g.reshape(d, IN_EXT)

    def both_dtypes(acc, ex, outs):
        outs[0][...] = acc
        outs[1][...] = acc.astype(BF16)

    tmp = _pick(s, MM_TM // 2, 128)
    proj_blk = pl.BlockSpec((tmp, IN_EXT), lambda i, j, q: (i, 0))
    proj, proj_h = _mm("proj", h1, w_in_e, "nn", (s // tmp, 1, 1), pl.BlockSpec((tmp, d), lambda i, j, q: (i, 0)),
                       pl.BlockSpec((d, IN_EXT), lambda i, j, q: (0, 0)),
                       [jax.ShapeDtypeStruct((s, IN_EXT), F32), jax.ShapeDtypeStruct((s, IN_EXT), BF16)],
                       [proj_blk, proj_blk], (tmp, IN_EXT), both_dtypes)
    def diag_pair(tok):
        x = jnp.stack([tok[:, :SWA_DH], tok[:, SWA_DH:]])
        zero = jnp.zeros_like(x)
        return jnp.concatenate([x, zero], axis=2), jnp.concatenate([zero, x], axis=2)

    k_top, k_bot = diag_pair(proj_h[:, OFF_K:OFF_V])
    v_top, v_bot = diag_pair(proj_h[:, OFF_V:OFF_CQ])
    o_a, lse_a0, lse_a1, w_out_g = _swa2_fwd(proj_h, k_top, k_bot, v_top, v_bot, bias, sinks,
                                             comm=[_Gather([w_out[0].astype(BF16)])])
    w_out_f = w_out_g.reshape(MIX, d)

    cq, ckv, kr = _mla_prep(proj, mla_q_norm_g, mla_kv_norm_g, table)
    q_b, k_b, v_b = _mla_qkv(cq, ckv, kr, w_uq_g, w_ukv_g, table)
    o_b, o_bh, lse_b, w_ff1_g = _mla_fwd(q_b, k_b, v_b, comm=[_Gather([w_ff1[0].astype(BF16)])])
    mix = jnp.concatenate([o_a, o_bh], axis=1)

    tm, tn, tk = _pick(s, MM_TM, 128), _pick(d, MM_TN, 128), _pick(MIX, MM_TK, 128)
    row_blk = pl.BlockSpec((tm, tn), lambda i, j, q: (i, j))
    gate_blk = pl.BlockSpec((1, tn), lambda i, j, q: (0, j))

    def gated_residual(acc, ex, outs):
        outs[0][...] = acc
        outs[1][...] = ex[0][...] + ex[1][...] * acc

    y1, x2 = _mm("out_proj", mix, w_out_f, "nn", (s // tm, d // tn, MIX // tk),
                 pl.BlockSpec((tm, tk), lambda i, j, q: (i, q)), pl.BlockSpec((tk, tn), lambda i, j, q: (q, j)),
                 [jax.ShapeDtypeStruct((s, d), F32)] * 2, [row_blk, row_blk], (tm, tn), gated_residual,
                 extras=(x2d, g1), extra_specs=(row_blk, gate_blk))

    (h2,) = _norm_mod("norm2", x2, mlp_norm_g, sc2, sh2)
    tnf, tkd = _pick(ffs, MM_TN, 128), _pick(d, MM_TK, 128)
    rf = ffs // tnf
    ff_blk = pl.BlockSpec((tm, tnf), lambda i, j, q: (i, j))

    def relu_sq(acc, ex, outs):
        u = jnp.maximum(acc, 0.0)
        outs[0][...] = u
        outs[1][...] = (u * u).astype(BF16)

    u, uu, w_ff2_g = _mm("ff1", h2, w_ff1_g, "nn", (s // tm, ff // tnf, d // tkd),
                         pl.BlockSpec((tm, tkd), lambda i, j, q: (i, q)),
                         pl.BlockSpec((None, tkd, tnf), lambda i, j, q: (j // rf, q, j % rf)),
                         [jax.ShapeDtypeStruct((s, ff), F32), jax.ShapeDtypeStruct((s, ff), BF16)], [ff_blk, ff_blk],
                         (tm, tnf), relu_sq, comm=[_Gather([w_ff2[0].astype(BF16)])])
    w_ff2_f = w_ff2_g.reshape(ff, d)
    tkf = _pick(ff, MM_TK, 128)
    y2, x3 = _mm("ff2", uu, w_ff2_f, "nn", (s // tm, d // tn, ff // tkf),
                 pl.BlockSpec((tm, tkf), lambda i, j, q: (i, q)), pl.BlockSpec((tkf, tn), lambda i, j, q: (q, j)),
                 [jax.ShapeDtypeStruct((s, d), F32)] * 2, [row_blk, row_blk], (tm, tn), gated_residual,
                 extras=(x2, g2), extra_specs=(row_blk, gate_blk))

    dx3, dy2, loss_p, dgf, dg2 = _loss_head(x3, tgt, y2, final_g, g2)

    def relu_sq_bwd(acc, ex, outs):
        outs[0][...] = (acc * (2.0 * ex[0][...])).astype(BF16)

    tnf2 = _pick(ff, MM_TN, 128)
    du = _mm("ff2_dx", dy2, w_ff2_f, "nt", (s // tm, ff // tnf2, d // tkd),
             pl.BlockSpec((tm, tkd), lambda i, j, q: (i, q)), pl.BlockSpec((tnf2, tkd), lambda i, j, q: (j, q)),
             [jax.ShapeDtypeStruct((s, ff), BF16)], [pl.BlockSpec((tm, tnf2), lambda i, j, q: (i, j))],
             (tm, tnf2), relu_sq_bwd, extras=(u,), extra_specs=(pl.BlockSpec((tm, tnf2), lambda i, j, q: (i, j)),))[0]
    gw_ff2 = _mm_plain("ff2_dw", uu, dy2, "tn", ff, d, s, BF16)
    tmd, tks = _pick(d, MM_TM, 128), _pick(s, MM_TK, 128)
    gw_ff2 = gw_ff2.reshape(NDEV, ffs, d)
    dh2, s_ff2 = _mm("ff1_dx", du, w_ff1_g, "nt", (s // tm, d // tn, NDEV // 2),
                     pl.BlockSpec((tm, 2 * ffs), lambda i, j, q: (i, q)),
                     pl.BlockSpec((2, tn, ffs), lambda i, j, q: (q, j, 0)),
                     [jax.ShapeDtypeStruct((s, d), F32)], [row_blk], (tm, tn), _store(F32),
                     comm=[_PairSwap([gw_ff2])], b_parts=2)
    c_ff2 = _pair_sum("pair_ff2", gw_ff2, s_ff2, core)
    gw_ff1, p_ff2 = _mm("ff1_dw", h2, du, "tn", (d // tmd, ff // tnf, s // tks),
                        pl.BlockSpec((tks, tmd), lambda i, j, q: (q, i)), pl.BlockSpec((tks, tnf), lambda i, j, q: (q, j)),
                        [jax.ShapeDtypeStruct((NDEV, d, ffs), BF16)],
                        [pl.BlockSpec((None, tmd, tnf), lambda i, j, q: (j // rf, i, j % rf))], (tmd, tnf), _store(BF16),
                        comm=[_ChipScatter([c_ff2])])
    dx2, dy1, dsc2, dsh2, dgm, dg1, s_ff1 = _norm_mod_bwd("norm2_bwd", x2, dh2, dx3, mlp_norm_g, sc2, y1, g1,
                                                          comm=[_PairSwap([gw_ff1])])
    c_ff1 = _pair_sum("pair_ff1", gw_ff1, s_ff1, core)

    dmix = _mm_plain("out_proj_dx", dy1, w_out_f, "nt", s, MIX, d, F32)
    gw_out = _mm_plain("out_proj_dw", mix, dy1, "tn", MIX, d, s, BF16).reshape(NDEV, MIX // NDEV, d)

    delta_b = _mla_delta(dmix, o_b).reshape(MLA_H, 1, s)
    dq_b, dk_b, dv_b, p_ff1, s_out = _mla_bwd(q_b, k_b, v_b, dmix, delta_b, lse_b.reshape(MLA_H, 1, s),
                                              comm=[_ChipScatter([c_ff1]), _PairSwap([gw_out])])
    c_out = _pair_sum("pair_out", gw_out, s_out, core)
    dcq, dckv, dkr, gw_uq_e, gw_ukv, p_out = _mla_qkv_bwd(dq_b, dk_b, dv_b, cq, ckv, w_uq_g, w_ukv_g, table,
                                                          comm=[_ChipScatter([c_out])])
    dtail, dgq, dgkv = _mla_prep_bwd(proj, dcq, dckv, dkr, mla_q_norm_g, mla_kv_norm_g, table)
    gw_uq = _fold_swapped(gw_uq_e, NOPE, NOPE + ROPE).astype(BF16)
    gw_ukv = gw_ukv.astype(BF16)

    dq_a, dkp, dkc, dvp, dvc, dbias, dsink0, dsink1, s_uq, s_ukv = _swa2_bwd(
        proj_h, dmix, k_top, k_bot, v_top, v_bot, (lse_a0, lse_a1), bias, sinks, comm=[_PairSwap([gw_uq, gw_ukv])])
    c_uq = _pair_sum("pair_uq", gw_uq, s_uq, core)
    c_ukv = _pair_sum("pair_ukv", gw_ukv, s_ukv, core)

    def band_grad(cur, prv):
        g = cur + jnp.concatenate([prv[:, BLOCK:], jnp.zeros_like(prv[:, :BLOCK])], axis=1)
        g = g[:, :, :SWA_DH] + g[:, :, SWA_DH:]
        return jnp.concatenate([g[0], g[1]], axis=1)

    dbias = dbias.reshape(SWA_KV, PAIRS, BLOCK, 2, 2 * BLOCK).transpose(0, 1, 3, 2, 4)
    dsink = jnp.stack([dsink0.reshape(SWA_KV, PAIRS, BLOCK), dsink1.reshape(SWA_KV, PAIRS, BLOCK)], axis=2)
    drel_t, dsinks = _bias_reduce(dbias.reshape(SWA_HEADS, BLOCK * 2 * BLOCK), onehot, dsink.reshape(SWA_HEADS, BLOCK))
    dproj = jnp.concatenate([dq_a, band_grad(dkc, dkp).astype(BF16), band_grad(dvc, dvp).astype(BF16), dtail], axis=1)
    gw_in_e = _mm_plain("proj_dw", h1, dproj, "tn", d, IN_EXT, s, F32, tn=TAIL)
    gw_in = _fold_swapped(gw_in_e, OFF_KR, IN_COLS).reshape(NDEV, d // NDEV, IN_COLS).astype(BF16)
    tkt = IN_EXT
    dh1, s_in, p_uq, p_ukv = _mm(
        "proj_dx", dproj, w_in_e, "nt", (s // tm, d // tn, IN_EXT // tkt),
        pl.BlockSpec((tm, tkt), lambda i, j, q: (i, q)), pl.BlockSpec((tn, tkt), lambda i, j, q: (j, q)),
        [jax.ShapeDtypeStruct((s, d), F32)], [row_blk], (tm, tn), _store(F32),
        comm=[_PairSwap([gw_in]), _ChipScatter([c_uq, c_ukv])])
    c_in = _pair_sum("pair_in", gw_in, s_in, core)
    gx, dsc1, dsh1, dga, p_in = _norm_mod_bwd("norm1_bwd", x2d, dh1, dx2, attn_norm_g, sc1,
                                              comm=[_ChipScatter([c_in])])

    small = [jnp.concatenate([dsh1, dsc1, dg1, dsh2, dsc2, dg2], axis=1), dga, dgm, dgf, dgq, dgkv,
             dsinks.reshape(1, SWA_HEADS), drel_t.T.reshape(1, REL_BUCKETS * SWA_HEADS)]
    n_small = sum(a.shape[1] for a in small)
    n_pad = -(n_small + 1) % 1024 + 1
    rows_small = (n_small + n_pad) // 128
    pad = jnp.zeros((1, n_pad), F32)
    pack = lambda parts, tail=pad: jnp.concatenate([p.reshape(1, -1) for p in parts] + [tail], axis=1).reshape(rows_small, 128)
    (small_g,) = _exchange("gather_small", _Gather([pack(small, jnp.concatenate([loss_p, pad[:, 1:]], axis=1))]))
    small_names = (b_mod, attn_norm_g, mlp_norm_g, final_norm_g, mla_q_norm_g, mla_kv_norm_g, swa_sinks, rel_bias)
    small_m = (m_b_mod, m_attn_norm_g, m_mlp_norm_g, m_final_norm_g, m_mla_q_norm_g, m_mla_kv_norm_g, m_swa_sinks, m_rel_bias)
    small_v = (v_b_mod, v_attn_norm_g, v_mlp_norm_g, v_final_norm_g, v_mla_q_norm_g, v_mla_kv_norm_g, v_swa_sinks, v_rel_bias)
    small_out = _adamw("adamw_small", pack(small_names), small_g, pack(small_m), pack(small_v), parts=True)

    def unpack(flat):
        flat = flat.reshape(1, -1)
        out, off = [], 0
        for a in small_names:
            out.append(flat[:, off:off + a.size].reshape(a.shape))
            off += a.size
        return out

    sg, sd, sm, sv = [unpack(o) for o in small_out]
    loss = small_out[0].reshape(-1)[n_small]

    dmod_cols = lax.dynamic_slice(small_g.reshape(NDEV, -1), (0, me * nmod), (NDEV, nmod))
    gw_mod = _mod_wgrad(act_all, dmod_cols)
    big = {"w_mod": _adamw("adamw_w_mod", w_mod[0], gw_mod, m_w_mod[0], v_w_mod[0], parts=False)}

    for name, w, p, m, v in (("w_in", w_in, p_in, m_w_in, v_w_in), ("w_uq", w_uq, p_uq, m_w_uq, v_w_uq),
                             ("w_ukv", w_ukv, p_ukv, m_w_ukv, v_w_ukv), ("w_out", w_out, p_out, m_w_out, v_w_out),
                             ("w_ff1", w_ff1, p_ff1, m_w_ff1, v_w_ff1), ("w_ff2", w_ff2, p_ff2, m_w_ff2, v_w_ff2)):
        big[name] = _adamw("adamw_" + name, w[0], p, m[0], v[0], parts=True)

    order = ("w_mod", "b_mod", "attn_norm_g", "w_in", "swa_sinks", "rel_bias", "mla_q_norm_g", "w_uq", "mla_kv_norm_g",
             "w_ukv", "w_out", "mlp_norm_g", "w_ff1", "w_ff2", "final_norm_g")
    small_idx = {"b_mod": 0, "attn_norm_g": 1, "mlp_norm_g": 2, "final_norm_g": 3, "mla_q_norm_g": 4,
                 "mla_kv_norm_g": 5, "swa_sinks": 6, "rel_bias": 7}
    outs = []
    for kind, small_list in enumerate((sg, sd, sm, sv)):
        for name in order:
            outs.append(small_list[small_idx[name]] if name in small_idx else big[name][kind][None])
    return (loss, gx[None], *outs)
```

```python
import functools
import math

import jax
import jax.numpy as jnp
from jax import lax
from jax.experimental import pallas as pl
from jax.experimental.pallas import tpu as pltpu

F32 = jnp.float32
BF16 = jnp.bfloat16

NDEV = 8
EPS = 1e-6
BLOCK = 128
SWA_HEADS, SWA_KV, SWA_DH, SWA_GROUP = 16, 2, 64, 8
REL_BUCKETS, REL_MAX_DIST = 32, 128
MLA_H, Q_RANK, KV_RANK, NOPE, ROPE, VDIM = 8, 384, 128, 128, 64, 128
ROPE_THETA = 10000.0
OFF_K, OFF_V, OFF_CQ, OFF_CKV, OFF_KR, IN_COLS = 1024, 1152, 1280, 1664, 1792, 1856
IN_EXT = IN_COLS + ROPE
TAIL0, TAIL = OFF_CQ, IN_EXT - OFF_CQ
QW = NOPE + 2 * ROPE
MIX = SWA_HEADS * SWA_DH + MLA_H * VDIM
MLA_SCALE = (NOPE + ROPE) ** -0.5
SWA_SCALE = SWA_DH ** -0.5

ADAM_LR, ADAM_B1, ADAM_B2, ADAM_EPS, ADAM_WD, ADAM_STEP = 0.001, 0.9, 0.999, 1e-08, 0.01, 10

VMEM_LIMIT = 52 * 1024 * 1024
ROW_TILE = 256
MM_TM, MM_TN, MM_TK = 1024, 1024, 2048
ATT_T = 512
MLA_HB = 2
ADAM_ELEMS = 256 * 1024


MESH_ID = pl.DeviceIdType.MESH


def _place():
    x, y, c = lax.axis_index("x"), lax.axis_index("y"), lax.axis_index("c")
    return x, y, c, 2 * x + y


def _chip(x, y, k):
    return (1 - x if k & 2 else x, 1 - y if k & 1 else y)


def _dma_sems(*counts):
    return [pltpu.SemaphoreType.DMA((n,)) for n in counts]


class _Gather:
    def __init__(self, arrays):
        self.arrays = list(arrays)
        n = len(self.arrays)
        self.out_shape = [jax.ShapeDtypeStruct((NDEV,) + a.shape, a.dtype) for a in self.arrays]
        self.sems = _dma_sems(7 * n, 7 * n, n)

    def _copy(self, sems, a, k, src, dst, to):
        return pltpu.make_async_remote_copy(src_ref=src, dst_ref=dst, send_sem=sems[0].at[7 * a + k],
                                            recv_sem=sems[1].at[7 * a + k], device_id=to, device_id_type=MESH_ID)

    def start(self, ins, outs, sems):
        x, y, c, q = _place()
        me = 2 * q + c
        for a in range(len(ins)):
            pltpu.make_async_copy(ins[a], outs[a].at[me], sems[2].at[a]).start()
            self._copy(sems, a, 0, ins[a], outs[a].at[me], (x, y, 1 - c)).start()
            for k in (1, 2, 3):
                self._copy(sems, a, k, ins[a], outs[a].at[me], (*_chip(x, y, k), c)).start()

    def finish(self, ins, outs, sems):
        x, y, c, q = _place()
        me, sib = 2 * q + c, (x, y, 1 - c)
        n = len(ins)
        for k in (1, 2, 3):
            for a in range(n):
                blk = outs[a].at[2 * (q ^ k) + c]
                self._copy(sems, a, k, ins[a], blk, (*_chip(x, y, k), c)).wait_recv()
                self._copy(sems, a, 3 + k, blk, blk, sib).start()
        for a in range(n):
            self._copy(sems, a, 0, ins[a], outs[a].at[2 * q + 1 - c], sib).wait_recv()
            for k in (1, 2, 3):
                blk = outs[a].at[2 * (q ^ k) + 1 - c]
                self._copy(sems, a, 3 + k, blk, blk, sib).wait_recv()
        for a in range(n):
            for k in range(7):
                self._copy(sems, a, k, ins[a], outs[a].at[me], sib).wait_send()
            pltpu.make_async_copy(ins[a], outs[a].at[me], sems[2].at[a]).wait()


class _PairSwap:
    def __init__(self, arrays):
        self.arrays = list(arrays)
        n = len(self.arrays)
        self.out_shape = [jax.ShapeDtypeStruct((NDEV // 2,) + a.shape[1:], a.dtype) for a in self.arrays]
        self.sems = _dma_sems(4 * n, 4 * n)

    def _copy(self, sems, a, p, src, dst, to):
        return pltpu.make_async_remote_copy(src_ref=src, dst_ref=dst, send_sem=sems[0].at[4 * a + p],
                                            recv_sem=sems[1].at[4 * a + p], device_id=to, device_id_type=MESH_ID)

    def start(self, ins, outs, sems):
        x, y, c, _ = _place()
        for a in range(len(ins)):
            for p in range(4):
                self._copy(sems, a, p, ins[a].at[2 * p + 1 - c], outs[a].at[p], (x, y, 1 - c)).start()

    def finish(self, ins, outs, sems):
        x, y, c, _ = _place()
        for a in range(len(ins)):
            for p in range(4):
                cp = self._copy(sems, a, p, ins[a].at[2 * p + 1 - c], outs[a].at[p], (x, y, 1 - c))
                cp.wait_recv()
                cp.wait_send()


class _ChipScatter:
    def __init__(self, arrays):
        self.arrays = list(arrays)
        n = len(self.arrays)
        self.out_shape = [jax.ShapeDtypeStruct(a.shape, a.dtype) for a in self.arrays]
        self.sems = _dma_sems(3 * n, 3 * n, n)

    def _copy(self, sems, a, k, src, dst, to):
        return pltpu.make_async_remote_copy(src_ref=src, dst_ref=dst, send_sem=sems[0].at[3 * a + k - 1],
                                            recv_sem=sems[1].at[3 * a + k - 1], device_id=to, device_id_type=MESH_ID)

    def start(self, ins, outs, sems):
        x, y, c, q = _place()
        for a in range(len(ins)):
            pltpu.make_async_copy(ins[a].at[q], outs[a].at[q], sems[2].at[a]).start()
            for k in (1, 2, 3):
                self._copy(sems, a, k, ins[a].at[q ^ k], outs[a].at[q], (*_chip(x, y, k), c)).start()

    def finish(self, ins, outs, sems):
        x, y, c, q = _place()
        for a in range(len(ins)):
            for k in (1, 2, 3):
                cp = self._copy(sems, a, k, ins[a].at[q ^ k], outs[a].at[q ^ k], (*_chip(x, y, k), c))
                cp.wait_recv()
                cp.wait_send()
            pltpu.make_async_copy(ins[a].at[q], outs[a].at[q], sems[2].at[a]).wait()


def _call(body, **kw):
    return pl.pallas_call(body, **kw)


def _pcall(body, comm=None, **kw):
    if not comm:
        return _call(body, **kw)
    grid = kw["grid"]
    in_specs, out_specs, out_shape = list(kw["in_specs"]), list(kw["out_specs"]), list(kw["out_shape"])
    scratch = list(kw.get("scratch_shapes", ()))
    n_in, n_out, n_scr = len(in_specs), len(out_shape), len(scratch)
    n_cin = [len(j.arrays) for j in comm]
    n_sem = [len(j.sems) for j in comm]
    n = sum(n_cin)
    hbm = pl.BlockSpec(memory_space=pltpu.HBM)

    def carried(*refs):
        ins, cins = refs[:n_in], refs[n_in:n_in + n]
        outs, couts = refs[n_in + n:n_in + n + n_out], refs[n_in + n + n_out:n_in + 2 * n + n_out]
        scr, sems = refs[n_in + 2 * n + n_out:n_in + 2 * n + n_out + n_scr], refs[n_in + 2 * n + n_out + n_scr:]
        ids = [pl.program_id(ax) for ax in range(len(grid))]
        first = functools.reduce(jnp.logical_and, [i == 0 for i in ids])
        last = functools.reduce(jnp.logical_and, [i == g - 1 for i, g in zip(ids, grid)])

        def each(method):
            ai = si = 0
            for job, na, ns in zip(comm, n_cin, n_sem):
                getattr(job, method)(cins[ai:ai + na], couts[ai:ai + na], sems[si:si + ns])
                ai, si = ai + na, si + ns

        @pl.when(first)
        def _():
            each("start")

        body(*ins, *outs, *scr)

        @pl.when(last)
        def _():
            each("finish")

    kw.update(in_specs=in_specs + [hbm] * n, out_specs=out_specs + [hbm] * n,
              out_shape=out_shape + [o for j in comm for o in j.out_shape],
              scratch_shapes=scratch + [sm for j in comm for sm in j.sems],
              compiler_params=_cparams(("arbitrary",) * len(grid)))
    call = _call(carried, **kw)
    return lambda *args: call(*args, *[a for j in comm for a in j.arrays])


def _cparams(sem):
    return pltpu.CompilerParams(dimension_semantics=sem, vmem_limit_bytes=VMEM_LIMIT)


def _pick(n, pref, align):
    if n <= pref:
        return n
    t = (pref // align) * align
    while t >= align:
        if n % t == 0:
            return t
        t -= align
    return n


def _split3(x):
    a = x.astype(BF16)
    r = x - a.astype(F32)
    b = r.astype(BF16)
    c = (r - b.astype(F32)).astype(BF16)
    return a, b, c


def _exchange(name, job):
    n = len(job.arrays)

    def body(*refs):
        ins, outs, sems = refs[:n], refs[n:2 * n], refs[2 * n:]
        job.start(ins, outs, sems)
        job.finish(ins, outs, sems)

    hbm = pl.BlockSpec(memory_space=pltpu.HBM)
    return _call(body, name=name, out_shape=job.out_shape, in_specs=[hbm] * n, out_specs=[hbm] * n,
                 scratch_shapes=job.sems)(*job.arrays)


def _pair_sum(name, g, r, core):
    _, rr, cc = g.shape
    tr = rr if rr * cc <= 4 * ADAM_ELEMS else _pick(rr, max(16, 4 * ADAM_ELEMS // cc // 16 * 16), 16)

    def body(g_ref, r_ref, c_ref, o_ref):
        north = c_ref[:, 0:1] > 0.5
        mine = jnp.where(north, g_ref[1].astype(F32), g_ref[0].astype(F32))
        o_ref[...] = (mine + r_ref[...].astype(F32)).astype(o_ref.dtype)

    return _pcall(
        body, name=name, grid=(NDEV // 2, rr // tr),
        in_specs=[pl.BlockSpec((None, 2, tr, cc), lambda p, i: (p, 0, i, 0)),
                  pl.BlockSpec((None, tr, cc), lambda p, i: (p, i, 0)), pl.BlockSpec((1, 128), lambda p, i: (0, 0))],
        out_specs=pl.BlockSpec((None, tr, cc), lambda p, i: (p, i, 0)),
        out_shape=jax.ShapeDtypeStruct((NDEV // 2, rr, cc), g.dtype),
        compiler_params=_cparams(("parallel", "parallel")))(g.reshape(NDEV // 2, 2, rr, cc), r, core)


_DIMS = {"nn": (((1,), (0,)), ((), ())), "nt": (((1,), (1,)), ((), ())), "tn": (((0,), (0,)), ((), ()))}


def _mm(name, a, b, kind, grid, a_spec, b_spec, out_shape, out_specs, acc_shape, epilogue,
        extras=(), extra_specs=(), comm=None, b_parts=1):
    nk, ne, no = grid[2], len(extras), len(out_shape)

    def body(*refs):
        a_ref, b_ref = refs[0], refs[1]
        ex, outs = refs[2:2 + ne], refs[2 + ne:2 + ne + no]
        if b_parts == 1:
            part = lax.dot_general(a_ref[...].astype(BF16), b_ref[...].astype(BF16), _DIMS[kind],
                                   preferred_element_type=F32)
        else:
            kp = a_ref.shape[1] // b_parts
            part = sum(lax.dot_general(a_ref[:, p * kp:(p + 1) * kp].astype(BF16), b_ref[p].astype(BF16), _DIMS[kind],
                                       preferred_element_type=F32) for p in range(b_parts))
        if nk == 1:
            epilogue(part, ex, outs)
            return
        acc = refs[-1]
        k = pl.program_id(2)

        @pl.when(k == 0)
        def _():
            acc[...] = part

        @pl.when(jnp.logical_and(k > 0, k < nk - 1))
        def _():
            acc[...] += part

        @pl.when(k == nk - 1)
        def _():
            epilogue(acc[...] + part, ex, outs)

    return _pcall(
        body, comm=comm, name=name, grid=grid, in_specs=[a_spec, b_spec, *extra_specs], out_specs=out_specs,
        out_shape=out_shape, scratch_shapes=[pltpu.VMEM(acc_shape, F32)] if nk > 1 else [],
        compiler_params=_cparams(("parallel", "parallel", "arbitrary")),
    )(a, b, *extras)


def _store(dtype):
    def epi(acc, ex, outs):
        outs[0][...] = acc.astype(dtype)
    return epi


def _mm_plain(name, a, b, kind, m, n, k, out_dtype, tm=None, tn=None, tk=None):
    tm = _pick(m, tm or MM_TM, 128)
    tn = _pick(n, tn or MM_TN, 128)
    tk = _pick(k, tk or MM_TK, 128)
    a_spec = pl.BlockSpec((tk, tm), lambda i, j, q: (q, i)) if kind == "tn" else pl.BlockSpec((tm, tk), lambda i, j, q: (i, q))
    b_spec = pl.BlockSpec((tn, tk), lambda i, j, q: (j, q)) if kind == "nt" else pl.BlockSpec((tk, tn), lambda i, j, q: (q, j))
    return _mm(name, a, b, kind, (m // tm, n // tn, k // tk), a_spec, b_spec,
               [jax.ShapeDtypeStruct((m, n), out_dtype)], [pl.BlockSpec((tm, tn), lambda i, j, q: (i, j))],
               (tm, tn), _store(out_dtype))[0]


def _row(ts, d):
    return pl.BlockSpec((ts, d), lambda i: (i, 0))


def _vec(d):
    return pl.BlockSpec((1, d), lambda i: (0, 0))


def _norm_mod(name, x, gain, sc, sh, comm=None):
    s, d = x.shape
    ts = _pick(s, ROW_TILE, 16)

    def body(x_ref, g_ref, sc_ref, sh_ref, h_ref):
        xv = x_ref[...]
        r = lax.rsqrt(jnp.mean(xv * xv, axis=-1, keepdims=True) + EPS)
        h_ref[...] = ((xv * r) * g_ref[...] * (1.0 + sc_ref[...]) + sh_ref[...]).astype(BF16)

    return _pcall(body, comm=comm, name=name, grid=(s // ts,), in_specs=[_row(ts, d), _vec(d), _vec(d), _vec(d)],
                  out_specs=[_row(ts, d)], out_shape=[jax.ShapeDtypeStruct((s, d), BF16)],
                  compiler_params=_cparams(("parallel",)))(x, gain, sc, sh)


def _loss_head(x3, tgt, y2, gf, g2):
    s, d = x3.shape
    ts = _pick(s, ROW_TILE, 16)

    def body(x_ref, t_ref, y_ref, gf_ref, g2_ref, dx_ref, dy_ref, loss_ref, dgf_ref, dg2_ref):
        @pl.when(pl.program_id(0) == 0)
        def _():
            loss_ref[...] = jnp.zeros_like(loss_ref)
            dgf_ref[...] = jnp.zeros_like(dgf_ref)
            dg2_ref[...] = jnp.zeros_like(dg2_ref)

        xv = x_ref[...]
        r = lax.rsqrt(jnp.mean(xv * xv, axis=-1, keepdims=True) + EPS)
        xn = xv * r
        err = xn * gf_ref[...] - t_ref[...]
        loss_ref[...] += 0.5 * jnp.sum(jnp.mean(err * err, axis=-1, keepdims=True), axis=0, keepdims=True)
        dout = err * (1.0 / d)
        dgf_ref[...] += jnp.sum(dout * xn, axis=0, keepdims=True)
        dxn = dout * gf_ref[...]
        dx = r * (dxn - xn * jnp.mean(dxn * xn, axis=-1, keepdims=True))
        dx_ref[...] = dx
        dy_ref[...] = (dx * g2_ref[...]).astype(BF16)
        dg2_ref[...] += jnp.sum(dx * y_ref[...], axis=0, keepdims=True)

    one = pl.BlockSpec((1, 1), lambda i: (0, 0))
    return _pcall(
        body, name="loss_head", grid=(s // ts,),
        in_specs=[_row(ts, d), _row(ts, d), _row(ts, d), _vec(d), _vec(d)],
        out_specs=[_row(ts, d), _row(ts, d), one, _vec(d), _vec(d)],
        out_shape=[jax.ShapeDtypeStruct((s, d), F32), jax.ShapeDtypeStruct((s, d), BF16),
                   jax.ShapeDtypeStruct((1, 1), F32), jax.ShapeDtypeStruct((1, d), F32),
                   jax.ShapeDtypeStruct((1, d), F32)],
        compiler_params=_cparams(("arbitrary",)))(x3, tgt, y2, gf, g2)


def _norm_mod_bwd(name, x, dh, dres, gain, sc, y_prev=None, gate=None, comm=None):
    s, d = x.shape
    ts = _pick(s, ROW_TILE, 16)
    gated = y_prev is not None

    def body(*refs):
        if gated:
            x_ref, dh_ref, dr_ref, g_ref, sc_ref, y_ref, gt_ref, dx_ref, dy_ref, dsc_ref, dsh_ref, dg_ref, dgt_ref = refs
        else:
            x_ref, dh_ref, dr_ref, g_ref, sc_ref, dx_ref, dsc_ref, dsh_ref, dg_ref = refs

        @pl.when(pl.program_id(0) == 0)
        def _():
            dsc_ref[...] = jnp.zeros_like(dsc_ref)
            dsh_ref[...] = jnp.zeros_like(dsh_ref)
            dg_ref[...] = jnp.zeros_like(dg_ref)
            if gated:
                dgt_ref[...] = jnp.zeros_like(dgt_ref)

        xv, dhv = x_ref[...], dh_ref[...]
        r = lax.rsqrt(jnp.mean(xv * xv, axis=-1, keepdims=True) + EPS)
        xn = xv * r
        dsc_ref[...] += jnp.sum(dhv * (xn * g_ref[...]), axis=0, keepdims=True)
        dsh_ref[...] += jnp.sum(dhv, axis=0, keepdims=True)
        da = dhv * (1.0 + sc_ref[...])
        dg_ref[...] += jnp.sum(da * xn, axis=0, keepdims=True)
        dxn = da * g_ref[...]
        dx = dr_ref[...] + r * (dxn - xn * jnp.mean(dxn * xn, axis=-1, keepdims=True))
        dx_ref[...] = dx
        if gated:
            dy_ref[...] = (dx * gt_ref[...]).astype(BF16)
            dgt_ref[...] += jnp.sum(dx * y_ref[...], axis=0, keepdims=True)

    ins = [x, dh, dres, gain, sc] + ([y_prev, gate] if gated else [])
    in_specs = [_row(ts, d)] * 3 + [_vec(d)] * 2 + ([_row(ts, d), _vec(d)] if gated else [])
    vec_out = jax.ShapeDtypeStruct((1, d), F32)
    out_shape = [jax.ShapeDtypeStruct((s, d), F32)] + ([jax.ShapeDtypeStruct((s, d), BF16)] if gated else [])
    out_shape += [vec_out] * (4 if gated else 3)
    out_specs = [_row(ts, d)] * (2 if gated else 1) + [_vec(d)] * (4 if gated else 3)
    return _pcall(body, comm=comm, name=name, grid=(s // ts,), in_specs=in_specs, out_specs=out_specs,
                  out_shape=out_shape, compiler_params=_cparams(("arbitrary",)))(*ins)


def _dot3(a, b, dims):
    a1, a2, _ = _split3(a)
    b1, b2, _ = _split3(b)
    dot = functools.partial(lax.dot_general, dimension_numbers=dims, preferred_element_type=F32)
    return dot(a1, b1) + (dot(a1, b2) + dot(a2, b1))


def _mod_fwd(c_all, w, b_cols, comm=None):
    nb, d = c_all.shape
    n = w.shape[1]
    tk = _pick(d, 512, 128)
    nk = d // tk

    def body(c_ref, w_ref, b_ref, act_ref, out_ref):
        k = pl.program_id(0)
        cv = c_ref[...]
        act = cv * (1.0 / (1.0 + jnp.exp(-cv)))
        act_ref[...] = act

        @pl.when(k == 0)
        def _():
            out_ref[...] = jnp.broadcast_to(b_ref[...], out_ref.shape)

        out_ref[...] += _dot3(act, w_ref[...], _DIMS["nn"])

    return _pcall(
        body, comm=comm, name="mod_fwd", grid=(nk,),
        in_specs=[pl.BlockSpec((nb, tk), lambda k: (0, k)), pl.BlockSpec((tk, n), lambda k: (k, 0)),
                  pl.BlockSpec((1, n), lambda k: (0, 0))],
        out_specs=[pl.BlockSpec((nb, tk), lambda k: (0, k)), pl.BlockSpec((nb, n), lambda k: (0, 0))],
        out_shape=[jax.ShapeDtypeStruct((nb, d), F32), jax.ShapeDtypeStruct((nb, n), F32)],
        compiler_params=_cparams(("arbitrary",)))(c_all, w, b_cols)


def _mod_wgrad(act_all, dmod_cols):
    nb, d = act_all.shape
    n = dmod_cols.shape[1]
    tm = _pick(d, 512, 128)

    def body(a_ref, d_ref, o_ref):
        o_ref[...] = _dot3(a_ref[...], d_ref[...], _DIMS["tn"])

    return _pcall(
        body, name="mod_wgrad", grid=(d // tm,),
        in_specs=[pl.BlockSpec((nb, tm), lambda i: (0, i)), pl.BlockSpec((nb, n), lambda i: (0, 0))],
        out_specs=pl.BlockSpec((tm, n), lambda i: (i, 0)), out_shape=jax.ShapeDtypeStruct((d, n), F32),
        compiler_params=_cparams(("parallel",)))(act_all, dmod_cols)


def _bias_expand(rel_t, onehot_t):
    h, _ = rel_t.shape
    n = onehot_t.shape[1]

    def body(r_ref, o_ref, out_ref):
        a, b, c = _split3(r_ref[...])
        dot = functools.partial(lax.dot_general, dimension_numbers=_DIMS["nn"], preferred_element_type=F32)
        oh = o_ref[...]
        out_ref[...] = dot(a, oh) + (dot(b, oh) + dot(c, oh))

    full = lambda shp: pl.BlockSpec(shp, lambda: (0,) * len(shp))
    return _pcall(body, name="bias_expand", in_specs=[full(rel_t.shape), full(onehot_t.shape)],
                  out_specs=full((h, n)), out_shape=jax.ShapeDtypeStruct((h, n), F32),
                  compiler_params=pltpu.CompilerParams(vmem_limit_bytes=VMEM_LIMIT))(rel_t, onehot_t)


def _bias_reduce(dbias, onehot, dsink_rows):
    h, n = dbias.shape

    def body(d_ref, o_ref, s_ref, out_ref, so_ref):
        a, b, c = _split3(d_ref[...])
        dot = functools.partial(lax.dot_general, dimension_numbers=_DIMS["nn"], preferred_element_type=F32)
        oh = o_ref[...]
        out_ref[...] = dot(a, oh) + (dot(b, oh) + dot(c, oh))
        so_ref[...] = jnp.sum(s_ref[...], axis=-1, keepdims=True)

    full = lambda shp: pl.BlockSpec(shp, lambda: (0,) * len(shp))
    return _pcall(body, name="bias_reduce", in_specs=[full(dbias.shape), full(onehot.shape), full(dsink_rows.shape)],
                  out_specs=[full((h, REL_BUCKETS)), full((h, 1))],
                  out_shape=[jax.ShapeDtypeStruct((h, REL_BUCKETS), F32), jax.ShapeDtypeStruct((h, 1), F32)],
                  compiler_params=pltpu.CompilerParams(vmem_limit_bytes=VMEM_LIMIT))(dbias, onehot, dsink_rows)


PAIRS = SWA_GROUP // 2
PROWS = PAIRS * BLOCK
PCOLS = 2 * 2 * BLOCK


def _swa2_specs():
    tok = lambda width: pl.BlockSpec((BLOCK, width), lambda g, n: (n, g))
    prev = pl.BlockSpec((None, BLOCK, 2 * SWA_DH), lambda g, n: (g, jnp.maximum(n - 1, 0), 0))
    cur = pl.BlockSpec((None, BLOCK, 2 * SWA_DH), lambda g, n: (g, n, 0))
    bias_spec = pl.BlockSpec((None, PROWS, PCOLS), lambda g, n: (g, 0, 0))
    col_spec = pl.BlockSpec((None, PROWS, 1), lambda g, n: (g, 0, 0))
    lse_spec = pl.BlockSpec((None, None, PROWS, 1), lambda g, n: (g, n, 0, 0))
    return tok, prev, cur, bias_spec, col_spec, lse_spec


def _stack_pairs(blk):
    return jnp.concatenate([blk[:, p * 2 * SWA_DH:(p + 1) * 2 * SWA_DH] for p in range(PAIRS)], axis=0)


def _band(tp, tc, bp, bc):
    return jnp.concatenate([tp[...], tc[...], bp[...], bc[...]], axis=0)


def _swa2_scores(q_ref, kd, bias_ref, n):
    q2 = _stack_pairs(q_ref[...])
    s2 = lax.dot_general(q2, kd, _DIMS["nt"], preferred_element_type=F32) * SWA_SCALE + bias_ref[...]
    col = lax.broadcasted_iota(jnp.int32, s2.shape, 1)
    before_start = jnp.logical_and(n == 0, (col & (2 * BLOCK - 1)) < BLOCK)
    return q2, jnp.where(before_start, -jnp.inf, s2)


def _swa2_fwd(src, ktop, kbot, vtop, vbot, bias, sinks, comm=None):
    s = src.shape[0]
    nb = s // BLOCK
    tok, prev, cur, bias_spec, col_spec, lse_spec = _swa2_specs()

    def body(q_ref, ktp, ktc, kbp, kbc, vtp, vtc, vbp, vbc, bias_ref, sa_ref, sb_ref, o_ref, la_ref, lb_ref):
        n = pl.program_id(1)
        _, s2 = _swa2_scores(q_ref, _band(ktp, ktc, kbp, kbc), bias_ref, n)
        row = lax.broadcasted_iota(jnp.int32, (PCOLS, 2 * SWA_DH), 0)
        lane = lax.broadcasted_iota(jnp.int32, (PCOLS, 2 * SWA_DH), 1)
        ones = jnp.where(lane == row // (2 * BLOCK), 1.0, 0.0).astype(BF16)
        ps, ms, sinks_ = [], [], []
        for half, sink_ref in enumerate((sa_ref, sb_ref)):
            sc = s2[:, half * 2 * BLOCK:(half + 1) * 2 * BLOCK]
            m = jnp.maximum(jnp.max(sc, axis=-1, keepdims=True), sink_ref[...])
            ps.append(jnp.exp(sc - m).astype(BF16))
            ms.append(m)
        acc = lax.dot_general(jnp.concatenate(ps, axis=1), jnp.concatenate([_band(vtp, vtc, vbp, vbc), ones], axis=1),
                              _DIMS["nn"], preferred_element_type=F32)
        dens = []
        for half, (sink_ref, lse_ref) in enumerate(((sa_ref, la_ref), (sb_ref, lb_ref))):
            den = acc[:, 2 * SWA_DH + half:2 * SWA_DH + half + 1] + jnp.exp(sink_ref[...] - ms[half])
            lse_ref[...] = ms[half] + jnp.log(den)
            dens.append(den)
        lo = lax.broadcasted_iota(jnp.int32, (PROWS, 2 * SWA_DH), 1) < SWA_DH
        o2 = acc[:, 0:2 * SWA_DH] / jnp.where(lo, dens[0], dens[1])
        for p in range(PAIRS):
            o_ref[:, p * 2 * SWA_DH:(p + 1) * 2 * SWA_DH] = o2[p * BLOCK:(p + 1) * BLOCK].astype(BF16)

    lse_shape = jax.ShapeDtypeStruct((SWA_KV, nb, PROWS, 1), F32)
    return _pcall(
        body, comm=comm, name="swa_fwd", grid=(SWA_KV, nb),
        in_specs=[tok(PROWS), prev, cur, prev, cur, prev, cur, prev, cur, bias_spec, col_spec, col_spec],
        out_specs=[tok(PROWS), lse_spec, lse_spec],
        out_shape=[jax.ShapeDtypeStruct((s, MIX), BF16), lse_shape, lse_shape],
        compiler_params=_cparams(("parallel", "parallel")))(
            src, ktop, ktop, kbot, kbot, vtop, vtop, vbot, vbot, bias, sinks[0], sinks[1])


def _swa2_bwd(src, dsrc, ktop, kbot, vtop, vbot, lses, bias, sinks, comm=None):
    s = src.shape[0]
    nb = s // BLOCK
    tok, prev, cur, bias_spec, col_spec, lse_spec = _swa2_specs()
    lane_lo = lambda shape: lax.broadcasted_iota(jnp.int32, shape, 1) < SWA_DH

    def body(q_ref, do_ref, ktp, ktc, kbp, kbc, vtp, vtc, vbp, vbc, la_ref, lb_ref, bias_ref, sa_ref, sb_ref,
             dq_ref, dkp_ref, dkc_ref, dvp_ref, dvc_ref, dbias_ref, dsa_ref, dsb_ref):
        n = pl.program_id(1)

        @pl.when(n == 0)
        def _():
            dbias_ref[...] = jnp.zeros_like(dbias_ref)
            dsa_ref[...] = jnp.zeros_like(dsa_ref)
            dsb_ref[...] = jnp.zeros_like(dsb_ref)

        kd = _band(ktp, ktc, kbp, kbc)
        q2, s2 = _swa2_scores(q_ref, kd, bias_ref, n)
        do2 = _stack_pairs(do_ref[...]).astype(BF16)
        dp2 = lax.dot_general(do2, _band(vtp, vtc, vbp, vbc), _DIMS["nt"], preferred_element_type=F32)
        ps, dss = [], []
        for half, (sink_ref, lse_ref, dsink_ref) in enumerate(((sa_ref, la_ref, dsa_ref), (sb_ref, lb_ref, dsb_ref))):
            cols = slice(half * 2 * BLOCK, (half + 1) * 2 * BLOCK)
            lse_v = lse_ref[...]
            p = jnp.exp(s2[:, cols] - lse_v)
            dp = dp2[:, cols]
            delta = jnp.sum(p * dp, axis=-1, keepdims=True)
            ds = p * (dp - delta)
            dsink_ref[...] += -jnp.exp(sink_ref[...] - lse_v) * delta
            ps.append(p.astype(BF16))
            dss.append(ds)
        ds2 = jnp.concatenate(dss, axis=1)
        dbias_ref[...] += ds2
        dsb2 = (ds2 * SWA_SCALE).astype(BF16)
        dq2 = lax.dot_general(dsb2, kd, _DIMS["nn"], preferred_element_type=F32)
        for p in range(PAIRS):
            dq_ref[:, p * 2 * SWA_DH:(p + 1) * 2 * SWA_DH] = dq2[p * BLOCK:(p + 1) * BLOCK].astype(BF16)
        dk = lax.dot_general(dsb2, q2, _DIMS["tn"], preferred_element_type=F32)
        dv = lax.dot_general(jnp.concatenate(ps, axis=1), do2, _DIMS["tn"], preferred_element_type=F32)
        for full, prev_ref, cur_ref in ((dk, dkp_ref, dkc_ref), (dv, dvp_ref, dvc_ref)):
            own = jnp.where(lane_lo((2 * BLOCK, 2 * SWA_DH)), full[:2 * BLOCK], full[2 * BLOCK:])
            prev_ref[...] = own[:BLOCK]
            cur_ref[...] = own[BLOCK:]

    kv_out = jax.ShapeDtypeStruct((SWA_KV, s, 2 * SWA_DH), F32)
    col_out = jax.ShapeDtypeStruct((SWA_KV, PROWS, 1), F32)
    return _pcall(
        body, comm=comm, name="swa_bwd", grid=(SWA_KV, nb),
        in_specs=[tok(PROWS), tok(PROWS), prev, cur, prev, cur, prev, cur, prev, cur, lse_spec, lse_spec, bias_spec,
                  col_spec, col_spec],
        out_specs=[tok(PROWS), cur, cur, cur, cur, bias_spec, col_spec, col_spec],
        out_shape=[jax.ShapeDtypeStruct((s, IN_EXT), BF16), kv_out, kv_out, kv_out, kv_out,
                   jax.ShapeDtypeStruct(bias.shape, F32), col_out, col_out],
        compiler_params=_cparams(("arbitrary", "arbitrary")))(
            src, dsrc, ktop, ktop, kbot, kbot, vtop, vtop, vbot, vbot, lses[0], lses[1], bias, sinks[0], sinks[1])


def _rope_slab(slab, table):
    t = slab * table
    return t + pltpu.roll(t, ROPE, 1)


def _low_lanes(v):
    lane = lax.broadcasted_iota(jnp.int32, v.shape, 1)
    return jnp.where(lane < ROPE, v, 0.0)


def _rms(xv, g):
    r = lax.rsqrt(jnp.mean(xv * xv, axis=-1, keepdims=True) + EPS)
    return xv * r, r


def _mla_prep(proj, gq, gkv, table):
    s = proj.shape[0]
    ts = _pick(s, ROW_TILE, 16)

    def body(p_ref, gq_ref, gkv_ref, t_ref, cq_ref, ckv_ref, kr_ref):
        xq, _ = _rms(p_ref[:, 0:Q_RANK], None)
        cq_ref[...] = (xq * gq_ref[...]).astype(BF16)
        xkv, _ = _rms(p_ref[:, Q_RANK:Q_RANK + KV_RANK], None)
        ckv_ref[...] = (xkv * gkv_ref[...]).astype(BF16)
        kr_ref[...] = _low_lanes(_rope_slab(p_ref[:, Q_RANK + KV_RANK:TAIL], t_ref[...]))

    return _pcall(
        body, name="mla_prep", grid=(s // ts,),
        in_specs=[pl.BlockSpec((ts, TAIL), lambda i: (i, TAIL0 // TAIL)), _vec(Q_RANK), _vec(KV_RANK), _row(ts, 2 * ROPE)],
        out_specs=[_row(ts, Q_RANK), _row(ts, KV_RANK), _row(ts, 2 * ROPE)],
        out_shape=[jax.ShapeDtypeStruct((s, Q_RANK), BF16), jax.ShapeDtypeStruct((s, KV_RANK), BF16),
                   jax.ShapeDtypeStruct((s, 2 * ROPE), F32)],
        compiler_params=_cparams(("parallel",)))(proj, gq, gkv, table)


def _mla_prep_bwd(proj, dcq, dckv, dkr, gq, gkv, table, dproj):
    s = proj.shape[0]
    ts = _pick(s, ROW_TILE, 16)

    def norm_bwd(xv, dy, g):
        xn, r = _rms(xv, None)
        dg = jnp.sum(dy * xn, axis=0, keepdims=True)
        dxn = dy * g
        return r * (dxn - xn * jnp.mean(dxn * xn, axis=-1, keepdims=True)), dg

    def body(p_ref, dcq_ref, dckv_ref, dkr_ref, gq_ref, gkv_ref, t_ref, _, dt_ref, dgq_ref, dgkv_ref):
        @pl.when(pl.program_id(0) == 0)
        def _():
            dgq_ref[...] = jnp.zeros_like(dgq_ref)
            dgkv_ref[...] = jnp.zeros_like(dgkv_ref)

        dxq, dgq = norm_bwd(p_ref[:, 0:Q_RANK], dcq_ref[...], gq_ref[...])
        dxkv, dgkv = norm_bwd(p_ref[:, Q_RANK:Q_RANK + KV_RANK], dckv_ref[...], gkv_ref[...])
        dgq_ref[...] += dgq
        dgkv_ref[...] += dgkv
        d = _low_lanes(dkr_ref[...])
        dslab = (d + pltpu.roll(d, ROPE, 1)) * t_ref[...]
        dt_ref[:, 0:Q_RANK] = dxq.astype(BF16)
        dt_ref[:, Q_RANK:Q_RANK + KV_RANK] = dxkv.astype(BF16)
        dt_ref[:, Q_RANK + KV_RANK:TAIL] = dslab.astype(BF16)

    return _pcall(
        body, name="mla_prep_bwd", grid=(s // ts,),
        in_specs=[pl.BlockSpec((ts, TAIL), lambda i: (i, TAIL0 // TAIL)), _row(ts, Q_RANK), _row(ts, KV_RANK),
                  _row(ts, 2 * ROPE), _vec(Q_RANK), _vec(KV_RANK), _row(ts, 2 * ROPE), pl.BlockSpec(memory_space=pl.ANY)],
        out_specs=[pl.BlockSpec((ts, TAIL), lambda i: (i, TAIL0 // TAIL)), _vec(Q_RANK), _vec(KV_RANK)],
        out_shape=[jax.ShapeDtypeStruct(dproj.shape, BF16), jax.ShapeDtypeStruct((1, Q_RANK), F32),
                   jax.ShapeDtypeStruct((1, KV_RANK), F32)],
        input_output_aliases={7: 0},
        compiler_params=_cparams(("arbitrary",)))(proj, dcq, dckv, dkr, gq, gkv, table, dproj)


def _head_specs(ts):
    tok = lambda w: pl.BlockSpec((ts, w), lambda h, i: (i, 0))
    head = lambda w: pl.BlockSpec((None, ts, w), lambda h, i: (h, i, 0))
    wgt = lambda r, c: pl.BlockSpec((None, r, c), lambda h, i: (h, 0, 0))
    return tok, head, wgt


def _mla_qkv(cq, ckv, kr, wq, wkv, table):
    s = cq.shape[0]
    ts = _pick(s, 4 * ROW_TILE, 16)
    tok, head, wgt = _head_specs(ts)

    def body(cq_ref, ckv_ref, kr_ref, wq_ref, wkv_ref, t_ref, q_ref, k_ref, v_ref):
        qf = lax.dot_general(cq_ref[...], wq_ref[...], _DIMS["nn"], preferred_element_type=F32)
        q_ref[:, 0:NOPE] = qf[:, 0:NOPE].astype(BF16)
        q_ref[:, NOPE:QW] = _rope_slab(qf[:, NOPE:QW], t_ref[...]).astype(BF16)
        kv = lax.dot_general(ckv_ref[...], wkv_ref[...], _DIMS["nn"], preferred_element_type=F32)
        k_ref[:, 0:NOPE] = kv[:, 0:NOPE].astype(BF16)
        k_ref[:, NOPE:QW] = kr_ref[...].astype(BF16)
        v_ref[:, 0:VDIM] = kv[:, NOPE:NOPE + VDIM].astype(BF16)
        lane = lax.broadcasted_iota(jnp.int32, (ts, VDIM), 1)
        v_ref[:, VDIM:2 * VDIM] = jnp.where(lane == 0, 1.0, 0.0).astype(BF16)

    return _pcall(
        body, name="mla_qkv", grid=(MLA_H, s // ts),
        in_specs=[tok(Q_RANK), tok(KV_RANK), tok(2 * ROPE), wgt(Q_RANK, QW), wgt(KV_RANK, NOPE + VDIM), tok(2 * ROPE)],
        out_specs=[head(QW), head(QW), head(2 * VDIM)],
        out_shape=[jax.ShapeDtypeStruct((MLA_H, s, QW), BF16), jax.ShapeDtypeStruct((MLA_H, s, QW), BF16),
                   jax.ShapeDtypeStruct((MLA_H, s, 2 * VDIM), BF16)],
        compiler_params=_cparams(("parallel", "parallel")))(cq, ckv, kr, wq, wkv, table)


def _mla_qkv_bwd(dq, dk, dv, cq, ckv, wq, wkv, table, comm=None):
    s = cq.shape[0]
    ts = _pick(s, 4 * ROW_TILE, 16)
    tok, head, wgt = _head_specs(ts)
    whole = lambda w: pl.BlockSpec((s, w), lambda h, i: (0, 0))

    def body(dq_ref, dk_ref, dv_ref, cq_ref, ckv_ref, wq_ref, wkv_ref, t_ref,
             dcq_ref, dckv_ref, dkr_ref, gwq_ref, gwkv_ref):
        h, i = pl.program_id(0), pl.program_id(1)
        rows = pl.ds(pl.multiple_of(i * ts, ts), ts)
        d = dq_ref[:, NOPE:QW]
        dslab = (d + pltpu.roll(d, ROPE, 1)) * t_ref[...]
        dqe = jnp.concatenate([dq_ref[:, 0:NOPE], dslab], axis=1).astype(BF16)
        dkv = jnp.concatenate([dk_ref[:, 0:NOPE], dv_ref[...]], axis=1).astype(BF16)
        dcq = lax.dot_general(dqe, wq_ref[...], _DIMS["nt"], preferred_element_type=F32)
        dckv = lax.dot_general(dkv, wkv_ref[...], _DIMS["nt"], preferred_element_type=F32)
        gwq = lax.dot_general(cq_ref[...], dqe, _DIMS["tn"], preferred_element_type=F32)
        gwkv = lax.dot_general(ckv_ref[...], dkv, _DIMS["tn"], preferred_element_type=F32)
        dkr = dk_ref[:, NOPE:QW].astype(F32)

        @pl.when(h == 0)
        def _():
            dcq_ref[rows, :] = dcq
            dckv_ref[rows, :] = dckv
            dkr_ref[rows, :] = dkr

        @pl.when(h > 0)
        def _():
            dcq_ref[rows, :] += dcq
            dckv_ref[rows, :] += dckv
            dkr_ref[rows, :] += dkr

        @pl.when(i == 0)
        def _():
            gwq_ref[...] = gwq
            gwkv_ref[...] = gwkv

        @pl.when(i > 0)
        def _():
            gwq_ref[...] += gwq
            gwkv_ref[...] += gwkv

    return _pcall(
        body, comm=comm, name="mla_qkv_bwd", grid=(MLA_H, s // ts),
        in_specs=[head(QW), head(QW), head(VDIM), tok(Q_RANK), tok(KV_RANK), wgt(Q_RANK, QW),
                  wgt(KV_RANK, NOPE + VDIM), tok(2 * ROPE)],
        out_specs=[whole(Q_RANK), whole(KV_RANK), whole(2 * ROPE), wgt(Q_RANK, QW), wgt(KV_RANK, NOPE + VDIM)],
        out_shape=[jax.ShapeDtypeStruct((s, Q_RANK), F32), jax.ShapeDtypeStruct((s, KV_RANK), F32),
                   jax.ShapeDtypeStruct((s, 2 * ROPE), F32), jax.ShapeDtypeStruct((MLA_H, Q_RANK, QW), F32),
                   jax.ShapeDtypeStruct((MLA_H, KV_RANK, NOPE + VDIM), F32)],
        compiler_params=_cparams(("arbitrary", "arbitrary")))(dq, dk, dv, cq, ckv, wq, wkv, table)


def _diag_mask(t):
    return lax.broadcasted_iota(jnp.int32, (t, t), 1) <= lax.broadcasted_iota(jnp.int32, (t, t), 0)


def _mla_fwd(q, k, v, mix, comm=None):
    s = q.shape[1]
    t = _pick(s, ATT_T, 128)
    nt = s // t
    assert nt % 2 == 0
    hb = MLA_H

    def fold(p, u):
        first = u <= p
        return jnp.where(first, p, nt - 1 - p), jnp.where(first, u, u - p - 1)

    to_log2 = MLA_SCALE * math.log2(math.e)

    def body(q_ref, k_ref, v_ref, _, o_ref, oh_ref, lse_ref, m_ref, acc_ref):
        i, j = fold(pl.program_id(1), pl.program_id(2))

        @pl.when(j == 0)
        def _():
            m_ref[...] = jnp.full_like(m_ref, -jnp.inf)
            acc_ref[...] = jnp.zeros_like(acc_ref)

        def step(diagonal):
            for h in range(hb):
                sc = lax.dot_general(q_ref[h], k_ref[h], _DIMS["nt"], preferred_element_type=F32)
                if diagonal:
                    sc = jnp.where(_diag_mask(t), sc, -jnp.inf)
                m_old = m_ref[h]
                m_new = jnp.maximum(m_old, jnp.max(sc, axis=-1, keepdims=True))
                alpha = jnp.exp2((m_old - m_new) * to_log2)
                p = jnp.exp2((sc - m_new) * to_log2)
                acc_ref[h] = alpha * acc_ref[h] + lax.dot_general(p.astype(BF16), v_ref[h], _DIMS["nn"],
                                                                  preferred_element_type=F32)
                m_ref[h] = m_new

        @pl.when(j < i)
        def _():
            step(False)

        @pl.when(j == i)
        def _():
            step(True)
            for h in range(hb):
                den = acc_ref[h, :, VDIM:VDIM + 1]
                o = acc_ref[h, :, 0:VDIM] / den
                o_ref[:, h * VDIM:(h + 1) * VDIM] = o
                oh_ref[:, h * VDIM:(h + 1) * VDIM] = o.astype(BF16)
                lse_ref[h] = m_ref[h] * MLA_SCALE + jnp.log(den)

    o_spec = pl.BlockSpec((t, hb * VDIM), lambda h, p, u: (fold(p, u)[0], h))
    first = (MIX - MLA_H * VDIM) // (hb * VDIM)
    mix_spec = pl.BlockSpec((t, hb * VDIM), lambda h, p, u: (fold(p, u)[0], first + h))
    return _pcall(
        body, comm=comm, name="mla_fwd", grid=(MLA_H // hb, nt // 2, nt + 1),
        in_specs=[pl.BlockSpec((hb, t, QW), lambda h, p, u: (h, fold(p, u)[0], 0)),
                  pl.BlockSpec((hb, t, QW), lambda h, p, u: (h, fold(p, u)[1], 0)),
                  pl.BlockSpec((hb, t, 2 * VDIM), lambda h, p, u: (h, fold(p, u)[1], 0)),
                  pl.BlockSpec(memory_space=pl.ANY)],
        out_specs=[o_spec, mix_spec, pl.BlockSpec((hb, t, 1), lambda h, p, u: (h, fold(p, u)[0], 0))],
        out_shape=[jax.ShapeDtypeStruct((s, MLA_H * VDIM), F32), jax.ShapeDtypeStruct(mix.shape, BF16),
                   jax.ShapeDtypeStruct((MLA_H, s, 1), F32)],
        input_output_aliases={3: 1},
        scratch_shapes=[pltpu.VMEM((hb, t, 1), F32), pltpu.VMEM((hb, t, 2 * VDIM), F32)],
        compiler_params=_cparams(("parallel", "parallel", "arbitrary")))(q, k, v, mix)


def _mla_delta(dmix, o):
    s = o.shape[0]
    ts = _pick(s, 2 * ROW_TILE, 16)
    w = MLA_H * VDIM

    def body(d_ref, o_ref, out_ref):
        prod = d_ref[...] * o_ref[...]
        for h in range(MLA_H):
            out_ref[h] = jnp.sum(prod[:, h * VDIM:(h + 1) * VDIM], axis=-1, keepdims=True)

    return _pcall(body, name="mla_delta", grid=(s // ts,),
                  in_specs=[pl.BlockSpec((ts, w), lambda i: (i, SWA_HEADS * SWA_DH // w)), pl.BlockSpec((ts, w), lambda i: (i, 0))],
                  out_specs=pl.BlockSpec((MLA_H, ts, 1), lambda i: (0, i, 0)),
                  out_shape=jax.ShapeDtypeStruct((MLA_H, s, 1), F32), compiler_params=_cparams(("parallel",)))(dmix, o)


def _mla_bwd(q, k, v, dmix, delta, lse, comm=None):
    s = q.shape[1]
    t = _pick(s, ATT_T, 128)
    nt = s // t
    assert nt % 2 == 0
    hb = 2 * MLA_HB
    o_blk0 = SWA_HEADS * SWA_DH // (hb * VDIM)

    def fold(p, u):
        first = u < nt - p
        return jnp.where(first, p, nt - 1 - p), jnp.where(first, p + u, u - 1)

    log2e = math.log2(math.e)

    def body(q_ref, k_ref, v_ref, do_ref, delta_ref, lse_ref, dq_ref, dk_ref, dv_ref, dk_acc, dv_acc):
        j, i = fold(pl.program_id(1), pl.program_id(2))
        rows = pl.ds(pl.multiple_of(i * t, t), t)

        @pl.when(i == j)
        def _():
            dk_acc[...] = jnp.zeros_like(dk_acc)
            dv_acc[...] = jnp.zeros_like(dv_acc)

        def step(diagonal):
            for h in range(hb):
                qv, kv_ = q_ref[h], k_ref[h]
                dob = do_ref[:, h * VDIM:(h + 1) * VDIM].astype(BF16)
                st = lax.dot_general(kv_, qv, _DIMS["nt"], preferred_element_type=F32)
                pt = jnp.exp2(st * (MLA_SCALE * log2e) - lse_ref[h] * log2e)
                if diagonal:
                    keep = lax.broadcasted_iota(jnp.int32, (t, t), 0) <= lax.broadcasted_iota(jnp.int32, (t, t), 1)
                    pt = jnp.where(keep, pt, 0.0)
                dpt = lax.dot_general(v_ref[h], dob, _DIMS["nt"], preferred_element_type=F32)
                dst = (pt * (dpt - delta_ref[h]) * MLA_SCALE).astype(BF16)
                dv_acc[h] += lax.dot_general(pt.astype(BF16), dob, _DIMS["nn"], preferred_element_type=F32)
                dk_acc[h] += lax.dot_general(dst, qv, _DIMS["nn"], preferred_element_type=F32)
                dqv = lax.dot_general(dst, kv_, _DIMS["tn"], preferred_element_type=F32)

                @pl.when(j == 0)
                def _():
                    dq_ref[h, rows, :] = dqv

                @pl.when(j > 0)
                def _():
                    dq_ref[h, rows, :] += dqv

        @pl.when(i > j)
        def _():
            step(False)

        @pl.when(i == j)
        def _():
            step(True)

        @pl.when(i == nt - 1)
        def _():
            dk_ref[...] = dk_acc[...].astype(BF16)
            dv_ref[...] = dv_acc[...].astype(BF16)

    qi = lambda h, p, u: (h, fold(p, u)[1], 0)
    kj = lambda h, p, u: (h, fold(p, u)[0], 0)
    row = pl.BlockSpec((hb, 1, t), lambda h, p, u: (h, 0, fold(p, u)[1]))
    return _pcall(
        body, comm=comm, name="mla_bwd", grid=(MLA_H // hb, nt // 2, nt + 1),
        in_specs=[pl.BlockSpec((hb, t, QW), qi), pl.BlockSpec((hb, t, QW), kj), pl.BlockSpec((hb, t, VDIM), kj),
                  pl.BlockSpec((t, hb * VDIM), lambda h, p, u: (fold(p, u)[1], o_blk0 + h)), row, row],
        out_specs=[pl.BlockSpec((hb, s, QW), lambda h, p, u: (h, 0, 0)), pl.BlockSpec((hb, t, QW), kj),
                   pl.BlockSpec((hb, t, VDIM), kj)],
        out_shape=[jax.ShapeDtypeStruct((MLA_H, s, QW), F32), jax.ShapeDtypeStruct((MLA_H, s, QW), BF16),
                   jax.ShapeDtypeStruct((MLA_H, s, VDIM), BF16)],
        scratch_shapes=[pltpu.VMEM((hb, t, QW), F32), pltpu.VMEM((hb, t, VDIM), F32)],
        compiler_params=_cparams(("arbitrary", "arbitrary", "arbitrary")))(q, k, v, dmix, delta, lse)


def _adamw(name, w, g, m, v, parts):
    r, c = w.shape
    n_parts = g.shape[0] if parts else 1
    tr = r if r * c <= ADAM_ELEMS else _pick(r, max(8, ADAM_ELEMS // c // 8 * 8), 8)
    c1 = 1.0 - ADAM_B1 ** ADAM_STEP
    c2 = 1.0 - ADAM_B2 ** ADAM_STEP

    def body(w_ref, g_ref, m_ref, v_ref, go_ref, d_ref, mo_ref, vo_ref):
        if parts:
            gv = g_ref[0].astype(F32)
            for j in range(1, n_parts):
                gv = gv + g_ref[j].astype(F32)
        else:
            gv = g_ref[...]
        mv = ADAM_B1 * m_ref[...] + (1.0 - ADAM_B1) * gv
        vv = ADAM_B2 * v_ref[...] + (1.0 - ADAM_B2) * (gv * gv)
        go_ref[...] = gv
        mo_ref[...] = mv
        vo_ref[...] = vv
        d_ref[...] = -ADAM_LR * ((mv / c1) / (jnp.sqrt(vv / c2) + ADAM_EPS) + ADAM_WD * w_ref[...])

    blk = pl.BlockSpec((tr, c), lambda i: (i, 0))
    g_spec = pl.BlockSpec((n_parts, tr, c), lambda i: (0, i, 0)) if parts else blk
    out = jax.ShapeDtypeStruct((r, c), F32)
    return _pcall(body, name=name, grid=(r // tr,), in_specs=[blk, g_spec, blk, blk], out_specs=[blk] * 4,
                  out_shape=[out] * 4, compiler_params=_cparams(("parallel",)))(w, g, m, v)


def _t5_bucket(dist):
    n = jnp.maximum(dist, 0)
    max_exact = REL_BUCKETS // 2
    nf = jnp.maximum(n, 1).astype(F32)
    large = max_exact + (jnp.log(nf / max_exact) / math.log(REL_MAX_DIST / max_exact)
                         * (REL_BUCKETS - max_exact)).astype(jnp.int32)
    return jnp.where(n < max_exact, n, jnp.minimum(large, REL_BUCKETS - 1))


def _swap_halves(w, r0):
    return jnp.concatenate([w[:, r0 + ROPE // 2:r0 + ROPE], w[:, r0:r0 + ROPE // 2]], axis=1)


def _fold_swapped(g, r0, width):
    sw = g[..., width:width + ROPE]
    half = ROPE // 2
    return jnp.concatenate([g[..., :r0], g[..., r0:r0 + half] + sw[..., half:], g[..., r0 + half:r0 + ROPE] + sw[..., :half],
                            g[..., r0 + ROPE:width]], axis=-1)


def kernel(x, c, w_mod, b_mod, attn_norm_g, w_in, swa_sinks, rel_bias, mla_q_norm_g, w_uq, mla_kv_norm_g, w_ukv, w_out, mlp_norm_g, w_ff1, w_ff2, final_norm_g, loss_target, m_w_mod, m_b_mod, m_attn_norm_g, m_w_in, m_swa_sinks, m_rel_bias, m_mla_q_norm_g, m_w_uq, m_mla_kv_norm_g, m_w_ukv, m_w_out, m_mlp_norm_g, m_w_ff1, m_w_ff2, m_final_norm_g, v_w_mod, v_b_mod, v_attn_norm_g, v_w_in, v_swa_sinks, v_rel_bias, v_mla_q_norm_g, v_w_uq, v_mla_kv_norm_g, v_w_ukv, v_w_out, v_mlp_norm_g, v_w_ff1, v_w_ff2, v_final_norm_g):
    s, d = x.shape[1], x.shape[2]
    ffs = w_ff1.shape[2]
    ff = ffs * NDEV
    nmod = w_mod.shape[2]
    me = 4 * lax.axis_index("x") + 2 * lax.axis_index("y") + lax.axis_index("c")
    x2d, tgt = x[0], loss_target[0]
    final_g = final_norm_g.reshape(1, d)

    w_in_l = jnp.concatenate([w_in[0], _swap_halves(w_in[0], OFF_KR)], axis=1).astype(BF16)
    w_uq_l = jnp.concatenate([w_uq[0], _swap_halves(w_uq[0], NOPE)], axis=1).astype(BF16)
    core = jnp.full((1, 128), lax.axis_index("c"), F32)
    (c_all,) = _exchange("gather_c", _Gather([c]))

    b_cols = lax.dynamic_slice(b_mod, (0, me * nmod), (1, nmod))
    act_all, mod_cols = _mod_fwd(c_all.reshape(NDEV, d), w_mod[0], b_cols)
    (mod_g,) = _exchange("gather_mod", _Gather([mod_cols]))
    mod = lax.dynamic_index_in_dim(mod_g, me, axis=1, keepdims=False).reshape(1, 6 * d)
    sh1, sc1, g1, sh2, sc2, g2 = [mod[:, i * d:(i + 1) * d] for i in range(6)]

    pos = jnp.arange(s, dtype=F32)
    inv_freq = ROPE_THETA ** (-jnp.arange(ROPE // 2, dtype=F32) / (ROPE // 2))
    ang = pos[:, None] * inv_freq[None, :]
    cos, sin = jnp.cos(ang), jnp.sin(ang)
    table = jnp.concatenate([cos, cos, -sin, sin], axis=1)
    q_loc = jnp.arange(BLOCK)[:, None]
    k_loc = jnp.arange(2 * BLOCK)[None, :]
    dist = q_loc + BLOCK - k_loc
    in_window = (dist >= 0) & (dist < BLOCK)
    onehot = (_t5_bucket(dist).reshape(-1, 1) == jnp.arange(REL_BUCKETS)[None, :]).astype(BF16)
    bias = _bias_expand(rel_bias.T, onehot.T).reshape(SWA_HEADS, BLOCK, 2 * BLOCK)
    bias = jnp.where(in_window[None], bias, -jnp.inf).reshape(SWA_KV, PAIRS, 2, BLOCK, 2 * BLOCK)
    bias = bias.transpose(0, 1, 3, 2, 4).reshape(SWA_KV, PROWS, PCOLS)
    sinks = jnp.broadcast_to(swa_sinks.reshape(SWA_KV, PAIRS, 1, 2), (SWA_KV, PAIRS, BLOCK, 2)).reshape(SWA_KV, PROWS, 2)
    sinks = (sinks[:, :, 0:1], sinks[:, :, 1:2])

    h1, w_in_g, w_uq_g, w_ukv_g = _norm_mod("norm1", x2d, attn_norm_g, sc1, sh1,
                                            comm=[_Gather([w_in_l, w_uq_l, w_ukv[0].astype(BF16)])])
    w_in_e = w_in_g.reshape(d, IN_EXT)

    def both_dtypes(acc, ex, outs):
        outs[0][...] = acc
        outs[1][...] = acc.astype(BF16)

    tmp = _pick(s, MM_TM // 2, 128)
    proj_blk = pl.BlockSpec((tmp, IN_EXT), lambda i, j, q: (i, 0))
    proj, proj_h = _mm("proj", h1, w_in_e, "nn", (s // tmp, 1, 1), pl.BlockSpec((tmp, d), lambda i, j, q: (i, 0)),
                       pl.BlockSpec((d, IN_EXT), lambda i, j, q: (0, 0)),
                       [jax.ShapeDtypeStruct((s, IN_EXT), F32), jax.ShapeDtypeStruct((s, IN_EXT), BF16)],
                       [proj_blk, proj_blk], (tmp, IN_EXT), both_dtypes)
    def diag_pair(tok):
        x = jnp.stack([tok[:, :SWA_DH], tok[:, SWA_DH:]])
        zero = jnp.zeros_like(x)
        return jnp.concatenate([x, zero], axis=2), jnp.concatenate([zero, x], axis=2)

    k_top, k_bot = diag_pair(proj_h[:, OFF_K:OFF_V])
    v_top, v_bot = diag_pair(proj_h[:, OFF_V:OFF_CQ])
    o_a, lse_a0, lse_a1, w_out_g = _swa2_fwd(proj_h, k_top, k_bot, v_top, v_bot, bias, sinks,
                                             comm=[_Gather([w_out[0].astype(BF16)])])
    w_out_f = w_out_g.reshape(MIX, d)

    cq, ckv, kr = _mla_prep(proj, mla_q_norm_g, mla_kv_norm_g, table)
    q_b, k_b, v_b = _mla_qkv(cq, ckv, kr, w_uq_g, w_ukv_g, table)
    o_b, mix, lse_b, w_ff1_g = _mla_fwd(q_b, k_b, v_b, o_a, comm=[_Gather([w_ff1[0].astype(BF16)])])

    tm, tn, tk = _pick(s, MM_TM, 128), _pick(d, MM_TN, 128), _pick(MIX, MM_TK, 128)
    row_blk = pl.BlockSpec((tm, tn), lambda i, j, q: (i, j))
    gate_blk = pl.BlockSpec((1, tn), lambda i, j, q: (0, j))

    def gated_residual(acc, ex, outs):
        outs[0][...] = acc
        outs[1][...] = ex[0][...] + ex[1][...] * acc

    y1, x2 = _mm("out_proj", mix, w_out_f, "nn", (s // tm, d // tn, MIX // tk),
                 pl.BlockSpec((tm, tk), lambda i, j, q: (i, q)), pl.BlockSpec((tk, tn), lambda i, j, q: (q, j)),
                 [jax.ShapeDtypeStruct((s, d), F32)] * 2, [row_blk, row_blk], (tm, tn), gated_residual,
                 extras=(x2d, g1), extra_specs=(row_blk, gate_blk))

    (h2,) = _norm_mod("norm2", x2, mlp_norm_g, sc2, sh2)
    tnf, tkd = _pick(ffs, MM_TN, 128), _pick(d, MM_TK, 128)
    rf = ffs // tnf
    ff_blk = pl.BlockSpec((tm, tnf), lambda i, j, q: (i, j))

    def relu_sq(acc, ex, outs):
        u = jnp.maximum(acc, 0.0)
        outs[0][...] = u
        outs[1][...] = (u * u).astype(BF16)

    u, uu, w_ff2_g = _mm("ff1", h2, w_ff1_g, "nn", (s // tm, ff // tnf, d // tkd),
                         pl.BlockSpec((tm, tkd), lambda i, j, q: (i, q)),
                         pl.BlockSpec((None, tkd, tnf), lambda i, j, q: (j // rf, q, j % rf)),
                         [jax.ShapeDtypeStruct((s, ff), F32), jax.ShapeDtypeStruct((s, ff), BF16)], [ff_blk, ff_blk],
                         (tm, tnf), relu_sq, comm=[_Gather([w_ff2[0].astype(BF16)])])
    w_ff2_f = w_ff2_g.reshape(ff, d)
    tkf = _pick(ff, MM_TK, 128)
    y2, x3 = _mm("ff2", uu, w_ff2_f, "nn", (s // tm, d // tn, ff // tkf),
                 pl.BlockSpec((tm, tkf), lambda i, j, q: (i, q)), pl.BlockSpec((tkf, tn), lambda i, j, q: (q, j)),
                 [jax.ShapeDtypeStruct((s, d), F32)] * 2, [row_blk, row_blk], (tm, tn), gated_residual,
                 extras=(x2, g2), extra_specs=(row_blk, gate_blk))

    dx3, dy2, loss_p, dgf, dg2 = _loss_head(x3, tgt, y2, final_g, g2)

    def relu_sq_bwd(acc, ex, outs):
        outs[0][...] = (acc * (2.0 * ex[0][...])).astype(BF16)

    tnf2 = _pick(ff, MM_TN, 128)
    du = _mm("ff2_dx", dy2, w_ff2_f, "nt", (s // tm, ff // tnf2, d // tkd),
             pl.BlockSpec((tm, tkd), lambda i, j, q: (i, q)), pl.BlockSpec((tnf2, tkd), lambda i, j, q: (j, q)),
             [jax.ShapeDtypeStruct((s, ff), BF16)], [pl.BlockSpec((tm, tnf2), lambda i, j, q: (i, j))],
             (tm, tnf2), relu_sq_bwd, extras=(u,), extra_specs=(pl.BlockSpec((tm, tnf2), lambda i, j, q: (i, j)),))[0]
    gw_ff2 = _mm_plain("ff2_dw", uu, dy2, "tn", ff, d, s, BF16)
    tmd, tks = _pick(d, MM_TM, 128), _pick(s, MM_TK, 128)
    gw_ff2 = gw_ff2.reshape(NDEV, ffs, d)
    dh2, s_ff2 = _mm("ff1_dx", du, w_ff1_g, "nt", (s // tm, d // tn, NDEV // 2),
                     pl.BlockSpec((tm, 2 * ffs), lambda i, j, q: (i, q)),
                     pl.BlockSpec((2, tn, ffs), lambda i, j, q: (q, j, 0)),
                     [jax.ShapeDtypeStruct((s, d), F32)], [row_blk], (tm, tn), _store(F32),
                     comm=[_PairSwap([gw_ff2])], b_parts=2)
    c_ff2 = _pair_sum("pair_ff2", gw_ff2, s_ff2, core)
    gw_ff1, p_ff2 = _mm("ff1_dw", h2, du, "tn", (d // tmd, ff // tnf, s // tks),
                        pl.BlockSpec((tks, tmd), lambda i, j, q: (q, i)), pl.BlockSpec((tks, tnf), lambda i, j, q: (q, j)),
                        [jax.ShapeDtypeStruct((NDEV, d, ffs), BF16)],
                        [pl.BlockSpec((None, tmd, tnf), lambda i, j, q: (j // rf, i, j % rf))], (tmd, tnf), _store(BF16),
                        comm=[_ChipScatter([c_ff2])])
    dx2, dy1, dsc2, dsh2, dgm, dg1, s_ff1 = _norm_mod_bwd("norm2_bwd", x2, dh2, dx3, mlp_norm_g, sc2, y1, g1,
                                                          comm=[_PairSwap([gw_ff1])])
    c_ff1 = _pair_sum("pair_ff1", gw_ff1, s_ff1, core)

    dmix = _mm_plain("out_proj_dx", dy1, w_out_f, "nt", s, MIX, d, F32)
    gw_out = _mm_plain("out_proj_dw", mix, dy1, "tn", MIX, d, s, BF16).reshape(NDEV, MIX // NDEV, d)

    delta_b = _mla_delta(dmix, o_b).reshape(MLA_H, 1, s)
    dq_b, dk_b, dv_b, p_ff1, s_out = _mla_bwd(q_b, k_b, v_b, dmix, delta_b, lse_b.reshape(MLA_H, 1, s),
                                              comm=[_ChipScatter([c_ff1]), _PairSwap([gw_out])])
    c_out = _pair_sum("pair_out", gw_out, s_out, core)
    dcq, dckv, dkr, gw_uq_e, gw_ukv, p_out = _mla_qkv_bwd(dq_b, dk_b, dv_b, cq, ckv, w_uq_g, w_ukv_g, table,
                                                          comm=[_ChipScatter([c_out])])
    gw_uq = _fold_swapped(gw_uq_e, NOPE, NOPE + ROPE).astype(BF16)
    gw_ukv = gw_ukv.astype(BF16)

    dproj, dkp, dkc, dvp, dvc, dbias, dsink0, dsink1, s_uq, s_ukv = _swa2_bwd(
        proj_h, dmix, k_top, k_bot, v_top, v_bot, (lse_a0, lse_a1), bias, sinks, comm=[_PairSwap([gw_uq, gw_ukv])])
    c_uq = _pair_sum("pair_uq", gw_uq, s_uq, core)
    c_ukv = _pair_sum("pair_ukv", gw_ukv, s_ukv, core)
    dproj, dgq, dgkv = _mla_prep_bwd(proj, dcq, dckv, dkr, mla_q_norm_g, mla_kv_norm_g, table, dproj)

    def band_grad(cur, prv):
        g = cur + jnp.concatenate([prv[:, BLOCK:], jnp.zeros_like(prv[:, :BLOCK])], axis=1)
        g = g[:, :, :SWA_DH] + g[:, :, SWA_DH:]
        return jnp.concatenate([g[0], g[1]], axis=1)

    dbias = dbias.reshape(SWA_KV, PAIRS, BLOCK, 2, 2 * BLOCK).transpose(0, 1, 3, 2, 4)
    dsink = jnp.stack([dsink0.reshape(SWA_KV, PAIRS, BLOCK), dsink1.reshape(SWA_KV, PAIRS, BLOCK)], axis=2)
    drel_t, dsinks = _bias_reduce(dbias.reshape(SWA_HEADS, BLOCK * 2 * BLOCK), onehot, dsink.reshape(SWA_HEADS, BLOCK))
    dkv = jnp.concatenate([band_grad(dkc, dkp), band_grad(dvc, dvp)], axis=1).astype(BF16)
    dproj = lax.dynamic_update_slice(dproj, dkv, (0, OFF_K))
    gw_in_e = _mm_plain("proj_dw", h1, dproj, "tn", d, IN_EXT, s, F32, tn=TAIL)
    gw_in = _fold_swapped(gw_in_e, OFF_KR, IN_COLS).reshape(NDEV, d // NDEV, IN_COLS).astype(BF16)
    tkt = IN_EXT
    dh1, s_in, p_uq, p_ukv = _mm(
        "proj_dx", dproj, w_in_e, "nt", (s // tm, d // tn, IN_EXT // tkt),
        pl.BlockSpec((tm, tkt), lambda i, j, q: (i, q)), pl.BlockSpec((tn, tkt), lambda i, j, q: (j, q)),
        [jax.ShapeDtypeStruct((s, d), F32)], [row_blk], (tm, tn), _store(F32),
        comm=[_PairSwap([gw_in]), _ChipScatter([c_uq, c_ukv])])
    c_in = _pair_sum("pair_in", gw_in, s_in, core)
    gx, dsc1, dsh1, dga, p_in = _norm_mod_bwd("norm1_bwd", x2d, dh1, dx2, attn_norm_g, sc1,
                                              comm=[_ChipScatter([c_in])])

    small = [jnp.concatenate([dsh1, dsc1, dg1, dsh2, dsc2, dg2], axis=1), dga, dgm, dgf, dgq, dgkv,
             dsinks.reshape(1, SWA_HEADS), drel_t.T.reshape(1, REL_BUCKETS * SWA_HEADS)]
    n_small = sum(a.shape[1] for a in small)
    n_pad = -(n_small + 1) % 1024 + 1
    rows_small = (n_small + n_pad) // 128
    pad = jnp.zeros((1, n_pad), F32)
    pack = lambda parts, tail=pad: jnp.concatenate([p.reshape(1, -1) for p in parts] + [tail], axis=1).reshape(rows_small, 128)
    (small_g,) = _exchange("gather_small", _Gather([pack(small, jnp.concatenate([loss_p, pad[:, 1:]], axis=1))]))
    small_names = (b_mod, attn_norm_g, mlp_norm_g, final_norm_g, mla_q_norm_g, mla_kv_norm_g, swa_sinks, rel_bias)
    small_m = (m_b_mod, m_attn_norm_g, m_mlp_norm_g, m_final_norm_g, m_mla_q_norm_g, m_mla_kv_norm_g, m_swa_sinks, m_rel_bias)
    small_v = (v_b_mod, v_attn_norm_g, v_mlp_norm_g, v_final_norm_g, v_mla_q_norm_g, v_mla_kv_norm_g, v_swa_sinks, v_rel_bias)
    small_out = _adamw("adamw_small", pack(small_names), small_g, pack(small_m), pack(small_v), parts=True)

    def unpack(flat):
        flat = flat.reshape(1, -1)
        out, off = [], 0
        for a in small_names:
            out.append(flat[:, off:off + a.size].reshape(a.shape))
            off += a.size
        return out

    sg, sd, sm, sv = [unpack(o) for o in small_out]
    loss = small_out[0].reshape(-1)[n_small]

    dmod_cols = lax.dynamic_slice(small_g.reshape(NDEV, -1), (0, me * nmod), (NDEV, nmod))
    gw_mod = _mod_wgrad(act_all, dmod_cols)
    big = {"w_mod": _adamw("adamw_w_mod", w_mod[0], gw_mod, m_w_mod[0], v_w_mod[0], parts=False)}

    for name, w, p, m, v in (("w_in", w_in, p_in, m_w_in, v_w_in), ("w_uq", w_uq, p_uq, m_w_uq, v_w_uq),
                             ("w_ukv", w_ukv, p_ukv, m_w_ukv, v_w_ukv), ("w_out", w_out, p_out, m_w_out, v_w_out),
                             ("w_ff1", w_ff1, p_ff1, m_w_ff1, v_w_ff1), ("w_ff2", w_ff2, p_ff2, m_w_ff2, v_w_ff2)):
        big[name] = _adamw("adamw_" + name, w[0], p, m[0], v[0], parts=True)

    order = ("w_mod", "b_mod", "attn_norm_g", "w_in", "swa_sinks", "rel_bias", "mla_q_norm_g", "w_uq", "mla_kv_norm_g",
             "w_ukv", "w_out", "mlp_norm_g", "w_ff1", "w_ff2", "final_norm_g")
    small_idx = {"b_mod": 0, "attn_norm_g": 1, "mlp_norm_g": 2, "final_norm_g": 3, "mla_q_norm_g": 4,
                 "mla_kv_norm_g": 5, "swa_sinks": 6, "rel_bias": 7}
    outs = []
    for kind, small_list in enumerate((sg, sd, sm, sv)):
        for name in order:
            outs.append(small_list[small_idx[name]] if name in small_idx else big[name][kind][None])
    return (loss, gx[None], *outs)
```

```python
import functools
import math

import jax
import jax.numpy as jnp
from jax import lax
from jax.experimental import pallas as pl
from jax.experimental.pallas import tpu as pltpu

F32 = jnp.float32
BF16 = jnp.bfloat16

NDEV = 8
EPS = 1e-6
BLOCK = 128
SWA_HEADS, SWA_KV, SWA_DH, SWA_GROUP = 16, 2, 64, 8
REL_BUCKETS, REL_MAX_DIST = 32, 128
MLA_H, Q_RANK, KV_RANK, NOPE, ROPE, VDIM = 8, 384, 128, 128, 64, 128
ROPE_THETA = 10000.0
OFF_K, OFF_V, OFF_CQ, OFF_CKV, OFF_KR, IN_COLS = 1024, 1152, 1280, 1664, 1792, 1856
IN_EXT = IN_COLS + ROPE
TAIL0, TAIL = OFF_CQ, IN_EXT - OFF_CQ
QW = NOPE + 2 * ROPE
MIX = SWA_HEADS * SWA_DH + MLA_H * VDIM
MLA_SCALE = (NOPE + ROPE) ** -0.5
SWA_SCALE = SWA_DH ** -0.5

ADAM_LR, ADAM_B1, ADAM_B2, ADAM_EPS, ADAM_WD, ADAM_STEP = 0.001, 0.9, 0.999, 1e-08, 0.01, 10

VMEM_LIMIT = 52 * 1024 * 1024
ROW_TILE = 256
MM_TM, MM_TN, MM_TK = 1024, 1024, 2048
ATT_T = 512
MLA_HB = 2
ADAM_ELEMS = 256 * 1024


MESH_ID = pl.DeviceIdType.MESH


def _place():
    x, y, c = lax.axis_index("x"), lax.axis_index("y"), lax.axis_index("c")
    return x, y, c, 2 * x + y


def _chip(x, y, k):
    return (1 - x if k & 2 else x, 1 - y if k & 1 else y)


def _dma_sems(*counts):
    return [pltpu.SemaphoreType.DMA((n,)) for n in counts]


class _Gather:
    def __init__(self, arrays):
        self.arrays = list(arrays)
        n = len(self.arrays)
        self.out_shape = [jax.ShapeDtypeStruct((NDEV,) + a.shape, a.dtype) for a in self.arrays]
        self.sems = _dma_sems(7 * n, 7 * n, n)

    def _copy(self, sems, a, k, src, dst, to):
        return pltpu.make_async_remote_copy(src_ref=src, dst_ref=dst, send_sem=sems[0].at[7 * a + k],
                                            recv_sem=sems[1].at[7 * a + k], device_id=to, device_id_type=MESH_ID)

    def start(self, ins, outs, sems):
        x, y, c, q = _place()
        me = 2 * q + c
        for a in range(len(ins)):
            pltpu.make_async_copy(ins[a], outs[a].at[me], sems[2].at[a]).start()
            self._copy(sems, a, 0, ins[a], outs[a].at[me], (x, y, 1 - c)).start()
            for k in (1, 2, 3):
                self._copy(sems, a, k, ins[a], outs[a].at[me], (*_chip(x, y, k), c)).start()

    def finish(self, ins, outs, sems):
        x, y, c, q = _place()
        me, sib = 2 * q + c, (x, y, 1 - c)
        n = len(ins)
        for k in (1, 2, 3):
            for a in range(n):
                blk = outs[a].at[2 * (q ^ k) + c]
                self._copy(sems, a, k, ins[a], blk, (*_chip(x, y, k), c)).wait_recv()
                self._copy(sems, a, 3 + k, blk, blk, sib).start()
        for a in range(n):
            self._copy(sems, a, 0, ins[a], outs[a].at[2 * q + 1 - c], sib).wait_recv()
            for k in (1, 2, 3):
                blk = outs[a].at[2 * (q ^ k) + 1 - c]
                self._copy(sems, a, 3 + k, blk, blk, sib).wait_recv()
        for a in range(n):
            for k in range(7):
                self._copy(sems, a, k, ins[a], outs[a].at[me], sib).wait_send()
            pltpu.make_async_copy(ins[a], outs[a].at[me], sems[2].at[a]).wait()


class _PairSwap:
    def __init__(self, arrays):
        self.arrays = list(arrays)
        n = len(self.arrays)
        self.out_shape = [jax.ShapeDtypeStruct((NDEV // 2,) + a.shape[1:], a.dtype) for a in self.arrays]
        self.sems = _dma_sems(4 * n, 4 * n)

    def _copy(self, sems, a, p, src, dst, to):
        return pltpu.make_async_remote_copy(src_ref=src, dst_ref=dst, send_sem=sems[0].at[4 * a + p],
                                            recv_sem=sems[1].at[4 * a + p], device_id=to, device_id_type=MESH_ID)

    def start(self, ins, outs, sems):
        x, y, c, _ = _place()
        for a in range(len(ins)):
            for p in range(4):
                self._copy(sems, a, p, ins[a].at[2 * p + 1 - c], outs[a].at[p], (x, y, 1 - c)).start()

    def finish(self, ins, outs, sems):
        x, y, c, _ = _place()
        for a in range(len(ins)):
            for p in range(4):
                cp = self._copy(sems, a, p, ins[a].at[2 * p + 1 - c], outs[a].at[p], (x, y, 1 - c))
                cp.wait_recv()
                cp.wait_send()


class _ChipScatter:
    def __init__(self, arrays):
        self.arrays = list(arrays)
        n = len(self.arrays)
        self.out_shape = [jax.ShapeDtypeStruct(a.shape, a.dtype) for a in self.arrays]
        self.sems = _dma_sems(3 * n, 3 * n, n)

    def _copy(self, sems, a, k, src, dst, to):
        return pltpu.make_async_remote_copy(src_ref=src, dst_ref=dst, send_sem=sems[0].at[3 * a + k - 1],
                                            recv_sem=sems[1].at[3 * a + k - 1], device_id=to, device_id_type=MESH_ID)

    def start(self, ins, outs, sems):
        x, y, c, q = _place()
        for a in range(len(ins)):
            pltpu.make_async_copy(ins[a].at[q], outs[a].at[q], sems[2].at[a]).start()
            for k in (1, 2, 3):
                self._copy(sems, a, k, ins[a].at[q ^ k], outs[a].at[q], (*_chip(x, y, k), c)).start()

    def finish(self, ins, outs, sems):
        x, y, c, q = _place()
        for a in range(len(ins)):
            for k in (1, 2, 3):
                cp = self._copy(sems, a, k, ins[a].at[q ^ k], outs[a].at[q ^ k], (*_chip(x, y, k), c))
                cp.wait_recv()
                cp.wait_send()
            pltpu.make_async_copy(ins[a].at[q], outs[a].at[q], sems[2].at[a]).wait()


def _call(body, **kw):
    return pl.pallas_call(body, **kw)


def _pcall(body, comm=None, **kw):
    if not comm:
        return _call(body, **kw)
    grid = kw["grid"]
    in_specs, out_specs, out_shape = list(kw["in_specs"]), list(kw["out_specs"]), list(kw["out_shape"])
    scratch = list(kw.get("scratch_shapes", ()))
    n_in, n_out, n_scr = len(in_specs), len(out_shape), len(scratch)
    n_cin = [len(j.arrays) for j in comm]
    n_sem = [len(j.sems) for j in comm]
    n = sum(n_cin)
    hbm = pl.BlockSpec(memory_space=pltpu.HBM)

    def carried(*refs):
        ins, cins = refs[:n_in], refs[n_in:n_in + n]
        outs, couts = refs[n_in + n:n_in + n + n_out], refs[n_in + n + n_out:n_in + 2 * n + n_out]
        scr, sems = refs[n_in + 2 * n + n_out:n_in + 2 * n + n_out + n_scr], refs[n_in + 2 * n + n_out + n_scr:]
        ids = [pl.program_id(ax) for ax in range(len(grid))]
        first = functools.reduce(jnp.logical_and, [i == 0 for i in ids])
        last = functools.reduce(jnp.logical_and, [i == g - 1 for i, g in zip(ids, grid)])

        def each(method):
            ai = si = 0
            for job, na, ns in zip(comm, n_cin, n_sem):
                getattr(job, method)(cins[ai:ai + na], couts[ai:ai + na], sems[si:si + ns])
                ai, si = ai + na, si + ns

        @pl.when(first)
        def _():
            each("start")

        body(*ins, *outs, *scr)

        @pl.when(last)
        def _():
            each("finish")

    kw.update(in_specs=in_specs + [hbm] * n, out_specs=out_specs + [hbm] * n,
              out_shape=out_shape + [o for j in comm for o in j.out_shape],
              scratch_shapes=scratch + [sm for j in comm for sm in j.sems],
              compiler_params=_cparams(("arbitrary",) * len(grid)))
    call = _call(carried, **kw)
    return lambda *args: call(*args, *[a for j in comm for a in j.arrays])


def _cparams(sem):
    return pltpu.CompilerParams(dimension_semantics=sem, vmem_limit_bytes=VMEM_LIMIT)


def _pick(n, pref, align):
    if n <= pref:
        return n
    t = (pref // align) * align
    while t >= align:
        if n % t == 0:
            return t
        t -= align
    return n


def _split3(x):
    a = x.astype(BF16)
    r = x - a.astype(F32)
    b = r.astype(BF16)
    c = (r - b.astype(F32)).astype(BF16)
    return a, b, c


def _exchange(name, job):
    n = len(job.arrays)

    def body(*refs):
        ins, outs, sems = refs[:n], refs[n:2 * n], refs[2 * n:]
        job.start(ins, outs, sems)
        job.finish(ins, outs, sems)

    hbm = pl.BlockSpec(memory_space=pltpu.HBM)
    return _call(body, name=name, out_shape=job.out_shape, in_specs=[hbm] * n, out_specs=[hbm] * n,
                 scratch_shapes=job.sems)(*job.arrays)


def _pair_sum(name, g, r, core):
    _, rr, cc = g.shape
    tr = rr if rr * cc <= 4 * ADAM_ELEMS else _pick(rr, max(16, 4 * ADAM_ELEMS // cc // 16 * 16), 16)

    def body(g_ref, r_ref, c_ref, o_ref):
        north = c_ref[:, 0:1] > 0.5
        mine = jnp.where(north, g_ref[1].astype(F32), g_ref[0].astype(F32))
        o_ref[...] = (mine + r_ref[...].astype(F32)).astype(o_ref.dtype)

    return _pcall(
        body, name=name, grid=(NDEV // 2, rr // tr),
        in_specs=[pl.BlockSpec((None, 2, tr, cc), lambda p, i: (p, 0, i, 0)),
                  pl.BlockSpec((None, tr, cc), lambda p, i: (p, i, 0)), pl.BlockSpec((1, 128), lambda p, i: (0, 0))],
        out_specs=pl.BlockSpec((None, tr, cc), lambda p, i: (p, i, 0)),
        out_shape=jax.ShapeDtypeStruct((NDEV // 2, rr, cc), g.dtype),
        compiler_params=_cparams(("parallel", "parallel")))(g.reshape(NDEV // 2, 2, rr, cc), r, core)


_DIMS = {"nn": (((1,), (0,)), ((), ())), "nt": (((1,), (1,)), ((), ())), "tn": (((0,), (0,)), ((), ()))}


def _mm(name, a, b, kind, grid, a_spec, b_spec, out_shape, out_specs, acc_shape, epilogue,
        extras=(), extra_specs=(), comm=None, b_parts=1):
    nk, ne, no = grid[2], len(extras), len(out_shape)

    def body(*refs):
        a_ref, b_ref = refs[0], refs[1]
        ex, outs = refs[2:2 + ne], refs[2 + ne:2 + ne + no]
        if b_parts == 1:
            part = lax.dot_general(a_ref[...].astype(BF16), b_ref[...].astype(BF16), _DIMS[kind],
                                   preferred_element_type=F32)
        else:
            kp = a_ref.shape[1] // b_parts
            part = sum(lax.dot_general(a_ref[:, p * kp:(p + 1) * kp].astype(BF16), b_ref[p].astype(BF16), _DIMS[kind],
                                       preferred_element_type=F32) for p in range(b_parts))
        if nk == 1:
            epilogue(part, ex, outs)
            return
        acc = refs[-1]
        k = pl.program_id(2)

        @pl.when(k == 0)
        def _():
            acc[...] = part

        @pl.when(jnp.logical_and(k > 0, k < nk - 1))
        def _():
            acc[...] += part

        @pl.when(k == nk - 1)
        def _():
            epilogue(acc[...] + part, ex, outs)

    return _pcall(
        body, comm=comm, name=name, grid=grid, in_specs=[a_spec, b_spec, *extra_specs], out_specs=out_specs,
        out_shape=out_shape, scratch_shapes=[pltpu.VMEM(acc_shape, F32)] if nk > 1 else [],
        compiler_params=_cparams(("parallel", "parallel", "arbitrary")),
    )(a, b, *extras)


def _store(dtype):
    def epi(acc, ex, outs):
        outs[0][...] = acc.astype(dtype)
    return epi


def _mm_plain(name, a, b, kind, m, n, k, out_dtype, tm=None, tn=None, tk=None):
    tm = _pick(m, tm or MM_TM, 128)
    tn = _pick(n, tn or MM_TN, 128)
    tk = _pick(k, tk or MM_TK, 128)
    a_spec = pl.BlockSpec((tk, tm), lambda i, j, q: (q, i)) if kind == "tn" else pl.BlockSpec((tm, tk), lambda i, j, q: (i, q))
    b_spec = pl.BlockSpec((tn, tk), lambda i, j, q: (j, q)) if kind == "nt" else pl.BlockSpec((tk, tn), lambda i, j, q: (q, j))
    return _mm(name, a, b, kind, (m // tm, n // tn, k // tk), a_spec, b_spec,
               [jax.ShapeDtypeStruct((m, n), out_dtype)], [pl.BlockSpec((tm, tn), lambda i, j, q: (i, j))],
               (tm, tn), _store(out_dtype))[0]


def _row(ts, d):
    return pl.BlockSpec((ts, d), lambda i: (i, 0))


def _vec(d):
    return pl.BlockSpec((1, d), lambda i: (0, 0))


def _norm_mod(name, x, gain, sc, sh, comm=None):
    s, d = x.shape
    ts = _pick(s, ROW_TILE, 16)

    def body(x_ref, g_ref, sc_ref, sh_ref, h_ref):
        xv = x_ref[...]
        r = lax.rsqrt(jnp.mean(xv * xv, axis=-1, keepdims=True) + EPS)
        h_ref[...] = ((xv * r) * g_ref[...] * (1.0 + sc_ref[...]) + sh_ref[...]).astype(BF16)

    return _pcall(body, comm=comm, name=name, grid=(s // ts,), in_specs=[_row(ts, d), _vec(d), _vec(d), _vec(d)],
                  out_specs=[_row(ts, d)], out_shape=[jax.ShapeDtypeStruct((s, d), BF16)],
                  compiler_params=_cparams(("parallel",)))(x, gain, sc, sh)


def _loss_head(x3, tgt, y2, gf, g2):
    s, d = x3.shape
    ts = _pick(s, ROW_TILE, 16)

    def body(x_ref, t_ref, y_ref, gf_ref, g2_ref, dx_ref, dy_ref, loss_ref, dgf_ref, dg2_ref):
        @pl.when(pl.program_id(0) == 0)
        def _():
            loss_ref[...] = jnp.zeros_like(loss_ref)
            dgf_ref[...] = jnp.zeros_like(dgf_ref)
            dg2_ref[...] = jnp.zeros_like(dg2_ref)

        xv = x_ref[...]
        r = lax.rsqrt(jnp.mean(xv * xv, axis=-1, keepdims=True) + EPS)
        xn = xv * r
        err = xn * gf_ref[...] - t_ref[...]
        loss_ref[...] += 0.5 * jnp.sum(jnp.mean(err * err, axis=-1, keepdims=True), axis=0, keepdims=True)
        dout = err * (1.0 / d)
        dgf_ref[...] += jnp.sum(dout * xn, axis=0, keepdims=True)
        dxn = dout * gf_ref[...]
        dx = r * (dxn - xn * jnp.mean(dxn * xn, axis=-1, keepdims=True))
        dx_ref[...] = dx
        dy_ref[...] = (dx * g2_ref[...]).astype(BF16)
        dg2_ref[...] += jnp.sum(dx * y_ref[...], axis=0, keepdims=True)

    one = pl.BlockSpec((1, 1), lambda i: (0, 0))
    return _pcall(
        body, name="loss_head", grid=(s // ts,),
        in_specs=[_row(ts, d), _row(ts, d), _row(ts, d), _vec(d), _vec(d)],
        out_specs=[_row(ts, d), _row(ts, d), one, _vec(d), _vec(d)],
        out_shape=[jax.ShapeDtypeStruct((s, d), F32), jax.ShapeDtypeStruct((s, d), BF16),
                   jax.ShapeDtypeStruct((1, 1), F32), jax.ShapeDtypeStruct((1, d), F32),
                   jax.ShapeDtypeStruct((1, d), F32)],
        compiler_params=_cparams(("arbitrary",)))(x3, tgt, y2, gf, g2)


def _norm_mod_bwd(name, x, dh, dres, gain, sc, y_prev=None, gate=None, comm=None):
    s, d = x.shape
    ts = _pick(s, ROW_TILE, 16)
    gated = y_prev is not None

    def body(*refs):
        if gated:
            x_ref, dh_ref, dr_ref, g_ref, sc_ref, y_ref, gt_ref, dx_ref, dy_ref, dsc_ref, dsh_ref, dg_ref, dgt_ref = refs
        else:
            x_ref, dh_ref, dr_ref, g_ref, sc_ref, dx_ref, dsc_ref, dsh_ref, dg_ref = refs

        @pl.when(pl.program_id(0) == 0)
        def _():
            dsc_ref[...] = jnp.zeros_like(dsc_ref)
            dsh_ref[...] = jnp.zeros_like(dsh_ref)
            dg_ref[...] = jnp.zeros_like(dg_ref)
            if gated:
                dgt_ref[...] = jnp.zeros_like(dgt_ref)

        xv, dhv = x_ref[...], dh_ref[...]
        r = lax.rsqrt(jnp.mean(xv * xv, axis=-1, keepdims=True) + EPS)
        xn = xv * r
        dsc_ref[...] += jnp.sum(dhv * (xn * g_ref[...]), axis=0, keepdims=True)
        dsh_ref[...] += jnp.sum(dhv, axis=0, keepdims=True)
        da = dhv * (1.0 + sc_ref[...])
        dg_ref[...] += jnp.sum(da * xn, axis=0, keepdims=True)
        dxn = da * g_ref[...]
        dx = dr_ref[...] + r * (dxn - xn * jnp.mean(dxn * xn, axis=-1, keepdims=True))
        dx_ref[...] = dx
        if gated:
            dy_ref[...] = (dx * gt_ref[...]).astype(BF16)
            dgt_ref[...] += jnp.sum(dx * y_ref[...], axis=0, keepdims=True)

    ins = [x, dh, dres, gain, sc] + ([y_prev, gate] if gated else [])
    in_specs = [_row(ts, d)] * 3 + [_vec(d)] * 2 + ([_row(ts, d), _vec(d)] if gated else [])
    vec_out = jax.ShapeDtypeStruct((1, d), F32)
    out_shape = [jax.ShapeDtypeStruct((s, d), F32)] + ([jax.ShapeDtypeStruct((s, d), BF16)] if gated else [])
    out_shape += [vec_out] * (4 if gated else 3)
    out_specs = [_row(ts, d)] * (2 if gated else 1) + [_vec(d)] * (4 if gated else 3)
    return _pcall(body, comm=comm, name=name, grid=(s // ts,), in_specs=in_specs, out_specs=out_specs,
                  out_shape=out_shape, compiler_params=_cparams(("arbitrary",)))(*ins)


def _dot3(a, b, dims):
    a1, a2, _ = _split3(a)
    b1, b2, _ = _split3(b)
    dot = functools.partial(lax.dot_general, dimension_numbers=dims, preferred_element_type=F32)
    return dot(a1, b1) + (dot(a1, b2) + dot(a2, b1))


def _mod_fwd(c_all, w, b_cols, comm=None):
    nb, d = c_all.shape
    n = w.shape[1]
    tk = _pick(d, 512, 128)
    nk = d // tk

    def body(c_ref, w_ref, b_ref, act_ref, out_ref):
        k = pl.program_id(0)
        cv = c_ref[...]
        act = cv * (1.0 / (1.0 + jnp.exp(-cv)))
        act_ref[...] = act

        @pl.when(k == 0)
        def _():
            out_ref[...] = jnp.broadcast_to(b_ref[...], out_ref.shape)

        out_ref[...] += _dot3(act, w_ref[...], _DIMS["nn"])

    return _pcall(
        body, comm=comm, name="mod_fwd", grid=(nk,),
        in_specs=[pl.BlockSpec((nb, tk), lambda k: (0, k)), pl.BlockSpec((tk, n), lambda k: (k, 0)),
                  pl.BlockSpec((1, n), lambda k: (0, 0))],
        out_specs=[pl.BlockSpec((nb, tk), lambda k: (0, k)), pl.BlockSpec((nb, n), lambda k: (0, 0))],
        out_shape=[jax.ShapeDtypeStruct((nb, d), F32), jax.ShapeDtypeStruct((nb, n), F32)],
        compiler_params=_cparams(("arbitrary",)))(c_all, w, b_cols)


def _mod_wgrad(act_all, dmod_cols):
    nb, d = act_all.shape
    n = dmod_cols.shape[1]
    tm = _pick(d, 512, 128)

    def body(a_ref, d_ref, o_ref):
        o_ref[...] = _dot3(a_ref[...], d_ref[...], _DIMS["tn"])

    return _pcall(
        body, name="mod_wgrad", grid=(d // tm,),
        in_specs=[pl.BlockSpec((nb, tm), lambda i: (0, i)), pl.BlockSpec((nb, n), lambda i: (0, 0))],
        out_specs=pl.BlockSpec((tm, n), lambda i: (i, 0)), out_shape=jax.ShapeDtypeStruct((d, n), F32),
        compiler_params=_cparams(("parallel",)))(act_all, dmod_cols)


def _bias_expand(rel_t, onehot_t):
    h, _ = rel_t.shape
    n = onehot_t.shape[1]

    def body(r_ref, o_ref, out_ref):
        a, b, c = _split3(r_ref[...])
        dot = functools.partial(lax.dot_general, dimension_numbers=_DIMS["nn"], preferred_element_type=F32)
        oh = o_ref[...]
        out_ref[...] = dot(a, oh) + (dot(b, oh) + dot(c, oh))

    full = lambda shp: pl.BlockSpec(shp, lambda: (0,) * len(shp))
    return _pcall(body, name="bias_expand", in_specs=[full(rel_t.shape), full(onehot_t.shape)],
                  out_specs=full((h, n)), out_shape=jax.ShapeDtypeStruct((h, n), F32),
                  compiler_params=pltpu.CompilerParams(vmem_limit_bytes=VMEM_LIMIT))(rel_t, onehot_t)


def _bias_reduce(dbias, onehot, dsink_rows):
    h, n = dbias.shape

    def body(d_ref, o_ref, s_ref, out_ref, so_ref):
        a, b, c = _split3(d_ref[...])
        dot = functools.partial(lax.dot_general, dimension_numbers=_DIMS["nn"], preferred_element_type=F32)
        oh = o_ref[...]
        out_ref[...] = dot(a, oh) + (dot(b, oh) + dot(c, oh))
        so_ref[...] = jnp.sum(s_ref[...], axis=-1, keepdims=True)

    full = lambda shp: pl.BlockSpec(shp, lambda: (0,) * len(shp))
    return _pcall(body, name="bias_reduce", in_specs=[full(dbias.shape), full(onehot.shape), full(dsink_rows.shape)],
                  out_specs=[full((h, REL_BUCKETS)), full((h, 1))],
                  out_shape=[jax.ShapeDtypeStruct((h, REL_BUCKETS), F32), jax.ShapeDtypeStruct((h, 1), F32)],
                  compiler_params=pltpu.CompilerParams(vmem_limit_bytes=VMEM_LIMIT))(dbias, onehot, dsink_rows)


PAIRS = SWA_GROUP // 2
PROWS = PAIRS * BLOCK
PCOLS = 2 * 2 * BLOCK


def _swa2_specs():
    tok = lambda width: pl.BlockSpec((BLOCK, width), lambda g, n: (n, g))
    prev = pl.BlockSpec((None, BLOCK, 2 * SWA_DH), lambda g, n: (g, jnp.maximum(n - 1, 0), 0))
    cur = pl.BlockSpec((None, BLOCK, 2 * SWA_DH), lambda g, n: (g, n, 0))
    bias_spec = pl.BlockSpec((None, PROWS, PCOLS), lambda g, n: (g, 0, 0))
    col_spec = pl.BlockSpec((None, PROWS, 1), lambda g, n: (g, 0, 0))
    lse_spec = pl.BlockSpec((None, None, PROWS, 1), lambda g, n: (g, n, 0, 0))
    return tok, prev, cur, bias_spec, col_spec, lse_spec


def _stack_pairs(blk):
    return jnp.concatenate([blk[:, p * 2 * SWA_DH:(p + 1) * 2 * SWA_DH] for p in range(PAIRS)], axis=0)


def _band(tp, tc, bp, bc):
    return jnp.concatenate([tp[...], tc[...], bp[...], bc[...]], axis=0)


def _swa2_scores(q_ref, kd, bias_ref, n):
    q2 = _stack_pairs(q_ref[...])
    s2 = lax.dot_general(q2, kd, _DIMS["nt"], preferred_element_type=F32) * SWA_SCALE + bias_ref[...]
    col = lax.broadcasted_iota(jnp.int32, s2.shape, 1)
    before_start = jnp.logical_and(n == 0, (col & (2 * BLOCK - 1)) < BLOCK)
    return q2, jnp.where(before_start, -jnp.inf, s2)


def _swa2_fwd(src, ktop, kbot, vtop, vbot, bias, sinks, comm=None):
    s = src.shape[0]
    nb = s // BLOCK
    tok, prev, cur, bias_spec, col_spec, lse_spec = _swa2_specs()

    def body(q_ref, ktp, ktc, kbp, kbc, vtp, vtc, vbp, vbc, bias_ref, sa_ref, sb_ref, o_ref, la_ref, lb_ref):
        n = pl.program_id(1)
        _, s2 = _swa2_scores(q_ref, _band(ktp, ktc, kbp, kbc), bias_ref, n)
        row = lax.broadcasted_iota(jnp.int32, (PCOLS, 2 * SWA_DH), 0)
        lane = lax.broadcasted_iota(jnp.int32, (PCOLS, 2 * SWA_DH), 1)
        ones = jnp.where(lane == row // (2 * BLOCK), 1.0, 0.0).astype(BF16)
        ps, ms, sinks_ = [], [], []
        for half, sink_ref in enumerate((sa_ref, sb_ref)):
            sc = s2[:, half * 2 * BLOCK:(half + 1) * 2 * BLOCK]
            m = jnp.maximum(jnp.max(sc, axis=-1, keepdims=True), sink_ref[...])
            ps.append(jnp.exp(sc - m).astype(BF16))
            ms.append(m)
        acc = lax.dot_general(jnp.concatenate(ps, axis=1), jnp.concatenate([_band(vtp, vtc, vbp, vbc), ones], axis=1),
                              _DIMS["nn"], preferred_element_type=F32)
        dens = []
        for half, (sink_ref, lse_ref) in enumerate(((sa_ref, la_ref), (sb_ref, lb_ref))):
            den = acc[:, 2 * SWA_DH + half:2 * SWA_DH + half + 1] + jnp.exp(sink_ref[...] - ms[half])
            lse_ref[...] = ms[half] + jnp.log(den)
            dens.append(den)
        lo = lax.broadcasted_iota(jnp.int32, (PROWS, 2 * SWA_DH), 1) < SWA_DH
        o2 = acc[:, 0:2 * SWA_DH] / jnp.where(lo, dens[0], dens[1])
        for p in range(PAIRS):
            o_ref[:, p * 2 * SWA_DH:(p + 1) * 2 * SWA_DH] = o2[p * BLOCK:(p + 1) * BLOCK].astype(BF16)

    lse_shape = jax.ShapeDtypeStruct((SWA_KV, nb, PROWS, 1), F32)
    return _pcall(
        body, comm=comm, name="swa_fwd", grid=(SWA_KV, nb),
        in_specs=[tok(PROWS), prev, cur, prev, cur, prev, cur, prev, cur, bias_spec, col_spec, col_spec],
        out_specs=[tok(PROWS), lse_spec, lse_spec],
        out_shape=[jax.ShapeDtypeStruct((s, MIX), BF16), lse_shape, lse_shape],
        compiler_params=_cparams(("parallel", "parallel")))(
            src, ktop, ktop, kbot, kbot, vtop, vtop, vbot, vbot, bias, sinks[0], sinks[1])


def _swa2_bwd(src, dsrc, ktop, kbot, vtop, vbot, lses, bias, sinks, comm=None):
    s = src.shape[0]
    nb = s // BLOCK
    tok, prev, cur, bias_spec, col_spec, lse_spec = _swa2_specs()
    lane_lo = lambda shape: lax.broadcasted_iota(jnp.int32, shape, 1) < SWA_DH

    def body(q_ref, do_ref, ktp, ktc, kbp, kbc, vtp, vtc, vbp, vbc, la_ref, lb_ref, bias_ref, sa_ref, sb_ref,
             dq_ref, dkp_ref, dkc_ref, dvp_ref, dvc_ref, dbias_ref, dsa_ref, dsb_ref):
        n = pl.program_id(1)

        @pl.when(n == 0)
        def _():
            dbias_ref[...] = jnp.zeros_like(dbias_ref)
            dsa_ref[...] = jnp.zeros_like(dsa_ref)
            dsb_ref[...] = jnp.zeros_like(dsb_ref)

        kd = _band(ktp, ktc, kbp, kbc)
        q2, s2 = _swa2_scores(q_ref, kd, bias_ref, n)
        do2 = _stack_pairs(do_ref[...]).astype(BF16)
        dp2 = lax.dot_general(do2, _band(vtp, vtc, vbp, vbc), _DIMS["nt"], preferred_element_type=F32)
        ps, dss = [], []
        for half, (sink_ref, lse_ref, dsink_ref) in enumerate(((sa_ref, la_ref, dsa_ref), (sb_ref, lb_ref, dsb_ref))):
            cols = slice(half * 2 * BLOCK, (half + 1) * 2 * BLOCK)
            lse_v = lse_ref[...]
            p = jnp.exp(s2[:, cols] - lse_v)
            dp = dp2[:, cols]
            delta = jnp.sum(p * dp, axis=-1, keepdims=True)
            ds = p * (dp - delta)
            dsink_ref[...] += -jnp.exp(sink_ref[...] - lse_v) * delta
            ps.append(p.astype(BF16))
            dss.append(ds)
        ds2 = jnp.concatenate(dss, axis=1)
        dbias_ref[...] += ds2
        dsb2 = (ds2 * SWA_SCALE).astype(BF16)
        dq2 = lax.dot_general(dsb2, kd, _DIMS["nn"], preferred_element_type=F32)
        for p in range(PAIRS):
            dq_ref[:, p * 2 * SWA_DH:(p + 1) * 2 * SWA_DH] = dq2[p * BLOCK:(p + 1) * BLOCK].astype(BF16)
        dk = lax.dot_general(dsb2, q2, _DIMS["tn"], preferred_element_type=F32)
        dv = lax.dot_general(jnp.concatenate(ps, axis=1), do2, _DIMS["tn"], preferred_element_type=F32)
        for full, prev_ref, cur_ref in ((dk, dkp_ref, dkc_ref), (dv, dvp_ref, dvc_ref)):
            own = jnp.where(lane_lo((2 * BLOCK, 2 * SWA_DH)), full[:2 * BLOCK], full[2 * BLOCK:])
            prev_ref[...] = own[:BLOCK]
            cur_ref[...] = own[BLOCK:]

    kv_out = jax.ShapeDtypeStruct((SWA_KV, s, 2 * SWA_DH), F32)
    col_out = jax.ShapeDtypeStruct((SWA_KV, PROWS, 1), F32)
    return _pcall(
        body, comm=comm, name="swa_bwd", grid=(SWA_KV, nb),
        in_specs=[tok(PROWS), tok(PROWS), prev, cur, prev, cur, prev, cur, prev, cur, lse_spec, lse_spec, bias_spec,
                  col_spec, col_spec],
        out_specs=[tok(PROWS), cur, cur, cur, cur, bias_spec, col_spec, col_spec],
        out_shape=[jax.ShapeDtypeStruct((s, IN_EXT), BF16), kv_out, kv_out, kv_out, kv_out,
                   jax.ShapeDtypeStruct(bias.shape, F32), col_out, col_out],
        compiler_params=_cparams(("arbitrary", "arbitrary")))(
            src, dsrc, ktop, ktop, kbot, kbot, vtop, vtop, vbot, vbot, lses[0], lses[1], bias, sinks[0], sinks[1])


def _rope_slab(slab, table):
    t = slab * table
    return t + pltpu.roll(t, ROPE, 1)


def _low_lanes(v):
    lane = lax.broadcasted_iota(jnp.int32, v.shape, 1)
    return jnp.where(lane < ROPE, v, 0.0)


def _rms(xv, g):
    r = lax.rsqrt(jnp.mean(xv * xv, axis=-1, keepdims=True) + EPS)
    return xv * r, r


def _mla_prep(proj, gq, gkv, table):
    s = proj.shape[0]
    ts = _pick(s, ROW_TILE, 16)

    def body(p_ref, gq_ref, gkv_ref, t_ref, cq_ref, ckv_ref, kr_ref):
        xq, _ = _rms(p_ref[:, 0:Q_RANK], None)
        cq_ref[...] = (xq * gq_ref[...]).astype(BF16)
        xkv, _ = _rms(p_ref[:, Q_RANK:Q_RANK + KV_RANK], None)
        ckv_ref[...] = (xkv * gkv_ref[...]).astype(BF16)
        kr_ref[...] = _low_lanes(_rope_slab(p_ref[:, Q_RANK + KV_RANK:TAIL], t_ref[...]))

    return _pcall(
        body, name="mla_prep", grid=(s // ts,),
        in_specs=[pl.BlockSpec((ts, TAIL), lambda i: (i, TAIL0 // TAIL)), _vec(Q_RANK), _vec(KV_RANK), _row(ts, 2 * ROPE)],
        out_specs=[_row(ts, Q_RANK), _row(ts, KV_RANK), _row(ts, 2 * ROPE)],
        out_shape=[jax.ShapeDtypeStruct((s, Q_RANK), BF16), jax.ShapeDtypeStruct((s, KV_RANK), BF16),
                   jax.ShapeDtypeStruct((s, 2 * ROPE), F32)],
        compiler_params=_cparams(("parallel",)))(proj, gq, gkv, table)


def _mla_prep_bwd(proj, dcq, dckv, dkr, gq, gkv, table, dproj):
    s = proj.shape[0]
    ts = _pick(s, ROW_TILE, 16)

    def norm_bwd(xv, dy, g):
        xn, r = _rms(xv, None)
        dg = jnp.sum(dy * xn, axis=0, keepdims=True)
        dxn = dy * g
        return r * (dxn - xn * jnp.mean(dxn * xn, axis=-1, keepdims=True)), dg

    def body(p_ref, dcq_ref, dckv_ref, dkr_ref, gq_ref, gkv_ref, t_ref, _, dt_ref, dgq_ref, dgkv_ref):
        @pl.when(pl.program_id(0) == 0)
        def _():
            dgq_ref[...] = jnp.zeros_like(dgq_ref)
            dgkv_ref[...] = jnp.zeros_like(dgkv_ref)

        dxq, dgq = norm_bwd(p_ref[:, 0:Q_RANK], dcq_ref[...], gq_ref[...])
        dxkv, dgkv = norm_bwd(p_ref[:, Q_RANK:Q_RANK + KV_RANK], dckv_ref[...], gkv_ref[...])
        dgq_ref[...] += dgq
        dgkv_ref[...] += dgkv
        d = _low_lanes(dkr_ref[...])
        dslab = (d + pltpu.roll(d, ROPE, 1)) * t_ref[...]
        dt_ref[:, 0:Q_RANK] = dxq.astype(BF16)
        dt_ref[:, Q_RANK:Q_RANK + KV_RANK] = dxkv.astype(BF16)
        dt_ref[:, Q_RANK + KV_RANK:TAIL] = dslab.astype(BF16)

    return _pcall(
        body, name="mla_prep_bwd", grid=(s // ts,),
        in_specs=[pl.BlockSpec((ts, TAIL), lambda i: (i, TAIL0 // TAIL)), _row(ts, Q_RANK), _row(ts, KV_RANK),
                  _row(ts, 2 * ROPE), _vec(Q_RANK), _vec(KV_RANK), _row(ts, 2 * ROPE), pl.BlockSpec(memory_space=pl.ANY)],
        out_specs=[pl.BlockSpec((ts, TAIL), lambda i: (i, TAIL0 // TAIL)), _vec(Q_RANK), _vec(KV_RANK)],
        out_shape=[jax.ShapeDtypeStruct(dproj.shape, BF16), jax.ShapeDtypeStruct((1, Q_RANK), F32),
                   jax.ShapeDtypeStruct((1, KV_RANK), F32)],
        input_output_aliases={7: 0},
        compiler_params=_cparams(("arbitrary",)))(proj, dcq, dckv, dkr, gq, gkv, table, dproj)


def _head_specs(ts):
    tok = lambda w: pl.BlockSpec((ts, w), lambda h, i: (i, 0))
    head = lambda w: pl.BlockSpec((None, ts, w), lambda h, i: (h, i, 0))
    wgt = lambda r, c: pl.BlockSpec((None, r, c), lambda h, i: (h, 0, 0))
    return tok, head, wgt


def _mla_qkv(cq, ckv, kr, wq, wkv, table):
    s = cq.shape[0]
    ts = _pick(s, 4 * ROW_TILE, 16)
    tok, head, wgt = _head_specs(ts)

    def body(cq_ref, ckv_ref, kr_ref, wq_ref, wkv_ref, t_ref, q_ref, k_ref, v_ref):
        qf = lax.dot_general(cq_ref[...], wq_ref[...], _DIMS["nn"], preferred_element_type=F32)
        q_ref[:, 0:NOPE] = qf[:, 0:NOPE].astype(BF16)
        q_ref[:, NOPE:QW] = _rope_slab(qf[:, NOPE:QW], t_ref[...]).astype(BF16)
        kv = lax.dot_general(ckv_ref[...], wkv_ref[...], _DIMS["nn"], preferred_element_type=F32)
        k_ref[:, 0:NOPE] = kv[:, 0:NOPE].astype(BF16)
        k_ref[:, NOPE:QW] = kr_ref[...].astype(BF16)
        v_ref[:, 0:VDIM] = kv[:, NOPE:NOPE + VDIM].astype(BF16)
        lane = lax.broadcasted_iota(jnp.int32, (ts, VDIM), 1)
        v_ref[:, VDIM:2 * VDIM] = jnp.where(lane == 0, 1.0, 0.0).astype(BF16)

    return _pcall(
        body, name="mla_qkv", grid=(MLA_H, s // ts),
        in_specs=[tok(Q_RANK), tok(KV_RANK), tok(2 * ROPE), wgt(Q_RANK, QW), wgt(KV_RANK, NOPE + VDIM), tok(2 * ROPE)],
        out_specs=[head(QW), head(QW), head(2 * VDIM)],
        out_shape=[jax.ShapeDtypeStruct((MLA_H, s, QW), BF16), jax.ShapeDtypeStruct((MLA_H, s, QW), BF16),
                   jax.ShapeDtypeStruct((MLA_H, s, 2 * VDIM), BF16)],
        compiler_params=_cparams(("parallel", "parallel")))(cq, ckv, kr, wq, wkv, table)


def _mla_qkv_bwd(dq, dk, dv, cq, ckv, wq, wkv, table, comm=None):
    s = cq.shape[0]
    ts = _pick(s, 4 * ROW_TILE, 16)
    tok, head, wgt = _head_specs(ts)
    whole = lambda w: pl.BlockSpec((s, w), lambda h, i: (0, 0))

    def body(dq_ref, dk_ref, dv_ref, cq_ref, ckv_ref, wq_ref, wkv_ref, t_ref,
             dcq_ref, dckv_ref, dkr_ref, gwq_ref, gwkv_ref):
        h, i = pl.program_id(0), pl.program_id(1)
        rows = pl.ds(pl.multiple_of(i * ts, ts), ts)
        d = dq_ref[:, NOPE:QW]
        dslab = (d + pltpu.roll(d, ROPE, 1)) * t_ref[...]
        dqe = jnp.concatenate([dq_ref[:, 0:NOPE], dslab], axis=1).astype(BF16)
        dkv = jnp.concatenate([dk_ref[:, 0:NOPE], dv_ref[...]], axis=1).astype(BF16)
        dcq = lax.dot_general(dqe, wq_ref[...], _DIMS["nt"], preferred_element_type=F32)
        dckv = lax.dot_general(dkv, wkv_ref[...], _DIMS["nt"], preferred_element_type=F32)
        gwq = lax.dot_general(cq_ref[...], dqe, _DIMS["tn"], preferred_element_type=F32)
        gwkv = lax.dot_general(ckv_ref[...], dkv, _DIMS["tn"], preferred_element_type=F32)
        dkr = dk_ref[:, NOPE:QW].astype(F32)

        @pl.when(h == 0)
        def _():
            dcq_ref[rows, :] = dcq
            dckv_ref[rows, :] = dckv
            dkr_ref[rows, :] = dkr

        @pl.when(h > 0)
        def _():
            dcq_ref[rows, :] += dcq
            dckv_ref[rows, :] += dckv
            dkr_ref[rows, :] += dkr

        @pl.when(i == 0)
        def _():
            gwq_ref[...] = gwq
            gwkv_ref[...] = gwkv

        @pl.when(i > 0)
        def _():
            gwq_ref[...] += gwq
            gwkv_ref[...] += gwkv

    return _pcall(
        body, comm=comm, name="mla_qkv_bwd", grid=(MLA_H, s // ts),
        in_specs=[head(QW), head(QW), head(VDIM), tok(Q_RANK), tok(KV_RANK), wgt(Q_RANK, QW),
                  wgt(KV_RANK, NOPE + VDIM), tok(2 * ROPE)],
        out_specs=[whole(Q_RANK), whole(KV_RANK), whole(2 * ROPE), wgt(Q_RANK, QW), wgt(KV_RANK, NOPE + VDIM)],
        out_shape=[jax.ShapeDtypeStruct((s, Q_RANK), F32), jax.ShapeDtypeStruct((s, KV_RANK), F32),
                   jax.ShapeDtypeStruct((s, 2 * ROPE), F32), jax.ShapeDtypeStruct((MLA_H, Q_RANK, QW), F32),
                   jax.ShapeDtypeStruct((MLA_H, KV_RANK, NOPE + VDIM), F32)],
        compiler_params=_cparams(("arbitrary", "arbitrary")))(dq, dk, dv, cq, ckv, wq, wkv, table)


def _diag_mask(t):
    return lax.broadcasted_iota(jnp.int32, (t, t), 1) <= lax.broadcasted_iota(jnp.int32, (t, t), 0)


def _mla_fwd(q, k, v, mix, comm=None):
    s = q.shape[1]
    t = _pick(s, ATT_T, 128)
    nt = s // t
    assert nt % 2 == 0
    hb = MLA_H

    def fold(p, u):
        first = u <= p
        return jnp.where(first, p, nt - 1 - p), jnp.where(first, u, u - p - 1)

    to_log2 = MLA_SCALE * math.log2(math.e)

    def body(q_ref, k_ref, v_ref, _, o_ref, oh_ref, lse_ref, m_ref, acc_ref):
        i, j = fold(pl.program_id(1), pl.program_id(2))

        @pl.when(j == 0)
        def _():
            m_ref[...] = jnp.full_like(m_ref, -jnp.inf)
            acc_ref[...] = jnp.zeros_like(acc_ref)

        def step(diagonal):
            for h in range(hb):
                sc = lax.dot_general(q_ref[h], k_ref[h], _DIMS["nt"], preferred_element_type=F32)
                if diagonal:
                    sc = jnp.where(_diag_mask(t), sc, -jnp.inf)
                m_old = m_ref[h]
                m_new = jnp.maximum(m_old, jnp.max(sc, axis=-1, keepdims=True))
                alpha = jnp.exp2((m_old - m_new) * to_log2)
                p = jnp.exp2((sc - m_new) * to_log2)
                acc_ref[h] = alpha * acc_ref[h] + lax.dot_general(p.astype(BF16), v_ref[h], _DIMS["nn"],
                                                                  preferred_element_type=F32)
                m_ref[h] = m_new

        @pl.when(j < i)
        def _():
            step(False)

        @pl.when(j == i)
        def _():
            step(True)
            for h in range(hb):
                den = acc_ref[h, :, VDIM:VDIM + 1]
                o = acc_ref[h, :, 0:VDIM] / den
                o_ref[:, h * VDIM:(h + 1) * VDIM] = o
                oh_ref[:, h * VDIM:(h + 1) * VDIM] = o.astype(BF16)
                lse_ref[h] = m_ref[h] * MLA_SCALE + jnp.log(den)

    o_spec = pl.BlockSpec((t, hb * VDIM), lambda h, p, u: (fold(p, u)[0], h))
    first = (MIX - MLA_H * VDIM) // (hb * VDIM)
    mix_spec = pl.BlockSpec((t, hb * VDIM), lambda h, p, u: (fold(p, u)[0], first + h))
    return _pcall(
        body, comm=comm, name="mla_fwd", grid=(MLA_H // hb, nt // 2, nt + 1),
        in_specs=[pl.BlockSpec((hb, t, QW), lambda h, p, u: (h, fold(p, u)[0], 0)),
                  pl.BlockSpec((hb, t, QW), lambda h, p, u: (h, fold(p, u)[1], 0)),
                  pl.BlockSpec((hb, t, 2 * VDIM), lambda h, p, u: (h, fold(p, u)[1], 0)),
                  pl.BlockSpec(memory_space=pl.ANY)],
        out_specs=[o_spec, mix_spec, pl.BlockSpec((hb, t, 1), lambda h, p, u: (h, fold(p, u)[0], 0))],
        out_shape=[jax.ShapeDtypeStruct((s, MLA_H * VDIM), F32), jax.ShapeDtypeStruct(mix.shape, BF16),
                   jax.ShapeDtypeStruct((MLA_H, s, 1), F32)],
        input_output_aliases={3: 1},
        scratch_shapes=[pltpu.VMEM((hb, t, 1), F32), pltpu.VMEM((hb, t, 2 * VDIM), F32)],
        compiler_params=_cparams(("parallel", "parallel", "arbitrary")))(q, k, v, mix)


def _mla_delta(dmix, o):
    s = o.shape[0]
    ts = _pick(s, 2 * ROW_TILE, 16)
    w = MLA_H * VDIM

    def body(d_ref, o_ref, out_ref):
        prod = d_ref[...] * o_ref[...]
        for h in range(MLA_H):
            out_ref[h] = jnp.sum(prod[:, h * VDIM:(h + 1) * VDIM], axis=-1, keepdims=True)

    return _pcall(body, name="mla_delta", grid=(s // ts,),
                  in_specs=[pl.BlockSpec((ts, w), lambda i: (i, SWA_HEADS * SWA_DH // w)), pl.BlockSpec((ts, w), lambda i: (i, 0))],
                  out_specs=pl.BlockSpec((MLA_H, ts, 1), lambda i: (0, i, 0)),
                  out_shape=jax.ShapeDtypeStruct((MLA_H, s, 1), F32), compiler_params=_cparams(("parallel",)))(dmix, o)


def _mla_bwd(q, k, v, dmix, delta, lse, comm=None):
    s = q.shape[1]
    t = _pick(s, ATT_T, 128)
    nt = s // t
    assert nt % 2 == 0
    hb = 2 * MLA_HB
    o_blk0 = SWA_HEADS * SWA_DH // (hb * VDIM)

    def fold(p, u):
        first = u < nt - p
        return jnp.where(first, p, nt - 1 - p), jnp.where(first, p + u, u - 1)

    log2e = math.log2(math.e)

    def body(q_ref, k_ref, v_ref, do_ref, delta_ref, lse_ref, dq_ref, dk_ref, dv_ref, dk_acc, dv_acc):
        j, i = fold(pl.program_id(1), pl.program_id(2))
        rows = pl.ds(pl.multiple_of(i * t, t), t)

        @pl.when(i == j)
        def _():
            dk_acc[...] = jnp.zeros_like(dk_acc)
            dv_acc[...] = jnp.zeros_like(dv_acc)

        def step(diagonal):
            for h in range(hb):
                qv, kv_ = q_ref[h], k_ref[h]
                dob = do_ref[:, h * VDIM:(h + 1) * VDIM].astype(BF16)
                st = lax.dot_general(kv_, qv, _DIMS["nt"], preferred_element_type=F32)
                pt = jnp.exp2(st * (MLA_SCALE * log2e) - lse_ref[h] * log2e)
                if diagonal:
                    keep = lax.broadcasted_iota(jnp.int32, (t, t), 0) <= lax.broadcasted_iota(jnp.int32, (t, t), 1)
                    pt = jnp.where(keep, pt, 0.0)
                dpt = lax.dot_general(v_ref[h], dob, _DIMS["nt"], preferred_element_type=F32)
                dst = (pt * (dpt - delta_ref[h]) * MLA_SCALE).astype(BF16)
                dv_acc[h] += lax.dot_general(pt.astype(BF16), dob, _DIMS["nn"], preferred_element_type=F32)
                dk_acc[h] += lax.dot_general(dst, qv, _DIMS["nn"], preferred_element_type=F32)
                dqv = lax.dot_general(dst, kv_, _DIMS["tn"], preferred_element_type=F32)

                @pl.when(j == 0)
                def _():
                    dq_ref[h, rows, :] = dqv

                @pl.when(j > 0)
                def _():
                    dq_ref[h, rows, :] += dqv

        @pl.when(i > j)
        def _():
            step(False)

        @pl.when(i == j)
        def _():
            step(True)

        @pl.when(i == nt - 1)
        def _():
            dk_ref[...] = dk_acc[...].astype(BF16)
            dv_ref[...] = dv_acc[...].astype(BF16)

    qi = lambda h, p, u: (h, fold(p, u)[1], 0)
    kj = lambda h, p, u: (h, fold(p, u)[0], 0)
    row = pl.BlockSpec((hb, 1, t), lambda h, p, u: (h, 0, fold(p, u)[1]))
    return _pcall(
        body, comm=comm, name="mla_bwd", grid=(MLA_H // hb, nt // 2, nt + 1),
        in_specs=[pl.BlockSpec((hb, t, QW), qi), pl.BlockSpec((hb, t, QW), kj), pl.BlockSpec((hb, t, VDIM), kj),
                  pl.BlockSpec((t, hb * VDIM), lambda h, p, u: (fold(p, u)[1], o_blk0 + h)), row, row],
        out_specs=[pl.BlockSpec((hb, s, QW), lambda h, p, u: (h, 0, 0)), pl.BlockSpec((hb, t, QW), kj),
                   pl.BlockSpec((hb, t, VDIM), kj)],
        out_shape=[jax.ShapeDtypeStruct((MLA_H, s, QW), F32), jax.ShapeDtypeStruct((MLA_H, s, QW), BF16),
                   jax.ShapeDtypeStruct((MLA_H, s, VDIM), BF16)],
        scratch_shapes=[pltpu.VMEM((hb, t, QW), F32), pltpu.VMEM((hb, t, VDIM), F32)],
        compiler_params=_cparams(("arbitrary", "arbitrary", "arbitrary")))(q, k, v, dmix, delta, lse)


def _adamw(name, w, g, m, v, parts):
    r, c = w.shape
    n_parts = g.shape[0] if parts else 1
    tr = r if r * c <= ADAM_ELEMS else _pick(r, max(8, ADAM_ELEMS // c // 8 * 8), 8)
    c1 = 1.0 - ADAM_B1 ** ADAM_STEP
    c2 = 1.0 - ADAM_B2 ** ADAM_STEP

    def body(w_ref, g_ref, m_ref, v_ref, go_ref, d_ref, mo_ref, vo_ref):
        if parts:
            gv = g_ref[0].astype(F32)
            for j in range(1, n_parts):
                gv = gv + g_ref[j].astype(F32)
        else:
            gv = g_ref[...]
        mv = ADAM_B1 * m_ref[...] + (1.0 - ADAM_B1) * gv
        vv = ADAM_B2 * v_ref[...] + (1.0 - ADAM_B2) * (gv * gv)
        go_ref[...] = gv
        mo_ref[...] = mv
        vo_ref[...] = vv
        d_ref[...] = -ADAM_LR * ((mv / c1) / (jnp.sqrt(vv / c2) + ADAM_EPS) + ADAM_WD * w_ref[...])

    blk = pl.BlockSpec((tr, c), lambda i: (i, 0))
    g_spec = pl.BlockSpec((n_parts, tr, c), lambda i: (0, i, 0)) if parts else blk
    out = jax.ShapeDtypeStruct((r, c), F32)
    return _pcall(body, name=name, grid=(r // tr,), in_specs=[blk, g_spec, blk, blk], out_specs=[blk] * 4,
                  out_shape=[out] * 4, compiler_params=_cparams(("parallel",)))(w, g, m, v)


def _t5_bucket(dist):
    n = jnp.maximum(dist, 0)
    max_exact = REL_BUCKETS // 2
    nf = jnp.maximum(n, 1).astype(F32)
    large = max_exact + (jnp.log(nf / max_exact) / math.log(REL_MAX_DIST / max_exact)
                         * (REL_BUCKETS - max_exact)).astype(jnp.int32)
    return jnp.where(n < max_exact, n, jnp.minimum(large, REL_BUCKETS - 1))


def _swap_halves(w, r0):
    return jnp.concatenate([w[:, r0 + ROPE // 2:r0 + ROPE], w[:, r0:r0 + ROPE // 2]], axis=1)


def _fold_swapped(g, r0, width):
    sw = g[..., width:width + ROPE]
    half = ROPE // 2
    return jnp.concatenate([g[..., :r0], g[..., r0:r0 + half] + sw[..., half:], g[..., r0 + half:r0 + ROPE] + sw[..., :half],
                            g[..., r0 + ROPE:width]], axis=-1)


def kernel(x, c, w_mod, b_mod, attn_norm_g, w_in, swa_sinks, rel_bias, mla_q_norm_g, w_uq, mla_kv_norm_g, w_ukv, w_out, mlp_norm_g, w_ff1, w_ff2, final_norm_g, loss_target, m_w_mod, m_b_mod, m_attn_norm_g, m_w_in, m_swa_sinks, m_rel_bias, m_mla_q_norm_g, m_w_uq, m_mla_kv_norm_g, m_w_ukv, m_w_out, m_mlp_norm_g, m_w_ff1, m_w_ff2, m_final_norm_g, v_w_mod, v_b_mod, v_attn_norm_g, v_w_in, v_swa_sinks, v_rel_bias, v_mla_q_norm_g, v_w_uq, v_mla_kv_norm_g, v_w_ukv, v_w_out, v_mlp_norm_g, v_w_ff1, v_w_ff2, v_final_norm_g):
    s, d = x.shape[1], x.shape[2]
    ffs = w_ff1.shape[2]
    ff = ffs * NDEV
    nmod = w_mod.shape[2]
    me = 4 * lax.axis_index("x") + 2 * lax.axis_index("y") + lax.axis_index("c")
    x2d, tgt = x[0], loss_target[0]
    final_g = final_norm_g.reshape(1, d)

    w_in_l = jnp.concatenate([w_in[0], _swap_halves(w_in[0], OFF_KR)], axis=1).astype(BF16)
    w_uq_l = jnp.concatenate([w_uq[0], _swap_halves(w_uq[0], NOPE)], axis=1).astype(BF16)
    core = jnp.full((1, 128), lax.axis_index("c"), F32)
    (c_all,) = _exchange("gather_c", _Gather([c]))

    b_cols = lax.dynamic_slice(b_mod, (0, me * nmod), (1, nmod))
    act_all, mod_cols = _mod_fwd(c_all.reshape(NDEV, d), w_mod[0], b_cols)
    (mod_g,) = _exchange("gather_mod", _Gather([mod_cols]))
    mod = lax.dynamic_index_in_dim(mod_g, me, axis=1, keepdims=False).reshape(1, 6 * d)
    sh1, sc1, g1, sh2, sc2, g2 = [mod[:, i * d:(i + 1) * d] for i in range(6)]

    pos = jnp.arange(s, dtype=F32)
    inv_freq = ROPE_THETA ** (-jnp.arange(ROPE // 2, dtype=F32) / (ROPE // 2))
    ang = pos[:, None] * inv_freq[None, :]
    cos, sin = jnp.cos(ang), jnp.sin(ang)
    table = jnp.concatenate([cos, cos, -sin, sin], axis=1)
    q_loc = jnp.arange(BLOCK)[:, None]
    k_loc = jnp.arange(2 * BLOCK)[None, :]
    dist = q_loc + BLOCK - k_loc
    in_window = (dist >= 0) & (dist < BLOCK)
    onehot = (_t5_bucket(dist).reshape(-1, 1) == jnp.arange(REL_BUCKETS)[None, :]).astype(BF16)
    bias = _bias_expand(rel_bias.T, onehot.T).reshape(SWA_HEADS, BLOCK, 2 * BLOCK)
    bias = jnp.where(in_window[None], bias, -jnp.inf).reshape(SWA_KV, PAIRS, 2, BLOCK, 2 * BLOCK)
    bias = bias.transpose(0, 1, 3, 2, 4).reshape(SWA_KV, PROWS, PCOLS)
    sinks = jnp.broadcast_to(swa_sinks.reshape(SWA_KV, PAIRS, 1, 2), (SWA_KV, PAIRS, BLOCK, 2)).reshape(SWA_KV, PROWS, 2)
    sinks = (sinks[:, :, 0:1], sinks[:, :, 1:2])

    h1, w_in_g, w_uq_g, w_ukv_g = _norm_mod("norm1", x2d, attn_norm_g, sc1, sh1,
                                            comm=[_Gather([w_in_l, w_uq_l, w_ukv[0].astype(BF16)])])
    w_in_e = w_in_g.reshape(d, IN_EXT)

    def both_dtypes(acc, ex, outs):
        outs[0][...] = acc
        outs[1][...] = acc.astype(BF16)

    tmp = _pick(s, MM_TM // 2, 128)
    proj_blk = pl.BlockSpec((tmp, IN_EXT), lambda i, j, q: (i, 0))
    proj, proj_h = _mm("proj", h1, w_in_e, "nn", (s // tmp, 1, 1), pl.BlockSpec((tmp, d), lambda i, j, q: (i, 0)),
                       pl.BlockSpec((d, IN_EXT), lambda i, j, q: (0, 0)),
                       [jax.ShapeDtypeStruct((s, IN_EXT), F32), jax.ShapeDtypeStruct((s, IN_EXT), BF16)],
                       [proj_blk, proj_blk], (tmp, IN_EXT), both_dtypes)
    def diag_pair(tok):
        x = jnp.stack([tok[:, :SWA_DH], tok[:, SWA_DH:]])
        zero = jnp.zeros_like(x)
        return jnp.concatenate([x, zero], axis=2), jnp.concatenate([zero, x], axis=2)

    k_top, k_bot = diag_pair(proj_h[:, OFF_K:OFF_V])
    v_top, v_bot = diag_pair(proj_h[:, OFF_V:OFF_CQ])
    o_a, lse_a0, lse_a1, w_out_g = _swa2_fwd(proj_h, k_top, k_bot, v_top, v_bot, bias, sinks,
                                             comm=[_Gather([w_out[0].astype(BF16)])])
    w_out_f = w_out_g.reshape(MIX, d)

    cq, ckv, kr = _mla_prep(proj, mla_q_norm_g, mla_kv_norm_g, table)
    q_b, k_b, v_b = _mla_qkv(cq, ckv, kr, w_uq_g, w_ukv_g, table)
    o_b, mix, lse_b, w_ff1_g = _mla_fwd(q_b, k_b, v_b, o_a, comm=[_Gather([w_ff1[0].astype(BF16)])])

    tm, tn, tk = _pick(s, MM_TM, 128), _pick(d, MM_TN, 128), _pick(MIX, MM_TK, 128)
    row_blk = pl.BlockSpec((tm, tn), lambda i, j, q: (i, j))
    gate_blk = pl.BlockSpec((1, tn), lambda i, j, q: (0, j))

    def gated_residual(acc, ex, outs):
        outs[0][...] = acc
        outs[1][...] = ex[0][...] + ex[1][...] * acc

    y1, x2 = _mm("out_proj", mix, w_out_f, "nn", (s // tm, d // tn, MIX // tk),
                 pl.BlockSpec((tm, tk), lambda i, j, q: (i, q)), pl.BlockSpec((tk, tn), lambda i, j, q: (q, j)),
                 [jax.ShapeDtypeStruct((s, d), F32)] * 2, [row_blk, row_blk], (tm, tn), gated_residual,
                 extras=(x2d, g1), extra_specs=(row_blk, gate_blk))

    (h2,) = _norm_mod("norm2", x2, mlp_norm_g, sc2, sh2)
    tnf, tkd = _pick(ffs, MM_TN, 128), _pick(d, MM_TK, 128)
    rf = ffs // tnf
    ff_blk = pl.BlockSpec((tm, tnf), lambda i, j, q: (i, j))

    def relu_sq(acc, ex, outs):
        u = jnp.maximum(acc, 0.0)
        outs[0][...] = u
        outs[1][...] = (u * u).astype(BF16)

    u, uu, w_ff2_g = _mm("ff1", h2, w_ff1_g, "nn", (s // tm, ff // tnf, d // tkd),
                         pl.BlockSpec((tm, tkd), lambda i, j, q: (i, q)),
                         pl.BlockSpec((None, tkd, tnf), lambda i, j, q: (j // rf, q, j % rf)),
                         [jax.ShapeDtypeStruct((s, ff), F32), jax.ShapeDtypeStruct((s, ff), BF16)], [ff_blk, ff_blk],
                         (tm, tnf), relu_sq, comm=[_Gather([w_ff2[0].astype(BF16)])])
    w_ff2_f = w_ff2_g.reshape(ff, d)
    tkf = _pick(ff, MM_TK, 128)
    y2, x3 = _mm("ff2", uu, w_ff2_f, "nn", (s // tm, d // tn, ff // tkf),
                 pl.BlockSpec((tm, tkf), lambda i, j, q: (i, q)), pl.BlockSpec((tkf, tn), lambda i, j, q: (q, j)),
                 [jax.ShapeDtypeStruct((s, d), F32)] * 2, [row_blk, row_blk], (tm, tn), gated_residual,
                 extras=(x2, g2), extra_specs=(row_blk, gate_blk))

    dx3, dy2, loss_p, dgf, dg2 = _loss_head(x3, tgt, y2, final_g, g2)

    def relu_sq_bwd(acc, ex, outs):
        outs[0][...] = (acc * (2.0 * ex[0][...])).astype(BF16)

    tnf2 = _pick(ff, MM_TN, 128)
    du = _mm("ff2_dx", dy2, w_ff2_f, "nt", (s // tm, ff // tnf2, d // tkd),
             pl.BlockSpec((tm, tkd), lambda i, j, q: (i, q)), pl.BlockSpec((tnf2, tkd), lambda i, j, q: (j, q)),
             [jax.ShapeDtypeStruct((s, ff), BF16)], [pl.BlockSpec((tm, tnf2), lambda i, j, q: (i, j))],
             (tm, tnf2), relu_sq_bwd, extras=(u,), extra_specs=(pl.BlockSpec((tm, tnf2), lambda i, j, q: (i, j)),))[0]
    gw_ff2 = _mm_plain("ff2_dw", uu, dy2, "tn", ff, d, s, BF16)
    tmd, tks = _pick(d, MM_TM, 128), _pick(s, MM_TK, 128)
    gw_ff2 = gw_ff2.reshape(NDEV, ffs, d)
    dh2, s_ff2 = _mm("ff1_dx", du, w_ff1_g, "nt", (s // tm, d // tn, NDEV // 2),
                     pl.BlockSpec((tm, 2 * ffs), lambda i, j, q: (i, q)),
                     pl.BlockSpec((2, tn, ffs), lambda i, j, q: (q, j, 0)),
                     [jax.ShapeDtypeStruct((s, d), F32)], [row_blk], (tm, tn), _store(F32),
                     comm=[_PairSwap([gw_ff2])], b_parts=2)
    c_ff2 = _pair_sum("pair_ff2", gw_ff2, s_ff2, core)
    gw_ff1, p_ff2 = _mm("ff1_dw", h2, du, "tn", (d // tmd, ff // tnf, s // tks),
                        pl.BlockSpec((tks, tmd), lambda i, j, q: (q, i)), pl.BlockSpec((tks, tnf), lambda i, j, q: (q, j)),
                        [jax.ShapeDtypeStruct((NDEV, d, ffs), BF16)],
                        [pl.BlockSpec((None, tmd, tnf), lambda i, j, q: (j // rf, i, j % rf))], (tmd, tnf), _store(BF16),
                        comm=[_ChipScatter([c_ff2])])
    dx2, dy1, dsc2, dsh2, dgm, dg1, s_ff1 = _norm_mod_bwd("norm2_bwd", x2, dh2, dx3, mlp_norm_g, sc2, y1, g1,
                                                          comm=[_PairSwap([gw_ff1])])
    c_ff1 = _pair_sum("pair_ff1", gw_ff1, s_ff1, core)

    dmix = _mm_plain("out_proj_dx", dy1, w_out_f, "nt", s, MIX, d, F32)
    gw_out = _mm_plain("out_proj_dw", mix, dy1, "tn", MIX, d, s, BF16).reshape(NDEV, MIX // NDEV, d)

    delta_b = _mla_delta(dmix, o_b).reshape(MLA_H, 1, s)
    dq_b, dk_b, dv_b, p_ff1, s_out = _mla_bwd(q_b, k_b, v_b, dmix, delta_b, lse_b.reshape(MLA_H, 1, s),
                                              comm=[_ChipScatter([c_ff1]), _PairSwap([gw_out])])
    c_out = _pair_sum("pair_out", gw_out, s_out, core)
    dcq, dckv, dkr, gw_uq_e, gw_ukv, p_out = _mla_qkv_bwd(dq_b, dk_b, dv_b, cq, ckv, w_uq_g, w_ukv_g, table,
                                                          comm=[_ChipScatter([c_out])])
    gw_uq = _fold_swapped(gw_uq_e, NOPE, NOPE + ROPE).astype(BF16)
    gw_ukv = gw_ukv.astype(BF16)

    dproj, dkp, dkc, dvp, dvc, dbias, dsink0, dsink1, s_uq, s_ukv = _swa2_bwd(
        proj_h, dmix, k_top, k_bot, v_top, v_bot, (lse_a0, lse_a1), bias, sinks, comm=[_PairSwap([gw_uq, gw_ukv])])
    c_uq = _pair_sum("pair_uq", gw_uq, s_uq, core)
    c_ukv = _pair_sum("pair_ukv", gw_ukv, s_ukv, core)
    dproj, dgq, dgkv = _mla_prep_bwd(proj, dcq, dckv, dkr, mla_q_norm_g, mla_kv_norm_g, table, dproj)

    def band_grad(cur, prv):
        g = cur + jnp.concatenate([prv[:, BLOCK:], jnp.zeros_like(prv[:, :BLOCK])], axis=1)
        g = g[:, :, :SWA_DH] + g[:, :, SWA_DH:]
        return jnp.concatenate([g[0], g[1]], axis=1)

    dbias = dbias.reshape(SWA_KV, PAIRS, BLOCK, 2, 2 * BLOCK).transpose(0, 1, 3, 2, 4)
    dsink = jnp.stack([dsink0.reshape(SWA_KV, PAIRS, BLOCK), dsink1.reshape(SWA_KV, PAIRS, BLOCK)], axis=2)
    drel_t, dsinks = _bias_reduce(dbias.reshape(SWA_HEADS, BLOCK * 2 * BLOCK), onehot, dsink.reshape(SWA_HEADS, BLOCK))
    dkv = jnp.concatenate([band_grad(dkc, dkp), band_grad(dvc, dvp)], axis=1).astype(BF16)
    dproj = lax.dynamic_update_slice(dproj, dkv, (0, OFF_K))
    def fold_rotary(acc, ex, outs):
        slab = acc[:, TAIL - 2 * ROPE:TAIL]
        lane = lax.broadcasted_iota(jnp.int32, slab.shape, 1)
        folded = slab + jnp.where(lane < ROPE // 2, pltpu.roll(slab, ROPE // 2, 1), pltpu.roll(slab, 3 * ROPE // 2, 1))
        last = pl.program_id(1) == IN_EXT // TAIL - 1
        outs[0][:, 0:TAIL - 2 * ROPE] = acc[:, 0:TAIL - 2 * ROPE].astype(BF16)
        outs[0][:, TAIL - 2 * ROPE:TAIL] = jnp.where(last, folded, slab).astype(BF16)

    tmw, tks = _pick(d, MM_TM, 128), _pick(s, MM_TK, 128)
    gw_in = _mm("proj_dw", h1, dproj, "tn", (d // tmw, IN_EXT // TAIL, s // tks),
                pl.BlockSpec((tks, tmw), lambda i, j, q: (q, i)), pl.BlockSpec((tks, TAIL), lambda i, j, q: (q, j)),
                [jax.ShapeDtypeStruct((d, IN_COLS), BF16)], [pl.BlockSpec((tmw, TAIL), lambda i, j, q: (i, j))],
                (tmw, TAIL), fold_rotary)[0].reshape(NDEV, d // NDEV, IN_COLS)
    tkt = IN_EXT
    dh1, s_in, p_uq, p_ukv = _mm(
        "proj_dx", dproj, w_in_e, "nt", (s // tm, d // tn, IN_EXT // tkt),
        pl.BlockSpec((tm, tkt), lambda i, j, q: (i, q)), pl.BlockSpec((tn, tkt), lambda i, j, q: (j, q)),
        [jax.ShapeDtypeStruct((s, d), F32)], [row_blk], (tm, tn), _store(F32),
        comm=[_PairSwap([gw_in]), _ChipScatter([c_uq, c_ukv])])
    c_in = _pair_sum("pair_in", gw_in, s_in, core)
    gx, dsc1, dsh1, dga, p_in = _norm_mod_bwd("norm1_bwd", x2d, dh1, dx2, attn_norm_g, sc1,
                                              comm=[_ChipScatter([c_in])])

    small = [jnp.concatenate([dsh1, dsc1, dg1, dsh2, dsc2, dg2], axis=1), dga, dgm, dgf, dgq, dgkv,
             dsinks.reshape(1, SWA_HEADS), drel_t.T.reshape(1, REL_BUCKETS * SWA_HEADS)]
    n_small = sum(a.shape[1] for a in small)
    n_pad = -(n_small + 1) % 1024 + 1
    rows_small = (n_small + n_pad) // 128
    pad = jnp.zeros((1, n_pad), F32)
    pack = lambda parts, tail=pad: jnp.concatenate([p.reshape(1, -1) for p in parts] + [tail], axis=1).reshape(rows_small, 128)
    (small_g,) = _exchange("gather_small", _Gather([pack(small, jnp.concatenate([loss_p, pad[:, 1:]], axis=1))]))
    small_names = (b_mod, attn_norm_g, mlp_norm_g, final_norm_g, mla_q_norm_g, mla_kv_norm_g, swa_sinks, rel_bias)
    small_m = (m_b_mod, m_attn_norm_g, m_mlp_norm_g, m_final_norm_g, m_mla_q_norm_g, m_mla_kv_norm_g, m_swa_sinks, m_rel_bias)
    small_v = (v_b_mod, v_attn_norm_g, v_mlp_norm_g, v_final_norm_g, v_mla_q_norm_g, v_mla_kv_norm_g, v_swa_sinks, v_rel_bias)
    small_out = _adamw("adamw_small", pack(small_names), small_g, pack(small_m), pack(small_v), parts=True)

    def unpack(flat):
        flat = flat.reshape(1, -1)
        out, off = [], 0
        for a in small_names:
            out.append(flat[:, off:off + a.size].reshape(a.shape))
            off += a.size
        return out

    sg, sd, sm, sv = [unpack(o) for o in small_out]
    loss = small_out[0].reshape(-1)[n_small]

    dmod_cols = lax.dynamic_slice(small_g.reshape(NDEV, -1), (0, me * nmod), (NDEV, nmod))
    gw_mod = _mod_wgrad(act_all, dmod_cols)
    big = {"w_mod": _adamw("adamw_w_mod", w_mod[0], gw_mod, m_w_mod[0], v_w_mod[0], parts=False)}

    for name, w, p, m, v in (("w_in", w_in, p_in, m_w_in, v_w_in), ("w_uq", w_uq, p_uq, m_w_uq, v_w_uq),
                             ("w_ukv", w_ukv, p_ukv, m_w_ukv, v_w_ukv), ("w_out", w_out, p_out, m_w_out, v_w_out),
                             ("w_ff1", w_ff1, p_ff1, m_w_ff1, v_w_ff1), ("w_ff2", w_ff2, p_ff2, m_w_ff2, v_w_ff2)):
        big[name] = _adamw("adamw_" + name, w[0], p, m[0], v[0], parts=True)

    order = ("w_mod", "b_mod", "attn_norm_g", "w_in", "swa_sinks", "rel_bias", "mla_q_norm_g", "w_uq", "mla_kv_norm_g",
             "w_ukv", "w_out", "mlp_norm_g", "w_ff1", "w_ff2", "final_norm_g")
    small_idx = {"b_mod": 0, "attn_norm_g": 1, "mlp_norm_g": 2, "final_norm_g": 3, "mla_q_norm_g": 4,
                 "mla_kv_norm_g": 5, "swa_sinks": 6, "rel_bias": 7}
    outs = []
    for kind, small_list in enumerate((sg, sd, sm, sv)):
        for name in order:
            outs.append(small_list[small_idx[name]] if name in small_idx else big[name][kind][None])
    return (loss, gx[None], *outs)
```

```python
import functools
import math

import jax
import jax.numpy as jnp
from jax import lax
from jax.experimental import pallas as pl
from jax.experimental.pallas import tpu as pltpu

F32 = jnp.float32
BF16 = jnp.bfloat16

NDEV = 8
EPS = 1e-6
BLOCK = 128
SWA_HEADS, SWA_KV, SWA_DH, SWA_GROUP = 16, 2, 64, 8
REL_BUCKETS, REL_MAX_DIST = 32, 128
MLA_H, Q_RANK, KV_RANK, NOPE, ROPE, VDIM = 8, 384, 128, 128, 64, 128
ROPE_THETA = 10000.0
OFF_K, OFF_V, OFF_CQ, OFF_CKV, OFF_KR, IN_COLS = 1024, 1152, 1280, 1664, 1792, 1856
IN_EXT = IN_COLS + ROPE
TAIL0, TAIL = OFF_CQ, IN_EXT - OFF_CQ
QW = NOPE + 2 * ROPE
MIX = SWA_HEADS * SWA_DH + MLA_H * VDIM
MLA_SCALE = (NOPE + ROPE) ** -0.5
SWA_SCALE = SWA_DH ** -0.5

ADAM_LR, ADAM_B1, ADAM_B2, ADAM_EPS, ADAM_WD, ADAM_STEP = 0.001, 0.9, 0.999, 1e-08, 0.01, 10

VMEM_LIMIT = 52 * 1024 * 1024
ROW_TILE = 256
MM_TM, MM_TN, MM_TK = 1024, 1024, 2048
ATT_T = 512
MLA_HB = 2
ADAM_ELEMS = 256 * 1024


MESH_ID = pl.DeviceIdType.MESH


def _place():
    x, y, c = lax.axis_index("x"), lax.axis_index("y"), lax.axis_index("c")
    return x, y, c, 2 * x + y


def _chip(x, y, k):
    return (1 - x if k & 2 else x, 1 - y if k & 1 else y)


def _dma_sems(*counts):
    return [pltpu.SemaphoreType.DMA((n,)) for n in counts]


class _Gather:
    def __init__(self, arrays):
        self.arrays = list(arrays)
        n = len(self.arrays)
        self.out_shape = [jax.ShapeDtypeStruct((NDEV,) + a.shape, a.dtype) for a in self.arrays]
        self.sems = _dma_sems(7 * n, 7 * n, n)

    def _copy(self, sems, a, k, src, dst, to):
        return pltpu.make_async_remote_copy(src_ref=src, dst_ref=dst, send_sem=sems[0].at[7 * a + k],
                                            recv_sem=sems[1].at[7 * a + k], device_id=to, device_id_type=MESH_ID)

    def start(self, ins, outs, sems):
        x, y, c, q = _place()
        me = 2 * q + c
        for a in range(len(ins)):
            pltpu.make_async_copy(ins[a], outs[a].at[me], sems[2].at[a]).start()
            self._copy(sems, a, 0, ins[a], outs[a].at[me], (x, y, 1 - c)).start()
            for k in (1, 2, 3):
                self._copy(sems, a, k, ins[a], outs[a].at[me], (*_chip(x, y, k), c)).start()

    def relay(self, ins, outs, sems):
        x, y, c, q = _place()
        sib = (x, y, 1 - c)
        for k in (1, 2, 3):
            for a in range(len(ins)):
                blk = outs[a].at[2 * (q ^ k) + c]
                self._copy(sems, a, k, ins[a], blk, (*_chip(x, y, k), c)).wait_recv()
                self._copy(sems, a, 3 + k, blk, blk, sib).start()

    def finish(self, ins, outs, sems):
        x, y, c, q = _place()
        me, sib = 2 * q + c, (x, y, 1 - c)
        n = len(ins)
        for a in range(n):
            self._copy(sems, a, 0, ins[a], outs[a].at[2 * q + 1 - c], sib).wait_recv()
            for k in (1, 2, 3):
                blk = outs[a].at[2 * (q ^ k) + 1 - c]
                self._copy(sems, a, 3 + k, blk, blk, sib).wait_recv()
        for a in range(n):
            for k in range(7):
                self._copy(sems, a, k, ins[a], outs[a].at[me], sib).wait_send()
            pltpu.make_async_copy(ins[a], outs[a].at[me], sems[2].at[a]).wait()


class _PairSwap:
    def __init__(self, arrays):
        self.arrays = list(arrays)
        n = len(self.arrays)
        self.out_shape = [jax.ShapeDtypeStruct((NDEV // 2,) + a.shape[1:], a.dtype) for a in self.arrays]
        self.sems = _dma_sems(4 * n, 4 * n)

    def _copy(self, sems, a, p, src, dst, to):
        return pltpu.make_async_remote_copy(src_ref=src, dst_ref=dst, send_sem=sems[0].at[4 * a + p],
                                            recv_sem=sems[1].at[4 * a + p], device_id=to, device_id_type=MESH_ID)

    def start(self, ins, outs, sems):
        x, y, c, _ = _place()
        for a in range(len(ins)):
            for p in range(4):
                self._copy(sems, a, p, ins[a].at[2 * p + 1 - c], outs[a].at[p], (x, y, 1 - c)).start()

    def finish(self, ins, outs, sems):
        x, y, c, _ = _place()
        for a in range(len(ins)):
            for p in range(4):
                cp = self._copy(sems, a, p, ins[a].at[2 * p + 1 - c], outs[a].at[p], (x, y, 1 - c))
                cp.wait_recv()
                cp.wait_send()


class _ChipScatter:
    def __init__(self, arrays):
        self.arrays = list(arrays)
        n = len(self.arrays)
        self.out_shape = [jax.ShapeDtypeStruct(a.shape, a.dtype) for a in self.arrays]
        self.sems = _dma_sems(3 * n, 3 * n, n)

    def _copy(self, sems, a, k, src, dst, to):
        return pltpu.make_async_remote_copy(src_ref=src, dst_ref=dst, send_sem=sems[0].at[3 * a + k - 1],
                                            recv_sem=sems[1].at[3 * a + k - 1], device_id=to, device_id_type=MESH_ID)

    def start(self, ins, outs, sems):
        x, y, c, q = _place()
        for a in range(len(ins)):
            pltpu.make_async_copy(ins[a].at[q], outs[a].at[q], sems[2].at[a]).start()
            for k in (1, 2, 3):
                self._copy(sems, a, k, ins[a].at[q ^ k], outs[a].at[q], (*_chip(x, y, k), c)).start()

    def finish(self, ins, outs, sems):
        x, y, c, q = _place()
        for a in range(len(ins)):
            for k in (1, 2, 3):
                cp = self._copy(sems, a, k, ins[a].at[q ^ k], outs[a].at[q ^ k], (*_chip(x, y, k), c))
                cp.wait_recv()
                cp.wait_send()
            pltpu.make_async_copy(ins[a].at[q], outs[a].at[q], sems[2].at[a]).wait()


def _call(body, **kw):
    return pl.pallas_call(body, **kw)


def _pcall(body, comm=None, **kw):
    if not comm:
        return _call(body, **kw)
    grid = kw["grid"]
    in_specs, out_specs, out_shape = list(kw["in_specs"]), list(kw["out_specs"]), list(kw["out_shape"])
    scratch = list(kw.get("scratch_shapes", ()))
    n_in, n_out, n_scr = len(in_specs), len(out_shape), len(scratch)
    n_cin = [len(j.arrays) for j in comm]
    n_sem = [len(j.sems) for j in comm]
    n = sum(n_cin)
    hbm = pl.BlockSpec(memory_space=pltpu.HBM)

    def carried(*refs):
        ins, cins = refs[:n_in], refs[n_in:n_in + n]
        outs, couts = refs[n_in + n:n_in + n + n_out], refs[n_in + n + n_out:n_in + 2 * n + n_out]
        scr, sems = refs[n_in + 2 * n + n_out:n_in + 2 * n + n_out + n_scr], refs[n_in + 2 * n + n_out + n_scr:]
        ids = [pl.program_id(ax) for ax in range(len(grid))]
        first = functools.reduce(jnp.logical_and, [i == 0 for i in ids])
        last = functools.reduce(jnp.logical_and, [i == g - 1 for i, g in zip(ids, grid)])

        def each(method):
            ai = si = 0
            for job, na, ns in zip(comm, n_cin, n_sem):
                if hasattr(job, method):
                    getattr(job, method)(cins[ai:ai + na], couts[ai:ai + na], sems[si:si + ns])
                ai, si = ai + na, si + ns

        @pl.when(first)
        def _():
            each("start")

        steps = math.prod(grid)
        if steps > 1:
            at, rest = [], steps - 2
            for g in reversed(grid):
                at.append(rest % g)
                rest //= g
            before_last = functools.reduce(jnp.logical_and, [i == a for i, a in zip(ids, reversed(at))])

            @pl.when(before_last)
            def _():
                each("relay")

        body(*ins, *outs, *scr)

        @pl.when(last)
        def _():
            if steps == 1:
                each("relay")
            each("finish")

    kw.update(in_specs=in_specs + [hbm] * n, out_specs=out_specs + [hbm] * n,
              out_shape=out_shape + [o for j in comm for o in j.out_shape],
              scratch_shapes=scratch + [sm for j in comm for sm in j.sems],
              compiler_params=_cparams(("arbitrary",) * len(grid)))
    call = _call(carried, **kw)
    return lambda *args: call(*args, *[a for j in comm for a in j.arrays])


def _cparams(sem):
    return pltpu.CompilerParams(dimension_semantics=sem, vmem_limit_bytes=VMEM_LIMIT)


def _pick(n, pref, align):
    if n <= pref:
        return n
    t = (pref // align) * align
    while t >= align:
        if n % t == 0:
            return t
        t -= align
    return n


def _split3(x):
    a = x.astype(BF16)
    r = x - a.astype(F32)
    b = r.astype(BF16)
    c = (r - b.astype(F32)).astype(BF16)
    return a, b, c


def _exchange(name, job):
    n = len(job.arrays)

    def body(*refs):
        ins, outs, sems = refs[:n], refs[n:2 * n], refs[2 * n:]
        job.start(ins, outs, sems)
        if hasattr(job, "relay"):
            job.relay(ins, outs, sems)
        job.finish(ins, outs, sems)

    hbm = pl.BlockSpec(memory_space=pltpu.HBM)
    return _call(body, name=name, out_shape=job.out_shape, in_specs=[hbm] * n, out_specs=[hbm] * n,
                 scratch_shapes=job.sems)(*job.arrays)


def _pair_sum(name, g, r, core):
    _, rr, cc = g.shape
    tr = rr if rr * cc <= 4 * ADAM_ELEMS else _pick(rr, max(16, 4 * ADAM_ELEMS // cc // 16 * 16), 16)

    def body(g_ref, r_ref, c_ref, o_ref):
        north = c_ref[:, 0:1] > 0.5
        mine = jnp.where(north, g_ref[1].astype(F32), g_ref[0].astype(F32))
        o_ref[...] = (mine + r_ref[...].astype(F32)).astype(o_ref.dtype)

    return _pcall(
        body, name=name, grid=(NDEV // 2, rr // tr),
        in_specs=[pl.BlockSpec((None, 2, tr, cc), lambda p, i: (p, 0, i, 0)),
                  pl.BlockSpec((None, tr, cc), lambda p, i: (p, i, 0)), pl.BlockSpec((1, 128), lambda p, i: (0, 0))],
        out_specs=pl.BlockSpec((None, tr, cc), lambda p, i: (p, i, 0)),
        out_shape=jax.ShapeDtypeStruct((NDEV // 2, rr, cc), g.dtype),
        compiler_params=_cparams(("parallel", "parallel")))(g.reshape(NDEV // 2, 2, rr, cc), r, core)


_DIMS = {"nn": (((1,), (0,)), ((), ())), "nt": (((1,), (1,)), ((), ())), "tn": (((0,), (0,)), ((), ()))}


def _mm(name, a, b, kind, grid, a_spec, b_spec, out_shape, out_specs, acc_shape, epilogue,
        extras=(), extra_specs=(), comm=None, b_parts=1):
    nk, ne, no = grid[2], len(extras), len(out_shape)

    def body(*refs):
        a_ref, b_ref = refs[0], refs[1]
        ex, outs = refs[2:2 + ne], refs[2 + ne:2 + ne + no]
        if b_parts == 1:
            part = lax.dot_general(a_ref[...].astype(BF16), b_ref[...].astype(BF16), _DIMS[kind],
                                   preferred_element_type=F32)
        else:
            kp = a_ref.shape[1] // b_parts
            part = sum(lax.dot_general(a_ref[:, p * kp:(p + 1) * kp].astype(BF16), b_ref[p].astype(BF16), _DIMS[kind],
                                       preferred_element_type=F32) for p in range(b_parts))
        if nk == 1:
            epilogue(part, ex, outs)
            return
        acc = refs[-1]
        k = pl.program_id(2)

        @pl.when(k == 0)
        def _():
            acc[...] = part

        @pl.when(jnp.logical_and(k > 0, k < nk - 1))
        def _():
            acc[...] += part

        @pl.when(k == nk - 1)
        def _():
            epilogue(acc[...] + part, ex, outs)

    return _pcall(
        body, comm=comm, name=name, grid=grid, in_specs=[a_spec, b_spec, *extra_specs], out_specs=out_specs,
        out_shape=out_shape, scratch_shapes=[pltpu.VMEM(acc_shape, F32)] if nk > 1 else [],
        compiler_params=_cparams(("parallel", "parallel", "arbitrary")),
    )(a, b, *extras)


def _store(dtype):
    def epi(acc, ex, outs):
        outs[0][...] = acc.astype(dtype)
    return epi


def _mm_plain(name, a, b, kind, m, n, k, out_dtype, tm=None, tn=None, tk=None):
    tm = _pick(m, tm or MM_TM, 128)
    tn = _pick(n, tn or MM_TN, 128)
    tk = _pick(k, tk or MM_TK, 128)
    a_spec = pl.BlockSpec((tk, tm), lambda i, j, q: (q, i)) if kind == "tn" else pl.BlockSpec((tm, tk), lambda i, j, q: (i, q))
    b_spec = pl.BlockSpec((tn, tk), lambda i, j, q: (j, q)) if kind == "nt" else pl.BlockSpec((tk, tn), lambda i, j, q: (q, j))
    return _mm(name, a, b, kind, (m // tm, n // tn, k // tk), a_spec, b_spec,
               [jax.ShapeDtypeStruct((m, n), out_dtype)], [pl.BlockSpec((tm, tn), lambda i, j, q: (i, j))],
               (tm, tn), _store(out_dtype))[0]


def _row(ts, d):
    return pl.BlockSpec((ts, d), lambda i: (i, 0))


def _vec(d):
    return pl.BlockSpec((1, d), lambda i: (0, 0))


def _norm_mod(name, x, gain, sc, sh, comm=None):
    s, d = x.shape
    ts = _pick(s, ROW_TILE, 16)

    def body(x_ref, g_ref, sc_ref, sh_ref, h_ref):
        xv = x_ref[...]
        r = lax.rsqrt(jnp.mean(xv * xv, axis=-1, keepdims=True) + EPS)
        h_ref[...] = ((xv * r) * g_ref[...] * (1.0 + sc_ref[...]) + sh_ref[...]).astype(BF16)

    return _pcall(body, comm=comm, name=name, grid=(s // ts,), in_specs=[_row(ts, d), _vec(d), _vec(d), _vec(d)],
                  out_specs=[_row(ts, d)], out_shape=[jax.ShapeDtypeStruct((s, d), BF16)],
                  compiler_params=_cparams(("parallel",)))(x, gain, sc, sh)


def _loss_head(x3, tgt, y2, gf, g2):
    s, d = x3.shape
    ts = _pick(s, ROW_TILE, 16)

    def body(x_ref, t_ref, y_ref, gf_ref, g2_ref, dx_ref, dy_ref, loss_ref, dgf_ref, dg2_ref):
        @pl.when(pl.program_id(0) == 0)
        def _():
            loss_ref[...] = jnp.zeros_like(loss_ref)
            dgf_ref[...] = jnp.zeros_like(dgf_ref)
            dg2_ref[...] = jnp.zeros_like(dg2_ref)

        xv = x_ref[...]
        r = lax.rsqrt(jnp.mean(xv * xv, axis=-1, keepdims=True) + EPS)
        xn = xv * r
        err = xn * gf_ref[...] - t_ref[...]
        loss_ref[...] += 0.5 * jnp.sum(jnp.mean(err * err, axis=-1, keepdims=True), axis=0, keepdims=True)
        dout = err * (1.0 / d)
        dgf_ref[...] += jnp.sum(dout * xn, axis=0, keepdims=True)
        dxn = dout * gf_ref[...]
        dx = r * (dxn - xn * jnp.mean(dxn * xn, axis=-1, keepdims=True))
        dx_ref[...] = dx
        dy_ref[...] = (dx * g2_ref[...]).astype(BF16)
        dg2_ref[...] += jnp.sum(dx * y_ref[...], axis=0, keepdims=True)

    one = pl.BlockSpec((1, 1), lambda i: (0, 0))
    return _pcall(
        body, name="loss_head", grid=(s // ts,),
        in_specs=[_row(ts, d), _row(ts, d), _row(ts, d), _vec(d), _vec(d)],
        out_specs=[_row(ts, d), _row(ts, d), one, _vec(d), _vec(d)],
        out_shape=[jax.ShapeDtypeStruct((s, d), F32), jax.ShapeDtypeStruct((s, d), BF16),
                   jax.ShapeDtypeStruct((1, 1), F32), jax.ShapeDtypeStruct((1, d), F32),
                   jax.ShapeDtypeStruct((1, d), F32)],
        compiler_params=_cparams(("arbitrary",)))(x3, tgt, y2, gf, g2)


def _norm_mod_bwd(name, x, dh, dres, gain, sc, y_prev=None, gate=None, comm=None):
    s, d = x.shape
    ts = _pick(s, ROW_TILE, 16)
    gated = y_prev is not None

    def body(*refs):
        if gated:
            x_ref, dh_ref, dr_ref, g_ref, sc_ref, y_ref, gt_ref, dx_ref, dy_ref, dsc_ref, dsh_ref, dg_ref, dgt_ref = refs
        else:
            x_ref, dh_ref, dr_ref, g_ref, sc_ref, dx_ref, dsc_ref, dsh_ref, dg_ref = refs

        @pl.when(pl.program_id(0) == 0)
        def _():
            dsc_ref[...] = jnp.zeros_like(dsc_ref)
            dsh_ref[...] = jnp.zeros_like(dsh_ref)
            dg_ref[...] = jnp.zeros_like(dg_ref)
            if gated:
                dgt_ref[...] = jnp.zeros_like(dgt_ref)

        xv, dhv = x_ref[...], dh_ref[...]
        r = lax.rsqrt(jnp.mean(xv * xv, axis=-1, keepdims=True) + EPS)
        xn = xv * r
        dsc_ref[...] += jnp.sum(dhv * (xn * g_ref[...]), axis=0, keepdims=True)
        dsh_ref[...] += jnp.sum(dhv, axis=0, keepdims=True)
        da = dhv * (1.0 + sc_ref[...])
        dg_ref[...] += jnp.sum(da * xn, axis=0, keepdims=True)
        dxn = da * g_ref[...]
        dx = dr_ref[...] + r * (dxn - xn * jnp.mean(dxn * xn, axis=-1, keepdims=True))
        dx_ref[...] = dx
        if gated:
            dy_ref[...] = (dx * gt_ref[...]).astype(BF16)
            dgt_ref[...] += jnp.sum(dx * y_ref[...], axis=0, keepdims=True)

    ins = [x, dh, dres, gain, sc] + ([y_prev, gate] if gated else [])
    in_specs = [_row(ts, d)] * 3 + [_vec(d)] * 2 + ([_row(ts, d), _vec(d)] if gated else [])
    vec_out = jax.ShapeDtypeStruct((1, d), F32)
    out_shape = [jax.ShapeDtypeStruct((s, d), F32)] + ([jax.ShapeDtypeStruct((s, d), BF16)] if gated else [])
    out_shape += [vec_out] * (4 if gated else 3)
    out_specs = [_row(ts, d)] * (2 if gated else 1) + [_vec(d)] * (4 if gated else 3)
    return _pcall(body, comm=comm, name=name, grid=(s // ts,), in_specs=in_specs, out_specs=out_specs,
                  out_shape=out_shape, compiler_params=_cparams(("arbitrary",)))(*ins)


def _dot3(a, b, dims):
    a1, a2, _ = _split3(a)
    b1, b2, _ = _split3(b)
    dot = functools.partial(lax.dot_general, dimension_numbers=dims, preferred_element_type=F32)
    return dot(a1, b1) + (dot(a1, b2) + dot(a2, b1))


def _mod_fwd(c_all, w, b_cols, comm=None):
    nb, d = c_all.shape
    n = w.shape[1]
    tk = _pick(d, 512, 128)
    nk = d // tk

    def body(c_ref, w_ref, b_ref, act_ref, out_ref):
        k = pl.program_id(0)
        cv = c_ref[...]
        act = cv * (1.0 / (1.0 + jnp.exp(-cv)))
        act_ref[...] = act

        @pl.when(k == 0)
        def _():
            out_ref[...] = jnp.broadcast_to(b_ref[...], out_ref.shape)

        out_ref[...] += _dot3(act, w_ref[...], _DIMS["nn"])

    return _pcall(
        body, comm=comm, name="mod_fwd", grid=(nk,),
        in_specs=[pl.BlockSpec((nb, tk), lambda k: (0, k)), pl.BlockSpec((tk, n), lambda k: (k, 0)),
                  pl.BlockSpec((1, n), lambda k: (0, 0))],
        out_specs=[pl.BlockSpec((nb, tk), lambda k: (0, k)), pl.BlockSpec((nb, n), lambda k: (0, 0))],
        out_shape=[jax.ShapeDtypeStruct((nb, d), F32), jax.ShapeDtypeStruct((nb, n), F32)],
        compiler_params=_cparams(("arbitrary",)))(c_all, w, b_cols)


def _mod_wgrad(act_all, dmod_cols):
    nb, d = act_all.shape
    n = dmod_cols.shape[1]
    tm = _pick(d, 512, 128)

    def body(a_ref, d_ref, o_ref):
        o_ref[...] = _dot3(a_ref[...], d_ref[...], _DIMS["tn"])

    return _pcall(
        body, name="mod_wgrad", grid=(d // tm,),
        in_specs=[pl.BlockSpec((nb, tm), lambda i: (0, i)), pl.BlockSpec((nb, n), lambda i: (0, 0))],
        out_specs=pl.BlockSpec((tm, n), lambda i: (i, 0)), out_shape=jax.ShapeDtypeStruct((d, n), F32),
        compiler_params=_cparams(("parallel",)))(act_all, dmod_cols)


def _bias_expand(rel_t, onehot_t):
    h, _ = rel_t.shape
    n = onehot_t.shape[1]

    def body(r_ref, o_ref, out_ref):
        a, b, c = _split3(r_ref[...])
        dot = functools.partial(lax.dot_general, dimension_numbers=_DIMS["nn"], preferred_element_type=F32)
        oh = o_ref[...]
        out_ref[...] = dot(a, oh) + (dot(b, oh) + dot(c, oh))

    full = lambda shp: pl.BlockSpec(shp, lambda: (0,) * len(shp))
    return _pcall(body, name="bias_expand", in_specs=[full(rel_t.shape), full(onehot_t.shape)],
                  out_specs=full((h, n)), out_shape=jax.ShapeDtypeStruct((h, n), F32),
                  compiler_params=pltpu.CompilerParams(vmem_limit_bytes=VMEM_LIMIT))(rel_t, onehot_t)


def _bias_reduce(dbias, onehot, dsink_rows):
    h, n = dbias.shape

    def body(d_ref, o_ref, s_ref, out_ref, so_ref):
        a, b, c = _split3(d_ref[...])
        dot = functools.partial(lax.dot_general, dimension_numbers=_DIMS["nn"], preferred_element_type=F32)
        oh = o_ref[...]
        out_ref[...] = dot(a, oh) + (dot(b, oh) + dot(c, oh))
        so_ref[...] = jnp.sum(s_ref[...], axis=-1, keepdims=True)

    full = lambda shp: pl.BlockSpec(shp, lambda: (0,) * len(shp))
    return _pcall(body, name="bias_reduce", in_specs=[full(dbias.shape), full(onehot.shape), full(dsink_rows.shape)],
                  out_specs=[full((h, REL_BUCKETS)), full((h, 1))],
                  out_shape=[jax.ShapeDtypeStruct((h, REL_BUCKETS), F32), jax.ShapeDtypeStruct((h, 1), F32)],
                  compiler_params=pltpu.CompilerParams(vmem_limit_bytes=VMEM_LIMIT))(dbias, onehot, dsink_rows)


PAIRS = SWA_GROUP // 2
PROWS = PAIRS * BLOCK
PCOLS = 2 * 2 * BLOCK


def _swa2_specs():
    tok = lambda width: pl.BlockSpec((BLOCK, width), lambda g, n: (n, g))
    prev = pl.BlockSpec((None, BLOCK, 2 * SWA_DH), lambda g, n: (g, jnp.maximum(n - 1, 0), 0))
    cur = pl.BlockSpec((None, BLOCK, 2 * SWA_DH), lambda g, n: (g, n, 0))
    bias_spec = pl.BlockSpec((None, PROWS, PCOLS), lambda g, n: (g, 0, 0))
    col_spec = pl.BlockSpec((None, PROWS, 1), lambda g, n: (g, 0, 0))
    lse_spec = pl.BlockSpec((None, None, PROWS, 1), lambda g, n: (g, n, 0, 0))
    return tok, prev, cur, bias_spec, col_spec, lse_spec


def _stack_pairs(blk):
    return jnp.concatenate([blk[:, p * 2 * SWA_DH:(p + 1) * 2 * SWA_DH] for p in range(PAIRS)], axis=0)


def _band(tp, tc, bp, bc):
    return jnp.concatenate([tp[...], tc[...], bp[...], bc[...]], axis=0)


def _swa2_scores(q_ref, kd, bias_ref, n):
    q2 = _stack_pairs(q_ref[...])
    s2 = lax.dot_general(q2, kd, _DIMS["nt"], preferred_element_type=F32) * SWA_SCALE + bias_ref[...]
    col = lax.broadcasted_iota(jnp.int32, s2.shape, 1)
    before_start = jnp.logical_and(n == 0, (col & (2 * BLOCK - 1)) < BLOCK)
    return q2, jnp.where(before_start, -jnp.inf, s2)


def _swa2_fwd(src, ktop, kbot, vtop, vbot, bias, sinks, comm=None):
    s = src.shape[0]
    nb = s // BLOCK
    tok, prev, cur, bias_spec, col_spec, lse_spec = _swa2_specs()

    def body(q_ref, ktp, ktc, kbp, kbc, vtp, vtc, vbp, vbc, bias_ref, sa_ref, sb_ref, o_ref, la_ref, lb_ref):
        n = pl.program_id(1)
        _, s2 = _swa2_scores(q_ref, _band(ktp, ktc, kbp, kbc), bias_ref, n)
        row = lax.broadcasted_iota(jnp.int32, (PCOLS, 2 * SWA_DH), 0)
        lane = lax.broadcasted_iota(jnp.int32, (PCOLS, 2 * SWA_DH), 1)
        ones = jnp.where(lane == row // (2 * BLOCK), 1.0, 0.0).astype(BF16)
        ps, ms, sinks_ = [], [], []
        for half, sink_ref in enumerate((sa_ref, sb_ref)):
            sc = s2[:, half * 2 * BLOCK:(half + 1) * 2 * BLOCK]
            m = jnp.maximum(jnp.max(sc, axis=-1, keepdims=True), sink_ref[...])
            ps.append(jnp.exp(sc - m).astype(BF16))
            ms.append(m)
        acc = lax.dot_general(jnp.concatenate(ps, axis=1), jnp.concatenate([_band(vtp, vtc, vbp, vbc), ones], axis=1),
                              _DIMS["nn"], preferred_element_type=F32)
        dens = []
        for half, (sink_ref, lse_ref) in enumerate(((sa_ref, la_ref), (sb_ref, lb_ref))):
            den = acc[:, 2 * SWA_DH + half:2 * SWA_DH + half + 1] + jnp.exp(sink_ref[...] - ms[half])
            lse_ref[...] = ms[half] + jnp.log(den)
            dens.append(den)
        lo = lax.broadcasted_iota(jnp.int32, (PROWS, 2 * SWA_DH), 1) < SWA_DH
        o2 = acc[:, 0:2 * SWA_DH] / jnp.where(lo, dens[0], dens[1])
        for p in range(PAIRS):
            o_ref[:, p * 2 * SWA_DH:(p + 1) * 2 * SWA_DH] = o2[p * BLOCK:(p + 1) * BLOCK].astype(BF16)

    lse_shape = jax.ShapeDtypeStruct((SWA_KV, nb, PROWS, 1), F32)
    return _pcall(
        body, comm=comm, name="swa_fwd", grid=(SWA_KV, nb),
        in_specs=[tok(PROWS), prev, cur, prev, cur, prev, cur, prev, cur, bias_spec, col_spec, col_spec],
        out_specs=[tok(PROWS), lse_spec, lse_spec],
        out_shape=[jax.ShapeDtypeStruct((s, MIX), BF16), lse_shape, lse_shape],
        compiler_params=_cparams(("parallel", "parallel")))(
            src, ktop, ktop, kbot, kbot, vtop, vtop, vbot, vbot, bias, sinks[0], sinks[1])


def _swa2_bwd(src, dsrc, ktop, kbot, vtop, vbot, lses, bias, sinks, comm=None):
    s = src.shape[0]
    nb = s // BLOCK
    tok, prev, cur, bias_spec, col_spec, lse_spec = _swa2_specs()
    lane_lo = lambda shape: lax.broadcasted_iota(jnp.int32, shape, 1) < SWA_DH

    def body(q_ref, do_ref, ktp, ktc, kbp, kbc, vtp, vtc, vbp, vbc, la_ref, lb_ref, bias_ref, sa_ref, sb_ref,
             dq_ref, dkp_ref, dkc_ref, dvp_ref, dvc_ref, dbias_ref, dsa_ref, dsb_ref):
        n = pl.program_id(1)

        @pl.when(n == 0)
        def _():
            dbias_ref[...] = jnp.zeros_like(dbias_ref)
            dsa_ref[...] = jnp.zeros_like(dsa_ref)
            dsb_ref[...] = jnp.zeros_like(dsb_ref)

        kd = _band(ktp, ktc, kbp, kbc)
        q2, s2 = _swa2_scores(q_ref, kd, bias_ref, n)
        do2 = _stack_pairs(do_ref[...]).astype(BF16)
        dp2 = lax.dot_general(do2, _band(vtp, vtc, vbp, vbc), _DIMS["nt"], preferred_element_type=F32)
        ps, dss = [], []
        for half, (sink_ref, lse_ref, dsink_ref) in enumerate(((sa_ref, la_ref, dsa_ref), (sb_ref, lb_ref, dsb_ref))):
            cols = slice(half * 2 * BLOCK, (half + 1) * 2 * BLOCK)
            lse_v = lse_ref[...]
            p = jnp.exp(s2[:, cols] - lse_v)
            dp = dp2[:, cols]
            delta = jnp.sum(p * dp, axis=-1, keepdims=True)
            ds = p * (dp - delta)
            dsink_ref[...] += -jnp.exp(sink_ref[...] - lse_v) * delta
            ps.append(p.astype(BF16))
            dss.append(ds)
        ds2 = jnp.concatenate(dss, axis=1)
        dbias_ref[...] += ds2
        dsb2 = (ds2 * SWA_SCALE).astype(BF16)
        dq2 = lax.dot_general(dsb2, kd, _DIMS["nn"], preferred_element_type=F32)
        for p in range(PAIRS):
            dq_ref[:, p * 2 * SWA_DH:(p + 1) * 2 * SWA_DH] = dq2[p * BLOCK:(p + 1) * BLOCK].astype(BF16)
        dk = lax.dot_general(dsb2, q2, _DIMS["tn"], preferred_element_type=F32)
        dv = lax.dot_general(jnp.concatenate(ps, axis=1), do2, _DIMS["tn"], preferred_element_type=F32)
        for full, prev_ref, cur_ref in ((dk, dkp_ref, dkc_ref), (dv, dvp_ref, dvc_ref)):
            own = jnp.where(lane_lo((2 * BLOCK, 2 * SWA_DH)), full[:2 * BLOCK], full[2 * BLOCK:])
            prev_ref[...] = own[:BLOCK]
            cur_ref[...] = own[BLOCK:]

    kv_out = jax.ShapeDtypeStruct((SWA_KV, s, 2 * SWA_DH), F32)
    col_out = jax.ShapeDtypeStruct((SWA_KV, PROWS, 1), F32)
    return _pcall(
        body, comm=comm, name="swa_bwd", grid=(SWA_KV, nb),
        in_specs=[tok(PROWS), tok(PROWS), prev, cur, prev, cur, prev, cur, prev, cur, lse_spec, lse_spec, bias_spec,
                  col_spec, col_spec],
        out_specs=[tok(PROWS), cur, cur, cur, cur, bias_spec, col_spec, col_spec],
        out_shape=[jax.ShapeDtypeStruct((s, IN_EXT), BF16), kv_out, kv_out, kv_out, kv_out,
                   jax.ShapeDtypeStruct(bias.shape, F32), col_out, col_out],
        compiler_params=_cparams(("arbitrary", "arbitrary")))(
            src, dsrc, ktop, ktop, kbot, kbot, vtop, vtop, vbot, vbot, lses[0], lses[1], bias, sinks[0], sinks[1])


def _rope_slab(slab, table):
    t = slab * table
    return t + pltpu.roll(t, ROPE, 1)


def _low_lanes(v):
    lane = lax.broadcasted_iota(jnp.int32, v.shape, 1)
    return jnp.where(lane < ROPE, v, 0.0)


def _rms(xv, g):
    r = lax.rsqrt(jnp.mean(xv * xv, axis=-1, keepdims=True) + EPS)
    return xv * r, r


def _mla_prep(proj, gq, gkv, table):
    s = proj.shape[0]
    ts = _pick(s, ROW_TILE, 16)

    def body(p_ref, gq_ref, gkv_ref, t_ref, cq_ref, ckv_ref, kr_ref):
        xq, _ = _rms(p_ref[:, 0:Q_RANK], None)
        cq_ref[...] = (xq * gq_ref[...]).astype(BF16)
        xkv, _ = _rms(p_ref[:, Q_RANK:Q_RANK + KV_RANK], None)
        ckv_ref[...] = (xkv * gkv_ref[...]).astype(BF16)
        kr_ref[...] = _low_lanes(_rope_slab(p_ref[:, Q_RANK + KV_RANK:TAIL], t_ref[...]))

    return _pcall(
        body, name="mla_prep", grid=(s // ts,),
        in_specs=[pl.BlockSpec((ts, TAIL), lambda i: (i, TAIL0 // TAIL)), _vec(Q_RANK), _vec(KV_RANK), _row(ts, 2 * ROPE)],
        out_specs=[_row(ts, Q_RANK), _row(ts, KV_RANK), _row(ts, 2 * ROPE)],
        out_shape=[jax.ShapeDtypeStruct((s, Q_RANK), BF16), jax.ShapeDtypeStruct((s, KV_RANK), BF16),
                   jax.ShapeDtypeStruct((s, 2 * ROPE), F32)],
        compiler_params=_cparams(("parallel",)))(proj, gq, gkv, table)


def _mla_prep_bwd(proj, dcq, dckv, dkr, gq, gkv, table, dproj):
    s = proj.shape[0]
    ts = _pick(s, ROW_TILE, 16)

    def norm_bwd(xv, dy, g):
        xn, r = _rms(xv, None)
        dg = jnp.sum(dy * xn, axis=0, keepdims=True)
        dxn = dy * g
        return r * (dxn - xn * jnp.mean(dxn * xn, axis=-1, keepdims=True)), dg

    def body(p_ref, dcq_ref, dckv_ref, dkr_ref, gq_ref, gkv_ref, t_ref, _, dt_ref, dgq_ref, dgkv_ref):
        @pl.when(pl.program_id(0) == 0)
        def _():
            dgq_ref[...] = jnp.zeros_like(dgq_ref)
            dgkv_ref[...] = jnp.zeros_like(dgkv_ref)

        dxq, dgq = norm_bwd(p_ref[:, 0:Q_RANK], dcq_ref[...], gq_ref[...])
        dxkv, dgkv = norm_bwd(p_ref[:, Q_RANK:Q_RANK + KV_RANK], dckv_ref[...], gkv_ref[...])
        dgq_ref[...] += dgq
        dgkv_ref[...] += dgkv
        d = _low_lanes(dkr_ref[...])
        dslab = (d + pltpu.roll(d, ROPE, 1)) * t_ref[...]
        dt_ref[:, 0:Q_RANK] = dxq.astype(BF16)
        dt_ref[:, Q_RANK:Q_RANK + KV_RANK] = dxkv.astype(BF16)
        dt_ref[:, Q_RANK + KV_RANK:TAIL] = dslab.astype(BF16)

    return _pcall(
        body, name="mla_prep_bwd", grid=(s // ts,),
        in_specs=[pl.BlockSpec((ts, TAIL), lambda i: (i, TAIL0 // TAIL)), _row(ts, Q_RANK), _row(ts, KV_RANK),
                  _row(ts, 2 * ROPE), _vec(Q_RANK), _vec(KV_RANK), _row(ts, 2 * ROPE), pl.BlockSpec(memory_space=pl.ANY)],
        out_specs=[pl.BlockSpec((ts, TAIL), lambda i: (i, TAIL0 // TAIL)), _vec(Q_RANK), _vec(KV_RANK)],
        out_shape=[jax.ShapeDtypeStruct(dproj.shape, BF16), jax.ShapeDtypeStruct((1, Q_RANK), F32),
                   jax.ShapeDtypeStruct((1, KV_RANK), F32)],
        input_output_aliases={7: 0},
        compiler_params=_cparams(("arbitrary",)))(proj, dcq, dckv, dkr, gq, gkv, table, dproj)


def _head_specs(ts):
    tok = lambda w: pl.BlockSpec((ts, w), lambda h, i: (i, 0))
    head = lambda w: pl.BlockSpec((None, ts, w), lambda h, i: (h, i, 0))
    wgt = lambda r, c: pl.BlockSpec((None, r, c), lambda h, i: (h, 0, 0))
    return tok, head, wgt


def _mla_qkv(cq, ckv, kr, wq, wkv, table):
    s = cq.shape[0]
    ts = _pick(s, 4 * ROW_TILE, 16)
    tok, head, wgt = _head_specs(ts)

    def body(cq_ref, ckv_ref, kr_ref, wq_ref, wkv_ref, t_ref, q_ref, k_ref, v_ref):
        qf = lax.dot_general(cq_ref[...], wq_ref[...], _DIMS["nn"], preferred_element_type=F32)
        q_ref[:, 0:NOPE] = qf[:, 0:NOPE].astype(BF16)
        q_ref[:, NOPE:QW] = _rope_slab(qf[:, NOPE:QW], t_ref[...]).astype(BF16)
        kv = lax.dot_general(ckv_ref[...], wkv_ref[...], _DIMS["nn"], preferred_element_type=F32)
        k_ref[:, 0:NOPE] = kv[:, 0:NOPE].astype(BF16)
        k_ref[:, NOPE:QW] = kr_ref[...].astype(BF16)
        v_ref[:, 0:VDIM] = kv[:, NOPE:NOPE + VDIM].astype(BF16)
        lane = lax.broadcasted_iota(jnp.int32, (ts, VDIM), 1)
        v_ref[:, VDIM:2 * VDIM] = jnp.where(lane == 0, 1.0, 0.0).astype(BF16)

    return _pcall(
        body, name="mla_qkv", grid=(MLA_H, s // ts),
        in_specs=[tok(Q_RANK), tok(KV_RANK), tok(2 * ROPE), wgt(Q_RANK, QW), wgt(KV_RANK, NOPE + VDIM), tok(2 * ROPE)],
        out_specs=[head(QW), head(QW), head(2 * VDIM)],
        out_shape=[jax.ShapeDtypeStruct((MLA_H, s, QW), BF16), jax.ShapeDtypeStruct((MLA_H, s, QW), BF16),
                   jax.ShapeDtypeStruct((MLA_H, s, 2 * VDIM), BF16)],
        compiler_params=_cparams(("parallel", "parallel")))(cq, ckv, kr, wq, wkv, table)


def _mla_qkv_bwd(dq, dk, dv, cq, ckv, wq, wkv, table, comm=None):
    s = cq.shape[0]
    ts = _pick(s, 4 * ROW_TILE, 16)
    tok, head, wgt = _head_specs(ts)
    whole = lambda w: pl.BlockSpec((s, w), lambda h, i: (0, 0))

    def body(dq_ref, dk_ref, dv_ref, cq_ref, ckv_ref, wq_ref, wkv_ref, t_ref,
             dcq_ref, dckv_ref, dkr_ref, gwq_ref, gwkv_ref):
        h, i = pl.program_id(0), pl.program_id(1)
        rows = pl.ds(pl.multiple_of(i * ts, ts), ts)
        d = dq_ref[:, NOPE:QW]
        dslab = (d + pltpu.roll(d, ROPE, 1)) * t_ref[...]
        dqe = jnp.concatenate([dq_ref[:, 0:NOPE], dslab], axis=1).astype(BF16)
        dkv = jnp.concatenate([dk_ref[:, 0:NOPE], dv_ref[...]], axis=1).astype(BF16)
        dcq = lax.dot_general(dqe, wq_ref[...], _DIMS["nt"], preferred_element_type=F32)
        dckv = lax.dot_general(dkv, wkv_ref[...], _DIMS["nt"], preferred_element_type=F32)
        gwq = lax.dot_general(cq_ref[...], dqe, _DIMS["tn"], preferred_element_type=F32)
        gwkv = lax.dot_general(ckv_ref[...], dkv, _DIMS["tn"], preferred_element_type=F32)
        dkr = dk_ref[:, NOPE:QW].astype(F32)

        @pl.when(h == 0)
        def _():
            dcq_ref[rows, :] = dcq
            dckv_ref[rows, :] = dckv
            dkr_ref[rows, :] = dkr

        @pl.when(h > 0)
        def _():
            dcq_ref[rows, :] += dcq
            dckv_ref[rows, :] += dckv
            dkr_ref[rows, :] += dkr

        @pl.when(i == 0)
        def _():
            gwq_ref[...] = gwq
            gwkv_ref[...] = gwkv

        @pl.when(i > 0)
        def _():
            gwq_ref[...] += gwq
            gwkv_ref[...] += gwkv

    return _pcall(
        body, comm=comm, name="mla_qkv_bwd", grid=(MLA_H, s // ts),
        in_specs=[head(QW), head(QW), head(VDIM), tok(Q_RANK), tok(KV_RANK), wgt(Q_RANK, QW),
                  wgt(KV_RANK, NOPE + VDIM), tok(2 * ROPE)],
        out_specs=[whole(Q_RANK), whole(KV_RANK), whole(2 * ROPE), wgt(Q_RANK, QW), wgt(KV_RANK, NOPE + VDIM)],
        out_shape=[jax.ShapeDtypeStruct((s, Q_RANK), F32), jax.ShapeDtypeStruct((s, KV_RANK), F32),
                   jax.ShapeDtypeStruct((s, 2 * ROPE), F32), jax.ShapeDtypeStruct((MLA_H, Q_RANK, QW), F32),
                   jax.ShapeDtypeStruct((MLA_H, KV_RANK, NOPE + VDIM), F32)],
        compiler_params=_cparams(("arbitrary", "arbitrary")))(dq, dk, dv, cq, ckv, wq, wkv, table)


def _diag_mask(t):
    return lax.broadcasted_iota(jnp.int32, (t, t), 1) <= lax.broadcasted_iota(jnp.int32, (t, t), 0)


def _mla_fwd(q, k, v, mix, comm=None):
    s = q.shape[1]
    t = _pick(s, ATT_T, 128)
    nt = s // t
    assert nt % 2 == 0
    hb = MLA_H

    def fold(p, u):
        first = u <= p
        return jnp.where(first, p, nt - 1 - p), jnp.where(first, u, u - p - 1)

    to_log2 = MLA_SCALE * math.log2(math.e)

    def body(q_ref, k_ref, v_ref, _, o_ref, oh_ref, lse_ref, m_ref, acc_ref):
        i, j = fold(pl.program_id(1), pl.program_id(2))

        @pl.when(j == 0)
        def _():
            m_ref[...] = jnp.full_like(m_ref, -jnp.inf)
            acc_ref[...] = jnp.zeros_like(acc_ref)

        def step(diagonal):
            for h in range(hb):
                sc = lax.dot_general(q_ref[h], k_ref[h], _DIMS["nt"], preferred_element_type=F32)
                if diagonal:
                    sc = jnp.where(_diag_mask(t), sc, -jnp.inf)
                m_old = m_ref[h]
                m_new = jnp.maximum(m_old, jnp.max(sc, axis=-1, keepdims=True))
                alpha = jnp.exp2((m_old - m_new) * to_log2)
                p = jnp.exp2((sc - m_new) * to_log2)
                acc_ref[h] = alpha * acc_ref[h] + lax.dot_general(p.astype(BF16), v_ref[h], _DIMS["nn"],
                                                                  preferred_element_type=F32)
                m_ref[h] = m_new

        @pl.when(j < i)
        def _():
            step(False)

        @pl.when(j == i)
        def _():
            step(True)
            for h in range(hb):
                den = acc_ref[h, :, VDIM:VDIM + 1]
                o = acc_ref[h, :, 0:VDIM] / den
                o_ref[:, h * VDIM:(h + 1) * VDIM] = o
                oh_ref[:, h * VDIM:(h + 1) * VDIM] = o.astype(BF16)
                lse_ref[h] = m_ref[h] * MLA_SCALE + jnp.log(den)

    o_spec = pl.BlockSpec((t, hb * VDIM), lambda h, p, u: (fold(p, u)[0], h))
    first = (MIX - MLA_H * VDIM) // (hb * VDIM)
    mix_spec = pl.BlockSpec((t, hb * VDIM), lambda h, p, u: (fold(p, u)[0], first + h))
    return _pcall(
        body, comm=comm, name="mla_fwd", grid=(MLA_H // hb, nt // 2, nt + 1),
        in_specs=[pl.BlockSpec((hb, t, QW), lambda h, p, u: (h, fold(p, u)[0], 0)),
                  pl.BlockSpec((hb, t, QW), lambda h, p, u: (h, fold(p, u)[1], 0)),
                  pl.BlockSpec((hb, t, 2 * VDIM), lambda h, p, u: (h, fold(p, u)[1], 0)),
                  pl.BlockSpec(memory_space=pl.ANY)],
        out_specs=[o_spec, mix_spec, pl.BlockSpec((hb, t, 1), lambda h, p, u: (h, fold(p, u)[0], 0))],
        out_shape=[jax.ShapeDtypeStruct((s, MLA_H * VDIM), F32), jax.ShapeDtypeStruct(mix.shape, BF16),
                   jax.ShapeDtypeStruct((MLA_H, s, 1), F32)],
        input_output_aliases={3: 1},
        scratch_shapes=[pltpu.VMEM((hb, t, 1), F32), pltpu.VMEM((hb, t, 2 * VDIM), F32)],
        compiler_params=_cparams(("parallel", "parallel", "arbitrary")))(q, k, v, mix)


def _mla_delta(dmix, o):
    s = o.shape[0]
    ts = _pick(s, 2 * ROW_TILE, 16)
    w = MLA_H * VDIM

    def body(d_ref, o_ref, out_ref):
        prod = d_ref[...] * o_ref[...]
        for h in range(MLA_H):
            out_ref[h] = jnp.sum(prod[:, h * VDIM:(h + 1) * VDIM], axis=-1, keepdims=True)

    return _pcall(body, name="mla_delta", grid=(s // ts,),
                  in_specs=[pl.BlockSpec((ts, w), lambda i: (i, SWA_HEADS * SWA_DH // w)), pl.BlockSpec((ts, w), lambda i: (i, 0))],
                  out_specs=pl.BlockSpec((MLA_H, ts, 1), lambda i: (0, i, 0)),
                  out_shape=jax.ShapeDtypeStruct((MLA_H, s, 1), F32), compiler_params=_cparams(("parallel",)))(dmix, o)


def _mla_bwd(q, k, v, dmix, delta, lse, comm=None):
    s = q.shape[1]
    t = _pick(s, ATT_T, 128)
    nt = s // t
    assert nt % 2 == 0
    hb = 2 * MLA_HB
    o_blk0 = SWA_HEADS * SWA_DH // (hb * VDIM)

    def fold(p, u):
        first = u < nt - p
        return jnp.where(first, p, nt - 1 - p), jnp.where(first, p + u, u - 1)

    log2e = math.log2(math.e)

    def body(q_ref, k_ref, v_ref, do_ref, delta_ref, lse_ref, dq_ref, dk_ref, dv_ref, dk_acc, dv_acc):
        j, i = fold(pl.program_id(1), pl.program_id(2))
        rows = pl.ds(pl.multiple_of(i * t, t), t)

        @pl.when(i == j)
        def _():
            dk_acc[...] = jnp.zeros_like(dk_acc)
            dv_acc[...] = jnp.zeros_like(dv_acc)

        def step(diagonal):
            for h in range(hb):
                qv, kv_ = q_ref[h], k_ref[h]
                dob = do_ref[:, h * VDIM:(h + 1) * VDIM].astype(BF16)
                st = lax.dot_general(kv_, qv, _DIMS["nt"], preferred_element_type=F32)
                pt = jnp.exp2(st * (MLA_SCALE * log2e) - lse_ref[h] * log2e)
                if diagonal:
                    keep = lax.broadcasted_iota(jnp.int32, (t, t), 0) <= lax.broadcasted_iota(jnp.int32, (t, t), 1)
                    pt = jnp.where(keep, pt, 0.0)
                dpt = lax.dot_general(v_ref[h], dob, _DIMS["nt"], preferred_element_type=F32)
                dst = (pt * (dpt - delta_ref[h]) * MLA_SCALE).astype(BF16)
                dv_acc[h] += lax.dot_general(pt.astype(BF16), dob, _DIMS["nn"], preferred_element_type=F32)
                dk_acc[h] += lax.dot_general(dst, qv, _DIMS["nn"], preferred_element_type=F32)
                dqv = lax.dot_general(dst, kv_, _DIMS["tn"], preferred_element_type=F32)

                @pl.when(j == 0)
                def _():
                    dq_ref[h, rows, :] = dqv

                @pl.when(j > 0)
                def _():
                    dq_ref[h, rows, :] += dqv

        @pl.when(i > j)
        def _():
            step(False)

        @pl.when(i == j)
        def _():
            step(True)

        @pl.when(i == nt - 1)
        def _():
            dk_ref[...] = dk_acc[...].astype(BF16)
            dv_ref[...] = dv_acc[...].astype(BF16)

    qi = lambda h, p, u: (h, fold(p, u)[1], 0)
    kj = lambda h, p, u: (h, fold(p, u)[0], 0)
    row = pl.BlockSpec((hb, 1, t), lambda h, p, u: (h, 0, fold(p, u)[1]))
    return _pcall(
        body, comm=comm, name="mla_bwd", grid=(MLA_H // hb, nt // 2, nt + 1),
        in_specs=[pl.BlockSpec((hb, t, QW), qi), pl.BlockSpec((hb, t, QW), kj), pl.BlockSpec((hb, t, VDIM), kj),
                  pl.BlockSpec((t, hb * VDIM), lambda h, p, u: (fold(p, u)[1], o_blk0 + h)), row, row],
        out_specs=[pl.BlockSpec((hb, s, QW), lambda h, p, u: (h, 0, 0)), pl.BlockSpec((hb, t, QW), kj),
                   pl.BlockSpec((hb, t, VDIM), kj)],
        out_shape=[jax.ShapeDtypeStruct((MLA_H, s, QW), F32), jax.ShapeDtypeStruct((MLA_H, s, QW), BF16),
                   jax.ShapeDtypeStruct((MLA_H, s, VDIM), BF16)],
        scratch_shapes=[pltpu.VMEM((hb, t, QW), F32), pltpu.VMEM((hb, t, VDIM), F32)],
        compiler_params=_cparams(("arbitrary", "arbitrary", "arbitrary")))(q, k, v, dmix, delta, lse)


def _adamw(name, w, g, m, v, parts):
    r, c = w.shape
    n_parts = g.shape[0] if parts else 1
    tr = r if r * c <= ADAM_ELEMS else _pick(r, max(8, ADAM_ELEMS // c // 8 * 8), 8)
    c1 = 1.0 - ADAM_B1 ** ADAM_STEP
    c2 = 1.0 - ADAM_B2 ** ADAM_STEP

    def body(w_ref, g_ref, m_ref, v_ref, go_ref, d_ref, mo_ref, vo_ref):
        if parts:
            gv = g_ref[0].astype(F32)
            for j in range(1, n_parts):
                gv = gv + g_ref[j].astype(F32)
        else:
            gv = g_ref[...]
        mv = ADAM_B1 * m_ref[...] + (1.0 - ADAM_B1) * gv
        vv = ADAM_B2 * v_ref[...] + (1.0 - ADAM_B2) * (gv * gv)
        go_ref[...] = gv
        mo_ref[...] = mv
        vo_ref[...] = vv
        d_ref[...] = -ADAM_LR * ((mv / c1) / (jnp.sqrt(vv / c2) + ADAM_EPS) + ADAM_WD * w_ref[...])

    blk = pl.BlockSpec((tr, c), lambda i: (i, 0))
    g_spec = pl.BlockSpec((n_parts, tr, c), lambda i: (0, i, 0)) if parts else blk
    out = jax.ShapeDtypeStruct((r, c), F32)
    return _pcall(body, name=name, grid=(r // tr,), in_specs=[blk, g_spec, blk, blk], out_specs=[blk] * 4,
                  out_shape=[out] * 4, compiler_params=_cparams(("parallel",)))(w, g, m, v)


def _t5_bucket(dist):
    n = jnp.maximum(dist, 0)
    max_exact = REL_BUCKETS // 2
    nf = jnp.maximum(n, 1).astype(F32)
    large = max_exact + (jnp.log(nf / max_exact) / math.log(REL_MAX_DIST / max_exact)
                         * (REL_BUCKETS - max_exact)).astype(jnp.int32)
    return jnp.where(n < max_exact, n, jnp.minimum(large, REL_BUCKETS - 1))


def _swap_halves(w, r0):
    return jnp.concatenate([w[:, r0 + ROPE // 2:r0 + ROPE], w[:, r0:r0 + ROPE // 2]], axis=1)


def _fold_swapped(g, r0, width):
    sw = g[..., width:width + ROPE]
    half = ROPE // 2
    return jnp.concatenate([g[..., :r0], g[..., r0:r0 + half] + sw[..., half:], g[..., r0 + half:r0 + ROPE] + sw[..., :half],
                            g[..., r0 + ROPE:width]], axis=-1)


def kernel(x, c, w_mod, b_mod, attn_norm_g, w_in, swa_sinks, rel_bias, mla_q_norm_g, w_uq, mla_kv_norm_g, w_ukv, w_out, mlp_norm_g, w_ff1, w_ff2, final_norm_g, loss_target, m_w_mod, m_b_mod, m_attn_norm_g, m_w_in, m_swa_sinks, m_rel_bias, m_mla_q_norm_g, m_w_uq, m_mla_kv_norm_g, m_w_ukv, m_w_out, m_mlp_norm_g, m_w_ff1, m_w_ff2, m_final_norm_g, v_w_mod, v_b_mod, v_attn_norm_g, v_w_in, v_swa_sinks, v_rel_bias, v_mla_q_norm_g, v_w_uq, v_mla_kv_norm_g, v_w_ukv, v_w_out, v_mlp_norm_g, v_w_ff1, v_w_ff2, v_final_norm_g):
    s, d = x.shape[1], x.shape[2]
    ffs = w_ff1.shape[2]
    ff = ffs * NDEV
    nmod = w_mod.shape[2]
    me = 4 * lax.axis_index("x") + 2 * lax.axis_index("y") + lax.axis_index("c")
    x2d, tgt = x[0], loss_target[0]
    final_g = final_norm_g.reshape(1, d)

    w_in_l = jnp.concatenate([w_in[0], _swap_halves(w_in[0], OFF_KR)], axis=1).astype(BF16)
    w_uq_l = jnp.concatenate([w_uq[0], _swap_halves(w_uq[0], NOPE)], axis=1).astype(BF16)
    core = jnp.full((1, 128), lax.axis_index("c"), F32)
    (c_all,) = _exchange("gather_c", _Gather([c]))

    b_cols = lax.dynamic_slice(b_mod, (0, me * nmod), (1, nmod))
    act_all, mod_cols = _mod_fwd(c_all.reshape(NDEV, d), w_mod[0], b_cols)
    (mod_g,) = _exchange("gather_mod", _Gather([mod_cols]))
    mod = lax.dynamic_index_in_dim(mod_g, me, axis=1, keepdims=False).reshape(1, 6 * d)
    sh1, sc1, g1, sh2, sc2, g2 = [mod[:, i * d:(i + 1) * d] for i in range(6)]

    pos = jnp.arange(s, dtype=F32)
    inv_freq = ROPE_THETA ** (-jnp.arange(ROPE // 2, dtype=F32) / (ROPE // 2))
    ang = pos[:, None] * inv_freq[None, :]
    cos, sin = jnp.cos(ang), jnp.sin(ang)
    table = jnp.concatenate([cos, cos, -sin, sin], axis=1)
    q_loc = jnp.arange(BLOCK)[:, None]
    k_loc = jnp.arange(2 * BLOCK)[None, :]
    dist = q_loc + BLOCK - k_loc
    in_window = (dist >= 0) & (dist < BLOCK)
    onehot = (_t5_bucket(dist).reshape(-1, 1) == jnp.arange(REL_BUCKETS)[None, :]).astype(BF16)
    bias = _bias_expand(rel_bias.T, onehot.T).reshape(SWA_HEADS, BLOCK, 2 * BLOCK)
    bias = jnp.where(in_window[None], bias, -jnp.inf).reshape(SWA_KV, PAIRS, 2, BLOCK, 2 * BLOCK)
    bias = bias.transpose(0, 1, 3, 2, 4).reshape(SWA_KV, PROWS, PCOLS)
    sinks = jnp.broadcast_to(swa_sinks.reshape(SWA_KV, PAIRS, 1, 2), (SWA_KV, PAIRS, BLOCK, 2)).reshape(SWA_KV, PROWS, 2)
    sinks = (sinks[:, :, 0:1], sinks[:, :, 1:2])

    h1, w_in_g, w_uq_g, w_ukv_g = _norm_mod("norm1", x2d, attn_norm_g, sc1, sh1,
                                            comm=[_Gather([w_in_l, w_uq_l, w_ukv[0].astype(BF16)])])
    w_in_e = w_in_g.reshape(d, IN_EXT)

    def both_dtypes(acc, ex, outs):
        outs[0][...] = acc
        outs[1][...] = acc.astype(BF16)

    tmp = _pick(s, MM_TM // 2, 128)
    proj_blk = pl.BlockSpec((tmp, IN_EXT), lambda i, j, q: (i, 0))
    proj, proj_h = _mm("proj", h1, w_in_e, "nn", (s // tmp, 1, 1), pl.BlockSpec((tmp, d), lambda i, j, q: (i, 0)),
                       pl.BlockSpec((d, IN_EXT), lambda i, j, q: (0, 0)),
                       [jax.ShapeDtypeStruct((s, IN_EXT), F32), jax.ShapeDtypeStruct((s, IN_EXT), BF16)],
                       [proj_blk, proj_blk], (tmp, IN_EXT), both_dtypes)
    def diag_pair(tok):
        x = jnp.stack([tok[:, :SWA_DH], tok[:, SWA_DH:]])
        zero = jnp.zeros_like(x)
        return jnp.concatenate([x, zero], axis=2), jnp.concatenate([zero, x], axis=2)

    k_top, k_bot = diag_pair(proj_h[:, OFF_K:OFF_V])
    v_top, v_bot = diag_pair(proj_h[:, OFF_V:OFF_CQ])
    o_a, lse_a0, lse_a1, w_out_g = _swa2_fwd(proj_h, k_top, k_bot, v_top, v_bot, bias, sinks,
                                             comm=[_Gather([w_out[0].astype(BF16)])])
    w_out_f = w_out_g.reshape(MIX, d)

    cq, ckv, kr = _mla_prep(proj, mla_q_norm_g, mla_kv_norm_g, table)
    q_b, k_b, v_b = _mla_qkv(cq, ckv, kr, w_uq_g, w_ukv_g, table)
    o_b, mix, lse_b, w_ff1_g = _mla_fwd(q_b, k_b, v_b, o_a, comm=[_Gather([w_ff1[0].astype(BF16)])])

    tm, tn, tk = _pick(s, MM_TM, 128), _pick(d, MM_TN, 128), _pick(MIX, MM_TK, 128)
    row_blk = pl.BlockSpec((tm, tn), lambda i, j, q: (i, j))
    gate_blk = pl.BlockSpec((1, tn), lambda i, j, q: (0, j))

    def gated_residual(acc, ex, outs):
        outs[0][...] = acc
        outs[1][...] = ex[0][...] + ex[1][...] * acc

    y1, x2 = _mm("out_proj", mix, w_out_f, "nn", (s // tm, d // tn, MIX // tk),
                 pl.BlockSpec((tm, tk), lambda i, j, q: (i, q)), pl.BlockSpec((tk, tn), lambda i, j, q: (q, j)),
                 [jax.ShapeDtypeStruct((s, d), F32)] * 2, [row_blk, row_blk], (tm, tn), gated_residual,
                 extras=(x2d, g1), extra_specs=(row_blk, gate_blk))

    (h2,) = _norm_mod("norm2", x2, mlp_norm_g, sc2, sh2)
    tnf, tkd = _pick(ffs, MM_TN, 128), _pick(d, MM_TK, 128)
    rf = ffs // tnf
    ff_blk = pl.BlockSpec((tm, tnf), lambda i, j, q: (i, j))

    def relu_sq(acc, ex, outs):
        u = jnp.maximum(acc, 0.0)
        outs[0][...] = u
        outs[1][...] = (u * u).astype(BF16)

    u, uu, w_ff2_g = _mm("ff1", h2, w_ff1_g, "nn", (s // tm, ff // tnf, d // tkd),
                         pl.BlockSpec((tm, tkd), lambda i, j, q: (i, q)),
                         pl.BlockSpec((None, tkd, tnf), lambda i, j, q: (j // rf, q, j % rf)),
                         [jax.ShapeDtypeStruct((s, ff), F32), jax.ShapeDtypeStruct((s, ff), BF16)], [ff_blk, ff_blk],
                         (tm, tnf), relu_sq, comm=[_Gather([w_ff2[0].astype(BF16)])])
    w_ff2_f = w_ff2_g.reshape(ff, d)
    tkf = _pick(ff, MM_TK, 128)
    y2, x3 = _mm("ff2", uu, w_ff2_f, "nn", (s // tm, d // tn, ff // tkf),
                 pl.BlockSpec((tm, tkf), lambda i, j, q: (i, q)), pl.BlockSpec((tkf, tn), lambda i, j, q: (q, j)),
                 [jax.ShapeDtypeStruct((s, d), F32)] * 2, [row_blk, row_blk], (tm, tn), gated_residual,
                 extras=(x2, g2), extra_specs=(row_blk, gate_blk))

    dx3, dy2, loss_p, dgf, dg2 = _loss_head(x3, tgt, y2, final_g, g2)

    def relu_sq_bwd(acc, ex, outs):
        outs[0][...] = (acc * (2.0 * ex[0][...])).astype(BF16)

    tnf2 = _pick(ff, MM_TN, 128)
    du = _mm("ff2_dx", dy2, w_ff2_f, "nt", (s // tm, ff // tnf2, d // tkd),
             pl.BlockSpec((tm, tkd), lambda i, j, q: (i, q)), pl.BlockSpec((tnf2, tkd), lambda i, j, q: (j, q)),
             [jax.ShapeDtypeStruct((s, ff), BF16)], [pl.BlockSpec((tm, tnf2), lambda i, j, q: (i, j))],
             (tm, tnf2), relu_sq_bwd, extras=(u,), extra_specs=(pl.BlockSpec((tm, tnf2), lambda i, j, q: (i, j)),))[0]
    gw_ff2 = _mm_plain("ff2_dw", uu, dy2, "tn", ff, d, s, BF16)
    tmd, tks = _pick(d, MM_TM, 128), _pick(s, MM_TK, 128)
    gw_ff2 = gw_ff2.reshape(NDEV, ffs, d)
    dh2, s_ff2 = _mm("ff1_dx", du, w_ff1_g, "nt", (s // tm, d // tn, NDEV // 2),
                     pl.BlockSpec((tm, 2 * ffs), lambda i, j, q: (i, q)),
                     pl.BlockSpec((2, tn, ffs), lambda i, j, q: (q, j, 0)),
                     [jax.ShapeDtypeStruct((s, d), F32)], [row_blk], (tm, tn), _store(F32),
                     comm=[_PairSwap([gw_ff2])], b_parts=2)
    c_ff2 = _pair_sum("pair_ff2", gw_ff2, s_ff2, core)
    gw_ff1, p_ff2 = _mm("ff1_dw", h2, du, "tn", (d // tmd, ff // tnf, s // tks),
                        pl.BlockSpec((tks, tmd), lambda i, j, q: (q, i)), pl.BlockSpec((tks, tnf), lambda i, j, q: (q, j)),
                        [jax.ShapeDtypeStruct((NDEV, d, ffs), BF16)],
                        [pl.BlockSpec((None, tmd, tnf), lambda i, j, q: (j // rf, i, j % rf))], (tmd, tnf), _store(BF16),
                        comm=[_ChipScatter([c_ff2])])
    dx2, dy1, dsc2, dsh2, dgm, dg1, s_ff1 = _norm_mod_bwd("norm2_bwd", x2, dh2, dx3, mlp_norm_g, sc2, y1, g1,
                                                          comm=[_PairSwap([gw_ff1])])
    c_ff1 = _pair_sum("pair_ff1", gw_ff1, s_ff1, core)

    dmix = _mm_plain("out_proj_dx", dy1, w_out_f, "nt", s, MIX, d, F32)
    gw_out = _mm_plain("out_proj_dw", mix, dy1, "tn", MIX, d, s, BF16).reshape(NDEV, MIX // NDEV, d)

    delta_b = _mla_delta(dmix, o_b).reshape(MLA_H, 1, s)
    dq_b, dk_b, dv_b, p_ff1, s_out = _mla_bwd(q_b, k_b, v_b, dmix, delta_b, lse_b.reshape(MLA_H, 1, s),
                                              comm=[_ChipScatter([c_ff1]), _PairSwap([gw_out])])
    c_out = _pair_sum("pair_out", gw_out, s_out, core)
    dcq, dckv, dkr, gw_uq_e, gw_ukv, p_out = _mla_qkv_bwd(dq_b, dk_b, dv_b, cq, ckv, w_uq_g, w_ukv_g, table,
                                                          comm=[_ChipScatter([c_out])])
    gw_uq = _fold_swapped(gw_uq_e, NOPE, NOPE + ROPE).astype(BF16)
    gw_ukv = gw_ukv.astype(BF16)

    dproj, dkp, dkc, dvp, dvc, dbias, dsink0, dsink1, s_uq, s_ukv = _swa2_bwd(
        proj_h, dmix, k_top, k_bot, v_top, v_bot, (lse_a0, lse_a1), bias, sinks, comm=[_PairSwap([gw_uq, gw_ukv])])
    c_uq = _pair_sum("pair_uq", gw_uq, s_uq, core)
    c_ukv = _pair_sum("pair_ukv", gw_ukv, s_ukv, core)
    dproj, dgq, dgkv = _mla_prep_bwd(proj, dcq, dckv, dkr, mla_q_norm_g, mla_kv_norm_g, table, dproj)

    def band_grad(cur, prv):
        g = cur + jnp.concatenate([prv[:, BLOCK:], jnp.zeros_like(prv[:, :BLOCK])], axis=1)
        g = g[:, :, :SWA_DH] + g[:, :, SWA_DH:]
        return jnp.concatenate([g[0], g[1]], axis=1)

    dbias = dbias.reshape(SWA_KV, PAIRS, BLOCK, 2, 2 * BLOCK).transpose(0, 1, 3, 2, 4)
    dsink = jnp.stack([dsink0.reshape(SWA_KV, PAIRS, BLOCK), dsink1.reshape(SWA_KV, PAIRS, BLOCK)], axis=2)
    drel_t, dsinks = _bias_reduce(dbias.reshape(SWA_HEADS, BLOCK * 2 * BLOCK), onehot, dsink.reshape(SWA_HEADS, BLOCK))
    dkv = jnp.concatenate([band_grad(dkc, dkp), band_grad(dvc, dvp)], axis=1).astype(BF16)
    dproj = lax.dynamic_update_slice(dproj, dkv, (0, OFF_K))
    def fold_rotary(acc, ex, outs):
        slab = acc[:, TAIL - 2 * ROPE:TAIL]
        lane = lax.broadcasted_iota(jnp.int32, slab.shape, 1)
        folded = slab + jnp.where(lane < ROPE // 2, pltpu.roll(slab, ROPE // 2, 1), pltpu.roll(slab, 3 * ROPE // 2, 1))
        last = pl.program_id(1) == IN_EXT // TAIL - 1
        outs[0][:, 0:TAIL - 2 * ROPE] = acc[:, 0:TAIL - 2 * ROPE].astype(BF16)
        outs[0][:, TAIL - 2 * ROPE:TAIL] = jnp.where(last, folded, slab).astype(BF16)

    tmw, tks = _pick(d, MM_TM, 128), _pick(s, MM_TK, 128)
    gw_in = _mm("proj_dw", h1, dproj, "tn", (d // tmw, IN_EXT // TAIL, s // tks),
                pl.BlockSpec((tks, tmw), lambda i, j, q: (q, i)), pl.BlockSpec((tks, TAIL), lambda i, j, q: (q, j)),
                [jax.ShapeDtypeStruct((d, IN_COLS), BF16)], [pl.BlockSpec((tmw, TAIL), lambda i, j, q: (i, j))],
                (tmw, TAIL), fold_rotary)[0].reshape(NDEV, d // NDEV, IN_COLS)
    tkt = IN_EXT
    dh1, s_in, p_uq, p_ukv = _mm(
        "proj_dx", dproj, w_in_e, "nt", (s // tm, d // tn, IN_EXT // tkt),
        pl.BlockSpec((tm, tkt), lambda i, j, q: (i, q)), pl.BlockSpec((tn, tkt), lambda i, j, q: (j, q)),
        [jax.ShapeDtypeStruct((s, d), F32)], [row_blk], (tm, tn), _store(F32),
        comm=[_PairSwap([gw_in]), _ChipScatter([c_uq, c_ukv])])
    c_in = _pair_sum("pair_in", gw_in, s_in, core)
    gx, dsc1, dsh1, dga, p_in = _norm_mod_bwd("norm1_bwd", x2d, dh1, dx2, attn_norm_g, sc1,
                                              comm=[_ChipScatter([c_in])])

    small = [jnp.concatenate([dsh1, dsc1, dg1, dsh2, dsc2, dg2], axis=1), dga, dgm, dgf, dgq, dgkv,
             dsinks.reshape(1, SWA_HEADS), drel_t.T.reshape(1, REL_BUCKETS * SWA_HEADS)]
    n_small = sum(a.shape[1] for a in small)
    n_pad = -(n_small + 1) % 1024 + 1
    rows_small = (n_small + n_pad) // 128
    pad = jnp.zeros((1, n_pad), F32)
    pack = lambda parts, tail=pad: jnp.concatenate([p.reshape(1, -1) for p in parts] + [tail], axis=1).reshape(rows_small, 128)
    (small_g,) = _exchange("gather_small", _Gather([pack(small, jnp.concatenate([loss_p, pad[:, 1:]], axis=1))]))
    small_names = (b_mod, attn_norm_g, mlp_norm_g, final_norm_g, mla_q_norm_g, mla_kv_norm_g, swa_sinks, rel_bias)
    small_m = (m_b_mod, m_attn_norm_g, m_mlp_norm_g, m_final_norm_g, m_mla_q_norm_g, m_mla_kv_norm_g, m_swa_sinks, m_rel_bias)
    small_v = (v_b_mod, v_attn_norm_g, v_mlp_norm_g, v_final_norm_g, v_mla_q_norm_g, v_mla_kv_norm_g, v_swa_sinks, v_rel_bias)
    small_out = _adamw("adamw_small", pack(small_names), small_g, pack(small_m), pack(small_v), parts=True)

    def unpack(flat):
        flat = flat.reshape(1, -1)
        out, off = [], 0
        for a in small_names:
            out.append(flat[:, off:off + a.size].reshape(a.shape))
            off += a.size
        return out

    sg, sd, sm, sv = [unpack(o) for o in small_out]
    loss = small_out[0].reshape(-1)[n_small]

    dmod_cols = lax.dynamic_slice(small_g.reshape(NDEV, -1), (0, me * nmod), (NDEV, nmod))
    gw_mod = _mod_wgrad(act_all, dmod_cols)
    big = {"w_mod": _adamw("adamw_w_mod", w_mod[0], gw_mod, m_w_mod[0], v_w_mod[0], parts=False)}

    for name, w, p, m, v in (("w_in", w_in, p_in, m_w_in, v_w_in), ("w_uq", w_uq, p_uq, m_w_uq, v_w_uq),
                             ("w_ukv", w_ukv, p_ukv, m_w_ukv, v_w_ukv), ("w_out", w_out, p_out, m_w_out, v_w_out),
                             ("w_ff1", w_ff1, p_ff1, m_w_ff1, v_w_ff1), ("w_ff2", w_ff2, p_ff2, m_w_ff2, v_w_ff2)):
        big[name] = _adamw("adamw_" + name, w[0], p, m[0], v[0], parts=True)

    order = ("w_mod", "b_mod", "attn_norm_g", "w_in", "swa_sinks", "rel_bias", "mla_q_norm_g", "w_uq", "mla_kv_norm_g",
             "w_ukv", "w_out", "mlp_norm_g", "w_ff1", "w_ff2", "final_norm_g")
    small_idx = {"b_mod": 0, "attn_norm_g": 1, "mlp_norm_g": 2, "final_norm_g": 3, "mla_q_norm_g": 4,
                 "mla_kv_norm_g": 5, "swa_sinks": 6, "rel_bias": 7}
    outs = []
    for kind, small_list in enumerate((sg, sd, sm, sv)):
        for name in order:
            outs.append(small_list[small_idx[name]] if name in small_idx else big[name][kind][None])
    return (loss, gx[None], *outs)
```

```python
import functools
import math

import jax
import jax.numpy as jnp
from jax import lax
from jax.experimental import pallas as pl
from jax.experimental.pallas import tpu as pltpu

F32 = jnp.float32
BF16 = jnp.bfloat16

NDEV = 8
EPS = 1e-6
BLOCK = 128
SWA_HEADS, SWA_KV, SWA_DH, SWA_GROUP = 16, 2, 64, 8
REL_BUCKETS, REL_MAX_DIST = 32, 128
MLA_H, Q_RANK, KV_RANK, NOPE, ROPE, VDIM = 8, 384, 128, 128, 64, 128
ROPE_THETA = 10000.0
OFF_K, OFF_V, OFF_CQ, OFF_CKV, OFF_KR, IN_COLS = 1024, 1152, 1280, 1664, 1792, 1856
IN_EXT = IN_COLS + ROPE
TAIL0, TAIL = OFF_CQ, IN_EXT - OFF_CQ
QW = NOPE + 2 * ROPE
MIX = SWA_HEADS * SWA_DH + MLA_H * VDIM
MLA_SCALE = (NOPE + ROPE) ** -0.5
SWA_SCALE = SWA_DH ** -0.5

ADAM_LR, ADAM_B1, ADAM_B2, ADAM_EPS, ADAM_WD, ADAM_STEP = 0.001, 0.9, 0.999, 1e-08, 0.01, 10

VMEM_LIMIT = 52 * 1024 * 1024
ROW_TILE = 256
MM_TM, MM_TN, MM_TK = 1024, 1024, 2048
ATT_T = 512
MLA_HB = 2
ADAM_ELEMS = 256 * 1024


MESH_ID = pl.DeviceIdType.MESH


def _place():
    x, y, c = lax.axis_index("x"), lax.axis_index("y"), lax.axis_index("c")
    return x, y, c, 2 * x + y


def _chip(x, y, k):
    return (1 - x if k & 2 else x, 1 - y if k & 1 else y)


def _dma_sems(*counts):
    return [pltpu.SemaphoreType.DMA((n,)) for n in counts]


class _Gather:
    def __init__(self, arrays):
        self.arrays = list(arrays)
        n = len(self.arrays)
        self.out_shape = [jax.ShapeDtypeStruct((NDEV,) + a.shape, a.dtype) for a in self.arrays]
        self.sems = _dma_sems(7 * n, 7 * n, n)

    def _copy(self, sems, a, k, src, dst, to):
        return pltpu.make_async_remote_copy(src_ref=src, dst_ref=dst, send_sem=sems[0].at[7 * a + k],
                                            recv_sem=sems[1].at[7 * a + k], device_id=to, device_id_type=MESH_ID)

    def start(self, ins, outs, sems):
        x, y, c, q = _place()
        me = 2 * q + c
        for a in range(len(ins)):
            pltpu.make_async_copy(ins[a], outs[a].at[me], sems[2].at[a]).start()
            self._copy(sems, a, 0, ins[a], outs[a].at[me], (x, y, 1 - c)).start()
            for k in (1, 2, 3):
                self._copy(sems, a, k, ins[a], outs[a].at[me], (*_chip(x, y, k), c)).start()

    def relay(self, ins, outs, sems):
        x, y, c, q = _place()
        sib = (x, y, 1 - c)
        for k in (1, 2, 3):
            for a in range(len(ins)):
                blk = outs[a].at[2 * (q ^ k) + c]
                self._copy(sems, a, k, ins[a], blk, (*_chip(x, y, k), c)).wait_recv()
                self._copy(sems, a, 3 + k, blk, blk, sib).start()

    def finish(self, ins, outs, sems):
        x, y, c, q = _place()
        me, sib = 2 * q + c, (x, y, 1 - c)
        n = len(ins)
        for a in range(n):
            self._copy(sems, a, 0, ins[a], outs[a].at[2 * q + 1 - c], sib).wait_recv()
            for k in (1, 2, 3):
                blk = outs[a].at[2 * (q ^ k) + 1 - c]
                self._copy(sems, a, 3 + k, blk, blk, sib).wait_recv()
        for a in range(n):
            for k in range(7):
                self._copy(sems, a, k, ins[a], outs[a].at[me], sib).wait_send()
            pltpu.make_async_copy(ins[a], outs[a].at[me], sems[2].at[a]).wait()


class _PairSwap:
    def __init__(self, arrays):
        self.arrays = list(arrays)
        n = len(self.arrays)
        self.out_shape = [jax.ShapeDtypeStruct((NDEV // 2,) + a.shape[1:], a.dtype) for a in self.arrays]
        self.sems = _dma_sems(4 * n, 4 * n)

    def _copy(self, sems, a, p, src, dst, to):
        return pltpu.make_async_remote_copy(src_ref=src, dst_ref=dst, send_sem=sems[0].at[4 * a + p],
                                            recv_sem=sems[1].at[4 * a + p], device_id=to, device_id_type=MESH_ID)

    def start(self, ins, outs, sems):
        x, y, c, _ = _place()
        for a in range(len(ins)):
            for p in range(4):
                self._copy(sems, a, p, ins[a].at[2 * p + 1 - c], outs[a].at[p], (x, y, 1 - c)).start()

    def finish(self, ins, outs, sems):
        x, y, c, _ = _place()
        for a in range(len(ins)):
            for p in range(4):
                cp = self._copy(sems, a, p, ins[a].at[2 * p + 1 - c], outs[a].at[p], (x, y, 1 - c))
                cp.wait_recv()
                cp.wait_send()


class _ChipScatter:
    def __init__(self, arrays):
        self.arrays = list(arrays)
        n = len(self.arrays)
        self.out_shape = [jax.ShapeDtypeStruct(a.shape, a.dtype) for a in self.arrays]
        self.sems = _dma_sems(3 * n, 3 * n, n)

    def _copy(self, sems, a, k, src, dst, to):
        return pltpu.make_async_remote_copy(src_ref=src, dst_ref=dst, send_sem=sems[0].at[3 * a + k - 1],
                                            recv_sem=sems[1].at[3 * a + k - 1], device_id=to, device_id_type=MESH_ID)

    def start(self, ins, outs, sems):
        x, y, c, q = _place()
        for a in range(len(ins)):
            pltpu.make_async_copy(ins[a].at[q], outs[a].at[q], sems[2].at[a]).start()
            for k in (1, 2, 3):
                self._copy(sems, a, k, ins[a].at[q ^ k], outs[a].at[q], (*_chip(x, y, k), c)).start()

    def finish(self, ins, outs, sems):
        x, y, c, q = _place()
        for a in range(len(ins)):
            for k in (1, 2, 3):
                cp = self._copy(sems, a, k, ins[a].at[q ^ k], outs[a].at[q ^ k], (*_chip(x, y, k), c))
                cp.wait_recv()
                cp.wait_send()
            pltpu.make_async_copy(ins[a].at[q], outs[a].at[q], sems[2].at[a]).wait()


def _call(body, **kw):
    return pl.pallas_call(body, **kw)


def _pcall(body, comm=None, **kw):
    if not comm:
        return _call(body, **kw)
    grid = kw["grid"]
    in_specs, out_specs, out_shape = list(kw["in_specs"]), list(kw["out_specs"]), list(kw["out_shape"])
    scratch = list(kw.get("scratch_shapes", ()))
    n_in, n_out, n_scr = len(in_specs), len(out_shape), len(scratch)
    n_cin = [len(j.arrays) for j in comm]
    n_sem = [len(j.sems) for j in comm]
    n = sum(n_cin)
    hbm = pl.BlockSpec(memory_space=pltpu.HBM)

    def carried(*refs):
        ins, cins = refs[:n_in], refs[n_in:n_in + n]
        outs, couts = refs[n_in + n:n_in + n + n_out], refs[n_in + n + n_out:n_in + 2 * n + n_out]
        scr, sems = refs[n_in + 2 * n + n_out:n_in + 2 * n + n_out + n_scr], refs[n_in + 2 * n + n_out + n_scr:]
        ids = [pl.program_id(ax) for ax in range(len(grid))]
        first = functools.reduce(jnp.logical_and, [i == 0 for i in ids])
        last = functools.reduce(jnp.logical_and, [i == g - 1 for i, g in zip(ids, grid)])

        def each(method):
            ai = si = 0
            for job, na, ns in zip(comm, n_cin, n_sem):
                if hasattr(job, method):
                    getattr(job, method)(cins[ai:ai + na], couts[ai:ai + na], sems[si:si + ns])
                ai, si = ai + na, si + ns

        @pl.when(first)
        def _():
            each("start")

        steps = math.prod(grid)
        if steps > 1:
            at, rest = [], steps - 2
            for g in reversed(grid):
                at.append(rest % g)
                rest //= g
            before_last = functools.reduce(jnp.logical_and, [i == a for i, a in zip(ids, reversed(at))])

            @pl.when(before_last)
            def _():
                each("relay")

        body(*ins, *outs, *scr)

        @pl.when(last)
        def _():
            if steps == 1:
                each("relay")
            each("finish")

    kw.update(in_specs=in_specs + [hbm] * n, out_specs=out_specs + [hbm] * n,
              out_shape=out_shape + [o for j in comm for o in j.out_shape],
              scratch_shapes=scratch + [sm for j in comm for sm in j.sems],
              compiler_params=_cparams(("arbitrary",) * len(grid)))
    call = _call(carried, **kw)
    return lambda *args: call(*args, *[a for j in comm for a in j.arrays])


def _cparams(sem):
    return pltpu.CompilerParams(dimension_semantics=sem, vmem_limit_bytes=VMEM_LIMIT)


def _pick(n, pref, align):
    if n <= pref:
        return n
    t = (pref // align) * align
    while t >= align:
        if n % t == 0:
            return t
        t -= align
    return n


def _split3(x):
    a = x.astype(BF16)
    r = x - a.astype(F32)
    b = r.astype(BF16)
    c = (r - b.astype(F32)).astype(BF16)
    return a, b, c


def _exchange(name, job):
    n = len(job.arrays)

    def body(*refs):
        ins, outs, sems = refs[:n], refs[n:2 * n], refs[2 * n:]
        job.start(ins, outs, sems)
        if hasattr(job, "relay"):
            job.relay(ins, outs, sems)
        job.finish(ins, outs, sems)

    hbm = pl.BlockSpec(memory_space=pltpu.HBM)
    return _call(body, name=name, out_shape=job.out_shape, in_specs=[hbm] * n, out_specs=[hbm] * n,
                 scratch_shapes=job.sems)(*job.arrays)


def _pair_sum(name, g, r, core):
    _, rr, cc = g.shape
    tr = rr if rr * cc <= 4 * ADAM_ELEMS else _pick(rr, max(16, 4 * ADAM_ELEMS // cc // 16 * 16), 16)

    def body(g_ref, r_ref, c_ref, o_ref):
        north = c_ref[:, 0:1] > 0.5
        mine = jnp.where(north, g_ref[1].astype(F32), g_ref[0].astype(F32))
        o_ref[...] = (mine + r_ref[...].astype(F32)).astype(o_ref.dtype)

    return _pcall(
        body, name=name, grid=(NDEV // 2, rr // tr),
        in_specs=[pl.BlockSpec((None, 2, tr, cc), lambda p, i: (p, 0, i, 0)),
                  pl.BlockSpec((None, tr, cc), lambda p, i: (p, i, 0)), pl.BlockSpec((1, 128), lambda p, i: (0, 0))],
        out_specs=pl.BlockSpec((None, tr, cc), lambda p, i: (p, i, 0)),
        out_shape=jax.ShapeDtypeStruct((NDEV // 2, rr, cc), g.dtype),
        compiler_params=_cparams(("parallel", "parallel")))(g.reshape(NDEV // 2, 2, rr, cc), r, core)


_DIMS = {"nn": (((1,), (0,)), ((), ())), "nt": (((1,), (1,)), ((), ())), "tn": (((0,), (0,)), ((), ()))}


def _mm(name, a, b, kind, grid, a_spec, b_spec, out_shape, out_specs, acc_shape, epilogue,
        extras=(), extra_specs=(), comm=None, b_parts=1):
    nk, ne, no = grid[2], len(extras), len(out_shape)

    def body(*refs):
        a_ref, b_ref = refs[0], refs[1]
        ex, outs = refs[2:2 + ne], refs[2 + ne:2 + ne + no]
        if b_parts == 1:
            part = lax.dot_general(a_ref[...].astype(BF16), b_ref[...].astype(BF16), _DIMS[kind],
                                   preferred_element_type=F32)
        else:
            kp = a_ref.shape[1] // b_parts
            part = sum(lax.dot_general(a_ref[:, p * kp:(p + 1) * kp].astype(BF16), b_ref[p].astype(BF16), _DIMS[kind],
                                       preferred_element_type=F32) for p in range(b_parts))
        if nk == 1:
            epilogue(part, ex, outs)
            return
        acc = refs[-1]
        k = pl.program_id(2)

        @pl.when(k == 0)
        def _():
            acc[...] = part

        @pl.when(jnp.logical_and(k > 0, k < nk - 1))
        def _():
            acc[...] += part

        @pl.when(k == nk - 1)
        def _():
            epilogue(acc[...] + part, ex, outs)

    return _pcall(
        body, comm=comm, name=name, grid=grid, in_specs=[a_spec, b_spec, *extra_specs], out_specs=out_specs,
        out_shape=out_shape, scratch_shapes=[pltpu.VMEM(acc_shape, F32)] if nk > 1 else [],
        compiler_params=_cparams(("parallel", "parallel", "arbitrary")),
    )(a, b, *extras)


def _store(dtype):
    def epi(acc, ex, outs):
        outs[0][...] = acc.astype(dtype)
    return epi


def _mm_plain(name, a, b, kind, m, n, k, out_dtype, tm=None, tn=None, tk=None):
    tm = _pick(m, tm or MM_TM, 128)
    tn = _pick(n, tn or MM_TN, 128)
    tk = _pick(k, tk or MM_TK, 128)
    a_spec = pl.BlockSpec((tk, tm), lambda i, j, q: (q, i)) if kind == "tn" else pl.BlockSpec((tm, tk), lambda i, j, q: (i, q))
    b_spec = pl.BlockSpec((tn, tk), lambda i, j, q: (j, q)) if kind == "nt" else pl.BlockSpec((tk, tn), lambda i, j, q: (q, j))
    return _mm(name, a, b, kind, (m // tm, n // tn, k // tk), a_spec, b_spec,
               [jax.ShapeDtypeStruct((m, n), out_dtype)], [pl.BlockSpec((tm, tn), lambda i, j, q: (i, j))],
               (tm, tn), _store(out_dtype))[0]


def _row(ts, d):
    return pl.BlockSpec((ts, d), lambda i: (i, 0))


def _vec(d):
    return pl.BlockSpec((1, d), lambda i: (0, 0))


def _norm_mod(name, x, gain, sc, sh, comm=None):
    s, d = x.shape
    ts = _pick(s, ROW_TILE, 16)

    def body(x_ref, g_ref, sc_ref, sh_ref, h_ref):
        xv = x_ref[...]
        r = lax.rsqrt(jnp.mean(xv * xv, axis=-1, keepdims=True) + EPS)
        h_ref[...] = ((xv * r) * g_ref[...] * (1.0 + sc_ref[...]) + sh_ref[...]).astype(BF16)

    return _pcall(body, comm=comm, name=name, grid=(s // ts,), in_specs=[_row(ts, d), _vec(d), _vec(d), _vec(d)],
                  out_specs=[_row(ts, d)], out_shape=[jax.ShapeDtypeStruct((s, d), BF16)],
                  compiler_params=_cparams(("parallel",)))(x, gain, sc, sh)


def _loss_head(x3, tgt, y2, gf, g2):
    s, d = x3.shape
    ts = _pick(s, ROW_TILE, 16)

    def body(x_ref, t_ref, y_ref, gf_ref, g2_ref, dx_ref, dy_ref, loss_ref, dgf_ref, dg2_ref):
        @pl.when(pl.program_id(0) == 0)
        def _():
            loss_ref[...] = jnp.zeros_like(loss_ref)
            dgf_ref[...] = jnp.zeros_like(dgf_ref)
            dg2_ref[...] = jnp.zeros_like(dg2_ref)

        xv = x_ref[...]
        r = lax.rsqrt(jnp.mean(xv * xv, axis=-1, keepdims=True) + EPS)
        xn = xv * r
        err = xn * gf_ref[...] - t_ref[...]
        loss_ref[...] += 0.5 * jnp.sum(jnp.mean(err * err, axis=-1, keepdims=True), axis=0, keepdims=True)
        dout = err * (1.0 / d)
        dgf_ref[...] += jnp.sum(dout * xn, axis=0, keepdims=True)
        dxn = dout * gf_ref[...]
        dx = r * (dxn - xn * jnp.mean(dxn * xn, axis=-1, keepdims=True))
        dx_ref[...] = dx
        dy_ref[...] = (dx * g2_ref[...]).astype(BF16)
        dg2_ref[...] += jnp.sum(dx * y_ref[...], axis=0, keepdims=True)

    one = pl.BlockSpec((1, 1), lambda i: (0, 0))
    return _pcall(
        body, name="loss_head", grid=(s // ts,),
        in_specs=[_row(ts, d), _row(ts, d), _row(ts, d), _vec(d), _vec(d)],
        out_specs=[_row(ts, d), _row(ts, d), one, _vec(d), _vec(d)],
        out_shape=[jax.ShapeDtypeStruct((s, d), F32), jax.ShapeDtypeStruct((s, d), BF16),
                   jax.ShapeDtypeStruct((1, 1), F32), jax.ShapeDtypeStruct((1, d), F32),
                   jax.ShapeDtypeStruct((1, d), F32)],
        compiler_params=_cparams(("arbitrary",)))(x3, tgt, y2, gf, g2)


def _norm_mod_bwd(name, x, dh, dres, gain, sc, y_prev=None, gate=None, comm=None):
    s, d = x.shape
    ts = _pick(s, ROW_TILE, 16)
    gated = y_prev is not None

    def body(*refs):
        if gated:
            x_ref, dh_ref, dr_ref, g_ref, sc_ref, y_ref, gt_ref, dx_ref, dy_ref, dsc_ref, dsh_ref, dg_ref, dgt_ref = refs
        else:
            x_ref, dh_ref, dr_ref, g_ref, sc_ref, dx_ref, dsc_ref, dsh_ref, dg_ref = refs

        @pl.when(pl.program_id(0) == 0)
        def _():
            dsc_ref[...] = jnp.zeros_like(dsc_ref)
            dsh_ref[...] = jnp.zeros_like(dsh_ref)
            dg_ref[...] = jnp.zeros_like(dg_ref)
            if gated:
                dgt_ref[...] = jnp.zeros_like(dgt_ref)

        xv, dhv = x_ref[...], dh_ref[...]
        r = lax.rsqrt(jnp.mean(xv * xv, axis=-1, keepdims=True) + EPS)
        xn = xv * r
        dsc_ref[...] += jnp.sum(dhv * (xn * g_ref[...]), axis=0, keepdims=True)
        dsh_ref[...] += jnp.sum(dhv, axis=0, keepdims=True)
        da = dhv * (1.0 + sc_ref[...])
        dg_ref[...] += jnp.sum(da * xn, axis=0, keepdims=True)
        dxn = da * g_ref[...]
        dx = dr_ref[...] + r * (dxn - xn * jnp.mean(dxn * xn, axis=-1, keepdims=True))
        dx_ref[...] = dx
        if gated:
            dy_ref[...] = (dx * gt_ref[...]).astype(BF16)
            dgt_ref[...] += jnp.sum(dx * y_ref[...], axis=0, keepdims=True)

    ins = [x, dh, dres, gain, sc] + ([y_prev, gate] if gated else [])
    in_specs = [_row(ts, d)] * 3 + [_vec(d)] * 2 + ([_row(ts, d), _vec(d)] if gated else [])
    vec_out = jax.ShapeDtypeStruct((1, d), F32)
    out_shape = [jax.ShapeDtypeStruct((s, d), F32)] + ([jax.ShapeDtypeStruct((s, d), BF16)] if gated else [])
    out_shape += [vec_out] * (4 if gated else 3)
    out_specs = [_row(ts, d)] * (2 if gated else 1) + [_vec(d)] * (4 if gated else 3)
    return _pcall(body, comm=comm, name=name, grid=(s // ts,), in_specs=in_specs, out_specs=out_specs,
                  out_shape=out_shape, compiler_params=_cparams(("arbitrary",)))(*ins)


def _dot3(a, b, dims):
    a1, a2, _ = _split3(a)
    b1, b2, _ = _split3(b)
    dot = functools.partial(lax.dot_general, dimension_numbers=dims, preferred_element_type=F32)
    return dot(a1, b1) + (dot(a1, b2) + dot(a2, b1))


def _mod_fwd(c_all, w, b_cols, comm=None):
    nb, d = c_all.shape
    n = w.shape[1]
    tk = _pick(d, 512, 128)
    nk = d // tk

    def body(c_ref, w_ref, b_ref, act_ref, out_ref):
        k = pl.program_id(0)
        cv = c_ref[...]
        act = cv * (1.0 / (1.0 + jnp.exp(-cv)))
        act_ref[...] = act

        @pl.when(k == 0)
        def _():
            out_ref[...] = jnp.broadcast_to(b_ref[...], out_ref.shape)

        out_ref[...] += _dot3(act, w_ref[...], _DIMS["nn"])

    return _pcall(
        body, comm=comm, name="mod_fwd", grid=(nk,),
        in_specs=[pl.BlockSpec((nb, tk), lambda k: (0, k)), pl.BlockSpec((tk, n), lambda k: (k, 0)),
                  pl.BlockSpec((1, n), lambda k: (0, 0))],
        out_specs=[pl.BlockSpec((nb, tk), lambda k: (0, k)), pl.BlockSpec((nb, n), lambda k: (0, 0))],
        out_shape=[jax.ShapeDtypeStruct((nb, d), F32), jax.ShapeDtypeStruct((nb, n), F32)],
        compiler_params=_cparams(("arbitrary",)))(c_all, w, b_cols)


def _mod_wgrad(act_all, dmod_cols):
    nb, d = act_all.shape
    n = dmod_cols.shape[1]
    tm = _pick(d, 512, 128)

    def body(a_ref, d_ref, o_ref):
        o_ref[...] = _dot3(a_ref[...], d_ref[...], _DIMS["tn"])

    return _pcall(
        body, name="mod_wgrad", grid=(d // tm,),
        in_specs=[pl.BlockSpec((nb, tm), lambda i: (0, i)), pl.BlockSpec((nb, n), lambda i: (0, 0))],
        out_specs=pl.BlockSpec((tm, n), lambda i: (i, 0)), out_shape=jax.ShapeDtypeStruct((d, n), F32),
        compiler_params=_cparams(("parallel",)))(act_all, dmod_cols)


def _bias_expand(rel_t, onehot_t):
    h, _ = rel_t.shape
    n = onehot_t.shape[1]

    def body(r_ref, o_ref, out_ref):
        a, b, c = _split3(r_ref[...])
        dot = functools.partial(lax.dot_general, dimension_numbers=_DIMS["nn"], preferred_element_type=F32)
        oh = o_ref[...]
        out_ref[...] = dot(a, oh) + (dot(b, oh) + dot(c, oh))

    full = lambda shp: pl.BlockSpec(shp, lambda: (0,) * len(shp))
    return _pcall(body, name="bias_expand", in_specs=[full(rel_t.shape), full(onehot_t.shape)],
                  out_specs=full((h, n)), out_shape=jax.ShapeDtypeStruct((h, n), F32),
                  compiler_params=pltpu.CompilerParams(vmem_limit_bytes=VMEM_LIMIT))(rel_t, onehot_t)


def _bias_reduce(dbias, onehot, dsink_rows):
    h, n = dbias.shape

    def body(d_ref, o_ref, s_ref, out_ref, so_ref):
        a, b, c = _split3(d_ref[...])
        dot = functools.partial(lax.dot_general, dimension_numbers=_DIMS["nn"], preferred_element_type=F32)
        oh = o_ref[...]
        out_ref[...] = dot(a, oh) + (dot(b, oh) + dot(c, oh))
        so_ref[...] = jnp.sum(s_ref[...], axis=-1, keepdims=True)

    full = lambda shp: pl.BlockSpec(shp, lambda: (0,) * len(shp))
    return _pcall(body, name="bias_reduce", in_specs=[full(dbias.shape), full(onehot.shape), full(dsink_rows.shape)],
                  out_specs=[full((h, REL_BUCKETS)), full((h, 1))],
                  out_shape=[jax.ShapeDtypeStruct((h, REL_BUCKETS), F32), jax.ShapeDtypeStruct((h, 1), F32)],
                  compiler_params=pltpu.CompilerParams(vmem_limit_bytes=VMEM_LIMIT))(dbias, onehot, dsink_rows)


PAIRS = SWA_GROUP // 2
PROWS = PAIRS * BLOCK
PCOLS = 2 * 2 * BLOCK


def _swa2_specs():
    tok = lambda width: pl.BlockSpec((BLOCK, width), lambda g, n: (n, g))
    prev = pl.BlockSpec((None, BLOCK, 2 * SWA_DH), lambda g, n: (g, jnp.maximum(n - 1, 0), 0))
    cur = pl.BlockSpec((None, BLOCK, 2 * SWA_DH), lambda g, n: (g, n, 0))
    bias_spec = pl.BlockSpec((None, PROWS, PCOLS), lambda g, n: (g, 0, 0))
    col_spec = pl.BlockSpec((None, PROWS, 1), lambda g, n: (g, 0, 0))
    lse_spec = pl.BlockSpec((None, None, PROWS, 1), lambda g, n: (g, n, 0, 0))
    return tok, prev, cur, bias_spec, col_spec, lse_spec


def _stack_pairs(blk):
    return jnp.concatenate([blk[:, p * 2 * SWA_DH:(p + 1) * 2 * SWA_DH] for p in range(PAIRS)], axis=0)


def _band(tp, tc, bp, bc):
    return jnp.concatenate([tp[...], tc[...], bp[...], bc[...]], axis=0)


def _swa2_scores(q_ref, kd, bias_ref, n):
    q2 = _stack_pairs(q_ref[...])
    s2 = lax.dot_general(q2, kd, _DIMS["nt"], preferred_element_type=F32) * SWA_SCALE + bias_ref[...]
    col = lax.broadcasted_iota(jnp.int32, s2.shape, 1)
    before_start = jnp.logical_and(n == 0, (col & (2 * BLOCK - 1)) < BLOCK)
    return q2, jnp.where(before_start, -jnp.inf, s2)


def _swa2_fwd(src, ktop, kbot, vtop, vbot, bias, sinks, comm=None):
    s = src.shape[0]
    nb = s // BLOCK
    tok, prev, cur, bias_spec, col_spec, lse_spec = _swa2_specs()

    def body(q_ref, ktp, ktc, kbp, kbc, vtp, vtc, vbp, vbc, bias_ref, sa_ref, sb_ref, o_ref, la_ref, lb_ref):
        n = pl.program_id(1)
        _, s2 = _swa2_scores(q_ref, _band(ktp, ktc, kbp, kbc), bias_ref, n)
        row = lax.broadcasted_iota(jnp.int32, (PCOLS, 2 * SWA_DH), 0)
        lane = lax.broadcasted_iota(jnp.int32, (PCOLS, 2 * SWA_DH), 1)
        ones = jnp.where(lane == row // (2 * BLOCK), 1.0, 0.0).astype(BF16)
        ps, ms, sinks_ = [], [], []
        for half, sink_ref in enumerate((sa_ref, sb_ref)):
            sc = s2[:, half * 2 * BLOCK:(half + 1) * 2 * BLOCK]
            m = jnp.maximum(jnp.max(sc, axis=-1, keepdims=True), sink_ref[...])
            ps.append(jnp.exp(sc - m).astype(BF16))
            ms.append(m)
        acc = lax.dot_general(jnp.concatenate(ps, axis=1), jnp.concatenate([_band(vtp, vtc, vbp, vbc), ones], axis=1),
                              _DIMS["nn"], preferred_element_type=F32)
        dens = []
        for half, (sink_ref, lse_ref) in enumerate(((sa_ref, la_ref), (sb_ref, lb_ref))):
            den = acc[:, 2 * SWA_DH + half:2 * SWA_DH + half + 1] + jnp.exp(sink_ref[...] - ms[half])
            lse_ref[...] = ms[half] + jnp.log(den)
            dens.append(den)
        lo = lax.broadcasted_iota(jnp.int32, (PROWS, 2 * SWA_DH), 1) < SWA_DH
        o2 = acc[:, 0:2 * SWA_DH] / jnp.where(lo, dens[0], dens[1])
        for p in range(PAIRS):
            o_ref[:, p * 2 * SWA_DH:(p + 1) * 2 * SWA_DH] = o2[p * BLOCK:(p + 1) * BLOCK].astype(BF16)

    lse_shape = jax.ShapeDtypeStruct((SWA_KV, nb, PROWS, 1), F32)
    return _pcall(
        body, comm=comm, name="swa_fwd", grid=(SWA_KV, nb),
        in_specs=[tok(PROWS), prev, cur, prev, cur, prev, cur, prev, cur, bias_spec, col_spec, col_spec],
        out_specs=[tok(PROWS), lse_spec, lse_spec],
        out_shape=[jax.ShapeDtypeStruct((s, MIX), BF16), lse_shape, lse_shape],
        compiler_params=_cparams(("parallel", "parallel")))(
            src, ktop, ktop, kbot, kbot, vtop, vtop, vbot, vbot, bias, sinks[0], sinks[1])


def _swa2_bwd(src, dsrc, ktop, kbot, vtop, vbot, lses, bias, sinks, comm=None):
    s = src.shape[0]
    nb = s // BLOCK
    tok, prev, cur, bias_spec, col_spec, lse_spec = _swa2_specs()
    lane_lo = lambda shape: lax.broadcasted_iota(jnp.int32, shape, 1) < SWA_DH

    def body(q_ref, do_ref, ktp, ktc, kbp, kbc, vtp, vtc, vbp, vbc, la_ref, lb_ref, bias_ref, sa_ref, sb_ref,
             dq_ref, dkp_ref, dkc_ref, dvp_ref, dvc_ref, dbias_ref, dsa_ref, dsb_ref):
        n = pl.program_id(1)

        @pl.when(n == 0)
        def _():
            dbias_ref[...] = jnp.zeros_like(dbias_ref)
            dsa_ref[...] = jnp.zeros_like(dsa_ref)
            dsb_ref[...] = jnp.zeros_like(dsb_ref)

        kd = _band(ktp, ktc, kbp, kbc)
        q2, s2 = _swa2_scores(q_ref, kd, bias_ref, n)
        do2 = _stack_pairs(do_ref[...]).astype(BF16)
        dp2 = lax.dot_general(do2, _band(vtp, vtc, vbp, vbc), _DIMS["nt"], preferred_element_type=F32)
        ps, dss = [], []
        for half, (sink_ref, lse_ref, dsink_ref) in enumerate(((sa_ref, la_ref, dsa_ref), (sb_ref, lb_ref, dsb_ref))):
            cols = slice(half * 2 * BLOCK, (half + 1) * 2 * BLOCK)
            lse_v = lse_ref[...]
            p = jnp.exp(s2[:, cols] - lse_v)
            dp = dp2[:, cols]
            delta = jnp.sum(p * dp, axis=-1, keepdims=True)
            ds = p * (dp - delta)
            dsink_ref[...] += -jnp.exp(sink_ref[...] - lse_v) * delta
            ps.append(p.astype(BF16))
            dss.append(ds)
        ds2 = jnp.concatenate(dss, axis=1)
        dbias_ref[...] += ds2
        dsb2 = (ds2 * SWA_SCALE).astype(BF16)
        dq2 = lax.dot_general(dsb2, kd, _DIMS["nn"], preferred_element_type=F32)
        for p in range(PAIRS):
            dq_ref[:, p * 2 * SWA_DH:(p + 1) * 2 * SWA_DH] = dq2[p * BLOCK:(p + 1) * BLOCK].astype(BF16)
        dk = lax.dot_general(dsb2, q2, _DIMS["tn"], preferred_element_type=F32)
        dv = lax.dot_general(jnp.concatenate(ps, axis=1), do2, _DIMS["tn"], preferred_element_type=F32)
        for full, prev_ref, cur_ref in ((dk, dkp_ref, dkc_ref), (dv, dvp_ref, dvc_ref)):
            own = jnp.where(lane_lo((2 * BLOCK, 2 * SWA_DH)), full[:2 * BLOCK], full[2 * BLOCK:])
            prev_ref[...] = own[:BLOCK]
            cur_ref[...] = own[BLOCK:]

    kv_out = jax.ShapeDtypeStruct((SWA_KV, s, 2 * SWA_DH), F32)
    col_out = jax.ShapeDtypeStruct((SWA_KV, PROWS, 1), F32)
    return _pcall(
        body, comm=comm, name="swa_bwd", grid=(SWA_KV, nb),
        in_specs=[tok(PROWS), tok(PROWS), prev, cur, prev, cur, prev, cur, prev, cur, lse_spec, lse_spec, bias_spec,
                  col_spec, col_spec],
        out_specs=[tok(PROWS), cur, cur, cur, cur, bias_spec, col_spec, col_spec],
        out_shape=[jax.ShapeDtypeStruct((s, IN_EXT), BF16), kv_out, kv_out, kv_out, kv_out,
                   jax.ShapeDtypeStruct(bias.shape, F32), col_out, col_out],
        compiler_params=_cparams(("arbitrary", "arbitrary")))(
            src, dsrc, ktop, ktop, kbot, kbot, vtop, vtop, vbot, vbot, lses[0], lses[1], bias, sinks[0], sinks[1])


def _rope_slab(slab, table):
    t = slab * table
    return t + pltpu.roll(t, ROPE, 1)


def _low_lanes(v):
    lane = lax.broadcasted_iota(jnp.int32, v.shape, 1)
    return jnp.where(lane < ROPE, v, 0.0)


def _rms(xv, g):
    r = lax.rsqrt(jnp.mean(xv * xv, axis=-1, keepdims=True) + EPS)
    return xv * r, r


def _mla_prep(proj, gq, gkv, table):
    s = proj.shape[0]
    ts = _pick(s, ROW_TILE, 16)

    def body(p_ref, gq_ref, gkv_ref, t_ref, cq_ref, ckv_ref, kr_ref):
        xq, _ = _rms(p_ref[:, 0:Q_RANK], None)
        cq_ref[...] = (xq * gq_ref[...]).astype(BF16)
        xkv, _ = _rms(p_ref[:, Q_RANK:Q_RANK + KV_RANK], None)
        ckv_ref[...] = (xkv * gkv_ref[...]).astype(BF16)
        kr_ref[...] = _low_lanes(_rope_slab(p_ref[:, Q_RANK + KV_RANK:TAIL], t_ref[...]))

    return _pcall(
        body, name="mla_prep", grid=(s // ts,),
        in_specs=[pl.BlockSpec((ts, TAIL), lambda i: (i, TAIL0 // TAIL)), _vec(Q_RANK), _vec(KV_RANK), _row(ts, 2 * ROPE)],
        out_specs=[_row(ts, Q_RANK), _row(ts, KV_RANK), _row(ts, 2 * ROPE)],
        out_shape=[jax.ShapeDtypeStruct((s, Q_RANK), BF16), jax.ShapeDtypeStruct((s, KV_RANK), BF16),
                   jax.ShapeDtypeStruct((s, 2 * ROPE), F32)],
        compiler_params=_cparams(("parallel",)))(proj, gq, gkv, table)


def _mla_prep_bwd(proj, dcq, dckv, dkr, gq, gkv, table, dproj):
    s = proj.shape[0]
    ts = _pick(s, ROW_TILE, 16)

    def norm_bwd(xv, dy, g):
        xn, r = _rms(xv, None)
        dg = jnp.sum(dy * xn, axis=0, keepdims=True)
        dxn = dy * g
        return r * (dxn - xn * jnp.mean(dxn * xn, axis=-1, keepdims=True)), dg

    def body(p_ref, dcq_ref, dckv_ref, dkr_ref, gq_ref, gkv_ref, t_ref, _, dt_ref, dgq_ref, dgkv_ref):
        @pl.when(pl.program_id(0) == 0)
        def _():
            dgq_ref[...] = jnp.zeros_like(dgq_ref)
            dgkv_ref[...] = jnp.zeros_like(dgkv_ref)

        dxq, dgq = norm_bwd(p_ref[:, 0:Q_RANK], dcq_ref[...], gq_ref[...])
        dxkv, dgkv = norm_bwd(p_ref[:, Q_RANK:Q_RANK + KV_RANK], dckv_ref[...], gkv_ref[...])
        dgq_ref[...] += dgq
        dgkv_ref[...] += dgkv
        d = _low_lanes(dkr_ref[...])
        dslab = (d + pltpu.roll(d, ROPE, 1)) * t_ref[...]
        dt_ref[:, 0:Q_RANK] = dxq.astype(BF16)
        dt_ref[:, Q_RANK:Q_RANK + KV_RANK] = dxkv.astype(BF16)
        dt_ref[:, Q_RANK + KV_RANK:TAIL] = dslab.astype(BF16)

    return _pcall(
        body, name="mla_prep_bwd", grid=(s // ts,),
        in_specs=[pl.BlockSpec((ts, TAIL), lambda i: (i, TAIL0 // TAIL)), _row(ts, Q_RANK), _row(ts, KV_RANK),
                  _row(ts, 2 * ROPE), _vec(Q_RANK), _vec(KV_RANK), _row(ts, 2 * ROPE), pl.BlockSpec(memory_space=pl.ANY)],
        out_specs=[pl.BlockSpec((ts, TAIL), lambda i: (i, TAIL0 // TAIL)), _vec(Q_RANK), _vec(KV_RANK)],
        out_shape=[jax.ShapeDtypeStruct(dproj.shape, BF16), jax.ShapeDtypeStruct((1, Q_RANK), F32),
                   jax.ShapeDtypeStruct((1, KV_RANK), F32)],
        input_output_aliases={7: 0},
        compiler_params=_cparams(("arbitrary",)))(proj, dcq, dckv, dkr, gq, gkv, table, dproj)


def _head_specs(ts):
    tok = lambda w: pl.BlockSpec((ts, w), lambda h, i: (i, 0))
    head = lambda w: pl.BlockSpec((None, ts, w), lambda h, i: (h, i, 0))
    wgt = lambda r, c: pl.BlockSpec((None, r, c), lambda h, i: (h, 0, 0))
    return tok, head, wgt


def _mla_qkv(cq, ckv, kr, wq, wkv, table):
    s = cq.shape[0]
    ts = _pick(s, 4 * ROW_TILE, 16)
    tok, head, wgt = _head_specs(ts)

    def body(cq_ref, ckv_ref, kr_ref, wq_ref, wkv_ref, t_ref, q_ref, k_ref, v_ref):
        qf = lax.dot_general(cq_ref[...], wq_ref[...], _DIMS["nn"], preferred_element_type=F32)
        q_ref[:, 0:NOPE] = qf[:, 0:NOPE].astype(BF16)
        q_ref[:, NOPE:QW] = _rope_slab(qf[:, NOPE:QW], t_ref[...]).astype(BF16)
        kv = lax.dot_general(ckv_ref[...], wkv_ref[...], _DIMS["nn"], preferred_element_type=F32)
        k_ref[:, 0:NOPE] = kv[:, 0:NOPE].astype(BF16)
        k_ref[:, NOPE:QW] = kr_ref[...].astype(BF16)
        v_ref[:, 0:VDIM] = kv[:, NOPE:NOPE + VDIM].astype(BF16)
        lane = lax.broadcasted_iota(jnp.int32, (ts, VDIM), 1)
        v_ref[:, VDIM:2 * VDIM] = jnp.where(lane == 0, 1.0, 0.0).astype(BF16)

    return _pcall(
        body, name="mla_qkv", grid=(MLA_H, s // ts),
        in_specs=[tok(Q_RANK), tok(KV_RANK), tok(2 * ROPE), wgt(Q_RANK, QW), wgt(KV_RANK, NOPE + VDIM), tok(2 * ROPE)],
        out_specs=[head(QW), head(QW), head(2 * VDIM)],
        out_shape=[jax.ShapeDtypeStruct((MLA_H, s, QW), BF16), jax.ShapeDtypeStruct((MLA_H, s, QW), BF16),
                   jax.ShapeDtypeStruct((MLA_H, s, 2 * VDIM), BF16)],
        compiler_params=_cparams(("parallel", "parallel")))(cq, ckv, kr, wq, wkv, table)


def _mla_qkv_bwd(dq, dk, dv, cq, ckv, wq, wkv, table, comm=None):
    s = cq.shape[0]
    ts = _pick(s, 4 * ROW_TILE, 16)
    tok, head, wgt = _head_specs(ts)
    whole = lambda w: pl.BlockSpec((s, w), lambda h, i: (0, 0))

    def body(dq_ref, dk_ref, dv_ref, cq_ref, ckv_ref, wq_ref, wkv_ref, t_ref,
             dcq_ref, dckv_ref, dkr_ref, gwq_ref, gwkv_ref):
        h, i = pl.program_id(0), pl.program_id(1)
        rows = pl.ds(pl.multiple_of(i * ts, ts), ts)
        d = dq_ref[:, NOPE:QW]
        dslab = (d + pltpu.roll(d, ROPE, 1)) * t_ref[...]
        dqe = jnp.concatenate([dq_ref[:, 0:NOPE], dslab], axis=1).astype(BF16)
        dkv = jnp.concatenate([dk_ref[:, 0:NOPE], dv_ref[...]], axis=1).astype(BF16)
        dcq = lax.dot_general(dqe, wq_ref[...], _DIMS["nt"], preferred_element_type=F32)
        dckv = lax.dot_general(dkv, wkv_ref[...], _DIMS["nt"], preferred_element_type=F32)
        gwq = lax.dot_general(cq_ref[...], dqe, _DIMS["tn"], preferred_element_type=F32)
        gwkv = lax.dot_general(ckv_ref[...], dkv, _DIMS["tn"], preferred_element_type=F32)
        dkr = dk_ref[:, NOPE:QW].astype(F32)

        @pl.when(h == 0)
        def _():
            dcq_ref[rows, :] = dcq
            dckv_ref[rows, :] = dckv
            dkr_ref[rows, :] = dkr

        @pl.when(h > 0)
        def _():
            dcq_ref[rows, :] += dcq
            dckv_ref[rows, :] += dckv
            dkr_ref[rows, :] += dkr

        @pl.when(i == 0)
        def _():
            gwq_ref[...] = gwq
            gwkv_ref[...] = gwkv

        @pl.when(i > 0)
        def _():
            gwq_ref[...] += gwq
            gwkv_ref[...] += gwkv

    return _pcall(
        body, comm=comm, name="mla_qkv_bwd", grid=(MLA_H, s // ts),
        in_specs=[head(QW), head(QW), head(VDIM), tok(Q_RANK), tok(KV_RANK), wgt(Q_RANK, QW),
                  wgt(KV_RANK, NOPE + VDIM), tok(2 * ROPE)],
        out_specs=[whole(Q_RANK), whole(KV_RANK), whole(2 * ROPE), wgt(Q_RANK, QW), wgt(KV_RANK, NOPE + VDIM)],
        out_shape=[jax.ShapeDtypeStruct((s, Q_RANK), F32), jax.ShapeDtypeStruct((s, KV_RANK), F32),
                   jax.ShapeDtypeStruct((s, 2 * ROPE), F32), jax.ShapeDtypeStruct((MLA_H, Q_RANK, QW), F32),
                   jax.ShapeDtypeStruct((MLA_H, KV_RANK, NOPE + VDIM), F32)],
        compiler_params=_cparams(("arbitrary", "arbitrary")))(dq, dk, dv, cq, ckv, wq, wkv, table)


def _as_row(col):
    return jnp.broadcast_to(col, (col.shape[0], 128)).T[0:1, :]


def _diag_mask(t):
    return lax.broadcasted_iota(jnp.int32, (t, t), 1) <= lax.broadcasted_iota(jnp.int32, (t, t), 0)


def _mla_fwd(q, k, v, mix, comm=None):
    s = q.shape[1]
    t = _pick(s, ATT_T, 128)
    nt = s // t
    assert nt % 2 == 0
    hb = MLA_H

    def fold(p, u):
        first = u <= p
        return jnp.where(first, p, nt - 1 - p), jnp.where(first, u, u - p - 1)

    to_log2 = MLA_SCALE * math.log2(math.e)

    def body(q_ref, k_ref, v_ref, _, o_ref, oh_ref, lse_ref, m_ref, acc_ref):
        i, j = fold(pl.program_id(1), pl.program_id(2))

        @pl.when(j == 0)
        def _():
            m_ref[...] = jnp.full_like(m_ref, -jnp.inf)
            acc_ref[...] = jnp.zeros_like(acc_ref)

        def step(diagonal):
            for h in range(hb):
                sc = lax.dot_general(q_ref[h], k_ref[h], _DIMS["nt"], preferred_element_type=F32)
                if diagonal:
                    sc = jnp.where(_diag_mask(t), sc, -jnp.inf)
                m_old = m_ref[h]
                m_new = jnp.maximum(m_old, jnp.max(sc, axis=-1, keepdims=True))
                alpha = jnp.exp2((m_old - m_new) * to_log2)
                p = jnp.exp2((sc - m_new) * to_log2)
                acc_ref[h] = alpha * acc_ref[h] + lax.dot_general(p.astype(BF16), v_ref[h], _DIMS["nn"],
                                                                  preferred_element_type=F32)
                m_ref[h] = m_new

        @pl.when(j < i)
        def _():
            step(False)

        @pl.when(j == i)
        def _():
            step(True)
            for h in range(hb):
                den = acc_ref[h, :, VDIM:VDIM + 1]
                o = acc_ref[h, :, 0:VDIM] / den
                o_ref[:, h * VDIM:(h + 1) * VDIM] = o
                oh_ref[:, h * VDIM:(h + 1) * VDIM] = o.astype(BF16)
                lse_ref[h] = _as_row(m_ref[h] * MLA_SCALE + jnp.log(den))

    o_spec = pl.BlockSpec((t, hb * VDIM), lambda h, p, u: (fold(p, u)[0], h))
    first = (MIX - MLA_H * VDIM) // (hb * VDIM)
    mix_spec = pl.BlockSpec((t, hb * VDIM), lambda h, p, u: (fold(p, u)[0], first + h))
    return _pcall(
        body, comm=comm, name="mla_fwd", grid=(MLA_H // hb, nt // 2, nt + 1),
        in_specs=[pl.BlockSpec((hb, t, QW), lambda h, p, u: (h, fold(p, u)[0], 0)),
                  pl.BlockSpec((hb, t, QW), lambda h, p, u: (h, fold(p, u)[1], 0)),
                  pl.BlockSpec((hb, t, 2 * VDIM), lambda h, p, u: (h, fold(p, u)[1], 0)),
                  pl.BlockSpec(memory_space=pl.ANY)],
        out_specs=[o_spec, mix_spec, pl.BlockSpec((hb, 1, t), lambda h, p, u: (h, 0, fold(p, u)[0]))],
        out_shape=[jax.ShapeDtypeStruct((s, MLA_H * VDIM), F32), jax.ShapeDtypeStruct(mix.shape, BF16),
                   jax.ShapeDtypeStruct((MLA_H, 1, s), F32)],
        input_output_aliases={3: 1},
        scratch_shapes=[pltpu.VMEM((hb, t, 1), F32), pltpu.VMEM((hb, t, 2 * VDIM), F32)],
        compiler_params=_cparams(("parallel", "parallel", "arbitrary")))(q, k, v, mix)


def _mla_delta(dmix, o):
    s = o.shape[0]
    ts = _pick(s, 2 * ROW_TILE, 16)
    w = MLA_H * VDIM

    def body(d_ref, o_ref, out_ref):
        prod = d_ref[...] * o_ref[...]
        for h in range(MLA_H):
            out_ref[h] = _as_row(jnp.sum(prod[:, h * VDIM:(h + 1) * VDIM], axis=-1, keepdims=True))

    return _pcall(body, name="mla_delta", grid=(s // ts,),
                  in_specs=[pl.BlockSpec((ts, w), lambda i: (i, SWA_HEADS * SWA_DH // w)), pl.BlockSpec((ts, w), lambda i: (i, 0))],
                  out_specs=pl.BlockSpec((MLA_H, 1, ts), lambda i: (0, 0, i)),
                  out_shape=jax.ShapeDtypeStruct((MLA_H, 1, s), F32), compiler_params=_cparams(("parallel",)))(dmix, o)


def _mla_bwd(q, k, v, dmix, delta, lse, comm=None):
    s = q.shape[1]
    t = _pick(s, ATT_T, 128)
    nt = s // t
    assert nt % 2 == 0
    hb = 2 * MLA_HB
    o_blk0 = SWA_HEADS * SWA_DH // (hb * VDIM)

    def fold(p, u):
        first = u < nt - p
        return jnp.where(first, p, nt - 1 - p), jnp.where(first, p + u, u - 1)

    log2e = math.log2(math.e)

    def body(q_ref, k_ref, v_ref, do_ref, delta_ref, lse_ref, dq_ref, dk_ref, dv_ref, dk_acc, dv_acc):
        j, i = fold(pl.program_id(1), pl.program_id(2))
        rows = pl.ds(pl.multiple_of(i * t, t), t)

        @pl.when(i == j)
        def _():
            dk_acc[...] = jnp.zeros_like(dk_acc)
            dv_acc[...] = jnp.zeros_like(dv_acc)

        def step(diagonal):
            for h in range(hb):
                qv, kv_ = q_ref[h], k_ref[h]
                dob = do_ref[:, h * VDIM:(h + 1) * VDIM].astype(BF16)
                st = lax.dot_general(kv_, qv, _DIMS["nt"], preferred_element_type=F32)
                pt = jnp.exp2(st * (MLA_SCALE * log2e) - lse_ref[h] * log2e)
                if diagonal:
                    keep = lax.broadcasted_iota(jnp.int32, (t, t), 0) <= lax.broadcasted_iota(jnp.int32, (t, t), 1)
                    pt = jnp.where(keep, pt, 0.0)
                dpt = lax.dot_general(v_ref[h], dob, _DIMS["nt"], preferred_element_type=F32)
                dst = (pt * (dpt - delta_ref[h]) * MLA_SCALE).astype(BF16)
                dv_acc[h] += lax.dot_general(pt.astype(BF16), dob, _DIMS["nn"], preferred_element_type=F32)
                dk_acc[h] += lax.dot_general(dst, qv, _DIMS["nn"], preferred_element_type=F32)
                dqv = lax.dot_general(dst, kv_, _DIMS["tn"], preferred_element_type=F32)

                @pl.when(j == 0)
                def _():
                    dq_ref[h, rows, :] = dqv

                @pl.when(j > 0)
                def _():
                    dq_ref[h, rows, :] += dqv

        @pl.when(i > j)
        def _():
            step(False)

        @pl.when(i == j)
        def _():
            step(True)

        @pl.when(i == nt - 1)
        def _():
            dk_ref[...] = dk_acc[...].astype(BF16)
            dv_ref[...] = dv_acc[...].astype(BF16)

    qi = lambda h, p, u: (h, fold(p, u)[1], 0)
    kj = lambda h, p, u: (h, fold(p, u)[0], 0)
    row = pl.BlockSpec((hb, 1, t), lambda h, p, u: (h, 0, fold(p, u)[1]))
    return _pcall(
        body, comm=comm, name="mla_bwd", grid=(MLA_H // hb, nt // 2, nt + 1),
        in_specs=[pl.BlockSpec((hb, t, QW), qi), pl.BlockSpec((hb, t, QW), kj), pl.BlockSpec((hb, t, VDIM), kj),
                  pl.BlockSpec((t, hb * VDIM), lambda h, p, u: (fold(p, u)[1], o_blk0 + h)), row, row],
        out_specs=[pl.BlockSpec((hb, s, QW), lambda h, p, u: (h, 0, 0)), pl.BlockSpec((hb, t, QW), kj),
                   pl.BlockSpec((hb, t, VDIM), kj)],
        out_shape=[jax.ShapeDtypeStruct((MLA_H, s, QW), F32), jax.ShapeDtypeStruct((MLA_H, s, QW), BF16),
                   jax.ShapeDtypeStruct((MLA_H, s, VDIM), BF16)],
        scratch_shapes=[pltpu.VMEM((hb, t, QW), F32), pltpu.VMEM((hb, t, VDIM), F32)],
        compiler_params=_cparams(("arbitrary", "arbitrary", "arbitrary")))(q, k, v, dmix, delta, lse)


def _adamw(name, w, g, m, v, parts):
    r, c = w.shape
    n_parts = g.shape[0] if parts else 1
    tr = r if r * c <= ADAM_ELEMS else _pick(r, max(8, ADAM_ELEMS // c // 8 * 8), 8)
    c1 = 1.0 - ADAM_B1 ** ADAM_STEP
    c2 = 1.0 - ADAM_B2 ** ADAM_STEP

    def body(w_ref, g_ref, m_ref, v_ref, go_ref, d_ref, mo_ref, vo_ref):
        if parts:
            gv = g_ref[0].astype(F32)
            for j in range(1, n_parts):
                gv = gv + g_ref[j].astype(F32)
        else:
            gv = g_ref[...]
        mv = ADAM_B1 * m_ref[...] + (1.0 - ADAM_B1) * gv
        vv = ADAM_B2 * v_ref[...] + (1.0 - ADAM_B2) * (gv * gv)
        go_ref[...] = gv
        mo_ref[...] = mv
        vo_ref[...] = vv
        d_ref[...] = -ADAM_LR * ((mv / c1) / (jnp.sqrt(vv / c2) + ADAM_EPS) + ADAM_WD * w_ref[...])

    blk = pl.BlockSpec((tr, c), lambda i: (i, 0))
    g_spec = pl.BlockSpec((n_parts, tr, c), lambda i: (0, i, 0)) if parts else blk
    out = jax.ShapeDtypeStruct((r, c), F32)
    return _pcall(body, name=name, grid=(r // tr,), in_specs=[blk, g_spec, blk, blk], out_specs=[blk] * 4,
                  out_shape=[out] * 4, compiler_params=_cparams(("parallel",)))(w, g, m, v)


def _t5_bucket(dist):
    n = jnp.maximum(dist, 0)
    max_exact = REL_BUCKETS // 2
    nf = jnp.maximum(n, 1).astype(F32)
    large = max_exact + (jnp.log(nf / max_exact) / math.log(REL_MAX_DIST / max_exact)
                         * (REL_BUCKETS - max_exact)).astype(jnp.int32)
    return jnp.where(n < max_exact, n, jnp.minimum(large, REL_BUCKETS - 1))


def _swap_halves(w, r0):
    return jnp.concatenate([w[:, r0 + ROPE // 2:r0 + ROPE], w[:, r0:r0 + ROPE // 2]], axis=1)


def _fold_swapped(g, r0, width):
    sw = g[..., width:width + ROPE]
    half = ROPE // 2
    return jnp.concatenate([g[..., :r0], g[..., r0:r0 + half] + sw[..., half:], g[..., r0 + half:r0 + ROPE] + sw[..., :half],
                            g[..., r0 + ROPE:width]], axis=-1)


def kernel(x, c, w_mod, b_mod, attn_norm_g, w_in, swa_sinks, rel_bias, mla_q_norm_g, w_uq, mla_kv_norm_g, w_ukv, w_out, mlp_norm_g, w_ff1, w_ff2, final_norm_g, loss_target, m_w_mod, m_b_mod, m_attn_norm_g, m_w_in, m_swa_sinks, m_rel_bias, m_mla_q_norm_g, m_w_uq, m_mla_kv_norm_g, m_w_ukv, m_w_out, m_mlp_norm_g, m_w_ff1, m_w_ff2, m_final_norm_g, v_w_mod, v_b_mod, v_attn_norm_g, v_w_in, v_swa_sinks, v_rel_bias, v_mla_q_norm_g, v_w_uq, v_mla_kv_norm_g, v_w_ukv, v_w_out, v_mlp_norm_g, v_w_ff1, v_w_ff2, v_final_norm_g):
    s, d = x.shape[1], x.shape[2]
    ffs = w_ff1.shape[2]
    ff = ffs * NDEV
    nmod = w_mod.shape[2]
    me = 4 * lax.axis_index("x") + 2 * lax.axis_index("y") + lax.axis_index("c")
    x2d, tgt = x[0], loss_target[0]
    final_g = final_norm_g.reshape(1, d)

    w_in_l = jnp.concatenate([w_in[0], _swap_halves(w_in[0], OFF_KR)], axis=1).astype(BF16)
    w_uq_l = jnp.concatenate([w_uq[0], _swap_halves(w_uq[0], NOPE)], axis=1).astype(BF16)
    core = jnp.full((1, 128), lax.axis_index("c"), F32)
    (c_all,) = _exchange("gather_c", _Gather([c]))

    b_cols = lax.dynamic_slice(b_mod, (0, me * nmod), (1, nmod))
    act_all, mod_cols = _mod_fwd(c_all.reshape(NDEV, d), w_mod[0], b_cols)
    (mod_g,) = _exchange("gather_mod", _Gather([mod_cols]))
    mod = lax.dynamic_index_in_dim(mod_g, me, axis=1, keepdims=False).reshape(1, 6 * d)
    sh1, sc1, g1, sh2, sc2, g2 = [mod[:, i * d:(i + 1) * d] for i in range(6)]

    pos = jnp.arange(s, dtype=F32)
    inv_freq = ROPE_THETA ** (-jnp.arange(ROPE // 2, dtype=F32) / (ROPE // 2))
    ang = pos[:, None] * inv_freq[None, :]
    cos, sin = jnp.cos(ang), jnp.sin(ang)
    table = jnp.concatenate([cos, cos, -sin, sin], axis=1)
    q_loc = jnp.arange(BLOCK)[:, None]
    k_loc = jnp.arange(2 * BLOCK)[None, :]
    dist = q_loc + BLOCK - k_loc
    in_window = (dist >= 0) & (dist < BLOCK)
    onehot = (_t5_bucket(dist).reshape(-1, 1) == jnp.arange(REL_BUCKETS)[None, :]).astype(BF16)
    bias = _bias_expand(rel_bias.T, onehot.T).reshape(SWA_HEADS, BLOCK, 2 * BLOCK)
    bias = jnp.where(in_window[None], bias, -jnp.inf).reshape(SWA_KV, PAIRS, 2, BLOCK, 2 * BLOCK)
    bias = bias.transpose(0, 1, 3, 2, 4).reshape(SWA_KV, PROWS, PCOLS)
    sinks = jnp.broadcast_to(swa_sinks.reshape(SWA_KV, PAIRS, 1, 2), (SWA_KV, PAIRS, BLOCK, 2)).reshape(SWA_KV, PROWS, 2)
    sinks = (sinks[:, :, 0:1], sinks[:, :, 1:2])

    h1, w_in_g, w_uq_g, w_ukv_g = _norm_mod("norm1", x2d, attn_norm_g, sc1, sh1,
                                            comm=[_Gather([w_in_l, w_uq_l, w_ukv[0].astype(BF16)])])
    w_in_e = w_in_g.reshape(d, IN_EXT)

    def both_dtypes(acc, ex, outs):
        outs[0][...] = acc
        outs[1][...] = acc.astype(BF16)

    tmp = _pick(s, MM_TM // 2, 128)
    proj_blk = pl.BlockSpec((tmp, IN_EXT), lambda i, j, q: (i, 0))
    proj, proj_h = _mm("proj", h1, w_in_e, "nn", (s // tmp, 1, 1), pl.BlockSpec((tmp, d), lambda i, j, q: (i, 0)),
                       pl.BlockSpec((d, IN_EXT), lambda i, j, q: (0, 0)),
                       [jax.ShapeDtypeStruct((s, IN_EXT), F32), jax.ShapeDtypeStruct((s, IN_EXT), BF16)],
                       [proj_blk, proj_blk], (tmp, IN_EXT), both_dtypes)
    def diag_pair(tok):
        x = jnp.stack([tok[:, :SWA_DH], tok[:, SWA_DH:]])
        zero = jnp.zeros_like(x)
        return jnp.concatenate([x, zero], axis=2), jnp.concatenate([zero, x], axis=2)

    k_top, k_bot = diag_pair(proj_h[:, OFF_K:OFF_V])
    v_top, v_bot = diag_pair(proj_h[:, OFF_V:OFF_CQ])
    o_a, lse_a0, lse_a1, w_out_g = _swa2_fwd(proj_h, k_top, k_bot, v_top, v_bot, bias, sinks,
                                             comm=[_Gather([w_out[0].astype(BF16)])])
    w_out_f = w_out_g.reshape(MIX, d)

    cq, ckv, kr = _mla_prep(proj, mla_q_norm_g, mla_kv_norm_g, table)
    q_b, k_b, v_b = _mla_qkv(cq, ckv, kr, w_uq_g, w_ukv_g, table)
    o_b, mix, lse_b, w_ff1_g = _mla_fwd(q_b, k_b, v_b, o_a, comm=[_Gather([w_ff1[0].astype(BF16)])])

    tm, tn, tk = _pick(s, MM_TM, 128), _pick(d, MM_TN, 128), _pick(MIX, MM_TK, 128)
    row_blk = pl.BlockSpec((tm, tn), lambda i, j, q: (i, j))
    gate_blk = pl.BlockSpec((1, tn), lambda i, j, q: (0, j))

    def gated_residual(acc, ex, outs):
        outs[0][...] = acc
        outs[1][...] = ex[0][...] + ex[1][...] * acc

    y1, x2 = _mm("out_proj", mix, w_out_f, "nn", (s // tm, d // tn, MIX // tk),
                 pl.BlockSpec((tm, tk), lambda i, j, q: (i, q)), pl.BlockSpec((tk, tn), lambda i, j, q: (q, j)),
                 [jax.ShapeDtypeStruct((s, d), F32)] * 2, [row_blk, row_blk], (tm, tn), gated_residual,
                 extras=(x2d, g1), extra_specs=(row_blk, gate_blk))

    (h2,) = _norm_mod("norm2", x2, mlp_norm_g, sc2, sh2)
    tnf, tkd = _pick(ffs, MM_TN, 128), _pick(d, MM_TK, 128)
    rf = ffs // tnf
    ff_blk = pl.BlockSpec((tm, tnf), lambda i, j, q: (i, j))

    def relu_sq(acc, ex, outs):
        u = jnp.maximum(acc, 0.0)
        outs[0][...] = u
        outs[1][...] = (u * u).astype(BF16)

    u, uu, w_ff2_g = _mm("ff1", h2, w_ff1_g, "nn", (s // tm, ff // tnf, d // tkd),
                         pl.BlockSpec((tm, tkd), lambda i, j, q: (i, q)),
                         pl.BlockSpec((None, tkd, tnf), lambda i, j, q: (j // rf, q, j % rf)),
                         [jax.ShapeDtypeStruct((s, ff), F32), jax.ShapeDtypeStruct((s, ff), BF16)], [ff_blk, ff_blk],
                         (tm, tnf), relu_sq, comm=[_Gather([w_ff2[0].astype(BF16)])])
    w_ff2_f = w_ff2_g.reshape(ff, d)
    tkf = _pick(ff, MM_TK, 128)
    y2, x3 = _mm("ff2", uu, w_ff2_f, "nn", (s // tm, d // tn, ff // tkf),
                 pl.BlockSpec((tm, tkf), lambda i, j, q: (i, q)), pl.BlockSpec((tkf, tn), lambda i, j, q: (q, j)),
                 [jax.ShapeDtypeStruct((s, d), F32)] * 2, [row_blk, row_blk], (tm, tn), gated_residual,
                 extras=(x2, g2), extra_specs=(row_blk, gate_blk))

    dx3, dy2, loss_p, dgf, dg2 = _loss_head(x3, tgt, y2, final_g, g2)

    def relu_sq_bwd(acc, ex, outs):
        outs[0][...] = (acc * (2.0 * ex[0][...])).astype(BF16)

    tnf2 = _pick(ff, MM_TN, 128)
    du = _mm("ff2_dx", dy2, w_ff2_f, "nt", (s // tm, ff // tnf2, d // tkd),
             pl.BlockSpec((tm, tkd), lambda i, j, q: (i, q)), pl.BlockSpec((tnf2, tkd), lambda i, j, q: (j, q)),
             [jax.ShapeDtypeStruct((s, ff), BF16)], [pl.BlockSpec((tm, tnf2), lambda i, j, q: (i, j))],
             (tm, tnf2), relu_sq_bwd, extras=(u,), extra_specs=(pl.BlockSpec((tm, tnf2), lambda i, j, q: (i, j)),))[0]
    gw_ff2 = _mm_plain("ff2_dw", uu, dy2, "tn", ff, d, s, BF16)
    tmd, tks = _pick(d, MM_TM, 128), _pick(s, MM_TK, 128)
    gw_ff2 = gw_ff2.reshape(NDEV, ffs, d)
    dh2, s_ff2 = _mm("ff1_dx", du, w_ff1_g, "nt", (s // tm, d // tn, NDEV // 2),
                     pl.BlockSpec((tm, 2 * ffs), lambda i, j, q: (i, q)),
                     pl.BlockSpec((2, tn, ffs), lambda i, j, q: (q, j, 0)),
                     [jax.ShapeDtypeStruct((s, d), F32)], [row_blk], (tm, tn), _store(F32),
                     comm=[_PairSwap([gw_ff2])], b_parts=2)
    c_ff2 = _pair_sum("pair_ff2", gw_ff2, s_ff2, core)
    gw_ff1, p_ff2 = _mm("ff1_dw", h2, du, "tn", (d // tmd, ff // tnf, s // tks),
                        pl.BlockSpec((tks, tmd), lambda i, j, q: (q, i)), pl.BlockSpec((tks, tnf), lambda i, j, q: (q, j)),
                        [jax.ShapeDtypeStruct((NDEV, d, ffs), BF16)],
                        [pl.BlockSpec((None, tmd, tnf), lambda i, j, q: (j // rf, i, j % rf))], (tmd, tnf), _store(BF16),
                        comm=[_ChipScatter([c_ff2])])
    dx2, dy1, dsc2, dsh2, dgm, dg1, s_ff1 = _norm_mod_bwd("norm2_bwd", x2, dh2, dx3, mlp_norm_g, sc2, y1, g1,
                                                          comm=[_PairSwap([gw_ff1])])
    c_ff1 = _pair_sum("pair_ff1", gw_ff1, s_ff1, core)

    dmix = _mm_plain("out_proj_dx", dy1, w_out_f, "nt", s, MIX, d, F32)
    gw_out = _mm_plain("out_proj_dw", mix, dy1, "tn", MIX, d, s, BF16).reshape(NDEV, MIX // NDEV, d)

    delta_b = _mla_delta(dmix, o_b)
    dq_b, dk_b, dv_b, p_ff1, s_out = _mla_bwd(q_b, k_b, v_b, dmix, delta_b, lse_b,
                                              comm=[_ChipScatter([c_ff1]), _PairSwap([gw_out])])
    c_out = _pair_sum("pair_out", gw_out, s_out, core)
    dcq, dckv, dkr, gw_uq_e, gw_ukv, p_out = _mla_qkv_bwd(dq_b, dk_b, dv_b, cq, ckv, w_uq_g, w_ukv_g, table,
                                                          comm=[_ChipScatter([c_out])])
    gw_uq = _fold_swapped(gw_uq_e, NOPE, NOPE + ROPE).astype(BF16)
    gw_ukv = gw_ukv.astype(BF16)

    dproj, dkp, dkc, dvp, dvc, dbias, dsink0, dsink1, s_uq, s_ukv = _swa2_bwd(
        proj_h, dmix, k_top, k_bot, v_top, v_bot, (lse_a0, lse_a1), bias, sinks, comm=[_PairSwap([gw_uq, gw_ukv])])
    c_uq = _pair_sum("pair_uq", gw_uq, s_uq, core)
    c_ukv = _pair_sum("pair_ukv", gw_ukv, s_ukv, core)
    dproj, dgq, dgkv = _mla_prep_bwd(proj, dcq, dckv, dkr, mla_q_norm_g, mla_kv_norm_g, table, dproj)

    def band_grad(cur, prv):
        g = cur + jnp.concatenate([prv[:, BLOCK:], jnp.zeros_like(prv[:, :BLOCK])], axis=1)
        g = g[:, :, :SWA_DH] + g[:, :, SWA_DH:]
        return jnp.concatenate([g[0], g[1]], axis=1)

    dbias = dbias.reshape(SWA_KV, PAIRS, BLOCK, 2, 2 * BLOCK).transpose(0, 1, 3, 2, 4)
    dsink = jnp.stack([dsink0.reshape(SWA_KV, PAIRS, BLOCK), dsink1.reshape(SWA_KV, PAIRS, BLOCK)], axis=2)
    drel_t, dsinks = _bias_reduce(dbias.reshape(SWA_HEADS, BLOCK * 2 * BLOCK), onehot, dsink.reshape(SWA_HEADS, BLOCK))
    dkv = jnp.concatenate([band_grad(dkc, dkp), band_grad(dvc, dvp)], axis=1).astype(BF16)
    dproj = lax.dynamic_update_slice(dproj, dkv, (0, OFF_K))
    def fold_rotary(acc, ex, outs):
        slab = acc[:, TAIL - 2 * ROPE:TAIL]
        lane = lax.broadcasted_iota(jnp.int32, slab.shape, 1)
        folded = slab + jnp.where(lane < ROPE // 2, pltpu.roll(slab, ROPE // 2, 1), pltpu.roll(slab, 3 * ROPE // 2, 1))
        last = pl.program_id(1) == IN_EXT // TAIL - 1
        outs[0][:, 0:TAIL - 2 * ROPE] = acc[:, 0:TAIL - 2 * ROPE].astype(BF16)
        outs[0][:, TAIL - 2 * ROPE:TAIL] = jnp.where(last, folded, slab).astype(BF16)

    tmw, tks = _pick(d, MM_TM, 128), _pick(s, MM_TK, 128)
    gw_in = _mm("proj_dw", h1, dproj, "tn", (d // tmw, IN_EXT // TAIL, s // tks),
                pl.BlockSpec((tks, tmw), lambda i, j, q: (q, i)), pl.BlockSpec((tks, TAIL), lambda i, j, q: (q, j)),
                [jax.ShapeDtypeStruct((d, IN_COLS), BF16)], [pl.BlockSpec((tmw, TAIL), lambda i, j, q: (i, j))],
                (tmw, TAIL), fold_rotary)[0].reshape(NDEV, d // NDEV, IN_COLS)
    tkt = IN_EXT
    dh1, s_in, p_uq, p_ukv = _mm(
        "proj_dx", dproj, w_in_e, "nt", (s // tm, d // tn, IN_EXT // tkt),
        pl.BlockSpec((tm, tkt), lambda i, j, q: (i, q)), pl.BlockSpec((tn, tkt), lambda i, j, q: (j, q)),
        [jax.ShapeDtypeStruct((s, d), F32)], [row_blk], (tm, tn), _store(F32),
        comm=[_PairSwap([gw_in]), _ChipScatter([c_uq, c_ukv])])
    c_in = _pair_sum("pair_in", gw_in, s_in, core)
    gx, dsc1, dsh1, dga, p_in = _norm_mod_bwd("norm1_bwd", x2d, dh1, dx2, attn_norm_g, sc1,
                                              comm=[_ChipScatter([c_in])])

    small = [jnp.concatenate([dsh1, dsc1, dg1, dsh2, dsc2, dg2], axis=1), dga, dgm, dgf, dgq, dgkv,
             dsinks.reshape(1, SWA_HEADS), drel_t.T.reshape(1, REL_BUCKETS * SWA_HEADS)]
    n_small = sum(a.shape[1] for a in small)
    n_pad = -(n_small + 1) % 1024 + 1
    rows_small = (n_small + n_pad) // 128
    pad = jnp.zeros((1, n_pad), F32)
    pack = lambda parts, tail=pad: jnp.concatenate([p.reshape(1, -1) for p in parts] + [tail], axis=1).reshape(rows_small, 128)
    (small_g,) = _exchange("gather_small", _Gather([pack(small, jnp.concatenate([loss_p, pad[:, 1:]], axis=1))]))
    small_names = (b_mod, attn_norm_g, mlp_norm_g, final_norm_g, mla_q_norm_g, mla_kv_norm_g, swa_sinks, rel_bias)
    small_m = (m_b_mod, m_attn_norm_g, m_mlp_norm_g, m_final_norm_g, m_mla_q_norm_g, m_mla_kv_norm_g, m_swa_sinks, m_rel_bias)
    small_v = (v_b_mod, v_attn_norm_g, v_mlp_norm_g, v_final_norm_g, v_mla_q_norm_g, v_mla_kv_norm_g, v_swa_sinks, v_rel_bias)
    small_out = _adamw("adamw_small", pack(small_names), small_g, pack(small_m), pack(small_v), parts=True)

    def unpack(flat):
        flat = flat.reshape(1, -1)
        out, off = [], 0
        for a in small_names:
            out.append(flat[:, off:off + a.size].reshape(a.shape))
            off += a.size
        return out

    sg, sd, sm, sv = [unpack(o) for o in small_out]
    loss = small_out[0].reshape(-1)[n_small]

    dmod_cols = lax.dynamic_slice(small_g.reshape(NDEV, -1), (0, me * nmod), (NDEV, nmod))
    gw_mod = _mod_wgrad(act_all, dmod_cols)
    big = {"w_mod": _adamw("adamw_w_mod", w_mod[0], gw_mod, m_w_mod[0], v_w_mod[0], parts=False)}

    for name, w, p, m, v in (("w_in", w_in, p_in, m_w_in, v_w_in), ("w_uq", w_uq, p_uq, m_w_uq, v_w_uq),
                             ("w_ukv", w_ukv, p_ukv, m_w_ukv, v_w_ukv), ("w_out", w_out, p_out, m_w_out, v_w_out),
                             ("w_ff1", w_ff1, p_ff1, m_w_ff1, v_w_ff1), ("w_ff2", w_ff2, p_ff2, m_w_ff2, v_w_ff2)):
        big[name] = _adamw("adamw_" + name, w[0], p, m[0], v[0], parts=True)

    order = ("w_mod", "b_mod", "attn_norm_g", "w_in", "swa_sinks", "rel_bias", "mla_q_norm_g", "w_uq", "mla_kv_norm_g",
             "w_ukv", "w_out", "mlp_norm_g", "w_ff1", "w_ff2", "final_norm_g")
    small_idx = {"b_mod": 0, "attn_norm_g": 1, "mlp_norm_g": 2, "final_norm_g": 3, "mla_q_norm_g": 4,
                 "mla_kv_norm_g": 5, "swa_sinks": 6, "rel_bias": 7}
    outs = []
    for kind, small_list in enumerate((sg, sd, sm, sv)):
        for name in order:
            outs.append(small_list[small_idx[name]] if name in small_idx else big[name][kind][None])
    return (loss, gx[None], *outs)
```

```python
import functools
import math

import jax
import jax.numpy as jnp
from jax import lax
from jax.experimental import pallas as pl
from jax.experimental.pallas import tpu as pltpu

F32 = jnp.float32
BF16 = jnp.bfloat16

NDEV = 8
EPS = 1e-6
BLOCK = 128
SWA_HEADS, SWA_KV, SWA_DH, SWA_GROUP = 16, 2, 64, 8
REL_BUCKETS, REL_MAX_DIST = 32, 128
MLA_H, Q_RANK, KV_RANK, NOPE, ROPE, VDIM = 8, 384, 128, 128, 64, 128
ROPE_THETA = 10000.0
OFF_K, OFF_V, OFF_CQ, OFF_CKV, OFF_KR, IN_COLS = 1024, 1152, 1280, 1664, 1792, 1856
IN_EXT = IN_COLS + ROPE
TAIL0, TAIL = OFF_CQ, IN_EXT - OFF_CQ
QW = NOPE + 2 * ROPE
MIX = SWA_HEADS * SWA_DH + MLA_H * VDIM
MLA_SCALE = (NOPE + ROPE) ** -0.5
SWA_SCALE = SWA_DH ** -0.5

ADAM_LR, ADAM_B1, ADAM_B2, ADAM_EPS, ADAM_WD, ADAM_STEP = 0.001, 0.9, 0.999, 1e-08, 0.01, 10

VMEM_LIMIT = 52 * 1024 * 1024
ROW_TILE = 256
MM_TM, MM_TN, MM_TK = 1024, 1024, 2048
ATT_T = 512
MLA_HB = 2
ADAM_ELEMS = 256 * 1024


MESH_ID = pl.DeviceIdType.MESH


def _place():
    x, y, c = lax.axis_index("x"), lax.axis_index("y"), lax.axis_index("c")
    return x, y, c, 2 * x + y


def _chip(x, y, k):
    return (1 - x if k & 2 else x, 1 - y if k & 1 else y)


def _dma_sems(*counts):
    return [pltpu.SemaphoreType.DMA((n,)) for n in counts]


class _Gather:
    def __init__(self, arrays):
        self.arrays = list(arrays)
        n = len(self.arrays)
        self.out_shape = [jax.ShapeDtypeStruct((NDEV,) + a.shape, a.dtype) for a in self.arrays]
        self.sems = _dma_sems(7 * n, 7 * n, n)

    def _copy(self, sems, a, k, src, dst, to):
        return pltpu.make_async_remote_copy(src_ref=src, dst_ref=dst, send_sem=sems[0].at[7 * a + k],
                                            recv_sem=sems[1].at[7 * a + k], device_id=to, device_id_type=MESH_ID)

    def start(self, ins, outs, sems):
        x, y, c, q = _place()
        me = 2 * q + c
        for a in range(len(ins)):
            pltpu.make_async_copy(ins[a], outs[a].at[me], sems[2].at[a]).start()
            self._copy(sems, a, 0, ins[a], outs[a].at[me], (x, y, 1 - c)).start()
            for k in (1, 2, 3):
                self._copy(sems, a, k, ins[a], outs[a].at[me], (*_chip(x, y, k), c)).start()

    def relay(self, ins, outs, sems):
        x, y, c, q = _place()
        sib = (x, y, 1 - c)
        for k in (1, 2, 3):
            for a in range(len(ins)):
                blk = outs[a].at[2 * (q ^ k) + c]
                self._copy(sems, a, k, ins[a], blk, (*_chip(x, y, k), c)).wait_recv()
                self._copy(sems, a, 3 + k, blk, blk, sib).start()

    def finish(self, ins, outs, sems):
        x, y, c, q = _place()
        me, sib = 2 * q + c, (x, y, 1 - c)
        n = len(ins)
        for a in range(n):
            self._copy(sems, a, 0, ins[a], outs[a].at[2 * q + 1 - c], sib).wait_recv()
            for k in (1, 2, 3):
                blk = outs[a].at[2 * (q ^ k) + 1 - c]
                self._copy(sems, a, 3 + k, blk, blk, sib).wait_recv()
        for a in range(n):
            for k in range(7):
                self._copy(sems, a, k, ins[a], outs[a].at[me], sib).wait_send()
            pltpu.make_async_copy(ins[a], outs[a].at[me], sems[2].at[a]).wait()


class _PairSwap:
    def __init__(self, arrays):
        self.arrays = list(arrays)
        n = len(self.arrays)
        self.out_shape = [jax.ShapeDtypeStruct((NDEV // 2,) + a.shape[1:], a.dtype) for a in self.arrays]
        self.sems = _dma_sems(4 * n, 4 * n)

    def _copy(self, sems, a, p, src, dst, to):
        return pltpu.make_async_remote_copy(src_ref=src, dst_ref=dst, send_sem=sems[0].at[4 * a + p],
                                            recv_sem=sems[1].at[4 * a + p], device_id=to, device_id_type=MESH_ID)

    def start(self, ins, outs, sems):
        x, y, c, _ = _place()
        for a in range(len(ins)):
            for p in range(4):
                self._copy(sems, a, p, ins[a].at[2 * p + 1 - c], outs[a].at[p], (x, y, 1 - c)).start()

    def finish(self, ins, outs, sems):
        x, y, c, _ = _place()
        for a in range(len(ins)):
            for p in range(4):
                cp = self._copy(sems, a, p, ins[a].at[2 * p + 1 - c], outs[a].at[p], (x, y, 1 - c))
                cp.wait_recv()
                cp.wait_send()


class _ChipScatter:
    def __init__(self, arrays):
        self.arrays = list(arrays)
        n = len(self.arrays)
        self.out_shape = [jax.ShapeDtypeStruct(a.shape, a.dtype) for a in self.arrays]
        self.sems = _dma_sems(3 * n, 3 * n, n)

    def _copy(self, sems, a, k, src, dst, to):
        return pltpu.make_async_remote_copy(src_ref=src, dst_ref=dst, send_sem=sems[0].at[3 * a + k - 1],
                                            recv_sem=sems[1].at[3 * a + k - 1], device_id=to, device_id_type=MESH_ID)

    def start(self, ins, outs, sems):
        x, y, c, q = _place()
        for a in range(len(ins)):
            pltpu.make_async_copy(ins[a].at[q], outs[a].at[q], sems[2].at[a]).start()
            for k in (1, 2, 3):
                self._copy(sems, a, k, ins[a].at[q ^ k], outs[a].at[q], (*_chip(x, y, k), c)).start()

    def finish(self, ins, outs, sems):
        x, y, c, q = _place()
        for a in range(len(ins)):
            for k in (1, 2, 3):
                cp = self._copy(sems, a, k, ins[a].at[q ^ k], outs[a].at[q ^ k], (*_chip(x, y, k), c))
                cp.wait_recv()
                cp.wait_send()
            pltpu.make_async_copy(ins[a].at[q], outs[a].at[q], sems[2].at[a]).wait()


def _call(body, **kw):
    return pl.pallas_call(body, **kw)


def _pcall(body, comm=None, **kw):
    if not comm:
        return _call(body, **kw)
    grid = kw["grid"]
    in_specs, out_specs, out_shape = list(kw["in_specs"]), list(kw["out_specs"]), list(kw["out_shape"])
    scratch = list(kw.get("scratch_shapes", ()))
    n_in, n_out, n_scr = len(in_specs), len(out_shape), len(scratch)
    n_cin = [len(j.arrays) for j in comm]
    n_sem = [len(j.sems) for j in comm]
    n = sum(n_cin)
    hbm = pl.BlockSpec(memory_space=pltpu.HBM)

    def carried(*refs):
        ins, cins = refs[:n_in], refs[n_in:n_in + n]
        outs, couts = refs[n_in + n:n_in + n + n_out], refs[n_in + n + n_out:n_in + 2 * n + n_out]
        scr, sems = refs[n_in + 2 * n + n_out:n_in + 2 * n + n_out + n_scr], refs[n_in + 2 * n + n_out + n_scr:]
        ids = [pl.program_id(ax) for ax in range(len(grid))]
        first = functools.reduce(jnp.logical_and, [i == 0 for i in ids])
        last = functools.reduce(jnp.logical_and, [i == g - 1 for i, g in zip(ids, grid)])

        def each(method):
            ai = si = 0
            for job, na, ns in zip(comm, n_cin, n_sem):
                if hasattr(job, method):
                    getattr(job, method)(cins[ai:ai + na], couts[ai:ai + na], sems[si:si + ns])
                ai, si = ai + na, si + ns

        @pl.when(first)
        def _():
            each("start")

        steps = math.prod(grid)
        if steps > 1:
            at, rest = [], steps - 2
            for g in reversed(grid):
                at.append(rest % g)
                rest //= g
            before_last = functools.reduce(jnp.logical_and, [i == a for i, a in zip(ids, reversed(at))])

            @pl.when(before_last)
            def _():
                each("relay")

        body(*ins, *outs, *scr)

        @pl.when(last)
        def _():
            if steps == 1:
                each("relay")
            each("finish")

    kw.update(in_specs=in_specs + [hbm] * n, out_specs=out_specs + [hbm] * n,
              out_shape=out_shape + [o for j in comm for o in j.out_shape],
              scratch_shapes=scratch + [sm for j in comm for sm in j.sems],
              compiler_params=_cparams(("arbitrary",) * len(grid)))
    call = _call(carried, **kw)
    return lambda *args: call(*args, *[a for j in comm for a in j.arrays])


def _cparams(sem):
    return pltpu.CompilerParams(dimension_semantics=sem, vmem_limit_bytes=VMEM_LIMIT)


def _pick(n, pref, align):
    if n <= pref:
        return n
    t = (pref // align) * align
    while t >= align:
        if n % t == 0:
            return t
        t -= align
    return n


def _split3(x):
    a = x.astype(BF16)
    r = x - a.astype(F32)
    b = r.astype(BF16)
    c = (r - b.astype(F32)).astype(BF16)
    return a, b, c


def _exchange(name, job):
    n = len(job.arrays)

    def body(*refs):
        ins, outs, sems = refs[:n], refs[n:2 * n], refs[2 * n:]
        job.start(ins, outs, sems)
        if hasattr(job, "relay"):
            job.relay(ins, outs, sems)
        job.finish(ins, outs, sems)

    hbm = pl.BlockSpec(memory_space=pltpu.HBM)
    return _call(body, name=name, out_shape=job.out_shape, in_specs=[hbm] * n, out_specs=[hbm] * n,
                 scratch_shapes=job.sems)(*job.arrays)


def _pair_sum(name, g, r, core):
    _, rr, cc = g.shape
    tr = rr if rr * cc <= 4 * ADAM_ELEMS else _pick(rr, max(16, 4 * ADAM_ELEMS // cc // 16 * 16), 16)

    def body(g_ref, r_ref, c_ref, o_ref):
        north = c_ref[:, 0:1] > 0.5
        mine = jnp.where(north, g_ref[1].astype(F32), g_ref[0].astype(F32))
        o_ref[...] = (mine + r_ref[...].astype(F32)).astype(o_ref.dtype)

    return _pcall(
        body, name=name, grid=(NDEV // 2, rr // tr),
        in_specs=[pl.BlockSpec((None, 2, tr, cc), lambda p, i: (p, 0, i, 0)),
                  pl.BlockSpec((None, tr, cc), lambda p, i: (p, i, 0)), pl.BlockSpec((1, 128), lambda p, i: (0, 0))],
        out_specs=pl.BlockSpec((None, tr, cc), lambda p, i: (p, i, 0)),
        out_shape=jax.ShapeDtypeStruct((NDEV // 2, rr, cc), g.dtype),
        compiler_params=_cparams(("parallel", "parallel")))(g.reshape(NDEV // 2, 2, rr, cc), r, core)


_DIMS = {"nn": (((1,), (0,)), ((), ())), "nt": (((1,), (1,)), ((), ())), "tn": (((0,), (0,)), ((), ()))}


def _mm(name, a, b, kind, grid, a_spec, b_spec, out_shape, out_specs, acc_shape, epilogue,
        extras=(), extra_specs=(), comm=None, b_parts=1):
    nk, ne, no = grid[2], len(extras), len(out_shape)

    def body(*refs):
        a_ref, b_ref = refs[0], refs[1]
        ex, outs = refs[2:2 + ne], refs[2 + ne:2 + ne + no]
        if b_parts == 1:
            part = lax.dot_general(a_ref[...].astype(BF16), b_ref[...].astype(BF16), _DIMS[kind],
                                   preferred_element_type=F32)
        else:
            kp = a_ref.shape[1] // b_parts
            part = sum(lax.dot_general(a_ref[:, p * kp:(p + 1) * kp].astype(BF16), b_ref[p].astype(BF16), _DIMS[kind],
                                       preferred_element_type=F32) for p in range(b_parts))
        if nk == 1:
            epilogue(part, ex, outs)
            return
        acc = refs[-1]
        k = pl.program_id(2)

        @pl.when(k == 0)
        def _():
            acc[...] = part

        @pl.when(jnp.logical_and(k > 0, k < nk - 1))
        def _():
            acc[...] += part

        @pl.when(k == nk - 1)
        def _():
            epilogue(acc[...] + part, ex, outs)

    return _pcall(
        body, comm=comm, name=name, grid=grid, in_specs=[a_spec, b_spec, *extra_specs], out_specs=out_specs,
        out_shape=out_shape, scratch_shapes=[pltpu.VMEM(acc_shape, F32)] if nk > 1 else [],
        compiler_params=_cparams(("parallel", "parallel", "arbitrary")),
    )(a, b, *extras)


def _store(dtype):
    def epi(acc, ex, outs):
        outs[0][...] = acc.astype(dtype)
    return epi


def _mm_plain(name, a, b, kind, m, n, k, out_dtype, tm=None, tn=None, tk=None):
    tm = _pick(m, tm or MM_TM, 128)
    tn = _pick(n, tn or MM_TN, 128)
    tk = _pick(k, tk or MM_TK, 128)
    a_spec = pl.BlockSpec((tk, tm), lambda i, j, q: (q, i)) if kind == "tn" else pl.BlockSpec((tm, tk), lambda i, j, q: (i, q))
    b_spec = pl.BlockSpec((tn, tk), lambda i, j, q: (j, q)) if kind == "nt" else pl.BlockSpec((tk, tn), lambda i, j, q: (q, j))
    return _mm(name, a, b, kind, (m // tm, n // tn, k // tk), a_spec, b_spec,
               [jax.ShapeDtypeStruct((m, n), out_dtype)], [pl.BlockSpec((tm, tn), lambda i, j, q: (i, j))],
               (tm, tn), _store(out_dtype))[0]


def _row(ts, d):
    return pl.BlockSpec((ts, d), lambda i: (i, 0))


def _vec(d):
    return pl.BlockSpec((1, d), lambda i: (0, 0))


def _norm_mod(name, x, gain, sc, sh, comm=None):
    s, d = x.shape
    ts = _pick(s, ROW_TILE, 16)

    def body(x_ref, g_ref, sc_ref, sh_ref, h_ref):
        xv = x_ref[...]
        r = lax.rsqrt(jnp.mean(xv * xv, axis=-1, keepdims=True) + EPS)
        h_ref[...] = ((xv * r) * g_ref[...] * (1.0 + sc_ref[...]) + sh_ref[...]).astype(BF16)

    return _pcall(body, comm=comm, name=name, grid=(s // ts,), in_specs=[_row(ts, d), _vec(d), _vec(d), _vec(d)],
                  out_specs=[_row(ts, d)], out_shape=[jax.ShapeDtypeStruct((s, d), BF16)],
                  compiler_params=_cparams(("parallel",)))(x, gain, sc, sh)


def _loss_head(x3, tgt, y2, gf, g2):
    s, d = x3.shape
    ts = _pick(s, ROW_TILE, 16)

    def body(x_ref, t_ref, y_ref, gf_ref, g2_ref, dx_ref, dy_ref, loss_ref, dgf_ref, dg2_ref):
        @pl.when(pl.program_id(0) == 0)
        def _():
            loss_ref[...] = jnp.zeros_like(loss_ref)
            dgf_ref[...] = jnp.zeros_like(dgf_ref)
            dg2_ref[...] = jnp.zeros_like(dg2_ref)

        xv = x_ref[...]
        r = lax.rsqrt(jnp.mean(xv * xv, axis=-1, keepdims=True) + EPS)
        xn = xv * r
        err = xn * gf_ref[...] - t_ref[...]
        loss_ref[...] += 0.5 * jnp.sum(jnp.mean(err * err, axis=-1, keepdims=True), axis=0, keepdims=True)
        dout = err * (1.0 / d)
        dgf_ref[...] += jnp.sum(dout * xn, axis=0, keepdims=True)
        dxn = dout * gf_ref[...]
        dx = r * (dxn - xn * jnp.mean(dxn * xn, axis=-1, keepdims=True))
        dx_ref[...] = dx
        dy_ref[...] = (dx * g2_ref[...]).astype(BF16)
        dg2_ref[...] += jnp.sum(dx * y_ref[...], axis=0, keepdims=True)

    one = pl.BlockSpec((1, 1), lambda i: (0, 0))
    return _pcall(
        body, name="loss_head", grid=(s // ts,),
        in_specs=[_row(ts, d), _row(ts, d), _row(ts, d), _vec(d), _vec(d)],
        out_specs=[_row(ts, d), _row(ts, d), one, _vec(d), _vec(d)],
        out_shape=[jax.ShapeDtypeStruct((s, d), F32), jax.ShapeDtypeStruct((s, d), BF16),
                   jax.ShapeDtypeStruct((1, 1), F32), jax.ShapeDtypeStruct((1, d), F32),
                   jax.ShapeDtypeStruct((1, d), F32)],
        compiler_params=_cparams(("arbitrary",)))(x3, tgt, y2, gf, g2)


def _norm_mod_bwd(name, x, dh, dres, gain, sc, y_prev=None, gate=None, comm=None):
    s, d = x.shape
    ts = _pick(s, ROW_TILE, 16)
    gated = y_prev is not None

    def body(*refs):
        if gated:
            x_ref, dh_ref, dr_ref, g_ref, sc_ref, y_ref, gt_ref, dx_ref, dy_ref, dsc_ref, dsh_ref, dg_ref, dgt_ref = refs
        else:
            x_ref, dh_ref, dr_ref, g_ref, sc_ref, dx_ref, dsc_ref, dsh_ref, dg_ref = refs

        @pl.when(pl.program_id(0) == 0)
        def _():
            dsc_ref[...] = jnp.zeros_like(dsc_ref)
            dsh_ref[...] = jnp.zeros_like(dsh_ref)
            dg_ref[...] = jnp.zeros_like(dg_ref)
            if gated:
                dgt_ref[...] = jnp.zeros_like(dgt_ref)

        xv, dhv = x_ref[...], dh_ref[...]
        r = lax.rsqrt(jnp.mean(xv * xv, axis=-1, keepdims=True) + EPS)
        xn = xv * r
        dsc_ref[...] += jnp.sum(dhv * (xn * g_ref[...]), axis=0, keepdims=True)
        dsh_ref[...] += jnp.sum(dhv, axis=0, keepdims=True)
        da = dhv * (1.0 + sc_ref[...])
        dg_ref[...] += jnp.sum(da * xn, axis=0, keepdims=True)
        dxn = da * g_ref[...]
        dx = dr_ref[...] + r * (dxn - xn * jnp.mean(dxn * xn, axis=-1, keepdims=True))
        dx_ref[...] = dx
        if gated:
            dy_ref[...] = (dx * gt_ref[...]).astype(BF16)
            dgt_ref[...] += jnp.sum(dx * y_ref[...], axis=0, keepdims=True)

    ins = [x, dh, dres, gain, sc] + ([y_prev, gate] if gated else [])
    in_specs = [_row(ts, d)] * 3 + [_vec(d)] * 2 + ([_row(ts, d), _vec(d)] if gated else [])
    vec_out = jax.ShapeDtypeStruct((1, d), F32)
    out_shape = [jax.ShapeDtypeStruct((s, d), F32)] + ([jax.ShapeDtypeStruct((s, d), BF16)] if gated else [])
    out_shape += [vec_out] * (4 if gated else 3)
    out_specs = [_row(ts, d)] * (2 if gated else 1) + [_vec(d)] * (4 if gated else 3)
    return _pcall(body, comm=comm, name=name, grid=(s // ts,), in_specs=in_specs, out_specs=out_specs,
                  out_shape=out_shape, compiler_params=_cparams(("arbitrary",)))(*ins)


def _dot3(a, b, dims):
    a1, a2, _ = _split3(a)
    b1, b2, _ = _split3(b)
    dot = functools.partial(lax.dot_general, dimension_numbers=dims, preferred_element_type=F32)
    return dot(a1, b1) + (dot(a1, b2) + dot(a2, b1))


def _mod_fwd(c_all, w, b_cols, comm=None):
    nb, d = c_all.shape
    n = w.shape[1]
    tk = _pick(d, 512, 128)
    nk = d // tk

    def body(c_ref, w_ref, b_ref, act_ref, out_ref):
        k = pl.program_id(0)
        cv = c_ref[...]
        act = cv * (1.0 / (1.0 + jnp.exp(-cv)))
        act_ref[...] = act

        @pl.when(k == 0)
        def _():
            out_ref[...] = jnp.broadcast_to(b_ref[...], out_ref.shape)

        out_ref[...] += _dot3(act, w_ref[...], _DIMS["nn"])

    return _pcall(
        body, comm=comm, name="mod_fwd", grid=(nk,),
        in_specs=[pl.BlockSpec((nb, tk), lambda k: (0, k)), pl.BlockSpec((tk, n), lambda k: (k, 0)),
                  pl.BlockSpec((1, n), lambda k: (0, 0))],
        out_specs=[pl.BlockSpec((nb, tk), lambda k: (0, k)), pl.BlockSpec((nb, n), lambda k: (0, 0))],
        out_shape=[jax.ShapeDtypeStruct((nb, d), F32), jax.ShapeDtypeStruct((nb, n), F32)],
        compiler_params=_cparams(("arbitrary",)))(c_all, w, b_cols)


def _mod_wgrad(act_all, dmod_cols):
    nb, d = act_all.shape
    n = dmod_cols.shape[1]
    tm = _pick(d, 512, 128)

    def body(a_ref, d_ref, o_ref):
        o_ref[...] = _dot3(a_ref[...], d_ref[...], _DIMS["tn"])

    return _pcall(
        body, name="mod_wgrad", grid=(d // tm,),
        in_specs=[pl.BlockSpec((nb, tm), lambda i: (0, i)), pl.BlockSpec((nb, n), lambda i: (0, 0))],
        out_specs=pl.BlockSpec((tm, n), lambda i: (i, 0)), out_shape=jax.ShapeDtypeStruct((d, n), F32),
        compiler_params=_cparams(("parallel",)))(act_all, dmod_cols)


def _bias_expand(rel_t, onehot_t):
    h, _ = rel_t.shape
    n = onehot_t.shape[1]

    def body(r_ref, o_ref, out_ref):
        a, b, c = _split3(r_ref[...])
        dot = functools.partial(lax.dot_general, dimension_numbers=_DIMS["nn"], preferred_element_type=F32)
        oh = o_ref[...]
        out_ref[...] = dot(a, oh) + (dot(b, oh) + dot(c, oh))

    full = lambda shp: pl.BlockSpec(shp, lambda: (0,) * len(shp))
    return _pcall(body, name="bias_expand", in_specs=[full(rel_t.shape), full(onehot_t.shape)],
                  out_specs=full((h, n)), out_shape=jax.ShapeDtypeStruct((h, n), F32),
                  compiler_params=pltpu.CompilerParams(vmem_limit_bytes=VMEM_LIMIT))(rel_t, onehot_t)


def _bias_reduce(dbias, onehot, dsink_rows):
    h, n = dbias.shape

    def body(d_ref, o_ref, s_ref, out_ref, so_ref):
        a, b, c = _split3(d_ref[...])
        dot = functools.partial(lax.dot_general, dimension_numbers=_DIMS["nn"], preferred_element_type=F32)
        oh = o_ref[...]
        out_ref[...] = dot(a, oh) + (dot(b, oh) + dot(c, oh))
        so_ref[...] = jnp.sum(s_ref[...], axis=-1, keepdims=True)

    full = lambda shp: pl.BlockSpec(shp, lambda: (0,) * len(shp))
    return _pcall(body, name="bias_reduce", in_specs=[full(dbias.shape), full(onehot.shape), full(dsink_rows.shape)],
                  out_specs=[full((h, REL_BUCKETS)), full((h, 1))],
                  out_shape=[jax.ShapeDtypeStruct((h, REL_BUCKETS), F32), jax.ShapeDtypeStruct((h, 1), F32)],
                  compiler_params=pltpu.CompilerParams(vmem_limit_bytes=VMEM_LIMIT))(dbias, onehot, dsink_rows)


PAIRS = SWA_GROUP // 2
PROWS = PAIRS * BLOCK
PCOLS = 2 * 2 * BLOCK


def _swa2_specs():
    tok = lambda width: pl.BlockSpec((BLOCK, width), lambda g, n: (n, g))
    prev = pl.BlockSpec((None, BLOCK, 2 * SWA_DH), lambda g, n: (g, jnp.maximum(n - 1, 0), 0))
    cur = pl.BlockSpec((None, BLOCK, 2 * SWA_DH), lambda g, n: (g, n, 0))
    bias_spec = pl.BlockSpec((None, PROWS, PCOLS), lambda g, n: (g, 0, 0))
    col_spec = pl.BlockSpec((None, PROWS, 1), lambda g, n: (g, 0, 0))
    lse_spec = pl.BlockSpec((None, None, PROWS, 1), lambda g, n: (g, n, 0, 0))
    return tok, prev, cur, bias_spec, col_spec, lse_spec


def _stack_pairs(blk):
    return jnp.concatenate([blk[:, p * 2 * SWA_DH:(p + 1) * 2 * SWA_DH] for p in range(PAIRS)], axis=0)


def _band(tp, tc, bp, bc):
    return jnp.concatenate([tp[...], tc[...], bp[...], bc[...]], axis=0)


def _swa2_scores(q_ref, kd, bias_ref, n):
    q2 = _stack_pairs(q_ref[...])
    s2 = lax.dot_general(q2, kd, _DIMS["nt"], preferred_element_type=F32) * SWA_SCALE + bias_ref[...]
    col = lax.broadcasted_iota(jnp.int32, s2.shape, 1)
    before_start = jnp.logical_and(n == 0, (col & (2 * BLOCK - 1)) < BLOCK)
    return q2, jnp.where(before_start, -jnp.inf, s2)


def _swa2_fwd(src, ktop, kbot, vtop, vbot, bias, sinks, comm=None):
    s = src.shape[0]
    nb = s // BLOCK
    tok, prev, cur, bias_spec, col_spec, lse_spec = _swa2_specs()

    def body(q_ref, ktp, ktc, kbp, kbc, vtp, vtc, vbp, vbc, bias_ref, sa_ref, sb_ref, o_ref, la_ref, lb_ref):
        n = pl.program_id(1)
        _, s2 = _swa2_scores(q_ref, _band(ktp, ktc, kbp, kbc), bias_ref, n)
        row = lax.broadcasted_iota(jnp.int32, (PCOLS, 2 * SWA_DH), 0)
        lane = lax.broadcasted_iota(jnp.int32, (PCOLS, 2 * SWA_DH), 1)
        ones = jnp.where(lane == row // (2 * BLOCK), 1.0, 0.0).astype(BF16)
        ps, ms, sinks_ = [], [], []
        for half, sink_ref in enumerate((sa_ref, sb_ref)):
            sc = s2[:, half * 2 * BLOCK:(half + 1) * 2 * BLOCK]
            m = jnp.maximum(jnp.max(sc, axis=-1, keepdims=True), sink_ref[...])
            ps.append(jnp.exp(sc - m).astype(BF16))
            ms.append(m)
        acc = lax.dot_general(jnp.concatenate(ps, axis=1), jnp.concatenate([_band(vtp, vtc, vbp, vbc), ones], axis=1),
                              _DIMS["nn"], preferred_element_type=F32)
        dens = []
        for half, (sink_ref, lse_ref) in enumerate(((sa_ref, la_ref), (sb_ref, lb_ref))):
            den = acc[:, 2 * SWA_DH + half:2 * SWA_DH + half + 1] + jnp.exp(sink_ref[...] - ms[half])
            lse_ref[...] = ms[half] + jnp.log(den)
            dens.append(den)
        lo = lax.broadcasted_iota(jnp.int32, (PROWS, 2 * SWA_DH), 1) < SWA_DH
        o2 = acc[:, 0:2 * SWA_DH] / jnp.where(lo, dens[0], dens[1])
        for p in range(PAIRS):
            o_ref[:, p * 2 * SWA_DH:(p + 1) * 2 * SWA_DH] = o2[p * BLOCK:(p + 1) * BLOCK].astype(BF16)

    lse_shape = jax.ShapeDtypeStruct((SWA_KV, nb, PROWS, 1), F32)
    return _pcall(
        body, comm=comm, name="swa_fwd", grid=(SWA_KV, nb),
        in_specs=[tok(PROWS), prev, cur, prev, cur, prev, cur, prev, cur, bias_spec, col_spec, col_spec],
        out_specs=[tok(PROWS), lse_spec, lse_spec],
        out_shape=[jax.ShapeDtypeStruct((s, MIX), BF16), lse_shape, lse_shape],
        compiler_params=_cparams(("parallel", "parallel")))(
            src, ktop, ktop, kbot, kbot, vtop, vtop, vbot, vbot, bias, sinks[0], sinks[1])


def _swa2_bwd(src, dsrc, ktop, kbot, vtop, vbot, lses, bias, sinks, comm=None):
    s = src.shape[0]
    nb = s // BLOCK
    tok, prev, cur, bias_spec, col_spec, lse_spec = _swa2_specs()
    lane_lo = lambda shape: lax.broadcasted_iota(jnp.int32, shape, 1) < SWA_DH

    def body(q_ref, do_ref, ktp, ktc, kbp, kbc, vtp, vtc, vbp, vbc, la_ref, lb_ref, bias_ref, sa_ref, sb_ref,
             dq_ref, dkp_ref, dkc_ref, dvp_ref, dvc_ref, dbias_ref, dsa_ref, dsb_ref):
        n = pl.program_id(1)

        @pl.when(n == 0)
        def _():
            dbias_ref[...] = jnp.zeros_like(dbias_ref)
            dsa_ref[...] = jnp.zeros_like(dsa_ref)
            dsb_ref[...] = jnp.zeros_like(dsb_ref)

        kd = _band(ktp, ktc, kbp, kbc)
        q2, s2 = _swa2_scores(q_ref, kd, bias_ref, n)
        do2 = _stack_pairs(do_ref[...]).astype(BF16)
        dp2 = lax.dot_general(do2, _band(vtp, vtc, vbp, vbc), _DIMS["nt"], preferred_element_type=F32)
        ps, dss = [], []
        for half, (sink_ref, lse_ref, dsink_ref) in enumerate(((sa_ref, la_ref, dsa_ref), (sb_ref, lb_ref, dsb_ref))):
            cols = slice(half * 2 * BLOCK, (half + 1) * 2 * BLOCK)
            lse_v = lse_ref[...]
            p = jnp.exp(s2[:, cols] - lse_v)
            dp = dp2[:, cols]
            delta = jnp.sum(p * dp, axis=-1, keepdims=True)
            ds = p * (dp - delta)
            dsink_ref[...] += -jnp.exp(sink_ref[...] - lse_v) * delta
            ps.append(p.astype(BF16))
            dss.append(ds)
        ds2 = jnp.concatenate(dss, axis=1)
        dbias_ref[...] += ds2
        dsb2 = (ds2 * SWA_SCALE).astype(BF16)
        dq2 = lax.dot_general(dsb2, kd, _DIMS["nn"], preferred_element_type=F32)
        for p in range(PAIRS):
            dq_ref[:, p * 2 * SWA_DH:(p + 1) * 2 * SWA_DH] = dq2[p * BLOCK:(p + 1) * BLOCK].astype(BF16)
        dk = lax.dot_general(dsb2, q2, _DIMS["tn"], preferred_element_type=F32)
        dv = lax.dot_general(jnp.concatenate(ps, axis=1), do2, _DIMS["tn"], preferred_element_type=F32)
        for full, prev_ref, cur_ref in ((dk, dkp_ref, dkc_ref), (dv, dvp_ref, dvc_ref)):
            own = jnp.where(lane_lo((2 * BLOCK, 2 * SWA_DH)), full[:2 * BLOCK], full[2 * BLOCK:])
            prev_ref[...] = own[:BLOCK]
            cur_ref[...] = own[BLOCK:]

    kv_out = jax.ShapeDtypeStruct((SWA_KV, s, 2 * SWA_DH), F32)
    col_out = jax.ShapeDtypeStruct((SWA_KV, PROWS, 1), F32)
    return _pcall(
        body, comm=comm, name="swa_bwd", grid=(SWA_KV, nb),
        in_specs=[tok(PROWS), tok(PROWS), prev, cur, prev, cur, prev, cur, prev, cur, lse_spec, lse_spec, bias_spec,
                  col_spec, col_spec],
        out_specs=[tok(PROWS), cur, cur, cur, cur, bias_spec, col_spec, col_spec],
        out_shape=[jax.ShapeDtypeStruct((s, IN_EXT), BF16), kv_out, kv_out, kv_out, kv_out,
                   jax.ShapeDtypeStruct(bias.shape, F32), col_out, col_out],
        compiler_params=_cparams(("arbitrary", "arbitrary")))(
            src, dsrc, ktop, ktop, kbot, kbot, vtop, vtop, vbot, vbot, lses[0], lses[1], bias, sinks[0], sinks[1])


def _rope_slab(slab, table):
    t = slab * table
    return t + pltpu.roll(t, ROPE, 1)


def _low_lanes(v):
    lane = lax.broadcasted_iota(jnp.int32, v.shape, 1)
    return jnp.where(lane < ROPE, v, 0.0)


def _rms(xv, g):
    r = lax.rsqrt(jnp.mean(xv * xv, axis=-1, keepdims=True) + EPS)
    return xv * r, r


def _mla_prep(proj, gq, gkv, table):
    s = proj.shape[0]
    ts = _pick(s, ROW_TILE, 16)

    def body(p_ref, gq_ref, gkv_ref, t_ref, cq_ref, ckv_ref, kr_ref):
        xq, _ = _rms(p_ref[:, 0:Q_RANK], None)
        cq_ref[...] = (xq * gq_ref[...]).astype(BF16)
        xkv, _ = _rms(p_ref[:, Q_RANK:Q_RANK + KV_RANK], None)
        ckv_ref[...] = (xkv * gkv_ref[...]).astype(BF16)
        kr_ref[...] = _low_lanes(_rope_slab(p_ref[:, Q_RANK + KV_RANK:TAIL], t_ref[...]))

    return _pcall(
        body, name="mla_prep", grid=(s // ts,),
        in_specs=[pl.BlockSpec((ts, TAIL), lambda i: (i, TAIL0 // TAIL)), _vec(Q_RANK), _vec(KV_RANK), _row(ts, 2 * ROPE)],
        out_specs=[_row(ts, Q_RANK), _row(ts, KV_RANK), _row(ts, 2 * ROPE)],
        out_shape=[jax.ShapeDtypeStruct((s, Q_RANK), BF16), jax.ShapeDtypeStruct((s, KV_RANK), BF16),
                   jax.ShapeDtypeStruct((s, 2 * ROPE), F32)],
        compiler_params=_cparams(("parallel",)))(proj, gq, gkv, table)


def _mla_prep_bwd(proj, dcq, dckv, dkr, gq, gkv, table, dproj):
    s = proj.shape[0]
    ts = _pick(s, ROW_TILE, 16)

    def norm_bwd(xv, dy, g):
        xn, r = _rms(xv, None)
        dg = jnp.sum(dy * xn, axis=0, keepdims=True)
        dxn = dy * g
        return r * (dxn - xn * jnp.mean(dxn * xn, axis=-1, keepdims=True)), dg

    def body(p_ref, dcq_ref, dckv_ref, dkr_ref, gq_ref, gkv_ref, t_ref, _, dt_ref, dgq_ref, dgkv_ref):
        @pl.when(pl.program_id(0) == 0)
        def _():
            dgq_ref[...] = jnp.zeros_like(dgq_ref)
            dgkv_ref[...] = jnp.zeros_like(dgkv_ref)

        dxq, dgq = norm_bwd(p_ref[:, 0:Q_RANK], dcq_ref[...], gq_ref[...])
        dxkv, dgkv = norm_bwd(p_ref[:, Q_RANK:Q_RANK + KV_RANK], dckv_ref[...], gkv_ref[...])
        dgq_ref[...] += dgq
        dgkv_ref[...] += dgkv
        d = _low_lanes(dkr_ref[...])
        dslab = (d + pltpu.roll(d, ROPE, 1)) * t_ref[...]
        dt_ref[:, 0:Q_RANK] = dxq.astype(BF16)
        dt_ref[:, Q_RANK:Q_RANK + KV_RANK] = dxkv.astype(BF16)
        dt_ref[:, Q_RANK + KV_RANK:TAIL] = dslab.astype(BF16)

    return _pcall(
        body, name="mla_prep_bwd", grid=(s // ts,),
        in_specs=[pl.BlockSpec((ts, TAIL), lambda i: (i, TAIL0 // TAIL)), _row(ts, Q_RANK), _row(ts, KV_RANK),
                  _row(ts, 2 * ROPE), _vec(Q_RANK), _vec(KV_RANK), _row(ts, 2 * ROPE), pl.BlockSpec(memory_space=pl.ANY)],
        out_specs=[pl.BlockSpec((ts, TAIL), lambda i: (i, TAIL0 // TAIL)), _vec(Q_RANK), _vec(KV_RANK)],
        out_shape=[jax.ShapeDtypeStruct(dproj.shape, BF16), jax.ShapeDtypeStruct((1, Q_RANK), F32),
                   jax.ShapeDtypeStruct((1, KV_RANK), F32)],
        input_output_aliases={7: 0},
        compiler_params=_cparams(("arbitrary",)))(proj, dcq, dckv, dkr, gq, gkv, table, dproj)


def _head_specs(ts):
    tok = lambda w: pl.BlockSpec((ts, w), lambda h, i: (i, 0))
    head = lambda w: pl.BlockSpec((None, ts, w), lambda h, i: (h, i, 0))
    wgt = lambda r, c: pl.BlockSpec((None, r, c), lambda h, i: (h, 0, 0))
    return tok, head, wgt


def _mla_qkv(cq, ckv, kr, wq, wkv, table):
    s = cq.shape[0]
    ts = _pick(s, 4 * ROW_TILE, 16)
    tok, head, wgt = _head_specs(ts)

    def body(cq_ref, ckv_ref, kr_ref, wq_ref, wkv_ref, t_ref, q_ref, k_ref, v_ref):
        qf = lax.dot_general(cq_ref[...], wq_ref[...], _DIMS["nn"], preferred_element_type=F32)
        q_ref[:, 0:NOPE] = qf[:, 0:NOPE].astype(BF16)
        q_ref[:, NOPE:QW] = _rope_slab(qf[:, NOPE:QW], t_ref[...]).astype(BF16)
        kv = lax.dot_general(ckv_ref[...], wkv_ref[...], _DIMS["nn"], preferred_element_type=F32)
        k_ref[:, 0:NOPE] = kv[:, 0:NOPE].astype(BF16)
        k_ref[:, NOPE:QW] = kr_ref[...].astype(BF16)
        v_ref[:, 0:VDIM] = kv[:, NOPE:NOPE + VDIM].astype(BF16)
        lane = lax.broadcasted_iota(jnp.int32, (ts, VDIM), 1)
        v_ref[:, VDIM:2 * VDIM] = jnp.where(lane == 0, 1.0, 0.0).astype(BF16)

    return _pcall(
        body, name="mla_qkv", grid=(MLA_H, s // ts),
        in_specs=[tok(Q_RANK), tok(KV_RANK), tok(2 * ROPE), wgt(Q_RANK, QW), wgt(KV_RANK, NOPE + VDIM), tok(2 * ROPE)],
        out_specs=[head(QW), head(QW), head(2 * VDIM)],
        out_shape=[jax.ShapeDtypeStruct((MLA_H, s, QW), BF16), jax.ShapeDtypeStruct((MLA_H, s, QW), BF16),
                   jax.ShapeDtypeStruct((MLA_H, s, 2 * VDIM), BF16)],
        compiler_params=_cparams(("parallel", "parallel")))(cq, ckv, kr, wq, wkv, table)


def _mla_qkv_bwd(dq, dk, dv, cq, ckv, wq, wkv, table, comm=None):
    s = cq.shape[0]
    ts = _pick(s, 4 * ROW_TILE, 16)
    tok, head, wgt = _head_specs(ts)
    whole = lambda w: pl.BlockSpec((s, w), lambda h, i: (0, 0))

    def body(dq_ref, dk_ref, dv_ref, cq_ref, ckv_ref, wq_ref, wkv_ref, t_ref,
             dcq_ref, dckv_ref, dkr_ref, gwq_ref, gwkv_ref):
        h, i = pl.program_id(0), pl.program_id(1)
        rows = pl.ds(pl.multiple_of(i * ts, ts), ts)
        d = dq_ref[:, NOPE:QW]
        dslab = (d + pltpu.roll(d, ROPE, 1)) * t_ref[...]
        dqe = jnp.concatenate([dq_ref[:, 0:NOPE], dslab], axis=1).astype(BF16)
        dkv = jnp.concatenate([dk_ref[:, 0:NOPE], dv_ref[...]], axis=1).astype(BF16)
        dcq = lax.dot_general(dqe, wq_ref[...], _DIMS["nt"], preferred_element_type=F32)
        dckv = lax.dot_general(dkv, wkv_ref[...], _DIMS["nt"], preferred_element_type=F32)
        gwq = lax.dot_general(cq_ref[...], dqe, _DIMS["tn"], preferred_element_type=F32)
        gwkv = lax.dot_general(ckv_ref[...], dkv, _DIMS["tn"], preferred_element_type=F32)
        dkr = dk_ref[:, NOPE:QW].astype(F32)

        @pl.when(h == 0)
        def _():
            dcq_ref[rows, :] = dcq
            dckv_ref[rows, :] = dckv
            dkr_ref[rows, :] = dkr

        @pl.when(h > 0)
        def _():
            dcq_ref[rows, :] += dcq
            dckv_ref[rows, :] += dckv
            dkr_ref[rows, :] += dkr

        @pl.when(i == 0)
        def _():
            gwq_ref[...] = gwq
            gwkv_ref[...] = gwkv

        @pl.when(i > 0)
        def _():
            gwq_ref[...] += gwq
            gwkv_ref[...] += gwkv

    return _pcall(
        body, comm=comm, name="mla_qkv_bwd", grid=(MLA_H, s // ts),
        in_specs=[head(QW), head(QW), head(VDIM), tok(Q_RANK), tok(KV_RANK), wgt(Q_RANK, QW),
                  wgt(KV_RANK, NOPE + VDIM), tok(2 * ROPE)],
        out_specs=[whole(Q_RANK), whole(KV_RANK), whole(2 * ROPE), wgt(Q_RANK, QW), wgt(KV_RANK, NOPE + VDIM)],
        out_shape=[jax.ShapeDtypeStruct((s, Q_RANK), F32), jax.ShapeDtypeStruct((s, KV_RANK), F32),
                   jax.ShapeDtypeStruct((s, 2 * ROPE), F32), jax.ShapeDtypeStruct((MLA_H, Q_RANK, QW), F32),
                   jax.ShapeDtypeStruct((MLA_H, KV_RANK, NOPE + VDIM), F32)],
        compiler_params=_cparams(("arbitrary", "arbitrary")))(dq, dk, dv, cq, ckv, wq, wkv, table)


def _as_row(col):
    return jnp.broadcast_to(col, (col.shape[0], 128)).T[0:1, :]


def _diag_mask(t):
    return lax.broadcasted_iota(jnp.int32, (t, t), 1) <= lax.broadcasted_iota(jnp.int32, (t, t), 0)


def _mla_fwd(q, k, v, mix, comm=None):
    s = q.shape[1]
    t = _pick(s, ATT_T, 128)
    nt = s // t
    assert nt % 2 == 0
    hb = MLA_H

    def fold(p, u):
        first = u <= p
        return jnp.where(first, p, nt - 1 - p), jnp.where(first, u, u - p - 1)

    to_log2 = MLA_SCALE * math.log2(math.e)

    def body(q_ref, k_ref, v_ref, _, o_ref, oh_ref, lse_ref, m_ref, acc_ref):
        i, j = fold(pl.program_id(1), pl.program_id(2))

        @pl.when(j == 0)
        def _():
            m_ref[...] = jnp.full_like(m_ref, -jnp.inf)
            acc_ref[...] = jnp.zeros_like(acc_ref)

        def step(diagonal):
            for h in range(hb):
                sc = lax.dot_general(q_ref[h], k_ref[h], _DIMS["nt"], preferred_element_type=F32)
                if diagonal:
                    sc = jnp.where(_diag_mask(t), sc, -jnp.inf)
                m_old = m_ref[h]
                m_new = jnp.maximum(m_old, jnp.max(sc, axis=-1, keepdims=True))
                alpha = jnp.exp2((m_old - m_new) * to_log2)
                p = jnp.exp2((sc - m_new) * to_log2)
                acc_ref[h] = alpha * acc_ref[h] + lax.dot_general(p.astype(BF16), v_ref[h], _DIMS["nn"],
                                                                  preferred_element_type=F32)
                m_ref[h] = m_new

        @pl.when(j < i)
        def _():
            step(False)

        @pl.when(j == i)
        def _():
            step(True)
            for h in range(hb):
                den = acc_ref[h, :, VDIM:VDIM + 1]
                o = acc_ref[h, :, 0:VDIM] / den
                o_ref[:, h * VDIM:(h + 1) * VDIM] = o
                oh_ref[:, h * VDIM:(h + 1) * VDIM] = o.astype(BF16)
                lse_ref[h] = _as_row(m_ref[h] * MLA_SCALE + jnp.log(den))

    o_spec = pl.BlockSpec((t, hb * VDIM), lambda h, p, u: (fold(p, u)[0], h))
    first = (MIX - MLA_H * VDIM) // (hb * VDIM)
    mix_spec = pl.BlockSpec((t, hb * VDIM), lambda h, p, u: (fold(p, u)[0], first + h))
    return _pcall(
        body, comm=comm, name="mla_fwd", grid=(MLA_H // hb, nt // 2, nt + 1),
        in_specs=[pl.BlockSpec((hb, t, QW), lambda h, p, u: (h, fold(p, u)[0], 0)),
                  pl.BlockSpec((hb, t, QW), lambda h, p, u: (h, fold(p, u)[1], 0)),
                  pl.BlockSpec((hb, t, 2 * VDIM), lambda h, p, u: (h, fold(p, u)[1], 0)),
                  pl.BlockSpec(memory_space=pl.ANY)],
        out_specs=[o_spec, mix_spec, pl.BlockSpec((hb, 1, t), lambda h, p, u: (h, 0, fold(p, u)[0]))],
        out_shape=[jax.ShapeDtypeStruct((s, MLA_H * VDIM), F32), jax.ShapeDtypeStruct(mix.shape, BF16),
                   jax.ShapeDtypeStruct((MLA_H, 1, s), F32)],
        input_output_aliases={3: 1},
        scratch_shapes=[pltpu.VMEM((hb, t, 1), F32), pltpu.VMEM((hb, t, 2 * VDIM), F32)],
        compiler_params=_cparams(("parallel", "parallel", "arbitrary")))(q, k, v, mix)


def _mla_delta(dmix, o):
    s = o.shape[0]
    ts = _pick(s, 2 * ROW_TILE, 16)
    w = MLA_H * VDIM

    def body(d_ref, o_ref, out_ref):
        prod = d_ref[...] * o_ref[...]
        for h in range(MLA_H):
            out_ref[h] = _as_row(jnp.sum(prod[:, h * VDIM:(h + 1) * VDIM], axis=-1, keepdims=True))

    return _pcall(body, name="mla_delta", grid=(s // ts,),
                  in_specs=[pl.BlockSpec((ts, w), lambda i: (i, SWA_HEADS * SWA_DH // w)), pl.BlockSpec((ts, w), lambda i: (i, 0))],
                  out_specs=pl.BlockSpec((MLA_H, 1, ts), lambda i: (0, 0, i)),
                  out_shape=jax.ShapeDtypeStruct((MLA_H, 1, s), F32), compiler_params=_cparams(("parallel",)))(dmix, o)


def _mla_bwd(q, k, v, dmix, delta, lse, comm=None):
    s = q.shape[1]
    t = _pick(s, ATT_T, 128)
    nt = s // t
    assert nt % 2 == 0
    hb = 2 * MLA_HB
    o_blk0 = SWA_HEADS * SWA_DH // (hb * VDIM)

    def fold(p, u):
        first = u < nt - p
        return jnp.where(first, p, nt - 1 - p), jnp.where(first, p + u, u - 1)

    log2e = math.log2(math.e)

    def body(q_ref, k_ref, v_ref, do_ref, delta_ref, lse_ref, dq_ref, dk_ref, dv_ref, dk_acc, dv_acc):
        j, i = fold(pl.program_id(1), pl.program_id(2))
        rows = pl.ds(pl.multiple_of(i * t, t), t)

        @pl.when(i == j)
        def _():
            dk_acc[...] = jnp.zeros_like(dk_acc)
            dv_acc[...] = jnp.zeros_like(dv_acc)

        def step(diagonal):
            for h in range(hb):
                qv, kv_ = q_ref[h], k_ref[h]
                dob = do_ref[:, h * VDIM:(h + 1) * VDIM].astype(BF16)
                st = lax.dot_general(kv_, qv, _DIMS["nt"], preferred_element_type=F32)
                pt = jnp.exp2(st * (MLA_SCALE * log2e) - lse_ref[h] * log2e)
                if diagonal:
                    keep = lax.broadcasted_iota(jnp.int32, (t, t), 0) <= lax.broadcasted_iota(jnp.int32, (t, t), 1)
                    pt = jnp.where(keep, pt, 0.0)
                dpt = lax.dot_general(v_ref[h], dob, _DIMS["nt"], preferred_element_type=F32)
                dst = (pt * (dpt - delta_ref[h]) * MLA_SCALE).astype(BF16)
                dv_acc[h] += lax.dot_general(pt.astype(BF16), dob, _DIMS["nn"], preferred_element_type=F32)
                dk_acc[h] += lax.dot_general(dst, qv, _DIMS["nn"], preferred_element_type=F32)
                dqv = lax.dot_general(dst, kv_, _DIMS["tn"], preferred_element_type=F32)

                @pl.when(j == 0)
                def _():
                    dq_ref[h, rows, :] = dqv

                @pl.when(j > 0)
                def _():
                    dq_ref[h, rows, :] += dqv

        @pl.when(i > j)
        def _():
            step(False)

        @pl.when(i == j)
        def _():
            step(True)

        @pl.when(i == nt - 1)
        def _():
            dk_ref[...] = dk_acc[...].astype(BF16)
            dv_ref[...] = dv_acc[...].astype(BF16)

    qi = lambda h, p, u: (h, fold(p, u)[1], 0)
    kj = lambda h, p, u: (h, fold(p, u)[0], 0)
    row = pl.BlockSpec((hb, 1, t), lambda h, p, u: (h, 0, fold(p, u)[1]))
    return _pcall(
        body, comm=comm, name="mla_bwd", grid=(MLA_H // hb, nt // 2, nt + 1),
        in_specs=[pl.BlockSpec((hb, t, QW), qi), pl.BlockSpec((hb, t, QW), kj), pl.BlockSpec((hb, t, VDIM), kj),
                  pl.BlockSpec((t, hb * VDIM), lambda h, p, u: (fold(p, u)[1], o_blk0 + h)), row, row],
        out_specs=[pl.BlockSpec((hb, s, QW), lambda h, p, u: (h, 0, 0)), pl.BlockSpec((hb, t, QW), kj),
                   pl.BlockSpec((hb, t, VDIM), kj)],
        out_shape=[jax.ShapeDtypeStruct((MLA_H, s, QW), F32), jax.ShapeDtypeStruct((MLA_H, s, QW), BF16),
                   jax.ShapeDtypeStruct((MLA_H, s, VDIM), BF16)],
        scratch_shapes=[pltpu.VMEM((hb, t, QW), F32), pltpu.VMEM((hb, t, VDIM), F32)],
        compiler_params=_cparams(("arbitrary", "arbitrary", "arbitrary")))(q, k, v, dmix, delta, lse)


def _adamw(name, w, g, m, v, parts):
    r, c = w.shape
    n_parts = g.shape[0] if parts else 1
    tr = r if r * c <= ADAM_ELEMS else _pick(r, max(8, ADAM_ELEMS // c // 8 * 8), 8)
    c1 = 1.0 - ADAM_B1 ** ADAM_STEP
    c2 = 1.0 - ADAM_B2 ** ADAM_STEP

    def body(w_ref, g_ref, m_ref, v_ref, go_ref, d_ref, mo_ref, vo_ref):
        if parts:
            gv = g_ref[0].astype(F32)
            for j in range(1, n_parts):
                gv = gv + g_ref[j].astype(F32)
        else:
            gv = g_ref[...]
        mv = ADAM_B1 * m_ref[...] + (1.0 - ADAM_B1) * gv
        vv = ADAM_B2 * v_ref[...] + (1.0 - ADAM_B2) * (gv * gv)
        go_ref[...] = gv
        mo_ref[...] = mv
        vo_ref[...] = vv
        d_ref[...] = -ADAM_LR * ((mv / c1) / (jnp.sqrt(vv / c2) + ADAM_EPS) + ADAM_WD * w_ref[...])

    blk = pl.BlockSpec((tr, c), lambda i: (i, 0))
    g_spec = pl.BlockSpec((n_parts, tr, c), lambda i: (0, i, 0)) if parts else blk
    out = jax.ShapeDtypeStruct((r, c), F32)
    return _pcall(body, name=name, grid=(r // tr,), in_specs=[blk, g_spec, blk, blk], out_specs=[blk] * 4,
                  out_shape=[out] * 4, compiler_params=_cparams(("parallel",)))(w, g, m, v)


def _t5_bucket(dist):
    n = jnp.maximum(dist, 0)
    max_exact = REL_BUCKETS // 2
    nf = jnp.maximum(n, 1).astype(F32)
    large = max_exact + (jnp.log(nf / max_exact) / math.log(REL_MAX_DIST / max_exact)
                         * (REL_BUCKETS - max_exact)).astype(jnp.int32)
    return jnp.where(n < max_exact, n, jnp.minimum(large, REL_BUCKETS - 1))


def _swap_halves(w, r0):
    return jnp.concatenate([w[:, r0 + ROPE // 2:r0 + ROPE], w[:, r0:r0 + ROPE // 2]], axis=1)


def _fold_swapped(g, r0, width):
    sw = g[..., width:width + ROPE]
    half = ROPE // 2
    return jnp.concatenate([g[..., :r0], g[..., r0:r0 + half] + sw[..., half:], g[..., r0 + half:r0 + ROPE] + sw[..., :half],
                            g[..., r0 + ROPE:width]], axis=-1)


def kernel(x, c, w_mod, b_mod, attn_norm_g, w_in, swa_sinks, rel_bias, mla_q_norm_g, w_uq, mla_kv_norm_g, w_ukv, w_out, mlp_norm_g, w_ff1, w_ff2, final_norm_g, loss_target, m_w_mod, m_b_mod, m_attn_norm_g, m_w_in, m_swa_sinks, m_rel_bias, m_mla_q_norm_g, m_w_uq, m_mla_kv_norm_g, m_w_ukv, m_w_out, m_mlp_norm_g, m_w_ff1, m_w_ff2, m_final_norm_g, v_w_mod, v_b_mod, v_attn_norm_g, v_w_in, v_swa_sinks, v_rel_bias, v_mla_q_norm_g, v_w_uq, v_mla_kv_norm_g, v_w_ukv, v_w_out, v_mlp_norm_g, v_w_ff1, v_w_ff2, v_final_norm_g):
    s, d = x.shape[1], x.shape[2]
    ffs = w_ff1.shape[2]
    ff = ffs * NDEV
    nmod = w_mod.shape[2]
    me = 4 * lax.axis_index("x") + 2 * lax.axis_index("y") + lax.axis_index("c")
    x2d, tgt = x[0], loss_target[0]
    final_g = final_norm_g.reshape(1, d)

    w_in_l = jnp.concatenate([w_in[0], _swap_halves(w_in[0], OFF_KR)], axis=1).astype(BF16)
    w_uq_l = jnp.concatenate([w_uq[0], _swap_halves(w_uq[0], NOPE)], axis=1).astype(BF16)
    core = jnp.full((1, 128), lax.axis_index("c"), F32)
    (c_all,) = _exchange("gather_c", _Gather([c]))

    b_cols = lax.dynamic_slice(b_mod, (0, me * nmod), (1, nmod))
    act_all, mod_cols = _mod_fwd(c_all.reshape(NDEV, d), w_mod[0], b_cols)
    (mod_g,) = _exchange("gather_mod", _Gather([mod_cols]))
    mod = lax.dynamic_index_in_dim(mod_g, me, axis=1, keepdims=False).reshape(1, 6 * d)
    sh1, sc1, g1, sh2, sc2, g2 = [mod[:, i * d:(i + 1) * d] for i in range(6)]

    pos = jnp.arange(s, dtype=F32)
    inv_freq = ROPE_THETA ** (-jnp.arange(ROPE // 2, dtype=F32) / (ROPE // 2))
    ang = pos[:, None] * inv_freq[None, :]
    cos, sin = jnp.cos(ang), jnp.sin(ang)
    table = jnp.concatenate([cos, cos, -sin, sin], axis=1)
    q_loc = jnp.arange(BLOCK)[:, None]
    k_loc = jnp.arange(2 * BLOCK)[None, :]
    dist = q_loc + BLOCK - k_loc
    in_window = (dist >= 0) & (dist < BLOCK)
    onehot = (_t5_bucket(dist).reshape(-1, 1) == jnp.arange(REL_BUCKETS)[None, :]).astype(BF16)
    bias = _bias_expand(rel_bias.T, onehot.T).reshape(SWA_HEADS, BLOCK, 2 * BLOCK)
    bias = jnp.where(in_window[None], bias, -jnp.inf).reshape(SWA_KV, PAIRS, 2, BLOCK, 2 * BLOCK)
    bias = bias.transpose(0, 1, 3, 2, 4).reshape(SWA_KV, PROWS, PCOLS)
    sinks = jnp.broadcast_to(swa_sinks.reshape(SWA_KV, PAIRS, 1, 2), (SWA_KV, PAIRS, BLOCK, 2)).reshape(SWA_KV, PROWS, 2)
    sinks = (sinks[:, :, 0:1], sinks[:, :, 1:2])

    h1, w_in_g, w_uq_g, w_ukv_g = _norm_mod("norm1", x2d, attn_norm_g, sc1, sh1,
                                            comm=[_Gather([w_in_l, w_uq_l, w_ukv[0].astype(BF16)])])
    w_in_e = w_in_g.reshape(d, IN_EXT)

    def both_dtypes(acc, ex, outs):
        outs[0][...] = acc
        outs[1][...] = acc.astype(BF16)

    tmp = _pick(s, MM_TM // 2, 128)
    proj_blk = pl.BlockSpec((tmp, IN_EXT), lambda i, j, q: (i, 0))
    proj, proj_h = _mm("proj", h1, w_in_e, "nn", (s // tmp, 1, 1), pl.BlockSpec((tmp, d), lambda i, j, q: (i, 0)),
                       pl.BlockSpec((d, IN_EXT), lambda i, j, q: (0, 0)),
                       [jax.ShapeDtypeStruct((s, IN_EXT), F32), jax.ShapeDtypeStruct((s, IN_EXT), BF16)],
                       [proj_blk, proj_blk], (tmp, IN_EXT), both_dtypes)
    def diag_pair(tok):
        x = jnp.stack([tok[:, :SWA_DH], tok[:, SWA_DH:]])
        zero = jnp.zeros_like(x)
        return jnp.concatenate([x, zero], axis=2), jnp.concatenate([zero, x], axis=2)

    k_top, k_bot = diag_pair(proj_h[:, OFF_K:OFF_V])
    v_top, v_bot = diag_pair(proj_h[:, OFF_V:OFF_CQ])
    o_a, lse_a0, lse_a1, w_out_g = _swa2_fwd(proj_h, k_top, k_bot, v_top, v_bot, bias, sinks,
                                             comm=[_Gather([w_out[0].astype(BF16)])])
    w_out_f = w_out_g.reshape(MIX, d)

    cq, ckv, kr = _mla_prep(proj, mla_q_norm_g, mla_kv_norm_g, table)
    q_b, k_b, v_b = _mla_qkv(cq, ckv, kr, w_uq_g, w_ukv_g, table)
    o_b, mix, lse_b, w_ff1_g = _mla_fwd(q_b, k_b, v_b, o_a, comm=[_Gather([w_ff1[0].astype(BF16)])])

    tm, tn, tk = _pick(s, MM_TM, 128), _pick(d, MM_TN, 128), _pick(MIX, MM_TK, 128)
    row_blk = pl.BlockSpec((tm, tn), lambda i, j, q: (i, j))
    gate_blk = pl.BlockSpec((1, tn), lambda i, j, q: (0, j))

    def gated_residual(acc, ex, outs):
        outs[0][...] = acc.astype(BF16)
        outs[1][...] = ex[0][...] + ex[1][...] * acc

    branch_out = [jax.ShapeDtypeStruct((s, d), BF16), jax.ShapeDtypeStruct((s, d), F32)]

    y1, x2 = _mm("out_proj", mix, w_out_f, "nn", (s // tm, d // tn, MIX // tk),
                 pl.BlockSpec((tm, tk), lambda i, j, q: (i, q)), pl.BlockSpec((tk, tn), lambda i, j, q: (q, j)),
                 branch_out, [row_blk, row_blk], (tm, tn), gated_residual,
                 extras=(x2d, g1), extra_specs=(row_blk, gate_blk))

    (h2,) = _norm_mod("norm2", x2, mlp_norm_g, sc2, sh2)
    tnf, tkd = _pick(ffs, MM_TN, 128), _pick(d, MM_TK, 128)
    rf = ffs // tnf
    ff_blk = pl.BlockSpec((tm, tnf), lambda i, j, q: (i, j))

    def relu_sq(acc, ex, outs):
        u = jnp.maximum(acc, 0.0)
        outs[0][...] = u
        outs[1][...] = (u * u).astype(BF16)

    u, uu, w_ff2_g = _mm("ff1", h2, w_ff1_g, "nn", (s // tm, ff // tnf, d // tkd),
                         pl.BlockSpec((tm, tkd), lambda i, j, q: (i, q)),
                         pl.BlockSpec((None, tkd, tnf), lambda i, j, q: (j // rf, q, j % rf)),
                         [jax.ShapeDtypeStruct((s, ff), F32), jax.ShapeDtypeStruct((s, ff), BF16)], [ff_blk, ff_blk],
                         (tm, tnf), relu_sq, comm=[_Gather([w_ff2[0].astype(BF16)])])
    w_ff2_f = w_ff2_g.reshape(ff, d)
    tkf = _pick(ff, MM_TK, 128)
    y2, x3 = _mm("ff2", uu, w_ff2_f, "nn", (s // tm, d // tn, ff // tkf),
                 pl.BlockSpec((tm, tkf), lambda i, j, q: (i, q)), pl.BlockSpec((tkf, tn), lambda i, j, q: (q, j)),
                 branch_out, [row_blk, row_blk], (tm, tn), gated_residual,
                 extras=(x2, g2), extra_specs=(row_blk, gate_blk))

    dx3, dy2, loss_p, dgf, dg2 = _loss_head(x3, tgt, y2, final_g, g2)

    def relu_sq_bwd(acc, ex, outs):
        outs[0][...] = (acc * (2.0 * ex[0][...])).astype(BF16)

    tnf2 = _pick(ff, MM_TN, 128)
    du = _mm("ff2_dx", dy2, w_ff2_f, "nt", (s // tm, ff // tnf2, d // tkd),
             pl.BlockSpec((tm, tkd), lambda i, j, q: (i, q)), pl.BlockSpec((tnf2, tkd), lambda i, j, q: (j, q)),
             [jax.ShapeDtypeStruct((s, ff), BF16)], [pl.BlockSpec((tm, tnf2), lambda i, j, q: (i, j))],
             (tm, tnf2), relu_sq_bwd, extras=(u,), extra_specs=(pl.BlockSpec((tm, tnf2), lambda i, j, q: (i, j)),))[0]
    gw_ff2 = _mm_plain("ff2_dw", uu, dy2, "tn", ff, d, s, BF16)
    tmd, tks = _pick(d, MM_TM, 128), _pick(s, MM_TK, 128)
    gw_ff2 = gw_ff2.reshape(NDEV, ffs, d)
    dh2, s_ff2 = _mm("ff1_dx", du, w_ff1_g, "nt", (s // tm, d // tn, NDEV // 2),
                     pl.BlockSpec((tm, 2 * ffs), lambda i, j, q: (i, q)),
                     pl.BlockSpec((2, tn, ffs), lambda i, j, q: (q, j, 0)),
                     [jax.ShapeDtypeStruct((s, d), F32)], [row_blk], (tm, tn), _store(F32),
                     comm=[_PairSwap([gw_ff2])], b_parts=2)
    c_ff2 = _pair_sum("pair_ff2", gw_ff2, s_ff2, core)
    gw_ff1, p_ff2 = _mm("ff1_dw", h2, du, "tn", (d // tmd, ff // tnf, s // tks),
                        pl.BlockSpec((tks, tmd), lambda i, j, q: (q, i)), pl.BlockSpec((tks, tnf), lambda i, j, q: (q, j)),
                        [jax.ShapeDtypeStruct((NDEV, d, ffs), BF16)],
                        [pl.BlockSpec((None, tmd, tnf), lambda i, j, q: (j // rf, i, j % rf))], (tmd, tnf), _store(BF16),
                        comm=[_ChipScatter([c_ff2])])
    dx2, dy1, dsc2, dsh2, dgm, dg1, s_ff1 = _norm_mod_bwd("norm2_bwd", x2, dh2, dx3, mlp_norm_g, sc2, y1, g1,
                                                          comm=[_PairSwap([gw_ff1])])
    c_ff1 = _pair_sum("pair_ff1", gw_ff1, s_ff1, core)

    dmix = _mm_plain("out_proj_dx", dy1, w_out_f, "nt", s, MIX, d, F32)
    gw_out = _mm_plain("out_proj_dw", mix, dy1, "tn", MIX, d, s, BF16).reshape(NDEV, MIX // NDEV, d)

    delta_b = _mla_delta(dmix, o_b)
    dq_b, dk_b, dv_b, p_ff1, s_out = _mla_bwd(q_b, k_b, v_b, dmix, delta_b, lse_b,
                                              comm=[_ChipScatter([c_ff1]), _PairSwap([gw_out])])
    c_out = _pair_sum("pair_out", gw_out, s_out, core)
    dcq, dckv, dkr, gw_uq_e, gw_ukv, p_out = _mla_qkv_bwd(dq_b, dk_b, dv_b, cq, ckv, w_uq_g, w_ukv_g, table,
                                                          comm=[_ChipScatter([c_out])])
    gw_uq = _fold_swapped(gw_uq_e, NOPE, NOPE + ROPE).astype(BF16)
    gw_ukv = gw_ukv.astype(BF16)

    dproj, dkp, dkc, dvp, dvc, dbias, dsink0, dsink1, s_uq, s_ukv = _swa2_bwd(
        proj_h, dmix, k_top, k_bot, v_top, v_bot, (lse_a0, lse_a1), bias, sinks, comm=[_PairSwap([gw_uq, gw_ukv])])
    c_uq = _pair_sum("pair_uq", gw_uq, s_uq, core)
    c_ukv = _pair_sum("pair_ukv", gw_ukv, s_ukv, core)
    dproj, dgq, dgkv = _mla_prep_bwd(proj, dcq, dckv, dkr, mla_q_norm_g, mla_kv_norm_g, table, dproj)

    def band_grad(cur, prv):
        g = cur + jnp.concatenate([prv[:, BLOCK:], jnp.zeros_like(prv[:, :BLOCK])], axis=1)
        g = g[:, :, :SWA_DH] + g[:, :, SWA_DH:]
        return jnp.concatenate([g[0], g[1]], axis=1)

    dbias = dbias.reshape(SWA_KV, PAIRS, BLOCK, 2, 2 * BLOCK).transpose(0, 1, 3, 2, 4)
    dsink = jnp.stack([dsink0.reshape(SWA_KV, PAIRS, BLOCK), dsink1.reshape(SWA_KV, PAIRS, BLOCK)], axis=2)
    drel_t, dsinks = _bias_reduce(dbias.reshape(SWA_HEADS, BLOCK * 2 * BLOCK), onehot, dsink.reshape(SWA_HEADS, BLOCK))
    dkv = jnp.concatenate([band_grad(dkc, dkp), band_grad(dvc, dvp)], axis=1).astype(BF16)
    dproj = lax.dynamic_update_slice(dproj, dkv, (0, OFF_K))
    def fold_rotary(acc, ex, outs):
        slab = acc[:, TAIL - 2 * ROPE:TAIL]
        lane = lax.broadcasted_iota(jnp.int32, slab.shape, 1)
        folded = slab + jnp.where(lane < ROPE // 2, pltpu.roll(slab, ROPE // 2, 1), pltpu.roll(slab, 3 * ROPE // 2, 1))
        last = pl.program_id(1) == IN_EXT // TAIL - 1
        outs[0][:, 0:TAIL - 2 * ROPE] = acc[:, 0:TAIL - 2 * ROPE].astype(BF16)
        outs[0][:, TAIL - 2 * ROPE:TAIL] = jnp.where(last, folded, slab).astype(BF16)

    tmw, tks = _pick(d, MM_TM, 128), _pick(s, MM_TK, 128)
    gw_in = _mm("proj_dw", h1, dproj, "tn", (d // tmw, IN_EXT // TAIL, s // tks),
                pl.BlockSpec((tks, tmw), lambda i, j, q: (q, i)), pl.BlockSpec((tks, TAIL), lambda i, j, q: (q, j)),
                [jax.ShapeDtypeStruct((d, IN_COLS), BF16)], [pl.BlockSpec((tmw, TAIL), lambda i, j, q: (i, j))],
                (tmw, TAIL), fold_rotary)[0].reshape(NDEV, d // NDEV, IN_COLS)
    tkt = IN_EXT
    dh1, s_in, p_uq, p_ukv = _mm(
        "proj_dx", dproj, w_in_e, "nt", (s // tm, d // tn, IN_EXT // tkt),
        pl.BlockSpec((tm, tkt), lambda i, j, q: (i, q)), pl.BlockSpec((tn, tkt), lambda i, j, q: (j, q)),
        [jax.ShapeDtypeStruct((s, d), F32)], [row_blk], (tm, tn), _store(F32),
        comm=[_PairSwap([gw_in]), _ChipScatter([c_uq, c_ukv])])
    c_in = _pair_sum("pair_in", gw_in, s_in, core)
    gx, dsc1, dsh1, dga, p_in = _norm_mod_bwd("norm1_bwd", x2d, dh1, dx2, attn_norm_g, sc1,
                                              comm=[_ChipScatter([c_in])])

    small = [jnp.concatenate([dsh1, dsc1, dg1, dsh2, dsc2, dg2], axis=1), dga, dgm, dgf, dgq, dgkv,
             dsinks.reshape(1, SWA_HEADS), drel_t.T.reshape(1, REL_BUCKETS * SWA_HEADS)]
    n_small = sum(a.shape[1] for a in small)
    n_pad = -(n_small + 1) % 1024 + 1
    rows_small = (n_small + n_pad) // 128
    pad = jnp.zeros((1, n_pad), F32)
    pack = lambda parts, tail=pad: jnp.concatenate([p.reshape(1, -1) for p in parts] + [tail], axis=1).reshape(rows_small, 128)
    (small_g,) = _exchange("gather_small", _Gather([pack(small, jnp.concatenate([loss_p, pad[:, 1:]], axis=1))]))
    small_names = (b_mod, attn_norm_g, mlp_norm_g, final_norm_g, mla_q_norm_g, mla_kv_norm_g, swa_sinks, rel_bias)
    small_m = (m_b_mod, m_attn_norm_g, m_mlp_norm_g, m_final_norm_g, m_mla_q_norm_g, m_mla_kv_norm_g, m_swa_sinks, m_rel_bias)
    small_v = (v_b_mod, v_attn_norm_g, v_mlp_norm_g, v_final_norm_g, v_mla_q_norm_g, v_mla_kv_norm_g, v_swa_sinks, v_rel_bias)
    small_out = _adamw("adamw_small", pack(small_names), small_g, pack(small_m), pack(small_v), parts=True)

    def unpack(flat):
        flat = flat.reshape(1, -1)
        out, off = [], 0
        for a in small_names:
            out.append(flat[:, off:off + a.size].reshape(a.shape))
            off += a.size
        return out

    sg, sd, sm, sv = [unpack(o) for o in small_out]
    loss = small_out[0].reshape(-1)[n_small]

    dmod_cols = lax.dynamic_slice(small_g.reshape(NDEV, -1), (0, me * nmod), (NDEV, nmod))
    gw_mod = _mod_wgrad(act_all, dmod_cols)
    big = {"w_mod": _adamw("adamw_w_mod", w_mod[0], gw_mod, m_w_mod[0], v_w_mod[0], parts=False)}

    for name, w, p, m, v in (("w_in", w_in, p_in, m_w_in, v_w_in), ("w_uq", w_uq, p_uq, m_w_uq, v_w_uq),
                             ("w_ukv", w_ukv, p_ukv, m_w_ukv, v_w_ukv), ("w_out", w_out, p_out, m_w_out, v_w_out),
                             ("w_ff1", w_ff1, p_ff1, m_w_ff1, v_w_ff1), ("w_ff2", w_ff2, p_ff2, m_w_ff2, v_w_ff2)):
        big[name] = _adamw("adamw_" + name, w[0], p, m[0], v[0], parts=True)

    order = ("w_mod", "b_mod", "attn_norm_g", "w_in", "swa_sinks", "rel_bias", "mla_q_norm_g", "w_uq", "mla_kv_norm_g",
             "w_ukv", "w_out", "mlp_norm_g", "w_ff1", "w_ff2", "final_norm_g")
    small_idx = {"b_mod": 0, "attn_norm_g": 1, "mlp_norm_g": 2, "final_norm_g": 3, "mla_q_norm_g": 4,
                 "mla_kv_norm_g": 5, "swa_sinks": 6, "rel_bias": 7}
    outs = []
    for kind, small_list in enumerate((sg, sd, sm, sv)):
        for name in order:
            outs.append(small_list[small_idx[name]] if name in small_idx else big[name][kind][None])
    return (loss, gx[None], *outs)
```

```python
import functools
import math

import jax
import jax.numpy as jnp
from jax import lax
from jax.experimental import pallas as pl
from jax.experimental.pallas import tpu as pltpu

F32 = jnp.float32
BF16 = jnp.bfloat16

NDEV = 8
EPS = 1e-6
BLOCK = 128
SWA_HEADS, SWA_KV, SWA_DH, SWA_GROUP = 16, 2, 64, 8
REL_BUCKETS, REL_MAX_DIST = 32, 128
MLA_H, Q_RANK, KV_RANK, NOPE, ROPE, VDIM = 8, 384, 128, 128, 64, 128
ROPE_THETA = 10000.0
OFF_K, OFF_V, OFF_CQ, OFF_CKV, OFF_KR, IN_COLS = 1024, 1152, 1280, 1664, 1792, 1856
IN_EXT = IN_COLS + ROPE
TAIL0, TAIL = OFF_CQ, IN_EXT - OFF_CQ
QW = NOPE + 2 * ROPE
MIX = SWA_HEADS * SWA_DH + MLA_H * VDIM
MLA_SCALE = (NOPE + ROPE) ** -0.5
SWA_SCALE = SWA_DH ** -0.5

ADAM_LR, ADAM_B1, ADAM_B2, ADAM_EPS, ADAM_WD, ADAM_STEP = 0.001, 0.9, 0.999, 1e-08, 0.01, 10

VMEM_LIMIT = 52 * 1024 * 1024
ROW_TILE = 256
MM_TM, MM_TN, MM_TK = 1024, 1024, 2048
ATT_T = 512
MLA_HB = 2
ADAM_ELEMS = 256 * 1024


MESH_ID = pl.DeviceIdType.MESH


def _place():
    x, y, c = lax.axis_index("x"), lax.axis_index("y"), lax.axis_index("c")
    return x, y, c, 2 * x + y


def _chip(x, y, k):
    return (1 - x if k & 2 else x, 1 - y if k & 1 else y)


def _dma_sems(*counts):
    return [pltpu.SemaphoreType.DMA((n,)) for n in counts]


class _Gather:
    def __init__(self, arrays):
        self.arrays = list(arrays)
        n = len(self.arrays)
        self.out_shape = [jax.ShapeDtypeStruct((NDEV,) + a.shape, a.dtype) for a in self.arrays]
        self.sems = _dma_sems(7 * n, 7 * n, n)

    def _copy(self, sems, a, k, src, dst, to):
        return pltpu.make_async_remote_copy(src_ref=src, dst_ref=dst, send_sem=sems[0].at[7 * a + k],
                                            recv_sem=sems[1].at[7 * a + k], device_id=to, device_id_type=MESH_ID)

    def start(self, ins, outs, sems):
        x, y, c, q = _place()
        me = 2 * q + c
        for a in range(len(ins)):
            pltpu.make_async_copy(ins[a], outs[a].at[me], sems[2].at[a]).start()
            self._copy(sems, a, 0, ins[a], outs[a].at[me], (x, y, 1 - c)).start()
            for k in (1, 2, 3):
                self._copy(sems, a, k, ins[a], outs[a].at[me], (*_chip(x, y, k), c)).start()

    def relay(self, ins, outs, sems):
        x, y, c, q = _place()
        sib = (x, y, 1 - c)
        for k in (1, 2, 3):
            for a in range(len(ins)):
                blk = outs[a].at[2 * (q ^ k) + c]
                self._copy(sems, a, k, ins[a], blk, (*_chip(x, y, k), c)).wait_recv()
                self._copy(sems, a, 3 + k, blk, blk, sib).start()

    def finish(self, ins, outs, sems):
        x, y, c, q = _place()
        me, sib = 2 * q + c, (x, y, 1 - c)
        n = len(ins)
        for a in range(n):
            self._copy(sems, a, 0, ins[a], outs[a].at[2 * q + 1 - c], sib).wait_recv()
            for k in (1, 2, 3):
                blk = outs[a].at[2 * (q ^ k) + 1 - c]
                self._copy(sems, a, 3 + k, blk, blk, sib).wait_recv()
        for a in range(n):
            for k in range(7):
                self._copy(sems, a, k, ins[a], outs[a].at[me], sib).wait_send()
            pltpu.make_async_copy(ins[a], outs[a].at[me], sems[2].at[a]).wait()


class _PairSwap:
    def __init__(self, arrays):
        self.arrays = list(arrays)
        n = len(self.arrays)
        self.out_shape = [jax.ShapeDtypeStruct((NDEV // 2,) + a.shape[1:], a.dtype) for a in self.arrays]
        self.sems = _dma_sems(4 * n, 4 * n)

    def _copy(self, sems, a, p, src, dst, to):
        return pltpu.make_async_remote_copy(src_ref=src, dst_ref=dst, send_sem=sems[0].at[4 * a + p],
                                            recv_sem=sems[1].at[4 * a + p], device_id=to, device_id_type=MESH_ID)

    def start(self, ins, outs, sems):
        x, y, c, _ = _place()
        for a in range(len(ins)):
            for p in range(4):
                self._copy(sems, a, p, ins[a].at[2 * p + 1 - c], outs[a].at[p], (x, y, 1 - c)).start()

    def finish(self, ins, outs, sems):
        x, y, c, _ = _place()
        for a in range(len(ins)):
            for p in range(4):
                cp = self._copy(sems, a, p, ins[a].at[2 * p + 1 - c], outs[a].at[p], (x, y, 1 - c))
                cp.wait_recv()
                cp.wait_send()


class _ChipScatter:
    def __init__(self, arrays):
        self.arrays = list(arrays)
        n = len(self.arrays)
        self.out_shape = [jax.ShapeDtypeStruct(a.shape, a.dtype) for a in self.arrays]
        self.sems = _dma_sems(3 * n, 3 * n, n)

    def _copy(self, sems, a, k, src, dst, to):
        return pltpu.make_async_remote_copy(src_ref=src, dst_ref=dst, send_sem=sems[0].at[3 * a + k - 1],
                                            recv_sem=sems[1].at[3 * a + k - 1], device_id=to, device_id_type=MESH_ID)

    def start(self, ins, outs, sems):
        x, y, c, q = _place()
        for a in range(len(ins)):
            pltpu.make_async_copy(ins[a].at[q], outs[a].at[q], sems[2].at[a]).start()
            for k in (1, 2, 3):
                self._copy(sems, a, k, ins[a].at[q ^ k], outs[a].at[q], (*_chip(x, y, k), c)).start()

    def finish(self, ins, outs, sems):
        x, y, c, q = _place()
        for a in range(len(ins)):
            for k in (1, 2, 3):
                cp = self._copy(sems, a, k, ins[a].at[q ^ k], outs[a].at[q ^ k], (*_chip(x, y, k), c))
                cp.wait_recv()
                cp.wait_send()
            pltpu.make_async_copy(ins[a].at[q], outs[a].at[q], sems[2].at[a]).wait()


def _call(body, **kw):
    return pl.pallas_call(body, **kw)


def _pcall(body, comm=None, **kw):
    if not comm:
        return _call(body, **kw)
    grid = kw["grid"]
    in_specs, out_specs, out_shape = list(kw["in_specs"]), list(kw["out_specs"]), list(kw["out_shape"])
    scratch = list(kw.get("scratch_shapes", ()))
    n_in, n_out, n_scr = len(in_specs), len(out_shape), len(scratch)
    n_cin = [len(j.arrays) for j in comm]
    n_sem = [len(j.sems) for j in comm]
    n = sum(n_cin)
    hbm = pl.BlockSpec(memory_space=pltpu.HBM)

    def carried(*refs):
        ins, cins = refs[:n_in], refs[n_in:n_in + n]
        outs, couts = refs[n_in + n:n_in + n + n_out], refs[n_in + n + n_out:n_in + 2 * n + n_out]
        scr, sems = refs[n_in + 2 * n + n_out:n_in + 2 * n + n_out + n_scr], refs[n_in + 2 * n + n_out + n_scr:]
        ids = [pl.program_id(ax) for ax in range(len(grid))]
        first = functools.reduce(jnp.logical_and, [i == 0 for i in ids])
        last = functools.reduce(jnp.logical_and, [i == g - 1 for i, g in zip(ids, grid)])

        def each(method):
            ai = si = 0
            for job, na, ns in zip(comm, n_cin, n_sem):
                if hasattr(job, method):
                    getattr(job, method)(cins[ai:ai + na], couts[ai:ai + na], sems[si:si + ns])
                ai, si = ai + na, si + ns

        @pl.when(first)
        def _():
            each("start")

        steps = math.prod(grid)
        if steps > 1:
            at, rest = [], steps - 2
            for g in reversed(grid):
                at.append(rest % g)
                rest //= g
            before_last = functools.reduce(jnp.logical_and, [i == a for i, a in zip(ids, reversed(at))])

            @pl.when(before_last)
            def _():
                each("relay")

        body(*ins, *outs, *scr)

        @pl.when(last)
        def _():
            if steps == 1:
                each("relay")
            each("finish")

    kw.update(in_specs=in_specs + [hbm] * n, out_specs=out_specs + [hbm] * n,
              out_shape=out_shape + [o for j in comm for o in j.out_shape],
              scratch_shapes=scratch + [sm for j in comm for sm in j.sems],
              compiler_params=_cparams(("arbitrary",) * len(grid)))
    call = _call(carried, **kw)
    return lambda *args: call(*args, *[a for j in comm for a in j.arrays])


def _cparams(sem):
    return pltpu.CompilerParams(dimension_semantics=sem, vmem_limit_bytes=VMEM_LIMIT)


def _pick(n, pref, align):
    if n <= pref:
        return n
    t = (pref // align) * align
    while t >= align:
        if n % t == 0:
            return t
        t -= align
    return n


def _split3(x):
    a = x.astype(BF16)
    r = x - a.astype(F32)
    b = r.astype(BF16)
    c = (r - b.astype(F32)).astype(BF16)
    return a, b, c


def _exchange(name, job):
    n = len(job.arrays)

    def body(*refs):
        ins, outs, sems = refs[:n], refs[n:2 * n], refs[2 * n:]
        job.start(ins, outs, sems)
        if hasattr(job, "relay"):
            job.relay(ins, outs, sems)
        job.finish(ins, outs, sems)

    hbm = pl.BlockSpec(memory_space=pltpu.HBM)
    return _call(body, name=name, out_shape=job.out_shape, in_specs=[hbm] * n, out_specs=[hbm] * n,
                 scratch_shapes=job.sems)(*job.arrays)


def _pair_sum(name, g, r, core):
    _, rr, cc = g.shape
    tr = rr if rr * cc <= 4 * ADAM_ELEMS else _pick(rr, max(16, 4 * ADAM_ELEMS // cc // 16 * 16), 16)

    def body(g_ref, r_ref, c_ref, o_ref):
        north = c_ref[:, 0:1] > 0.5
        mine = jnp.where(north, g_ref[1].astype(F32), g_ref[0].astype(F32))
        o_ref[...] = (mine + r_ref[...].astype(F32)).astype(o_ref.dtype)

    return _pcall(
        body, name=name, grid=(NDEV // 2, rr // tr),
        in_specs=[pl.BlockSpec((None, 2, tr, cc), lambda p, i: (p, 0, i, 0)),
                  pl.BlockSpec((None, tr, cc), lambda p, i: (p, i, 0)), pl.BlockSpec((1, 128), lambda p, i: (0, 0))],
        out_specs=pl.BlockSpec((None, tr, cc), lambda p, i: (p, i, 0)),
        out_shape=jax.ShapeDtypeStruct((NDEV // 2, rr, cc), g.dtype),
        compiler_params=_cparams(("parallel", "parallel")))(g.reshape(NDEV // 2, 2, rr, cc), r, core)


_DIMS = {"nn": (((1,), (0,)), ((), ())), "nt": (((1,), (1,)), ((), ())), "tn": (((0,), (0,)), ((), ()))}


def _mm(name, a, b, kind, grid, a_spec, b_spec, out_shape, out_specs, acc_shape, epilogue,
        extras=(), extra_specs=(), comm=None, b_parts=1):
    nk, ne, no = grid[2], len(extras), len(out_shape)

    def body(*refs):
        a_ref, b_ref = refs[0], refs[1]
        ex, outs = refs[2:2 + ne], refs[2 + ne:2 + ne + no]
        if b_parts == 1:
            part = lax.dot_general(a_ref[...].astype(BF16), b_ref[...].astype(BF16), _DIMS[kind],
                                   preferred_element_type=F32)
        else:
            kp = a_ref.shape[1] // b_parts
            part = sum(lax.dot_general(a_ref[:, p * kp:(p + 1) * kp].astype(BF16), b_ref[p].astype(BF16), _DIMS[kind],
                                       preferred_element_type=F32) for p in range(b_parts))
        if nk == 1:
            epilogue(part, ex, outs)
            return
        acc = refs[-1]
        k = pl.program_id(2)

        @pl.when(k == 0)
        def _():
            acc[...] = part

        @pl.when(jnp.logical_and(k > 0, k < nk - 1))
        def _():
            acc[...] += part

        @pl.when(k == nk - 1)
        def _():
            epilogue(acc[...] + part, ex, outs)

    return _pcall(
        body, comm=comm, name=name, grid=grid, in_specs=[a_spec, b_spec, *extra_specs], out_specs=out_specs,
        out_shape=out_shape, scratch_shapes=[pltpu.VMEM(acc_shape, F32)] if nk > 1 else [],
        compiler_params=_cparams(("parallel", "parallel", "arbitrary")),
    )(a, b, *extras)


def _store(dtype):
    def epi(acc, ex, outs):
        outs[0][...] = acc.astype(dtype)
    return epi


def _mm_plain(name, a, b, kind, m, n, k, out_dtype, tm=None, tn=None, tk=None):
    tm = _pick(m, tm or MM_TM, 128)
    tn = _pick(n, tn or MM_TN, 128)
    tk = _pick(k, tk or MM_TK, 128)
    a_spec = pl.BlockSpec((tk, tm), lambda i, j, q: (q, i)) if kind == "tn" else pl.BlockSpec((tm, tk), lambda i, j, q: (i, q))
    b_spec = pl.BlockSpec((tn, tk), lambda i, j, q: (j, q)) if kind == "nt" else pl.BlockSpec((tk, tn), lambda i, j, q: (q, j))
    return _mm(name, a, b, kind, (m // tm, n // tn, k // tk), a_spec, b_spec,
               [jax.ShapeDtypeStruct((m, n), out_dtype)], [pl.BlockSpec((tm, tn), lambda i, j, q: (i, j))],
               (tm, tn), _store(out_dtype))[0]


def _row(ts, d):
    return pl.BlockSpec((ts, d), lambda i: (i, 0))


def _vec(d):
    return pl.BlockSpec((1, d), lambda i: (0, 0))


def _norm_mod(name, x, gain, sc, sh, comm=None):
    s, d = x.shape
    ts = _pick(s, ROW_TILE, 16)

    def body(x_ref, g_ref, sc_ref, sh_ref, h_ref):
        xv = x_ref[...]
        r = lax.rsqrt(jnp.mean(xv * xv, axis=-1, keepdims=True) + EPS)
        h_ref[...] = ((xv * r) * g_ref[...] * (1.0 + sc_ref[...]) + sh_ref[...]).astype(BF16)

    return _pcall(body, comm=comm, name=name, grid=(s // ts,), in_specs=[_row(ts, d), _vec(d), _vec(d), _vec(d)],
                  out_specs=[_row(ts, d)], out_shape=[jax.ShapeDtypeStruct((s, d), BF16)],
                  compiler_params=_cparams(("parallel",)))(x, gain, sc, sh)


def _loss_head(x3, tgt, y2, gf, g2):
    s, d = x3.shape
    ts = _pick(s, ROW_TILE, 16)

    def body(x_ref, t_ref, y_ref, gf_ref, g2_ref, dx_ref, dy_ref, loss_ref, dgf_ref, dg2_ref):
        @pl.when(pl.program_id(0) == 0)
        def _():
            loss_ref[...] = jnp.zeros_like(loss_ref)
            dgf_ref[...] = jnp.zeros_like(dgf_ref)
            dg2_ref[...] = jnp.zeros_like(dg2_ref)

        xv = x_ref[...]
        r = lax.rsqrt(jnp.mean(xv * xv, axis=-1, keepdims=True) + EPS)
        xn = xv * r
        err = xn * gf_ref[...] - t_ref[...]
        loss_ref[...] += 0.5 * jnp.sum(jnp.mean(err * err, axis=-1, keepdims=True), axis=0, keepdims=True)
        dout = err * (1.0 / d)
        dgf_ref[...] += jnp.sum(dout * xn, axis=0, keepdims=True)
        dxn = dout * gf_ref[...]
        dx = r * (dxn - xn * jnp.mean(dxn * xn, axis=-1, keepdims=True))
        dx_ref[...] = dx
        dy_ref[...] = (dx * g2_ref[...]).astype(BF16)
        dg2_ref[...] += jnp.sum(dx * y_ref[...], axis=0, keepdims=True)

    one = pl.BlockSpec((1, 1), lambda i: (0, 0))
    return _pcall(
        body, name="loss_head", grid=(s // ts,),
        in_specs=[_row(ts, d), _row(ts, d), _row(ts, d), _vec(d), _vec(d)],
        out_specs=[_row(ts, d), _row(ts, d), one, _vec(d), _vec(d)],
        out_shape=[jax.ShapeDtypeStruct((s, d), F32), jax.ShapeDtypeStruct((s, d), BF16),
                   jax.ShapeDtypeStruct((1, 1), F32), jax.ShapeDtypeStruct((1, d), F32),
                   jax.ShapeDtypeStruct((1, d), F32)],
        compiler_params=_cparams(("arbitrary",)))(x3, tgt, y2, gf, g2)


def _norm_mod_bwd(name, x, dh, dres, gain, sc, y_prev=None, gate=None, comm=None):
    s, d = x.shape
    ts = _pick(s, ROW_TILE, 16)
    gated = y_prev is not None

    def body(*refs):
        if gated:
            x_ref, dh_ref, dr_ref, g_ref, sc_ref, y_ref, gt_ref, dx_ref, dy_ref, dsc_ref, dsh_ref, dg_ref, dgt_ref = refs
        else:
            x_ref, dh_ref, dr_ref, g_ref, sc_ref, dx_ref, dsc_ref, dsh_ref, dg_ref = refs

        @pl.when(pl.program_id(0) == 0)
        def _():
            dsc_ref[...] = jnp.zeros_like(dsc_ref)
            dsh_ref[...] = jnp.zeros_like(dsh_ref)
            dg_ref[...] = jnp.zeros_like(dg_ref)
            if gated:
                dgt_ref[...] = jnp.zeros_like(dgt_ref)

        xv, dhv = x_ref[...], dh_ref[...]
        r = lax.rsqrt(jnp.mean(xv * xv, axis=-1, keepdims=True) + EPS)
        xn = xv * r
        dsc_ref[...] += jnp.sum(dhv * (xn * g_ref[...]), axis=0, keepdims=True)
        dsh_ref[...] += jnp.sum(dhv, axis=0, keepdims=True)
        da = dhv * (1.0 + sc_ref[...])
        dg_ref[...] += jnp.sum(da * xn, axis=0, keepdims=True)
        dxn = da * g_ref[...]
        dx = dr_ref[...] + r * (dxn - xn * jnp.mean(dxn * xn, axis=-1, keepdims=True))
        dx_ref[...] = dx
        if gated:
            dy_ref[...] = (dx * gt_ref[...]).astype(BF16)
            dgt_ref[...] += jnp.sum(dx * y_ref[...], axis=0, keepdims=True)

    ins = [x, dh, dres, gain, sc] + ([y_prev, gate] if gated else [])
    in_specs = [_row(ts, d)] * 3 + [_vec(d)] * 2 + ([_row(ts, d), _vec(d)] if gated else [])
    vec_out = jax.ShapeDtypeStruct((1, d), F32)
    out_shape = [jax.ShapeDtypeStruct((s, d), F32)] + ([jax.ShapeDtypeStruct((s, d), BF16)] if gated else [])
    out_shape += [vec_out] * (4 if gated else 3)
    out_specs = [_row(ts, d)] * (2 if gated else 1) + [_vec(d)] * (4 if gated else 3)
    return _pcall(body, comm=comm, name=name, grid=(s // ts,), in_specs=in_specs, out_specs=out_specs,
                  out_shape=out_shape, compiler_params=_cparams(("arbitrary",)))(*ins)


def _dot3(a, b, dims):
    a1, a2, _ = _split3(a)
    b1, b2, _ = _split3(b)
    dot = functools.partial(lax.dot_general, dimension_numbers=dims, preferred_element_type=F32)
    return dot(a1, b1) + (dot(a1, b2) + dot(a2, b1))


def _mod_fwd(c_all, w, b_cols, comm=None):
    nb, d = c_all.shape
    n = w.shape[1]
    tk = _pick(d, 512, 128)
    nk = d // tk

    def body(c_ref, w_ref, b_ref, act_ref, out_ref):
        k = pl.program_id(0)
        cv = c_ref[...]
        act = cv * (1.0 / (1.0 + jnp.exp(-cv)))
        act_ref[...] = act

        @pl.when(k == 0)
        def _():
            out_ref[...] = jnp.broadcast_to(b_ref[...], out_ref.shape)

        out_ref[...] += _dot3(act, w_ref[...], _DIMS["nn"])

    return _pcall(
        body, comm=comm, name="mod_fwd", grid=(nk,),
        in_specs=[pl.BlockSpec((nb, tk), lambda k: (0, k)), pl.BlockSpec((tk, n), lambda k: (k, 0)),
                  pl.BlockSpec((1, n), lambda k: (0, 0))],
        out_specs=[pl.BlockSpec((nb, tk), lambda k: (0, k)), pl.BlockSpec((nb, n), lambda k: (0, 0))],
        out_shape=[jax.ShapeDtypeStruct((nb, d), F32), jax.ShapeDtypeStruct((nb, n), F32)],
        compiler_params=_cparams(("arbitrary",)))(c_all, w, b_cols)


def _mod_wgrad(act_all, dmod_cols):
    nb, d = act_all.shape
    n = dmod_cols.shape[1]
    tm = _pick(d, 512, 128)

    def body(a_ref, d_ref, o_ref):
        o_ref[...] = _dot3(a_ref[...], d_ref[...], _DIMS["tn"])

    return _pcall(
        body, name="mod_wgrad", grid=(d // tm,),
        in_specs=[pl.BlockSpec((nb, tm), lambda i: (0, i)), pl.BlockSpec((nb, n), lambda i: (0, 0))],
        out_specs=pl.BlockSpec((tm, n), lambda i: (i, 0)), out_shape=jax.ShapeDtypeStruct((d, n), F32),
        compiler_params=_cparams(("parallel",)))(act_all, dmod_cols)


def _bias_expand(rel_t, onehot_t):
    h, _ = rel_t.shape
    n = onehot_t.shape[1]

    def body(r_ref, o_ref, out_ref):
        a, b, c = _split3(r_ref[...])
        dot = functools.partial(lax.dot_general, dimension_numbers=_DIMS["nn"], preferred_element_type=F32)
        oh = o_ref[...]
        out_ref[...] = dot(a, oh) + (dot(b, oh) + dot(c, oh))

    full = lambda shp: pl.BlockSpec(shp, lambda: (0,) * len(shp))
    return _pcall(body, name="bias_expand", in_specs=[full(rel_t.shape), full(onehot_t.shape)],
                  out_specs=full((h, n)), out_shape=jax.ShapeDtypeStruct((h, n), F32),
                  compiler_params=pltpu.CompilerParams(vmem_limit_bytes=VMEM_LIMIT))(rel_t, onehot_t)


def _bias_reduce(dbias, onehot, dsink_rows):
    h, n = dbias.shape

    def body(d_ref, o_ref, s_ref, out_ref, so_ref):
        a, b, c = _split3(d_ref[...])
        dot = functools.partial(lax.dot_general, dimension_numbers=_DIMS["nn"], preferred_element_type=F32)
        oh = o_ref[...]
        out_ref[...] = dot(a, oh) + (dot(b, oh) + dot(c, oh))
        so_ref[...] = jnp.sum(s_ref[...], axis=-1, keepdims=True)

    full = lambda shp: pl.BlockSpec(shp, lambda: (0,) * len(shp))
    return _pcall(body, name="bias_reduce", in_specs=[full(dbias.shape), full(onehot.shape), full(dsink_rows.shape)],
                  out_specs=[full((h, REL_BUCKETS)), full((h, 1))],
                  out_shape=[jax.ShapeDtypeStruct((h, REL_BUCKETS), F32), jax.ShapeDtypeStruct((h, 1), F32)],
                  compiler_params=pltpu.CompilerParams(vmem_limit_bytes=VMEM_LIMIT))(dbias, onehot, dsink_rows)


PAIRS = SWA_GROUP // 2
PROWS = PAIRS * BLOCK
PCOLS = 2 * 2 * BLOCK


def _swa2_specs():
    tok = lambda width: pl.BlockSpec((BLOCK, width), lambda g, n: (n, g))
    prev = pl.BlockSpec((None, BLOCK, 2 * SWA_DH), lambda g, n: (g, jnp.maximum(n - 1, 0), 0))
    cur = pl.BlockSpec((None, BLOCK, 2 * SWA_DH), lambda g, n: (g, n, 0))
    bias_spec = pl.BlockSpec((None, PROWS, PCOLS), lambda g, n: (g, 0, 0))
    col_spec = pl.BlockSpec((None, PROWS, 1), lambda g, n: (g, 0, 0))
    lse_spec = pl.BlockSpec((None, None, PROWS, 1), lambda g, n: (g, n, 0, 0))
    return tok, prev, cur, bias_spec, col_spec, lse_spec


def _stack_pairs(blk):
    return jnp.concatenate([blk[:, p * 2 * SWA_DH:(p + 1) * 2 * SWA_DH] for p in range(PAIRS)], axis=0)


def _band(tp, tc, bp, bc):
    return jnp.concatenate([tp[...], tc[...], bp[...], bc[...]], axis=0)


def _swa2_scores(q_ref, kd, bias_ref, n):
    q2 = _stack_pairs(q_ref[...])
    s2 = lax.dot_general(q2, kd, _DIMS["nt"], preferred_element_type=F32) * SWA_SCALE + bias_ref[...]
    col = lax.broadcasted_iota(jnp.int32, s2.shape, 1)
    before_start = jnp.logical_and(n == 0, (col & (2 * BLOCK - 1)) < BLOCK)
    return q2, jnp.where(before_start, -jnp.inf, s2)


def _swa2_fwd(src, ktop, kbot, vtop, vbot, bias, sinks, comm=None):
    s = src.shape[0]
    nb = s // BLOCK
    tok, prev, cur, bias_spec, col_spec, lse_spec = _swa2_specs()

    def body(q_ref, ktp, ktc, kbp, kbc, vtp, vtc, vbp, vbc, bias_ref, sa_ref, sb_ref, o_ref, la_ref, lb_ref):
        n = pl.program_id(1)
        _, s2 = _swa2_scores(q_ref, _band(ktp, ktc, kbp, kbc), bias_ref, n)
        row = lax.broadcasted_iota(jnp.int32, (PCOLS, 2 * SWA_DH), 0)
        lane = lax.broadcasted_iota(jnp.int32, (PCOLS, 2 * SWA_DH), 1)
        ones = jnp.where(lane == row // (2 * BLOCK), 1.0, 0.0).astype(BF16)
        ps, ms, sinks_ = [], [], []
        for half, sink_ref in enumerate((sa_ref, sb_ref)):
            sc = s2[:, half * 2 * BLOCK:(half + 1) * 2 * BLOCK]
            m = jnp.maximum(jnp.max(sc, axis=-1, keepdims=True), sink_ref[...])
            ps.append(jnp.exp(sc - m).astype(BF16))
            ms.append(m)
        acc = lax.dot_general(jnp.concatenate(ps, axis=1), jnp.concatenate([_band(vtp, vtc, vbp, vbc), ones], axis=1),
                              _DIMS["nn"], preferred_element_type=F32)
        dens = []
        for half, (sink_ref, lse_ref) in enumerate(((sa_ref, la_ref), (sb_ref, lb_ref))):
            den = acc[:, 2 * SWA_DH + half:2 * SWA_DH + half + 1] + jnp.exp(sink_ref[...] - ms[half])
            lse_ref[...] = ms[half] + jnp.log(den)
            dens.append(den)
        lo = lax.broadcasted_iota(jnp.int32, (PROWS, 2 * SWA_DH), 1) < SWA_DH
        o2 = acc[:, 0:2 * SWA_DH] / jnp.where(lo, dens[0], dens[1])
        for p in range(PAIRS):
            o_ref[:, p * 2 * SWA_DH:(p + 1) * 2 * SWA_DH] = o2[p * BLOCK:(p + 1) * BLOCK].astype(BF16)

    lse_shape = jax.ShapeDtypeStruct((SWA_KV, nb, PROWS, 1), F32)
    return _pcall(
        body, comm=comm, name="swa_fwd", grid=(SWA_KV, nb),
        in_specs=[tok(PROWS), prev, cur, prev, cur, prev, cur, prev, cur, bias_spec, col_spec, col_spec],
        out_specs=[tok(PROWS), lse_spec, lse_spec],
        out_shape=[jax.ShapeDtypeStruct((s, MIX), BF16), lse_shape, lse_shape],
        compiler_params=_cparams(("parallel", "parallel")))(
            src, ktop, ktop, kbot, kbot, vtop, vtop, vbot, vbot, bias, sinks[0], sinks[1])


def _swa2_bwd(src, dsrc, ktop, kbot, vtop, vbot, lses, bias, sinks, comm=None):
    s = src.shape[0]
    nb = s // BLOCK
    tok, prev, cur, bias_spec, col_spec, lse_spec = _swa2_specs()
    lane_lo = lambda shape: lax.broadcasted_iota(jnp.int32, shape, 1) < SWA_DH

    def body(q_ref, do_ref, ktp, ktc, kbp, kbc, vtp, vtc, vbp, vbc, la_ref, lb_ref, bias_ref, sa_ref, sb_ref,
             dq_ref, dkp_ref, dkc_ref, dvp_ref, dvc_ref, dbias_ref, dsa_ref, dsb_ref):
        n = pl.program_id(1)

        @pl.when(n == 0)
        def _():
            dbias_ref[...] = jnp.zeros_like(dbias_ref)
            dsa_ref[...] = jnp.zeros_like(dsa_ref)
            dsb_ref[...] = jnp.zeros_like(dsb_ref)

        kd = _band(ktp, ktc, kbp, kbc)
        q2, s2 = _swa2_scores(q_ref, kd, bias_ref, n)
        do2 = _stack_pairs(do_ref[...]).astype(BF16)
        dp2 = lax.dot_general(do2, _band(vtp, vtc, vbp, vbc), _DIMS["nt"], preferred_element_type=F32)
        ps, dss = [], []
        for half, (sink_ref, lse_ref, dsink_ref) in enumerate(((sa_ref, la_ref, dsa_ref), (sb_ref, lb_ref, dsb_ref))):
            cols = slice(half * 2 * BLOCK, (half + 1) * 2 * BLOCK)
            lse_v = lse_ref[...]
            p = jnp.exp(s2[:, cols] - lse_v)
            dp = dp2[:, cols]
            delta = jnp.sum(p * dp, axis=-1, keepdims=True)
            ds = p * (dp - delta)
            dsink_ref[...] += -jnp.exp(sink_ref[...] - lse_v) * delta
            ps.append(p.astype(BF16))
            dss.append(ds)
        ds2 = jnp.concatenate(dss, axis=1)
        dbias_ref[...] += ds2
        dsb2 = (ds2 * SWA_SCALE).astype(BF16)
        dq2 = lax.dot_general(dsb2, kd, _DIMS["nn"], preferred_element_type=F32)
        for p in range(PAIRS):
            dq_ref[:, p * 2 * SWA_DH:(p + 1) * 2 * SWA_DH] = dq2[p * BLOCK:(p + 1) * BLOCK].astype(BF16)
        dk = lax.dot_general(dsb2, q2, _DIMS["tn"], preferred_element_type=F32)
        dv = lax.dot_general(jnp.concatenate(ps, axis=1), do2, _DIMS["tn"], preferred_element_type=F32)
        for full, prev_ref, cur_ref in ((dk, dkp_ref, dkc_ref), (dv, dvp_ref, dvc_ref)):
            own = jnp.where(lane_lo((2 * BLOCK, 2 * SWA_DH)), full[:2 * BLOCK], full[2 * BLOCK:])
            prev_ref[...] = own[:BLOCK]
            cur_ref[...] = own[BLOCK:]

    kv_out = jax.ShapeDtypeStruct((SWA_KV, s, 2 * SWA_DH), F32)
    col_out = jax.ShapeDtypeStruct((SWA_KV, PROWS, 1), F32)
    return _pcall(
        body, comm=comm, name="swa_bwd", grid=(SWA_KV, nb),
        in_specs=[tok(PROWS), tok(PROWS), prev, cur, prev, cur, prev, cur, prev, cur, lse_spec, lse_spec, bias_spec,
                  col_spec, col_spec],
        out_specs=[tok(PROWS), cur, cur, cur, cur, bias_spec, col_spec, col_spec],
        out_shape=[jax.ShapeDtypeStruct((s, IN_EXT), BF16), kv_out, kv_out, kv_out, kv_out,
                   jax.ShapeDtypeStruct(bias.shape, F32), col_out, col_out],
        compiler_params=_cparams(("arbitrary", "arbitrary")))(
            src, dsrc, ktop, ktop, kbot, kbot, vtop, vtop, vbot, vbot, lses[0], lses[1], bias, sinks[0], sinks[1])


def _rope_slab(slab, table):
    t = slab * table
    return t + pltpu.roll(t, ROPE, 1)


def _low_lanes(v):
    lane = lax.broadcasted_iota(jnp.int32, v.shape, 1)
    return jnp.where(lane < ROPE, v, 0.0)


def _rms(xv, g):
    r = lax.rsqrt(jnp.mean(xv * xv, axis=-1, keepdims=True) + EPS)
    return xv * r, r


def _mla_prep(proj, gq, gkv, table):
    s = proj.shape[0]
    ts = _pick(s, ROW_TILE, 16)

    def body(p_ref, gq_ref, gkv_ref, t_ref, cq_ref, ckv_ref, kr_ref):
        xq, _ = _rms(p_ref[:, 0:Q_RANK], None)
        cq_ref[...] = (xq * gq_ref[...]).astype(BF16)
        xkv, _ = _rms(p_ref[:, Q_RANK:Q_RANK + KV_RANK], None)
        ckv_ref[...] = (xkv * gkv_ref[...]).astype(BF16)
        kr_ref[...] = _low_lanes(_rope_slab(p_ref[:, Q_RANK + KV_RANK:TAIL], t_ref[...]))

    return _pcall(
        body, name="mla_prep", grid=(s // ts,),
        in_specs=[pl.BlockSpec((ts, TAIL), lambda i: (i, TAIL0 // TAIL)), _vec(Q_RANK), _vec(KV_RANK), _row(ts, 2 * ROPE)],
        out_specs=[_row(ts, Q_RANK), _row(ts, KV_RANK), _row(ts, 2 * ROPE)],
        out_shape=[jax.ShapeDtypeStruct((s, Q_RANK), BF16), jax.ShapeDtypeStruct((s, KV_RANK), BF16),
                   jax.ShapeDtypeStruct((s, 2 * ROPE), F32)],
        compiler_params=_cparams(("parallel",)))(proj, gq, gkv, table)


def _mla_prep_bwd(proj, dcq, dckv, dkr, gq, gkv, table, dproj):
    s = proj.shape[0]
    ts = _pick(s, ROW_TILE, 16)

    def norm_bwd(xv, dy, g):
        xn, r = _rms(xv, None)
        dg = jnp.sum(dy * xn, axis=0, keepdims=True)
        dxn = dy * g
        return r * (dxn - xn * jnp.mean(dxn * xn, axis=-1, keepdims=True)), dg

    def body(p_ref, dcq_ref, dckv_ref, dkr_ref, gq_ref, gkv_ref, t_ref, _, dt_ref, dgq_ref, dgkv_ref):
        @pl.when(pl.program_id(0) == 0)
        def _():
            dgq_ref[...] = jnp.zeros_like(dgq_ref)
            dgkv_ref[...] = jnp.zeros_like(dgkv_ref)

        dxq, dgq = norm_bwd(p_ref[:, 0:Q_RANK], dcq_ref[...], gq_ref[...])
        dxkv, dgkv = norm_bwd(p_ref[:, Q_RANK:Q_RANK + KV_RANK], dckv_ref[...], gkv_ref[...])
        dgq_ref[...] += dgq
        dgkv_ref[...] += dgkv
        d = _low_lanes(dkr_ref[...])
        dslab = (d + pltpu.roll(d, ROPE, 1)) * t_ref[...]
        dt_ref[:, 0:Q_RANK] = dxq.astype(BF16)
        dt_ref[:, Q_RANK:Q_RANK + KV_RANK] = dxkv.astype(BF16)
        dt_ref[:, Q_RANK + KV_RANK:TAIL] = dslab.astype(BF16)

    return _pcall(
        body, name="mla_prep_bwd", grid=(s // ts,),
        in_specs=[pl.BlockSpec((ts, TAIL), lambda i: (i, TAIL0 // TAIL)), _row(ts, Q_RANK), _row(ts, KV_RANK),
                  _row(ts, 2 * ROPE), _vec(Q_RANK), _vec(KV_RANK), _row(ts, 2 * ROPE), pl.BlockSpec(memory_space=pl.ANY)],
        out_specs=[pl.BlockSpec((ts, TAIL), lambda i: (i, TAIL0 // TAIL)), _vec(Q_RANK), _vec(KV_RANK)],
        out_shape=[jax.ShapeDtypeStruct(dproj.shape, BF16), jax.ShapeDtypeStruct((1, Q_RANK), F32),
                   jax.ShapeDtypeStruct((1, KV_RANK), F32)],
        input_output_aliases={7: 0},
        compiler_params=_cparams(("arbitrary",)))(proj, dcq, dckv, dkr, gq, gkv, table, dproj)


def _head_specs(ts):
    tok = lambda w: pl.BlockSpec((ts, w), lambda h, i: (i, 0))
    head = lambda w: pl.BlockSpec((None, ts, w), lambda h, i: (h, i, 0))
    wgt = lambda r, c: pl.BlockSpec((None, r, c), lambda h, i: (h, 0, 0))
    return tok, head, wgt


def _mla_qkv(cq, ckv, kr, wq, wkv, table):
    s = cq.shape[0]
    ts = _pick(s, 4 * ROW_TILE, 16)
    tok, head, wgt = _head_specs(ts)

    def body(cq_ref, ckv_ref, kr_ref, wq_ref, wkv_ref, t_ref, q_ref, k_ref, v_ref):
        qf = lax.dot_general(cq_ref[...], wq_ref[...], _DIMS["nn"], preferred_element_type=F32)
        q_ref[:, 0:NOPE] = qf[:, 0:NOPE].astype(BF16)
        q_ref[:, NOPE:QW] = _rope_slab(qf[:, NOPE:QW], t_ref[...]).astype(BF16)
        kv = lax.dot_general(ckv_ref[...], wkv_ref[...], _DIMS["nn"], preferred_element_type=F32)
        k_ref[:, 0:NOPE] = kv[:, 0:NOPE].astype(BF16)
        k_ref[:, NOPE:QW] = kr_ref[...].astype(BF16)
        v_ref[:, 0:VDIM] = kv[:, NOPE:NOPE + VDIM].astype(BF16)
        lane = lax.broadcasted_iota(jnp.int32, (ts, VDIM), 1)
        v_ref[:, VDIM:2 * VDIM] = jnp.where(lane == 0, 1.0, 0.0).astype(BF16)

    return _pcall(
        body, name="mla_qkv", grid=(MLA_H, s // ts),
        in_specs=[tok(Q_RANK), tok(KV_RANK), tok(2 * ROPE), wgt(Q_RANK, QW), wgt(KV_RANK, NOPE + VDIM), tok(2 * ROPE)],
        out_specs=[head(QW), head(QW), head(2 * VDIM)],
        out_shape=[jax.ShapeDtypeStruct((MLA_H, s, QW), BF16), jax.ShapeDtypeStruct((MLA_H, s, QW), BF16),
                   jax.ShapeDtypeStruct((MLA_H, s, 2 * VDIM), BF16)],
        compiler_params=_cparams(("parallel", "parallel")))(cq, ckv, kr, wq, wkv, table)


def _mla_qkv_bwd(dq, dk, dv, cq, ckv, wq, wkv, table, comm=None):
    s = cq.shape[0]
    ts = _pick(s, 4 * ROW_TILE, 16)
    tok, head, wgt = _head_specs(ts)
    whole = lambda w: pl.BlockSpec((s, w), lambda h, i: (0, 0))

    def body(dq_ref, dk_ref, dv_ref, cq_ref, ckv_ref, wq_ref, wkv_ref, t_ref,
             dcq_ref, dckv_ref, dkr_ref, gwq_ref, gwkv_ref):
        h, i = pl.program_id(0), pl.program_id(1)
        rows = pl.ds(pl.multiple_of(i * ts, ts), ts)
        d = dq_ref[:, NOPE:QW]
        dslab = (d + pltpu.roll(d, ROPE, 1)) * t_ref[...]
        dqe = jnp.concatenate([dq_ref[:, 0:NOPE], dslab], axis=1).astype(BF16)
        dkv = jnp.concatenate([dk_ref[:, 0:NOPE], dv_ref[...]], axis=1).astype(BF16)
        dcq = lax.dot_general(dqe, wq_ref[...], _DIMS["nt"], preferred_element_type=F32)
        dckv = lax.dot_general(dkv, wkv_ref[...], _DIMS["nt"], preferred_element_type=F32)
        gwq = lax.dot_general(cq_ref[...], dqe, _DIMS["tn"], preferred_element_type=F32)
        gwkv = lax.dot_general(ckv_ref[...], dkv, _DIMS["tn"], preferred_element_type=F32)
        dkr = dk_ref[:, NOPE:QW].astype(F32)

        @pl.when(h == 0)
        def _():
            dcq_ref[rows, :] = dcq
            dckv_ref[rows, :] = dckv
            dkr_ref[rows, :] = dkr

        @pl.when(h > 0)
        def _():
            dcq_ref[rows, :] += dcq
            dckv_ref[rows, :] += dckv
            dkr_ref[rows, :] += dkr

        @pl.when(i == 0)
        def _():
            gwq_ref[...] = gwq
            gwkv_ref[...] = gwkv

        @pl.when(i > 0)
        def _():
            gwq_ref[...] += gwq
            gwkv_ref[...] += gwkv

    return _pcall(
        body, comm=comm, name="mla_qkv_bwd", grid=(MLA_H, s // ts),
        in_specs=[head(QW), head(QW), head(VDIM), tok(Q_RANK), tok(KV_RANK), wgt(Q_RANK, QW),
                  wgt(KV_RANK, NOPE + VDIM), tok(2 * ROPE)],
        out_specs=[whole(Q_RANK), whole(KV_RANK), whole(2 * ROPE), wgt(Q_RANK, QW), wgt(KV_RANK, NOPE + VDIM)],
        out_shape=[jax.ShapeDtypeStruct((s, Q_RANK), F32), jax.ShapeDtypeStruct((s, KV_RANK), F32),
                   jax.ShapeDtypeStruct((s, 2 * ROPE), F32), jax.ShapeDtypeStruct((MLA_H, Q_RANK, QW), F32),
                   jax.ShapeDtypeStruct((MLA_H, KV_RANK, NOPE + VDIM), F32)],
        compiler_params=_cparams(("arbitrary", "arbitrary")))(dq, dk, dv, cq, ckv, wq, wkv, table)


def _as_row(col):
    return jnp.broadcast_to(col, (col.shape[0], 128)).T[0:1, :]


def _diag_mask(t):
    return lax.broadcasted_iota(jnp.int32, (t, t), 1) <= lax.broadcasted_iota(jnp.int32, (t, t), 0)


def _mla_fwd(q, k, v, mix, comm=None):
    s = q.shape[1]
    t = _pick(s, ATT_T, 128)
    nt = s // t
    assert nt % 2 == 0
    hb = MLA_H

    def fold(p, u):
        first = u <= p
        return jnp.where(first, p, nt - 1 - p), jnp.where(first, u, u - p - 1)

    to_log2 = MLA_SCALE * math.log2(math.e)

    def body(q_ref, k_ref, v_ref, _, o_ref, oh_ref, lse_ref, m_ref, acc_ref):
        i, j = fold(pl.program_id(1), pl.program_id(2))

        @pl.when(j == 0)
        def _():
            m_ref[...] = jnp.full_like(m_ref, -jnp.inf)
            acc_ref[...] = jnp.zeros_like(acc_ref)

        def step(diagonal):
            for h in range(hb):
                sc = lax.dot_general(q_ref[h], k_ref[h], _DIMS["nt"], preferred_element_type=F32)
                if diagonal:
                    sc = jnp.where(_diag_mask(t), sc, -jnp.inf)
                m_old = m_ref[h]
                m_new = jnp.maximum(m_old, jnp.max(sc, axis=-1, keepdims=True))
                alpha = jnp.exp2((m_old - m_new) * to_log2)
                p = jnp.exp2((sc - m_new) * to_log2)
                acc_ref[h] = alpha * acc_ref[h] + lax.dot_general(p.astype(BF16), v_ref[h], _DIMS["nn"],
                                                                  preferred_element_type=F32)
                m_ref[h] = m_new

        @pl.when(j < i)
        def _():
            step(False)

        @pl.when(j == i)
        def _():
            step(True)
            for h in range(hb):
                den = acc_ref[h, :, VDIM:VDIM + 1]
                o = acc_ref[h, :, 0:VDIM] / den
                o_ref[:, h * VDIM:(h + 1) * VDIM] = o
                oh_ref[:, h * VDIM:(h + 1) * VDIM] = o.astype(BF16)
                lse_ref[h] = _as_row(m_ref[h] * MLA_SCALE + jnp.log(den))

    o_spec = pl.BlockSpec((t, hb * VDIM), lambda h, p, u: (fold(p, u)[0], h))
    first = (MIX - MLA_H * VDIM) // (hb * VDIM)
    mix_spec = pl.BlockSpec((t, hb * VDIM), lambda h, p, u: (fold(p, u)[0], first + h))
    return _pcall(
        body, comm=comm, name="mla_fwd", grid=(MLA_H // hb, nt // 2, nt + 1),
        in_specs=[pl.BlockSpec((hb, t, QW), lambda h, p, u: (h, fold(p, u)[0], 0)),
                  pl.BlockSpec((hb, t, QW), lambda h, p, u: (h, fold(p, u)[1], 0)),
                  pl.BlockSpec((hb, t, 2 * VDIM), lambda h, p, u: (h, fold(p, u)[1], 0)),
                  pl.BlockSpec(memory_space=pl.ANY)],
        out_specs=[o_spec, mix_spec, pl.BlockSpec((hb, 1, t), lambda h, p, u: (h, 0, fold(p, u)[0]))],
        out_shape=[jax.ShapeDtypeStruct((s, MLA_H * VDIM), F32), jax.ShapeDtypeStruct(mix.shape, BF16),
                   jax.ShapeDtypeStruct((MLA_H, 1, s), F32)],
        input_output_aliases={3: 1},
        scratch_shapes=[pltpu.VMEM((hb, t, 1), F32), pltpu.VMEM((hb, t, 2 * VDIM), F32)],
        compiler_params=_cparams(("parallel", "parallel", "arbitrary")))(q, k, v, mix)


def _mla_delta(dmix, o):
    s = o.shape[0]
    ts = _pick(s, 2 * ROW_TILE, 16)
    w = MLA_H * VDIM

    def body(d_ref, o_ref, out_ref):
        prod = d_ref[...] * o_ref[...]
        for h in range(MLA_H):
            out_ref[h] = _as_row(jnp.sum(prod[:, h * VDIM:(h + 1) * VDIM], axis=-1, keepdims=True))

    return _pcall(body, name="mla_delta", grid=(s // ts,),
                  in_specs=[pl.BlockSpec((ts, w), lambda i: (i, SWA_HEADS * SWA_DH // w)), pl.BlockSpec((ts, w), lambda i: (i, 0))],
                  out_specs=pl.BlockSpec((MLA_H, 1, ts), lambda i: (0, 0, i)),
                  out_shape=jax.ShapeDtypeStruct((MLA_H, 1, s), F32), compiler_params=_cparams(("parallel",)))(dmix, o)


def _mla_bwd(q, k, v, dmix, delta, lse, comm=None):
    s = q.shape[1]
    t = _pick(s, ATT_T, 128)
    nt = s // t
    assert nt % 2 == 0
    hb = 2 * MLA_HB
    o_blk0 = SWA_HEADS * SWA_DH // (hb * VDIM)

    def fold(p, u):
        first = u < nt - p
        return jnp.where(first, p, nt - 1 - p), jnp.where(first, p + u, u - 1)

    log2e = math.log2(math.e)

    def body(q_ref, k_ref, v_ref, do_ref, delta_ref, lse_ref, dq_ref, dk_ref, dv_ref, dk_acc, dv_acc):
        j, i = fold(pl.program_id(1), pl.program_id(2))
        rows = pl.ds(pl.multiple_of(i * t, t), t)

        @pl.when(i == j)
        def _():
            dk_acc[...] = jnp.zeros_like(dk_acc)
            dv_acc[...] = jnp.zeros_like(dv_acc)

        @pl.when(jnp.logical_and(pl.program_id(1) == 0, pl.program_id(2) == 0))
        def _():
            dq_ref[...] = jnp.zeros_like(dq_ref)

        def step(diagonal):
            for h in range(hb):
                qv, kv_ = q_ref[h], k_ref[h]
                dob = do_ref[:, h * VDIM:(h + 1) * VDIM].astype(BF16)
                st = lax.dot_general(kv_, qv, _DIMS["nt"], preferred_element_type=F32)
                pt = jnp.exp2(st * (MLA_SCALE * log2e) - lse_ref[h] * log2e)
                if diagonal:
                    keep = lax.broadcasted_iota(jnp.int32, (t, t), 0) <= lax.broadcasted_iota(jnp.int32, (t, t), 1)
                    pt = jnp.where(keep, pt, 0.0)
                dpt = lax.dot_general(v_ref[h], dob, _DIMS["nt"], preferred_element_type=F32)
                dst = (pt * (dpt - delta_ref[h]) * MLA_SCALE).astype(BF16)
                dv_acc[h] += lax.dot_general(pt.astype(BF16), dob, _DIMS["nn"], preferred_element_type=F32)
                dk_acc[h] += lax.dot_general(dst, qv, _DIMS["nn"], preferred_element_type=F32)
                dq_ref[h, rows, :] += lax.dot_general(dst, kv_, _DIMS["tn"], preferred_element_type=F32)

        @pl.when(i > j)
        def _():
            step(False)

        @pl.when(i == j)
        def _():
            step(True)

        @pl.when(i == nt - 1)
        def _():
            dk_ref[...] = dk_acc[...].astype(BF16)
            dv_ref[...] = dv_acc[...].astype(BF16)

    qi = lambda h, p, u: (h, fold(p, u)[1], 0)
    kj = lambda h, p, u: (h, fold(p, u)[0], 0)
    row = pl.BlockSpec((hb, 1, t), lambda h, p, u: (h, 0, fold(p, u)[1]))
    return _pcall(
        body, comm=comm, name="mla_bwd", grid=(MLA_H // hb, nt // 2, nt + 1),
        in_specs=[pl.BlockSpec((hb, t, QW), qi), pl.BlockSpec((hb, t, QW), kj), pl.BlockSpec((hb, t, VDIM), kj),
                  pl.BlockSpec((t, hb * VDIM), lambda h, p, u: (fold(p, u)[1], o_blk0 + h)), row, row],
        out_specs=[pl.BlockSpec((hb, s, QW), lambda h, p, u: (h, 0, 0)), pl.BlockSpec((hb, t, QW), kj),
                   pl.BlockSpec((hb, t, VDIM), kj)],
        out_shape=[jax.ShapeDtypeStruct((MLA_H, s, QW), F32), jax.ShapeDtypeStruct((MLA_H, s, QW), BF16),
                   jax.ShapeDtypeStruct((MLA_H, s, VDIM), BF16)],
        scratch_shapes=[pltpu.VMEM((hb, t, QW), F32), pltpu.VMEM((hb, t, VDIM), F32)],
        compiler_params=_cparams(("arbitrary", "arbitrary", "arbitrary")))(q, k, v, dmix, delta, lse)


def _adamw(name, w, g, m, v, parts):
    r, c = w.shape
    n_parts = g.shape[0] if parts else 1
    tr = r if r * c <= ADAM_ELEMS else _pick(r, max(8, ADAM_ELEMS // c // 8 * 8), 8)
    c1 = 1.0 - ADAM_B1 ** ADAM_STEP
    c2 = 1.0 - ADAM_B2 ** ADAM_STEP

    def body(w_ref, g_ref, m_ref, v_ref, go_ref, d_ref, mo_ref, vo_ref):
        if parts:
            gv = g_ref[0].astype(F32)
            for j in range(1, n_parts):
                gv = gv + g_ref[j].astype(F32)
        else:
            gv = g_ref[...]
        mv = ADAM_B1 * m_ref[...] + (1.0 - ADAM_B1) * gv
        vv = ADAM_B2 * v_ref[...] + (1.0 - ADAM_B2) * (gv * gv)
        go_ref[...] = gv
        mo_ref[...] = mv
        vo_ref[...] = vv
        d_ref[...] = -ADAM_LR * ((mv / c1) / (jnp.sqrt(vv / c2) + ADAM_EPS) + ADAM_WD * w_ref[...])

    blk = pl.BlockSpec((tr, c), lambda i: (i, 0))
    g_spec = pl.BlockSpec((n_parts, tr, c), lambda i: (0, i, 0)) if parts else blk
    out = jax.ShapeDtypeStruct((r, c), F32)
    return _pcall(body, name=name, grid=(r // tr,), in_specs=[blk, g_spec, blk, blk], out_specs=[blk] * 4,
                  out_shape=[out] * 4, compiler_params=_cparams(("parallel",)))(w, g, m, v)


def _t5_bucket(dist):
    n = jnp.maximum(dist, 0)
    max_exact = REL_BUCKETS // 2
    nf = jnp.maximum(n, 1).astype(F32)
    large = max_exact + (jnp.log(nf / max_exact) / math.log(REL_MAX_DIST / max_exact)
                         * (REL_BUCKETS - max_exact)).astype(jnp.int32)
    return jnp.where(n < max_exact, n, jnp.minimum(large, REL_BUCKETS - 1))


def _swap_halves(w, r0):
    return jnp.concatenate([w[:, r0 + ROPE // 2:r0 + ROPE], w[:, r0:r0 + ROPE // 2]], axis=1)


def _fold_swapped(g, r0, width):
    sw = g[..., width:width + ROPE]
    half = ROPE // 2
    return jnp.concatenate([g[..., :r0], g[..., r0:r0 + half] + sw[..., half:], g[..., r0 + half:r0 + ROPE] + sw[..., :half],
                            g[..., r0 + ROPE:width]], axis=-1)


def kernel(x, c, w_mod, b_mod, attn_norm_g, w_in, swa_sinks, rel_bias, mla_q_norm_g, w_uq, mla_kv_norm_g, w_ukv, w_out, mlp_norm_g, w_ff1, w_ff2, final_norm_g, loss_target, m_w_mod, m_b_mod, m_attn_norm_g, m_w_in, m_swa_sinks, m_rel_bias, m_mla_q_norm_g, m_w_uq, m_mla_kv_norm_g, m_w_ukv, m_w_out, m_mlp_norm_g, m_w_ff1, m_w_ff2, m_final_norm_g, v_w_mod, v_b_mod, v_attn_norm_g, v_w_in, v_swa_sinks, v_rel_bias, v_mla_q_norm_g, v_w_uq, v_mla_kv_norm_g, v_w_ukv, v_w_out, v_mlp_norm_g, v_w_ff1, v_w_ff2, v_final_norm_g):
    s, d = x.shape[1], x.shape[2]
    ffs = w_ff1.shape[2]
    ff = ffs * NDEV
    nmod = w_mod.shape[2]
    me = 4 * lax.axis_index("x") + 2 * lax.axis_index("y") + lax.axis_index("c")
    x2d, tgt = x[0], loss_target[0]
    final_g = final_norm_g.reshape(1, d)

    w_in_l = jnp.concatenate([w_in[0], _swap_halves(w_in[0], OFF_KR)], axis=1).astype(BF16)
    w_uq_l = jnp.concatenate([w_uq[0], _swap_halves(w_uq[0], NOPE)], axis=1).astype(BF16)
    core = jnp.full((1, 128), lax.axis_index("c"), F32)
    (c_all,) = _exchange("gather_c", _Gather([c]))

    b_cols = lax.dynamic_slice(b_mod, (0, me * nmod), (1, nmod))
    act_all, mod_cols = _mod_fwd(c_all.reshape(NDEV, d), w_mod[0], b_cols)
    (mod_g,) = _exchange("gather_mod", _Gather([mod_cols]))
    mod = lax.dynamic_index_in_dim(mod_g, me, axis=1, keepdims=False).reshape(1, 6 * d)
    sh1, sc1, g1, sh2, sc2, g2 = [mod[:, i * d:(i + 1) * d] for i in range(6)]

    pos = jnp.arange(s, dtype=F32)
    inv_freq = ROPE_THETA ** (-jnp.arange(ROPE // 2, dtype=F32) / (ROPE // 2))
    ang = pos[:, None] * inv_freq[None, :]
    cos, sin = jnp.cos(ang), jnp.sin(ang)
    table = jnp.concatenate([cos, cos, -sin, sin], axis=1)
    q_loc = jnp.arange(BLOCK)[:, None]
    k_loc = jnp.arange(2 * BLOCK)[None, :]
    dist = q_loc + BLOCK - k_loc
    in_window = (dist >= 0) & (dist < BLOCK)
    onehot = (_t5_bucket(dist).reshape(-1, 1) == jnp.arange(REL_BUCKETS)[None, :]).astype(BF16)
    bias = _bias_expand(rel_bias.T, onehot.T).reshape(SWA_HEADS, BLOCK, 2 * BLOCK)
    bias = jnp.where(in_window[None], bias, -jnp.inf).reshape(SWA_KV, PAIRS, 2, BLOCK, 2 * BLOCK)
    bias = bias.transpose(0, 1, 3, 2, 4).reshape(SWA_KV, PROWS, PCOLS)
    sinks = jnp.broadcast_to(swa_sinks.reshape(SWA_KV, PAIRS, 1, 2), (SWA_KV, PAIRS, BLOCK, 2)).reshape(SWA_KV, PROWS, 2)
    sinks = (sinks[:, :, 0:1], sinks[:, :, 1:2])

    h1, w_in_g, w_uq_g, w_ukv_g = _norm_mod("norm1", x2d, attn_norm_g, sc1, sh1,
                                            comm=[_Gather([w_in_l, w_uq_l, w_ukv[0].astype(BF16)])])
    w_in_e = w_in_g.reshape(d, IN_EXT)

    def both_dtypes(acc, ex, outs):
        outs[0][...] = acc
        outs[1][...] = acc.astype(BF16)

    tmp = _pick(s, MM_TM // 2, 128)
    proj_blk = pl.BlockSpec((tmp, IN_EXT), lambda i, j, q: (i, 0))
    proj, proj_h = _mm("proj", h1, w_in_e, "nn", (s // tmp, 1, 1), pl.BlockSpec((tmp, d), lambda i, j, q: (i, 0)),
                       pl.BlockSpec((d, IN_EXT), lambda i, j, q: (0, 0)),
                       [jax.ShapeDtypeStruct((s, IN_EXT), F32), jax.ShapeDtypeStruct((s, IN_EXT), BF16)],
                       [proj_blk, proj_blk], (tmp, IN_EXT), both_dtypes)
    def diag_pair(tok):
        x = jnp.stack([tok[:, :SWA_DH], tok[:, SWA_DH:]])
        zero = jnp.zeros_like(x)
        return jnp.concatenate([x, zero], axis=2), jnp.concatenate([zero, x], axis=2)

    k_top, k_bot = diag_pair(proj_h[:, OFF_K:OFF_V])
    v_top, v_bot = diag_pair(proj_h[:, OFF_V:OFF_CQ])
    o_a, lse_a0, lse_a1, w_out_g = _swa2_fwd(proj_h, k_top, k_bot, v_top, v_bot, bias, sinks,
                                             comm=[_Gather([w_out[0].astype(BF16)])])
    w_out_f = w_out_g.reshape(MIX, d)

    cq, ckv, kr = _mla_prep(proj, mla_q_norm_g, mla_kv_norm_g, table)
    q_b, k_b, v_b = _mla_qkv(cq, ckv, kr, w_uq_g, w_ukv_g, table)
    o_b, mix, lse_b, w_ff1_g = _mla_fwd(q_b, k_b, v_b, o_a, comm=[_Gather([w_ff1[0].astype(BF16)])])

    tm, tn, tk = _pick(s, MM_TM, 128), _pick(d, MM_TN, 128), _pick(MIX, MM_TK, 128)
    row_blk = pl.BlockSpec((tm, tn), lambda i, j, q: (i, j))
    gate_blk = pl.BlockSpec((1, tn), lambda i, j, q: (0, j))

    def gated_residual(acc, ex, outs):
        outs[0][...] = acc.astype(BF16)
        outs[1][...] = ex[0][...] + ex[1][...] * acc

    branch_out = [jax.ShapeDtypeStruct((s, d), BF16), jax.ShapeDtypeStruct((s, d), F32)]

    y1, x2 = _mm("out_proj", mix, w_out_f, "nn", (s // tm, d // tn, MIX // tk),
                 pl.BlockSpec((tm, tk), lambda i, j, q: (i, q)), pl.BlockSpec((tk, tn), lambda i, j, q: (q, j)),
                 branch_out, [row_blk, row_blk], (tm, tn), gated_residual,
                 extras=(x2d, g1), extra_specs=(row_blk, gate_blk))

    (h2,) = _norm_mod("norm2", x2, mlp_norm_g, sc2, sh2)
    tnf, tkd = _pick(ffs, MM_TN, 128), _pick(d, MM_TK, 128)
    rf = ffs // tnf
    ff_blk = pl.BlockSpec((tm, tnf), lambda i, j, q: (i, j))

    def relu_sq(acc, ex, outs):
        u = jnp.maximum(acc, 0.0)
        outs[0][...] = u
        outs[1][...] = (u * u).astype(BF16)

    u, uu, w_ff2_g = _mm("ff1", h2, w_ff1_g, "nn", (s // tm, ff // tnf, d // tkd),
                         pl.BlockSpec((tm, tkd), lambda i, j, q: (i, q)),
                         pl.BlockSpec((None, tkd, tnf), lambda i, j, q: (j // rf, q, j % rf)),
                         [jax.ShapeDtypeStruct((s, ff), F32), jax.ShapeDtypeStruct((s, ff), BF16)], [ff_blk, ff_blk],
                         (tm, tnf), relu_sq, comm=[_Gather([w_ff2[0].astype(BF16)])])
    w_ff2_f = w_ff2_g.reshape(ff, d)
    tkf = _pick(ff, MM_TK, 128)
    y2, x3 = _mm("ff2", uu, w_ff2_f, "nn", (s // tm, d // tn, ff // tkf),
                 pl.BlockSpec((tm, tkf), lambda i, j, q: (i, q)), pl.BlockSpec((tkf, tn), lambda i, j, q: (q, j)),
                 branch_out, [row_blk, row_blk], (tm, tn), gated_residual,
                 extras=(x2, g2), extra_specs=(row_blk, gate_blk))

    dx3, dy2, loss_p, dgf, dg2 = _loss_head(x3, tgt, y2, final_g, g2)

    def relu_sq_bwd(acc, ex, outs):
        outs[0][...] = (acc * (2.0 * ex[0][...])).astype(BF16)

    tnf2 = _pick(ff, MM_TN, 128)
    du = _mm("ff2_dx", dy2, w_ff2_f, "nt", (s // tm, ff // tnf2, d // tkd),
             pl.BlockSpec((tm, tkd), lambda i, j, q: (i, q)), pl.BlockSpec((tnf2, tkd), lambda i, j, q: (j, q)),
             [jax.ShapeDtypeStruct((s, ff), BF16)], [pl.BlockSpec((tm, tnf2), lambda i, j, q: (i, j))],
             (tm, tnf2), relu_sq_bwd, extras=(u,), extra_specs=(pl.BlockSpec((tm, tnf2), lambda i, j, q: (i, j)),))[0]
    gw_ff2 = _mm_plain("ff2_dw", uu, dy2, "tn", ff, d, s, BF16)
    tmd, tks = _pick(d, MM_TM, 128), _pick(s, MM_TK, 128)
    gw_ff2 = gw_ff2.reshape(NDEV, ffs, d)
    dh2, s_ff2 = _mm("ff1_dx", du, w_ff1_g, "nt", (s // tm, d // tn, NDEV // 2),
                     pl.BlockSpec((tm, 2 * ffs), lambda i, j, q: (i, q)),
                     pl.BlockSpec((2, tn, ffs), lambda i, j, q: (q, j, 0)),
                     [jax.ShapeDtypeStruct((s, d), F32)], [row_blk], (tm, tn), _store(F32),
                     comm=[_PairSwap([gw_ff2])], b_parts=2)
    c_ff2 = _pair_sum("pair_ff2", gw_ff2, s_ff2, core)
    gw_ff1, p_ff2 = _mm("ff1_dw", h2, du, "tn", (d // tmd, ff // tnf, s // tks),
                        pl.BlockSpec((tks, tmd), lambda i, j, q: (q, i)), pl.BlockSpec((tks, tnf), lambda i, j, q: (q, j)),
                        [jax.ShapeDtypeStruct((NDEV, d, ffs), BF16)],
                        [pl.BlockSpec((None, tmd, tnf), lambda i, j, q: (j // rf, i, j % rf))], (tmd, tnf), _store(BF16),
                        comm=[_ChipScatter([c_ff2])])
    dx2, dy1, dsc2, dsh2, dgm, dg1, s_ff1 = _norm_mod_bwd("norm2_bwd", x2, dh2, dx3, mlp_norm_g, sc2, y1, g1,
                                                          comm=[_PairSwap([gw_ff1])])
    c_ff1 = _pair_sum("pair_ff1", gw_ff1, s_ff1, core)

    dmix = _mm_plain("out_proj_dx", dy1, w_out_f, "nt", s, MIX, d, F32)
    gw_out = _mm_plain("out_proj_dw", mix, dy1, "tn", MIX, d, s, BF16).reshape(NDEV, MIX // NDEV, d)

    delta_b = _mla_delta(dmix, o_b)
    dq_b, dk_b, dv_b, p_ff1, s_out = _mla_bwd(q_b, k_b, v_b, dmix, delta_b, lse_b,
                                              comm=[_ChipScatter([c_ff1]), _PairSwap([gw_out])])
    c_out = _pair_sum("pair_out", gw_out, s_out, core)
    dcq, dckv, dkr, gw_uq_e, gw_ukv, p_out = _mla_qkv_bwd(dq_b, dk_b, dv_b, cq, ckv, w_uq_g, w_ukv_g, table,
                                                          comm=[_ChipScatter([c_out])])
    gw_uq = _fold_swapped(gw_uq_e, NOPE, NOPE + ROPE).astype(BF16)
    gw_ukv = gw_ukv.astype(BF16)

    dproj, dkp, dkc, dvp, dvc, dbias, dsink0, dsink1, s_uq, s_ukv = _swa2_bwd(
        proj_h, dmix, k_top, k_bot, v_top, v_bot, (lse_a0, lse_a1), bias, sinks, comm=[_PairSwap([gw_uq, gw_ukv])])
    c_uq = _pair_sum("pair_uq", gw_uq, s_uq, core)
    c_ukv = _pair_sum("pair_ukv", gw_ukv, s_ukv, core)
    dproj, dgq, dgkv = _mla_prep_bwd(proj, dcq, dckv, dkr, mla_q_norm_g, mla_kv_norm_g, table, dproj)

    def band_grad(cur, prv):
        g = cur + jnp.concatenate([prv[:, BLOCK:], jnp.zeros_like(prv[:, :BLOCK])], axis=1)
        g = g[:, :, :SWA_DH] + g[:, :, SWA_DH:]
        return jnp.concatenate([g[0], g[1]], axis=1)

    dbias = dbias.reshape(SWA_KV, PAIRS, BLOCK, 2, 2 * BLOCK).transpose(0, 1, 3, 2, 4)
    dsink = jnp.stack([dsink0.reshape(SWA_KV, PAIRS, BLOCK), dsink1.reshape(SWA_KV, PAIRS, BLOCK)], axis=2)
    drel_t, dsinks = _bias_reduce(dbias.reshape(SWA_HEADS, BLOCK * 2 * BLOCK), onehot, dsink.reshape(SWA_HEADS, BLOCK))
    dkv = jnp.concatenate([band_grad(dkc, dkp), band_grad(dvc, dvp)], axis=1).astype(BF16)
    dproj = lax.dynamic_update_slice(dproj, dkv, (0, OFF_K))
    def fold_rotary(acc, ex, outs):
        slab = acc[:, TAIL - 2 * ROPE:TAIL]
        lane = lax.broadcasted_iota(jnp.int32, slab.shape, 1)
        folded = slab + jnp.where(lane < ROPE // 2, pltpu.roll(slab, ROPE // 2, 1), pltpu.roll(slab, 3 * ROPE // 2, 1))
        last = pl.program_id(1) == IN_EXT // TAIL - 1
        outs[0][:, 0:TAIL - 2 * ROPE] = acc[:, 0:TAIL - 2 * ROPE].astype(BF16)
        outs[0][:, TAIL - 2 * ROPE:TAIL] = jnp.where(last, folded, slab).astype(BF16)

    tmw, tks = _pick(d, MM_TM, 128), _pick(s, MM_TK, 128)
    gw_in = _mm("proj_dw", h1, dproj, "tn", (d // tmw, IN_EXT // TAIL, s // tks),
                pl.BlockSpec((tks, tmw), lambda i, j, q: (q, i)), pl.BlockSpec((tks, TAIL), lambda i, j, q: (q, j)),
                [jax.ShapeDtypeStruct((d, IN_COLS), BF16)], [pl.BlockSpec((tmw, TAIL), lambda i, j, q: (i, j))],
                (tmw, TAIL), fold_rotary)[0].reshape(NDEV, d // NDEV, IN_COLS)
    tkt = IN_EXT
    dh1, s_in, p_uq, p_ukv = _mm(
        "proj_dx", dproj, w_in_e, "nt", (s // tm, d // tn, IN_EXT // tkt),
        pl.BlockSpec((tm, tkt), lambda i, j, q: (i, q)), pl.BlockSpec((tn, tkt), lambda i, j, q: (j, q)),
        [jax.ShapeDtypeStruct((s, d), F32)], [row_blk], (tm, tn), _store(F32),
        comm=[_PairSwap([gw_in]), _ChipScatter([c_uq, c_ukv])])
    c_in = _pair_sum("pair_in", gw_in, s_in, core)
    gx, dsc1, dsh1, dga, p_in = _norm_mod_bwd("norm1_bwd", x2d, dh1, dx2, attn_norm_g, sc1,
                                              comm=[_ChipScatter([c_in])])

    small = [jnp.concatenate([dsh1, dsc1, dg1, dsh2, dsc2, dg2], axis=1), dga, dgm, dgf, dgq, dgkv,
             dsinks.reshape(1, SWA_HEADS), drel_t.T.reshape(1, REL_BUCKETS * SWA_HEADS)]
    n_small = sum(a.shape[1] for a in small)
    n_pad = -(n_small + 1) % 1024 + 1
    rows_small = (n_small + n_pad) // 128
    pad = jnp.zeros((1, n_pad), F32)
    pack = lambda parts, tail=pad: jnp.concatenate([p.reshape(1, -1) for p in parts] + [tail], axis=1).reshape(rows_small, 128)
    (small_g,) = _exchange("gather_small", _Gather([pack(small, jnp.concatenate([loss_p, pad[:, 1:]], axis=1))]))
    small_names = (b_mod, attn_norm_g, mlp_norm_g, final_norm_g, mla_q_norm_g, mla_kv_norm_g, swa_sinks, rel_bias)
    small_m = (m_b_mod, m_attn_norm_g, m_mlp_norm_g, m_final_norm_g, m_mla_q_norm_g, m_mla_kv_norm_g, m_swa_sinks, m_rel_bias)
    small_v = (v_b_mod, v_attn_norm_g, v_mlp_norm_g, v_final_norm_g, v_mla_q_norm_g, v_mla_kv_norm_g, v_swa_sinks, v_rel_bias)
    small_out = _adamw("adamw_small", pack(small_names), small_g, pack(small_m), pack(small_v), parts=True)

    def unpack(flat):
        flat = flat.reshape(1, -1)
        out, off = [], 0
        for a in small_names:
            out.append(flat[:, off:off + a.size].reshape(a.shape))
            off += a.size
        return out

    sg, sd, sm, sv = [unpack(o) for o in small_out]
    loss = small_out[0].reshape(-1)[n_small]

    dmod_cols = lax.dynamic_slice(small_g.reshape(NDEV, -1), (0, me * nmod), (NDEV, nmod))
    gw_mod = _mod_wgrad(act_all, dmod_cols)
    big = {"w_mod": _adamw("adamw_w_mod", w_mod[0], gw_mod, m_w_mod[0], v_w_mod[0], parts=False)}

    for name, w, p, m, v in (("w_in", w_in, p_in, m_w_in, v_w_in), ("w_uq", w_uq, p_uq, m_w_uq, v_w_uq),
                             ("w_ukv", w_ukv, p_ukv, m_w_ukv, v_w_ukv), ("w_out", w_out, p_out, m_w_out, v_w_out),
                             ("w_ff1", w_ff1, p_ff1, m_w_ff1, v_w_ff1), ("w_ff2", w_ff2, p_ff2, m_w_ff2, v_w_ff2)):
        big[name] = _adamw("adamw_" + name, w[0], p, m[0], v[0], parts=True)

    order = ("w_mod", "b_mod", "attn_norm_g", "w_in", "swa_sinks", "rel_bias", "mla_q_norm_g", "w_uq", "mla_kv_norm_g",
             "w_ukv", "w_out", "mlp_norm_g", "w_ff1", "w_ff2", "final_norm_g")
    small_idx = {"b_mod": 0, "attn_norm_g": 1, "mlp_norm_g": 2, "final_norm_g": 3, "mla_q_norm_g": 4,
                 "mla_kv_norm_g": 5, "swa_sinks": 6, "rel_bias": 7}
    outs = []
    for kind, small_list in enumerate((sg, sd, sm, sv)):
        for name in order:
            outs.append(small_list[small_idx[name]] if name in small_idx else big[name][kind][None])
    return (loss, gx[None], *outs)
```
